```python
import jax, jax.numpy as jnp
from jax import lax
import numpy as np

D_MODEL = 1024
BATCH = 8
SEQ = 16384
DEPTH = 1

HEAD_DIM = 64
N_HEADS = D_MODEL // HEAD_DIM
N_SB_HEADS = N_HEADS // 2
N_FOX_HEADS = N_HEADS - N_SB_HEADS
SB_WIDTH = N_SB_HEADS * HEAD_DIM
FOX_WIDTH = N_FOX_HEADS * HEAD_DIM
IN_WIDTH = 3 * SB_WIDTH + 3 * FOX_WIDTH + N_FOX_HEADS
D_FF = ((-(-8 * D_MODEL // 3) + 255) // 256) * 256
BLOCK_Q = 128
DEEPNORM_ALPHA = (2 * DEPTH) ** 0.25
DEEPNORM_BETA = (8 * DEPTH) ** -0.25
LN_EPS = 1e-5
RMS_EPS = 1e-6

kernel_name = "hybrid_stickbreaking_forgetting_deepnorm"


def layer_norm(x, g, b):
    xf = x.astype(jnp.float32)
    mu = jnp.mean(xf, axis=-1, keepdims=True)
    var = jnp.mean(jnp.square(xf - mu), axis=-1, keepdims=True)
    return ((xf - mu) * lax.rsqrt(var + LN_EPS) * g + b).astype(x.dtype)


def head_rmsnorm(o, g):
    B, H, S, d = o.shape
    of = o.astype(jnp.float32)
    of = of * lax.rsqrt(jnp.mean(jnp.square(of), axis=-1, keepdims=True) + RMS_EPS)
    of = of * g.reshape(1, H, 1, d).astype(jnp.float32)
    return of.transpose(0, 2, 1, 3).reshape(B, S, H * d).astype(o.dtype)


def stick_breaking_attention(q, k, v):
    B, H, S, d = q.shape
    nb = S // BLOCK_Q
    scale = d ** -0.5
    qb = q.reshape(B, H, nb, BLOCK_Q, d).transpose(2, 0, 1, 3, 4)
    k_pos = jnp.arange(S)

    def block(args):
        qi, i = args
        z = jnp.einsum('bhqd,bhkd->bhqk', qi, k, preferred_element_type=jnp.float32) * scale
        q_pos = i * BLOCK_Q + jnp.arange(BLOCK_Q)
        mask = k_pos[None, :] < q_pos[:, None]
        log_keep = jnp.where(mask, jax.nn.log_sigmoid(-z), 0.0)
        log_after = lax.cumsum(log_keep, axis=3, reverse=True) - log_keep
        w = jnp.where(mask, jnp.exp(jax.nn.log_sigmoid(z) + log_after), 0.0)
        return jnp.einsum('bhqk,bhkd->bhqd', w.astype(v.dtype), v,
                          preferred_element_type=jnp.float32).astype(v.dtype)

    o = lax.map(block, (qb, jnp.arange(nb)))
    return o.transpose(1, 2, 0, 3, 4).reshape(B, H, S, d)


def forgetting_attention(q, k, v, c):
    B, H, S, d = q.shape
    nb = S // BLOCK_Q
    scale = d ** -0.5
    qb = q.reshape(B, H, nb, BLOCK_Q, d).transpose(2, 0, 1, 3, 4)
    cb = c.reshape(B, H, nb, BLOCK_Q).transpose(2, 0, 1, 3)
    k_pos = jnp.arange(S)

    def block(args):
        qi, ci, i = args
        z = jnp.einsum('bhqd,bhkd->bhqk', qi, k, preferred_element_type=jnp.float32) * scale
        z = z + ci[..., :, None] - c[:, :, None, :]
        q_pos = i * BLOCK_Q + jnp.arange(BLOCK_Q)
        mask = k_pos[None, :] <= q_pos[:, None]
        p = jax.nn.softmax(jnp.where(mask, z, -jnp.inf), axis=-1)
        return jnp.einsum('bhqk,bhkd->bhqd', p.astype(v.dtype), v,
                          preferred_element_type=jnp.float32).astype(v.dtype)

    o = lax.map(block, (qb, cb, jnp.arange(nb)))
    return o.transpose(1, 2, 0, 3, 4).reshape(B, H, S, d)


def hybrid_mixer(x, w_in, b_f, g_sb, g_fox, w_out):
    B, S, _ = x.shape
    proj = jnp.einsum('bsd,de->bse', x, w_in)
    splits = [int(s) for s in np.cumsum([SB_WIDTH] * 3 + [FOX_WIDTH] * 3)]
    q_sb, k_sb, v_sb, q_fx, k_fx, v_fx, f_logit = jnp.split(proj, splits, axis=-1)

    def heads(t, h):
        return t.reshape(B, S, h, HEAD_DIM).transpose(0, 2, 1, 3)

    o_sb = stick_breaking_attention(heads(q_sb, N_SB_HEADS), heads(k_sb, N_SB_HEADS),
                                    heads(v_sb, N_SB_HEADS))
    log_f = jax.nn.log_sigmoid((f_logit + b_f).astype(jnp.float32))
    c = jnp.cumsum(log_f, axis=1).transpose(0, 2, 1)
    o_fx = forgetting_attention(heads(q_fx, N_FOX_HEADS), heads(k_fx, N_FOX_HEADS),
                                heads(v_fx, N_FOX_HEADS), c)
    o = jnp.concatenate([head_rmsnorm(o_sb, g_sb), head_rmsnorm(o_fx, g_fox)], axis=-1)
    return jnp.einsum('bse,ed->bsd', o, w_out)


def swiglu(h, w_gate_up, w_down):
    gu = jnp.einsum('bsd,df->bsf', h, w_gate_up)
    gate, up = jnp.split(gu, 2, axis=-1)
    return jnp.einsum('bsf,fd->bsd', jax.nn.silu(gate) * up, w_down)


def _fwd_setup_inputs(seed: int = 0) -> dict:
    key = jax.random.key(seed)
    ks = jax.random.split(key, 12)
    f32 = jnp.float32
    x = jax.random.normal(ks[0], (BATCH, SEQ, D_MODEL), f32)
    col_scale = np.ones((IN_WIDTH,), np.float32)
    col_scale[2 * SB_WIDTH:3 * SB_WIDTH] = DEEPNORM_BETA
    col_scale[3 * SB_WIDTH + 2 * FOX_WIDTH:3 * SB_WIDTH + 3 * FOX_WIDTH] = DEEPNORM_BETA
    w_in = (jax.random.normal(ks[1], (DEPTH, D_MODEL, IN_WIDTH), f32)
            * (D_MODEL ** -0.5) * jnp.asarray(col_scale))
    b_f = jax.random.uniform(ks[2], (DEPTH, N_FOX_HEADS), f32, 1.0, 4.0)
    g_sb = 1.0 + 0.02 * jax.random.normal(ks[3], (DEPTH, SB_WIDTH), f32)
    g_fox = 1.0 + 0.02 * jax.random.normal(ks[4], (DEPTH, FOX_WIDTH), f32)
    w_out = jax.random.normal(ks[5], (DEPTH, D_MODEL, D_MODEL), f32) * (D_MODEL ** -0.5) * DEEPNORM_BETA
    ln1_g = 1.0 + 0.02 * jax.random.normal(ks[6], (DEPTH, D_MODEL), f32)
    ln1_b = 0.02 * jax.random.normal(ks[7], (DEPTH, D_MODEL), f32)
    ln2_g = 1.0 + 0.02 * jax.random.normal(ks[8], (DEPTH, D_MODEL), f32)
    ln2_b = 0.02 * jax.random.normal(ks[9], (DEPTH, D_MODEL), f32)
    w_gate_up = jax.random.normal(ks[10], (DEPTH, D_MODEL, 2 * D_FF), f32) * (D_MODEL ** -0.5) * DEEPNORM_BETA
    w_down = jax.random.normal(ks[11], (DEPTH, D_FF, D_MODEL), f32) * (D_FF ** -0.5) * DEEPNORM_BETA
    return {"x": x, "w_in": w_in, "b_f": b_f, "g_sb": g_sb, "g_fox": g_fox, "w_out": w_out,
            "ln1_g": ln1_g, "ln1_b": ln1_b, "ln2_g": ln2_g, "ln2_b": ln2_b,
            "w_gate_up": w_gate_up, "w_down": w_down}


def _fwd_reference(x, w_in, b_f, g_sb, g_fox, w_out, ln1_g, ln1_b, ln2_g, ln2_b, w_gate_up, w_down):
    h = x
    for l in range(DEPTH):
        mix = hybrid_mixer(h, w_in[l], b_f[l], g_sb[l], g_fox[l], w_out[l])
        h = layer_norm(DEEPNORM_ALPHA * h + mix, ln1_g[l], ln1_b[l])
        ff = swiglu(h, w_gate_up[l], w_down[l])
        h = layer_norm(DEEPNORM_ALPHA * h + ff, ln2_g[l], ln2_b[l])
    return h


import jax as _jax
import jax.numpy as _jnp

TWIN_FORMAT = 'train_step'
FWD_PARAMS = ['x', 'w_in', 'b_f', 'g_sb', 'g_fox', 'w_out', 'ln1_g', 'ln1_b', 'ln2_g', 'ln2_b', 'w_gate_up', 'w_down']
TWIN_WEIGHTS = ['w_in', 'b_f', 'g_sb', 'g_fox', 'w_out', 'ln1_g', 'ln1_b', 'ln2_g', 'ln2_b', 'w_gate_up', 'w_down']
TWIN_DIFF_INPUT = 'x'
TWIN_INPUTS = ['x', 'w_in', 'b_f', 'g_sb', 'g_fox', 'w_out', 'ln1_g', 'ln1_b', 'ln2_g', 'ln2_b', 'w_gate_up', 'w_down', 'loss_target', 'm_w_in', 'm_b_f', 'm_g_sb', 'm_g_fox', 'm_w_out', 'm_ln1_g', 'm_ln1_b', 'm_ln2_g', 'm_ln2_b', 'm_w_gate_up', 'm_w_down', 'v_w_in', 'v_b_f', 'v_g_sb', 'v_g_fox', 'v_w_out', 'v_ln1_g', 'v_ln1_b', 'v_ln2_g', 'v_ln2_b', 'v_w_gate_up', 'v_w_down']
TWIN_OUTPUTS = ['loss', 'grad_x', 'grad_w_in', 'grad_b_f', 'grad_g_sb', 'grad_g_fox', 'grad_w_out', 'grad_ln1_g', 'grad_ln1_b', 'grad_ln2_g', 'grad_ln2_b', 'grad_w_gate_up', 'grad_w_down', 'delta_w_in', 'delta_b_f', 'delta_g_sb', 'delta_g_fox', 'delta_w_out', 'delta_ln1_g', 'delta_ln1_b', 'delta_ln2_g', 'delta_ln2_b', 'delta_w_gate_up', 'delta_w_down', 'new_m_w_in', 'new_m_b_f', 'new_m_g_sb', 'new_m_g_fox', 'new_m_w_out', 'new_m_ln1_g', 'new_m_ln1_b', 'new_m_ln2_g', 'new_m_ln2_b', 'new_m_w_gate_up', 'new_m_w_down', 'new_v_w_in', 'new_v_b_f', 'new_v_g_sb', 'new_v_g_fox', 'new_v_w_out', 'new_v_ln1_g', 'new_v_ln1_b', 'new_v_ln2_g', 'new_v_ln2_b', 'new_v_w_gate_up', 'new_v_w_down']
TWIN_LEAF_KINDS = {'loss': 'loss', 'grad_x': 'grad_x', 'grad_w_in': 'grad_w', 'grad_b_f': 'grad_w', 'grad_g_sb': 'grad_w', 'grad_g_fox': 'grad_w', 'grad_w_out': 'grad_w', 'grad_ln1_g': 'grad_w', 'grad_ln1_b': 'grad_w', 'grad_ln2_g': 'grad_w', 'grad_ln2_b': 'grad_w', 'grad_w_gate_up': 'grad_w', 'grad_w_down': 'grad_w', 'delta_w_in': 'delta_w', 'delta_b_f': 'delta_w', 'delta_g_sb': 'delta_w', 'delta_g_fox': 'delta_w', 'delta_w_out': 'delta_w', 'delta_ln1_g': 'delta_w', 'delta_ln1_b': 'delta_w', 'delta_ln2_g': 'delta_w', 'delta_ln2_b': 'delta_w', 'delta_w_gate_up': 'delta_w', 'delta_w_down': 'delta_w', 'new_m_w_in': 'new_m', 'new_m_b_f': 'new_m', 'new_m_g_sb': 'new_m', 'new_m_g_fox': 'new_m', 'new_m_w_out': 'new_m', 'new_m_ln1_g': 'new_m', 'new_m_ln1_b': 'new_m', 'new_m_ln2_g': 'new_m', 'new_m_ln2_b': 'new_m', 'new_m_w_gate_up': 'new_m', 'new_m_w_down': 'new_m', 'new_v_w_in': 'new_v', 'new_v_b_f': 'new_v', 'new_v_g_sb': 'new_v', 'new_v_g_fox': 'new_v', 'new_v_w_out': 'new_v', 'new_v_ln1_g': 'new_v', 'new_v_ln1_b': 'new_v', 'new_v_ln2_g': 'new_v', 'new_v_ln2_b': 'new_v', 'new_v_w_gate_up': 'new_v', 'new_v_w_down': 'new_v'}


def _forward(args):
    return _fwd_reference(*[args[k] for k in FWD_PARAMS])


def _output_shape():
    def fwd():
        inp = _fwd_setup_inputs(0)
        return _fwd_reference(*[inp[k] for k in FWD_PARAMS])
    out = _jax.eval_shape(fwd)
    return out.shape, out.dtype

N_MICROBATCH = 1
ADAM_LR = 0.001
ADAM_B1 = 0.9
ADAM_B2 = 0.999
ADAM_EPS = 1e-08
ADAM_WD = 0.01
ADAM_STEP = 10
PER_EXAMPLE_BATCH_AXIS = {'x': 0, 'loss_target': 0}
SHARED_INPUTS = []
_WEIGHT_DTYPES = {'w_in': _jnp.float32, 'b_f': _jnp.float32, 'g_sb': _jnp.float32, 'g_fox': _jnp.float32, 'w_out': _jnp.float32, 'ln1_g': _jnp.float32, 'ln1_b': _jnp.float32, 'ln2_g': _jnp.float32, 'ln2_b': _jnp.float32, 'w_gate_up': _jnp.float32, 'w_down': _jnp.float32}
MOMENT_SCALE = {'w_in': 1.793379e-01, 'b_f': 1.183624e+00, 'g_sb': 1.595760e-01, 'g_fox': 1.608634e-01, 'w_out': 2.693380e-01, 'ln1_g': 4.087270e+00, 'ln1_b': 1.883208e+00, 'ln2_g': 1.280871e+02, 'ln2_b': 2.240917e+00, 'w_gate_up': 3.520833e-02, 'w_down': 5.766219e-02}


def _to_microbatches(a, axis):
    t = _jnp.moveaxis(a, axis, 0)
    t = t.reshape((N_MICROBATCH, t.shape[0] // N_MICROBATCH) + t.shape[1:])
    return _jnp.moveaxis(t, 1, axis + 1)


def setup_inputs(seed: int = 0) -> dict:
    inp = _fwd_setup_inputs(seed)
    key = _jax.random.fold_in(_jax.random.key(seed), 7919)
    shape, _ = _output_shape()
    out = dict(inp)
    out["loss_target"] = _jax.random.normal(_jax.random.fold_in(key, 0), shape, _jnp.float32)
    for i, name in enumerate(TWIN_WEIGHTS):
        w = inp[name].astype(_jnp.float32)
        if MOMENT_SCALE is None:
            s = _jnp.sqrt(_jnp.mean(_jnp.square(w)) + 1e-30)
        else:
            s = MOMENT_SCALE[name]
        km, kv = _jax.random.split(_jax.random.fold_in(key, i + 1))
        out[name] = w
        out["m_" + name] = s * _jax.random.normal(km, w.shape, _jnp.float32)
        out["v_" + name] = (s * s) * _jax.random.uniform(kv, w.shape, _jnp.float32, 0.5, 1.5)
    if N_MICROBATCH > 1:
        for name, axis in PER_EXAMPLE_BATCH_AXIS.items():
            out[name] = _to_microbatches(out[name], axis)
    return {'x': out['x'], 'w_in': out['w_in'], 'b_f': out['b_f'], 'g_sb': out['g_sb'], 'g_fox': out['g_fox'], 'w_out': out['w_out'], 'ln1_g': out['ln1_g'], 'ln1_b': out['ln1_b'], 'ln2_g': out['ln2_g'], 'ln2_b': out['ln2_b'], 'w_gate_up': out['w_gate_up'], 'w_down': out['w_down'], 'loss_target': out['loss_target'], 'm_w_in': out['m_w_in'], 'm_b_f': out['m_b_f'], 'm_g_sb': out['m_g_sb'], 'm_g_fox': out['m_g_fox'], 'm_w_out': out['m_w_out'], 'm_ln1_g': out['m_ln1_g'], 'm_ln1_b': out['m_ln1_b'], 'm_ln2_g': out['m_ln2_g'], 'm_ln2_b': out['m_ln2_b'], 'm_w_gate_up': out['m_w_gate_up'], 'm_w_down': out['m_w_down'], 'v_w_in': out['v_w_in'], 'v_b_f': out['v_b_f'], 'v_g_sb': out['v_g_sb'], 'v_g_fox': out['v_g_fox'], 'v_w_out': out['v_w_out'], 'v_ln1_g': out['v_ln1_g'], 'v_ln1_b': out['v_ln1_b'], 'v_ln2_g': out['v_ln2_g'], 'v_ln2_b': out['v_ln2_b'], 'v_w_gate_up': out['v_w_gate_up'], 'v_w_down': out['v_w_down']}


def _loss(weights, diff, rest, loss_target):
    with _jax.named_scope("forward"):
        args = {**rest, TWIN_DIFF_INPUT: diff, **{k: w.astype(_WEIGHT_DTYPES[k]) for k, w in weights.items()}}
        y = _forward(args)
    with _jax.named_scope("loss_head"):
        err = _jnp.square(y.astype(_jnp.float32) - loss_target)
        return 0.5 * _jnp.sum(_jnp.mean(err, axis=-1)) if err.ndim else 0.5 * err


def _adamw(w, g, m, v):
    m = ADAM_B1 * m + (1.0 - ADAM_B1) * g
    v = ADAM_B2 * v + (1.0 - ADAM_B2) * _jnp.square(g)
    m_hat = m / (1.0 - ADAM_B1 ** ADAM_STEP)
    v_hat = v / (1.0 - ADAM_B2 ** ADAM_STEP)
    delta = -ADAM_LR * (m_hat / (_jnp.sqrt(v_hat) + ADAM_EPS) + ADAM_WD * w)
    return delta, m, v


def reference(x, w_in, b_f, g_sb, g_fox, w_out, ln1_g, ln1_b, ln2_g, ln2_b, w_gate_up, w_down, loss_target, m_w_in, m_b_f, m_g_sb, m_g_fox, m_w_out, m_ln1_g, m_ln1_b, m_ln2_g, m_ln2_b, m_w_gate_up, m_w_down, v_w_in, v_b_f, v_g_sb, v_g_fox, v_w_out, v_ln1_g, v_ln1_b, v_ln2_g, v_ln2_b, v_w_gate_up, v_w_down):
    given = dict(x=x, w_in=w_in, b_f=b_f, g_sb=g_sb, g_fox=g_fox, w_out=w_out, ln1_g=ln1_g, ln1_b=ln1_b, ln2_g=ln2_g, ln2_b=ln2_b, w_gate_up=w_gate_up, w_down=w_down, loss_target=loss_target, m_w_in=m_w_in, m_b_f=m_b_f, m_g_sb=m_g_sb, m_g_fox=m_g_fox, m_w_out=m_w_out, m_ln1_g=m_ln1_g, m_ln1_b=m_ln1_b, m_ln2_g=m_ln2_g, m_ln2_b=m_ln2_b, m_w_gate_up=m_w_gate_up, m_w_down=m_w_down, v_w_in=v_w_in, v_b_f=v_b_f, v_g_sb=v_g_sb, v_g_fox=v_g_fox, v_w_out=v_w_out, v_ln1_g=v_ln1_g, v_ln1_b=v_ln1_b, v_ln2_g=v_ln2_g, v_ln2_b=v_ln2_b, v_w_gate_up=v_w_gate_up, v_w_down=v_w_down)
    weights = {n: given[n] for n in TWIN_WEIGHTS}
    shared = {n: given[n] for n in SHARED_INPUTS}
    per_example = {n: given[n] for n in ['x']}
    grad_fn = _jax.value_and_grad(_loss, argnums=(0, 1))

    def one_microbatch(ex, loss_target):
        ex = dict(ex)
        diff = ex.pop(TWIN_DIFF_INPUT)
        return grad_fn(weights, diff, {**shared, **ex}, loss_target)

    if N_MICROBATCH == 1:
        loss, (grad_w, grad_x) = one_microbatch(per_example, given["loss_target"])
    else:
        def body(carry, xs):
            loss_sum, grad_sum = carry
            l_k, (gw_k, gx_k) = one_microbatch(xs[0], xs[1])
            with _jax.named_scope("update"):
                return (loss_sum + l_k, _jax.tree.map(_jnp.add, grad_sum, gw_k)), gx_k

        init = (_jnp.zeros((), _jnp.float32), _jax.tree.map(_jnp.zeros_like, weights))
        (loss, grad_w), grad_x = _jax.lax.scan(body, init, (per_example, given["loss_target"]))
    with _jax.named_scope("update"):
        delta_w, new_m, new_v = {}, {}, {}
        for n in TWIN_WEIGHTS:
            delta_w[n], new_m[n], new_v[n] = _adamw(weights[n], grad_w[n], given["m_" + n], given["v_" + n])
    return (loss, grad_x, *[grad_w[n] for n in TWIN_WEIGHTS], *[delta_w[n] for n in TWIN_WEIGHTS],
            *[new_m[n] for n in TWIN_WEIGHTS], *[new_v[n] for n in TWIN_WEIGHTS])
```

```python
import functools

import jax
import jax.numpy as jnp
from jax import lax
from jax.experimental import pallas as pl
from jax.experimental.pallas import tpu as pltpu

F32 = jnp.float32
BF16 = jnp.bfloat16
I32 = jnp.int32

N_DEV = 8
HEAD_DIM = 64
LANES = 128
SCALE = HEAD_DIM ** -0.5
ALPHA = 2.0 ** 0.25
LN_EPS = 1e-5
RMS_EPS = 1e-6
ADAM_LR, ADAM_B1, ADAM_B2, ADAM_EPS, ADAM_WD, ADAM_STEP = 0.001, 0.9, 0.999, 1e-08, 0.01, 10
NEG_BIG = -1e30
VMEM_LIMIT = 60 * 1024 * 1024
ROW_TILE = 256
ATT_BLOCK = 256
MESH = pl.DeviceIdType.MESH


def _cparams(sem):
    return pltpu.CompilerParams(dimension_semantics=sem, vmem_limit_bytes=VMEM_LIMIT)


def _dot(a, b):
    return jnp.dot(a, b, preferred_element_type=F32)


def _dot_nt(a, b):
    return lax.dot_general(a, b, (((1,), (1,)), ((), ())), preferred_element_type=F32)


def _dot_tn(a, b):
    return lax.dot_general(a, b, (((0,), (0,)), ((), ())), preferred_element_type=F32)


def _split2(a):
    hi = a.astype(BF16)
    lo = (a - hi.astype(F32)).astype(BF16)
    return hi, lo


def _split3(a):
    hi = a.astype(BF16)
    r1 = a - hi.astype(F32)
    mid = r1.astype(BF16)
    lo = (r1 - mid.astype(F32)).astype(BF16)
    return hi, mid, lo


def _dot_acc(a, m):
    hi, lo = _split2(a)
    return _dot(hi, m) + _dot(lo, m)


def _tri(n, fn):
    r = lax.broadcasted_iota(I32, (n, n), 0)
    c = lax.broadcasted_iota(I32, (n, n), 1)
    return fn(r, c).astype(BF16)


def _full(shape):
    nd = len(shape)
    return pl.BlockSpec(shape, lambda *_: (0,) * nd)


def _peer(k):
    x, y, c = lax.axis_index("x"), lax.axis_index("y"), lax.axis_index("c")
    return (1 - x if k & 4 else x, 1 - y if k & 2 else y, 1 - c if k & 1 else c)


def _my_index():
    return 4 * lax.axis_index("x") + 2 * lax.axis_index("y") + lax.axis_index("c")


def _all_gather(shards):
    n = len(shards)

    def body(*refs):
        ins, outs = refs[:n], refs[n:2 * n]
        send_sems, recv_sems, local_sems = refs[2 * n:]
        me = _my_index()
        local = [pltpu.make_async_copy(ins[a], outs[a].at[me], local_sems.at[a]) for a in range(n)]
        for cp in local:
            cp.start()
        sends = []
        for k in range(1, N_DEV):
            for a in range(n):
                cp = pltpu.make_async_remote_copy(
                    src_ref=ins[a], dst_ref=outs[a].at[me],
                    send_sem=send_sems.at[(k - 1) * n + a], recv_sem=recv_sems.at[(k - 1) * n + a],
                    device_id=_peer(k), device_id_type=MESH)
                cp.start()
                sends.append(cp)
        for k in range(1, N_DEV):
            src = me ^ k
            for a in range(n):
                pltpu.make_async_remote_copy(
                    src_ref=ins[a], dst_ref=outs[a].at[src],
                    send_sem=send_sems.at[(k - 1) * n + a], recv_sem=recv_sems.at[(k - 1) * n + a],
                    device_id=_peer(k), device_id_type=MESH).wait_recv()
        for cp in sends:
            cp.wait_send()
        for cp in local:
            cp.wait()

    any_spec = pl.BlockSpec(memory_space=pl.ANY)
    return pl.pallas_call(
        body, name="weights_all_gather",
        out_shape=[jax.ShapeDtypeStruct((N_DEV,) + s.shape, s.dtype) for s in shards],
        in_specs=[any_spec] * n, out_specs=[any_spec] * n,
        scratch_shapes=[pltpu.SemaphoreType.DMA(((N_DEV - 1) * n,)),
                        pltpu.SemaphoreType.DMA(((N_DEV - 1) * n,)),
                        pltpu.SemaphoreType.DMA((n,))],
    )(*shards)


def _grad_exchange(chunked):
    n = len(chunked)

    def body(*refs):
        ins, outs = refs[:n], refs[n:2 * n]
        send_sems, recv_sems, local_sems = refs[2 * n:]
        me = _my_index()
        local = [pltpu.make_async_copy(ins[a].at[me], outs[a].at[me], local_sems.at[a]) for a in range(n)]
        for cp in local:
            cp.start()
        sends = []
        for k in range(1, N_DEV):
            dst = me ^ k
            for a in range(n):
                cp = pltpu.make_async_remote_copy(
                    src_ref=ins[a].at[dst], dst_ref=outs[a].at[me],
                    send_sem=send_sems.at[(k - 1) * n + a], recv_sem=recv_sems.at[(k - 1) * n + a],
                    device_id=_peer(k), device_id_type=MESH)
                cp.start()
                sends.append(cp)
        for k in range(1, N_DEV):
            src = me ^ k
            for a in range(n):
                pltpu.make_async_remote_copy(
                    src_ref=ins[a].at[src], dst_ref=outs[a].at[src],
                    send_sem=send_sems.at[(k - 1) * n + a], recv_sem=recv_sems.at[(k - 1) * n + a],
                    device_id=_peer(k), device_id_type=MESH).wait_recv()
        for cp in sends:
            cp.wait_send()
        for cp in local:
            cp.wait()

    any_spec = pl.BlockSpec(memory_space=pl.ANY)
    return pl.pallas_call(
        body, name="grad_exchange",
        out_shape=[jax.ShapeDtypeStruct(s.shape, s.dtype) for s in chunked],
        in_specs=[any_spec] * n, out_specs=[any_spec] * n,
        scratch_shapes=[pltpu.SemaphoreType.DMA(((N_DEV - 1) * n,)),
                        pltpu.SemaphoreType.DMA(((N_DEV - 1) * n,)),
                        pltpu.SemaphoreType.DMA((n,))],
    )(*chunked)


def _log_sigmoid(u):
    return jnp.minimum(u, 0.0) - jnp.log1p(jnp.exp(-jnp.abs(u)))


def _proj_fwd(x, w_qkv, w_f, bf_pad, n_fox):
    S, D = x.shape
    N = w_qkv.shape[1]
    TM = ROW_TILE
    tri = _tri(TM, lambda r, c: c <= r)

    def body(x_ref, wq_ref, wf_ref, bf_ref, tri_ref, qkv_ref, u_ref, c_ref, run_ref):
        @pl.when(pl.program_id(0) == 0)
        def _():
            run_ref[...] = jnp.zeros_like(run_ref)

        xb = x_ref[...].astype(BF16)
        for n0 in range(0, N, D):
            qkv_ref[:, n0:n0 + D] = _dot(xb, wq_ref[:, n0:n0 + D]).astype(BF16)
        u = _dot(xb, wf_ref[...]) + bf_ref[...]
        lane = lax.broadcasted_iota(I32, u.shape, 1)
        logf = jnp.where(lane < n_fox, _log_sigmoid(u), 0.0)
        u_ref[...] = u
        hi, mid, lo = _split3(logf)
        t = tri_ref[...]
        cs = _dot(t, hi) + _dot(t, mid) + _dot(t, lo) + run_ref[...]
        c_ref[...] = cs
        run_ref[...] = cs[TM - 1:TM, :]

    return pl.pallas_call(
        body, name="proj_fwd", grid=(S // TM,),
        in_specs=[pl.BlockSpec((TM, D), lambda i: (i, 0)), _full(w_qkv.shape), _full(w_f.shape),
                  _full(bf_pad.shape), _full(tri.shape)],
        out_specs=[pl.BlockSpec((TM, N), lambda i: (i, 0)), pl.BlockSpec((TM, LANES), lambda i: (i, 0)),
                   pl.BlockSpec((TM, LANES), lambda i: (i, 0))],
        out_shape=[jax.ShapeDtypeStruct((S, N), BF16), jax.ShapeDtypeStruct((S, LANES), F32),
                   jax.ShapeDtypeStruct((S, LANES), F32)],
        scratch_shapes=[pltpu.VMEM((1, LANES), F32)],
        compiler_params=_cparams(("arbitrary",)),
    )(x, w_qkv, w_f, bf_pad, tri)


def _post_attn_fwd(o_sb, o_fx, x, g_cat, gmat, w_out, ln_g, ln_b):
    S, D = x.shape
    H = D // 2
    TM = ROW_TILE

    def body(osb_ref, ofx_ref, x_ref, g_ref, gm_ref, wo_ref, lg_ref, lb_ref,
             h1_ref, xh_ref, rs_ref, on_ref, rr_ref):
        o = jnp.concatenate([osb_ref[...], ofx_ref[...]], axis=1)
        ms = _dot_acc(o * o, gm_ref[...]) * (1.0 / HEAD_DIM)
        r = lax.rsqrt(ms + RMS_EPS)
        onb = (o * r * g_ref[...]).astype(BF16)
        hp = ALPHA * x_ref[...] + _dot(onb, wo_ref[...])
        mu = jnp.mean(hp, axis=-1, keepdims=True)
        d = hp - mu
        rstd = lax.rsqrt(jnp.mean(d * d, axis=-1, keepdims=True) + LN_EPS)
        xh = d * rstd
        h1_ref[...] = xh * lg_ref[...] + lb_ref[...]
        xh_ref[...] = xh
        rs_ref[...] = jnp.broadcast_to(rstd, (TM, LANES))
        on_ref[...] = onb
        rr_ref[...] = r

    row = lambda w: pl.BlockSpec((TM, w), lambda i: (i, 0))
    return pl.pallas_call(
        body, name="post_attn_fwd", grid=(S // TM,),
        in_specs=[row(H), row(H), row(D), _full((1, D)), _full((D, D)), _full((D, D)), _full((1, D)), _full((1, D))],
        out_specs=[row(D), row(D), row(LANES), row(D), row(D)],
        out_shape=[jax.ShapeDtypeStruct((S, D), F32), jax.ShapeDtypeStruct((S, D), F32),
                   jax.ShapeDtypeStruct((S, LANES), F32), jax.ShapeDtypeStruct((S, D), BF16),
                   jax.ShapeDtypeStruct((S, D), F32)],
        compiler_params=_cparams(("arbitrary",)),
    )(o_sb, o_fx, x, g_cat, gmat, w_out, ln_g, ln_b)


def _ln_bwd(dxh, xh, rstd):
    m1 = jnp.mean(dxh, axis=-1, keepdims=True)
    m2 = jnp.mean(dxh * xh, axis=-1, keepdims=True)
    return rstd * (dxh - m1 - xh * m2)


def _mlp_fwd(h1, target, w_gu, w_dn, ln_g, ln_b):
    S, D = h1.shape
    F = w_dn.shape[0]
    TM = ROW_TILE
    FC = F // 2

    def body(h1_ref, tg_ref, wgu_hbm, wdn_hbm, lg_ref, lb_ref, gu_ref, act_ref, dyp_ref, sm_ref, wgu, wdn):
        @pl.when(pl.program_id(0) == 0)
        def _():
            pltpu.sync_copy(wgu_hbm, wgu)
            pltpu.sync_copy(wdn_hbm, wdn)
            sm_ref[...] = jnp.zeros_like(sm_ref)

        h1v = h1_ref[...]
        hb = h1v.astype(BF16)
        ff = jnp.zeros((TM, D), F32)
        for c0 in range(0, F, FC):
            g = _dot(hb, wgu[:, c0:c0 + FC])
            u = _dot(hb, wgu[:, F + c0:F + c0 + FC])
            gu_ref[:, c0:c0 + FC] = g
            gu_ref[:, F + c0:F + c0 + FC] = u
            ab = ((g * jax.nn.sigmoid(g)) * u).astype(BF16)
            act_ref[:, c0:c0 + FC] = ab
            ff = ff + _dot(ab, wdn[c0:c0 + FC, :])
        yp = ALPHA * h1v + ff
        mu = jnp.mean(yp, axis=-1, keepdims=True)
        d = yp - mu
        rstd = lax.rsqrt(jnp.mean(d * d, axis=-1, keepdims=True) + LN_EPS)
        xh = d * rstd
        err = (xh * lg_ref[...] + lb_ref[...]) - tg_ref[...]
        dy = err * (1.0 / D)
        sm_ref[0:1, :] += jnp.sum(dy * xh, axis=0, keepdims=True)
        sm_ref[1:2, :] += jnp.sum(dy, axis=0, keepdims=True)
        sm_ref[2:3, :] += jnp.sum(err * err, axis=0, keepdims=True)
        dyp_ref[...] = _ln_bwd(dy * lg_ref[...], xh, rstd)

    row = lambda w: pl.BlockSpec((TM, w), lambda i: (i, 0))
    hbm = pl.BlockSpec(memory_space=pl.ANY)
    return pl.pallas_call(
        body, name="mlp_fwd", grid=(S // TM,),
        in_specs=[row(D), row(D), hbm, hbm, _full((1, D)), _full((1, D))],
        out_specs=[row(2 * F), row(F), row(D), _full((8, D))],
        out_shape=[jax.ShapeDtypeStruct((S, 2 * F), F32), jax.ShapeDtypeStruct((S, F), BF16),
                   jax.ShapeDtypeStruct((S, D), F32), jax.ShapeDtypeStruct((8, D), F32)],
        scratch_shapes=[pltpu.VMEM(w_gu.shape, BF16), pltpu.VMEM(w_dn.shape, BF16)],
        compiler_params=_cparams(("arbitrary",)),
    )(h1, target, w_gu, w_dn, ln_g, ln_b)


def _mlp_bwd(gu, dyp, w_guT, w_dnT):
    S, D = dyp.shape
    F = w_dnT.shape[1]
    TM = ROW_TILE
    FC = F // 2

    def body(gu_ref, dyp_ref, wguT_hbm, wdnT_hbm, dgu_ref, dh1_ref, wguT, wdnT):
        @pl.when(pl.program_id(0) == 0)
        def _():
            pltpu.sync_copy(wguT_hbm, wguT)
            pltpu.sync_copy(wdnT_hbm, wdnT)

        dypv = dyp_ref[...]
        db = dypv.astype(BF16)
        dh1 = ALPHA * dypv
        for c0 in range(0, F, FC):
            dact = _dot(db, wdnT[:, c0:c0 + FC])
            g = gu_ref[:, c0:c0 + FC]
            u = gu_ref[:, F + c0:F + c0 + FC]
            sg = jax.nn.sigmoid(g)
            dgb = (dact * u * (sg * (1.0 + g * (1.0 - sg)))).astype(BF16)
            dub = (dact * (g * sg)).astype(BF16)
            dgu_ref[:, c0:c0 + FC] = dgb
            dgu_ref[:, F + c0:F + c0 + FC] = dub
            dh1 = dh1 + _dot(dgb, wguT[c0:c0 + FC, :]) + _dot(dub, wguT[F + c0:F + c0 + FC, :])
        dh1_ref[...] = dh1

    row = lambda w: pl.BlockSpec((TM, w), lambda i: (i, 0))
    hbm = pl.BlockSpec(memory_space=pl.ANY)
    return pl.pallas_call(
        body, name="mlp_bwd", grid=(S // TM,),
        in_specs=[row(2 * F), row(D), hbm, hbm],
        out_specs=[row(2 * F), row(D)],
        out_shape=[jax.ShapeDtypeStruct((S, 2 * F), BF16), jax.ShapeDtypeStruct((S, D), F32)],
        scratch_shapes=[pltpu.VMEM(w_guT.shape, BF16), pltpu.VMEM(w_dnT.shape, BF16)],
        compiler_params=_cparams(("arbitrary",)),
    )(gu, dyp, w_guT, w_dnT)


def _post_attn_bwd(dh1, xh, rs, ln_g, o_sb, o_fx, rr, g_cat, gmat, w_outT):
    S, D = dh1.shape
    H = D // 2
    TM = ROW_TILE

    def body(dh1_ref, xh_ref, rs_ref, lg_ref, osb_ref, ofx_ref, rr_ref, g_ref, gm_ref, woT_ref,
             dxa_ref, dmix_ref, dosb_ref, dofx_ref, sm_ref):
        @pl.when(pl.program_id(0) == 0)
        def _():
            sm_ref[...] = jnp.zeros_like(sm_ref)

        dh = dh1_ref[...]
        xhv = xh_ref[...]
        dhp = _ln_bwd(dh * lg_ref[...], xhv, rs_ref[:, 0:1])
        dxa_ref[...] = ALPHA * dhp
        dmb = dhp.astype(BF16)
        dmix_ref[...] = dmb
        don = _dot(dmb, woT_ref[...])
        o = jnp.concatenate([osb_ref[...], ofx_ref[...]], axis=1)
        r = rr_ref[...]
        u = don * g_ref[...]
        t = _dot_acc(u * o, gm_ref[...]) * (1.0 / HEAD_DIM)
        do = r * u - o * (r * r * r) * t
        dosb_ref[...] = do[:, :H]
        dofx_ref[...] = do[:, H:]
        sm_ref[0:1, :] += jnp.sum(dh * xhv, axis=0, keepdims=True)
        sm_ref[1:2, :] += jnp.sum(dh, axis=0, keepdims=True)
        sm_ref[2:3, :] += jnp.sum(don * o * r, axis=0, keepdims=True)

    row = lambda w: pl.BlockSpec((TM, w), lambda i: (i, 0))
    return pl.pallas_call(
        body, name="post_attn_bwd", grid=(S // TM,),
        in_specs=[row(D), row(D), row(LANES), _full((1, D)), row(H), row(H), row(D), _full((1, D)),
                  _full((D, D)), _full((D, D))],
        out_specs=[row(D), row(D), row(H), row(H), _full((8, D))],
        out_shape=[jax.ShapeDtypeStruct((S, D), F32), jax.ShapeDtypeStruct((S, D), BF16),
                   jax.ShapeDtypeStruct((S, H), F32), jax.ShapeDtypeStruct((S, H), F32),
                   jax.ShapeDtypeStruct((8, D), F32)],
        compiler_params=_cparams(("arbitrary",)),
    )(dh1, xh, rs, ln_g, o_sb, o_fx, rr, g_cat, gmat, w_outT)


def _proj_bwd(dxa, pieces, dc, u, w_qkvT, w_fT, n_fox):
    S, D = dxa.shape
    H = D // 2
    TM = ROW_TILE
    nT = S // TM
    tri = _tri(TM, lambda r, c: c >= r)
    n_p = len(pieces)

    def body(*refs):
        dxa_ref = refs[0]
        p_refs = refs[1:1 + n_p]
        dc_ref, u_ref, wq_ref, wf_ref, tri_ref, dx_ref, df_ref, sm_ref, run_ref = refs[1 + n_p:]

        @pl.when(pl.program_id(0) == 0)
        def _():
            run_ref[...] = jnp.zeros_like(run_ref)
            sm_ref[...] = jnp.zeros_like(sm_ref)

        hi, mid, lo = _split3(dc_ref[...])
        t = tri_ref[...]
        dlogf = _dot(t, hi) + _dot(t, mid) + _dot(t, lo) + run_ref[...]
        run_ref[...] = dlogf[0:1, :]
        uv = u_ref[...]
        lane = lax.broadcasted_iota(I32, uv.shape, 1)
        df = jnp.where(lane < n_fox, dlogf * jax.nn.sigmoid(-uv), 0.0)
        sm_ref[0:1, :] += jnp.sum(df, axis=0, keepdims=True)
        dfb = df.astype(BF16)
        df_ref[...] = dfb
        acc = dxa_ref[...] + _dot(dfb, wf_ref[...])
        for a in range(n_p):
            acc = acc + _dot(p_refs[a][...].astype(BF16), wq_ref[a * H:(a + 1) * H, :])
        dx_ref[...] = acc

    rev = lambda w: pl.BlockSpec((TM, w), lambda i: (nT - 1 - i, 0))
    return pl.pallas_call(
        body, name="proj_bwd", grid=(nT,),
        in_specs=[rev(D)] + [rev(H)] * n_p + [rev(LANES), rev(LANES), _full(w_qkvT.shape), _full(w_fT.shape),
                                               _full(tri.shape)],
        out_specs=[rev(D), rev(LANES), _full((8, LANES))],
        out_shape=[jax.ShapeDtypeStruct((S, D), F32), jax.ShapeDtypeStruct((S, LANES), BF16),
                   jax.ShapeDtypeStruct((8, LANES), F32)],
        scratch_shapes=[pltpu.VMEM((1, LANES), F32)],
        compiler_params=_cparams(("arbitrary",)),
    )(dxa, *pieces, dc, u, w_qkvT, w_fT, tri)


def _matmul_tn(a, b, name):
    S, M = a.shape
    N = b.shape[1]
    TK = 512 if S % 512 == 0 else ROW_TILE
    TN = next(t for t in (512, 384, 256, 128) if N % t == 0)
    TMo = next(t for t in (1024, 512, 256, 128) if M % t == 0)

    def body(a_ref, b_ref, o_ref):
        @pl.when(pl.program_id(2) == 0)
        def _():
            o_ref[...] = jnp.zeros_like(o_ref)

        o_ref[...] += _dot_tn(a_ref[...].astype(BF16), b_ref[...].astype(BF16))

    return pl.pallas_call(
        body, name=name, grid=(M // TMo, N // TN, S // TK),
        in_specs=[pl.BlockSpec((TK, TMo), lambda m, n, k: (k, m)), pl.BlockSpec((TK, TN), lambda m, n, k: (k, n))],
        out_specs=pl.BlockSpec((TMo, TN), lambda m, n, k: (m, n)),
        out_shape=jax.ShapeDtypeStruct((M, N), F32),
        compiler_params=_cparams(("arbitrary", "arbitrary", "arbitrary")),
    )(a, b)


def _half_mask(half):
    lane = lax.broadcasted_iota(I32, (1, LANES), 1)
    return (lane >= half * HEAD_DIM) & (lane < half * HEAD_DIM + HEAD_DIM)


def _softplus_neg_abs(z):
    return jnp.log1p(jnp.exp(-jnp.abs(z)))


def _sb_fwd(qkv, n_pair):
    S = qkv.shape[0]
    B = ATT_BLOCK
    nq = S // B
    us = _tri(B, lambda r, c: r > c)

    def body(q_ref, k_ref, v_ref, us_ref, o_ref, st_ref, acc_ref, r_ref):
        i, half = pl.program_id(1), pl.program_id(2)
        hm = _half_mask(half)
        qv = q_ref[...]
        qs = jnp.where(hm, qv, jnp.zeros_like(qv)) * SCALE
        row = lax.broadcasted_iota(I32, (B, B), 0)
        col = lax.broadcasted_iota(I32, (B, B), 1)
        tri = col < row
        acc_ref[...] = jnp.zeros_like(acc_ref)
        r_ref[...] = jnp.zeros_like(r_ref)

        def block(j, diag):
            off = pl.multiple_of(j * B, B)
            kj = k_ref[pl.ds(off, B), :]
            vj = v_ref[pl.ds(off, B), :]
            z = _dot_nt(qs, kj)
            sp = _softplus_neg_abs(z)
            b = jnp.minimum(-z, 0.0) - sp
            a = jnp.minimum(z, 0.0) - sp
            if diag:
                b = jnp.where(tri, b, 0.0)
            lexc = _dot_acc(b, us_ref[...])
            w = jnp.exp(a + (r_ref[...] + lexc))
            if diag:
                w = jnp.where(tri, w, 0.0)
            acc_ref[...] += _dot(w.astype(BF16), vj)
            r_ref[...] += lexc[:, 0:1] + b[:, 0:1]

        block(i, True)

        def step(t, carry):
            block(i - 1 - t, False)
            return carry

        lax.fori_loop(0, i, step, 0)
        res = jnp.where(hm, acc_ref[...], 0.0)

        @pl.when(half == 0)
        def _():
            o_ref[...] = res

        @pl.when(half == 1)
        def _():
            o_ref[...] += res

        st_ref[0] = jnp.broadcast_to(r_ref[...], (B, LANES))

    return pl.pallas_call(
        body, name="sb_attn_fwd", grid=(n_pair, nq, 2),
        in_specs=[pl.BlockSpec((B, LANES), lambda p, i, h: (i, p)),
                  pl.BlockSpec((S, LANES), lambda p, i, h: (0, n_pair + p)),
                  pl.BlockSpec((S, LANES), lambda p, i, h: (0, 2 * n_pair + p)),
                  _full((B, B))],
        out_specs=[pl.BlockSpec((B, LANES), lambda p, i, h: (i, p)),
                   pl.BlockSpec((1, B, LANES), lambda p, i, h: (2 * p + h, i, 0))],
        out_shape=[jax.ShapeDtypeStruct((S, n_pair * LANES), F32),
                   jax.ShapeDtypeStruct((2 * n_pair, S, LANES), F32)],
        scratch_shapes=[pltpu.VMEM((B, LANES), F32), pltpu.VMEM((B, 1), F32)],
        compiler_params=_cparams(("arbitrary", "arbitrary", "arbitrary")),
    )(qkv, qkv, qkv, us)


def _sb_bwd(qkv, do, st, n_pair):
    S = qkv.shape[0]
    B = ATT_BLOCK
    nq = S // B
    us = _tri(B, lambda r, c: r > c)
    ti = _tri(B, lambda r, c: r <= c)

    def body(q_ref, k_ref, v_ref, do_ref, st_ref, us_ref, ti_ref, dq_ref, dk_ref, dv_ref,
             dqa_ref, pr_ref, er_ref):
        i, half = pl.program_id(1), pl.program_id(2)

        @pl.when((i == 0) & (half == 0))
        def _():
            dk_ref[...] = jnp.zeros_like(dk_ref)
            dv_ref[...] = jnp.zeros_like(dv_ref)

        hm = _half_mask(half)
        qv = q_ref[...]
        qs = jnp.where(hm, qv, jnp.zeros_like(qv)) * SCALE
        dob = jnp.where(hm, do_ref[...], 0.0).astype(BF16)
        rproc = st_ref[0][:, 0:1]
        row = lax.broadcasted_iota(I32, (B, B), 0)
        col = lax.broadcasted_iota(I32, (B, B), 1)
        tri = col < row
        dqa_ref[...] = jnp.zeros_like(dqa_ref)
        pr_ref[...] = jnp.zeros_like(pr_ref)
        er_ref[...] = jnp.zeros_like(er_ref)

        def block(j, diag):
            off = pl.multiple_of(j * B, B)
            kj = k_ref[pl.ds(off, B), :]
            vj = v_ref[pl.ds(off, B), :]
            z = _dot_nt(qs, kj)
            sp = _softplus_neg_abs(z)
            b = jnp.minimum(-z, 0.0) - sp
            a = jnp.minimum(z, 0.0) - sp
            if diag:
                b = jnp.where(tri, b, 0.0)
            lexc = _dot_acc(b, us_ref[...])
            pr_new = pr_ref[...] + (lexc[:, 0:1] + b[:, 0:1])
            pr_ref[...] = pr_new
            w = jnp.exp(a + ((rproc - pr_new) + lexc))
            if diag:
                w = jnp.where(tri, w, 0.0)
            e = _dot_nt(dob, vj) * w
            einc = _dot_acc(e, ti_ref[...])
            big_e = er_ref[...] + (einc - e)
            er_ref[...] += einc[:, B - 1:B]
            eb = jnp.exp(b)
            dzb = (e * eb - big_e * (1.0 - eb)).astype(BF16)
            dqa_ref[...] += _dot(dzb, kj)
            dk_ref[pl.ds(off, B), :] += _dot_tn(dzb, qs)
            dv_ref[pl.ds(off, B), :] += _dot_tn(w.astype(BF16), dob)

        def step(j, carry):
            block(j, False)
            return carry

        lax.fori_loop(0, i, step, 0)
        block(i, True)
        res = jnp.where(hm, dqa_ref[...] * SCALE, 0.0)

        @pl.when(half == 0)
        def _():
            dq_ref[...] = res

        @pl.when(half == 1)
        def _():
            dq_ref[...] += res

    W = n_pair * LANES
    return pl.pallas_call(
        body, name="sb_attn_bwd", grid=(n_pair, nq, 2),
        in_specs=[pl.BlockSpec((B, LANES), lambda p, i, h: (i, p)),
                  pl.BlockSpec((S, LANES), lambda p, i, h: (0, n_pair + p)),
                  pl.BlockSpec((S, LANES), lambda p, i, h: (0, 2 * n_pair + p)),
                  pl.BlockSpec((B, LANES), lambda p, i, h: (i, p)),
                  pl.BlockSpec((1, B, LANES), lambda p, i, h: (2 * p + h, i, 0)),
                  _full((B, B)), _full((B, B))],
        out_specs=[pl.BlockSpec((B, LANES), lambda p, i, h: (i, p)),
                   pl.BlockSpec((S, LANES), lambda p, i, h: (0, p)),
                   pl.BlockSpec((S, LANES), lambda p, i, h: (0, p))],
        out_shape=[jax.ShapeDtypeStruct((S, W), F32)] * 3,
        scratch_shapes=[pltpu.VMEM((B, LANES), F32), pltpu.VMEM((B, 1), F32), pltpu.VMEM((B, 1), F32)],
        compiler_params=_cparams(("arbitrary", "arbitrary", "arbitrary")),
    )(qkv, qkv, qkv, do, st, us, ti)


def _head_column(blk, head):
    lane = lax.broadcasted_iota(I32, (1, LANES), 1)
    return jnp.sum(jnp.where(lane == head, blk, 0.0), axis=1, keepdims=True)


def _fox_fwd(qkv, c, c_rows, n_pair):
    S = qkv.shape[0]
    B = ATT_BLOCK
    nq = S // B

    def body(q_ref, k_ref, v_ref, c_ref, cr_ref, o_ref, st_ref, acc_ref, m_ref, l_ref):
        p, i, half = pl.program_id(0), pl.program_id(1), pl.program_id(2)
        hm = _half_mask(half)
        qv = q_ref[...]
        qs = jnp.where(hm, qv, jnp.zeros_like(qv)) * SCALE
        ccol = _head_column(c_ref[...], 2 * p + half)
        row = lax.broadcasted_iota(I32, (B, B), 0)
        col = lax.broadcasted_iota(I32, (B, B), 1)
        acc_ref[...] = jnp.zeros_like(acc_ref)
        l_ref[...] = jnp.zeros_like(l_ref)
        m_ref[...] = jnp.full_like(m_ref, NEG_BIG)

        def block(j, diag):
            off = pl.multiple_of(j * B, B)
            kj = k_ref[pl.ds(off, B), :]
            vj = v_ref[pl.ds(off, B), :]
            s = _dot_nt(qs, kj) + (ccol - cr_ref[0, pl.ds(j, 1), :])
            if diag:
                s = jnp.where(col <= row, s, NEG_BIG)
            m_old = m_ref[...]
            m_new = jnp.maximum(m_old, jnp.max(s, axis=1, keepdims=True))
            alpha = jnp.exp(m_old - m_new)
            pv = jnp.exp(s - m_new)
            l_ref[...] = alpha * l_ref[...] + jnp.sum(pv, axis=1, keepdims=True)
            acc_ref[...] = alpha * acc_ref[...] + _dot(pv.astype(BF16), vj)
            m_ref[...] = m_new

        block(i, True)

        def step(t, carry):
            block(i - 1 - t, False)
            return carry

        lax.fori_loop(0, i, step, 0)
        res = jnp.where(hm, acc_ref[...] / l_ref[...], 0.0)

        @pl.when(half == 0)
        def _():
            o_ref[...] = res

        @pl.when(half == 1)
        def _():
            o_ref[...] += res

        st_ref[0] = jnp.broadcast_to(m_ref[...] + jnp.log(l_ref[...]), (B, LANES))

    return pl.pallas_call(
        body, name="fox_attn_fwd", grid=(n_pair, nq, 2),
        in_specs=[pl.BlockSpec((B, LANES), lambda p, i, h: (i, 3 * n_pair + p)),
                  pl.BlockSpec((S, LANES), lambda p, i, h: (0, 4 * n_pair + p)),
                  pl.BlockSpec((S, LANES), lambda p, i, h: (0, 5 * n_pair + p)),
                  pl.BlockSpec((B, LANES), lambda p, i, h: (i, 0)),
                  pl.BlockSpec((1, nq, B), lambda p, i, h: (2 * p + h, 0, 0))],
        out_specs=[pl.BlockSpec((B, LANES), lambda p, i, h: (i, p)),
                   pl.BlockSpec((1, B, LANES), lambda p, i, h: (2 * p + h, i, 0))],
        out_shape=[jax.ShapeDtypeStruct((S, n_pair * LANES), F32),
                   jax.ShapeDtypeStruct((2 * n_pair, S, LANES), F32)],
        scratch_shapes=[pltpu.VMEM((B, LANES), F32), pltpu.VMEM((B, 1), F32), pltpu.VMEM((B, 1), F32)],
        compiler_params=_cparams(("arbitrary", "arbitrary", "arbitrary")),
    )(qkv, qkv, qkv, c, c_rows)


def _fox_bwd(qkv, do, o, st, c, c_rows, n_pair):
    S = qkv.shape[0]
    B = ATT_BLOCK
    nq = S // B

    def body(q_ref, k_ref, v_ref, do_ref, o_ref, st_ref, c_ref, cr_ref, dq_ref, dk_ref, dv_ref, dc_ref,
             dcq_ref, dqa_ref, rs_ref):
        p, i, half = pl.program_id(0), pl.program_id(1), pl.program_id(2)

        @pl.when((i == 0) & (half == 0))
        def _():
            dk_ref[...] = jnp.zeros_like(dk_ref)
            dv_ref[...] = jnp.zeros_like(dv_ref)
            dc_ref[...] = jnp.zeros_like(dc_ref)

        hm = _half_mask(half)
        qv = q_ref[...]
        qs = jnp.where(hm, qv, jnp.zeros_like(qv)) * SCALE
        dov = jnp.where(hm, do_ref[...], 0.0)
        dob = dov.astype(BF16)
        delta = jnp.sum(dov * o_ref[...], axis=1, keepdims=True)
        lse = st_ref[0][:, 0:1]
        ccol = _head_column(c_ref[...], 2 * p + half) - lse
        row = lax.broadcasted_iota(I32, (B, B), 0)
        col = lax.broadcasted_iota(I32, (B, B), 1)
        dqa_ref[...] = jnp.zeros_like(dqa_ref)
        rs_ref[...] = jnp.zeros_like(rs_ref)

        def block(j, diag):
            off = pl.multiple_of(j * B, B)
            kj = k_ref[pl.ds(off, B), :]
            vj = v_ref[pl.ds(off, B), :]
            pv = jnp.exp(_dot_nt(qs, kj) + (ccol - cr_ref[0, pl.ds(j, 1), :]))
            if diag:
                pv = jnp.where(col <= row, pv, 0.0)
            ds = pv * (_dot_nt(dob, vj) - delta)
            dsb = ds.astype(BF16)
            dqa_ref[...] += _dot(dsb, kj)
            dk_ref[pl.ds(off, B), :] += _dot_tn(dsb, qs)
            dv_ref[pl.ds(off, B), :] += _dot_tn(pv.astype(BF16), dob)
            dc_ref[0, half, pl.ds(j, 1), :] -= jnp.sum(ds, axis=0, keepdims=True)
            rs_ref[...] += jnp.sum(ds, axis=1, keepdims=True)

        def step(j, carry):
            block(j, False)
            return carry

        lax.fori_loop(0, i, step, 0)
        block(i, True)
        res = jnp.where(hm, dqa_ref[...] * SCALE, 0.0)

        @pl.when(half == 0)
        def _():
            dq_ref[...] = res

        @pl.when(half == 1)
        def _():
            dq_ref[...] += res

        dcq_ref[0] = jnp.broadcast_to(rs_ref[...], (B, LANES))

    W = n_pair * LANES
    return pl.pallas_call(
        body, name="fox_attn_bwd", grid=(n_pair, nq, 2),
        in_specs=[pl.BlockSpec((B, LANES), lambda p, i, h: (i, 3 * n_pair + p)),
                  pl.BlockSpec((S, LANES), lambda p, i, h: (0, 4 * n_pair + p)),
                  pl.BlockSpec((S, LANES), lambda p, i, h: (0, 5 * n_pair + p)),
                  pl.BlockSpec((B, LANES), lambda p, i, h: (i, p)),
                  pl.BlockSpec((B, LANES), lambda p, i, h: (i, p)),
                  pl.BlockSpec((1, B, LANES), lambda p, i, h: (2 * p + h, i, 0)),
                  pl.BlockSpec((B, LANES), lambda p, i, h: (i, 0)),
                  pl.BlockSpec((1, nq, B), lambda p, i, h: (2 * p + h, 0, 0))],
        out_specs=[pl.BlockSpec((B, LANES), lambda p, i, h: (i, p)),
                   pl.BlockSpec((S, LANES), lambda p, i, h: (0, p)),
                   pl.BlockSpec((S, LANES), lambda p, i, h: (0, p)),
                   pl.BlockSpec((1, 2, nq, B), lambda p, i, h: (p, 0, 0, 0)),
                   pl.BlockSpec((1, B, LANES), lambda p, i, h: (2 * p + h, i, 0))],
        out_shape=[jax.ShapeDtypeStruct((S, W), F32)] * 3 + [jax.ShapeDtypeStruct((n_pair, 2, nq, B), F32),
                                                            jax.ShapeDtypeStruct((2 * n_pair, S, LANES), F32)],
        scratch_shapes=[pltpu.VMEM((B, LANES), F32), pltpu.VMEM((B, 1), F32)],
        compiler_params=_cparams(("arbitrary", "arbitrary", "arbitrary")),
    )(qkv, qkv, qkv, do, o, st, c, c_rows)


def _adam(w, g, m, v):
    m = ADAM_B1 * m + (1.0 - ADAM_B1) * g
    v = ADAM_B2 * v + (1.0 - ADAM_B2) * (g * g)
    m_hat = m / (1.0 - ADAM_B1 ** ADAM_STEP)
    v_hat = v / (1.0 - ADAM_B2 ** ADAM_STEP)
    delta = -ADAM_LR * (m_hat / (jnp.sqrt(v_hat) + ADAM_EPS) + ADAM_WD * w)
    return delta, m, v


def _reduce_adam(landing, w, m, v, name):
    R, C = w.shape
    TR = next(t for t in (256, 128, 88, 64, 8) if R % t == 0)

    def body(l_ref, w_ref, m_ref, v_ref, g_ref, d_ref, nm_ref, nv_ref):
        g = l_ref[0]
        for s in range(1, N_DEV):
            g = g + l_ref[s]
        d, nm, nv = _adam(w_ref[...], g, m_ref[...], v_ref[...])
        g_ref[...] = g
        d_ref[...] = d
        nm_ref[...] = nm
        nv_ref[...] = nv

    blk = pl.BlockSpec((TR, C), lambda i: (i, 0))
    return pl.pallas_call(
        body, name=name, grid=(R // TR,),
        in_specs=[pl.BlockSpec((N_DEV, TR, C), lambda i: (0, i, 0)), blk, blk, blk],
        out_specs=[blk] * 4,
        out_shape=[jax.ShapeDtypeStruct((R, C), F32)] * 4,
        compiler_params=_cparams(("arbitrary",)),
    )(landing, w, m, v)


def _reduce_adam_small(landing, w, m, v):
    R, C = w.shape

    def body(l_ref, w_ref, m_ref, v_ref, g_ref, d_ref, nm_ref, nv_ref, loss_ref):
        g = l_ref[0]
        for s in range(1, N_DEV):
            g = g + l_ref[s]
        d, nm, nv = _adam(w_ref[...], g, m_ref[...], v_ref[...])
        g_ref[...] = g
        d_ref[...] = d
        nm_ref[...] = nm
        nv_ref[...] = nv
        loss_ref[...] = jnp.broadcast_to(0.5 * jnp.sum(g[7:8, :], axis=1, keepdims=True), (1, LANES))

    return pl.pallas_call(
        body, name="reduce_adam_small",
        out_shape=[jax.ShapeDtypeStruct((R, C), F32)] * 4 + [jax.ShapeDtypeStruct((1, LANES), F32)],
    )(landing, w, m, v)


def _pad_lanes(a, width):
    return jnp.pad(a, ((0, 0), (0, width - a.shape[1])))


def _pack_small(D, n_fox, g_cat, l1g, l1b, l2g, l2b, bf, last):
    return jnp.concatenate([g_cat, l1g, l1b, l2g, l2b, _pad_lanes(bf, D), jnp.zeros((1, D), F32), last], axis=0)


def kernel(x, w_in, b_f, g_sb, g_fox, w_out, ln1_g, ln1_b, ln2_g, ln2_b, w_gate_up, w_down, loss_target, m_w_in, m_b_f, m_g_sb, m_g_fox, m_w_out, m_ln1_g, m_ln1_b, m_ln2_g, m_ln2_b, m_w_gate_up, m_w_down, v_w_in, v_b_f, v_g_sb, v_g_fox, v_w_out, v_ln1_g, v_ln1_b, v_ln2_g, v_ln2_b, v_w_gate_up, v_w_down):
    x2, tgt = x[0], loss_target[0]
    S, D = x2.shape
    W = D // 2
    n_pair = W // LANES
    n_fox = W // HEAD_DIM
    F = w_down.shape[1] * N_DEV
    B = ATT_BLOCK
    nq = S // B

    g_in, g_out, g_gu, g_dn = _all_gather([w_in[0].astype(BF16), w_out[0].astype(BF16),
                                           w_gate_up[0].astype(BF16), w_down[0].astype(BF16)])
    w_in_full = g_in.transpose(1, 0, 2).reshape(D, -1)
    w_qkv = w_in_full[:, :6 * W]
    w_f = _pad_lanes(w_in_full[:, 6 * W:], LANES)
    w_o = g_out.reshape(D, D)
    w_gu = g_gu.transpose(1, 0, 2).reshape(D, 2 * F)
    w_dn = g_dn.reshape(F, D)
    gmat = _tri(D, lambda r, c: (r // HEAD_DIM) == (c // HEAD_DIM))
    g_cat = jnp.concatenate([g_sb, g_fox], axis=1)

    qkv, u, c = _proj_fwd(x2, w_qkv, w_f, _pad_lanes(b_f, LANES), n_fox)
    c_rows = c[:, :n_fox].T.reshape(n_fox, nq, B)
    o_sb, st_sb = _sb_fwd(qkv, n_pair)
    o_fx, st_fx = _fox_fwd(qkv, c, c_rows, n_pair)
    h1, xh1, rs1, on_b, rr = _post_attn_fwd(o_sb, o_fx, x2, g_cat, gmat, w_o, ln1_g, ln1_b)
    gu, act_b, dyp, sm2 = _mlp_fwd(h1, tgt, w_gu, w_dn, ln2_g, ln2_b)

    dgu_b, dh1 = _mlp_bwd(gu, dyp, w_gu.T, w_dn.T)
    dxa, dmix_b, do_sb, do_fx, sm1 = _post_attn_bwd(dh1, xh1, rs1, ln1_g, o_sb, o_fx, rr, g_cat, gmat, w_o.T)
    dq_sb, dk_sb, dv_sb = _sb_bwd(qkv, do_sb, st_sb, n_pair)
    dq_fx, dk_fx, dv_fx, dcr, dcq = _fox_bwd(qkv, do_fx, o_fx, st_fx, c, c_rows, n_pair)
    dc = _pad_lanes((dcr.reshape(n_fox, S) + dcq[:, :, 0]).T, LANES)
    pieces = [dq_sb, dk_sb, dv_sb, dq_fx, dk_fx, dv_fx]
    dx, df_b, sm0 = _proj_bwd(dxa, pieces, dc, u, w_qkv.T, w_f.T, n_fox)

    gw_in = jnp.concatenate([_matmul_tn(x2, pc, "grad_w_in_%d" % a) for a, pc in enumerate(pieces)]
                            + [_matmul_tn(x2, df_b, "grad_w_f")[:, :n_fox]], axis=1)
    gw_out = _matmul_tn(on_b, dmix_b, "grad_w_out")
    gw_gu = _matmul_tn(h1, dgu_b, "grad_w_gate_up")
    gw_dn = _matmul_tn(act_b, dyp, "grad_w_down")

    small = _pack_small(D, n_fox, sm1[2:3], sm1[0:1], sm1[1:2], sm2[0:1], sm2[1:2], sm0[0:1, :n_fox],
                        sm2[2:3] * (1.0 / D))
    chunked = [gw_in.reshape(D, N_DEV, -1).transpose(1, 0, 2), gw_out.reshape(N_DEV, D // N_DEV, D),
               gw_gu.reshape(D, N_DEV, -1).transpose(1, 0, 2), gw_dn.reshape(N_DEV, F // N_DEV, D),
               jnp.broadcast_to(small[None], (N_DEV,) + small.shape)]
    l_in, l_out, l_gu, l_dn, l_small = _grad_exchange(chunked)

    r_in = _reduce_adam(l_in, w_in[0], m_w_in[0], v_w_in[0], "reduce_adam_w_in")
    r_out = _reduce_adam(l_out, w_out[0], m_w_out[0], v_w_out[0], "reduce_adam_w_out")
    r_gu = _reduce_adam(l_gu, w_gate_up[0], m_w_gate_up[0], v_w_gate_up[0], "reduce_adam_w_gate_up")
    r_dn = _reduce_adam(l_dn, w_down[0], m_w_down[0], v_w_down[0], "reduce_adam_w_down")
    zero = jnp.zeros((1, D), F32)
    pack = lambda gc, a, b_, c_, d_, bf: _pack_small(D, n_fox, gc, a, b_, c_, d_, bf, zero)
    r_small = _reduce_adam_small(
        l_small,
        pack(g_cat, ln1_g, ln1_b, ln2_g, ln2_b, b_f),
        pack(jnp.concatenate([m_g_sb, m_g_fox], axis=1), m_ln1_g, m_ln1_b, m_ln2_g, m_ln2_b, m_b_f),
        pack(jnp.concatenate([v_g_sb, v_g_fox], axis=1), v_ln1_g, v_ln1_b, v_ln2_g, v_ln2_b, v_b_f))
    loss = r_small[4][0, 0]

    def unpack(kind):
        big = [r_in[kind][None], None, None, None, r_out[kind][None], None, None, None, None,
               r_gu[kind][None], r_dn[kind][None]]
        s = r_small[kind]
        big[1] = s[5:6, :n_fox]
        big[2] = s[0:1, :W]
        big[3] = s[0:1, W:]
        big[5], big[6], big[7], big[8] = s[1:2], s[2:3], s[3:4], s[4:5]
        return big

    return (loss, dx[None], *unpack(0), *unpack(1), *unpack(2), *unpack(3))
```

```python
import functools

import jax
import jax.numpy as jnp
from jax import lax
from jax.experimental import pallas as pl
from jax.experimental.pallas import tpu as pltpu

F32 = jnp.float32
BF16 = jnp.bfloat16
I32 = jnp.int32

N_DEV = 8
HEAD_DIM = 64
LANES = 128
SCALE = HEAD_DIM ** -0.5
ALPHA = 2.0 ** 0.25
LN_EPS = 1e-5
RMS_EPS = 1e-6
ADAM_LR, ADAM_B1, ADAM_B2, ADAM_EPS, ADAM_WD, ADAM_STEP = 0.001, 0.9, 0.999, 1e-08, 0.01, 10
NEG_BIG = -1e30
NORM_SLACK = 1.01
EXP_ZERO = 106.0
VMEM_LIMIT = 60 * 1024 * 1024
ROW_TILE = 256
ATT_BLOCK = 256
MESH = pl.DeviceIdType.MESH


def _cparams(sem):
    return pltpu.CompilerParams(dimension_semantics=sem, vmem_limit_bytes=VMEM_LIMIT)


def _dot(a, b):
    return jnp.dot(a, b, preferred_element_type=F32)


def _dot_nt(a, b):
    return lax.dot_general(a, b, (((1,), (1,)), ((), ())), preferred_element_type=F32)


def _dot_tn(a, b):
    return lax.dot_general(a, b, (((0,), (0,)), ((), ())), preferred_element_type=F32)


def _split2(a):
    hi = a.astype(BF16)
    lo = (a - hi.astype(F32)).astype(BF16)
    return hi, lo


def _split3(a):
    hi = a.astype(BF16)
    r1 = a - hi.astype(F32)
    mid = r1.astype(BF16)
    lo = (r1 - mid.astype(F32)).astype(BF16)
    return hi, mid, lo


def _dot_acc(a, m):
    hi, lo = _split2(a)
    return _dot(hi, m) + _dot(lo, m)


def _tri(n, fn):
    r = lax.broadcasted_iota(I32, (n, n), 0)
    c = lax.broadcasted_iota(I32, (n, n), 1)
    return fn(r, c).astype(BF16)


def _full(shape):
    nd = len(shape)
    return pl.BlockSpec(shape, lambda *_: (0,) * nd)


def _peer(k):
    x, y, c = lax.axis_index("x"), lax.axis_index("y"), lax.axis_index("c")
    return (1 - x if k & 4 else x, 1 - y if k & 2 else y, 1 - c if k & 1 else c)


def _my_index():
    return 4 * lax.axis_index("x") + 2 * lax.axis_index("y") + lax.axis_index("c")


def _all_gather(shards):
    n = len(shards)

    def body(*refs):
        ins, outs = refs[:n], refs[n:2 * n]
        send_sems, recv_sems, local_sems = refs[2 * n:]
        me = _my_index()
        local = [pltpu.make_async_copy(ins[a], outs[a].at[me], local_sems.at[a]) for a in range(n)]
        for cp in local:
            cp.start()
        sends = []
        for k in range(1, N_DEV):
            for a in range(n):
                cp = pltpu.make_async_remote_copy(
                    src_ref=ins[a], dst_ref=outs[a].at[me],
                    send_sem=send_sems.at[(k - 1) * n + a], recv_sem=recv_sems.at[(k - 1) * n + a],
                    device_id=_peer(k), device_id_type=MESH)
                cp.start()
                sends.append(cp)
        for k in range(1, N_DEV):
            src = me ^ k
            for a in range(n):
                pltpu.make_async_remote_copy(
                    src_ref=ins[a], dst_ref=outs[a].at[src],
                    send_sem=send_sems.at[(k - 1) * n + a], recv_sem=recv_sems.at[(k - 1) * n + a],
                    device_id=_peer(k), device_id_type=MESH).wait_recv()
        for cp in sends:
            cp.wait_send()
        for cp in local:
            cp.wait()

    any_spec = pl.BlockSpec(memory_space=pl.ANY)
    return pl.pallas_call(
        body, name="weights_all_gather",
        out_shape=[jax.ShapeDtypeStruct((N_DEV,) + s.shape, s.dtype) for s in shards],
        in_specs=[any_spec] * n, out_specs=[any_spec] * n,
        scratch_shapes=[pltpu.SemaphoreType.DMA(((N_DEV - 1) * n,)),
                        pltpu.SemaphoreType.DMA(((N_DEV - 1) * n,)),
                        pltpu.SemaphoreType.DMA((n,))],
    )(*shards)


def _grad_exchange(chunked):
    n = len(chunked)

    def body(*refs):
        ins, outs = refs[:n], refs[n:2 * n]
        send_sems, recv_sems, local_sems = refs[2 * n:]
        me = _my_index()
        local = [pltpu.make_async_copy(ins[a].at[me], outs[a].at[me], local_sems.at[a]) for a in range(n)]
        for cp in local:
            cp.start()
        sends = []
        for k in range(1, N_DEV):
            dst = me ^ k
            for a in range(n):
                cp = pltpu.make_async_remote_copy(
                    src_ref=ins[a].at[dst], dst_ref=outs[a].at[me],
                    send_sem=send_sems.at[(k - 1) * n + a], recv_sem=recv_sems.at[(k - 1) * n + a],
                    device_id=_peer(k), device_id_type=MESH)
                cp.start()
                sends.append(cp)
        for k in range(1, N_DEV):
            src = me ^ k
            for a in range(n):
                pltpu.make_async_remote_copy(
                    src_ref=ins[a].at[src], dst_ref=outs[a].at[src],
                    send_sem=send_sems.at[(k - 1) * n + a], recv_sem=recv_sems.at[(k - 1) * n + a],
                    device_id=_peer(k), device_id_type=MESH).wait_recv()
        for cp in sends:
            cp.wait_send()
        for cp in local:
            cp.wait()

    any_spec = pl.BlockSpec(memory_space=pl.ANY)
    return pl.pallas_call(
        body, name="grad_exchange",
        out_shape=[jax.ShapeDtypeStruct(s.shape, s.dtype) for s in chunked],
        in_specs=[any_spec] * n, out_specs=[any_spec] * n,
        scratch_shapes=[pltpu.SemaphoreType.DMA(((N_DEV - 1) * n,)),
                        pltpu.SemaphoreType.DMA(((N_DEV - 1) * n,)),
                        pltpu.SemaphoreType.DMA((n,))],
    )(*chunked)


def _log_sigmoid(u):
    return jnp.minimum(u, 0.0) - jnp.log1p(jnp.exp(-jnp.abs(u)))


def _proj_fwd(x, w_qkv, w_f, bf_pad, n_fox):
    S, D = x.shape
    N = w_qkv.shape[1]
    W = D // 2
    TM = ROW_TILE
    tri = _tri(TM, lambda r, c: c <= r)
    r_ = lax.broadcasted_iota(I32, (W, LANES), 0)
    c_ = lax.broadcasted_iota(I32, (W, LANES), 1)
    head_of = (r_ // HEAD_DIM == c_).astype(BF16)

    def body(x_ref, wq_ref, wf_ref, bf_ref, tri_ref, ho_ref, qkv_ref, u_ref, c_ref, ksq_ref, run_ref):
        @pl.when(pl.program_id(0) == 0)
        def _():
            run_ref[...] = jnp.zeros_like(run_ref)
            ksq_ref[...] = jnp.zeros_like(ksq_ref)

        xb = x_ref[...].astype(BF16)
        for n0 in range(0, N, D):
            chunk = _dot(xb, wq_ref[:, n0:n0 + D]).astype(BF16)
            qkv_ref[:, n0:n0 + D] = chunk
            if n0 == 4 * W:
                kf = chunk[:, :W].astype(F32)
                ksq = jnp.max(_dot_acc(kf * kf, ho_ref[...]), axis=0, keepdims=True)
                ksq_ref[...] = jnp.maximum(ksq_ref[...], ksq)
        u = _dot(xb, wf_ref[...]) + bf_ref[...]
        lane = lax.broadcasted_iota(I32, u.shape, 1)
        logf = jnp.where(lane < n_fox, _log_sigmoid(u), 0.0)
        u_ref[...] = u
        hi, mid, lo = _split3(logf)
        t = tri_ref[...]
        cs = _dot(t, hi) + _dot(t, mid) + _dot(t, lo) + run_ref[...]
        c_ref[...] = cs
        run_ref[...] = cs[TM - 1:TM, :]

    return pl.pallas_call(
        body, name="proj_fwd", grid=(S // TM,),
        in_specs=[pl.BlockSpec((TM, D), lambda i: (i, 0)), _full(w_qkv.shape), _full(w_f.shape),
                  _full(bf_pad.shape), _full(tri.shape), _full(head_of.shape)],
        out_specs=[pl.BlockSpec((TM, N), lambda i: (i, 0)), pl.BlockSpec((TM, LANES), lambda i: (i, 0)),
                   pl.BlockSpec((TM, LANES), lambda i: (i, 0)), _full((8, LANES))],
        out_shape=[jax.ShapeDtypeStruct((S, N), BF16), jax.ShapeDtypeStruct((S, LANES), F32),
                   jax.ShapeDtypeStruct((S, LANES), F32), jax.ShapeDtypeStruct((8, LANES), F32)],
        scratch_shapes=[pltpu.VMEM((1, LANES), F32)],
        compiler_params=_cparams(("arbitrary",)),
    )(x, w_qkv, w_f, bf_pad, tri, head_of)


def _post_attn_fwd(o_sb, o_fx, x, g_cat, gmat, w_out, ln_g, ln_b):
    S, D = x.shape
    H = D // 2
    TM = ROW_TILE

    def body(osb_ref, ofx_ref, x_ref, g_ref, gm_ref, wo_ref, lg_ref, lb_ref,
             h1_ref, xh_ref, rs_ref, on_ref, rr_ref):
        o = jnp.concatenate([osb_ref[...], ofx_ref[...]], axis=1)
        ms = _dot_acc(o * o, gm_ref[...]) * (1.0 / HEAD_DIM)
        r = lax.rsqrt(ms + RMS_EPS)
        onb = (o * r * g_ref[...]).astype(BF16)
        hp = ALPHA * x_ref[...] + _dot(onb, wo_ref[...])
        mu = jnp.mean(hp, axis=-1, keepdims=True)
        d = hp - mu
        rstd = lax.rsqrt(jnp.mean(d * d, axis=-1, keepdims=True) + LN_EPS)
        xh = d * rstd
        h1_ref[...] = xh * lg_ref[...] + lb_ref[...]
        xh_ref[...] = xh
        rs_ref[...] = jnp.broadcast_to(rstd, (TM, LANES))
        on_ref[...] = onb
        rr_ref[...] = r

    row = lambda w: pl.BlockSpec((TM, w), lambda i: (i, 0))
    return pl.pallas_call(
        body, name="post_attn_fwd", grid=(S // TM,),
        in_specs=[row(H), row(H), row(D), _full((1, D)), _full((D, D)), _full((D, D)), _full((1, D)), _full((1, D))],
        out_specs=[row(D), row(D), row(LANES), row(D), row(D)],
        out_shape=[jax.ShapeDtypeStruct((S, D), F32), jax.ShapeDtypeStruct((S, D), F32),
                   jax.ShapeDtypeStruct((S, LANES), F32), jax.ShapeDtypeStruct((S, D), BF16),
                   jax.ShapeDtypeStruct((S, D), F32)],
        compiler_params=_cparams(("arbitrary",)),
    )(o_sb, o_fx, x, g_cat, gmat, w_out, ln_g, ln_b)


def _ln_bwd(dxh, xh, rstd):
    m1 = jnp.mean(dxh, axis=-1, keepdims=True)
    m2 = jnp.mean(dxh * xh, axis=-1, keepdims=True)
    return rstd * (dxh - m1 - xh * m2)


def _mlp_fwd(h1, target, w_gu, w_dn, ln_g, ln_b):
    S, D = h1.shape
    F = w_dn.shape[0]
    TM = ROW_TILE
    FC = F // 2

    def body(h1_ref, tg_ref, wgu_hbm, wdn_hbm, lg_ref, lb_ref, gu_ref, act_ref, dyp_ref, sm_ref, wgu, wdn):
        @pl.when(pl.program_id(0) == 0)
        def _():
            pltpu.sync_copy(wgu_hbm, wgu)
            pltpu.sync_copy(wdn_hbm, wdn)
            sm_ref[...] = jnp.zeros_like(sm_ref)

        h1v = h1_ref[...]
        hb = h1v.astype(BF16)
        ff = jnp.zeros((TM, D), F32)
        for c0 in range(0, F, FC):
            g = _dot(hb, wgu[:, c0:c0 + FC])
            u = _dot(hb, wgu[:, F + c0:F + c0 + FC])
            gu_ref[:, c0:c0 + FC] = g
            gu_ref[:, F + c0:F + c0 + FC] = u
            ab = ((g * jax.nn.sigmoid(g)) * u).astype(BF16)
            act_ref[:, c0:c0 + FC] = ab
            ff = ff + _dot(ab, wdn[c0:c0 + FC, :])
        yp = ALPHA * h1v + ff
        mu = jnp.mean(yp, axis=-1, keepdims=True)
        d = yp - mu
        rstd = lax.rsqrt(jnp.mean(d * d, axis=-1, keepdims=True) + LN_EPS)
        xh = d * rstd
        err = (xh * lg_ref[...] + lb_ref[...]) - tg_ref[...]
        dy = err * (1.0 / D)
        sm_ref[0:1, :] += jnp.sum(dy * xh, axis=0, keepdims=True)
        sm_ref[1:2, :] += jnp.sum(dy, axis=0, keepdims=True)
        sm_ref[2:3, :] += jnp.sum(err * err, axis=0, keepdims=True)
        dyp_ref[...] = _ln_bwd(dy * lg_ref[...], xh, rstd)

    row = lambda w: pl.BlockSpec((TM, w), lambda i: (i, 0))
    hbm = pl.BlockSpec(memory_space=pl.ANY)
    return pl.pallas_call(
        body, name="mlp_fwd", grid=(S // TM,),
        in_specs=[row(D), row(D), hbm, hbm, _full((1, D)), _full((1, D))],
        out_specs=[row(2 * F), row(F), row(D), _full((8, D))],
        out_shape=[jax.ShapeDtypeStruct((S, 2 * F), F32), jax.ShapeDtypeStruct((S, F), BF16),
                   jax.ShapeDtypeStruct((S, D), F32), jax.ShapeDtypeStruct((8, D), F32)],
        scratch_shapes=[pltpu.VMEM(w_gu.shape, BF16), pltpu.VMEM(w_dn.shape, BF16)],
        compiler_params=_cparams(("arbitrary",)),
    )(h1, target, w_gu, w_dn, ln_g, ln_b)


def _mlp_bwd(gu, dyp, w_guT, w_dnT):
    S, D = dyp.shape
    F = w_dnT.shape[1]
    TM = ROW_TILE
    FC = F // 2

    def body(gu_ref, dyp_ref, wguT_hbm, wdnT_hbm, dgu_ref, dh1_ref, wguT, wdnT):
        @pl.when(pl.program_id(0) == 0)
        def _():
            pltpu.sync_copy(wguT_hbm, wguT)
            pltpu.sync_copy(wdnT_hbm, wdnT)

        dypv = dyp_ref[...]
        db = dypv.astype(BF16)
        dh1 = ALPHA * dypv
        for c0 in range(0, F, FC):
            dact = _dot(db, wdnT[:, c0:c0 + FC])
            g = gu_ref[:, c0:c0 + FC]
            u = gu_ref[:, F + c0:F + c0 + FC]
            sg = jax.nn.sigmoid(g)
            dgb = (dact * u * (sg * (1.0 + g * (1.0 - sg)))).astype(BF16)
            dub = (dact * (g * sg)).astype(BF16)
            dgu_ref[:, c0:c0 + FC] = dgb
            dgu_ref[:, F + c0:F + c0 + FC] = dub
            dh1 = dh1 + _dot(dgb, wguT[c0:c0 + FC, :]) + _dot(dub, wguT[F + c0:F + c0 + FC, :])
        dh1_ref[...] = dh1

    row = lambda w: pl.BlockSpec((TM, w), lambda i: (i, 0))
    hbm = pl.BlockSpec(memory_space=pl.ANY)
    return pl.pallas_call(
        body, name="mlp_bwd", grid=(S // TM,),
        in_specs=[row(2 * F), row(D), hbm, hbm],
        out_specs=[row(2 * F), row(D)],
        out_shape=[jax.ShapeDtypeStruct((S, 2 * F), BF16), jax.ShapeDtypeStruct((S, D), F32)],
        scratch_shapes=[pltpu.VMEM(w_guT.shape, BF16), pltpu.VMEM(w_dnT.shape, BF16)],
        compiler_params=_cparams(("arbitrary",)),
    )(gu, dyp, w_guT, w_dnT)


def _post_attn_bwd(dh1, xh, rs, ln_g, o_sb, o_fx, rr, g_cat, gmat, w_outT):
    S, D = dh1.shape
    H = D // 2
    TM = ROW_TILE

    def body(dh1_ref, xh_ref, rs_ref, lg_ref, osb_ref, ofx_ref, rr_ref, g_ref, gm_ref, woT_ref,
             dxa_ref, dmix_ref, dosb_ref, dofx_ref, sm_ref):
        @pl.when(pl.program_id(0) == 0)
        def _():
            sm_ref[...] = jnp.zeros_like(sm_ref)

        dh = dh1_ref[...]
        xhv = xh_ref[...]
        dhp = _ln_bwd(dh * lg_ref[...], xhv, rs_ref[:, 0:1])
        dxa_ref[...] = ALPHA * dhp
        dmb = dhp.astype(BF16)
        dmix_ref[...] = dmb
        don = _dot(dmb, woT_ref[...])
        o = jnp.concatenate([osb_ref[...], ofx_ref[...]], axis=1)
        r = rr_ref[...]
        u = don * g_ref[...]
        t = _dot_acc(u * o, gm_ref[...]) * (1.0 / HEAD_DIM)
        do = r * u - o * (r * r * r) * t
        dosb_ref[...] = do[:, :H]
        dofx_ref[...] = do[:, H:]
        sm_ref[0:1, :] += jnp.sum(dh * xhv, axis=0, keepdims=True)
        sm_ref[1:2, :] += jnp.sum(dh, axis=0, keepdims=True)
        sm_ref[2:3, :] += jnp.sum(don * o * r, axis=0, keepdims=True)

    row = lambda w: pl.BlockSpec((TM, w), lambda i: (i, 0))
    return pl.pallas_call(
        body, name="post_attn_bwd", grid=(S // TM,),
        in_specs=[row(D), row(D), row(LANES), _full((1, D)), row(H), row(H), row(D), _full((1, D)),
                  _full((D, D)), _full((D, D))],
        out_specs=[row(D), row(D), row(H), row(H), _full((8, D))],
        out_shape=[jax.ShapeDtypeStruct((S, D), F32), jax.ShapeDtypeStruct((S, D), BF16),
                   jax.ShapeDtypeStruct((S, H), F32), jax.ShapeDtypeStruct((S, H), F32),
                   jax.ShapeDtypeStruct((8, D), F32)],
        compiler_params=_cparams(("arbitrary",)),
    )(dh1, xh, rs, ln_g, o_sb, o_fx, rr, g_cat, gmat, w_outT)


def _proj_bwd(dxa, pieces, dc, u, w_qkvT, w_fT, n_fox):
    S, D = dxa.shape
    H = D // 2
    TM = ROW_TILE
    nT = S // TM
    tri = _tri(TM, lambda r, c: c >= r)
    n_p = len(pieces)

    def body(*refs):
        dxa_ref = refs[0]
        p_refs = refs[1:1 + n_p]
        dc_ref, u_ref, wq_ref, wf_ref, tri_ref, dx_ref, df_ref, sm_ref, run_ref = refs[1 + n_p:]

        @pl.when(pl.program_id(0) == 0)
        def _():
            run_ref[...] = jnp.zeros_like(run_ref)
            sm_ref[...] = jnp.zeros_like(sm_ref)

        hi, mid, lo = _split3(dc_ref[...])
        t = tri_ref[...]
        dlogf = _dot(t, hi) + _dot(t, mid) + _dot(t, lo) + run_ref[...]
        run_ref[...] = dlogf[0:1, :]
        uv = u_ref[...]
        lane = lax.broadcasted_iota(I32, uv.shape, 1)
        df = jnp.where(lane < n_fox, dlogf * jax.nn.sigmoid(-uv), 0.0)
        sm_ref[0:1, :] += jnp.sum(df, axis=0, keepdims=True)
        dfb = df.astype(BF16)
        df_ref[...] = dfb
        acc = dxa_ref[...] + _dot(dfb, wf_ref[...])
        for a in range(n_p):
            acc = acc + _dot(p_refs[a][...].astype(BF16), wq_ref[a * H:(a + 1) * H, :])
        dx_ref[...] = acc

    rev = lambda w: pl.BlockSpec((TM, w), lambda i: (nT - 1 - i, 0))
    return pl.pallas_call(
        body, name="proj_bwd", grid=(nT,),
        in_specs=[rev(D)] + [rev(H)] * n_p + [rev(LANES), rev(LANES), _full(w_qkvT.shape), _full(w_fT.shape),
                                               _full(tri.shape)],
        out_specs=[rev(D), rev(LANES), _full((8, LANES))],
        out_shape=[jax.ShapeDtypeStruct((S, D), F32), jax.ShapeDtypeStruct((S, LANES), BF16),
                   jax.ShapeDtypeStruct((8, LANES), F32)],
        scratch_shapes=[pltpu.VMEM((1, LANES), F32)],
        compiler_params=_cparams(("arbitrary",)),
    )(dxa, *pieces, dc, u, w_qkvT, w_fT, tri)


def _matmul_tn(a, b, name):
    S, M = a.shape
    N = b.shape[1]
    TK = 512 if S % 512 == 0 else ROW_TILE
    TN = next(t for t in (512, 384, 256, 128) if N % t == 0)
    TMo = next(t for t in (1024, 512, 256, 128) if M % t == 0)

    def body(a_ref, b_ref, o_ref):
        @pl.when(pl.program_id(2) == 0)
        def _():
            o_ref[...] = jnp.zeros_like(o_ref)

        o_ref[...] += _dot_tn(a_ref[...].astype(BF16), b_ref[...].astype(BF16))

    return pl.pallas_call(
        body, name=name, grid=(M // TMo, N // TN, S // TK),
        in_specs=[pl.BlockSpec((TK, TMo), lambda m, n, k: (k, m)), pl.BlockSpec((TK, TN), lambda m, n, k: (k, n))],
        out_specs=pl.BlockSpec((TMo, TN), lambda m, n, k: (m, n)),
        out_shape=jax.ShapeDtypeStruct((M, N), F32),
        compiler_params=_cparams(("arbitrary", "arbitrary", "arbitrary")),
    )(a, b)


def _half_mask(half):
    lane = lax.broadcasted_iota(I32, (1, LANES), 1)
    return (lane >= half * HEAD_DIM) & (lane < half * HEAD_DIM + HEAD_DIM)


def _softplus_neg_abs(z):
    return jnp.log1p(jnp.exp(-jnp.abs(z)))


def _sb_fwd(qkv, n_pair):
    S = qkv.shape[0]
    B = ATT_BLOCK
    nq = S // B
    us = _tri(B, lambda r, c: r > c)

    def body(q_ref, k_ref, v_ref, us_ref, o_ref, st_ref, js_ref, acc_ref, r_ref):
        p, i, half = pl.program_id(0), pl.program_id(1), pl.program_id(2)
        hm = _half_mask(half)
        qv = q_ref[...]
        qs = jnp.where(hm, qv, jnp.zeros_like(qv)) * SCALE
        row = lax.broadcasted_iota(I32, (B, B), 0)
        col = lax.broadcasted_iota(I32, (B, B), 1)
        tri = col < row
        acc_ref[...] = jnp.zeros_like(acc_ref)
        r_ref[...] = jnp.zeros_like(r_ref)

        def block(j, diag):
            off = pl.multiple_of(j * B, B)
            kj = k_ref[pl.ds(off, B), :]
            vj = v_ref[pl.ds(off, B), :]
            z = _dot_nt(qs, kj)
            sp = _softplus_neg_abs(z)
            b = jnp.minimum(-z, 0.0) - sp
            a = jnp.minimum(z, 0.0) - sp
            if diag:
                b = jnp.where(tri, b, 0.0)
            lexc = _dot_acc(b, us_ref[...])
            w = jnp.exp(a + (r_ref[...] + lexc))
            if diag:
                w = jnp.where(tri, w, 0.0)
            acc_ref[...] += _dot(w.astype(BF16), vj)
            r_ref[...] += lexc[:, 0:1] + b[:, 0:1]

        def live():
            return (jnp.max(r_ref[...]) > -EXP_ZERO).astype(I32)

        block(i, True)

        def step(carry):
            j, _ = carry
            block(j, False)
            return j - 1, live()

        j_end, _ = lax.while_loop(lambda c: (c[0] >= 0) & (c[1] > 0), step, (i - 1, live()))
        js_ref[2 * p + half, i] = j_end + 1
        res = jnp.where(hm, acc_ref[...], 0.0)

        @pl.when(half == 0)
        def _():
            o_ref[...] = res

        @pl.when(half == 1)
        def _():
            o_ref[...] += res

        st_ref[0] = jnp.broadcast_to(r_ref[...], (B, LANES))

    return pl.pallas_call(
        body, name="sb_attn_fwd", grid=(n_pair, nq, 2),
        in_specs=[pl.BlockSpec((B, LANES), lambda p, i, h: (i, p)),
                  pl.BlockSpec((S, LANES), lambda p, i, h: (0, n_pair + p)),
                  pl.BlockSpec((S, LANES), lambda p, i, h: (0, 2 * n_pair + p)),
                  _full((B, B))],
        out_specs=[pl.BlockSpec((B, LANES), lambda p, i, h: (i, p)),
                   pl.BlockSpec((1, B, LANES), lambda p, i, h: (2 * p + h, i, 0)),
                   pl.BlockSpec(memory_space=pltpu.SMEM)],
        out_shape=[jax.ShapeDtypeStruct((S, n_pair * LANES), F32),
                   jax.ShapeDtypeStruct((2 * n_pair, S, LANES), F32),
                   jax.ShapeDtypeStruct((2 * n_pair, nq), I32)],
        scratch_shapes=[pltpu.VMEM((B, LANES), F32), pltpu.VMEM((B, 1), F32)],
        compiler_params=_cparams(("arbitrary", "arbitrary", "arbitrary")),
    )(qkv, qkv, qkv, us)


def _sb_bwd(qkv, do, st, js, n_pair):
    S = qkv.shape[0]
    B = ATT_BLOCK
    nq = S // B
    us = _tri(B, lambda r, c: r > c)
    ti = _tri(B, lambda r, c: r <= c)

    def body(js_ref, q_ref, k_ref, v_ref, do_ref, st_ref, us_ref, ti_ref, dq_ref, dk_ref, dv_ref,
             dqa_ref, pr_ref, er_ref):
        p, i, half = pl.program_id(0), pl.program_id(1), pl.program_id(2)

        @pl.when((i == 0) & (half == 0))
        def _():
            dk_ref[...] = jnp.zeros_like(dk_ref)
            dv_ref[...] = jnp.zeros_like(dv_ref)

        hm = _half_mask(half)
        qv = q_ref[...]
        qs = jnp.where(hm, qv, jnp.zeros_like(qv)) * SCALE
        dob = jnp.where(hm, do_ref[...], 0.0).astype(BF16)
        rproc = st_ref[0][:, 0:1]
        row = lax.broadcasted_iota(I32, (B, B), 0)
        col = lax.broadcasted_iota(I32, (B, B), 1)
        tri = col < row
        dqa_ref[...] = jnp.zeros_like(dqa_ref)
        pr_ref[...] = jnp.zeros_like(pr_ref)
        er_ref[...] = jnp.zeros_like(er_ref)

        def block(j, diag):
            off = pl.multiple_of(j * B, B)
            kj = k_ref[pl.ds(off, B), :]
            vj = v_ref[pl.ds(off, B), :]
            z = _dot_nt(qs, kj)
            sp = _softplus_neg_abs(z)
            b = jnp.minimum(-z, 0.0) - sp
            a = jnp.minimum(z, 0.0) - sp
            if diag:
                b = jnp.where(tri, b, 0.0)
            lexc = _dot_acc(b, us_ref[...])
            pr_new = pr_ref[...] + (lexc[:, 0:1] + b[:, 0:1])
            pr_ref[...] = pr_new
            w = jnp.exp(a + ((rproc - pr_new) + lexc))
            if diag:
                w = jnp.where(tri, w, 0.0)
            e = _dot_nt(dob, vj) * w
            einc = _dot_acc(e, ti_ref[...])
            big_e = er_ref[...] + (einc - e)
            er_ref[...] += einc[:, B - 1:B]
            eb = jnp.exp(b)
            dzb = (e * eb - big_e * (1.0 - eb)).astype(BF16)
            dqa_ref[...] += _dot(dzb, kj)
            dk_ref[pl.ds(off, B), :] += _dot_tn(dzb, qs)
            dv_ref[pl.ds(off, B), :] += _dot_tn(w.astype(BF16), dob)

        def step(j, carry):
            block(j, False)
            return carry

        lax.fori_loop(js_ref[2 * p + half, i], i, step, 0)
        block(i, True)
        res = jnp.where(hm, dqa_ref[...] * SCALE, 0.0)

        @pl.when(half == 0)
        def _():
            dq_ref[...] = res

        @pl.when(half == 1)
        def _():
            dq_ref[...] += res

    W = n_pair * LANES
    return pl.pallas_call(
        body, name="sb_attn_bwd",
        grid_spec=pltpu.PrefetchScalarGridSpec(
            num_scalar_prefetch=1, grid=(n_pair, nq, 2),
            in_specs=[pl.BlockSpec((B, LANES), lambda p, i, h, js: (i, p)),
                      pl.BlockSpec((S, LANES), lambda p, i, h, js: (0, n_pair + p)),
                      pl.BlockSpec((S, LANES), lambda p, i, h, js: (0, 2 * n_pair + p)),
                      pl.BlockSpec((B, LANES), lambda p, i, h, js: (i, p)),
                      pl.BlockSpec((1, B, LANES), lambda p, i, h, js: (2 * p + h, i, 0)),
                      pl.BlockSpec((B, B), lambda p, i, h, js: (0, 0)),
                      pl.BlockSpec((B, B), lambda p, i, h, js: (0, 0))],
            out_specs=[pl.BlockSpec((B, LANES), lambda p, i, h, js: (i, p)),
                       pl.BlockSpec((S, LANES), lambda p, i, h, js: (0, p)),
                       pl.BlockSpec((S, LANES), lambda p, i, h, js: (0, p))],
            scratch_shapes=[pltpu.VMEM((B, LANES), F32), pltpu.VMEM((B, 1), F32), pltpu.VMEM((B, 1), F32)]),
        out_shape=[jax.ShapeDtypeStruct((S, W), F32)] * 3,
        compiler_params=_cparams(("arbitrary", "arbitrary", "arbitrary")),
    )(js, qkv, qkv, qkv, do, st, us, ti)


def _head_column(blk, head):
    lane = lax.broadcasted_iota(I32, (1, LANES), 1)
    return jnp.sum(jnp.where(lane == head, blk, 0.0), axis=1, keepdims=True)


def _fox_fwd(qkv, c, c_rows, kmax, n_pair):
    S = qkv.shape[0]
    B = ATT_BLOCK
    nq = S // B

    def body(q_ref, k_ref, v_ref, c_ref, cr_ref, km_ref, o_ref, st_ref, js_ref, acc_ref, m_ref, l_ref):
        p, i, half = pl.program_id(0), pl.program_id(1), pl.program_id(2)
        hm = _half_mask(half)
        qv = q_ref[...]
        qs = jnp.where(hm, qv, jnp.zeros_like(qv)) * SCALE
        ccol = _head_column(c_ref[...], 2 * p + half)
        qf = qs.astype(F32)
        qk_bound = (jnp.sqrt(jnp.sum(qf * qf, axis=1, keepdims=True)) * NORM_SLACK
                    * _head_column(km_ref[...], 2 * p + half))
        row = lax.broadcasted_iota(I32, (B, B), 0)
        col = lax.broadcasted_iota(I32, (B, B), 1)
        acc_ref[...] = jnp.zeros_like(acc_ref)
        l_ref[...] = jnp.zeros_like(l_ref)
        m_ref[...] = jnp.full_like(m_ref, NEG_BIG)

        def block(j, diag):
            off = pl.multiple_of(j * B, B)
            kj = k_ref[pl.ds(off, B), :]
            vj = v_ref[pl.ds(off, B), :]
            s = _dot_nt(qs, kj) + (ccol - cr_ref[0, pl.ds(j, 1), :])
            if diag:
                s = jnp.where(col <= row, s, NEG_BIG)
            m_old = m_ref[...]
            m_new = jnp.maximum(m_old, jnp.max(s, axis=1, keepdims=True))
            alpha = jnp.exp(m_old - m_new)
            pv = jnp.exp(s - m_new)
            l_ref[...] = alpha * l_ref[...] + jnp.sum(pv, axis=1, keepdims=True)
            acc_ref[...] = alpha * acc_ref[...] + _dot(pv.astype(BF16), vj)
            m_ref[...] = m_new

        def live(j):
            c_end = cr_ref[0, pl.ds(jnp.maximum(j, 0), 1), :][:, B - 1:B]
            bound = qk_bound + (ccol - c_end) - m_ref[...]
            return (jnp.max(bound) > -EXP_ZERO).astype(I32)

        block(i, True)

        def step(carry):
            j, _ = carry
            block(j, False)
            return j - 1, live(j - 1)

        j_end, _ = lax.while_loop(lambda cr: (cr[0] >= 0) & (cr[1] > 0), step, (i - 1, live(i - 1)))
        js_ref[2 * p + half, i] = j_end + 1
        res = jnp.where(hm, acc_ref[...] / l_ref[...], 0.0)

        @pl.when(half == 0)
        def _():
            o_ref[...] = res

        @pl.when(half == 1)
        def _():
            o_ref[...] += res

        st_ref[0] = jnp.broadcast_to(m_ref[...] + jnp.log(l_ref[...]), (B, LANES))

    return pl.pallas_call(
        body, name="fox_attn_fwd", grid=(n_pair, nq, 2),
        in_specs=[pl.BlockSpec((B, LANES), lambda p, i, h: (i, 3 * n_pair + p)),
                  pl.BlockSpec((S, LANES), lambda p, i, h: (0, 4 * n_pair + p)),
                  pl.BlockSpec((S, LANES), lambda p, i, h: (0, 5 * n_pair + p)),
                  pl.BlockSpec((B, LANES), lambda p, i, h: (i, 0)),
                  pl.BlockSpec((1, nq, B), lambda p, i, h: (2 * p + h, 0, 0)),
                  _full((1, LANES))],
        out_specs=[pl.BlockSpec((B, LANES), lambda p, i, h: (i, p)),
                   pl.BlockSpec((1, B, LANES), lambda p, i, h: (2 * p + h, i, 0)),
                   pl.BlockSpec(memory_space=pltpu.SMEM)],
        out_shape=[jax.ShapeDtypeStruct((S, n_pair * LANES), F32),
                   jax.ShapeDtypeStruct((2 * n_pair, S, LANES), F32),
                   jax.ShapeDtypeStruct((2 * n_pair, nq), I32)],
        scratch_shapes=[pltpu.VMEM((B, LANES), F32), pltpu.VMEM((B, 1), F32), pltpu.VMEM((B, 1), F32)],
        compiler_params=_cparams(("arbitrary", "arbitrary", "arbitrary")),
    )(qkv, qkv, qkv, c, c_rows, kmax)


def _fox_bwd(qkv, do, o, st, c, c_rows, js, n_pair):
    S = qkv.shape[0]
    B = ATT_BLOCK
    nq = S // B

    def body(js_ref, q_ref, k_ref, v_ref, do_ref, o_ref, st_ref, c_ref, cr_ref, dq_ref, dk_ref, dv_ref,
             dc_ref, dcq_ref, dqa_ref, rs_ref):
        p, i, half = pl.program_id(0), pl.program_id(1), pl.program_id(2)

        @pl.when((i == 0) & (half == 0))
        def _():
            dk_ref[...] = jnp.zeros_like(dk_ref)
            dv_ref[...] = jnp.zeros_like(dv_ref)
            dc_ref[...] = jnp.zeros_like(dc_ref)

        hm = _half_mask(half)
        qv = q_ref[...]
        qs = jnp.where(hm, qv, jnp.zeros_like(qv)) * SCALE
        dov = jnp.where(hm, do_ref[...], 0.0)
        dob = dov.astype(BF16)
        delta = jnp.sum(dov * o_ref[...], axis=1, keepdims=True)
        lse = st_ref[0][:, 0:1]
        ccol = _head_column(c_ref[...], 2 * p + half) - lse
        row = lax.broadcasted_iota(I32, (B, B), 0)
        col = lax.broadcasted_iota(I32, (B, B), 1)
        dqa_ref[...] = jnp.zeros_like(dqa_ref)
        rs_ref[...] = jnp.zeros_like(rs_ref)

        def block(j, diag):
            off = pl.multiple_of(j * B, B)
            kj = k_ref[pl.ds(off, B), :]
            vj = v_ref[pl.ds(off, B), :]
            pv = jnp.exp(_dot_nt(qs, kj) + (ccol - cr_ref[0, pl.ds(j, 1), :]))
            if diag:
                pv = jnp.where(col <= row, pv, 0.0)
            ds = pv * (_dot_nt(dob, vj) - delta)
            dsb = ds.astype(BF16)
            dqa_ref[...] += _dot(dsb, kj)
            dk_ref[pl.ds(off, B), :] += _dot_tn(dsb, qs)
            dv_ref[pl.ds(off, B), :] += _dot_tn(pv.astype(BF16), dob)
            dc_ref[0, half, pl.ds(j, 1), :] -= jnp.sum(ds, axis=0, keepdims=True)
            rs_ref[...] += jnp.sum(ds, axis=1, keepdims=True)

        def step(j, carry):
            block(j, False)
            return carry

        lax.fori_loop(js_ref[2 * p + half, i], i, step, 0)
        block(i, True)
        res = jnp.where(hm, dqa_ref[...] * SCALE, 0.0)

        @pl.when(half == 0)
        def _():
            dq_ref[...] = res

        @pl.when(half == 1)
        def _():
            dq_ref[...] += res

        dcq_ref[0] = jnp.broadcast_to(rs_ref[...], (B, LANES))

    W = n_pair * LANES
    return pl.pallas_call(
        body, name="fox_attn_bwd",
        grid_spec=pltpu.PrefetchScalarGridSpec(
            num_scalar_prefetch=1, grid=(n_pair, nq, 2),
            in_specs=[pl.BlockSpec((B, LANES), lambda p, i, h, js: (i, 3 * n_pair + p)),
                      pl.BlockSpec((S, LANES), lambda p, i, h, js: (0, 4 * n_pair + p)),
                      pl.BlockSpec((S, LANES), lambda p, i, h, js: (0, 5 * n_pair + p)),
                      pl.BlockSpec((B, LANES), lambda p, i, h, js: (i, p)),
                      pl.BlockSpec((B, LANES), lambda p, i, h, js: (i, p)),
                      pl.BlockSpec((1, B, LANES), lambda p, i, h, js: (2 * p + h, i, 0)),
                      pl.BlockSpec((B, LANES), lambda p, i, h, js: (i, 0)),
                      pl.BlockSpec((1, nq, B), lambda p, i, h, js: (2 * p + h, 0, 0))],
            out_specs=[pl.BlockSpec((B, LANES), lambda p, i, h, js: (i, p)),
                       pl.BlockSpec((S, LANES), lambda p, i, h, js: (0, p)),
                       pl.BlockSpec((S, LANES), lambda p, i, h, js: (0, p)),
                       pl.BlockSpec((1, 2, nq, B), lambda p, i, h, js: (p, 0, 0, 0)),
                       pl.BlockSpec((1, B, LANES), lambda p, i, h, js: (2 * p + h, i, 0))],
            scratch_shapes=[pltpu.VMEM((B, LANES), F32), pltpu.VMEM((B, 1), F32)]),
        out_shape=[jax.ShapeDtypeStruct((S, W), F32)] * 3 + [jax.ShapeDtypeStruct((n_pair, 2, nq, B), F32),
                                                            jax.ShapeDtypeStruct((2 * n_pair, S, LANES), F32)],
        compiler_params=_cparams(("arbitrary", "arbitrary", "arbitrary")),
    )(js, qkv, qkv, qkv, do, o, st, c, c_rows)


def _adam(w, g, m, v):
    m = ADAM_B1 * m + (1.0 - ADAM_B1) * g
    v = ADAM_B2 * v + (1.0 - ADAM_B2) * (g * g)
    m_hat = m / (1.0 - ADAM_B1 ** ADAM_STEP)
    v_hat = v / (1.0 - ADAM_B2 ** ADAM_STEP)
    delta = -ADAM_LR * (m_hat / (jnp.sqrt(v_hat) + ADAM_EPS) + ADAM_WD * w)
    return delta, m, v


def _reduce_adam(landing, w, m, v, name):
    R, C = w.shape
    TR = next(t for t in (256, 128, 88, 64, 8) if R % t == 0)

    def body(l_ref, w_ref, m_ref, v_ref, g_ref, d_ref, nm_ref, nv_ref):
        g = l_ref[0]
        for s in range(1, N_DEV):
            g = g + l_ref[s]
        d, nm, nv = _adam(w_ref[...], g, m_ref[...], v_ref[...])
        g_ref[...] = g
        d_ref[...] = d
        nm_ref[...] = nm
        nv_ref[...] = nv

    blk = pl.BlockSpec((TR, C), lambda i: (i, 0))
    return pl.pallas_call(
        body, name=name, grid=(R // TR,),
        in_specs=[pl.BlockSpec((N_DEV, TR, C), lambda i: (0, i, 0)), blk, blk, blk],
        out_specs=[blk] * 4,
        out_shape=[jax.ShapeDtypeStruct((R, C), F32)] * 4,
        compiler_params=_cparams(("arbitrary",)),
    )(landing, w, m, v)


def _reduce_adam_small(landing, w, m, v):
    R, C = w.shape

    def body(l_ref, w_ref, m_ref, v_ref, g_ref, d_ref, nm_ref, nv_ref, loss_ref):
        g = l_ref[0]
        for s in range(1, N_DEV):
            g = g + l_ref[s]
        d, nm, nv = _adam(w_ref[...], g, m_ref[...], v_ref[...])
        g_ref[...] = g
        d_ref[...] = d
        nm_ref[...] = nm
        nv_ref[...] = nv
        loss_ref[...] = jnp.broadcast_to(0.5 * jnp.sum(g[7:8, :], axis=1, keepdims=True), (1, LANES))

    return pl.pallas_call(
        body, name="reduce_adam_small",
        out_shape=[jax.ShapeDtypeStruct((R, C), F32)] * 4 + [jax.ShapeDtypeStruct((1, LANES), F32)],
    )(landing, w, m, v)


def _pad_lanes(a, width):
    return jnp.pad(a, ((0, 0), (0, width - a.shape[1])))


def _pack_small(D, n_fox, g_cat, l1g, l1b, l2g, l2b, bf, last):
    return jnp.concatenate([g_cat, l1g, l1b, l2g, l2b, _pad_lanes(bf, D), jnp.zeros((1, D), F32), last], axis=0)


def kernel(x, w_in, b_f, g_sb, g_fox, w_out, ln1_g, ln1_b, ln2_g, ln2_b, w_gate_up, w_down, loss_target, m_w_in, m_b_f, m_g_sb, m_g_fox, m_w_out, m_ln1_g, m_ln1_b, m_ln2_g, m_ln2_b, m_w_gate_up, m_w_down, v_w_in, v_b_f, v_g_sb, v_g_fox, v_w_out, v_ln1_g, v_ln1_b, v_ln2_g, v_ln2_b, v_w_gate_up, v_w_down):
    x2, tgt = x[0], loss_target[0]
    S, D = x2.shape
    W = D // 2
    n_pair = W // LANES
    n_fox = W // HEAD_DIM
    F = w_down.shape[1] * N_DEV
    B = ATT_BLOCK
    nq = S // B

    g_in, g_out, g_gu, g_dn = _all_gather([w_in[0].astype(BF16), w_out[0].astype(BF16),
                                           w_gate_up[0].astype(BF16), w_down[0].astype(BF16)])
    w_in_full = g_in.transpose(1, 0, 2).reshape(D, -1)
    w_qkv = w_in_full[:, :6 * W]
    w_f = _pad_lanes(w_in_full[:, 6 * W:], LANES)
    w_o = g_out.reshape(D, D)
    w_gu = g_gu.transpose(1, 0, 2).reshape(D, 2 * F)
    w_dn = g_dn.reshape(F, D)
    gmat = _tri(D, lambda r, c: (r // HEAD_DIM) == (c // HEAD_DIM))
    g_cat = jnp.concatenate([g_sb, g_fox], axis=1)

    qkv, u, c, ksq = _proj_fwd(x2, w_qkv, w_f, _pad_lanes(b_f, LANES), n_fox)
    c_rows = c[:, :n_fox].T.reshape(n_fox, nq, B)
    kmax = jnp.sqrt(ksq[0:1]) * NORM_SLACK
    o_sb, st_sb, js_sb = _sb_fwd(qkv, n_pair)
    o_fx, st_fx, js_fx = _fox_fwd(qkv, c, c_rows, kmax, n_pair)
    h1, xh1, rs1, on_b, rr = _post_attn_fwd(o_sb, o_fx, x2, g_cat, gmat, w_o, ln1_g, ln1_b)
    gu, act_b, dyp, sm2 = _mlp_fwd(h1, tgt, w_gu, w_dn, ln2_g, ln2_b)

    dgu_b, dh1 = _mlp_bwd(gu, dyp, w_gu.T, w_dn.T)
    dxa, dmix_b, do_sb, do_fx, sm1 = _post_attn_bwd(dh1, xh1, rs1, ln1_g, o_sb, o_fx, rr, g_cat, gmat, w_o.T)
    dq_sb, dk_sb, dv_sb = _sb_bwd(qkv, do_sb, st_sb, js_sb, n_pair)
    dq_fx, dk_fx, dv_fx, dcr, dcq = _fox_bwd(qkv, do_fx, o_fx, st_fx, c, c_rows, js_fx, n_pair)
    dc = _pad_lanes((dcr.reshape(n_fox, S) + dcq[:, :, 0]).T, LANES)
    pieces = [dq_sb, dk_sb, dv_sb, dq_fx, dk_fx, dv_fx]
    dx, df_b, sm0 = _proj_bwd(dxa, pieces, dc, u, w_qkv.T, w_f.T, n_fox)

    gw_in = jnp.concatenate([_matmul_tn(x2, pc, "grad_w_in_%d" % a) for a, pc in enumerate(pieces)]
                            + [_matmul_tn(x2, df_b, "grad_w_f")[:, :n_fox]], axis=1)
    gw_out = _matmul_tn(on_b, dmix_b, "grad_w_out")
    gw_gu = _matmul_tn(h1, dgu_b, "grad_w_gate_up")
    gw_dn = _matmul_tn(act_b, dyp, "grad_w_down")

    small = _pack_small(D, n_fox, sm1[2:3], sm1[0:1], sm1[1:2], sm2[0:1], sm2[1:2], sm0[0:1, :n_fox],
                        sm2[2:3] * (1.0 / D))
    chunked = [gw_in.reshape(D, N_DEV, -1).transpose(1, 0, 2), gw_out.reshape(N_DEV, D // N_DEV, D),
               gw_gu.reshape(D, N_DEV, -1).transpose(1, 0, 2), gw_dn.reshape(N_DEV, F // N_DEV, D),
               jnp.broadcast_to(small[None], (N_DEV,) + small.shape)]
    l_in, l_out, l_gu, l_dn, l_small = _grad_exchange(chunked)

    r_in = _reduce_adam(l_in, w_in[0], m_w_in[0], v_w_in[0], "reduce_adam_w_in")
    r_out = _reduce_adam(l_out, w_out[0], m_w_out[0], v_w_out[0], "reduce_adam_w_out")
    r_gu = _reduce_adam(l_gu, w_gate_up[0], m_w_gate_up[0], v_w_gate_up[0], "reduce_adam_w_gate_up")
    r_dn = _reduce_adam(l_dn, w_down[0], m_w_down[0], v_w_down[0], "reduce_adam_w_down")
    zero = jnp.zeros((1, D), F32)
    pack = lambda gc, a, b_, c_, d_, bf: _pack_small(D, n_fox, gc, a, b_, c_, d_, bf, zero)
    r_small = _reduce_adam_small(
        l_small,
        pack(g_cat, ln1_g, ln1_b, ln2_g, ln2_b, b_f),
        pack(jnp.concatenate([m_g_sb, m_g_fox], axis=1), m_ln1_g, m_ln1_b, m_ln2_g, m_ln2_b, m_b_f),
        pack(jnp.concatenate([v_g_sb, v_g_fox], axis=1), v_ln1_g, v_ln1_b, v_ln2_g, v_ln2_b, v_b_f))
    loss = r_small[4][0, 0]

    def unpack(kind):
        big = [r_in[kind][None], None, None, None, r_out[kind][None], None, None, None, None,
               r_gu[kind][None], r_dn[kind][None]]
        s = r_small[kind]
        big[1] = s[5:6, :n_fox]
        big[2] = s[0:1, :W]
        big[3] = s[0:1, W:]
        big[5], big[6], big[7], big[8] = s[1:2], s[2:3], s[3:4], s[4:5]
        return big

    return (loss, dx[None], *unpack(0), *unpack(1), *unpack(2), *unpack(3))
```

```python
import functools

import jax
import jax.numpy as jnp
from jax import lax
from jax.experimental import pallas as pl
from jax.experimental.pallas import tpu as pltpu

F32 = jnp.float32
BF16 = jnp.bfloat16
I32 = jnp.int32

N_DEV = 8
HEAD_DIM = 64
LANES = 128
SCALE = HEAD_DIM ** -0.5
ALPHA = 2.0 ** 0.25
LN_EPS = 1e-5
RMS_EPS = 1e-6
ADAM_LR, ADAM_B1, ADAM_B2, ADAM_EPS, ADAM_WD, ADAM_STEP = 0.001, 0.9, 0.999, 1e-08, 0.01, 10
NEG_BIG = -1e30
NORM_SLACK = 1.01
EXP_ZERO = 106.0
VMEM_LIMIT = 60 * 1024 * 1024
ROW_TILE = 256
SB_BLOCK = 256
FOX_BLOCK = 512
MESH = pl.DeviceIdType.MESH


def _cparams(sem):
    return pltpu.CompilerParams(dimension_semantics=sem, vmem_limit_bytes=VMEM_LIMIT)


def _dot(a, b):
    return jnp.dot(a, b, preferred_element_type=F32)


def _dot_nt(a, b):
    return lax.dot_general(a, b, (((1,), (1,)), ((), ())), preferred_element_type=F32)


def _dot_tn(a, b):
    return lax.dot_general(a, b, (((0,), (0,)), ((), ())), preferred_element_type=F32)


def _split2(a):
    hi = a.astype(BF16)
    lo = (a - hi.astype(F32)).astype(BF16)
    return hi, lo


def _split3(a):
    hi = a.astype(BF16)
    r1 = a - hi.astype(F32)
    mid = r1.astype(BF16)
    lo = (r1 - mid.astype(F32)).astype(BF16)
    return hi, mid, lo


def _dot_acc(a, m):
    hi, lo = _split2(a)
    return _dot(hi, m) + _dot(lo, m)


def _tri(n, fn):
    r = lax.broadcasted_iota(I32, (n, n), 0)
    c = lax.broadcasted_iota(I32, (n, n), 1)
    return fn(r, c).astype(BF16)


def _full(shape):
    nd = len(shape)
    return pl.BlockSpec(shape, lambda *_: (0,) * nd)


def _peer(k):
    x, y, c = lax.axis_index("x"), lax.axis_index("y"), lax.axis_index("c")
    return (1 - x if k & 4 else x, 1 - y if k & 2 else y, 1 - c if k & 1 else c)


def _my_index():
    return 4 * lax.axis_index("x") + 2 * lax.axis_index("y") + lax.axis_index("c")


def _all_gather(shards):
    n = len(shards)

    def body(*refs):
        ins, outs = refs[:n], refs[n:2 * n]
        send_sems, recv_sems, local_sems = refs[2 * n:]
        me = _my_index()
        local = [pltpu.make_async_copy(ins[a], outs[a].at[me], local_sems.at[a]) for a in range(n)]
        for cp in local:
            cp.start()
        sends = []
        for k in range(1, N_DEV):
            for a in range(n):
                cp = pltpu.make_async_remote_copy(
                    src_ref=ins[a], dst_ref=outs[a].at[me],
                    send_sem=send_sems.at[(k - 1) * n + a], recv_sem=recv_sems.at[(k - 1) * n + a],
                    device_id=_peer(k), device_id_type=MESH)
                cp.start()
                sends.append(cp)
        for k in range(1, N_DEV):
            src = me ^ k
            for a in range(n):
                pltpu.make_async_remote_copy(
                    src_ref=ins[a], dst_ref=outs[a].at[src],
                    send_sem=send_sems.at[(k - 1) * n + a], recv_sem=recv_sems.at[(k - 1) * n + a],
                    device_id=_peer(k), device_id_type=MESH).wait_recv()
        for cp in sends:
            cp.wait_send()
        for cp in local:
            cp.wait()

    any_spec = pl.BlockSpec(memory_space=pl.ANY)
    return pl.pallas_call(
        body, name="weights_all_gather",
        out_shape=[jax.ShapeDtypeStruct((N_DEV,) + s.shape, s.dtype) for s in shards],
        in_specs=[any_spec] * n, out_specs=[any_spec] * n,
        scratch_shapes=[pltpu.SemaphoreType.DMA(((N_DEV - 1) * n,)),
                        pltpu.SemaphoreType.DMA(((N_DEV - 1) * n,)),
                        pltpu.SemaphoreType.DMA((n,))],
    )(*shards)


def _grad_exchange(chunked):
    n = len(chunked)

    def body(*refs):
        ins, outs = refs[:n], refs[n:2 * n]
        send_sems, recv_sems, local_sems = refs[2 * n:]
        me = _my_index()
        local = [pltpu.make_async_copy(ins[a].at[me], outs[a].at[me], local_sems.at[a]) for a in range(n)]
        for cp in local:
            cp.start()
        sends = []
        for k in range(1, N_DEV):
            dst = me ^ k
            for a in range(n):
                cp = pltpu.make_async_remote_copy(
                    src_ref=ins[a].at[dst], dst_ref=outs[a].at[me],
                    send_sem=send_sems.at[(k - 1) * n + a], recv_sem=recv_sems.at[(k - 1) * n + a],
                    device_id=_peer(k), device_id_type=MESH)
                cp.start()
                sends.append(cp)
        for k in range(1, N_DEV):
            src = me ^ k
            for a in range(n):
                pltpu.make_async_remote_copy(
                    src_ref=ins[a].at[src], dst_ref=outs[a].at[src],
                    send_sem=send_sems.at[(k - 1) * n + a], recv_sem=recv_sems.at[(k - 1) * n + a],
                    device_id=_peer(k), device_id_type=MESH).wait_recv()
        for cp in sends:
            cp.wait_send()
        for cp in local:
            cp.wait()

    any_spec = pl.BlockSpec(memory_space=pl.ANY)
    return pl.pallas_call(
        body, name="grad_exchange",
        out_shape=[jax.ShapeDtypeStruct(s.shape, s.dtype) for s in chunked],
        in_specs=[any_spec] * n, out_specs=[any_spec] * n,
        scratch_shapes=[pltpu.SemaphoreType.DMA(((N_DEV - 1) * n,)),
                        pltpu.SemaphoreType.DMA(((N_DEV - 1) * n,)),
                        pltpu.SemaphoreType.DMA((n,))],
    )(*chunked)


def _log_sigmoid(u):
    return jnp.minimum(u, 0.0) - jnp.log1p(jnp.exp(-jnp.abs(u)))


def _proj_fwd(x, w_qkv, w_f, bf_pad, n_fox):
    S, D = x.shape
    N = w_qkv.shape[1]
    W = D // 2
    TM = ROW_TILE
    tri = _tri(TM, lambda r, c: c <= r)
    r_ = lax.broadcasted_iota(I32, (W, LANES), 0)
    c_ = lax.broadcasted_iota(I32, (W, LANES), 1)
    head_of = (r_ // HEAD_DIM == c_).astype(BF16)

    def body(x_ref, wq_ref, wf_ref, bf_ref, tri_ref, ho_ref, qkv_ref, u_ref, c_ref, ksq_ref, run_ref):
        @pl.when(pl.program_id(0) == 0)
        def _():
            run_ref[...] = jnp.zeros_like(run_ref)
            ksq_ref[...] = jnp.zeros_like(ksq_ref)

        xb = x_ref[...].astype(BF16)
        for n0 in range(0, N, D):
            chunk = _dot(xb, wq_ref[:, n0:n0 + D]).astype(BF16)
            qkv_ref[:, n0:n0 + D] = chunk
            if n0 == 4 * W:
                kf = chunk[:, :W].astype(F32)
                ksq = jnp.max(_dot_acc(kf * kf, ho_ref[...]), axis=0, keepdims=True)
                ksq_ref[...] = jnp.maximum(ksq_ref[...], ksq)
        u = _dot(xb, wf_ref[...]) + bf_ref[...]
        lane = lax.broadcasted_iota(I32, u.shape, 1)
        logf = jnp.where(lane < n_fox, _log_sigmoid(u), 0.0)
        u_ref[...] = u
        hi, mid, lo = _split3(logf)
        t = tri_ref[...]
        cs = _dot(t, hi) + _dot(t, mid) + _dot(t, lo) + run_ref[...]
        c_ref[...] = cs
        run_ref[...] = cs[TM - 1:TM, :]

    return pl.pallas_call(
        body, name="proj_fwd", grid=(S // TM,),
        in_specs=[pl.BlockSpec((TM, D), lambda i: (i, 0)), _full(w_qkv.shape), _full(w_f.shape),
                  _full(bf_pad.shape), _full(tri.shape), _full(head_of.shape)],
        out_specs=[pl.BlockSpec((TM, N), lambda i: (i, 0)), pl.BlockSpec((TM, LANES), lambda i: (i, 0)),
                   pl.BlockSpec((TM, LANES), lambda i: (i, 0)), _full((8, LANES))],
        out_shape=[jax.ShapeDtypeStruct((S, N), BF16), jax.ShapeDtypeStruct((S, LANES), F32),
                   jax.ShapeDtypeStruct((S, LANES), F32), jax.ShapeDtypeStruct((8, LANES), F32)],
        scratch_shapes=[pltpu.VMEM((1, LANES), F32)],
        compiler_params=_cparams(("arbitrary",)),
    )(x, w_qkv, w_f, bf_pad, tri, head_of)


def _post_attn_fwd(o_sb, o_fx, x, g_cat, gmat, w_out, ln_g, ln_b):
    S, D = x.shape
    H = D // 2
    TM = ROW_TILE

    def body(osb_ref, ofx_ref, x_ref, g_ref, gm_ref, wo_ref, lg_ref, lb_ref,
             h1_ref, xh_ref, rs_ref, on_ref, rr_ref):
        o = jnp.concatenate([osb_ref[...], ofx_ref[...]], axis=1)
        ms = _dot_acc(o * o, gm_ref[...]) * (1.0 / HEAD_DIM)
        r = lax.rsqrt(ms + RMS_EPS)
        onb = (o * r * g_ref[...]).astype(BF16)
        hp = ALPHA * x_ref[...] + _dot(onb, wo_ref[...])
        mu = jnp.mean(hp, axis=-1, keepdims=True)
        d = hp - mu
        rstd = lax.rsqrt(jnp.mean(d * d, axis=-1, keepdims=True) + LN_EPS)
        xh = d * rstd
        h1_ref[...] = xh * lg_ref[...] + lb_ref[...]
        xh_ref[...] = xh
        rs_ref[...] = jnp.broadcast_to(rstd, (TM, LANES))
        on_ref[...] = onb
        rr_ref[...] = r

    row = lambda w: pl.BlockSpec((TM, w), lambda i: (i, 0))
    return pl.pallas_call(
        body, name="post_attn_fwd", grid=(S // TM,),
        in_specs=[row(H), row(H), row(D), _full((1, D)), _full((D, D)), _full((D, D)), _full((1, D)), _full((1, D))],
        out_specs=[row(D), row(D), row(LANES), row(D), row(D)],
        out_shape=[jax.ShapeDtypeStruct((S, D), F32), jax.ShapeDtypeStruct((S, D), F32),
                   jax.ShapeDtypeStruct((S, LANES), F32), jax.ShapeDtypeStruct((S, D), BF16),
                   jax.ShapeDtypeStruct((S, D), F32)],
        compiler_params=_cparams(("arbitrary",)),
    )(o_sb, o_fx, x, g_cat, gmat, w_out, ln_g, ln_b)


def _ln_bwd(dxh, xh, rstd):
    m1 = jnp.mean(dxh, axis=-1, keepdims=True)
    m2 = jnp.mean(dxh * xh, axis=-1, keepdims=True)
    return rstd * (dxh - m1 - xh * m2)


def _mlp_fwd(h1, target, w_gu, w_dn, ln_g, ln_b):
    S, D = h1.shape
    F = w_dn.shape[0]
    TM = ROW_TILE
    FC = F // 2

    def body(h1_ref, tg_ref, wgu_hbm, wdn_hbm, lg_ref, lb_ref, gu_ref, act_ref, dyp_ref, sm_ref, wgu, wdn):
        @pl.when(pl.program_id(0) == 0)
        def _():
            pltpu.sync_copy(wgu_hbm, wgu)
            pltpu.sync_copy(wdn_hbm, wdn)
            sm_ref[...] = jnp.zeros_like(sm_ref)

        h1v = h1_ref[...]
        hb = h1v.astype(BF16)
        ff = jnp.zeros((TM, D), F32)
        for c0 in range(0, F, FC):
            g = _dot(hb, wgu[:, c0:c0 + FC])
            u = _dot(hb, wgu[:, F + c0:F + c0 + FC])
            gu_ref[:, c0:c0 + FC] = g
            gu_ref[:, F + c0:F + c0 + FC] = u
            ab = ((g * jax.nn.sigmoid(g)) * u).astype(BF16)
            act_ref[:, c0:c0 + FC] = ab
            ff = ff + _dot(ab, wdn[c0:c0 + FC, :])
        yp = ALPHA * h1v + ff
        mu = jnp.mean(yp, axis=-1, keepdims=True)
        d = yp - mu
        rstd = lax.rsqrt(jnp.mean(d * d, axis=-1, keepdims=True) + LN_EPS)
        xh = d * rstd
        err = (xh * lg_ref[...] + lb_ref[...]) - tg_ref[...]
        dy = err * (1.0 / D)
        sm_ref[0:1, :] += jnp.sum(dy * xh, axis=0, keepdims=True)
        sm_ref[1:2, :] += jnp.sum(dy, axis=0, keepdims=True)
        sm_ref[2:3, :] += jnp.sum(err * err, axis=0, keepdims=True)
        dyp_ref[...] = _ln_bwd(dy * lg_ref[...], xh, rstd)

    row = lambda w: pl.BlockSpec((TM, w), lambda i: (i, 0))
    hbm = pl.BlockSpec(memory_space=pl.ANY)
    return pl.pallas_call(
        body, name="mlp_fwd", grid=(S // TM,),
        in_specs=[row(D), row(D), hbm, hbm, _full((1, D)), _full((1, D))],
        out_specs=[row(2 * F), row(F), row(D), _full((8, D))],
        out_shape=[jax.ShapeDtypeStruct((S, 2 * F), F32), jax.ShapeDtypeStruct((S, F), BF16),
                   jax.ShapeDtypeStruct((S, D), F32), jax.ShapeDtypeStruct((8, D), F32)],
        scratch_shapes=[pltpu.VMEM(w_gu.shape, BF16), pltpu.VMEM(w_dn.shape, BF16)],
        compiler_params=_cparams(("arbitrary",)),
    )(h1, target, w_gu, w_dn, ln_g, ln_b)


def _mlp_bwd(gu, dyp, w_guT, w_dnT):
    S, D = dyp.shape
    F = w_dnT.shape[1]
    TM = ROW_TILE
    FC = F // 2

    def body(gu_ref, dyp_ref, wguT_hbm, wdnT_hbm, dgu_ref, dh1_ref, wguT, wdnT):
        @pl.when(pl.program_id(0) == 0)
        def _():
            pltpu.sync_copy(wguT_hbm, wguT)
            pltpu.sync_copy(wdnT_hbm, wdnT)

        dypv = dyp_ref[...]
        db = dypv.astype(BF16)
        dh1 = ALPHA * dypv
        for c0 in range(0, F, FC):
            dact = _dot(db, wdnT[:, c0:c0 + FC])
            g = gu_ref[:, c0:c0 + FC]
            u = gu_ref[:, F + c0:F + c0 + FC]
            sg = jax.nn.sigmoid(g)
            dgb = (dact * u * (sg * (1.0 + g * (1.0 - sg)))).astype(BF16)
            dub = (dact * (g * sg)).astype(BF16)
            dgu_ref[:, c0:c0 + FC] = dgb
            dgu_ref[:, F + c0:F + c0 + FC] = dub
            dh1 = dh1 + _dot(dgb, wguT[c0:c0 + FC, :]) + _dot(dub, wguT[F + c0:F + c0 + FC, :])
        dh1_ref[...] = dh1

    row = lambda w: pl.BlockSpec((TM, w), lambda i: (i, 0))
    hbm = pl.BlockSpec(memory_space=pl.ANY)
    return pl.pallas_call(
        body, name="mlp_bwd", grid=(S // TM,),
        in_specs=[row(2 * F), row(D), hbm, hbm],
        out_specs=[row(2 * F), row(D)],
        out_shape=[jax.ShapeDtypeStruct((S, 2 * F), BF16), jax.ShapeDtypeStruct((S, D), F32)],
        scratch_shapes=[pltpu.VMEM(w_guT.shape, BF16), pltpu.VMEM(w_dnT.shape, BF16)],
        compiler_params=_cparams(("arbitrary",)),
    )(gu, dyp, w_guT, w_dnT)


def _post_attn_bwd(dh1, xh, rs, ln_g, o_sb, o_fx, rr, g_cat, gmat, w_outT):
    S, D = dh1.shape
    H = D // 2
    TM = ROW_TILE

    def body(dh1_ref, xh_ref, rs_ref, lg_ref, osb_ref, ofx_ref, rr_ref, g_ref, gm_ref, woT_ref,
             dxa_ref, dmix_ref, dosb_ref, dofx_ref, sm_ref):
        @pl.when(pl.program_id(0) == 0)
        def _():
            sm_ref[...] = jnp.zeros_like(sm_ref)

        dh = dh1_ref[...]
        xhv = xh_ref[...]
        dhp = _ln_bwd(dh * lg_ref[...], xhv, rs_ref[:, 0:1])
        dxa_ref[...] = ALPHA * dhp
        dmb = dhp.astype(BF16)
        dmix_ref[...] = dmb
        don = _dot(dmb, woT_ref[...])
        o = jnp.concatenate([osb_ref[...], ofx_ref[...]], axis=1)
        r = rr_ref[...]
        u = don * g_ref[...]
        t = _dot_acc(u * o, gm_ref[...]) * (1.0 / HEAD_DIM)
        do = r * u - o * (r * r * r) * t
        dosb_ref[...] = do[:, :H]
        dofx_ref[...] = do[:, H:]
        sm_ref[0:1, :] += jnp.sum(dh * xhv, axis=0, keepdims=True)
        sm_ref[1:2, :] += jnp.sum(dh, axis=0, keepdims=True)
        sm_ref[2:3, :] += jnp.sum(don * o * r, axis=0, keepdims=True)

    row = lambda w: pl.BlockSpec((TM, w), lambda i: (i, 0))
    return pl.pallas_call(
        body, name="post_attn_bwd", grid=(S // TM,),
        in_specs=[row(D), row(D), row(LANES), _full((1, D)), row(H), row(H), row(D), _full((1, D)),
                  _full((D, D)), _full((D, D))],
        out_specs=[row(D), row(D), row(H), row(H), _full((8, D))],
        out_shape=[jax.ShapeDtypeStruct((S, D), F32), jax.ShapeDtypeStruct((S, D), BF16),
                   jax.ShapeDtypeStruct((S, H), F32), jax.ShapeDtypeStruct((S, H), F32),
                   jax.ShapeDtypeStruct((8, D), F32)],
        compiler_params=_cparams(("arbitrary",)),
    )(dh1, xh, rs, ln_g, o_sb, o_fx, rr, g_cat, gmat, w_outT)


def _proj_bwd(dxa, pieces, dc, u, w_qkvT, w_fT, n_fox):
    S, D = dxa.shape
    H = D // 2
    TM = ROW_TILE
    nT = S // TM
    tri = _tri(TM, lambda r, c: c >= r)
    n_p = len(pieces)

    def body(*refs):
        dxa_ref = refs[0]
        p_refs = refs[1:1 + n_p]
        dc_ref, u_ref, wq_ref, wf_ref, tri_ref, dx_ref, df_ref, sm_ref, run_ref = refs[1 + n_p:]

        @pl.when(pl.program_id(0) == 0)
        def _():
            run_ref[...] = jnp.zeros_like(run_ref)
            sm_ref[...] = jnp.zeros_like(sm_ref)

        hi, mid, lo = _split3(dc_ref[...])
        t = tri_ref[...]
        dlogf = _dot(t, hi) + _dot(t, mid) + _dot(t, lo) + run_ref[...]
        run_ref[...] = dlogf[0:1, :]
        uv = u_ref[...]
        lane = lax.broadcasted_iota(I32, uv.shape, 1)
        df = jnp.where(lane < n_fox, dlogf * jax.nn.sigmoid(-uv), 0.0)
        sm_ref[0:1, :] += jnp.sum(df, axis=0, keepdims=True)
        dfb = df.astype(BF16)
        df_ref[...] = dfb
        acc = dxa_ref[...] + _dot(dfb, wf_ref[...])
        for a in range(n_p):
            acc = acc + _dot(p_refs[a][...].astype(BF16), wq_ref[a * H:(a + 1) * H, :])
        dx_ref[...] = acc

    rev = lambda w: pl.BlockSpec((TM, w), lambda i: (nT - 1 - i, 0))
    return pl.pallas_call(
        body, name="proj_bwd", grid=(nT,),
        in_specs=[rev(D)] + [rev(H)] * n_p + [rev(LANES), rev(LANES), _full(w_qkvT.shape), _full(w_fT.shape),
                                               _full(tri.shape)],
        out_specs=[rev(D), rev(LANES), _full((8, LANES))],
        out_shape=[jax.ShapeDtypeStruct((S, D), F32), jax.ShapeDtypeStruct((S, LANES), BF16),
                   jax.ShapeDtypeStruct((8, LANES), F32)],
        scratch_shapes=[pltpu.VMEM((1, LANES), F32)],
        compiler_params=_cparams(("arbitrary",)),
    )(dxa, *pieces, dc, u, w_qkvT, w_fT, tri)


def _matmul_tn(a, bs, name, n_split=1):
    S, M = a.shape
    widths = [b.shape[1] for b in bs]
    N = sum(widths)
    assert n_split == 1 or len(bs) == 1
    TK = 512 if S % 512 == 0 else ROW_TILE
    MC = 512 if M % 512 == 0 else 256
    nb = len(bs)

    def body(*refs):
        a_ref, b_refs, o_ref = refs[0], refs[1:1 + nb], refs[1 + nb]

        @pl.when(pl.program_id(1) == 0)
        def _():
            o_ref[...] = jnp.zeros_like(o_ref)

        n0 = 0
        for b_ref in b_refs:
            bv = b_ref[...].astype(BF16)
            w = bv.shape[1]
            for m0 in range(0, M, MC):
                o_ref[m0:m0 + MC, n0:n0 + w] += _dot_tn(a_ref[:, m0:m0 + MC].astype(BF16), bv)
            n0 += w

    return pl.pallas_call(
        body, name=name, grid=(n_split, S // TK),
        in_specs=[pl.BlockSpec((TK, M), lambda n, k: (k, 0))]
        + [pl.BlockSpec((TK, w // n_split), lambda n, k: (k, n)) for w in widths],
        out_specs=pl.BlockSpec((M, N // n_split), lambda n, k: (0, n)),
        out_shape=jax.ShapeDtypeStruct((M, N), F32),
        compiler_params=_cparams(("arbitrary", "arbitrary")),
    )(a, *bs)


def _half_mask(half):
    lane = lax.broadcasted_iota(I32, (1, LANES), 1)
    return (lane >= half * HEAD_DIM) & (lane < half * HEAD_DIM + HEAD_DIM)


def _softplus_neg_abs(z):
    return jnp.log1p(jnp.exp(-jnp.abs(z)))


def _sb_fwd(qkv, n_pair):
    S = qkv.shape[0]
    B = SB_BLOCK
    nq = S // B
    us = _tri(B, lambda r, c: r > c)

    def body(q_ref, k_ref, v_ref, us_ref, o_ref, st_ref, js_ref, acc_ref, r_ref):
        p, i, half = pl.program_id(0), pl.program_id(1), pl.program_id(2)
        hm = _half_mask(half)
        qv = q_ref[...]
        qs = jnp.where(hm, qv, jnp.zeros_like(qv)) * SCALE
        row = lax.broadcasted_iota(I32, (B, B), 0)
        col = lax.broadcasted_iota(I32, (B, B), 1)
        tri = col < row
        acc_ref[...] = jnp.zeros_like(acc_ref)
        r_ref[...] = jnp.zeros_like(r_ref)

        def block(j, diag):
            off = pl.multiple_of(j * B, B)
            kj = k_ref[pl.ds(off, B), :]
            vj = v_ref[pl.ds(off, B), :]
            z = _dot_nt(qs, kj)
            sp = _softplus_neg_abs(z)
            b = jnp.minimum(-z, 0.0) - sp
            a = jnp.minimum(z, 0.0) - sp
            if diag:
                b = jnp.where(tri, b, 0.0)
            lexc = _dot_acc(b, us_ref[...])
            w = jnp.exp(a + (r_ref[...] + lexc))
            if diag:
                w = jnp.where(tri, w, 0.0)
            acc_ref[...] += _dot(w.astype(BF16), vj)
            r_ref[...] += lexc[:, 0:1] + b[:, 0:1]

        def live():
            return (jnp.max(r_ref[...]) > -EXP_ZERO).astype(I32)

        block(i, True)

        def step(carry):
            j, _ = carry
            block(j, False)
            return j - 1, live()

        j_end, _ = lax.while_loop(lambda c: (c[0] >= 0) & (c[1] > 0), step, (i - 1, live()))
        js_ref[2 * p + half, i] = j_end + 1
        res = jnp.where(hm, acc_ref[...], 0.0)

        @pl.when(half == 0)
        def _():
            o_ref[...] = res

        @pl.when(half == 1)
        def _():
            o_ref[...] += res

        st_ref[0] = jnp.broadcast_to(r_ref[...], (B, LANES))

    return pl.pallas_call(
        body, name="sb_attn_fwd", grid=(n_pair, nq, 2),
        in_specs=[pl.BlockSpec((B, LANES), lambda p, i, h: (i, p)),
                  pl.BlockSpec((S, LANES), lambda p, i, h: (0, n_pair + p)),
                  pl.BlockSpec((S, LANES), lambda p, i, h: (0, 2 * n_pair + p)),
                  _full((B, B))],
        out_specs=[pl.BlockSpec((B, LANES), lambda p, i, h: (i, p)),
                   pl.BlockSpec((1, B, LANES), lambda p, i, h: (2 * p + h, i, 0)),
                   pl.BlockSpec(memory_space=pltpu.SMEM)],
        out_shape=[jax.ShapeDtypeStruct((S, n_pair * LANES), F32),
                   jax.ShapeDtypeStruct((2 * n_pair, S, LANES), F32),
                   jax.ShapeDtypeStruct((2 * n_pair, nq), I32)],
        scratch_shapes=[pltpu.VMEM((B, LANES), F32), pltpu.VMEM((B, 1), F32)],
        compiler_params=_cparams(("arbitrary", "arbitrary", "arbitrary")),
    )(qkv, qkv, qkv, us)


def _sb_bwd(qkv, do, st, js, n_pair):
    S = qkv.shape[0]
    B = SB_BLOCK
    nq = S // B
    us = _tri(B, lambda r, c: r > c)
    ti = _tri(B, lambda r, c: r <= c)

    def body(js_ref, q_ref, k_ref, v_ref, do_ref, st_ref, us_ref, ti_ref, dq_ref, dk_ref, dv_ref,
             dqa_ref, pr_ref, er_ref):
        p, i, half = pl.program_id(0), pl.program_id(1), pl.program_id(2)

        @pl.when((i == 0) & (half == 0))
        def _():
            dk_ref[...] = jnp.zeros_like(dk_ref)
            dv_ref[...] = jnp.zeros_like(dv_ref)

        hm = _half_mask(half)
        qv = q_ref[...]
        qs = jnp.where(hm, qv, jnp.zeros_like(qv)) * SCALE
        dob = jnp.where(hm, do_ref[...], 0.0).astype(BF16)
        rproc = st_ref[0][:, 0:1]
        row = lax.broadcasted_iota(I32, (B, B), 0)
        col = lax.broadcasted_iota(I32, (B, B), 1)
        tri = col < row
        dqa_ref[...] = jnp.zeros_like(dqa_ref)
        pr_ref[...] = jnp.zeros_like(pr_ref)
        er_ref[...] = jnp.zeros_like(er_ref)

        def block(j, diag):
            off = pl.multiple_of(j * B, B)
            kj = k_ref[pl.ds(off, B), :]
            vj = v_ref[pl.ds(off, B), :]
            z = _dot_nt(qs, kj)
            sp = _softplus_neg_abs(z)
            b = jnp.minimum(-z, 0.0) - sp
            a = jnp.minimum(z, 0.0) - sp
            if diag:
                b = jnp.where(tri, b, 0.0)
            lexc = _dot_acc(b, us_ref[...])
            pr_new = pr_ref[...] + (lexc[:, 0:1] + b[:, 0:1])
            pr_ref[...] = pr_new
            w = jnp.exp(a + ((rproc - pr_new) + lexc))
            if diag:
                w = jnp.where(tri, w, 0.0)
            e = _dot_nt(dob, vj) * w
            einc = _dot_acc(e, ti_ref[...])
            big_e = er_ref[...] + (einc - e)
            er_ref[...] += einc[:, B - 1:B]
            eb = jnp.exp(b)
            dzb = (e * eb - big_e * (1.0 - eb)).astype(BF16)
            dqa_ref[...] += _dot(dzb, kj)
            dk_ref[pl.ds(off, B), :] += _dot_tn(dzb, qs)
            dv_ref[pl.ds(off, B), :] += _dot_tn(w.astype(BF16), dob)

        def step(j, carry):
            block(j, False)
            return carry

        lax.fori_loop(js_ref[2 * p + half, i], i, step, 0)
        block(i, True)
        res = jnp.where(hm, dqa_ref[...] * SCALE, 0.0)

        @pl.when(half == 0)
        def _():
            dq_ref[...] = res

        @pl.when(half == 1)
        def _():
            dq_ref[...] += res

    W = n_pair * LANES
    return pl.pallas_call(
        body, name="sb_attn_bwd",
        grid_spec=pltpu.PrefetchScalarGridSpec(
            num_scalar_prefetch=1, grid=(n_pair, nq, 2),
            in_specs=[pl.BlockSpec((B, LANES), lambda p, i, h, js: (i, p)),
                      pl.BlockSpec((S, LANES), lambda p, i, h, js: (0, n_pair + p)),
                      pl.BlockSpec((S, LANES), lambda p, i, h, js: (0, 2 * n_pair + p)),
                      pl.BlockSpec((B, LANES), lambda p, i, h, js: (i, p)),
                      pl.BlockSpec((1, B, LANES), lambda p, i, h, js: (2 * p + h, i, 0)),
                      pl.BlockSpec((B, B), lambda p, i, h, js: (0, 0)),
                      pl.BlockSpec((B, B), lambda p, i, h, js: (0, 0))],
            out_specs=[pl.BlockSpec((B, LANES), lambda p, i, h, js: (i, p)),
                       pl.BlockSpec((S, LANES), lambda p, i, h, js: (0, p)),
                       pl.BlockSpec((S, LANES), lambda p, i, h, js: (0, p))],
            scratch_shapes=[pltpu.VMEM((B, LANES), F32), pltpu.VMEM((B, 1), F32), pltpu.VMEM((B, 1), F32)]),
        out_shape=[jax.ShapeDtypeStruct((S, W), F32)] * 3,
        compiler_params=_cparams(("arbitrary", "arbitrary", "arbitrary")),
    )(js, qkv, qkv, qkv, do, st, us, ti)


def _head_column(blk, head):
    lane = lax.broadcasted_iota(I32, (1, LANES), 1)
    return jnp.sum(jnp.where(lane == head, blk, 0.0), axis=1, keepdims=True)


def _fox_fwd(qkv, c, c_rows, kmax, n_pair):
    S = qkv.shape[0]
    B = FOX_BLOCK
    nq = S // B

    def body(q_ref, k_ref, v_ref, c_ref, cr_ref, km_ref, o_ref, st_ref, js_ref, acc_ref, m_ref, l_ref):
        p, i, half = pl.program_id(0), pl.program_id(1), pl.program_id(2)
        hm = _half_mask(half)
        qv = q_ref[...]
        qs = jnp.where(hm, qv, jnp.zeros_like(qv)) * SCALE
        ccol = _head_column(c_ref[...], 2 * p + half)
        qf = qs.astype(F32)
        qk_bound = (jnp.sqrt(jnp.sum(qf * qf, axis=1, keepdims=True)) * NORM_SLACK
                    * _head_column(km_ref[...], 2 * p + half))
        row = lax.broadcasted_iota(I32, (B, B), 0)
        col = lax.broadcasted_iota(I32, (B, B), 1)
        acc_ref[...] = jnp.zeros_like(acc_ref)
        l_ref[...] = jnp.zeros_like(l_ref)
        m_ref[...] = jnp.full_like(m_ref, NEG_BIG)

        def block(j, diag):
            off = pl.multiple_of(j * B, B)
            kj = k_ref[pl.ds(off, B), :]
            vj = v_ref[pl.ds(off, B), :]
            s = _dot_nt(qs, kj) + (ccol - cr_ref[0, pl.ds(j, 1), :])
            if diag:
                s = jnp.where(col <= row, s, NEG_BIG)
            m_old = m_ref[...]
            m_new = jnp.maximum(m_old, jnp.max(s, axis=1, keepdims=True))
            alpha = jnp.exp(m_old - m_new)
            pv = jnp.exp(s - m_new)
            l_ref[...] = alpha * l_ref[...] + jnp.sum(pv, axis=1, keepdims=True)
            acc_ref[...] = alpha * acc_ref[...] + _dot(pv.astype(BF16), vj)
            m_ref[...] = m_new

        def live(j):
            c_end = cr_ref[0, pl.ds(jnp.maximum(j, 0), 1), :][:, B - 1:B]
            bound = qk_bound + (ccol - c_end) - m_ref[...]
            return (jnp.max(bound) > -EXP_ZERO).astype(I32)

        block(i, True)

        def step(carry):
            j, _ = carry
            block(j, False)
            return j - 1, live(j - 1)

        j_end, _ = lax.while_loop(lambda cr: (cr[0] >= 0) & (cr[1] > 0), step, (i - 1, live(i - 1)))
        js_ref[2 * p + half, i] = j_end + 1
        res = jnp.where(hm, acc_ref[...] / l_ref[...], 0.0)

        @pl.when(half == 0)
        def _():
            o_ref[...] = res

        @pl.when(half == 1)
        def _():
            o_ref[...] += res

        st_ref[0] = jnp.broadcast_to(m_ref[...] + jnp.log(l_ref[...]), (B, LANES))

    return pl.pallas_call(
        body, name="fox_attn_fwd", grid=(n_pair, nq, 2),
        in_specs=[pl.BlockSpec((B, LANES), lambda p, i, h: (i, 3 * n_pair + p)),
                  pl.BlockSpec((S, LANES), lambda p, i, h: (0, 4 * n_pair + p)),
                  pl.BlockSpec((S, LANES), lambda p, i, h: (0, 5 * n_pair + p)),
                  pl.BlockSpec((B, LANES), lambda p, i, h: (i, 0)),
                  pl.BlockSpec((1, nq, B), lambda p, i, h: (2 * p + h, 0, 0)),
                  _full((1, LANES))],
        out_specs=[pl.BlockSpec((B, LANES), lambda p, i, h: (i, p)),
                   pl.BlockSpec((1, B, LANES), lambda p, i, h: (2 * p + h, i, 0)),
                   pl.BlockSpec(memory_space=pltpu.SMEM)],
        out_shape=[jax.ShapeDtypeStruct((S, n_pair * LANES), F32),
                   jax.ShapeDtypeStruct((2 * n_pair, S, LANES), F32),
                   jax.ShapeDtypeStruct((2 * n_pair, nq), I32)],
        scratch_shapes=[pltpu.VMEM((B, LANES), F32), pltpu.VMEM((B, 1), F32), pltpu.VMEM((B, 1), F32)],
        compiler_params=_cparams(("arbitrary", "arbitrary", "arbitrary")),
    )(qkv, qkv, qkv, c, c_rows, kmax)


def _fox_bwd(qkv, do, o, st, c, c_rows, js, n_pair):
    S = qkv.shape[0]
    B = FOX_BLOCK
    nq = S // B

    def body(js_ref, q_ref, k_ref, v_ref, do_ref, o_ref, st_ref, c_ref, cr_ref, dq_ref, dk_ref, dv_ref,
             dc_ref, dqa_ref, rs_ref):
        p, i, half = pl.program_id(0), pl.program_id(1), pl.program_id(2)

        @pl.when((i == 0) & (half == 0))
        def _():
            dk_ref[...] = jnp.zeros_like(dk_ref)
            dv_ref[...] = jnp.zeros_like(dv_ref)
            dc_ref[...] = jnp.zeros_like(dc_ref)

        hm = _half_mask(half)
        qv = q_ref[...]
        qs = jnp.where(hm, qv, jnp.zeros_like(qv)) * SCALE
        dov = jnp.where(hm, do_ref[...], 0.0)
        dob = dov.astype(BF16)
        delta = jnp.sum(dov * o_ref[...], axis=1, keepdims=True)
        lse = st_ref[0][:, 0:1]
        ccol = _head_column(c_ref[...], 2 * p + half) - lse
        row = lax.broadcasted_iota(I32, (B, B), 0)
        col = lax.broadcasted_iota(I32, (B, B), 1)
        dqa_ref[...] = jnp.zeros_like(dqa_ref)
        rs_ref[...] = jnp.zeros_like(rs_ref)

        def block(j, diag):
            off = pl.multiple_of(j * B, B)
            kj = k_ref[pl.ds(off, B), :]
            vj = v_ref[pl.ds(off, B), :]
            pv = jnp.exp(_dot_nt(qs, kj) + (ccol - cr_ref[0, pl.ds(j, 1), :]))
            if diag:
                pv = jnp.where(col <= row, pv, 0.0)
            ds = pv * (_dot_nt(dob, vj) - delta)
            dsb = ds.astype(BF16)
            dqa_ref[...] += _dot(dsb, kj)
            dk_ref[pl.ds(off, B), :] += _dot_tn(dsb, qs)
            dv_ref[pl.ds(off, B), :] += _dot_tn(pv.astype(BF16), dob)
            dc_ref[0, half, pl.ds(j, 1), :] -= jnp.sum(ds, axis=0, keepdims=True)
            rs_ref[...] += jnp.sum(ds, axis=1, keepdims=True)

        def step(j, carry):
            block(j, False)
            return carry

        lax.fori_loop(js_ref[2 * p + half, i], i, step, 0)
        block(i, True)
        res = jnp.where(hm, dqa_ref[...] * SCALE, 0.0)

        @pl.when(half == 0)
        def _():
            dq_ref[...] = res

        @pl.when(half == 1)
        def _():
            dq_ref[...] += res

        dc_ref[0, half, pl.ds(i, 1), :] += jnp.transpose(jnp.broadcast_to(rs_ref[...], (B, LANES)))[0:1, :]

    W = n_pair * LANES
    return pl.pallas_call(
        body, name="fox_attn_bwd",
        grid_spec=pltpu.PrefetchScalarGridSpec(
            num_scalar_prefetch=1, grid=(n_pair, nq, 2),
            in_specs=[pl.BlockSpec((B, LANES), lambda p, i, h, js: (i, 3 * n_pair + p)),
                      pl.BlockSpec((S, LANES), lambda p, i, h, js: (0, 4 * n_pair + p)),
                      pl.BlockSpec((S, LANES), lambda p, i, h, js: (0, 5 * n_pair + p)),
                      pl.BlockSpec((B, LANES), lambda p, i, h, js: (i, p)),
                      pl.BlockSpec((B, LANES), lambda p, i, h, js: (i, p)),
                      pl.BlockSpec((1, B, LANES), lambda p, i, h, js: (2 * p + h, i, 0)),
                      pl.BlockSpec((B, LANES), lambda p, i, h, js: (i, 0)),
                      pl.BlockSpec((1, nq, B), lambda p, i, h, js: (2 * p + h, 0, 0))],
            out_specs=[pl.BlockSpec((B, LANES), lambda p, i, h, js: (i, p)),
                       pl.BlockSpec((S, LANES), lambda p, i, h, js: (0, p)),
                       pl.BlockSpec((S, LANES), lambda p, i, h, js: (0, p)),
                       pl.BlockSpec((1, 2, nq, B), lambda p, i, h, js: (p, 0, 0, 0))],
            scratch_shapes=[pltpu.VMEM((B, LANES), F32), pltpu.VMEM((B, 1), F32)]),
        out_shape=[jax.ShapeDtypeStruct((S, W), F32)] * 3 + [jax.ShapeDtypeStruct((n_pair, 2, nq, B), F32)],
        compiler_params=_cparams(("arbitrary", "arbitrary", "arbitrary")),
    )(js, qkv, qkv, qkv, do, o, st, c, c_rows)


def _adam(w, g, m, v):
    m = ADAM_B1 * m + (1.0 - ADAM_B1) * g
    v = ADAM_B2 * v + (1.0 - ADAM_B2) * (g * g)
    m_hat = m / (1.0 - ADAM_B1 ** ADAM_STEP)
    v_hat = v / (1.0 - ADAM_B2 ** ADAM_STEP)
    delta = -ADAM_LR * (m_hat / (jnp.sqrt(v_hat) + ADAM_EPS) + ADAM_WD * w)
    return delta, m, v


def _reduce_adam(landing, w, m, v, name):
    R, C = w.shape
    TR = next(t for t in (256, 128, R) if R % t == 0)

    def body(l_ref, w_ref, m_ref, v_ref, g_ref, d_ref, nm_ref, nv_ref):
        g = l_ref[0].astype(F32)
        for s in range(1, N_DEV):
            g = g + l_ref[s].astype(F32)
        d, nm, nv = _adam(w_ref[...], g, m_ref[...], v_ref[...])
        g_ref[...] = g
        d_ref[...] = d
        nm_ref[...] = nm
        nv_ref[...] = nv

    blk = pl.BlockSpec((TR, C), lambda i: (i, 0))
    return pl.pallas_call(
        body, name=name, grid=(R // TR,),
        in_specs=[pl.BlockSpec((N_DEV, TR, C), lambda i: (0, i, 0)), blk, blk, blk],
        out_specs=[blk] * 4,
        out_shape=[jax.ShapeDtypeStruct((R, C), F32)] * 4,
        compiler_params=_cparams(("arbitrary",)),
    )(landing, w, m, v)


def _reduce_adam_small(landing, w, m, v):
    R, C = w.shape

    def body(l_ref, w_ref, m_ref, v_ref, g_ref, d_ref, nm_ref, nv_ref, loss_ref):
        g = l_ref[0]
        for s in range(1, N_DEV):
            g = g + l_ref[s]
        d, nm, nv = _adam(w_ref[...], g, m_ref[...], v_ref[...])
        g_ref[...] = g
        d_ref[...] = d
        nm_ref[...] = nm
        nv_ref[...] = nv
        loss_ref[...] = jnp.broadcast_to(0.5 * jnp.sum(g[7:8, :], axis=1, keepdims=True), (1, LANES))

    return pl.pallas_call(
        body, name="reduce_adam_small",
        out_shape=[jax.ShapeDtypeStruct((R, C), F32)] * 4 + [jax.ShapeDtypeStruct((1, LANES), F32)],
    )(landing, w, m, v)


def _pad_lanes(a, width):
    return jnp.pad(a, ((0, 0), (0, width - a.shape[1])))


def _pack_small(D, n_fox, g_cat, l1g, l1b, l2g, l2b, bf, last):
    return jnp.concatenate([g_cat, l1g, l1b, l2g, l2b, _pad_lanes(bf, D), jnp.zeros((1, D), F32), last], axis=0)


def kernel(x, w_in, b_f, g_sb, g_fox, w_out, ln1_g, ln1_b, ln2_g, ln2_b, w_gate_up, w_down, loss_target, m_w_in, m_b_f, m_g_sb, m_g_fox, m_w_out, m_ln1_g, m_ln1_b, m_ln2_g, m_ln2_b, m_w_gate_up, m_w_down, v_w_in, v_b_f, v_g_sb, v_g_fox, v_w_out, v_ln1_g, v_ln1_b, v_ln2_g, v_ln2_b, v_w_gate_up, v_w_down):
    x2, tgt = x[0], loss_target[0]
    S, D = x2.shape
    W = D // 2
    n_pair = W // LANES
    n_fox = W // HEAD_DIM
    F = w_down.shape[1] * N_DEV

    g_in, g_out, g_gu, g_dn = _all_gather([w_in[0].astype(BF16), w_out[0].astype(BF16),
                                           w_gate_up[0].astype(BF16), w_down[0].astype(BF16)])
    w_in_full = g_in.transpose(1, 0, 2).reshape(D, -1)
    w_qkv = w_in_full[:, :6 * W]
    w_f = _pad_lanes(w_in_full[:, 6 * W:], LANES)
    w_o = g_out.reshape(D, D)
    w_gu = g_gu.transpose(1, 0, 2).reshape(D, 2 * F)
    w_dn = g_dn.reshape(F, D)
    gmat = _tri(D, lambda r, c: (r // HEAD_DIM) == (c // HEAD_DIM))
    g_cat = jnp.concatenate([g_sb, g_fox], axis=1)

    qkv, u, c, ksq = _proj_fwd(x2, w_qkv, w_f, _pad_lanes(b_f, LANES), n_fox)
    c_rows = c[:, :n_fox].T.reshape(n_fox, S // FOX_BLOCK, FOX_BLOCK)
    kmax = jnp.sqrt(ksq[0:1]) * NORM_SLACK
    o_sb, st_sb, js_sb = _sb_fwd(qkv, n_pair)
    o_fx, st_fx, js_fx = _fox_fwd(qkv, c, c_rows, kmax, n_pair)
    h1, xh1, rs1, on_b, rr = _post_attn_fwd(o_sb, o_fx, x2, g_cat, gmat, w_o, ln1_g, ln1_b)
    gu, act_b, dyp, sm2 = _mlp_fwd(h1, tgt, w_gu, w_dn, ln2_g, ln2_b)

    dgu_b, dh1 = _mlp_bwd(gu, dyp, w_gu.T, w_dn.T)
    dxa, dmix_b, do_sb, do_fx, sm1 = _post_attn_bwd(dh1, xh1, rs1, ln1_g, o_sb, o_fx, rr, g_cat, gmat, w_o.T)
    dq_sb, dk_sb, dv_sb = _sb_bwd(qkv, do_sb, st_sb, js_sb, n_pair)
    dq_fx, dk_fx, dv_fx, dcr = _fox_bwd(qkv, do_fx, o_fx, st_fx, c, c_rows, js_fx, n_pair)
    dc = _pad_lanes(dcr.reshape(n_fox, S).T, LANES)
    pieces = [dq_sb, dk_sb, dv_sb, dq_fx, dk_fx, dv_fx]
    dx, df_b, sm0 = _proj_bwd(dxa, pieces, dc, u, w_qkv.T, w_f.T, n_fox)

    gw_in = _matmul_tn(x2, pieces + [df_b], "grad_w_in")[:, :6 * W + n_fox]
    gw_out = _matmul_tn(on_b, [dmix_b], "grad_w_out")
    gw_gu = _matmul_tn(h1, [dgu_b], "grad_w_gate_up", n_split=2)
    gw_dn = _matmul_tn(act_b, [dyp], "grad_w_down")

    small = _pack_small(D, n_fox, sm1[2:3], sm1[0:1], sm1[1:2], sm2[0:1], sm2[1:2], sm0[0:1, :n_fox],
                        sm2[2:3] * (1.0 / D))
    chunked = [gw_in.astype(BF16).reshape(D, N_DEV, -1).transpose(1, 0, 2),
               gw_out.astype(BF16).reshape(N_DEV, D // N_DEV, D),
               gw_gu.astype(BF16).reshape(D, N_DEV, -1).transpose(1, 0, 2),
               gw_dn.astype(BF16).reshape(N_DEV, F // N_DEV, D),
               jnp.broadcast_to(small[None], (N_DEV,) + small.shape)]
    l_in, l_out, l_gu, l_dn, l_small = _grad_exchange(chunked)

    r_in = _reduce_adam(l_in, w_in[0], m_w_in[0], v_w_in[0], "reduce_adam_w_in")
    r_out = _reduce_adam(l_out, w_out[0], m_w_out[0], v_w_out[0], "reduce_adam_w_out")
    r_gu = _reduce_adam(l_gu, w_gate_up[0], m_w_gate_up[0], v_w_gate_up[0], "reduce_adam_w_gate_up")
    r_dn = _reduce_adam(l_dn, w_down[0], m_w_down[0], v_w_down[0], "reduce_adam_w_down")
    zero = jnp.zeros((1, D), F32)
    pack = lambda gc, a, b_, c_, d_, bf: _pack_small(D, n_fox, gc, a, b_, c_, d_, bf, zero)
    r_small = _reduce_adam_small(
        l_small,
        pack(g_cat, ln1_g, ln1_b, ln2_g, ln2_b, b_f),
        pack(jnp.concatenate([m_g_sb, m_g_fox], axis=1), m_ln1_g, m_ln1_b, m_ln2_g, m_ln2_b, m_b_f),
        pack(jnp.concatenate([v_g_sb, v_g_fox], axis=1), v_ln1_g, v_ln1_b, v_ln2_g, v_ln2_b, v_b_f))
    loss = r_small[4][0, 0]

    def unpack(kind):
        big = [r_in[kind][None], None, None, None, r_out[kind][None], None, None, None, None,
               r_gu[kind][None], r_dn[kind][None]]
        s = r_small[kind]
        big[1] = s[5:6, :n_fox]
        big[2] = s[0:1, :W]
        big[3] = s[0:1, W:]
        big[5], big[6], big[7], big[8] = s[1:2], s[2:3], s[3:4], s[4:5]
        return big

    return (loss, dx[None], *unpack(0), *unpack(1), *unpack(2), *unpack(3))
```

```python
import functools

import jax
import jax.numpy as jnp
from jax import lax
from jax.experimental import pallas as pl
from jax.experimental.pallas import tpu as pltpu

F32 = jnp.float32
BF16 = jnp.bfloat16
I32 = jnp.int32

N_DEV = 8
HEAD_DIM = 64
LANES = 128
SCALE = HEAD_DIM ** -0.5
ALPHA = 2.0 ** 0.25
LN_EPS = 1e-5
RMS_EPS = 1e-6
ADAM_LR, ADAM_B1, ADAM_B2, ADAM_EPS, ADAM_WD, ADAM_STEP = 0.001, 0.9, 0.999, 1e-08, 0.01, 10
NEG_BIG = -1e30
NORM_SLACK = 1.01
EXP_ZERO = 106.0
VMEM_LIMIT = 60 * 1024 * 1024
ROW_TILE = 256
SB_BLOCK = 256
FOX_BLOCK = 512
FOX_BQ, FOX_BK = 512, 256
assert FOX_BQ == FOX_BLOCK and FOX_BLOCK % FOX_BK == 0
MESH = pl.DeviceIdType.MESH


def _cparams(sem):
    return pltpu.CompilerParams(dimension_semantics=sem, vmem_limit_bytes=VMEM_LIMIT)


def _dot(a, b):
    return jnp.dot(a, b, preferred_element_type=F32)


def _dot_nt(a, b):
    return lax.dot_general(a, b, (((1,), (1,)), ((), ())), preferred_element_type=F32)


def _dot_tn(a, b):
    return lax.dot_general(a, b, (((0,), (0,)), ((), ())), preferred_element_type=F32)


def _split2(a):
    hi = a.astype(BF16)
    lo = (a - hi.astype(F32)).astype(BF16)
    return hi, lo


def _split3(a):
    hi = a.astype(BF16)
    r1 = a - hi.astype(F32)
    mid = r1.astype(BF16)
    lo = (r1 - mid.astype(F32)).astype(BF16)
    return hi, mid, lo


def _dot_acc(a, m):
    hi, lo = _split2(a)
    return _dot(hi, m) + _dot(lo, m)


def _tri(n, fn):
    r = lax.broadcasted_iota(I32, (n, n), 0)
    c = lax.broadcasted_iota(I32, (n, n), 1)
    return fn(r, c).astype(BF16)


def _full(shape):
    nd = len(shape)
    return pl.BlockSpec(shape, lambda *_: (0,) * nd)


def _peer(k):
    x, y, c = lax.axis_index("x"), lax.axis_index("y"), lax.axis_index("c")
    return (1 - x if k & 4 else x, 1 - y if k & 2 else y, 1 - c if k & 1 else c)


def _my_index():
    return 4 * lax.axis_index("x") + 2 * lax.axis_index("y") + lax.axis_index("c")


def _all_gather(shards):
    n = len(shards)

    def body(*refs):
        ins, outs = refs[:n], refs[n:2 * n]
        send_sems, recv_sems, local_sems = refs[2 * n:]
        me = _my_index()
        local = [pltpu.make_async_copy(ins[a], outs[a].at[me], local_sems.at[a]) for a in range(n)]
        for cp in local:
            cp.start()
        sends = []
        for k in range(1, N_DEV):
            for a in range(n):
                cp = pltpu.make_async_remote_copy(
                    src_ref=ins[a], dst_ref=outs[a].at[me],
                    send_sem=send_sems.at[(k - 1) * n + a], recv_sem=recv_sems.at[(k - 1) * n + a],
                    device_id=_peer(k), device_id_type=MESH)
                cp.start()
                sends.append(cp)
        for k in range(1, N_DEV):
            src = me ^ k
            for a in range(n):
                pltpu.make_async_remote_copy(
                    src_ref=ins[a], dst_ref=outs[a].at[src],
                    send_sem=send_sems.at[(k - 1) * n + a], recv_sem=recv_sems.at[(k - 1) * n + a],
                    device_id=_peer(k), device_id_type=MESH).wait_recv()
        for cp in sends:
            cp.wait_send()
        for cp in local:
            cp.wait()

    any_spec = pl.BlockSpec(memory_space=pl.ANY)
    return pl.pallas_call(
        body, name="weights_all_gather",
        out_shape=[jax.ShapeDtypeStruct((N_DEV,) + s.shape, s.dtype) for s in shards],
        in_specs=[any_spec] * n, out_specs=[any_spec] * n,
        scratch_shapes=[pltpu.SemaphoreType.DMA(((N_DEV - 1) * n,)),
                        pltpu.SemaphoreType.DMA(((N_DEV - 1) * n,)),
                        pltpu.SemaphoreType.DMA((n,))],
    )(*shards)


def _grad_exchange(chunked):
    n = len(chunked)

    def body(*refs):
        ins, outs = refs[:n], refs[n:2 * n]
        send_sems, recv_sems, local_sems = refs[2 * n:]
        me = _my_index()
        local = [pltpu.make_async_copy(ins[a].at[me], outs[a].at[me], local_sems.at[a]) for a in range(n)]
        for cp in local:
            cp.start()
        sends = []
        for k in range(1, N_DEV):
            dst = me ^ k
            for a in range(n):
                cp = pltpu.make_async_remote_copy(
                    src_ref=ins[a].at[dst], dst_ref=outs[a].at[me],
                    send_sem=send_sems.at[(k - 1) * n + a], recv_sem=recv_sems.at[(k - 1) * n + a],
                    device_id=_peer(k), device_id_type=MESH)
                cp.start()
                sends.append(cp)
        for k in range(1, N_DEV):
            src = me ^ k
            for a in range(n):
                pltpu.make_async_remote_copy(
                    src_ref=ins[a].at[src], dst_ref=outs[a].at[src],
                    send_sem=send_sems.at[(k - 1) * n + a], recv_sem=recv_sems.at[(k - 1) * n + a],
                    device_id=_peer(k), device_id_type=MESH).wait_recv()
        for cp in sends:
            cp.wait_send()
        for cp in local:
            cp.wait()

    any_spec = pl.BlockSpec(memory_space=pl.ANY)
    return pl.pallas_call(
        body, name="grad_exchange",
        out_shape=[jax.ShapeDtypeStruct(s.shape, s.dtype) for s in chunked],
        in_specs=[any_spec] * n, out_specs=[any_spec] * n,
        scratch_shapes=[pltpu.SemaphoreType.DMA(((N_DEV - 1) * n,)),
                        pltpu.SemaphoreType.DMA(((N_DEV - 1) * n,)),
                        pltpu.SemaphoreType.DMA((n,))],
    )(*chunked)


def _log_sigmoid(u):
    return jnp.minimum(u, 0.0) - jnp.log1p(jnp.exp(-jnp.abs(u)))


def _proj_fwd(x, w_qkv, w_f, bf_pad, n_fox):
    S, D = x.shape
    N = w_qkv.shape[1]
    W = D // 2
    TM = ROW_TILE
    tri = _tri(TM, lambda r, c: c <= r)
    r_ = lax.broadcasted_iota(I32, (W, LANES), 0)
    c_ = lax.broadcasted_iota(I32, (W, LANES), 1)
    head_of = (r_ // HEAD_DIM == c_).astype(BF16)

    def body(x_ref, wq_ref, wf_ref, bf_ref, tri_ref, ho_ref, qkv_ref, u_ref, c_ref, ksq_ref, run_ref):
        @pl.when(pl.program_id(0) == 0)
        def _():
            run_ref[...] = jnp.zeros_like(run_ref)
            ksq_ref[...] = jnp.zeros_like(ksq_ref)

        xb = x_ref[...].astype(BF16)
        for n0 in range(0, N, D):
            chunk = _dot(xb, wq_ref[:, n0:n0 + D]).astype(BF16)
            qkv_ref[:, n0:n0 + D] = chunk
            if n0 == 4 * W:
                kf = chunk[:, :W].astype(F32)
                ksq = jnp.max(_dot_acc(kf * kf, ho_ref[...]), axis=0, keepdims=True)
                ksq_ref[...] = jnp.maximum(ksq_ref[...], ksq)
        u = _dot(xb, wf_ref[...]) + bf_ref[...]
        lane = lax.broadcasted_iota(I32, u.shape, 1)
        logf = jnp.where(lane < n_fox, _log_sigmoid(u), 0.0)
        u_ref[...] = u
        hi, mid, lo = _split3(logf)
        t = tri_ref[...]
        cs = _dot(t, hi) + _dot(t, mid) + _dot(t, lo) + run_ref[...]
        c_ref[...] = cs
        run_ref[...] = cs[TM - 1:TM, :]

    return pl.pallas_call(
        body, name="proj_fwd", grid=(S // TM,),
        in_specs=[pl.BlockSpec((TM, D), lambda i: (i, 0)), _full(w_qkv.shape), _full(w_f.shape),
                  _full(bf_pad.shape), _full(tri.shape), _full(head_of.shape)],
        out_specs=[pl.BlockSpec((TM, N), lambda i: (i, 0)), pl.BlockSpec((TM, LANES), lambda i: (i, 0)),
                   pl.BlockSpec((TM, LANES), lambda i: (i, 0)), _full((8, LANES))],
        out_shape=[jax.ShapeDtypeStruct((S, N), BF16), jax.ShapeDtypeStruct((S, LANES), F32),
                   jax.ShapeDtypeStruct((S, LANES), F32), jax.ShapeDtypeStruct((8, LANES), F32)],
        scratch_shapes=[pltpu.VMEM((1, LANES), F32)],
        compiler_params=_cparams(("arbitrary",)),
    )(x, w_qkv, w_f, bf_pad, tri, head_of)


def _post_attn_fwd(o_sb, o_fx, x, g_cat, gmat, w_out, ln_g, ln_b):
    S, D = x.shape
    H = D // 2
    TM = ROW_TILE

    def body(osb_ref, ofx_ref, x_ref, g_ref, gm_ref, wo_ref, lg_ref, lb_ref,
             h1_ref, xh_ref, rs_ref, on_ref, rr_ref):
        o = jnp.concatenate([osb_ref[...], ofx_ref[...]], axis=1)
        ms = _dot_acc(o * o, gm_ref[...]) * (1.0 / HEAD_DIM)
        r = lax.rsqrt(ms + RMS_EPS)
        onb = (o * r * g_ref[...]).astype(BF16)
        hp = ALPHA * x_ref[...] + _dot(onb, wo_ref[...])
        mu = jnp.mean(hp, axis=-1, keepdims=True)
        d = hp - mu
        rstd = lax.rsqrt(jnp.mean(d * d, axis=-1, keepdims=True) + LN_EPS)
        xh = d * rstd
        h1_ref[...] = xh * lg_ref[...] + lb_ref[...]
        xh_ref[...] = xh
        rs_ref[...] = jnp.broadcast_to(rstd, (TM, LANES))
        on_ref[...] = onb
        rr_ref[...] = r

    row = lambda w: pl.BlockSpec((TM, w), lambda i: (i, 0))
    return pl.pallas_call(
        body, name="post_attn_fwd", grid=(S // TM,),
        in_specs=[row(H), row(H), row(D), _full((1, D)), _full((D, D)), _full((D, D)), _full((1, D)), _full((1, D))],
        out_specs=[row(D), row(D), row(LANES), row(D), row(D)],
        out_shape=[jax.ShapeDtypeStruct((S, D), F32), jax.ShapeDtypeStruct((S, D), F32),
                   jax.ShapeDtypeStruct((S, LANES), F32), jax.ShapeDtypeStruct((S, D), BF16),
                   jax.ShapeDtypeStruct((S, D), F32)],
        compiler_params=_cparams(("arbitrary",)),
    )(o_sb, o_fx, x, g_cat, gmat, w_out, ln_g, ln_b)


def _ln_bwd(dxh, xh, rstd):
    m1 = jnp.mean(dxh, axis=-1, keepdims=True)
    m2 = jnp.mean(dxh * xh, axis=-1, keepdims=True)
    return rstd * (dxh - m1 - xh * m2)


def _mlp_fwd(h1, target, w_gu, w_dn, ln_g, ln_b):
    S, D = h1.shape
    F = w_dn.shape[0]
    TM = ROW_TILE
    FC = F // 2

    def body(h1_ref, tg_ref, wgu_hbm, wdn_hbm, lg_ref, lb_ref, gu_ref, act_ref, dyp_ref, sm_ref, wgu, wdn):
        @pl.when(pl.program_id(0) == 0)
        def _():
            pltpu.sync_copy(wgu_hbm, wgu)
            pltpu.sync_copy(wdn_hbm, wdn)
            sm_ref[...] = jnp.zeros_like(sm_ref)

        h1v = h1_ref[...]
        hb = h1v.astype(BF16)
        ff = jnp.zeros((TM, D), F32)
        for c0 in range(0, F, FC):
            g = _dot(hb, wgu[:, c0:c0 + FC])
            u = _dot(hb, wgu[:, F + c0:F + c0 + FC])
            gu_ref[:, c0:c0 + FC] = g
            gu_ref[:, F + c0:F + c0 + FC] = u
            ab = ((g * jax.nn.sigmoid(g)) * u).astype(BF16)
            act_ref[:, c0:c0 + FC] = ab
            ff = ff + _dot(ab, wdn[c0:c0 + FC, :])
        yp = ALPHA * h1v + ff
        mu = jnp.mean(yp, axis=-1, keepdims=True)
        d = yp - mu
        rstd = lax.rsqrt(jnp.mean(d * d, axis=-1, keepdims=True) + LN_EPS)
        xh = d * rstd
        err = (xh * lg_ref[...] + lb_ref[...]) - tg_ref[...]
        dy = err * (1.0 / D)
        sm_ref[0:1, :] += jnp.sum(dy * xh, axis=0, keepdims=True)
        sm_ref[1:2, :] += jnp.sum(dy, axis=0, keepdims=True)
        sm_ref[2:3, :] += jnp.sum(err * err, axis=0, keepdims=True)
        dyp_ref[...] = _ln_bwd(dy * lg_ref[...], xh, rstd)

    row = lambda w: pl.BlockSpec((TM, w), lambda i: (i, 0))
    hbm = pl.BlockSpec(memory_space=pl.ANY)
    return pl.pallas_call(
        body, name="mlp_fwd", grid=(S // TM,),
        in_specs=[row(D), row(D), hbm, hbm, _full((1, D)), _full((1, D))],
        out_specs=[row(2 * F), row(F), row(D), _full((8, D))],
        out_shape=[jax.ShapeDtypeStruct((S, 2 * F), F32), jax.ShapeDtypeStruct((S, F), BF16),
                   jax.ShapeDtypeStruct((S, D), F32), jax.ShapeDtypeStruct((8, D), F32)],
        scratch_shapes=[pltpu.VMEM(w_gu.shape, BF16), pltpu.VMEM(w_dn.shape, BF16)],
        compiler_params=_cparams(("arbitrary",)),
    )(h1, target, w_gu, w_dn, ln_g, ln_b)


def _mlp_bwd(gu, dyp, w_guT, w_dnT):
    S, D = dyp.shape
    F = w_dnT.shape[1]
    TM = ROW_TILE
    FC = F // 2

    def body(gu_ref, dyp_ref, wguT_hbm, wdnT_hbm, dgu_ref, dh1_ref, wguT, wdnT):
        @pl.when(pl.program_id(0) == 0)
        def _():
            pltpu.sync_copy(wguT_hbm, wguT)
            pltpu.sync_copy(wdnT_hbm, wdnT)

        dypv = dyp_ref[...]
        db = dypv.astype(BF16)
        dh1 = ALPHA * dypv
        for c0 in range(0, F, FC):
            dact = _dot(db, wdnT[:, c0:c0 + FC])
            g = gu_ref[:, c0:c0 + FC]
            u = gu_ref[:, F + c0:F + c0 + FC]
            sg = jax.nn.sigmoid(g)
            dgb = (dact * u * (sg * (1.0 + g * (1.0 - sg)))).astype(BF16)
            dub = (dact * (g * sg)).astype(BF16)
            dgu_ref[:, c0:c0 + FC] = dgb
            dgu_ref[:, F + c0:F + c0 + FC] = dub
            dh1 = dh1 + _dot(dgb, wguT[c0:c0 + FC, :]) + _dot(dub, wguT[F + c0:F + c0 + FC, :])
        dh1_ref[...] = dh1

    row = lambda w: pl.BlockSpec((TM, w), lambda i: (i, 0))
    hbm = pl.BlockSpec(memory_space=pl.ANY)
    return pl.pallas_call(
        body, name="mlp_bwd", grid=(S // TM,),
        in_specs=[row(2 * F), row(D), hbm, hbm],
        out_specs=[row(2 * F), row(D)],
        out_shape=[jax.ShapeDtypeStruct((S, 2 * F), BF16), jax.ShapeDtypeStruct((S, D), F32)],
        scratch_shapes=[pltpu.VMEM(w_guT.shape, BF16), pltpu.VMEM(w_dnT.shape, BF16)],
        compiler_params=_cparams(("arbitrary",)),
    )(gu, dyp, w_guT, w_dnT)


def _post_attn_bwd(dh1, xh, rs, ln_g, o_sb, o_fx, rr, g_cat, gmat, w_outT):
    S, D = dh1.shape
    H = D // 2
    TM = ROW_TILE

    def body(dh1_ref, xh_ref, rs_ref, lg_ref, osb_ref, ofx_ref, rr_ref, g_ref, gm_ref, woT_ref,
             dxa_ref, dmix_ref, dosb_ref, dofx_ref, sm_ref):
        @pl.when(pl.program_id(0) == 0)
        def _():
            sm_ref[...] = jnp.zeros_like(sm_ref)

        dh = dh1_ref[...]
        xhv = xh_ref[...]
        dhp = _ln_bwd(dh * lg_ref[...], xhv, rs_ref[:, 0:1])
        dxa_ref[...] = ALPHA * dhp
        dmb = dhp.astype(BF16)
        dmix_ref[...] = dmb
        don = _dot(dmb, woT_ref[...])
        o = jnp.concatenate([osb_ref[...], ofx_ref[...]], axis=1)
        r = rr_ref[...]
        u = don * g_ref[...]
        t = _dot_acc(u * o, gm_ref[...]) * (1.0 / HEAD_DIM)
        do = r * u - o * (r * r * r) * t
        dosb_ref[...] = do[:, :H]
        dofx_ref[...] = do[:, H:]
        sm_ref[0:1, :] += jnp.sum(dh * xhv, axis=0, keepdims=True)
        sm_ref[1:2, :] += jnp.sum(dh, axis=0, keepdims=True)
        sm_ref[2:3, :] += jnp.sum(don * o * r, axis=0, keepdims=True)

    row = lambda w: pl.BlockSpec((TM, w), lambda i: (i, 0))
    return pl.pallas_call(
        body, name="post_attn_bwd", grid=(S // TM,),
        in_specs=[row(D), row(D), row(LANES), _full((1, D)), row(H), row(H), row(D), _full((1, D)),
                  _full((D, D)), _full((D, D))],
        out_specs=[row(D), row(D), row(H), row(H), _full((8, D))],
        out_shape=[jax.ShapeDtypeStruct((S, D), F32), jax.ShapeDtypeStruct((S, D), BF16),
                   jax.ShapeDtypeStruct((S, H), F32), jax.ShapeDtypeStruct((S, H), F32),
                   jax.ShapeDtypeStruct((8, D), F32)],
        compiler_params=_cparams(("arbitrary",)),
    )(dh1, xh, rs, ln_g, o_sb, o_fx, rr, g_cat, gmat, w_outT)


def _proj_bwd(dxa, pieces, dc, u, w_qkvT, w_fT, n_fox):
    S, D = dxa.shape
    H = D // 2
    TM = ROW_TILE
    nT = S // TM
    tri = _tri(TM, lambda r, c: c >= r)
    n_p = len(pieces)

    def body(*refs):
        dxa_ref = refs[0]
        p_refs = refs[1:1 + n_p]
        dc_ref, u_ref, wq_ref, wf_ref, tri_ref, dx_ref, df_ref, sm_ref, run_ref = refs[1 + n_p:]

        @pl.when(pl.program_id(0) == 0)
        def _():
            run_ref[...] = jnp.zeros_like(run_ref)
            sm_ref[...] = jnp.zeros_like(sm_ref)

        hi, mid, lo = _split3(dc_ref[...])
        t = tri_ref[...]
        dlogf = _dot(t, hi) + _dot(t, mid) + _dot(t, lo) + run_ref[...]
        run_ref[...] = dlogf[0:1, :]
        uv = u_ref[...]
        lane = lax.broadcasted_iota(I32, uv.shape, 1)
        df = jnp.where(lane < n_fox, dlogf * jax.nn.sigmoid(-uv), 0.0)
        sm_ref[0:1, :] += jnp.sum(df, axis=0, keepdims=True)
        dfb = df.astype(BF16)
        df_ref[...] = dfb
        acc = dxa_ref[...] + _dot(dfb, wf_ref[...])
        for a in range(n_p):
            acc = acc + _dot(p_refs[a][...].astype(BF16), wq_ref[a * H:(a + 1) * H, :])
        dx_ref[...] = acc

    rev = lambda w: pl.BlockSpec((TM, w), lambda i: (nT - 1 - i, 0))
    return pl.pallas_call(
        body, name="proj_bwd", grid=(nT,),
        in_specs=[rev(D)] + [rev(H)] * n_p + [rev(LANES), rev(LANES), _full(w_qkvT.shape), _full(w_fT.shape),
                                               _full(tri.shape)],
        out_specs=[rev(D), rev(LANES), _full((8, LANES))],
        out_shape=[jax.ShapeDtypeStruct((S, D), F32), jax.ShapeDtypeStruct((S, LANES), BF16),
                   jax.ShapeDtypeStruct((8, LANES), F32)],
        scratch_shapes=[pltpu.VMEM((1, LANES), F32)],
        compiler_params=_cparams(("arbitrary",)),
    )(dxa, *pieces, dc, u, w_qkvT, w_fT, tri)


def _matmul_tn(a, bs, name, n_split=1):
    S, M = a.shape
    widths = [b.shape[1] for b in bs]
    N = sum(widths)
    assert n_split == 1 or len(bs) == 1
    TK = 512 if S % 512 == 0 else ROW_TILE
    MC = 512 if M % 512 == 0 else 256
    nb = len(bs)

    def body(*refs):
        a_ref, b_refs, o_ref = refs[0], refs[1:1 + nb], refs[1 + nb]

        @pl.when(pl.program_id(1) == 0)
        def _():
            o_ref[...] = jnp.zeros_like(o_ref)

        n0 = 0
        for b_ref in b_refs:
            bv = b_ref[...].astype(BF16)
            w = bv.shape[1]
            for m0 in range(0, M, MC):
                o_ref[m0:m0 + MC, n0:n0 + w] += _dot_tn(a_ref[:, m0:m0 + MC].astype(BF16), bv)
            n0 += w

    return pl.pallas_call(
        body, name=name, grid=(n_split, S // TK),
        in_specs=[pl.BlockSpec((TK, M), lambda n, k: (k, 0))]
        + [pl.BlockSpec((TK, w // n_split), lambda n, k: (k, n)) for w in widths],
        out_specs=pl.BlockSpec((M, N // n_split), lambda n, k: (0, n)),
        out_shape=jax.ShapeDtypeStruct((M, N), F32),
        compiler_params=_cparams(("arbitrary", "arbitrary")),
    )(a, *bs)


def _half_mask(half):
    lane = lax.broadcasted_iota(I32, (1, LANES), 1)
    return (lane >= half * HEAD_DIM) & (lane < half * HEAD_DIM + HEAD_DIM)


def _softplus_neg_abs(z):
    return jnp.log1p(jnp.exp(-jnp.abs(z)))


def _sb_fwd(qkv, n_pair):
    S = qkv.shape[0]
    B = SB_BLOCK
    nq = S // B
    us = _tri(B, lambda r, c: r > c)

    def body(q_ref, k_ref, v_ref, us_ref, o_ref, st_ref, js_ref, acc_ref, r_ref):
        p, i, half = pl.program_id(0), pl.program_id(1), pl.program_id(2)
        hm = _half_mask(half)
        qv = q_ref[...]
        qs = jnp.where(hm, qv, jnp.zeros_like(qv)) * SCALE
        row = lax.broadcasted_iota(I32, (B, B), 0)
        col = lax.broadcasted_iota(I32, (B, B), 1)
        tri = col < row
        acc_ref[...] = jnp.zeros_like(acc_ref)
        r_ref[...] = jnp.zeros_like(r_ref)

        def block(j, diag):
            off = pl.multiple_of(j * B, B)
            kj = k_ref[pl.ds(off, B), :]
            vj = v_ref[pl.ds(off, B), :]
            z = _dot_nt(qs, kj)
            sp = _softplus_neg_abs(z)
            b = jnp.minimum(-z, 0.0) - sp
            a = jnp.minimum(z, 0.0) - sp
            if diag:
                b = jnp.where(tri, b, 0.0)
            lexc = _dot_acc(b, us_ref[...])
            w = jnp.exp(a + (r_ref[...] + lexc))
            if diag:
                w = jnp.where(tri, w, 0.0)
            acc_ref[...] += _dot(w.astype(BF16), vj)
            r_ref[...] += lexc[:, 0:1] + b[:, 0:1]

        def live():
            return (jnp.max(r_ref[...]) > -EXP_ZERO).astype(I32)

        block(i, True)

        def step(carry):
            j, _ = carry
            block(j, False)
            return j - 1, live()

        j_end, _ = lax.while_loop(lambda c: (c[0] >= 0) & (c[1] > 0), step, (i - 1, live()))
        js_ref[2 * p + half, i] = j_end + 1
        res = jnp.where(hm, acc_ref[...], 0.0)

        @pl.when(half == 0)
        def _():
            o_ref[...] = res

        @pl.when(half == 1)
        def _():
            o_ref[...] += res

        st_ref[0] = jnp.broadcast_to(r_ref[...], (B, LANES))

    return pl.pallas_call(
        body, name="sb_attn_fwd", grid=(n_pair, nq, 2),
        in_specs=[pl.BlockSpec((B, LANES), lambda p, i, h: (i, p)),
                  pl.BlockSpec((S, LANES), lambda p, i, h: (0, n_pair + p)),
                  pl.BlockSpec((S, LANES), lambda p, i, h: (0, 2 * n_pair + p)),
                  _full((B, B))],
        out_specs=[pl.BlockSpec((B, LANES), lambda p, i, h: (i, p)),
                   pl.BlockSpec((1, B, LANES), lambda p, i, h: (2 * p + h, i, 0)),
                   pl.BlockSpec(memory_space=pltpu.SMEM)],
        out_shape=[jax.ShapeDtypeStruct((S, n_pair * LANES), F32),
                   jax.ShapeDtypeStruct((2 * n_pair, S, LANES), F32),
                   jax.ShapeDtypeStruct((2 * n_pair, nq), I32)],
        scratch_shapes=[pltpu.VMEM((B, LANES), F32), pltpu.VMEM((B, 1), F32)],
        compiler_params=_cparams(("arbitrary", "arbitrary", "arbitrary")),
    )(qkv, qkv, qkv, us)


def _sb_bwd(qkv, do, st, js, n_pair):
    S = qkv.shape[0]
    B = SB_BLOCK
    nq = S // B
    us = _tri(B, lambda r, c: r > c)
    ti = _tri(B, lambda r, c: r <= c)

    def body(js_ref, q_ref, k_ref, v_ref, do_ref, st_ref, us_ref, ti_ref, dq_ref, dk_ref, dv_ref,
             dqa_ref, pr_ref, er_ref):
        p, i, half = pl.program_id(0), pl.program_id(1), pl.program_id(2)

        @pl.when((i == 0) & (half == 0))
        def _():
            dk_ref[...] = jnp.zeros_like(dk_ref)
            dv_ref[...] = jnp.zeros_like(dv_ref)

        hm = _half_mask(half)
        qv = q_ref[...]
        qs = jnp.where(hm, qv, jnp.zeros_like(qv)) * SCALE
        dob = jnp.where(hm, do_ref[...], 0.0).astype(BF16)
        rproc = st_ref[0][:, 0:1]
        row = lax.broadcasted_iota(I32, (B, B), 0)
        col = lax.broadcasted_iota(I32, (B, B), 1)
        tri = col < row
        dqa_ref[...] = jnp.zeros_like(dqa_ref)
        pr_ref[...] = jnp.zeros_like(pr_ref)
        er_ref[...] = jnp.zeros_like(er_ref)

        def block(j, diag):
            off = pl.multiple_of(j * B, B)
            kj = k_ref[pl.ds(off, B), :]
            vj = v_ref[pl.ds(off, B), :]
            z = _dot_nt(qs, kj)
            sp = _softplus_neg_abs(z)
            b = jnp.minimum(-z, 0.0) - sp
            a = jnp.minimum(z, 0.0) - sp
            if diag:
                b = jnp.where(tri, b, 0.0)
            lexc = _dot_acc(b, us_ref[...])
            pr_new = pr_ref[...] + (lexc[:, 0:1] + b[:, 0:1])
            pr_ref[...] = pr_new
            w = jnp.exp(a + ((rproc - pr_new) + lexc))
            if diag:
                w = jnp.where(tri, w, 0.0)
            e = _dot_nt(dob, vj) * w
            einc = _dot_acc(e, ti_ref[...])
            big_e = er_ref[...] + (einc - e)
            er_ref[...] += einc[:, B - 1:B]
            eb = jnp.exp(b)
            dzb = (e * eb - big_e * (1.0 - eb)).astype(BF16)
            dqa_ref[...] += _dot(dzb, kj)
            dk_ref[pl.ds(off, B), :] += _dot_tn(dzb, qs)
            dv_ref[pl.ds(off, B), :] += _dot_tn(w.astype(BF16), dob)

        def step(j, carry):
            block(j, False)
            return carry

        lax.fori_loop(js_ref[2 * p + half, i], i, step, 0)
        block(i, True)
        res = jnp.where(hm, dqa_ref[...] * SCALE, 0.0)

        @pl.when(half == 0)
        def _():
            dq_ref[...] = res

        @pl.when(half == 1)
        def _():
            dq_ref[...] += res

    W = n_pair * LANES
    return pl.pallas_call(
        body, name="sb_attn_bwd",
        grid_spec=pltpu.PrefetchScalarGridSpec(
            num_scalar_prefetch=1, grid=(n_pair, nq, 2),
            in_specs=[pl.BlockSpec((B, LANES), lambda p, i, h, js: (i, p)),
                      pl.BlockSpec((S, LANES), lambda p, i, h, js: (0, n_pair + p)),
                      pl.BlockSpec((S, LANES), lambda p, i, h, js: (0, 2 * n_pair + p)),
                      pl.BlockSpec((B, LANES), lambda p, i, h, js: (i, p)),
                      pl.BlockSpec((1, B, LANES), lambda p, i, h, js: (2 * p + h, i, 0)),
                      pl.BlockSpec((B, B), lambda p, i, h, js: (0, 0)),
                      pl.BlockSpec((B, B), lambda p, i, h, js: (0, 0))],
            out_specs=[pl.BlockSpec((B, LANES), lambda p, i, h, js: (i, p)),
                       pl.BlockSpec((S, LANES), lambda p, i, h, js: (0, p)),
                       pl.BlockSpec((S, LANES), lambda p, i, h, js: (0, p))],
            scratch_shapes=[pltpu.VMEM((B, LANES), F32), pltpu.VMEM((B, 1), F32), pltpu.VMEM((B, 1), F32)]),
        out_shape=[jax.ShapeDtypeStruct((S, W), F32)] * 3,
        compiler_params=_cparams(("arbitrary", "arbitrary", "arbitrary")),
    )(js, qkv, qkv, qkv, do, st, us, ti)


def _head_column(blk, head):
    lane = lax.broadcasted_iota(I32, (1, LANES), 1)
    return jnp.sum(jnp.where(lane == head, blk, 0.0), axis=1, keepdims=True)


def _fox_fwd(qkv, c, c_rows, kmax, n_pair):
    S = qkv.shape[0]
    BQ, BK = FOX_BQ, FOX_BK
    R = BQ // BK
    nq = S // BQ

    def body(q_ref, k_ref, v_ref, c_ref, cr_ref, km_ref, o_ref, st_ref, js_ref, acc_ref, m_ref, cb_ref, qkb_ref):
        p, i, half = pl.program_id(0), pl.program_id(1), pl.program_id(2)
        hm = _half_mask(half)
        qv = q_ref[...]
        qs = jnp.where(hm, qv, jnp.zeros_like(qv)) * SCALE
        ccol = _head_column(c_ref[...], 2 * p + half)
        cb_ref[...] = jnp.broadcast_to(ccol, (BQ, BK))
        qf = qs.astype(F32)
        qkb_ref[...] = jnp.broadcast_to(
            jnp.sqrt(jnp.sum(qf * qf, axis=1, keepdims=True)) * NORM_SLACK
            * _head_column(km_ref[...], 2 * p + half) + ccol, (BQ, LANES))
        row = lax.broadcasted_iota(I32, (BQ, BK), 0)
        col = lax.broadcasted_iota(I32, (BQ, BK), 1)
        acc_ref[...] = jnp.zeros_like(acc_ref)
        m_ref[...] = jnp.full_like(m_ref, NEG_BIG)

        def block(j, d):
            off = pl.multiple_of(j * BK, BK)
            kj = k_ref[pl.ds(off, BK), :]
            vj = v_ref[pl.ds(off, BK), :]
            v1 = jnp.where(hm, vj, jnp.ones_like(vj))
            s = _dot_nt(qs, kj) + (cb_ref[...] - cr_ref[0, pl.ds(j, 1), :])
            if d is not None:
                keep = col + d * BK <= row
                s = jnp.where(keep, s, NEG_BIG)
            m_old = m_ref[...]
            m_new = jnp.maximum(m_old, jnp.broadcast_to(jnp.max(s, axis=1, keepdims=True), (BQ, LANES)))
            pv = jnp.exp(s - jnp.concatenate([m_new] * (BK // LANES), axis=1))
            if d is not None:
                pv = jnp.where(keep, pv, 0.0)
            acc_ref[...] = jnp.exp(m_old - m_new) * acc_ref[...] + _dot(pv.astype(BF16), v1)
            m_ref[...] = m_new

        def live(j):
            c_end = cr_ref[0, pl.ds(jnp.maximum(j, 0), 1), :][:, BK - 1:BK]
            return (jnp.max(qkb_ref[...] - c_end - m_ref[...]) > -EXP_ZERO).astype(I32)

        for d in range(R - 1, -1, -1):
            block(R * i + d, d)

        def step(carry):
            j, _ = carry
            go_on = live(j - R)
            for d in range(R):
                block(j - d, None)
            return j - R, go_on

        j_end, _ = lax.while_loop(lambda cr: (cr[0] >= 0) & (cr[1] > 0), step, (R * i - 1, live(R * i - 1)))
        js_ref[2 * p + half, i] = j_end + 1
        acc = acc_ref[...]
        denom = jnp.where(hm, pltpu.roll(acc, HEAD_DIM, 1), acc)
        res = jnp.where(hm, acc / denom, 0.0)

        @pl.when(half == 0)
        def _():
            o_ref[...] = res

        @pl.when(half == 1)
        def _():
            o_ref[...] += res

        st_ref[0] = m_ref[...] + jnp.log(denom)

    return pl.pallas_call(
        body, name="fox_attn_fwd", grid=(n_pair, nq, 2),
        in_specs=[pl.BlockSpec((BQ, LANES), lambda p, i, h: (i, 3 * n_pair + p)),
                  pl.BlockSpec((S, LANES), lambda p, i, h: (0, 4 * n_pair + p)),
                  pl.BlockSpec((S, LANES), lambda p, i, h: (0, 5 * n_pair + p)),
                  pl.BlockSpec((BQ, LANES), lambda p, i, h: (i, 0)),
                  pl.BlockSpec((1, S // BK, BK), lambda p, i, h: (2 * p + h, 0, 0)),
                  _full((1, LANES))],
        out_specs=[pl.BlockSpec((BQ, LANES), lambda p, i, h: (i, p)),
                   pl.BlockSpec((1, BQ, LANES), lambda p, i, h: (2 * p + h, i, 0)),
                   pl.BlockSpec(memory_space=pltpu.SMEM)],
        out_shape=[jax.ShapeDtypeStruct((S, n_pair * LANES), F32),
                   jax.ShapeDtypeStruct((2 * n_pair, S, LANES), F32),
                   jax.ShapeDtypeStruct((2 * n_pair, nq), I32)],
        scratch_shapes=[pltpu.VMEM((BQ, LANES), F32), pltpu.VMEM((BQ, LANES), F32), pltpu.VMEM((BQ, BK), F32),
                        pltpu.VMEM((BQ, LANES), F32)],
        compiler_params=_cparams(("arbitrary", "arbitrary", "arbitrary")),
    )(qkv, qkv, qkv, c, c_rows, kmax)


def _fox_bwd(qkv, do, o, st, c, c_rows, js, n_pair):
    S = qkv.shape[0]
    B = FOX_BLOCK
    nq = S // B

    def body(js_ref, q_ref, k_ref, v_ref, do_ref, o_ref, st_ref, c_ref, cr_ref, dq_ref, dk_ref, dv_ref,
             dc_ref, dqa_ref, rs_ref):
        p, i, half = pl.program_id(0), pl.program_id(1), pl.program_id(2)

        @pl.when((i == 0) & (half == 0))
        def _():
            dk_ref[...] = jnp.zeros_like(dk_ref)
            dv_ref[...] = jnp.zeros_like(dv_ref)
            dc_ref[...] = jnp.zeros_like(dc_ref)

        hm = _half_mask(half)
        qv = q_ref[...]
        qs = jnp.where(hm, qv, jnp.zeros_like(qv)) * SCALE
        dov = jnp.where(hm, do_ref[...], 0.0)
        dob = dov.astype(BF16)
        delta = jnp.sum(dov * o_ref[...], axis=1, keepdims=True)
        lse = st_ref[0][:, 0:1]
        ccol = _head_column(c_ref[...], 2 * p + half) - lse
        row = lax.broadcasted_iota(I32, (B, B), 0)
        col = lax.broadcasted_iota(I32, (B, B), 1)
        dqa_ref[...] = jnp.zeros_like(dqa_ref)
        rs_ref[...] = jnp.zeros_like(rs_ref)

        def block(j, diag):
            off = pl.multiple_of(j * B, B)
            kj = k_ref[pl.ds(off, B), :]
            vj = v_ref[pl.ds(off, B), :]
            pv = jnp.exp(_dot_nt(qs, kj) + (ccol - cr_ref[0, pl.ds(j, 1), :]))
            if diag:
                pv = jnp.where(col <= row, pv, 0.0)
            ds = pv * (_dot_nt(dob, vj) - delta)
            dsb = ds.astype(BF16)
            dqa_ref[...] += _dot(dsb, kj)
            dk_ref[pl.ds(off, B), :] += _dot_tn(dsb, qs)
            dv_ref[pl.ds(off, B), :] += _dot_tn(pv.astype(BF16), dob)
            dc_ref[0, half, pl.ds(j, 1), :] -= jnp.sum(ds, axis=0, keepdims=True)
            rs_ref[...] += jnp.sum(ds, axis=1, keepdims=True)

        def step(j, carry):
            block(j, False)
            return carry

        lax.fori_loop(js_ref[2 * p + half, i], i, step, 0)
        block(i, True)
        res = jnp.where(hm, dqa_ref[...] * SCALE, 0.0)

        @pl.when(half == 0)
        def _():
            dq_ref[...] = res

        @pl.when(half == 1)
        def _():
            dq_ref[...] += res

        dc_ref[0, half, pl.ds(i, 1), :] += jnp.transpose(jnp.broadcast_to(rs_ref[...], (B, LANES)))[0:1, :]

    W = n_pair * LANES
    return pl.pallas_call(
        body, name="fox_attn_bwd",
        grid_spec=pltpu.PrefetchScalarGridSpec(
            num_scalar_prefetch=1, grid=(n_pair, nq, 2),
            in_specs=[pl.BlockSpec((B, LANES), lambda p, i, h, js: (i, 3 * n_pair + p)),
                      pl.BlockSpec((S, LANES), lambda p, i, h, js: (0, 4 * n_pair + p)),
                      pl.BlockSpec((S, LANES), lambda p, i, h, js: (0, 5 * n_pair + p)),
                      pl.BlockSpec((B, LANES), lambda p, i, h, js: (i, p)),
                      pl.BlockSpec((B, LANES), lambda p, i, h, js: (i, p)),
                      pl.BlockSpec((1, B, LANES), lambda p, i, h, js: (2 * p + h, i, 0)),
                      pl.BlockSpec((B, LANES), lambda p, i, h, js: (i, 0)),
                      pl.BlockSpec((1, nq, B), lambda p, i, h, js: (2 * p + h, 0, 0))],
            out_specs=[pl.BlockSpec((B, LANES), lambda p, i, h, js: (i, p)),
                       pl.BlockSpec((S, LANES), lambda p, i, h, js: (0, p)),
                       pl.BlockSpec((S, LANES), lambda p, i, h, js: (0, p)),
                       pl.BlockSpec((1, 2, nq, B), lambda p, i, h, js: (p, 0, 0, 0))],
            scratch_shapes=[pltpu.VMEM((B, LANES), F32), pltpu.VMEM((B, 1), F32)]),
        out_shape=[jax.ShapeDtypeStruct((S, W), F32)] * 3 + [jax.ShapeDtypeStruct((n_pair, 2, nq, B), F32)],
        compiler_params=_cparams(("arbitrary", "arbitrary", "arbitrary")),
    )(js, qkv, qkv, qkv, do, o, st, c, c_rows)


def _adam(w, g, m, v):
    m = ADAM_B1 * m + (1.0 - ADAM_B1) * g
    v = ADAM_B2 * v + (1.0 - ADAM_B2) * (g * g)
    m_hat = m / (1.0 - ADAM_B1 ** ADAM_STEP)
    v_hat = v / (1.0 - ADAM_B2 ** ADAM_STEP)
    delta = -ADAM_LR * (m_hat / (jnp.sqrt(v_hat) + ADAM_EPS) + ADAM_WD * w)
    return delta, m, v


def _reduce_adam(landing, w, m, v, name):
    R, C = w.shape
    TR = next(t for t in (256, 128, R) if R % t == 0)

    def body(l_ref, w_ref, m_ref, v_ref, g_ref, d_ref, nm_ref, nv_ref):
        g = l_ref[0].astype(F32)
        for s in range(1, N_DEV):
            g = g + l_ref[s].astype(F32)
        d, nm, nv = _adam(w_ref[...], g, m_ref[...], v_ref[...])
        g_ref[...] = g
        d_ref[...] = d
        nm_ref[...] = nm
        nv_ref[...] = nv

    blk = pl.BlockSpec((TR, C), lambda i: (i, 0))
    return pl.pallas_call(
        body, name=name, grid=(R // TR,),
        in_specs=[pl.BlockSpec((N_DEV, TR, C), lambda i: (0, i, 0)), blk, blk, blk],
        out_specs=[blk] * 4,
        out_shape=[jax.ShapeDtypeStruct((R, C), F32)] * 4,
        compiler_params=_cparams(("arbitrary",)),
    )(landing, w, m, v)


def _reduce_adam_small(landing, w, m, v):
    R, C = w.shape

    def body(l_ref, w_ref, m_ref, v_ref, g_ref, d_ref, nm_ref, nv_ref, loss_ref):
        g = l_ref[0]
        for s in range(1, N_DEV):
            g = g + l_ref[s]
        d, nm, nv = _adam(w_ref[...], g, m_ref[...], v_ref[...])
        g_ref[...] = g
        d_ref[...] = d
        nm_ref[...] = nm
        nv_ref[...] = nv
        loss_ref[...] = jnp.broadcast_to(0.5 * jnp.sum(g[7:8, :], axis=1, keepdims=True), (1, LANES))

    return pl.pallas_call(
        body, name="reduce_adam_small",
        out_shape=[jax.ShapeDtypeStruct((R, C), F32)] * 4 + [jax.ShapeDtypeStruct((1, LANES), F32)],
    )(landing, w, m, v)


def _pad_lanes(a, width):
    return jnp.pad(a, ((0, 0), (0, width - a.shape[1])))


def _pack_small(D, n_fox, g_cat, l1g, l1b, l2g, l2b, bf, last):
    return jnp.concatenate([g_cat, l1g, l1b, l2g, l2b, _pad_lanes(bf, D), jnp.zeros((1, D), F32), last], axis=0)


def kernel(x, w_in, b_f, g_sb, g_fox, w_out, ln1_g, ln1_b, ln2_g, ln2_b, w_gate_up, w_down, loss_target, m_w_in, m_b_f, m_g_sb, m_g_fox, m_w_out, m_ln1_g, m_ln1_b, m_ln2_g, m_ln2_b, m_w_gate_up, m_w_down, v_w_in, v_b_f, v_g_sb, v_g_fox, v_w_out, v_ln1_g, v_ln1_b, v_ln2_g, v_ln2_b, v_w_gate_up, v_w_down):
    x2, tgt = x[0], loss_target[0]
    S, D = x2.shape
    W = D // 2
    n_pair = W // LANES
    n_fox = W // HEAD_DIM
    F = w_down.shape[1] * N_DEV

    g_in, g_out, g_gu, g_dn = _all_gather([w_in[0].astype(BF16), w_out[0].astype(BF16),
                                           w_gate_up[0].astype(BF16), w_down[0].astype(BF16)])
    w_in_full = g_in.transpose(1, 0, 2).reshape(D, -1)
    w_qkv = w_in_full[:, :6 * W]
    w_f = _pad_lanes(w_in_full[:, 6 * W:], LANES)
    w_o = g_out.reshape(D, D)
    w_gu = g_gu.transpose(1, 0, 2).reshape(D, 2 * F)
    w_dn = g_dn.reshape(F, D)
    gmat = _tri(D, lambda r, c: (r // HEAD_DIM) == (c // HEAD_DIM))
    g_cat = jnp.concatenate([g_sb, g_fox], axis=1)

    qkv, u, c, ksq = _proj_fwd(x2, w_qkv, w_f, _pad_lanes(b_f, LANES), n_fox)
    c_t = c[:, :n_fox].T
    c_rows = c_t.reshape(n_fox, S // FOX_BLOCK, FOX_BLOCK)
    kmax = jnp.sqrt(ksq[0:1]) * NORM_SLACK
    o_sb, st_sb, js_sb = _sb_fwd(qkv, n_pair)
    o_fx, st_fx, js_fx = _fox_fwd(qkv, c, c_t.reshape(n_fox, S // FOX_BK, FOX_BK), kmax, n_pair)
    js_fx = js_fx // (FOX_BLOCK // FOX_BK)
    h1, xh1, rs1, on_b, rr = _post_attn_fwd(o_sb, o_fx, x2, g_cat, gmat, w_o, ln1_g, ln1_b)
    gu, act_b, dyp, sm2 = _mlp_fwd(h1, tgt, w_gu, w_dn, ln2_g, ln2_b)

    dgu_b, dh1 = _mlp_bwd(gu, dyp, w_gu.T, w_dn.T)
    dxa, dmix_b, do_sb, do_fx, sm1 = _post_attn_bwd(dh1, xh1, rs1, ln1_g, o_sb, o_fx, rr, g_cat, gmat, w_o.T)
    dq_sb, dk_sb, dv_sb = _sb_bwd(qkv, do_sb, st_sb, js_sb, n_pair)
    dq_fx, dk_fx, dv_fx, dcr = _fox_bwd(qkv, do_fx, o_fx, st_fx, c, c_rows, js_fx, n_pair)
    dc = _pad_lanes(dcr.reshape(n_fox, S).T, LANES)
    pieces = [dq_sb, dk_sb, dv_sb, dq_fx, dk_fx, dv_fx]
    dx, df_b, sm0 = _proj_bwd(dxa, pieces, dc, u, w_qkv.T, w_f.T, n_fox)

    gw_in = _matmul_tn(x2, pieces + [df_b], "grad_w_in")[:, :6 * W + n_fox]
    gw_out = _matmul_tn(on_b, [dmix_b], "grad_w_out")
    gw_gu = _matmul_tn(h1, [dgu_b], "grad_w_gate_up", n_split=2)
    gw_dn = _matmul_tn(act_b, [dyp], "grad_w_down")

    small = _pack_small(D, n_fox, sm1[2:3], sm1[0:1], sm1[1:2], sm2[0:1], sm2[1:2], sm0[0:1, :n_fox],
                        sm2[2:3] * (1.0 / D))
    chunked = [gw_in.astype(BF16).reshape(D, N_DEV, -1).transpose(1, 0, 2),
               gw_out.astype(BF16).reshape(N_DEV, D // N_DEV, D),
               gw_gu.astype(BF16).reshape(D, N_DEV, -1).transpose(1, 0, 2),
               gw_dn.astype(BF16).reshape(N_DEV, F // N_DEV, D),
               jnp.broadcast_to(small[None], (N_DEV,) + small.shape)]
    l_in, l_out, l_gu, l_dn, l_small = _grad_exchange(chunked)

    r_in = _reduce_adam(l_in, w_in[0], m_w_in[0], v_w_in[0], "reduce_adam_w_in")
    r_out = _reduce_adam(l_out, w_out[0], m_w_out[0], v_w_out[0], "reduce_adam_w_out")
    r_gu = _reduce_adam(l_gu, w_gate_up[0], m_w_gate_up[0], v_w_gate_up[0], "reduce_adam_w_gate_up")
    r_dn = _reduce_adam(l_dn, w_down[0], m_w_down[0], v_w_down[0], "reduce_adam_w_down")
    zero = jnp.zeros((1, D), F32)
    pack = lambda gc, a, b_, c_, d_, bf: _pack_small(D, n_fox, gc, a, b_, c_, d_, bf, zero)
    r_small = _reduce_adam_small(
        l_small,
        pack(g_cat, ln1_g, ln1_b, ln2_g, ln2_b, b_f),
        pack(jnp.concatenate([m_g_sb, m_g_fox], axis=1), m_ln1_g, m_ln1_b, m_ln2_g, m_ln2_b, m_b_f),
        pack(jnp.concatenate([v_g_sb, v_g_fox], axis=1), v_ln1_g, v_ln1_b, v_ln2_g, v_ln2_b, v_b_f))
    loss = r_small[4][0, 0]

    def unpack(kind):
        big = [r_in[kind][None], None, None, None, r_out[kind][None], None, None, None, None,
               r_gu[kind][None], r_dn[kind][None]]
        s = r_small[kind]
        big[1] = s[5:6, :n_fox]
        big[2] = s[0:1, :W]
        big[3] = s[0:1, W:]
        big[5], big[6], big[7], big[8] = s[1:2], s[2:3], s[3:4], s[4:5]
        return big

    return (loss, dx[None], *unpack(0), *unpack(1), *unpack(2), *unpack(3))
```

```python
import functools

import jax
import jax.numpy as jnp
from jax import lax
from jax.experimental import pallas as pl
from jax.experimental.pallas import tpu as pltpu

F32 = jnp.float32
BF16 = jnp.bfloat16
I32 = jnp.int32

N_DEV = 8
HEAD_DIM = 64
LANES = 128
SCALE = HEAD_DIM ** -0.5
ALPHA = 2.0 ** 0.25
LN_EPS = 1e-5
RMS_EPS = 1e-6
ADAM_LR, ADAM_B1, ADAM_B2, ADAM_EPS, ADAM_WD, ADAM_STEP = 0.001, 0.9, 0.999, 1e-08, 0.01, 10
NEG_BIG = -1e30
NORM_SLACK = 1.01
EXP_ZERO = 106.0
VMEM_LIMIT = 60 * 1024 * 1024
ROW_TILE = 256
SB_BLOCK = 256
FOX_BLOCK = 512
FOX_BQ, FOX_BK = 512, 256
assert FOX_BQ == FOX_BLOCK and FOX_BLOCK % FOX_BK == 0
MESH = pl.DeviceIdType.MESH


def _cparams(sem):
    return pltpu.CompilerParams(dimension_semantics=sem, vmem_limit_bytes=VMEM_LIMIT)


def _dot(a, b):
    return jnp.dot(a, b, preferred_element_type=F32)


def _dot_nt(a, b):
    return lax.dot_general(a, b, (((1,), (1,)), ((), ())), preferred_element_type=F32)


def _dot_tn(a, b):
    return lax.dot_general(a, b, (((0,), (0,)), ((), ())), preferred_element_type=F32)


def _split2(a):
    hi = a.astype(BF16)
    lo = (a - hi.astype(F32)).astype(BF16)
    return hi, lo


def _split3(a):
    hi = a.astype(BF16)
    r1 = a - hi.astype(F32)
    mid = r1.astype(BF16)
    lo = (r1 - mid.astype(F32)).astype(BF16)
    return hi, mid, lo


def _dot_acc(a, m):
    hi, lo = _split2(a)
    return _dot(hi, m) + _dot(lo, m)


def _tri(n, fn):
    r = lax.broadcasted_iota(I32, (n, n), 0)
    c = lax.broadcasted_iota(I32, (n, n), 1)
    return fn(r, c).astype(BF16)


def _full(shape):
    nd = len(shape)
    return pl.BlockSpec(shape, lambda *_: (0,) * nd)


def _peer(k):
    x, y, c = lax.axis_index("x"), lax.axis_index("y"), lax.axis_index("c")
    return (1 - x if k & 4 else x, 1 - y if k & 2 else y, 1 - c if k & 1 else c)


def _my_index():
    return 4 * lax.axis_index("x") + 2 * lax.axis_index("y") + lax.axis_index("c")


def _all_gather(shards):
    n = len(shards)

    def body(*refs):
        ins, outs = refs[:n], refs[n:2 * n]
        send_sems, recv_sems, local_sems = refs[2 * n:]
        me = _my_index()
        local = [pltpu.make_async_copy(ins[a], outs[a].at[me], local_sems.at[a]) for a in range(n)]
        for cp in local:
            cp.start()
        sends = []
        for k in range(1, N_DEV):
            for a in range(n):
                cp = pltpu.make_async_remote_copy(
                    src_ref=ins[a], dst_ref=outs[a].at[me],
                    send_sem=send_sems.at[(k - 1) * n + a], recv_sem=recv_sems.at[(k - 1) * n + a],
                    device_id=_peer(k), device_id_type=MESH)
                cp.start()
                sends.append(cp)
        for k in range(1, N_DEV):
            src = me ^ k
            for a in range(n):
                pltpu.make_async_remote_copy(
                    src_ref=ins[a], dst_ref=outs[a].at[src],
                    send_sem=send_sems.at[(k - 1) * n + a], recv_sem=recv_sems.at[(k - 1) * n + a],
                    device_id=_peer(k), device_id_type=MESH).wait_recv()
        for cp in sends:
            cp.wait_send()
        for cp in local:
            cp.wait()

    any_spec = pl.BlockSpec(memory_space=pl.ANY)
    return pl.pallas_call(
        body, name="weights_all_gather",
        out_shape=[jax.ShapeDtypeStruct((N_DEV,) + s.shape, s.dtype) for s in shards],
        in_specs=[any_spec] * n, out_specs=[any_spec] * n,
        scratch_shapes=[pltpu.SemaphoreType.DMA(((N_DEV - 1) * n,)),
                        pltpu.SemaphoreType.DMA(((N_DEV - 1) * n,)),
                        pltpu.SemaphoreType.DMA((n,))],
    )(*shards)


def _grad_exchange(chunked):
    n = len(chunked)

    def body(*refs):
        ins, outs = refs[:n], refs[n:2 * n]
        send_sems, recv_sems, local_sems = refs[2 * n:]
        me = _my_index()
        local = [pltpu.make_async_copy(ins[a].at[me], outs[a].at[me], local_sems.at[a]) for a in range(n)]
        for cp in local:
            cp.start()
        sends = []
        for k in range(1, N_DEV):
            dst = me ^ k
            for a in range(n):
                cp = pltpu.make_async_remote_copy(
                    src_ref=ins[a].at[dst], dst_ref=outs[a].at[me],
                    send_sem=send_sems.at[(k - 1) * n + a], recv_sem=recv_sems.at[(k - 1) * n + a],
                    device_id=_peer(k), device_id_type=MESH)
                cp.start()
                sends.append(cp)
        for k in range(1, N_DEV):
            src = me ^ k
            for a in range(n):
                pltpu.make_async_remote_copy(
                    src_ref=ins[a].at[src], dst_ref=outs[a].at[src],
                    send_sem=send_sems.at[(k - 1) * n + a], recv_sem=recv_sems.at[(k - 1) * n + a],
                    device_id=_peer(k), device_id_type=MESH).wait_recv()
        for cp in sends:
            cp.wait_send()
        for cp in local:
            cp.wait()

    any_spec = pl.BlockSpec(memory_space=pl.ANY)
    return pl.pallas_call(
        body, name="grad_exchange",
        out_shape=[jax.ShapeDtypeStruct(s.shape, s.dtype) for s in chunked],
        in_specs=[any_spec] * n, out_specs=[any_spec] * n,
        scratch_shapes=[pltpu.SemaphoreType.DMA(((N_DEV - 1) * n,)),
                        pltpu.SemaphoreType.DMA(((N_DEV - 1) * n,)),
                        pltpu.SemaphoreType.DMA((n,))],
    )(*chunked)


def _log_sigmoid(u):
    return jnp.minimum(u, 0.0) - jnp.log1p(jnp.exp(-jnp.abs(u)))


def _proj_fwd(x, w_qkv, w_f, bf_pad, n_fox):
    S, D = x.shape
    N = w_qkv.shape[1]
    W = D // 2
    TM = ROW_TILE
    tri = _tri(TM, lambda r, c: c <= r)
    r_ = lax.broadcasted_iota(I32, (W, LANES), 0)
    c_ = lax.broadcasted_iota(I32, (W, LANES), 1)
    head_of = (r_ // HEAD_DIM == c_).astype(BF16)

    def body(x_ref, wq_ref, wf_ref, bf_ref, tri_ref, ho_ref, qkv_ref, u_ref, c_ref, ksq_ref, run_ref):
        @pl.when(pl.program_id(0) == 0)
        def _():
            run_ref[...] = jnp.zeros_like(run_ref)
            ksq_ref[...] = jnp.zeros_like(ksq_ref)

        xb = x_ref[...].astype(BF16)
        for n0 in range(0, N, D):
            chunk = _dot(xb, wq_ref[:, n0:n0 + D]).astype(BF16)
            qkv_ref[:, n0:n0 + D] = chunk
            if n0 == 4 * W:
                kf = chunk[:, :W].astype(F32)
                ksq = jnp.max(_dot_acc(kf * kf, ho_ref[...]), axis=0, keepdims=True)
                ksq_ref[...] = jnp.maximum(ksq_ref[...], ksq)
        u = _dot(xb, wf_ref[...]) + bf_ref[...]
        lane = lax.broadcasted_iota(I32, u.shape, 1)
        logf = jnp.where(lane < n_fox, _log_sigmoid(u), 0.0)
        u_ref[...] = u
        hi, mid, lo = _split3(logf)
        t = tri_ref[...]
        cs = _dot(t, hi) + _dot(t, mid) + _dot(t, lo) + run_ref[...]
        c_ref[...] = cs
        run_ref[...] = cs[TM - 1:TM, :]

    return pl.pallas_call(
        body, name="proj_fwd", grid=(S // TM,),
        in_specs=[pl.BlockSpec((TM, D), lambda i: (i, 0)), _full(w_qkv.shape), _full(w_f.shape),
                  _full(bf_pad.shape), _full(tri.shape), _full(head_of.shape)],
        out_specs=[pl.BlockSpec((TM, N), lambda i: (i, 0)), pl.BlockSpec((TM, LANES), lambda i: (i, 0)),
                   pl.BlockSpec((TM, LANES), lambda i: (i, 0)), _full((8, LANES))],
        out_shape=[jax.ShapeDtypeStruct((S, N), BF16), jax.ShapeDtypeStruct((S, LANES), F32),
                   jax.ShapeDtypeStruct((S, LANES), F32), jax.ShapeDtypeStruct((8, LANES), F32)],
        scratch_shapes=[pltpu.VMEM((1, LANES), F32)],
        compiler_params=_cparams(("arbitrary",)),
    )(x, w_qkv, w_f, bf_pad, tri, head_of)


def _post_attn_fwd(o_sb, o_fx, x, g_cat, gmat, w_out, ln_g, ln_b):
    S, D = x.shape
    H = D // 2
    TM = ROW_TILE

    def body(osb_ref, ofx_ref, x_ref, g_ref, gm_ref, wo_ref, lg_ref, lb_ref,
             h1_ref, xh_ref, rs_ref, on_ref, rr_ref):
        o = jnp.concatenate([osb_ref[...], ofx_ref[...]], axis=1)
        ms = _dot_acc(o * o, gm_ref[...]) * (1.0 / HEAD_DIM)
        r = lax.rsqrt(ms + RMS_EPS)
        onb = (o * r * g_ref[...]).astype(BF16)
        hp = ALPHA * x_ref[...] + _dot(onb, wo_ref[...])
        mu = jnp.mean(hp, axis=-1, keepdims=True)
        d = hp - mu
        rstd = lax.rsqrt(jnp.mean(d * d, axis=-1, keepdims=True) + LN_EPS)
        xh = d * rstd
        h1_ref[...] = xh * lg_ref[...] + lb_ref[...]
        xh_ref[...] = xh
        rs_ref[...] = jnp.broadcast_to(rstd, (TM, LANES))
        on_ref[...] = onb
        rr_ref[...] = r

    row = lambda w: pl.BlockSpec((TM, w), lambda i: (i, 0))
    return pl.pallas_call(
        body, name="post_attn_fwd", grid=(S // TM,),
        in_specs=[row(H), row(H), row(D), _full((1, D)), _full((D, D)), _full((D, D)), _full((1, D)), _full((1, D))],
        out_specs=[row(D), row(D), row(LANES), row(D), row(D)],
        out_shape=[jax.ShapeDtypeStruct((S, D), F32), jax.ShapeDtypeStruct((S, D), F32),
                   jax.ShapeDtypeStruct((S, LANES), F32), jax.ShapeDtypeStruct((S, D), BF16),
                   jax.ShapeDtypeStruct((S, D), F32)],
        compiler_params=_cparams(("arbitrary",)),
    )(o_sb, o_fx, x, g_cat, gmat, w_out, ln_g, ln_b)


def _ln_bwd(dxh, xh, rstd):
    m1 = jnp.mean(dxh, axis=-1, keepdims=True)
    m2 = jnp.mean(dxh * xh, axis=-1, keepdims=True)
    return rstd * (dxh - m1 - xh * m2)


def _mlp_fwd(h1, target, w_gu, w_dn, ln_g, ln_b):
    S, D = h1.shape
    F = w_dn.shape[0]
    TM = ROW_TILE
    FC = F // 2

    def body(h1_ref, tg_ref, wgu_hbm, wdn_hbm, lg_ref, lb_ref, gu_ref, act_ref, dyp_ref, sm_ref, wgu, wdn):
        @pl.when(pl.program_id(0) == 0)
        def _():
            pltpu.sync_copy(wgu_hbm, wgu)
            pltpu.sync_copy(wdn_hbm, wdn)
            sm_ref[...] = jnp.zeros_like(sm_ref)

        h1v = h1_ref[...]
        hb = h1v.astype(BF16)
        ff = jnp.zeros((TM, D), F32)
        for c0 in range(0, F, FC):
            g = _dot(hb, wgu[:, c0:c0 + FC])
            u = _dot(hb, wgu[:, F + c0:F + c0 + FC])
            gu_ref[:, c0:c0 + FC] = g
            gu_ref[:, F + c0:F + c0 + FC] = u
            ab = ((g * jax.nn.sigmoid(g)) * u).astype(BF16)
            act_ref[:, c0:c0 + FC] = ab
            ff = ff + _dot(ab, wdn[c0:c0 + FC, :])
        yp = ALPHA * h1v + ff
        mu = jnp.mean(yp, axis=-1, keepdims=True)
        d = yp - mu
        rstd = lax.rsqrt(jnp.mean(d * d, axis=-1, keepdims=True) + LN_EPS)
        xh = d * rstd
        err = (xh * lg_ref[...] + lb_ref[...]) - tg_ref[...]
        dy = err * (1.0 / D)
        sm_ref[0:1, :] += jnp.sum(dy * xh, axis=0, keepdims=True)
        sm_ref[1:2, :] += jnp.sum(dy, axis=0, keepdims=True)
        sm_ref[2:3, :] += jnp.sum(err * err, axis=0, keepdims=True)
        dyp_ref[...] = _ln_bwd(dy * lg_ref[...], xh, rstd)

    row = lambda w: pl.BlockSpec((TM, w), lambda i: (i, 0))
    hbm = pl.BlockSpec(memory_space=pl.ANY)
    return pl.pallas_call(
        body, name="mlp_fwd", grid=(S // TM,),
        in_specs=[row(D), row(D), hbm, hbm, _full((1, D)), _full((1, D))],
        out_specs=[row(2 * F), row(F), row(D), _full((8, D))],
        out_shape=[jax.ShapeDtypeStruct((S, 2 * F), F32), jax.ShapeDtypeStruct((S, F), BF16),
                   jax.ShapeDtypeStruct((S, D), F32), jax.ShapeDtypeStruct((8, D), F32)],
        scratch_shapes=[pltpu.VMEM(w_gu.shape, BF16), pltpu.VMEM(w_dn.shape, BF16)],
        compiler_params=_cparams(("arbitrary",)),
    )(h1, target, w_gu, w_dn, ln_g, ln_b)


def _mlp_bwd(gu, dyp, w_guT, w_dnT):
    S, D = dyp.shape
    F = w_dnT.shape[1]
    TM = ROW_TILE
    FC = F // 2

    def body(gu_ref, dyp_ref, wguT_hbm, wdnT_hbm, dgu_ref, dh1_ref, wguT, wdnT):
        @pl.when(pl.program_id(0) == 0)
        def _():
            pltpu.sync_copy(wguT_hbm, wguT)
            pltpu.sync_copy(wdnT_hbm, wdnT)

        dypv = dyp_ref[...]
        db = dypv.astype(BF16)
        dh1 = ALPHA * dypv
        for c0 in range(0, F, FC):
            dact = _dot(db, wdnT[:, c0:c0 + FC])
            g = gu_ref[:, c0:c0 + FC]
            u = gu_ref[:, F + c0:F + c0 + FC]
            sg = jax.nn.sigmoid(g)
            dgb = (dact * u * (sg * (1.0 + g * (1.0 - sg)))).astype(BF16)
            dub = (dact * (g * sg)).astype(BF16)
            dgu_ref[:, c0:c0 + FC] = dgb
            dgu_ref[:, F + c0:F + c0 + FC] = dub
            dh1 = dh1 + _dot(dgb, wguT[c0:c0 + FC, :]) + _dot(dub, wguT[F + c0:F + c0 + FC, :])
        dh1_ref[...] = dh1

    row = lambda w: pl.BlockSpec((TM, w), lambda i: (i, 0))
    hbm = pl.BlockSpec(memory_space=pl.ANY)
    return pl.pallas_call(
        body, name="mlp_bwd", grid=(S // TM,),
        in_specs=[row(2 * F), row(D), hbm, hbm],
        out_specs=[row(2 * F), row(D)],
        out_shape=[jax.ShapeDtypeStruct((S, 2 * F), BF16), jax.ShapeDtypeStruct((S, D), F32)],
        scratch_shapes=[pltpu.VMEM(w_guT.shape, BF16), pltpu.VMEM(w_dnT.shape, BF16)],
        compiler_params=_cparams(("arbitrary",)),
    )(gu, dyp, w_guT, w_dnT)


def _post_attn_bwd(dh1, xh, rs, ln_g, o_sb, o_fx, rr, g_cat, gmat, w_outT):
    S, D = dh1.shape
    H = D // 2
    TM = ROW_TILE

    def body(dh1_ref, xh_ref, rs_ref, lg_ref, osb_ref, ofx_ref, rr_ref, g_ref, gm_ref, woT_ref,
             dxa_ref, dmix_ref, dosb_ref, dofx_ref, sm_ref):
        @pl.when(pl.program_id(0) == 0)
        def _():
            sm_ref[...] = jnp.zeros_like(sm_ref)

        dh = dh1_ref[...]
        xhv = xh_ref[...]
        dhp = _ln_bwd(dh * lg_ref[...], xhv, rs_ref[:, 0:1])
        dxa_ref[...] = ALPHA * dhp
        dmb = dhp.astype(BF16)
        dmix_ref[...] = dmb
        don = _dot(dmb, woT_ref[...])
        o = jnp.concatenate([osb_ref[...], ofx_ref[...]], axis=1)
        r = rr_ref[...]
        u = don * g_ref[...]
        t = _dot_acc(u * o, gm_ref[...]) * (1.0 / HEAD_DIM)
        do = r * u - o * (r * r * r) * t
        dosb_ref[...] = do[:, :H]
        dofx_ref[...] = do[:, H:]
        sm_ref[0:1, :] += jnp.sum(dh * xhv, axis=0, keepdims=True)
        sm_ref[1:2, :] += jnp.sum(dh, axis=0, keepdims=True)
        sm_ref[2:3, :] += jnp.sum(don * o * r, axis=0, keepdims=True)

    row = lambda w: pl.BlockSpec((TM, w), lambda i: (i, 0))
    return pl.pallas_call(
        body, name="post_attn_bwd", grid=(S // TM,),
        in_specs=[row(D), row(D), row(LANES), _full((1, D)), row(H), row(H), row(D), _full((1, D)),
                  _full((D, D)), _full((D, D))],
        out_specs=[row(D), row(D), row(H), row(H), _full((8, D))],
        out_shape=[jax.ShapeDtypeStruct((S, D), F32), jax.ShapeDtypeStruct((S, D), BF16),
                   jax.ShapeDtypeStruct((S, H), F32), jax.ShapeDtypeStruct((S, H), F32),
                   jax.ShapeDtypeStruct((8, D), F32)],
        compiler_params=_cparams(("arbitrary",)),
    )(dh1, xh, rs, ln_g, o_sb, o_fx, rr, g_cat, gmat, w_outT)


def _proj_bwd(dxa, pieces, dc, u, w_qkvT, w_fT, n_fox):
    S, D = dxa.shape
    H = D // 2
    TM = ROW_TILE
    nT = S // TM
    tri = _tri(TM, lambda r, c: c >= r)
    n_p = len(pieces)

    def body(*refs):
        dxa_ref = refs[0]
        p_refs = refs[1:1 + n_p]
        dc_ref, u_ref, wq_ref, wf_ref, tri_ref, dx_ref, df_ref, sm_ref, run_ref = refs[1 + n_p:]

        @pl.when(pl.program_id(0) == 0)
        def _():
            run_ref[...] = jnp.zeros_like(run_ref)
            sm_ref[...] = jnp.zeros_like(sm_ref)

        hi, mid, lo = _split3(dc_ref[...])
        t = tri_ref[...]
        dlogf = _dot(t, hi) + _dot(t, mid) + _dot(t, lo) + run_ref[...]
        run_ref[...] = dlogf[0:1, :]
        uv = u_ref[...]
        lane = lax.broadcasted_iota(I32, uv.shape, 1)
        df = jnp.where(lane < n_fox, dlogf * jax.nn.sigmoid(-uv), 0.0)
        sm_ref[0:1, :] += jnp.sum(df, axis=0, keepdims=True)
        dfb = df.astype(BF16)
        df_ref[...] = dfb
        acc = dxa_ref[...] + _dot(dfb, wf_ref[...])
        for a in range(n_p):
            acc = acc + _dot(p_refs[a][...].astype(BF16), wq_ref[a * H:(a + 1) * H, :])
        dx_ref[...] = acc

    rev = lambda w: pl.BlockSpec((TM, w), lambda i: (nT - 1 - i, 0))
    return pl.pallas_call(
        body, name="proj_bwd", grid=(nT,),
        in_specs=[rev(D)] + [rev(H)] * n_p + [rev(LANES), rev(LANES), _full(w_qkvT.shape), _full(w_fT.shape),
                                               _full(tri.shape)],
        out_specs=[rev(D), rev(LANES), _full((8, LANES))],
        out_shape=[jax.ShapeDtypeStruct((S, D), F32), jax.ShapeDtypeStruct((S, LANES), BF16),
                   jax.ShapeDtypeStruct((8, LANES), F32)],
        scratch_shapes=[pltpu.VMEM((1, LANES), F32)],
        compiler_params=_cparams(("arbitrary",)),
    )(dxa, *pieces, dc, u, w_qkvT, w_fT, tri)


def _matmul_tn(a, bs, name, n_split=1):
    S, M = a.shape
    widths = [b.shape[1] for b in bs]
    N = sum(widths)
    assert n_split == 1 or len(bs) == 1
    TK = 512 if S % 512 == 0 else ROW_TILE
    MC = 512 if M % 512 == 0 else 256
    nb = len(bs)

    def body(*refs):
        a_ref, b_refs, o_ref = refs[0], refs[1:1 + nb], refs[1 + nb]

        @pl.when(pl.program_id(1) == 0)
        def _():
            o_ref[...] = jnp.zeros_like(o_ref)

        n0 = 0
        for b_ref in b_refs:
            bv = b_ref[...].astype(BF16)
            w = bv.shape[1]
            for m0 in range(0, M, MC):
                o_ref[m0:m0 + MC, n0:n0 + w] += _dot_tn(a_ref[:, m0:m0 + MC].astype(BF16), bv)
            n0 += w

    return pl.pallas_call(
        body, name=name, grid=(n_split, S // TK),
        in_specs=[pl.BlockSpec((TK, M), lambda n, k: (k, 0))]
        + [pl.BlockSpec((TK, w // n_split), lambda n, k: (k, n)) for w in widths],
        out_specs=pl.BlockSpec((M, N // n_split), lambda n, k: (0, n)),
        out_shape=jax.ShapeDtypeStruct((M, N), F32),
        compiler_params=_cparams(("arbitrary", "arbitrary")),
    )(a, *bs)


def _half_mask(half):
    lane = lax.broadcasted_iota(I32, (1, LANES), 1)
    return (lane >= half * HEAD_DIM) & (lane < half * HEAD_DIM + HEAD_DIM)


def _lane_tile(a, width):
    return jnp.concatenate([a] * (width // LANES), axis=1)


def _softplus_neg_abs(z):
    return jnp.log1p(jnp.exp(-jnp.abs(z)))


def _sb_fwd(qkv, n_pair):
    S = qkv.shape[0]
    B = SB_BLOCK
    nq = S // B
    us = _tri(B, lambda r, c: r > c)

    def body(q_ref, k_ref, v_ref, us_ref, o_ref, st_ref, js_ref, acc_ref, r_ref):
        p, i = pl.program_id(0), pl.program_id(1)
        hms = [_half_mask(h) for h in range(2)]
        qv = q_ref[...]
        qss = [jnp.where(hm, qv, jnp.zeros_like(qv)) * SCALE for hm in hms]
        row = lax.broadcasted_iota(I32, (B, B), 0)
        col = lax.broadcasted_iota(I32, (B, B), 1)
        tri = col < row
        acc_ref[...] = jnp.zeros_like(acc_ref)
        r_ref[...] = jnp.zeros_like(r_ref)

        def block(j, diag):
            off = pl.multiple_of(j * B, B)
            kj = k_ref[pl.ds(off, B), :]
            vj = v_ref[pl.ds(off, B), :]
            zs = [_dot_nt(qss[h], kj) for h in range(2)]
            sps = [_softplus_neg_abs(z) for z in zs]
            bs = [jnp.minimum(-z, 0.0) - sp for z, sp in zip(zs, sps)]
            if diag:
                bs = [jnp.where(tri, b, 0.0) for b in bs]
            lexcs = [_dot_acc(b, us_ref[...]) for b in bs]
            ws = [jnp.exp(jnp.minimum(z, 0.0) - sp + (_lane_tile(r_ref[h], B) + lexc))
                  for h, (z, sp, lexc) in enumerate(zip(zs, sps, lexcs))]
            if diag:
                ws = [jnp.where(tri, w, 0.0) for w in ws]
            for h in range(2):
                acc_ref[h] += _dot(ws[h].astype(BF16), vj)
                r_ref[h] += jnp.broadcast_to(lexcs[h][:, 0:1] + bs[h][:, 0:1], (B, LANES))

        def live():
            return (jnp.max(r_ref[...]) > -EXP_ZERO).astype(I32)

        block(i, True)

        def step(carry):
            j, _ = carry
            block(j, False)
            return j - 1, live()

        j_end, _ = lax.while_loop(lambda c: (c[0] >= 0) & (c[1] > 0), step, (i - 1, live()))
        js_ref[2 * p, i] = j_end + 1
        js_ref[2 * p + 1, i] = j_end + 1
        o_ref[...] = jnp.where(hms[0], acc_ref[0], acc_ref[1])
        st_ref[...] = r_ref[...]

    return pl.pallas_call(
        body, name="sb_attn_fwd", grid=(n_pair, nq),
        in_specs=[pl.BlockSpec((B, LANES), lambda p, i: (i, p)),
                  pl.BlockSpec((S, LANES), lambda p, i: (0, n_pair + p)),
                  pl.BlockSpec((S, LANES), lambda p, i: (0, 2 * n_pair + p)),
                  _full((B, B))],
        out_specs=[pl.BlockSpec((B, LANES), lambda p, i: (i, p)),
                   pl.BlockSpec((2, B, LANES), lambda p, i: (p, i, 0)),
                   pl.BlockSpec(memory_space=pltpu.SMEM)],
        out_shape=[jax.ShapeDtypeStruct((S, n_pair * LANES), F32),
                   jax.ShapeDtypeStruct((2 * n_pair, S, LANES), F32),
                   jax.ShapeDtypeStruct((2 * n_pair, nq), I32)],
        scratch_shapes=[pltpu.VMEM((2, B, LANES), F32), pltpu.VMEM((2, B, LANES), F32)],
        compiler_params=_cparams(("arbitrary", "arbitrary")),
    )(qkv, qkv, qkv, us)


def _sb_bwd(qkv, do, st, js, n_pair):
    S = qkv.shape[0]
    B = SB_BLOCK
    nq = S // B
    us = _tri(B, lambda r, c: r > c)
    ti = _tri(B, lambda r, c: r <= c)

    def body(js_ref, q_ref, k_ref, v_ref, do_ref, st_ref, us_ref, ti_ref, dq_ref, dk_ref, dv_ref,
             dqa_ref, pr_ref, er_ref):
        p, i = pl.program_id(0), pl.program_id(1)

        @pl.when(i == 0)
        def _():
            dk_ref[...] = jnp.zeros_like(dk_ref)
            dv_ref[...] = jnp.zeros_like(dv_ref)

        hms = [_half_mask(h) for h in range(2)]
        qv = q_ref[...]
        dov = do_ref[...]
        qss = [jnp.where(hm, qv, jnp.zeros_like(qv)) * SCALE for hm in hms]
        dobs = [jnp.where(hm, dov, 0.0).astype(BF16) for hm in hms]
        row = lax.broadcasted_iota(I32, (B, B), 0)
        col = lax.broadcasted_iota(I32, (B, B), 1)
        tri = col < row
        dqa_ref[...] = jnp.zeros_like(dqa_ref)
        pr_ref[...] = jnp.zeros_like(pr_ref)
        er_ref[...] = jnp.zeros_like(er_ref)

        def block(j, diag):
            off = pl.multiple_of(j * B, B)
            kj = k_ref[pl.ds(off, B), :]
            vj = v_ref[pl.ds(off, B), :]
            hs = range(2)
            zs = [_dot_nt(qss[h], kj) for h in hs]
            dws = [_dot_nt(dobs[h], vj) for h in hs]
            sps = [_softplus_neg_abs(z) for z in zs]
            bs = [jnp.minimum(-z, 0.0) - sp for z, sp in zip(zs, sps)]
            if diag:
                bs = [jnp.where(tri, b, 0.0) for b in bs]
            lexcs = [_dot_acc(b, us_ref[...]) for b in bs]
            ws = []
            for h in hs:
                pr_new = pr_ref[h] + jnp.broadcast_to(lexcs[h][:, 0:1] + bs[h][:, 0:1], (B, LANES))
                pr_ref[h] = pr_new
                w = jnp.exp(jnp.minimum(zs[h], 0.0) - sps[h] + (_lane_tile(st_ref[h] - pr_new, B) + lexcs[h]))
                ws.append(jnp.where(tri, w, 0.0) if diag else w)
            es = [dw * w for dw, w in zip(dws, ws)]
            eincs = [_dot_acc(e, ti_ref[...]) for e in es]
            dzbs = []
            for h in hs:
                er = er_ref[h]
                big_e = _lane_tile(er, B) + (eincs[h] - es[h])
                er_ref[h] = er + jnp.broadcast_to(eincs[h][:, B - 1:B], (B, LANES))
                eb = jnp.exp(bs[h])
                dzbs.append((es[h] * eb - big_e * (1.0 - eb)).astype(BF16))
            for h in hs:
                dqa_ref[h] += _dot(dzbs[h], kj)
            dk_ref[pl.ds(off, B), :] += _dot_tn(dzbs[0], qss[0]) + _dot_tn(dzbs[1], qss[1])
            dv_ref[pl.ds(off, B), :] += (_dot_tn(ws[0].astype(BF16), dobs[0])
                                         + _dot_tn(ws[1].astype(BF16), dobs[1]))

        def step(j, carry):
            block(j, False)
            return carry

        lax.fori_loop(js_ref[2 * p, i], i, step, 0)
        block(i, True)
        dq_ref[...] = jnp.where(hms[0], dqa_ref[0], dqa_ref[1]) * SCALE

    W = n_pair * LANES
    return pl.pallas_call(
        body, name="sb_attn_bwd",
        grid_spec=pltpu.PrefetchScalarGridSpec(
            num_scalar_prefetch=1, grid=(n_pair, nq),
            in_specs=[pl.BlockSpec((B, LANES), lambda p, i, js: (i, p)),
                      pl.BlockSpec((S, LANES), lambda p, i, js: (0, n_pair + p)),
                      pl.BlockSpec((S, LANES), lambda p, i, js: (0, 2 * n_pair + p)),
                      pl.BlockSpec((B, LANES), lambda p, i, js: (i, p)),
                      pl.BlockSpec((2, B, LANES), lambda p, i, js: (p, i, 0)),
                      pl.BlockSpec((B, B), lambda p, i, js: (0, 0)),
                      pl.BlockSpec((B, B), lambda p, i, js: (0, 0))],
            out_specs=[pl.BlockSpec((B, LANES), lambda p, i, js: (i, p)),
                       pl.BlockSpec((S, LANES), lambda p, i, js: (0, p)),
                       pl.BlockSpec((S, LANES), lambda p, i, js: (0, p))],
            scratch_shapes=[pltpu.VMEM((2, B, LANES), F32), pltpu.VMEM((2, B, LANES), F32),
                            pltpu.VMEM((2, B, LANES), F32)]),
        out_shape=[jax.ShapeDtypeStruct((S, W), F32)] * 3,
        compiler_params=_cparams(("arbitrary", "arbitrary")),
    )(js, qkv, qkv, qkv, do, st, us, ti)


def _head_column(blk, head):
    lane = lax.broadcasted_iota(I32, (1, LANES), 1)
    return jnp.sum(jnp.where(lane == head, blk, 0.0), axis=1, keepdims=True)


def _fox_fwd(qkv, c, c_rows, kmax, n_pair):
    S = qkv.shape[0]
    BQ, BK = FOX_BQ, FOX_BK
    R = BQ // BK
    nq = S // BQ

    def body(q_ref, k_ref, v_ref, c_ref, cr_ref, km_ref, o_ref, st_ref, js_ref, acc_ref, m_ref, cb_ref, qkb_ref):
        p, i, half = pl.program_id(0), pl.program_id(1), pl.program_id(2)
        hm = _half_mask(half)
        qv = q_ref[...]
        qs = jnp.where(hm, qv, jnp.zeros_like(qv)) * SCALE
        ccol = _head_column(c_ref[...], 2 * p + half)
        cb_ref[...] = jnp.broadcast_to(ccol, (BQ, BK))
        qf = qs.astype(F32)
        qkb_ref[...] = jnp.broadcast_to(
            jnp.sqrt(jnp.sum(qf * qf, axis=1, keepdims=True)) * NORM_SLACK
            * _head_column(km_ref[...], 2 * p + half) + ccol, (BQ, LANES))
        row = lax.broadcasted_iota(I32, (BQ, BK), 0)
        col = lax.broadcasted_iota(I32, (BQ, BK), 1)
        acc_ref[...] = jnp.zeros_like(acc_ref)
        m_ref[...] = jnp.full_like(m_ref, NEG_BIG)

        def blocks(j_top, diag):
            ss, v1s, keeps = [], [], []
            for d in range(R):
                j = j_top - d
                off = pl.multiple_of(j * BK, BK)
                vj = v_ref[pl.ds(off, BK), :]
                v1s.append(jnp.where(hm, vj, jnp.ones_like(vj)))
                s = _dot_nt(qs, k_ref[pl.ds(off, BK), :]) + (cb_ref[...] - cr_ref[0, pl.ds(j, 1), :])
                if diag:
                    keeps.append(col + (R - 1 - d) * BK <= row)
                    s = jnp.where(keeps[-1], s, NEG_BIG)
                ss.append(s)
            m_old = m_ref[...]
            s_max = jnp.max(functools.reduce(jnp.maximum, ss), axis=1, keepdims=True)
            m_new = jnp.maximum(m_old, jnp.broadcast_to(s_max, (BQ, LANES)))
            m_wide = _lane_tile(m_new, BK)
            pvs = [jnp.exp(s - m_wide) for s in ss]
            if diag:
                pvs = [jnp.where(keep, pv, 0.0) for keep, pv in zip(keeps, pvs)]
            new = _dot(pvs[0].astype(BF16), v1s[0])
            for pv, v1 in zip(pvs[1:], v1s[1:]):
                new = new + _dot(pv.astype(BF16), v1)
            acc_ref[...] = jnp.exp(m_old - m_new) * acc_ref[...] + new
            m_ref[...] = m_new

        def live(j):
            c_end = cr_ref[0, pl.ds(jnp.maximum(j, 0), 1), :][:, BK - 1:BK]
            return (jnp.max(qkb_ref[...] - c_end - m_ref[...]) > -EXP_ZERO).astype(I32)

        blocks(R * i + R - 1, True)

        def step(carry):
            j, _ = carry
            go_on = live(j - R)
            blocks(j, False)
            return j - R, go_on

        j_end, _ = lax.while_loop(lambda cr: (cr[0] >= 0) & (cr[1] > 0), step, (R * i - 1, live(R * i - 1)))
        js_ref[2 * p + half, i] = j_end + 1
        acc = acc_ref[...]
        denom = jnp.where(hm, pltpu.roll(acc, HEAD_DIM, 1), acc)
        res = jnp.where(hm, acc / denom, 0.0)

        @pl.when(half == 0)
        def _():
            o_ref[...] = res

        @pl.when(half == 1)
        def _():
            o_ref[...] += res

        st_ref[0] = m_ref[...] + jnp.log(denom)

    return pl.pallas_call(
        body, name="fox_attn_fwd", grid=(n_pair, nq, 2),
        in_specs=[pl.BlockSpec((BQ, LANES), lambda p, i, h: (i, 3 * n_pair + p)),
                  pl.BlockSpec((S, LANES), lambda p, i, h: (0, 4 * n_pair + p)),
                  pl.BlockSpec((S, LANES), lambda p, i, h: (0, 5 * n_pair + p)),
                  pl.BlockSpec((BQ, LANES), lambda p, i, h: (i, 0)),
                  pl.BlockSpec((1, S // BK, BK), lambda p, i, h: (2 * p + h, 0, 0)),
                  _full((1, LANES))],
        out_specs=[pl.BlockSpec((BQ, LANES), lambda p, i, h: (i, p)),
                   pl.BlockSpec((1, BQ, LANES), lambda p, i, h: (2 * p + h, i, 0)),
                   pl.BlockSpec(memory_space=pltpu.SMEM)],
        out_shape=[jax.ShapeDtypeStruct((S, n_pair * LANES), F32),
                   jax.ShapeDtypeStruct((2 * n_pair, S, LANES), F32),
                   jax.ShapeDtypeStruct((2 * n_pair, nq), I32)],
        scratch_shapes=[pltpu.VMEM((BQ, LANES), F32), pltpu.VMEM((BQ, LANES), F32), pltpu.VMEM((BQ, BK), F32),
                        pltpu.VMEM((BQ, LANES), F32)],
        compiler_params=_cparams(("arbitrary", "arbitrary", "arbitrary")),
    )(qkv, qkv, qkv, c, c_rows, kmax)


def _fox_bwd(qkv, do, o, st, c, c_rows, js, n_pair):
    S = qkv.shape[0]
    B = FOX_BLOCK
    nq = S // B

    def body(js_ref, q_ref, k_ref, v_ref, do_ref, o_ref, st_ref, c_ref, cr_ref, dq_ref, dk_ref, dv_ref,
             dc_ref, dqa_ref, rs_ref, cb_ref, db_ref):
        p, i, half = pl.program_id(0), pl.program_id(1), pl.program_id(2)

        @pl.when((i == 0) & (half == 0))
        def _():
            dk_ref[...] = jnp.zeros_like(dk_ref)
            dv_ref[...] = jnp.zeros_like(dv_ref)
            dc_ref[...] = jnp.zeros_like(dc_ref)

        hm = _half_mask(half)
        qv = q_ref[...]
        qs = jnp.where(hm, qv, jnp.zeros_like(qv)) * SCALE
        dov = jnp.where(hm, do_ref[...], 0.0)
        dob = dov.astype(BF16)
        cb_ref[...] = jnp.broadcast_to(_head_column(c_ref[...], 2 * p + half), (B, LANES)) - st_ref[0]
        db_ref[...] = jnp.broadcast_to(jnp.sum(dov * o_ref[...], axis=1, keepdims=True), (B, LANES))
        row = lax.broadcasted_iota(I32, (B, B), 0)
        col = lax.broadcasted_iota(I32, (B, B), 1)
        dqa_ref[...] = jnp.zeros_like(dqa_ref)
        rs_ref[...] = jnp.zeros_like(rs_ref)

        def block(j, diag):
            off = pl.multiple_of(j * B, B)
            kj = k_ref[pl.ds(off, B), :]
            vj = v_ref[pl.ds(off, B), :]
            pv = jnp.exp(_dot_nt(qs, kj) + (_lane_tile(cb_ref[...], B) - cr_ref[0, pl.ds(j, 1), :]))
            if diag:
                pv = jnp.where(col <= row, pv, 0.0)
            ds = pv * (_dot_nt(dob, vj) - _lane_tile(db_ref[...], B))
            dsb = ds.astype(BF16)
            dqa_ref[...] += _dot(dsb, kj)
            dk_ref[pl.ds(off, B), :] += _dot_tn(dsb, qs)
            dv_ref[pl.ds(off, B), :] += _dot_tn(pv.astype(BF16), dob)
            dc_ref[0, half, pl.ds(j, 1), :] -= jnp.sum(ds, axis=0, keepdims=True)
            rs_ref[...] += jnp.sum(ds, axis=1, keepdims=True)

        def step(j, carry):
            block(j, False)
            return carry

        lax.fori_loop(js_ref[2 * p + half, i], i, step, 0)
        block(i, True)
        res = jnp.where(hm, dqa_ref[...] * SCALE, 0.0)

        @pl.when(half == 0)
        def _():
            dq_ref[...] = res

        @pl.when(half == 1)
        def _():
            dq_ref[...] += res

        dc_ref[0, half, pl.ds(i, 1), :] += jnp.transpose(jnp.broadcast_to(rs_ref[...], (B, LANES)))[0:1, :]

    W = n_pair * LANES
    return pl.pallas_call(
        body, name="fox_attn_bwd",
        grid_spec=pltpu.PrefetchScalarGridSpec(
            num_scalar_prefetch=1, grid=(n_pair, nq, 2),
            in_specs=[pl.BlockSpec((B, LANES), lambda p, i, h, js: (i, 3 * n_pair + p)),
                      pl.BlockSpec((S, LANES), lambda p, i, h, js: (0, 4 * n_pair + p)),
                      pl.BlockSpec((S, LANES), lambda p, i, h, js: (0, 5 * n_pair + p)),
                      pl.BlockSpec((B, LANES), lambda p, i, h, js: (i, p)),
                      pl.BlockSpec((B, LANES), lambda p, i, h, js: (i, p)),
                      pl.BlockSpec((1, B, LANES), lambda p, i, h, js: (2 * p + h, i, 0)),
                      pl.BlockSpec((B, LANES), lambda p, i, h, js: (i, 0)),
                      pl.BlockSpec((1, nq, B), lambda p, i, h, js: (2 * p + h, 0, 0))],
            out_specs=[pl.BlockSpec((B, LANES), lambda p, i, h, js: (i, p)),
                       pl.BlockSpec((S, LANES), lambda p, i, h, js: (0, p)),
                       pl.BlockSpec((S, LANES), lambda p, i, h, js: (0, p)),
                       pl.BlockSpec((1, 2, nq, B), lambda p, i, h, js: (p, 0, 0, 0))],
            scratch_shapes=[pltpu.VMEM((B, LANES), F32), pltpu.VMEM((B, 1), F32), pltpu.VMEM((B, LANES), F32),
                            pltpu.VMEM((B, LANES), F32)]),
        out_shape=[jax.ShapeDtypeStruct((S, W), F32)] * 3 + [jax.ShapeDtypeStruct((n_pair, 2, nq, B), F32)],
        compiler_params=_cparams(("arbitrary", "arbitrary", "arbitrary")),
    )(js, qkv, qkv, qkv, do, o, st, c, c_rows)


def _adam(w, g, m, v):
    m = ADAM_B1 * m + (1.0 - ADAM_B1) * g
    v = ADAM_B2 * v + (1.0 - ADAM_B2) * (g * g)
    m_hat = m / (1.0 - ADAM_B1 ** ADAM_STEP)
    v_hat = v / (1.0 - ADAM_B2 ** ADAM_STEP)
    delta = -ADAM_LR * (m_hat / (jnp.sqrt(v_hat) + ADAM_EPS) + ADAM_WD * w)
    return delta, m, v


def _reduce_adam(landing, w, m, v, name):
    R, C = w.shape
    TR = next(t for t in (256, 128, R) if R % t == 0)

    def body(l_ref, w_ref, m_ref, v_ref, g_ref, d_ref, nm_ref, nv_ref):
        g = l_ref[0].astype(F32)
        for s in range(1, N_DEV):
            g = g + l_ref[s].astype(F32)
        d, nm, nv = _adam(w_ref[...], g, m_ref[...], v_ref[...])
        g_ref[...] = g
        d_ref[...] = d
        nm_ref[...] = nm
        nv_ref[...] = nv

    blk = pl.BlockSpec((TR, C), lambda i: (i, 0))
    return pl.pallas_call(
        body, name=name, grid=(R // TR,),
        in_specs=[pl.BlockSpec((N_DEV, TR, C), lambda i: (0, i, 0)), blk, blk, blk],
        out_specs=[blk] * 4,
        out_shape=[jax.ShapeDtypeStruct((R, C), F32)] * 4,
        compiler_params=_cparams(("arbitrary",)),
    )(landing, w, m, v)


def _reduce_adam_small(landing, w, m, v):
    R, C = w.shape

    def body(l_ref, w_ref, m_ref, v_ref, g_ref, d_ref, nm_ref, nv_ref, loss_ref):
        g = l_ref[0]
        for s in range(1, N_DEV):
            g = g + l_ref[s]
        d, nm, nv = _adam(w_ref[...], g, m_ref[...], v_ref[...])
        g_ref[...] = g
        d_ref[...] = d
        nm_ref[...] = nm
        nv_ref[...] = nv
        loss_ref[...] = jnp.broadcast_to(0.5 * jnp.sum(g[7:8, :], axis=1, keepdims=True), (1, LANES))

    return pl.pallas_call(
        body, name="reduce_adam_small",
        out_shape=[jax.ShapeDtypeStruct((R, C), F32)] * 4 + [jax.ShapeDtypeStruct((1, LANES), F32)],
    )(landing, w, m, v)


def _pad_lanes(a, width):
    return jnp.pad(a, ((0, 0), (0, width - a.shape[1])))


def _pack_small(D, n_fox, g_cat, l1g, l1b, l2g, l2b, bf, last):
    return jnp.concatenate([g_cat, l1g, l1b, l2g, l2b, _pad_lanes(bf, D), jnp.zeros((1, D), F32), last], axis=0)


def kernel(x, w_in, b_f, g_sb, g_fox, w_out, ln1_g, ln1_b, ln2_g, ln2_b, w_gate_up, w_down, loss_target, m_w_in, m_b_f, m_g_sb, m_g_fox, m_w_out, m_ln1_g, m_ln1_b, m_ln2_g, m_ln2_b, m_w_gate_up, m_w_down, v_w_in, v_b_f, v_g_sb, v_g_fox, v_w_out, v_ln1_g, v_ln1_b, v_ln2_g, v_ln2_b, v_w_gate_up, v_w_down):
    x2, tgt = x[0], loss_target[0]
    S, D = x2.shape
    W = D // 2
    n_pair = W // LANES
    n_fox = W // HEAD_DIM
    F = w_down.shape[1] * N_DEV

    g_in, g_out, g_gu, g_dn = _all_gather([w_in[0].astype(BF16), w_out[0].astype(BF16),
                                           w_gate_up[0].astype(BF16), w_down[0].astype(BF16)])
    w_in_full = g_in.transpose(1, 0, 2).reshape(D, -1)
    w_qkv = w_in_full[:, :6 * W]
    w_f = _pad_lanes(w_in_full[:, 6 * W:], LANES)
    w_o = g_out.reshape(D, D)
    w_gu = g_gu.transpose(1, 0, 2).reshape(D, 2 * F)
    w_dn = g_dn.reshape(F, D)
    gmat = _tri(D, lambda r, c: (r // HEAD_DIM) == (c // HEAD_DIM))
    g_cat = jnp.concatenate([g_sb, g_fox], axis=1)

    qkv, u, c, ksq = _proj_fwd(x2, w_qkv, w_f, _pad_lanes(b_f, LANES), n_fox)
    c_t = c[:, :n_fox].T
    c_rows = c_t.reshape(n_fox, S // FOX_BLOCK, FOX_BLOCK)
    kmax = jnp.sqrt(ksq[0:1]) * NORM_SLACK
    o_sb, st_sb, js_sb = _sb_fwd(qkv, n_pair)
    o_fx, st_fx, js_fx = _fox_fwd(qkv, c, c_t.reshape(n_fox, S // FOX_BK, FOX_BK), kmax, n_pair)
    js_fx = js_fx // (FOX_BLOCK // FOX_BK)
    h1, xh1, rs1, on_b, rr = _post_attn_fwd(o_sb, o_fx, x2, g_cat, gmat, w_o, ln1_g, ln1_b)
    gu, act_b, dyp, sm2 = _mlp_fwd(h1, tgt, w_gu, w_dn, ln2_g, ln2_b)

    dgu_b, dh1 = _mlp_bwd(gu, dyp, w_gu.T, w_dn.T)
    dxa, dmix_b, do_sb, do_fx, sm1 = _post_attn_bwd(dh1, xh1, rs1, ln1_g, o_sb, o_fx, rr, g_cat, gmat, w_o.T)
    dq_sb, dk_sb, dv_sb = _sb_bwd(qkv, do_sb, st_sb, js_sb, n_pair)
    dq_fx, dk_fx, dv_fx, dcr = _fox_bwd(qkv, do_fx, o_fx, st_fx, c, c_rows, js_fx, n_pair)
    dc = _pad_lanes(dcr.reshape(n_fox, S).T, LANES)
    pieces = [dq_sb, dk_sb, dv_sb, dq_fx, dk_fx, dv_fx]
    dx, df_b, sm0 = _proj_bwd(dxa, pieces, dc, u, w_qkv.T, w_f.T, n_fox)

    gw_in = _matmul_tn(x2, pieces + [df_b], "grad_w_in")[:, :6 * W + n_fox]
    gw_out = _matmul_tn(on_b, [dmix_b], "grad_w_out")
    gw_gu = _matmul_tn(h1, [dgu_b], "grad_w_gate_up", n_split=2)
    gw_dn = _matmul_tn(act_b, [dyp], "grad_w_down")

    small = _pack_small(D, n_fox, sm1[2:3], sm1[0:1], sm1[1:2], sm2[0:1], sm2[1:2], sm0[0:1, :n_fox],
                        sm2[2:3] * (1.0 / D))
    chunked = [gw_in.astype(BF16).reshape(D, N_DEV, -1).transpose(1, 0, 2),
               gw_out.astype(BF16).reshape(N_DEV, D // N_DEV, D),
               gw_gu.astype(BF16).reshape(D, N_DEV, -1).transpose(1, 0, 2),
               gw_dn.astype(BF16).reshape(N_DEV, F // N_DEV, D),
               jnp.broadcast_to(small[None], (N_DEV,) + small.shape)]
    l_in, l_out, l_gu, l_dn, l_small = _grad_exchange(chunked)

    r_in = _reduce_adam(l_in, w_in[0], m_w_in[0], v_w_in[0], "reduce_adam_w_in")
    r_out = _reduce_adam(l_out, w_out[0], m_w_out[0], v_w_out[0], "reduce_adam_w_out")
    r_gu = _reduce_adam(l_gu, w_gate_up[0], m_w_gate_up[0], v_w_gate_up[0], "reduce_adam_w_gate_up")
    r_dn = _reduce_adam(l_dn, w_down[0], m_w_down[0], v_w_down[0], "reduce_adam_w_down")
    zero = jnp.zeros((1, D), F32)
    pack = lambda gc, a, b_, c_, d_, bf: _pack_small(D, n_fox, gc, a, b_, c_, d_, bf, zero)
    r_small = _reduce_adam_small(
        l_small,
        pack(g_cat, ln1_g, ln1_b, ln2_g, ln2_b, b_f),
        pack(jnp.concatenate([m_g_sb, m_g_fox], axis=1), m_ln1_g, m_ln1_b, m_ln2_g, m_ln2_b, m_b_f),
        pack(jnp.concatenate([v_g_sb, v_g_fox], axis=1), v_ln1_g, v_ln1_b, v_ln2_g, v_ln2_b, v_b_f))
    loss = r_small[4][0, 0]

    def unpack(kind):
        big = [r_in[kind][None], None, None, None, r_out[kind][None], None, None, None, None,
               r_gu[kind][None], r_dn[kind][None]]
        s = r_small[kind]
        big[1] = s[5:6, :n_fox]
        big[2] = s[0:1, :W]
        big[3] = s[0:1, W:]
        big[5], big[6], big[7], big[8] = s[1:2], s[2:3], s[3:4], s[4:5]
        return big

    return (loss, dx[None], *unpack(0), *unpack(1), *unpack(2), *unpack(3))
```

```python
import functools

import jax
import jax.numpy as jnp
from jax import lax
from jax.experimental import pallas as pl
from jax.experimental.pallas import tpu as pltpu

F32 = jnp.float32
BF16 = jnp.bfloat16
I32 = jnp.int32

N_DEV = 8
HEAD_DIM = 64
LANES = 128
SCALE = HEAD_DIM ** -0.5
ALPHA = 2.0 ** 0.25
LN_EPS = 1e-5
RMS_EPS = 1e-6
ADAM_LR, ADAM_B1, ADAM_B2, ADAM_EPS, ADAM_WD, ADAM_STEP = 0.001, 0.9, 0.999, 1e-08, 0.01, 10
NEG_BIG = -1e30
NORM_SLACK = 1.01
EXP_ZERO = 106.0
VMEM_LIMIT = 60 * 1024 * 1024
ROW_TILE = 256
SB_BLOCK = 256
FOX_BLOCK = 512
FOX_BQ, FOX_BK = 512, 256
assert FOX_BQ == FOX_BLOCK and FOX_BLOCK % FOX_BK == 0
MESH = pl.DeviceIdType.MESH


def _cparams(sem):
    return pltpu.CompilerParams(dimension_semantics=sem, vmem_limit_bytes=VMEM_LIMIT)


def _dot(a, b):
    return jnp.dot(a, b, preferred_element_type=F32)


def _dot_nt(a, b):
    return lax.dot_general(a, b, (((1,), (1,)), ((), ())), preferred_element_type=F32)


def _dot_tn(a, b):
    return lax.dot_general(a, b, (((0,), (0,)), ((), ())), preferred_element_type=F32)


def _split2(a):
    hi = a.astype(BF16)
    lo = (a - hi.astype(F32)).astype(BF16)
    return hi, lo


def _split3(a):
    hi = a.astype(BF16)
    r1 = a - hi.astype(F32)
    mid = r1.astype(BF16)
    lo = (r1 - mid.astype(F32)).astype(BF16)
    return hi, mid, lo


def _dot_acc(a, m):
    hi, lo = _split2(a)
    return _dot(hi, m) + _dot(lo, m)


def _tri(n, fn):
    r = lax.broadcasted_iota(I32, (n, n), 0)
    c = lax.broadcasted_iota(I32, (n, n), 1)
    return fn(r, c).astype(BF16)


def _full(shape):
    nd = len(shape)
    return pl.BlockSpec(shape, lambda *_: (0,) * nd)


def _peer(k):
    x, y, c = lax.axis_index("x"), lax.axis_index("y"), lax.axis_index("c")
    return (1 - x if k & 4 else x, 1 - y if k & 2 else y, 1 - c if k & 1 else c)


def _my_index():
    return 4 * lax.axis_index("x") + 2 * lax.axis_index("y") + lax.axis_index("c")


class _Exchange:
    def __init__(self, n, scatter):
        self.n, self.scatter = n, scatter

    def sem_shapes(self):
        return [pltpu.SemaphoreType.DMA(((N_DEV - 1) * self.n,)), pltpu.SemaphoreType.DMA(((N_DEV - 1) * self.n,)),
                pltpu.SemaphoreType.DMA((self.n,))]

    def out_shapes(self, arrays):
        if self.scatter:
            return [jax.ShapeDtypeStruct(s.shape, s.dtype) for s in arrays]
        return [jax.ShapeDtypeStruct((N_DEV,) + s.shape, s.dtype) for s in arrays]

    def _copies(self, ins, outs, sems):
        send_sems, recv_sems, local_sems = sems
        me = _my_index()
        src = lambda a, d: ins[a].at[d] if self.scatter else ins[a]
        local = [pltpu.make_async_copy(src(a, me), outs[a].at[me], local_sems.at[a]) for a in range(self.n)]
        sent, landed = [], []
        for k in range(1, N_DEV):
            for a in range(self.n):
                pair = dict(send_sem=send_sems.at[(k - 1) * self.n + a], recv_sem=recv_sems.at[(k - 1) * self.n + a],
                            device_id=_peer(k), device_id_type=MESH)
                sent.append(pltpu.make_async_remote_copy(src_ref=src(a, me ^ k), dst_ref=outs[a].at[me], **pair))
                landed.append(pltpu.make_async_remote_copy(src_ref=src(a, me ^ k), dst_ref=outs[a].at[me ^ k],
                                                           **pair))
        return local, sent, landed

    def start(self, ins, outs, sems):
        local, sent, _ = self._copies(ins, outs, sems)
        for cp in local + sent:
            cp.start()

    def finish(self, ins, outs, sems):
        local, sent, landed = self._copies(ins, outs, sems)
        for cp in landed:
            cp.wait_recv()
        for cp in sent:
            cp.wait_send()
        for cp in local:
            cp.wait()


def _exchange_call(arrays, scatter, name):
    n = len(arrays)
    ex = _Exchange(n, scatter)

    def body(*refs):
        ins, outs, sems = refs[:n], refs[n:2 * n], refs[2 * n:]
        ex.start(ins, outs, sems)
        ex.finish(ins, outs, sems)

    any_spec = pl.BlockSpec(memory_space=pl.ANY)
    return pl.pallas_call(
        body, name=name, out_shape=ex.out_shapes(arrays),
        in_specs=[any_spec] * n, out_specs=[any_spec] * n, scratch_shapes=ex.sem_shapes(),
    )(*arrays)


def _log_sigmoid(u):
    return jnp.minimum(u, 0.0) - jnp.log1p(jnp.exp(-jnp.abs(u)))


def _proj_fwd(x, w_qkv, w_f, bf_pad, n_fox):
    S, D = x.shape
    N = w_qkv.shape[1]
    W = D // 2
    TM = ROW_TILE
    tri = _tri(TM, lambda r, c: c <= r)
    r_ = lax.broadcasted_iota(I32, (W, LANES), 0)
    c_ = lax.broadcasted_iota(I32, (W, LANES), 1)
    head_of = (r_ // HEAD_DIM == c_).astype(BF16)

    def body(x_ref, wq_ref, wf_ref, bf_ref, tri_ref, ho_ref, qkv_ref, u_ref, c_ref, ksq_ref, run_ref):
        @pl.when(pl.program_id(0) == 0)
        def _():
            run_ref[...] = jnp.zeros_like(run_ref)
            ksq_ref[...] = jnp.zeros_like(ksq_ref)

        xb = x_ref[...].astype(BF16)
        for n0 in range(0, N, D):
            chunk = _dot(xb, wq_ref[:, n0:n0 + D]).astype(BF16)
            qkv_ref[:, n0:n0 + D] = chunk
            if n0 == 4 * W:
                kf = chunk[:, :W].astype(F32)
                ksq = jnp.max(_dot_acc(kf * kf, ho_ref[...]), axis=0, keepdims=True)
                ksq_ref[...] = jnp.maximum(ksq_ref[...], ksq)
        u = _dot(xb, wf_ref[...]) + bf_ref[...]
        lane = lax.broadcasted_iota(I32, u.shape, 1)
        logf = jnp.where(lane < n_fox, _log_sigmoid(u), 0.0)
        u_ref[...] = u
        hi, mid, lo = _split3(logf)
        t = tri_ref[...]
        cs = _dot(t, hi) + _dot(t, mid) + _dot(t, lo) + run_ref[...]
        c_ref[...] = cs
        run_ref[...] = cs[TM - 1:TM, :]

    return pl.pallas_call(
        body, name="proj_fwd", grid=(S // TM,),
        in_specs=[pl.BlockSpec((TM, D), lambda i: (i, 0)), _full(w_qkv.shape), _full(w_f.shape),
                  _full(bf_pad.shape), _full(tri.shape), _full(head_of.shape)],
        out_specs=[pl.BlockSpec((TM, N), lambda i: (i, 0)), pl.BlockSpec((TM, LANES), lambda i: (i, 0)),
                   pl.BlockSpec((TM, LANES), lambda i: (i, 0)), _full((8, LANES))],
        out_shape=[jax.ShapeDtypeStruct((S, N), BF16), jax.ShapeDtypeStruct((S, LANES), F32),
                   jax.ShapeDtypeStruct((S, LANES), F32), jax.ShapeDtypeStruct((8, LANES), F32)],
        scratch_shapes=[pltpu.VMEM((1, LANES), F32)],
        compiler_params=_cparams(("arbitrary",)),
    )(x, w_qkv, w_f, bf_pad, tri, head_of)


def _post_attn_fwd(o_sb, o_fx, x, g_cat, gmat, w_out, ln_g, ln_b):
    S, D = x.shape
    H = D // 2
    TM = ROW_TILE

    def body(osb_ref, ofx_ref, x_ref, g_ref, gm_ref, wo_ref, lg_ref, lb_ref,
             h1_ref, xh_ref, rs_ref, on_ref, rr_ref):
        o = jnp.concatenate([osb_ref[...], ofx_ref[...]], axis=1)
        ms = _dot_acc(o * o, gm_ref[...]) * (1.0 / HEAD_DIM)
        r = lax.rsqrt(ms + RMS_EPS)
        onb = (o * r * g_ref[...]).astype(BF16)
        hp = ALPHA * x_ref[...] + _dot(onb, wo_ref[...])
        mu = jnp.mean(hp, axis=-1, keepdims=True)
        d = hp - mu
        rstd = lax.rsqrt(jnp.mean(d * d, axis=-1, keepdims=True) + LN_EPS)
        xh = d * rstd
        h1_ref[...] = xh * lg_ref[...] + lb_ref[...]
        xh_ref[...] = xh
        rs_ref[...] = jnp.broadcast_to(rstd, (TM, LANES))
        on_ref[...] = onb
        rr_ref[...] = r

    row = lambda w: pl.BlockSpec((TM, w), lambda i: (i, 0))
    return pl.pallas_call(
        body, name="post_attn_fwd", grid=(S // TM,),
        in_specs=[row(H), row(H), row(D), _full((1, D)), _full((D, D)), _full((D, D)), _full((1, D)), _full((1, D))],
        out_specs=[row(D), row(D), row(LANES), row(D), row(D)],
        out_shape=[jax.ShapeDtypeStruct((S, D), F32), jax.ShapeDtypeStruct((S, D), F32),
                   jax.ShapeDtypeStruct((S, LANES), F32), jax.ShapeDtypeStruct((S, D), BF16),
                   jax.ShapeDtypeStruct((S, D), F32)],
        compiler_params=_cparams(("arbitrary",)),
    )(o_sb, o_fx, x, g_cat, gmat, w_out, ln_g, ln_b)


def _ln_bwd(dxh, xh, rstd):
    m1 = jnp.mean(dxh, axis=-1, keepdims=True)
    m2 = jnp.mean(dxh * xh, axis=-1, keepdims=True)
    return rstd * (dxh - m1 - xh * m2)


def _mlp_fwd(h1, target, w_gu, w_dn, ln_g, ln_b):
    S, D = h1.shape
    F = w_dn.shape[0]
    TM = ROW_TILE
    FC = F // 2

    def body(h1_ref, tg_ref, wgu_hbm, wdn_hbm, lg_ref, lb_ref, gu_ref, act_ref, dyp_ref, sm_ref, wgu, wdn):
        @pl.when(pl.program_id(0) == 0)
        def _():
            pltpu.sync_copy(wgu_hbm, wgu)
            pltpu.sync_copy(wdn_hbm, wdn)
            sm_ref[...] = jnp.zeros_like(sm_ref)

        h1v = h1_ref[...]
        hb = h1v.astype(BF16)
        ff = jnp.zeros((TM, D), F32)
        for c0 in range(0, F, FC):
            g = _dot(hb, wgu[:, c0:c0 + FC])
            u = _dot(hb, wgu[:, F + c0:F + c0 + FC])
            gu_ref[:, c0:c0 + FC] = g
            gu_ref[:, F + c0:F + c0 + FC] = u
            ab = ((g * jax.nn.sigmoid(g)) * u).astype(BF16)
            act_ref[:, c0:c0 + FC] = ab
            ff = ff + _dot(ab, wdn[c0:c0 + FC, :])
        yp = ALPHA * h1v + ff
        mu = jnp.mean(yp, axis=-1, keepdims=True)
        d = yp - mu
        rstd = lax.rsqrt(jnp.mean(d * d, axis=-1, keepdims=True) + LN_EPS)
        xh = d * rstd
        err = (xh * lg_ref[...] + lb_ref[...]) - tg_ref[...]
        dy = err * (1.0 / D)
        sm_ref[0:1, :] += jnp.sum(dy * xh, axis=0, keepdims=True)
        sm_ref[1:2, :] += jnp.sum(dy, axis=0, keepdims=True)
        sm_ref[2:3, :] += jnp.sum(err * err, axis=0, keepdims=True)
        dyp_ref[...] = _ln_bwd(dy * lg_ref[...], xh, rstd)

    row = lambda w: pl.BlockSpec((TM, w), lambda i: (i, 0))
    hbm = pl.BlockSpec(memory_space=pl.ANY)
    return pl.pallas_call(
        body, name="mlp_fwd", grid=(S // TM,),
        in_specs=[row(D), row(D), hbm, hbm, _full((1, D)), _full((1, D))],
        out_specs=[row(2 * F), row(F), row(D), _full((8, D))],
        out_shape=[jax.ShapeDtypeStruct((S, 2 * F), F32), jax.ShapeDtypeStruct((S, F), BF16),
                   jax.ShapeDtypeStruct((S, D), F32), jax.ShapeDtypeStruct((8, D), F32)],
        scratch_shapes=[pltpu.VMEM(w_gu.shape, BF16), pltpu.VMEM(w_dn.shape, BF16)],
        compiler_params=_cparams(("arbitrary",)),
    )(h1, target, w_gu, w_dn, ln_g, ln_b)


def _mlp_bwd(gu, dyp, w_guT, w_dnT):
    S, D = dyp.shape
    F = w_dnT.shape[1]
    TM = ROW_TILE
    FC = F // 2

    def body(gu_ref, dyp_ref, wguT_hbm, wdnT_hbm, dgu_ref, dh1_ref, wguT, wdnT):
        @pl.when(pl.program_id(0) == 0)
        def _():
            pltpu.sync_copy(wguT_hbm, wguT)
            pltpu.sync_copy(wdnT_hbm, wdnT)

        dypv = dyp_ref[...]
        db = dypv.astype(BF16)
        dh1 = ALPHA * dypv
        for c0 in range(0, F, FC):
            dact = _dot(db, wdnT[:, c0:c0 + FC])
            g = gu_ref[:, c0:c0 + FC]
            u = gu_ref[:, F + c0:F + c0 + FC]
            sg = jax.nn.sigmoid(g)
            dgb = (dact * u * (sg * (1.0 + g * (1.0 - sg)))).astype(BF16)
            dub = (dact * (g * sg)).astype(BF16)
            dgu_ref[:, c0:c0 + FC] = dgb
            dgu_ref[:, F + c0:F + c0 + FC] = dub
            dh1 = dh1 + _dot(dgb, wguT[c0:c0 + FC, :]) + _dot(dub, wguT[F + c0:F + c0 + FC, :])
        dh1_ref[...] = dh1

    row = lambda w: pl.BlockSpec((TM, w), lambda i: (i, 0))
    hbm = pl.BlockSpec(memory_space=pl.ANY)
    return pl.pallas_call(
        body, name="mlp_bwd", grid=(S // TM,),
        in_specs=[row(2 * F), row(D), hbm, hbm],
        out_specs=[row(2 * F), row(D)],
        out_shape=[jax.ShapeDtypeStruct((S, 2 * F), BF16), jax.ShapeDtypeStruct((S, D), F32)],
        scratch_shapes=[pltpu.VMEM(w_guT.shape, BF16), pltpu.VMEM(w_dnT.shape, BF16)],
        compiler_params=_cparams(("arbitrary",)),
    )(gu, dyp, w_guT, w_dnT)


def _post_attn_bwd(dh1, xh, rs, ln_g, o_sb, o_fx, rr, g_cat, gmat, w_outT, chunked):
    S, D = dh1.shape
    H = D // 2
    TM = ROW_TILE
    nT = S // TM
    n_ch = len(chunked)
    ex = _Exchange(n_ch, scatter=True)

    def body(*refs):
        dh1_ref, xh_ref, rs_ref, lg_ref, osb_ref, ofx_ref, rr_ref, g_ref, gm_ref, woT_ref = refs[:10]
        ch_in = refs[10:10 + n_ch]
        dxa_ref, dmix_ref, dosb_ref, dofx_ref, sm_ref = refs[10 + n_ch:15 + n_ch]
        ch_out = refs[15 + n_ch:15 + 2 * n_ch]
        sems = refs[15 + 2 * n_ch:]

        @pl.when(pl.program_id(0) == 0)
        def _():
            ex.start(ch_in, ch_out, sems)
            sm_ref[...] = jnp.zeros_like(sm_ref)

        dh = dh1_ref[...]
        xhv = xh_ref[...]
        dhp = _ln_bwd(dh * lg_ref[...], xhv, rs_ref[:, 0:1])
        dxa_ref[...] = ALPHA * dhp
        dmb = dhp.astype(BF16)
        dmix_ref[...] = dmb
        don = _dot(dmb, woT_ref[...])
        o = jnp.concatenate([osb_ref[...], ofx_ref[...]], axis=1)
        r = rr_ref[...]
        u = don * g_ref[...]
        t = _dot_acc(u * o, gm_ref[...]) * (1.0 / HEAD_DIM)
        do = r * u - o * (r * r * r) * t
        dosb_ref[...] = do[:, :H]
        dofx_ref[...] = do[:, H:]
        sm_ref[0:1, :] += jnp.sum(dh * xhv, axis=0, keepdims=True)
        sm_ref[1:2, :] += jnp.sum(dh, axis=0, keepdims=True)
        sm_ref[2:3, :] += jnp.sum(don * o * r, axis=0, keepdims=True)

        @pl.when(pl.program_id(0) == nT - 1)
        def _():
            ex.finish(ch_in, ch_out, sems)

    row = lambda w: pl.BlockSpec((TM, w), lambda i: (i, 0))
    any_spec = pl.BlockSpec(memory_space=pl.ANY)
    return pl.pallas_call(
        body, name="post_attn_bwd", grid=(nT,),
        in_specs=[row(D), row(D), row(LANES), _full((1, D)), row(H), row(H), row(D), _full((1, D)),
                  _full((D, D)), _full((D, D))] + [any_spec] * n_ch,
        out_specs=[row(D), row(D), row(H), row(H), _full((8, D))] + [any_spec] * n_ch,
        out_shape=[jax.ShapeDtypeStruct((S, D), F32), jax.ShapeDtypeStruct((S, D), BF16),
                   jax.ShapeDtypeStruct((S, H), F32), jax.ShapeDtypeStruct((S, H), F32),
                   jax.ShapeDtypeStruct((8, D), F32)] + ex.out_shapes(chunked),
        scratch_shapes=ex.sem_shapes(),
        compiler_params=_cparams(("arbitrary",)),
    )(dh1, xh, rs, ln_g, o_sb, o_fx, rr, g_cat, gmat, w_outT, *chunked)


def _proj_bwd(dxa, pieces, dc, u, w_qkvT, w_fT, n_fox):
    S, D = dxa.shape
    H = D // 2
    TM = ROW_TILE
    nT = S // TM
    tri = _tri(TM, lambda r, c: c >= r)
    n_p = len(pieces)

    def body(*refs):
        dxa_ref = refs[0]
        p_refs = refs[1:1 + n_p]
        dc_ref, u_ref, wq_ref, wf_ref, tri_ref, dx_ref, df_ref, sm_ref, run_ref = refs[1 + n_p:]

        @pl.when(pl.program_id(0) == 0)
        def _():
            run_ref[...] = jnp.zeros_like(run_ref)
            sm_ref[...] = jnp.zeros_like(sm_ref)

        hi, mid, lo = _split3(dc_ref[...])
        t = tri_ref[...]
        dlogf = _dot(t, hi) + _dot(t, mid) + _dot(t, lo) + run_ref[...]
        run_ref[...] = dlogf[0:1, :]
        uv = u_ref[...]
        lane = lax.broadcasted_iota(I32, uv.shape, 1)
        df = jnp.where(lane < n_fox, dlogf * jax.nn.sigmoid(-uv), 0.0)
        sm_ref[0:1, :] += jnp.sum(df, axis=0, keepdims=True)
        dfb = df.astype(BF16)
        df_ref[...] = dfb
        acc = dxa_ref[...] + _dot(dfb, wf_ref[...])
        for a in range(n_p):
            acc = acc + _dot(p_refs[a][...].astype(BF16), wq_ref[a * H:(a + 1) * H, :])
        dx_ref[...] = acc

    rev = lambda w: pl.BlockSpec((TM, w), lambda i: (nT - 1 - i, 0))
    return pl.pallas_call(
        body, name="proj_bwd", grid=(nT,),
        in_specs=[rev(D)] + [rev(H)] * n_p + [rev(LANES), rev(LANES), _full(w_qkvT.shape), _full(w_fT.shape),
                                               _full(tri.shape)],
        out_specs=[rev(D), rev(LANES), _full((8, LANES))],
        out_shape=[jax.ShapeDtypeStruct((S, D), F32), jax.ShapeDtypeStruct((S, LANES), BF16),
                   jax.ShapeDtypeStruct((8, LANES), F32)],
        scratch_shapes=[pltpu.VMEM((1, LANES), F32)],
        compiler_params=_cparams(("arbitrary",)),
    )(dxa, *pieces, dc, u, w_qkvT, w_fT, tri)


def _matmul_tn(a, bs, name, n_split=1):
    S, M = a.shape
    widths = [b.shape[1] for b in bs]
    N = sum(widths)
    assert n_split == 1 or len(bs) == 1
    TK = 512 if S % 512 == 0 else ROW_TILE
    MC = 512 if M % 512 == 0 else 256
    nb = len(bs)

    def body(*refs):
        a_ref, b_refs, o_ref = refs[0], refs[1:1 + nb], refs[1 + nb]

        @pl.when(pl.program_id(1) == 0)
        def _():
            o_ref[...] = jnp.zeros_like(o_ref)

        n0 = 0
        for b_ref in b_refs:
            bv = b_ref[...].astype(BF16)
            w = bv.shape[1]
            for m0 in range(0, M, MC):
                o_ref[m0:m0 + MC, n0:n0 + w] += _dot_tn(a_ref[:, m0:m0 + MC].astype(BF16), bv)
            n0 += w

    return pl.pallas_call(
        body, name=name, grid=(n_split, S // TK),
        in_specs=[pl.BlockSpec((TK, M), lambda n, k: (k, 0))]
        + [pl.BlockSpec((TK, w // n_split), lambda n, k: (k, n)) for w in widths],
        out_specs=pl.BlockSpec((M, N // n_split), lambda n, k: (0, n)),
        out_shape=jax.ShapeDtypeStruct((M, N), F32),
        compiler_params=_cparams(("arbitrary", "arbitrary")),
    )(a, *bs)


def _half_mask(half):
    lane = lax.broadcasted_iota(I32, (1, LANES), 1)
    return (lane >= half * HEAD_DIM) & (lane < half * HEAD_DIM + HEAD_DIM)


def _lane_tile(a, width):
    return jnp.concatenate([a] * (width // LANES), axis=1)


def _softplus_neg_abs(z):
    return jnp.log1p(jnp.exp(-jnp.abs(z)))


def _sb_fwd(qkv, n_pair, shards):
    S = qkv.shape[0]
    B = SB_BLOCK
    nq = S // B
    us = _tri(B, lambda r, c: r > c)
    n_sh = len(shards)
    ex = _Exchange(n_sh, scatter=False)

    def body(*refs):
        q_ref, k_ref, v_ref, us_ref = refs[:4]
        sh_in = refs[4:4 + n_sh]
        o_ref, st_ref, js_ref = refs[4 + n_sh:7 + n_sh]
        sh_out = refs[7 + n_sh:7 + 2 * n_sh]
        acc_ref, r_ref = refs[7 + 2 * n_sh:9 + 2 * n_sh]
        sems = refs[9 + 2 * n_sh:]
        p, i = pl.program_id(0), pl.program_id(1)

        @pl.when((p == 0) & (i == 0))
        def _():
            ex.start(sh_in, sh_out, sems)

        hms = [_half_mask(h) for h in range(2)]
        qv = q_ref[...]
        qss = [jnp.where(hm, qv, jnp.zeros_like(qv)) * SCALE for hm in hms]
        row = lax.broadcasted_iota(I32, (B, B), 0)
        col = lax.broadcasted_iota(I32, (B, B), 1)
        tri = col < row
        acc_ref[...] = jnp.zeros_like(acc_ref)
        r_ref[...] = jnp.zeros_like(r_ref)

        def block(j, diag):
            off = pl.multiple_of(j * B, B)
            kj = k_ref[pl.ds(off, B), :]
            vj = v_ref[pl.ds(off, B), :]
            zs = [_dot_nt(qss[h], kj) for h in range(2)]
            sps = [_softplus_neg_abs(z) for z in zs]
            bs = [jnp.minimum(-z, 0.0) - sp for z, sp in zip(zs, sps)]
            if diag:
                bs = [jnp.where(tri, b, 0.0) for b in bs]
            lexcs = [_dot_acc(b, us_ref[...]) for b in bs]
            ws = [jnp.exp(jnp.minimum(z, 0.0) - sp + (_lane_tile(r_ref[h], B) + lexc))
                  for h, (z, sp, lexc) in enumerate(zip(zs, sps, lexcs))]
            if diag:
                ws = [jnp.where(tri, w, 0.0) for w in ws]
            for h in range(2):
                acc_ref[h] += _dot(ws[h].astype(BF16), vj)
                r_ref[h] += jnp.broadcast_to(lexcs[h][:, 0:1] + bs[h][:, 0:1], (B, LANES))

        def live():
            return (jnp.max(r_ref[...]) > -EXP_ZERO).astype(I32)

        block(i, True)

        def step(carry):
            j, _ = carry
            block(j, False)
            return j - 1, live()

        j_end, _ = lax.while_loop(lambda c: (c[0] >= 0) & (c[1] > 0), step, (i - 1, live()))
        js_ref[2 * p, i] = j_end + 1
        js_ref[2 * p + 1, i] = j_end + 1
        o_ref[...] = jnp.where(hms[0], acc_ref[0], acc_ref[1])
        st_ref[...] = r_ref[...]

        @pl.when((p == n_pair - 1) & (i == nq - 1))
        def _():
            ex.finish(sh_in, sh_out, sems)

    any_spec = pl.BlockSpec(memory_space=pl.ANY)
    return pl.pallas_call(
        body, name="sb_attn_fwd", grid=(n_pair, nq),
        in_specs=[pl.BlockSpec((B, LANES), lambda p, i: (i, p)),
                  pl.BlockSpec((S, LANES), lambda p, i: (0, n_pair + p)),
                  pl.BlockSpec((S, LANES), lambda p, i: (0, 2 * n_pair + p)),
                  _full((B, B))] + [any_spec] * n_sh,
        out_specs=[pl.BlockSpec((B, LANES), lambda p, i: (i, p)),
                   pl.BlockSpec((2, B, LANES), lambda p, i: (p, i, 0)),
                   pl.BlockSpec(memory_space=pltpu.SMEM)] + [any_spec] * n_sh,
        out_shape=[jax.ShapeDtypeStruct((S, n_pair * LANES), F32),
                   jax.ShapeDtypeStruct((2 * n_pair, S, LANES), F32),
                   jax.ShapeDtypeStruct((2 * n_pair, nq), I32)] + ex.out_shapes(shards),
        scratch_shapes=[pltpu.VMEM((2, B, LANES), F32), pltpu.VMEM((2, B, LANES), F32)] + ex.sem_shapes(),
        compiler_params=_cparams(("arbitrary", "arbitrary")),
    )(qkv, qkv, qkv, us, *shards)


def _sb_bwd(qkv, do, st, js, n_pair):
    S = qkv.shape[0]
    B = SB_BLOCK
    nq = S // B
    us = _tri(B, lambda r, c: r > c)
    ti = _tri(B, lambda r, c: r <= c)

    def body(js_ref, q_ref, k_ref, v_ref, do_ref, st_ref, us_ref, ti_ref, dq_ref, dk_ref, dv_ref,
             dqa_ref, pr_ref, er_ref):
        p, i = pl.program_id(0), pl.program_id(1)

        @pl.when(i == 0)
        def _():
            dk_ref[...] = jnp.zeros_like(dk_ref)
            dv_ref[...] = jnp.zeros_like(dv_ref)

        hms = [_half_mask(h) for h in range(2)]
        qv = q_ref[...]
        dov = do_ref[...]
        qss = [jnp.where(hm, qv, jnp.zeros_like(qv)) * SCALE for hm in hms]
        dobs = [jnp.where(hm, dov, 0.0).astype(BF16) for hm in hms]
        row = lax.broadcasted_iota(I32, (B, B), 0)
        col = lax.broadcasted_iota(I32, (B, B), 1)
        tri = col < row
        dqa_ref[...] = jnp.zeros_like(dqa_ref)
        pr_ref[...] = jnp.zeros_like(pr_ref)
        er_ref[...] = jnp.zeros_like(er_ref)

        def block(j, diag):
            off = pl.multiple_of(j * B, B)
            kj = k_ref[pl.ds(off, B), :]
            vj = v_ref[pl.ds(off, B), :]
            hs = range(2)
            zs = [_dot_nt(qss[h], kj) for h in hs]
            dws = [_dot_nt(dobs[h], vj) for h in hs]
            sps = [_softplus_neg_abs(z) for z in zs]
            bs = [jnp.minimum(-z, 0.0) - sp for z, sp in zip(zs, sps)]
            if diag:
                bs = [jnp.where(tri, b, 0.0) for b in bs]
            lexcs = [_dot_acc(b, us_ref[...]) for b in bs]
            ws = []
            for h in hs:
                pr_new = pr_ref[h] + jnp.broadcast_to(lexcs[h][:, 0:1] + bs[h][:, 0:1], (B, LANES))
                pr_ref[h] = pr_new
                w = jnp.exp(jnp.minimum(zs[h], 0.0) - sps[h] + (_lane_tile(st_ref[h] - pr_new, B) + lexcs[h]))
                ws.append(jnp.where(tri, w, 0.0) if diag else w)
            es = [dw * w for dw, w in zip(dws, ws)]
            eincs = [_dot_acc(e, ti_ref[...]) for e in es]
            dzbs = []
            for h in hs:
                er = er_ref[h]
                big_e = _lane_tile(er, B) + (eincs[h] - es[h])
                er_ref[h] = er + jnp.broadcast_to(eincs[h][:, B - 1:B], (B, LANES))
                eb = jnp.exp(bs[h])
                dzbs.append((es[h] * eb - big_e * (1.0 - eb)).astype(BF16))
            for h in hs:
                dqa_ref[h] += _dot(dzbs[h], kj)
            dk_ref[pl.ds(off, B), :] += _dot_tn(dzbs[0], qss[0]) + _dot_tn(dzbs[1], qss[1])
            dv_ref[pl.ds(off, B), :] += (_dot_tn(ws[0].astype(BF16), dobs[0])
                                         + _dot_tn(ws[1].astype(BF16), dobs[1]))

        def step(j, carry):
            block(j, False)
            return carry

        lax.fori_loop(js_ref[2 * p, i], i, step, 0)
        block(i, True)
        dq_ref[...] = jnp.where(hms[0], dqa_ref[0], dqa_ref[1]) * SCALE

    W = n_pair * LANES
    return pl.pallas_call(
        body, name="sb_attn_bwd",
        grid_spec=pltpu.PrefetchScalarGridSpec(
            num_scalar_prefetch=1, grid=(n_pair, nq),
            in_specs=[pl.BlockSpec((B, LANES), lambda p, i, js: (i, p)),
                      pl.BlockSpec((S, LANES), lambda p, i, js: (0, n_pair + p)),
                      pl.BlockSpec((S, LANES), lambda p, i, js: (0, 2 * n_pair + p)),
                      pl.BlockSpec((B, LANES), lambda p, i, js: (i, p)),
                      pl.BlockSpec((2, B, LANES), lambda p, i, js: (p, i, 0)),
                      pl.BlockSpec((B, B), lambda p, i, js: (0, 0)),
                      pl.BlockSpec((B, B), lambda p, i, js: (0, 0))],
            out_specs=[pl.BlockSpec((B, LANES), lambda p, i, js: (i, p)),
                       pl.BlockSpec((S, LANES), lambda p, i, js: (0, p)),
                       pl.BlockSpec((S, LANES), lambda p, i, js: (0, p))],
            scratch_shapes=[pltpu.VMEM((2, B, LANES), F32), pltpu.VMEM((2, B, LANES), F32),
                            pltpu.VMEM((2, B, LANES), F32)]),
        out_shape=[jax.ShapeDtypeStruct((S, W), F32)] * 3,
        compiler_params=_cparams(("arbitrary", "arbitrary")),
    )(js, qkv, qkv, qkv, do, st, us, ti)


def _head_column(blk, head):
    lane = lax.broadcasted_iota(I32, (1, LANES), 1)
    return jnp.sum(jnp.where(lane == head, blk, 0.0), axis=1, keepdims=True)


def _fox_fwd(qkv, c, c_rows, kmax, n_pair):
    S = qkv.shape[0]
    BQ, BK = FOX_BQ, FOX_BK
    R = BQ // BK
    nq = S // BQ

    def body(q_ref, k_ref, v_ref, c_ref, cr_ref, km_ref, o_ref, st_ref, js_ref, acc_ref, m_ref, cb_ref, qkb_ref):
        p, i, half = pl.program_id(0), pl.program_id(1), pl.program_id(2)
        hm = _half_mask(half)
        qv = q_ref[...]
        qs = jnp.where(hm, qv, jnp.zeros_like(qv)) * SCALE
        ccol = _head_column(c_ref[...], 2 * p + half)
        cb_ref[...] = jnp.broadcast_to(ccol, (BQ, BK))
        qf = qs.astype(F32)
        qkb_ref[...] = jnp.broadcast_to(
            jnp.sqrt(jnp.sum(qf * qf, axis=1, keepdims=True)) * NORM_SLACK
            * _head_column(km_ref[...], 2 * p + half) + ccol, (BQ, LANES))
        row = lax.broadcasted_iota(I32, (BQ, BK), 0)
        col = lax.broadcasted_iota(I32, (BQ, BK), 1)
        acc_ref[...] = jnp.zeros_like(acc_ref)
        m_ref[...] = jnp.full_like(m_ref, NEG_BIG)

        def blocks(j_top, diag):
            ss, v1s, keeps = [], [], []
            for d in range(R):
                j = j_top - d
                off = pl.multiple_of(j * BK, BK)
                vj = v_ref[pl.ds(off, BK), :]
                v1s.append(jnp.where(hm, vj, jnp.ones_like(vj)))
                s = _dot_nt(qs, k_ref[pl.ds(off, BK), :]) + (cb_ref[...] - cr_ref[0, pl.ds(j, 1), :])
                if diag:
                    keeps.append(col + (R - 1 - d) * BK <= row)
                    s = jnp.where(keeps[-1], s, NEG_BIG)
                ss.append(s)
            m_old = m_ref[...]
            s_max = jnp.max(functools.reduce(jnp.maximum, ss), axis=1, keepdims=True)
            m_new = jnp.maximum(m_old, jnp.broadcast_to(s_max, (BQ, LANES)))
            m_wide = _lane_tile(m_new, BK)
            pvs = [jnp.exp(s - m_wide) for s in ss]
            if diag:
                pvs = [jnp.where(keep, pv, 0.0) for keep, pv in zip(keeps, pvs)]
            new = _dot(pvs[0].astype(BF16), v1s[0])
            for pv, v1 in zip(pvs[1:], v1s[1:]):
                new = new + _dot(pv.astype(BF16), v1)
            acc_ref[...] = jnp.exp(m_old - m_new) * acc_ref[...] + new
            m_ref[...] = m_new

        def live(j):
            c_end = cr_ref[0, pl.ds(jnp.maximum(j, 0), 1), :][:, BK - 1:BK]
            return (jnp.max(qkb_ref[...] - c_end - m_ref[...]) > -EXP_ZERO).astype(I32)

        blocks(R * i + R - 1, True)

        def step(carry):
            j, _ = carry
            go_on = live(j - R)
            blocks(j, False)
            return j - R, go_on

        j_end, _ = lax.while_loop(lambda cr: (cr[0] >= 0) & (cr[1] > 0), step, (R * i - 1, live(R * i - 1)))
        js_ref[2 * p + half, i] = j_end + 1
        acc = acc_ref[...]
        denom = jnp.where(hm, pltpu.roll(acc, HEAD_DIM, 1), acc)
        res = jnp.where(hm, acc / denom, 0.0)

        @pl.when(half == 0)
        def _():
            o_ref[...] = res

        @pl.when(half == 1)
        def _():
            o_ref[...] += res

        st_ref[0] = m_ref[...] + jnp.log(denom)

    return pl.pallas_call(
        body, name="fox_attn_fwd", grid=(n_pair, nq, 2),
        in_specs=[pl.BlockSpec((BQ, LANES), lambda p, i, h: (i, 3 * n_pair + p)),
                  pl.BlockSpec((S, LANES), lambda p, i, h: (0, 4 * n_pair + p)),
                  pl.BlockSpec((S, LANES), lambda p, i, h: (0, 5 * n_pair + p)),
                  pl.BlockSpec((BQ, LANES), lambda p, i, h: (i, 0)),
                  pl.BlockSpec((1, S // BK, BK), lambda p, i, h: (2 * p + h, 0, 0)),
                  _full((1, LANES))],
        out_specs=[pl.BlockSpec((BQ, LANES), lambda p, i, h: (i, p)),
                   pl.BlockSpec((1, BQ, LANES), lambda p, i, h: (2 * p + h, i, 0)),
                   pl.BlockSpec(memory_space=pltpu.SMEM)],
        out_shape=[jax.ShapeDtypeStruct((S, n_pair * LANES), F32),
                   jax.ShapeDtypeStruct((2 * n_pair, S, LANES), F32),
                   jax.ShapeDtypeStruct((2 * n_pair, nq), I32)],
        scratch_shapes=[pltpu.VMEM((BQ, LANES), F32), pltpu.VMEM((BQ, LANES), F32), pltpu.VMEM((BQ, BK), F32),
                        pltpu.VMEM((BQ, LANES), F32)],
        compiler_params=_cparams(("arbitrary", "arbitrary", "arbitrary")),
    )(qkv, qkv, qkv, c, c_rows, kmax)


def _fox_bwd(qkv, do, o, st, c, c_rows, js, n_pair):
    S = qkv.shape[0]
    B = FOX_BLOCK
    nq = S // B

    def body(js_ref, q_ref, k_ref, v_ref, do_ref, o_ref, st_ref, c_ref, cr_ref, dq_ref, dk_ref, dv_ref,
             dc_ref, dqa_ref, rs_ref, cb_ref, db_ref):
        p, i, half = pl.program_id(0), pl.program_id(1), pl.program_id(2)

        @pl.when((i == 0) & (half == 0))
        def _():
            dk_ref[...] = jnp.zeros_like(dk_ref)
            dv_ref[...] = jnp.zeros_like(dv_ref)
            dc_ref[...] = jnp.zeros_like(dc_ref)

        hm = _half_mask(half)
        qv = q_ref[...]
        qs = jnp.where(hm, qv, jnp.zeros_like(qv)) * SCALE
        dov = jnp.where(hm, do_ref[...], 0.0)
        dob = dov.astype(BF16)
        cb_ref[...] = jnp.broadcast_to(_head_column(c_ref[...], 2 * p + half), (B, LANES)) - st_ref[0]
        db_ref[...] = jnp.broadcast_to(jnp.sum(dov * o_ref[...], axis=1, keepdims=True), (B, LANES))
        row = lax.broadcasted_iota(I32, (B, B), 0)
        col = lax.broadcasted_iota(I32, (B, B), 1)
        dqa_ref[...] = jnp.zeros_like(dqa_ref)
        rs_ref[...] = jnp.zeros_like(rs_ref)

        def block(j, diag):
            off = pl.multiple_of(j * B, B)
            kj = k_ref[pl.ds(off, B), :]
            vj = v_ref[pl.ds(off, B), :]
            pv = jnp.exp(_dot_nt(qs, kj) + (_lane_tile(cb_ref[...], B) - cr_ref[0, pl.ds(j, 1), :]))
            if diag:
                pv = jnp.where(col <= row, pv, 0.0)
            ds = pv * (_dot_nt(dob, vj) - _lane_tile(db_ref[...], B))
            dsb = ds.astype(BF16)
            dqa_ref[...] += _dot(dsb, kj)
            dk_ref[pl.ds(off, B), :] += _dot_tn(dsb, qs)
            dv_ref[pl.ds(off, B), :] += _dot_tn(pv.astype(BF16), dob)
            dc_ref[0, half, pl.ds(j, 1), :] -= jnp.sum(ds, axis=0, keepdims=True)
            rs_ref[...] += jnp.sum(ds, axis=1, keepdims=True)

        def step(j, carry):
            block(j, False)
            return carry

        lax.fori_loop(js_ref[2 * p + half, i], i, step, 0)
        block(i, True)
        res = jnp.where(hm, dqa_ref[...] * SCALE, 0.0)

        @pl.when(half == 0)
        def _():
            dq_ref[...] = res

        @pl.when(half == 1)
        def _():
            dq_ref[...] += res

        dc_ref[0, half, pl.ds(i, 1), :] += jnp.transpose(jnp.broadcast_to(rs_ref[...], (B, LANES)))[0:1, :]

    W = n_pair * LANES
    return pl.pallas_call(
        body, name="fox_attn_bwd",
        grid_spec=pltpu.PrefetchScalarGridSpec(
            num_scalar_prefetch=1, grid=(n_pair, nq, 2),
            in_specs=[pl.BlockSpec((B, LANES), lambda p, i, h, js: (i, 3 * n_pair + p)),
                      pl.BlockSpec((S, LANES), lambda p, i, h, js: (0, 4 * n_pair + p)),
                      pl.BlockSpec((S, LANES), lambda p, i, h, js: (0, 5 * n_pair + p)),
                      pl.BlockSpec((B, LANES), lambda p, i, h, js: (i, p)),
                      pl.BlockSpec((B, LANES), lambda p, i, h, js: (i, p)),
                      pl.BlockSpec((1, B, LANES), lambda p, i, h, js: (2 * p + h, i, 0)),
                      pl.BlockSpec((B, LANES), lambda p, i, h, js: (i, 0)),
                      pl.BlockSpec((1, nq, B), lambda p, i, h, js: (2 * p + h, 0, 0))],
            out_specs=[pl.BlockSpec((B, LANES), lambda p, i, h, js: (i, p)),
                       pl.BlockSpec((S, LANES), lambda p, i, h, js: (0, p)),
                       pl.BlockSpec((S, LANES), lambda p, i, h, js: (0, p)),
                       pl.BlockSpec((1, 2, nq, B), lambda p, i, h, js: (p, 0, 0, 0))],
            scratch_shapes=[pltpu.VMEM((B, LANES), F32), pltpu.VMEM((B, 1), F32), pltpu.VMEM((B, LANES), F32),
                            pltpu.VMEM((B, LANES), F32)]),
        out_shape=[jax.ShapeDtypeStruct((S, W), F32)] * 3 + [jax.ShapeDtypeStruct((n_pair, 2, nq, B), F32)],
        compiler_params=_cparams(("arbitrary", "arbitrary", "arbitrary")),
    )(js, qkv, qkv, qkv, do, o, st, c, c_rows)


def _adam(w, g, m, v):
    m = ADAM_B1 * m + (1.0 - ADAM_B1) * g
    v = ADAM_B2 * v + (1.0 - ADAM_B2) * (g * g)
    m_hat = m / (1.0 - ADAM_B1 ** ADAM_STEP)
    v_hat = v / (1.0 - ADAM_B2 ** ADAM_STEP)
    delta = -ADAM_LR * (m_hat / (jnp.sqrt(v_hat) + ADAM_EPS) + ADAM_WD * w)
    return delta, m, v


def _reduce_adam(landing, w, m, v, name):
    R, C = w.shape
    TR = next(t for t in (256, 128, R) if R % t == 0)

    def body(l_ref, w_ref, m_ref, v_ref, g_ref, d_ref, nm_ref, nv_ref):
        g = l_ref[0].astype(F32)
        for s in range(1, N_DEV):
            g = g + l_ref[s].astype(F32)
        d, nm, nv = _adam(w_ref[...], g, m_ref[...], v_ref[...])
        g_ref[...] = g
        d_ref[...] = d
        nm_ref[...] = nm
        nv_ref[...] = nv

    blk = pl.BlockSpec((TR, C), lambda i: (i, 0))
    return pl.pallas_call(
        body, name=name, grid=(R // TR,),
        in_specs=[pl.BlockSpec((N_DEV, TR, C), lambda i: (0, i, 0)), blk, blk, blk],
        out_specs=[blk] * 4,
        out_shape=[jax.ShapeDtypeStruct((R, C), F32)] * 4,
        compiler_params=_cparams(("arbitrary",)),
    )(landing, w, m, v)


def _reduce_adam_small(landing, w, m, v):
    R, C = w.shape

    def body(l_ref, w_ref, m_ref, v_ref, g_ref, d_ref, nm_ref, nv_ref, loss_ref):
        g = l_ref[0]
        for s in range(1, N_DEV):
            g = g + l_ref[s]
        d, nm, nv = _adam(w_ref[...], g, m_ref[...], v_ref[...])
        g_ref[...] = g
        d_ref[...] = d
        nm_ref[...] = nm
        nv_ref[...] = nv
        loss_ref[...] = jnp.broadcast_to(0.5 * jnp.sum(g[7:8, :], axis=1, keepdims=True), (1, LANES))

    return pl.pallas_call(
        body, name="reduce_adam_small",
        out_shape=[jax.ShapeDtypeStruct((R, C), F32)] * 4 + [jax.ShapeDtypeStruct((1, LANES), F32)],
    )(landing, w, m, v)


def _pad_lanes(a, width):
    return jnp.pad(a, ((0, 0), (0, width - a.shape[1])))


def _pack_small(D, n_fox, g_cat, l1g, l1b, l2g, l2b, bf, last):
    return jnp.concatenate([g_cat, l1g, l1b, l2g, l2b, _pad_lanes(bf, D), jnp.zeros((1, D), F32), last], axis=0)


def kernel(x, w_in, b_f, g_sb, g_fox, w_out, ln1_g, ln1_b, ln2_g, ln2_b, w_gate_up, w_down, loss_target, m_w_in, m_b_f, m_g_sb, m_g_fox, m_w_out, m_ln1_g, m_ln1_b, m_ln2_g, m_ln2_b, m_w_gate_up, m_w_down, v_w_in, v_b_f, v_g_sb, v_g_fox, v_w_out, v_ln1_g, v_ln1_b, v_ln2_g, v_ln2_b, v_w_gate_up, v_w_down):
    x2, tgt = x[0], loss_target[0]
    S, D = x2.shape
    W = D // 2
    n_pair = W // LANES
    n_fox = W // HEAD_DIM
    F = w_down.shape[1] * N_DEV

    (g_in,) = _exchange_call([w_in[0].astype(BF16)], False, "w_in_all_gather")
    w_in_full = g_in.transpose(1, 0, 2).reshape(D, -1)
    w_qkv = w_in_full[:, :6 * W]
    w_f = _pad_lanes(w_in_full[:, 6 * W:], LANES)
    gmat = _tri(D, lambda r, c: (r // HEAD_DIM) == (c // HEAD_DIM))
    g_cat = jnp.concatenate([g_sb, g_fox], axis=1)

    qkv, u, c, ksq = _proj_fwd(x2, w_qkv, w_f, _pad_lanes(b_f, LANES), n_fox)
    c_t = c[:, :n_fox].T
    c_rows = c_t.reshape(n_fox, S // FOX_BLOCK, FOX_BLOCK)
    kmax = jnp.sqrt(ksq[0:1]) * NORM_SLACK
    o_sb, st_sb, js_sb, g_out, g_gu, g_dn = _sb_fwd(
        qkv, n_pair, [w_out[0].astype(BF16), w_gate_up[0].astype(BF16), w_down[0].astype(BF16)])
    w_o = g_out.reshape(D, D)
    w_gu = g_gu.transpose(1, 0, 2).reshape(D, 2 * F)
    w_dn = g_dn.reshape(F, D)
    o_fx, st_fx, js_fx = _fox_fwd(qkv, c, c_t.reshape(n_fox, S // FOX_BK, FOX_BK), kmax, n_pair)
    js_fx = js_fx // (FOX_BLOCK // FOX_BK)
    h1, xh1, rs1, on_b, rr = _post_attn_fwd(o_sb, o_fx, x2, g_cat, gmat, w_o, ln1_g, ln1_b)
    gu, act_b, dyp, sm2 = _mlp_fwd(h1, tgt, w_gu, w_dn, ln2_g, ln2_b)

    dgu_b, dh1 = _mlp_bwd(gu, dyp, w_gu.T, w_dn.T)
    gw_gu = _matmul_tn(h1, [dgu_b], "grad_w_gate_up", n_split=2)
    gw_dn = _matmul_tn(act_b, [dyp], "grad_w_down")
    dxa, dmix_b, do_sb, do_fx, sm1, l_gu, l_dn = _post_attn_bwd(
        dh1, xh1, rs1, ln1_g, o_sb, o_fx, rr, g_cat, gmat, w_o.T,
        [gw_gu.astype(BF16).reshape(D, N_DEV, -1).transpose(1, 0, 2),
         gw_dn.astype(BF16).reshape(N_DEV, F // N_DEV, D)])
    dq_sb, dk_sb, dv_sb = _sb_bwd(qkv, do_sb, st_sb, js_sb, n_pair)
    dq_fx, dk_fx, dv_fx, dcr = _fox_bwd(qkv, do_fx, o_fx, st_fx, c, c_rows, js_fx, n_pair)
    dc = _pad_lanes(dcr.reshape(n_fox, S).T, LANES)
    pieces = [dq_sb, dk_sb, dv_sb, dq_fx, dk_fx, dv_fx]
    dx, df_b, sm0 = _proj_bwd(dxa, pieces, dc, u, w_qkv.T, w_f.T, n_fox)

    gw_in = _matmul_tn(x2, pieces + [df_b], "grad_w_in")[:, :6 * W + n_fox]
    gw_out = _matmul_tn(on_b, [dmix_b], "grad_w_out")

    small = _pack_small(D, n_fox, sm1[2:3], sm1[0:1], sm1[1:2], sm2[0:1], sm2[1:2], sm0[0:1, :n_fox],
                        sm2[2:3] * (1.0 / D))
    l_in, l_out, l_small = _exchange_call(
        [gw_in.astype(BF16).reshape(D, N_DEV, -1).transpose(1, 0, 2),
         gw_out.astype(BF16).reshape(N_DEV, D // N_DEV, D),
         jnp.broadcast_to(small[None], (N_DEV,) + small.shape)], True, "grad_exchange")

    r_in = _reduce_adam(l_in, w_in[0], m_w_in[0], v_w_in[0], "reduce_adam_w_in")
    r_out = _reduce_adam(l_out, w_out[0], m_w_out[0], v_w_out[0], "reduce_adam_w_out")
    r_gu = _reduce_adam(l_gu, w_gate_up[0], m_w_gate_up[0], v_w_gate_up[0], "reduce_adam_w_gate_up")
    r_dn = _reduce_adam(l_dn, w_down[0], m_w_down[0], v_w_down[0], "reduce_adam_w_down")
    zero = jnp.zeros((1, D), F32)
    pack = lambda gc, a, b_, c_, d_, bf: _pack_small(D, n_fox, gc, a, b_, c_, d_, bf, zero)
    r_small = _reduce_adam_small(
        l_small,
        pack(g_cat, ln1_g, ln1_b, ln2_g, ln2_b, b_f),
        pack(jnp.concatenate([m_g_sb, m_g_fox], axis=1), m_ln1_g, m_ln1_b, m_ln2_g, m_ln2_b, m_b_f),
        pack(jnp.concatenate([v_g_sb, v_g_fox], axis=1), v_ln1_g, v_ln1_b, v_ln2_g, v_ln2_b, v_b_f))
    loss = r_small[4][0, 0]

    def unpack(kind):
        big = [r_in[kind][None], None, None, None, r_out[kind][None], None, None, None, None,
               r_gu[kind][None], r_dn[kind][None]]
        s = r_small[kind]
        big[1] = s[5:6, :n_fox]
        big[2] = s[0:1, :W]
        big[3] = s[0:1, W:]
        big[5], big[6], big[7], big[8] = s[1:2], s[2:3], s[3:4], s[4:5]
        return big

    return (loss, dx[None], *unpack(0), *unpack(1), *unpack(2), *unpack(3))
```

```python
import functools

import jax
import jax.numpy as jnp
from jax import lax
from jax.experimental import pallas as pl
from jax.experimental.pallas import tpu as pltpu

F32 = jnp.float32
BF16 = jnp.bfloat16
I32 = jnp.int32

N_DEV = 8
HEAD_DIM = 64
LANES = 128
SCALE = HEAD_DIM ** -0.5
ALPHA = 2.0 ** 0.25
LN_EPS = 1e-5
RMS_EPS = 1e-6
ADAM_LR, ADAM_B1, ADAM_B2, ADAM_EPS, ADAM_WD, ADAM_STEP = 0.001, 0.9, 0.999, 1e-08, 0.01, 10
NEG_BIG = -1e30
NORM_SLACK = 1.01
EXP_ZERO = 106.0
VMEM_LIMIT = 60 * 1024 * 1024
ROW_TILE = 256
SB_BLOCK = 256
FOX_BLOCK = 512
FOX_BQ, FOX_BK = 512, 256
assert FOX_BQ == FOX_BLOCK and FOX_BLOCK % FOX_BK == 0
MESH = pl.DeviceIdType.MESH


def _cparams(sem):
    return pltpu.CompilerParams(dimension_semantics=sem, vmem_limit_bytes=VMEM_LIMIT)


def _dot(a, b):
    return jnp.dot(a, b, preferred_element_type=F32)


def _dot_nt(a, b):
    return lax.dot_general(a, b, (((1,), (1,)), ((), ())), preferred_element_type=F32)


def _dot_tn(a, b):
    return lax.dot_general(a, b, (((0,), (0,)), ((), ())), preferred_element_type=F32)


def _split2(a):
    hi = a.astype(BF16)
    lo = (a - hi.astype(F32)).astype(BF16)
    return hi, lo


def _split3(a):
    hi = a.astype(BF16)
    r1 = a - hi.astype(F32)
    mid = r1.astype(BF16)
    lo = (r1 - mid.astype(F32)).astype(BF16)
    return hi, mid, lo


def _dot_acc(a, m):
    hi, lo = _split2(a)
    return _dot(hi, m) + _dot(lo, m)


def _tri(n, fn):
    r = lax.broadcasted_iota(I32, (n, n), 0)
    c = lax.broadcasted_iota(I32, (n, n), 1)
    return fn(r, c).astype(BF16)


def _full(shape):
    nd = len(shape)
    return pl.BlockSpec(shape, lambda *_: (0,) * nd)


def _peer(k):
    x, y, c = lax.axis_index("x"), lax.axis_index("y"), lax.axis_index("c")
    return (1 - x if k & 4 else x, 1 - y if k & 2 else y, 1 - c if k & 1 else c)


def _my_index():
    return 4 * lax.axis_index("x") + 2 * lax.axis_index("y") + lax.axis_index("c")


class _Exchange:
    def __init__(self, n, scatter):
        self.n, self.scatter = n, scatter

    def sem_shapes(self):
        return [pltpu.SemaphoreType.DMA(((N_DEV - 1) * self.n,)), pltpu.SemaphoreType.DMA(((N_DEV - 1) * self.n,)),
                pltpu.SemaphoreType.DMA((self.n,))]

    def out_shapes(self, arrays):
        if self.scatter:
            return [jax.ShapeDtypeStruct(s.shape, s.dtype) for s in arrays]
        return [jax.ShapeDtypeStruct((N_DEV,) + s.shape, s.dtype) for s in arrays]

    def _copies(self, ins, outs, sems, landing):
        send_sems, recv_sems, local_sems = sems
        me = _my_index()
        src = lambda a, d: ins[a].at[d] if self.scatter else ins[a]
        local = [pltpu.make_async_copy(src(a, me), outs[a].at[me], local_sems.at[a]) for a in range(self.n)]
        remote = [pltpu.make_async_remote_copy(
            src_ref=src(a, me ^ k), dst_ref=outs[a].at[me ^ k if landing else me],
            send_sem=send_sems.at[(k - 1) * self.n + a], recv_sem=recv_sems.at[(k - 1) * self.n + a],
            device_id=_peer(k), device_id_type=MESH) for k in range(1, N_DEV) for a in range(self.n)]
        return local, remote

    def start(self, ins, outs, sems):
        local, sent = self._copies(ins, outs, sems, landing=False)
        for cp in local + sent:
            cp.start()

    def finish(self, ins, outs, sems):
        local, landed = self._copies(ins, outs, sems, landing=True)
        for cp in landed:
            cp.wait_recv()
        for cp in landed:
            cp.wait_send()
        for cp in local:
            cp.wait()


def _exchange_call(arrays, scatter, name):
    n = len(arrays)
    ex = _Exchange(n, scatter)

    def body(*refs):
        ins, outs, sems = refs[:n], refs[n:2 * n], refs[2 * n:]
        ex.start(ins, outs, sems)
        ex.finish(ins, outs, sems)

    any_spec = pl.BlockSpec(memory_space=pl.ANY)
    return pl.pallas_call(
        body, name=name, out_shape=ex.out_shapes(arrays),
        in_specs=[any_spec] * n, out_specs=[any_spec] * n, scratch_shapes=ex.sem_shapes(),
    )(*arrays)


def _log_sigmoid(u):
    return jnp.minimum(u, 0.0) - jnp.log1p(jnp.exp(-jnp.abs(u)))


def _proj_fwd(x, w_qkv, w_f, bf_pad, n_fox):
    S, D = x.shape
    N = w_qkv.shape[1]
    W = D // 2
    TM = ROW_TILE
    tri = _tri(TM, lambda r, c: c <= r)
    r_ = lax.broadcasted_iota(I32, (W, LANES), 0)
    c_ = lax.broadcasted_iota(I32, (W, LANES), 1)
    head_of = (r_ // HEAD_DIM == c_).astype(BF16)

    def body(x_ref, wq_ref, wf_ref, bf_ref, tri_ref, ho_ref, qkv_ref, u_ref, c_ref, ksq_ref, run_ref):
        @pl.when(pl.program_id(0) == 0)
        def _():
            run_ref[...] = jnp.zeros_like(run_ref)
            ksq_ref[...] = jnp.zeros_like(ksq_ref)

        xb = x_ref[...].astype(BF16)
        for n0 in range(0, N, D):
            chunk = _dot(xb, wq_ref[:, n0:n0 + D]).astype(BF16)
            qkv_ref[:, n0:n0 + D] = chunk
            if n0 == 4 * W:
                kf = chunk[:, :W].astype(F32)
                ksq = jnp.max(_dot_acc(kf * kf, ho_ref[...]), axis=0, keepdims=True)
                ksq_ref[...] = jnp.maximum(ksq_ref[...], ksq)
        u = _dot(xb, wf_ref[...]) + bf_ref[...]
        lane = lax.broadcasted_iota(I32, u.shape, 1)
        logf = jnp.where(lane < n_fox, _log_sigmoid(u), 0.0)
        u_ref[...] = u
        hi, mid, lo = _split3(logf)
        t = tri_ref[...]
        cs = _dot(t, hi) + _dot(t, mid) + _dot(t, lo) + run_ref[...]
        c_ref[...] = cs
        run_ref[...] = cs[TM - 1:TM, :]

    return pl.pallas_call(
        body, name="proj_fwd", grid=(S // TM,),
        in_specs=[pl.BlockSpec((TM, D), lambda i: (i, 0)), _full(w_qkv.shape), _full(w_f.shape),
                  _full(bf_pad.shape), _full(tri.shape), _full(head_of.shape)],
        out_specs=[pl.BlockSpec((TM, N), lambda i: (i, 0)), pl.BlockSpec((TM, LANES), lambda i: (i, 0)),
                   pl.BlockSpec((TM, LANES), lambda i: (i, 0)), _full((8, LANES))],
        out_shape=[jax.ShapeDtypeStruct((S, N), BF16), jax.ShapeDtypeStruct((S, LANES), F32),
                   jax.ShapeDtypeStruct((S, LANES), F32), jax.ShapeDtypeStruct((8, LANES), F32)],
        scratch_shapes=[pltpu.VMEM((1, LANES), F32)],
        compiler_params=_cparams(("arbitrary",)),
    )(x, w_qkv, w_f, bf_pad, tri, head_of)


def _post_attn_fwd(o_sb, o_fx, x, g_cat, gmat, w_out, ln_g, ln_b):
    S, D = x.shape
    H = D // 2
    TM = ROW_TILE

    def body(osb_ref, ofx_ref, x_ref, g_ref, gm_ref, wo_ref, lg_ref, lb_ref,
             h1_ref, xh_ref, rs_ref, on_ref, rr_ref):
        o = jnp.concatenate([osb_ref[...], ofx_ref[...]], axis=1)
        ms = _dot_acc(o * o, gm_ref[...]) * (1.0 / HEAD_DIM)
        r = lax.rsqrt(ms + RMS_EPS)
        onb = (o * r * g_ref[...]).astype(BF16)
        hp = ALPHA * x_ref[...] + _dot(onb, wo_ref[...])
        mu = jnp.mean(hp, axis=-1, keepdims=True)
        d = hp - mu
        rstd = lax.rsqrt(jnp.mean(d * d, axis=-1, keepdims=True) + LN_EPS)
        xh = d * rstd
        h1_ref[...] = xh * lg_ref[...] + lb_ref[...]
        xh_ref[...] = xh
        rs_ref[...] = jnp.broadcast_to(rstd, (TM, LANES))
        on_ref[...] = onb
        rr_ref[...] = r

    row = lambda w: pl.BlockSpec((TM, w), lambda i: (i, 0))
    return pl.pallas_call(
        body, name="post_attn_fwd", grid=(S // TM,),
        in_specs=[row(H), row(H), row(D), _full((1, D)), _full((D, D)), _full((D, D)), _full((1, D)), _full((1, D))],
        out_specs=[row(D), row(D), row(LANES), row(D), row(D)],
        out_shape=[jax.ShapeDtypeStruct((S, D), F32), jax.ShapeDtypeStruct((S, D), F32),
                   jax.ShapeDtypeStruct((S, LANES), F32), jax.ShapeDtypeStruct((S, D), BF16),
                   jax.ShapeDtypeStruct((S, D), F32)],
        compiler_params=_cparams(("arbitrary",)),
    )(o_sb, o_fx, x, g_cat, gmat, w_out, ln_g, ln_b)


def _ln_bwd(dxh, xh, rstd):
    m1 = jnp.mean(dxh, axis=-1, keepdims=True)
    m2 = jnp.mean(dxh * xh, axis=-1, keepdims=True)
    return rstd * (dxh - m1 - xh * m2)


def _mlp_fwd(h1, target, w_gu, w_dn, ln_g, ln_b):
    S, D = h1.shape
    F = w_dn.shape[0]
    TM = ROW_TILE
    FC = F // 2

    def body(h1_ref, tg_ref, wgu_hbm, wdn_hbm, lg_ref, lb_ref, gu_ref, act_ref, dyp_ref, sm_ref, wgu, wdn):
        @pl.when(pl.program_id(0) == 0)
        def _():
            pltpu.sync_copy(wgu_hbm, wgu)
            pltpu.sync_copy(wdn_hbm, wdn)
            sm_ref[...] = jnp.zeros_like(sm_ref)

        h1v = h1_ref[...]
        hb = h1v.astype(BF16)
        ff = jnp.zeros((TM, D), F32)
        for c0 in range(0, F, FC):
            g = _dot(hb, wgu[:, c0:c0 + FC])
            u = _dot(hb, wgu[:, F + c0:F + c0 + FC])
            gu_ref[:, c0:c0 + FC] = g.astype(BF16)
            gu_ref[:, F + c0:F + c0 + FC] = u.astype(BF16)
            ab = ((g * jax.nn.sigmoid(g)) * u).astype(BF16)
            act_ref[:, c0:c0 + FC] = ab
            ff = ff + _dot(ab, wdn[c0:c0 + FC, :])
        yp = ALPHA * h1v + ff
        mu = jnp.mean(yp, axis=-1, keepdims=True)
        d = yp - mu
        rstd = lax.rsqrt(jnp.mean(d * d, axis=-1, keepdims=True) + LN_EPS)
        xh = d * rstd
        err = (xh * lg_ref[...] + lb_ref[...]) - tg_ref[...]
        dy = err * (1.0 / D)
        sm_ref[0:1, :] += jnp.sum(dy * xh, axis=0, keepdims=True)
        sm_ref[1:2, :] += jnp.sum(dy, axis=0, keepdims=True)
        sm_ref[2:3, :] += jnp.sum(err * err, axis=0, keepdims=True)
        dyp_ref[...] = _ln_bwd(dy * lg_ref[...], xh, rstd)

    row = lambda w: pl.BlockSpec((TM, w), lambda i: (i, 0))
    hbm = pl.BlockSpec(memory_space=pl.ANY)
    return pl.pallas_call(
        body, name="mlp_fwd", grid=(S // TM,),
        in_specs=[row(D), row(D), hbm, hbm, _full((1, D)), _full((1, D))],
        out_specs=[row(2 * F), row(F), row(D), _full((8, D))],
        out_shape=[jax.ShapeDtypeStruct((S, 2 * F), BF16), jax.ShapeDtypeStruct((S, F), BF16),
                   jax.ShapeDtypeStruct((S, D), F32), jax.ShapeDtypeStruct((8, D), F32)],
        scratch_shapes=[pltpu.VMEM(w_gu.shape, BF16), pltpu.VMEM(w_dn.shape, BF16)],
        compiler_params=_cparams(("arbitrary",)),
    )(h1, target, w_gu, w_dn, ln_g, ln_b)


def _mlp_bwd(gu, dyp, w_guT, w_dnT):
    S, D = dyp.shape
    F = w_dnT.shape[1]
    TM = ROW_TILE
    FC = F // 2

    def body(gu_ref, dyp_ref, wguT_hbm, wdnT_hbm, dgu_ref, dh1_ref, wguT, wdnT):
        @pl.when(pl.program_id(0) == 0)
        def _():
            pltpu.sync_copy(wguT_hbm, wguT)
            pltpu.sync_copy(wdnT_hbm, wdnT)

        dypv = dyp_ref[...]
        db = dypv.astype(BF16)
        dh1 = ALPHA * dypv
        for c0 in range(0, F, FC):
            dact = _dot(db, wdnT[:, c0:c0 + FC])
            g = gu_ref[:, c0:c0 + FC].astype(F32)
            u = gu_ref[:, F + c0:F + c0 + FC].astype(F32)
            sg = jax.nn.sigmoid(g)
            dgb = (dact * u * (sg * (1.0 + g * (1.0 - sg)))).astype(BF16)
            dub = (dact * (g * sg)).astype(BF16)
            dgu_ref[:, c0:c0 + FC] = dgb
            dgu_ref[:, F + c0:F + c0 + FC] = dub
            dh1 = dh1 + _dot(dgb, wguT[c0:c0 + FC, :]) + _dot(dub, wguT[F + c0:F + c0 + FC, :])
        dh1_ref[...] = dh1

    row = lambda w: pl.BlockSpec((TM, w), lambda i: (i, 0))
    hbm = pl.BlockSpec(memory_space=pl.ANY)
    return pl.pallas_call(
        body, name="mlp_bwd", grid=(S // TM,),
        in_specs=[row(2 * F), row(D), hbm, hbm],
        out_specs=[row(2 * F), row(D)],
        out_shape=[jax.ShapeDtypeStruct((S, 2 * F), BF16), jax.ShapeDtypeStruct((S, D), F32)],
        scratch_shapes=[pltpu.VMEM(w_guT.shape, BF16), pltpu.VMEM(w_dnT.shape, BF16)],
        compiler_params=_cparams(("arbitrary",)),
    )(gu, dyp, w_guT, w_dnT)


def _post_attn_bwd(dh1, xh, rs, ln_g, o_sb, o_fx, rr, g_cat, gmat, w_outT, chunked):
    S, D = dh1.shape
    H = D // 2
    TM = ROW_TILE
    nT = S // TM
    n_ch = len(chunked)
    ex = _Exchange(n_ch, scatter=True)

    def body(*refs):
        dh1_ref, xh_ref, rs_ref, lg_ref, osb_ref, ofx_ref, rr_ref, g_ref, gm_ref, woT_ref = refs[:10]
        ch_in = refs[10:10 + n_ch]
        dxa_ref, dmix_ref, dosb_ref, dofx_ref, sm_ref = refs[10 + n_ch:15 + n_ch]
        ch_out = refs[15 + n_ch:15 + 2 * n_ch]
        sems = refs[15 + 2 * n_ch:]

        @pl.when(pl.program_id(0) == 0)
        def _():
            ex.start(ch_in, ch_out, sems)
            sm_ref[...] = jnp.zeros_like(sm_ref)

        dh = dh1_ref[...]
        xhv = xh_ref[...]
        dhp = _ln_bwd(dh * lg_ref[...], xhv, rs_ref[:, 0:1])
        dxa_ref[...] = ALPHA * dhp
        dmb = dhp.astype(BF16)
        dmix_ref[...] = dmb
        don = _dot(dmb, woT_ref[...])
        o = jnp.concatenate([osb_ref[...], ofx_ref[...]], axis=1)
        r = rr_ref[...]
        u = don * g_ref[...]
        t = _dot_acc(u * o, gm_ref[...]) * (1.0 / HEAD_DIM)
        do = r * u - o * (r * r * r) * t
        dosb_ref[...] = do[:, :H]
        dofx_ref[...] = do[:, H:]
        sm_ref[0:1, :] += jnp.sum(dh * xhv, axis=0, keepdims=True)
        sm_ref[1:2, :] += jnp.sum(dh, axis=0, keepdims=True)
        sm_ref[2:3, :] += jnp.sum(don * o * r, axis=0, keepdims=True)

        @pl.when(pl.program_id(0) == nT - 1)
        def _():
            ex.finish(ch_in, ch_out, sems)

    row = lambda w: pl.BlockSpec((TM, w), lambda i: (i, 0))
    any_spec = pl.BlockSpec(memory_space=pl.ANY)
    return pl.pallas_call(
        body, name="post_attn_bwd", grid=(nT,),
        in_specs=[row(D), row(D), row(LANES), _full((1, D)), row(H), row(H), row(D), _full((1, D)),
                  _full((D, D)), _full((D, D))] + [any_spec] * n_ch,
        out_specs=[row(D), row(D), row(H), row(H), _full((8, D))] + [any_spec] * n_ch,
        out_shape=[jax.ShapeDtypeStruct((S, D), F32), jax.ShapeDtypeStruct((S, D), BF16),
                   jax.ShapeDtypeStruct((S, H), F32), jax.ShapeDtypeStruct((S, H), F32),
                   jax.ShapeDtypeStruct((8, D), F32)] + ex.out_shapes(chunked),
        scratch_shapes=ex.sem_shapes(),
        compiler_params=_cparams(("arbitrary",)),
    )(dh1, xh, rs, ln_g, o_sb, o_fx, rr, g_cat, gmat, w_outT, *chunked)


def _proj_bwd(dxa, pieces, dc, u, w_qkvT, w_fT, n_fox, chunked):
    S, D = dxa.shape
    H = D // 2
    TM = ROW_TILE
    nT = S // TM
    tri = _tri(TM, lambda r, c: c >= r)
    n_p = len(pieces)
    n_ch = len(chunked)
    ex = _Exchange(n_ch, scatter=True)

    def body(*refs):
        dxa_ref = refs[0]
        p_refs = refs[1:1 + n_p]
        dc_ref, u_ref, wq_ref, wf_ref, tri_ref = refs[1 + n_p:6 + n_p]
        ch_in = refs[6 + n_p:6 + n_p + n_ch]
        dx_ref, df_ref, sm_ref = refs[6 + n_p + n_ch:9 + n_p + n_ch]
        ch_out = refs[9 + n_p + n_ch:9 + n_p + 2 * n_ch]
        run_ref = refs[9 + n_p + 2 * n_ch]
        sems = refs[10 + n_p + 2 * n_ch:]

        @pl.when(pl.program_id(0) == 0)
        def _():
            ex.start(ch_in, ch_out, sems)
            run_ref[...] = jnp.zeros_like(run_ref)
            sm_ref[...] = jnp.zeros_like(sm_ref)

        hi, mid, lo = _split3(dc_ref[...])
        t = tri_ref[...]
        dlogf = _dot(t, hi) + _dot(t, mid) + _dot(t, lo) + run_ref[...]
        run_ref[...] = dlogf[0:1, :]
        uv = u_ref[...]
        lane = lax.broadcasted_iota(I32, uv.shape, 1)
        df = jnp.where(lane < n_fox, dlogf * jax.nn.sigmoid(-uv), 0.0)
        sm_ref[0:1, :] += jnp.sum(df, axis=0, keepdims=True)
        dfb = df.astype(BF16)
        df_ref[...] = dfb
        acc = dxa_ref[...] + _dot(dfb, wf_ref[...])
        for a in range(n_p):
            acc = acc + _dot(p_refs[a][...].astype(BF16), wq_ref[a * H:(a + 1) * H, :])
        dx_ref[...] = acc

        @pl.when(pl.program_id(0) == nT - 1)
        def _():
            ex.finish(ch_in, ch_out, sems)

    rev = lambda w: pl.BlockSpec((TM, w), lambda i: (nT - 1 - i, 0))
    any_spec = pl.BlockSpec(memory_space=pl.ANY)
    return pl.pallas_call(
        body, name="proj_bwd", grid=(nT,),
        in_specs=[rev(D)] + [rev(H)] * n_p + [rev(LANES), rev(LANES), _full(w_qkvT.shape), _full(w_fT.shape),
                                               _full(tri.shape)] + [any_spec] * n_ch,
        out_specs=[rev(D), rev(LANES), _full((8, LANES))] + [any_spec] * n_ch,
        out_shape=[jax.ShapeDtypeStruct((S, D), F32), jax.ShapeDtypeStruct((S, LANES), BF16),
                   jax.ShapeDtypeStruct((8, LANES), F32)] + ex.out_shapes(chunked),
        scratch_shapes=[pltpu.VMEM((1, LANES), F32)] + ex.sem_shapes(),
        compiler_params=_cparams(("arbitrary",)),
    )(dxa, *pieces, dc, u, w_qkvT, w_fT, tri, *chunked)


def _matmul_tn(a, bs, name, n_split=1):
    S, M = a.shape
    widths = [b.shape[1] for b in bs]
    N = sum(widths)
    assert n_split == 1 or len(bs) == 1
    TK = 512 if S % 512 == 0 else ROW_TILE
    MC = 512 if M % 512 == 0 else 256
    nb = len(bs)

    def body(*refs):
        a_ref, b_refs, o_ref = refs[0], refs[1:1 + nb], refs[1 + nb]

        @pl.when(pl.program_id(1) == 0)
        def _():
            o_ref[...] = jnp.zeros_like(o_ref)

        n0 = 0
        for b_ref in b_refs:
            bv = b_ref[...].astype(BF16)
            w = bv.shape[1]
            for m0 in range(0, M, MC):
                o_ref[m0:m0 + MC, n0:n0 + w] += _dot_tn(a_ref[:, m0:m0 + MC].astype(BF16), bv)
            n0 += w

    return pl.pallas_call(
        body, name=name, grid=(n_split, S // TK),
        in_specs=[pl.BlockSpec((TK, M), lambda n, k: (k, 0))]
        + [pl.BlockSpec((TK, w // n_split), lambda n, k: (k, n)) for w in widths],
        out_specs=pl.BlockSpec((M, N // n_split), lambda n, k: (0, n)),
        out_shape=jax.ShapeDtypeStruct((M, N), F32),
        compiler_params=_cparams(("arbitrary", "arbitrary")),
    )(a, *bs)


def _half_mask(half):
    lane = lax.broadcasted_iota(I32, (1, LANES), 1)
    return (lane >= half * HEAD_DIM) & (lane < half * HEAD_DIM + HEAD_DIM)


def _lane_tile(a, width):
    return jnp.concatenate([a] * (width // LANES), axis=1)


def _softplus_neg_abs(z):
    return jnp.log1p(jnp.exp(-jnp.abs(z)))


def _sb_fwd(qkv, n_pair, shards):
    S = qkv.shape[0]
    B = SB_BLOCK
    nq = S // B
    us = _tri(B, lambda r, c: r > c)
    n_sh = len(shards)
    ex = _Exchange(n_sh, scatter=False)

    def body(*refs):
        q_ref, k_ref, v_ref, us_ref = refs[:4]
        sh_in = refs[4:4 + n_sh]
        o_ref, st_ref, js_ref = refs[4 + n_sh:7 + n_sh]
        sh_out = refs[7 + n_sh:7 + 2 * n_sh]
        acc_ref, r_ref = refs[7 + 2 * n_sh:9 + 2 * n_sh]
        sems = refs[9 + 2 * n_sh:]
        p, i = pl.program_id(0), pl.program_id(1)

        @pl.when((p == 0) & (i == 0))
        def _():
            ex.start(sh_in, sh_out, sems)

        hms = [_half_mask(h) for h in range(2)]
        qv = q_ref[...]
        qss = [jnp.where(hm, qv, jnp.zeros_like(qv)) * SCALE for hm in hms]
        row = lax.broadcasted_iota(I32, (B, B), 0)
        col = lax.broadcasted_iota(I32, (B, B), 1)
        tri = col < row
        acc_ref[...] = jnp.zeros_like(acc_ref)
        r_ref[...] = jnp.zeros_like(r_ref)

        def block(j, diag):
            off = pl.multiple_of(j * B, B)
            kj = k_ref[pl.ds(off, B), :]
            vj = v_ref[pl.ds(off, B), :]
            zs = [_dot_nt(qss[h], kj) for h in range(2)]
            sps = [_softplus_neg_abs(z) for z in zs]
            bs = [jnp.minimum(-z, 0.0) - sp for z, sp in zip(zs, sps)]
            if diag:
                bs = [jnp.where(tri, b, 0.0) for b in bs]
            lexcs = [_dot_acc(b, us_ref[...]) for b in bs]
            ws = [jnp.exp(jnp.minimum(z, 0.0) - sp + (_lane_tile(r_ref[h], B) + lexc))
                  for h, (z, sp, lexc) in enumerate(zip(zs, sps, lexcs))]
            if diag:
                ws = [jnp.where(tri, w, 0.0) for w in ws]
            for h in range(2):
                acc_ref[h] += _dot(ws[h].astype(BF16), vj)
                r_ref[h] += jnp.broadcast_to(lexcs[h][:, 0:1] + bs[h][:, 0:1], (B, LANES))

        def live():
            return (jnp.max(r_ref[...]) > -EXP_ZERO).astype(I32)

        block(i, True)

        def step(carry):
            j, _ = carry
            block(j, False)
            return j - 1, live()

        j_end, _ = lax.while_loop(lambda c: (c[0] >= 0) & (c[1] > 0), step, (i - 1, live()))
        js_ref[2 * p, i] = j_end + 1
        js_ref[2 * p + 1, i] = j_end + 1
        o_ref[...] = jnp.where(hms[0], acc_ref[0], acc_ref[1])
        st_ref[...] = r_ref[...]

        @pl.when((p == n_pair - 1) & (i == nq - 1))
        def _():
            ex.finish(sh_in, sh_out, sems)

    any_spec = pl.BlockSpec(memory_space=pl.ANY)
    return pl.pallas_call(
        body, name="sb_attn_fwd", grid=(n_pair, nq),
        in_specs=[pl.BlockSpec((B, LANES), lambda p, i: (i, p)),
                  pl.BlockSpec((S, LANES), lambda p, i: (0, n_pair + p)),
                  pl.BlockSpec((S, LANES), lambda p, i: (0, 2 * n_pair + p)),
                  _full((B, B))] + [any_spec] * n_sh,
        out_specs=[pl.BlockSpec((B, LANES), lambda p, i: (i, p)),
                   pl.BlockSpec((2, B, LANES), lambda p, i: (p, i, 0)),
                   pl.BlockSpec(memory_space=pltpu.SMEM)] + [any_spec] * n_sh,
        out_shape=[jax.ShapeDtypeStruct((S, n_pair * LANES), F32),
                   jax.ShapeDtypeStruct((2 * n_pair, S, LANES), F32),
                   jax.ShapeDtypeStruct((2 * n_pair, nq), I32)] + ex.out_shapes(shards),
        scratch_shapes=[pltpu.VMEM((2, B, LANES), F32), pltpu.VMEM((2, B, LANES), F32)] + ex.sem_shapes(),
        compiler_params=_cparams(("arbitrary", "arbitrary")),
    )(qkv, qkv, qkv, us, *shards)


def _sb_bwd(qkv, do, st, js, n_pair):
    S = qkv.shape[0]
    B = SB_BLOCK
    nq = S // B
    us = _tri(B, lambda r, c: r > c)
    ti = _tri(B, lambda r, c: r <= c)

    def body(js_ref, q_ref, k_ref, v_ref, do_ref, st_ref, us_ref, ti_ref, dq_ref, dk_ref, dv_ref,
             dqa_ref, pr_ref, er_ref):
        p, i = pl.program_id(0), pl.program_id(1)

        @pl.when(i == 0)
        def _():
            dk_ref[...] = jnp.zeros_like(dk_ref)
            dv_ref[...] = jnp.zeros_like(dv_ref)

        hms = [_half_mask(h) for h in range(2)]
        qv = q_ref[...]
        dov = do_ref[...]
        qss = [jnp.where(hm, qv, jnp.zeros_like(qv)) * SCALE for hm in hms]
        dobs = [jnp.where(hm, dov, 0.0).astype(BF16) for hm in hms]
        row = lax.broadcasted_iota(I32, (B, B), 0)
        col = lax.broadcasted_iota(I32, (B, B), 1)
        tri = col < row
        dqa_ref[...] = jnp.zeros_like(dqa_ref)
        pr_ref[...] = jnp.zeros_like(pr_ref)
        er_ref[...] = jnp.zeros_like(er_ref)

        def block(j, diag):
            off = pl.multiple_of(j * B, B)
            kj = k_ref[pl.ds(off, B), :]
            vj = v_ref[pl.ds(off, B), :]
            hs = range(2)
            zs = [_dot_nt(qss[h], kj) for h in hs]
            dws = [_dot_nt(dobs[h], vj) for h in hs]
            sps = [_softplus_neg_abs(z) for z in zs]
            bs = [jnp.minimum(-z, 0.0) - sp for z, sp in zip(zs, sps)]
            if diag:
                bs = [jnp.where(tri, b, 0.0) for b in bs]
            lexcs = [_dot_acc(b, us_ref[...]) for b in bs]
            ws = []
            for h in hs:
                pr_new = pr_ref[h] + jnp.broadcast_to(lexcs[h][:, 0:1] + bs[h][:, 0:1], (B, LANES))
                pr_ref[h] = pr_new
                w = jnp.exp(jnp.minimum(zs[h], 0.0) - sps[h] + (_lane_tile(st_ref[h] - pr_new, B) + lexcs[h]))
                ws.append(jnp.where(tri, w, 0.0) if diag else w)
            es = [dw * w for dw, w in zip(dws, ws)]
            eincs = [_dot_acc(e, ti_ref[...]) for e in es]
            dzbs = []
            for h in hs:
                er = er_ref[h]
                big_e = _lane_tile(er, B) + (eincs[h] - es[h])
                er_ref[h] = er + jnp.broadcast_to(eincs[h][:, B - 1:B], (B, LANES))
                eb = jnp.exp(bs[h])
                dzbs.append((es[h] * eb - big_e * (1.0 - eb)).astype(BF16))
            for h in hs:
                dqa_ref[h] += _dot(dzbs[h], kj)
            dk_ref[pl.ds(off, B), :] += _dot_tn(dzbs[0], qss[0]) + _dot_tn(dzbs[1], qss[1])
            dv_ref[pl.ds(off, B), :] += (_dot_tn(ws[0].astype(BF16), dobs[0])
                                         + _dot_tn(ws[1].astype(BF16), dobs[1]))

        def step(j, carry):
            block(j, False)
            return carry

        lax.fori_loop(js_ref[2 * p, i], i, step, 0)
        block(i, True)
        dq_ref[...] = jnp.where(hms[0], dqa_ref[0], dqa_ref[1]) * SCALE

    W = n_pair * LANES
    return pl.pallas_call(
        body, name="sb_attn_bwd",
        grid_spec=pltpu.PrefetchScalarGridSpec(
            num_scalar_prefetch=1, grid=(n_pair, nq),
            in_specs=[pl.BlockSpec((B, LANES), lambda p, i, js: (i, p)),
                      pl.BlockSpec((S, LANES), lambda p, i, js: (0, n_pair + p)),
                      pl.BlockSpec((S, LANES), lambda p, i, js: (0, 2 * n_pair + p)),
                      pl.BlockSpec((B, LANES), lambda p, i, js: (i, p)),
                      pl.BlockSpec((2, B, LANES), lambda p, i, js: (p, i, 0)),
                      pl.BlockSpec((B, B), lambda p, i, js: (0, 0)),
                      pl.BlockSpec((B, B), lambda p, i, js: (0, 0))],
            out_specs=[pl.BlockSpec((B, LANES), lambda p, i, js: (i, p)),
                       pl.BlockSpec((S, LANES), lambda p, i, js: (0, p)),
                       pl.BlockSpec((S, LANES), lambda p, i, js: (0, p))],
            scratch_shapes=[pltpu.VMEM((2, B, LANES), F32), pltpu.VMEM((2, B, LANES), F32),
                            pltpu.VMEM((2, B, LANES), F32)]),
        out_shape=[jax.ShapeDtypeStruct((S, W), F32)] * 3,
        compiler_params=_cparams(("arbitrary", "arbitrary")),
    )(js, qkv, qkv, qkv, do, st, us, ti)


def _head_column(blk, head):
    lane = lax.broadcasted_iota(I32, (1, LANES), 1)
    return jnp.sum(jnp.where(lane == head, blk, 0.0), axis=1, keepdims=True)


def _fox_fwd(qkv, c, c_rows, kmax, n_pair):
    S = qkv.shape[0]
    BQ, BK = FOX_BQ, FOX_BK
    R = BQ // BK
    nq = S // BQ

    def body(q_ref, k_ref, v_ref, c_ref, cr_ref, km_ref, o_ref, st_ref, js_ref, acc_ref, m_ref, cb_ref, qkb_ref):
        p, i, half = pl.program_id(0), pl.program_id(1), pl.program_id(2)
        hm = _half_mask(half)
        qv = q_ref[...]
        qs = jnp.where(hm, qv, jnp.zeros_like(qv)) * SCALE
        ccol = _head_column(c_ref[...], 2 * p + half)
        cb_ref[...] = jnp.broadcast_to(ccol, (BQ, BK))
        qf = qs.astype(F32)
        qkb_ref[...] = jnp.broadcast_to(
            jnp.sqrt(jnp.sum(qf * qf, axis=1, keepdims=True)) * NORM_SLACK
            * _head_column(km_ref[...], 2 * p + half) + ccol, (BQ, LANES))
        row = lax.broadcasted_iota(I32, (BQ, BK), 0)
        col = lax.broadcasted_iota(I32, (BQ, BK), 1)
        acc_ref[...] = jnp.zeros_like(acc_ref)
        m_ref[...] = jnp.full_like(m_ref, NEG_BIG)

        def blocks(j_top, diag):
            ss, v1s, keeps = [], [], []
            for d in range(R):
                j = j_top - d
                off = pl.multiple_of(j * BK, BK)
                vj = v_ref[pl.ds(off, BK), :]
                v1s.append(jnp.where(hm, vj, jnp.ones_like(vj)))
                s = _dot_nt(qs, k_ref[pl.ds(off, BK), :]) + (cb_ref[...] - cr_ref[0, pl.ds(j, 1), :])
                if diag:
                    keeps.append(col + (R - 1 - d) * BK <= row)
                    s = jnp.where(keeps[-1], s, NEG_BIG)
                ss.append(s)
            m_old = m_ref[...]
            s_max = jnp.max(functools.reduce(jnp.maximum, ss), axis=1, keepdims=True)
            m_new = jnp.maximum(m_old, jnp.broadcast_to(s_max, (BQ, LANES)))
            m_wide = _lane_tile(m_new, BK)
            pvs = [jnp.exp(s - m_wide) for s in ss]
            if diag:
                pvs = [jnp.where(keep, pv, 0.0) for keep, pv in zip(keeps, pvs)]
            new = _dot(pvs[0].astype(BF16), v1s[0])
            for pv, v1 in zip(pvs[1:], v1s[1:]):
                new = new + _dot(pv.astype(BF16), v1)
            acc_ref[...] = jnp.exp(m_old - m_new) * acc_ref[...] + new
            m_ref[...] = m_new

        def live(j):
            c_end = cr_ref[0, pl.ds(jnp.maximum(j, 0), 1), :][:, BK - 1:BK]
            return (jnp.max(qkb_ref[...] - c_end - m_ref[...]) > -EXP_ZERO).astype(I32)

        blocks(R * i + R - 1, True)

        def step(carry):
            j, _ = carry
            go_on = live(j - R)
            blocks(j, False)
            return j - R, go_on

        j_end, _ = lax.while_loop(lambda cr: (cr[0] >= 0) & (cr[1] > 0), step, (R * i - 1, live(R * i - 1)))
        js_ref[2 * p + half, i] = j_end + 1
        acc = acc_ref[...]
        denom = jnp.where(hm, pltpu.roll(acc, HEAD_DIM, 1), acc)
        res = jnp.where(hm, acc / denom, 0.0)

        @pl.when(half == 0)
        def _():
            o_ref[...] = res

        @pl.when(half == 1)
        def _():
            o_ref[...] += res

        st_ref[0] = m_ref[...] + jnp.log(denom)

    return pl.pallas_call(
        body, name="fox_attn_fwd", grid=(n_pair, nq, 2),
        in_specs=[pl.BlockSpec((BQ, LANES), lambda p, i, h: (i, 3 * n_pair + p)),
                  pl.BlockSpec((S, LANES), lambda p, i, h: (0, 4 * n_pair + p)),
                  pl.BlockSpec((S, LANES), lambda p, i, h: (0, 5 * n_pair + p)),
                  pl.BlockSpec((BQ, LANES), lambda p, i, h: (i, 0)),
                  pl.BlockSpec((1, S // BK, BK), lambda p, i, h: (2 * p + h, 0, 0)),
                  _full((1, LANES))],
        out_specs=[pl.BlockSpec((BQ, LANES), lambda p, i, h: (i, p)),
                   pl.BlockSpec((1, BQ, LANES), lambda p, i, h: (2 * p + h, i, 0)),
                   pl.BlockSpec(memory_space=pltpu.SMEM)],
        out_shape=[jax.ShapeDtypeStruct((S, n_pair * LANES), F32),
                   jax.ShapeDtypeStruct((2 * n_pair, S, LANES), F32),
                   jax.ShapeDtypeStruct((2 * n_pair, nq), I32)],
        scratch_shapes=[pltpu.VMEM((BQ, LANES), F32), pltpu.VMEM((BQ, LANES), F32), pltpu.VMEM((BQ, BK), F32),
                        pltpu.VMEM((BQ, LANES), F32)],
        compiler_params=_cparams(("arbitrary", "arbitrary", "arbitrary")),
    )(qkv, qkv, qkv, c, c_rows, kmax)


def _fox_bwd(qkv, do, o, st, c, c_rows, js, n_pair):
    S = qkv.shape[0]
    B = FOX_BLOCK
    nq = S // B

    def body(js_ref, q_ref, k_ref, v_ref, do_ref, o_ref, st_ref, c_ref, cr_ref, dq_ref, dk_ref, dv_ref,
             dc_ref, dqa_ref, rs_ref, cb_ref, db_ref):
        p, i, half = pl.program_id(0), pl.program_id(1), pl.program_id(2)

        @pl.when((i == 0) & (half == 0))
        def _():
            dk_ref[...] = jnp.zeros_like(dk_ref)
            dv_ref[...] = jnp.zeros_like(dv_ref)
            dc_ref[...] = jnp.zeros_like(dc_ref)

        hm = _half_mask(half)
        qv = q_ref[...]
        qs = jnp.where(hm, qv, jnp.zeros_like(qv)) * SCALE
        dov = jnp.where(hm, do_ref[...], 0.0)
        dob = dov.astype(BF16)
        cb_ref[...] = jnp.broadcast_to(_head_column(c_ref[...], 2 * p + half), (B, LANES)) - st_ref[0]
        db_ref[...] = jnp.broadcast_to(jnp.sum(dov * o_ref[...], axis=1, keepdims=True), (B, LANES))
        row = lax.broadcasted_iota(I32, (B, B), 0)
        col = lax.broadcasted_iota(I32, (B, B), 1)
        dqa_ref[...] = jnp.zeros_like(dqa_ref)
        rs_ref[...] = jnp.zeros_like(rs_ref)

        def block(j, diag):
            off = pl.multiple_of(j * B, B)
            kj = k_ref[pl.ds(off, B), :]
            vj = v_ref[pl.ds(off, B), :]
            pv = jnp.exp(_dot_nt(qs, kj) + (_lane_tile(cb_ref[...], B) - cr_ref[0, pl.ds(j, 1), :]))
            if diag:
                pv = jnp.where(col <= row, pv, 0.0)
            ds = pv * (_dot_nt(dob, vj) - _lane_tile(db_ref[...], B))
            dsb = ds.astype(BF16)
            dqa_ref[...] += _dot(dsb, kj)
            dk_ref[pl.ds(off, B), :] += _dot_tn(dsb, qs)
            dv_ref[pl.ds(off, B), :] += _dot_tn(pv.astype(BF16), dob)
            dc_ref[0, half, pl.ds(j, 1), :] -= jnp.sum(ds, axis=0, keepdims=True)
            rs_ref[...] += jnp.sum(ds, axis=1, keepdims=True)

        def step(j, carry):
            block(j, False)
            return carry

        lax.fori_loop(js_ref[2 * p + half, i], i, step, 0)
        block(i, True)
        res = jnp.where(hm, dqa_ref[...] * SCALE, 0.0)

        @pl.when(half == 0)
        def _():
            dq_ref[...] = res

        @pl.when(half == 1)
        def _():
            dq_ref[...] += res

        dc_ref[0, half, pl.ds(i, 1), :] += jnp.transpose(jnp.broadcast_to(rs_ref[...], (B, LANES)))[0:1, :]

    W = n_pair * LANES
    return pl.pallas_call(
        body, name="fox_attn_bwd",
        grid_spec=pltpu.PrefetchScalarGridSpec(
            num_scalar_prefetch=1, grid=(n_pair, nq, 2),
            in_specs=[pl.BlockSpec((B, LANES), lambda p, i, h, js: (i, 3 * n_pair + p)),
                      pl.BlockSpec((S, LANES), lambda p, i, h, js: (0, 4 * n_pair + p)),
                      pl.BlockSpec((S, LANES), lambda p, i, h, js: (0, 5 * n_pair + p)),
                      pl.BlockSpec((B, LANES), lambda p, i, h, js: (i, p)),
                      pl.BlockSpec((B, LANES), lambda p, i, h, js: (i, p)),
                      pl.BlockSpec((1, B, LANES), lambda p, i, h, js: (2 * p + h, i, 0)),
                      pl.BlockSpec((B, LANES), lambda p, i, h, js: (i, 0)),
                      pl.BlockSpec((1, nq, B), lambda p, i, h, js: (2 * p + h, 0, 0))],
            out_specs=[pl.BlockSpec((B, LANES), lambda p, i, h, js: (i, p)),
                       pl.BlockSpec((S, LANES), lambda p, i, h, js: (0, p)),
                       pl.BlockSpec((S, LANES), lambda p, i, h, js: (0, p)),
                       pl.BlockSpec((1, 2, nq, B), lambda p, i, h, js: (p, 0, 0, 0))],
            scratch_shapes=[pltpu.VMEM((B, LANES), F32), pltpu.VMEM((B, 1), F32), pltpu.VMEM((B, LANES), F32),
                            pltpu.VMEM((B, LANES), F32)]),
        out_shape=[jax.ShapeDtypeStruct((S, W), F32)] * 3 + [jax.ShapeDtypeStruct((n_pair, 2, nq, B), F32)],
        compiler_params=_cparams(("arbitrary", "arbitrary", "arbitrary")),
    )(js, qkv, qkv, qkv, do, o, st, c, c_rows)


def _adam(w, g, m, v):
    m = ADAM_B1 * m + (1.0 - ADAM_B1) * g
    v = ADAM_B2 * v + (1.0 - ADAM_B2) * (g * g)
    m_hat = m / (1.0 - ADAM_B1 ** ADAM_STEP)
    v_hat = v / (1.0 - ADAM_B2 ** ADAM_STEP)
    delta = -ADAM_LR * (m_hat / (jnp.sqrt(v_hat) + ADAM_EPS) + ADAM_WD * w)
    return delta, m, v


def _reduce_adam(landing, w, m, v, name, extra=None):
    R, C = w.shape
    TR = next(t for t in (256, 128, R) if R % t == 0)
    more = [] if extra is None else [extra]

    def body(*refs):
        l_ref, w_ref, m_ref, v_ref = refs[:4]
        g_ref, d_ref, nm_ref, nv_ref = refs[4 + len(more):]
        g = l_ref[0].astype(F32)
        for s in range(1, N_DEV):
            g = g + l_ref[s].astype(F32)
        if more:
            g = g + refs[4][...]
        d, nm, nv = _adam(w_ref[...], g, m_ref[...], v_ref[...])
        g_ref[...] = g
        d_ref[...] = d
        nm_ref[...] = nm
        nv_ref[...] = nv

    blk = pl.BlockSpec((TR, C), lambda i: (i, 0))
    return pl.pallas_call(
        body, name=name, grid=(R // TR,),
        in_specs=[pl.BlockSpec((N_DEV, TR, C), lambda i: (0, i, 0)), blk, blk, blk] + [blk] * len(more),
        out_specs=[blk] * 4,
        out_shape=[jax.ShapeDtypeStruct((R, C), F32)] * 4,
        compiler_params=_cparams(("arbitrary",)),
    )(landing, w, m, v, *more)


def _reduce_adam_small(landing, w, m, v):
    R, C = w.shape

    def body(l_ref, w_ref, m_ref, v_ref, g_ref, d_ref, nm_ref, nv_ref, loss_ref):
        g = l_ref[0]
        for s in range(1, N_DEV):
            g = g + l_ref[s]
        d, nm, nv = _adam(w_ref[...], g, m_ref[...], v_ref[...])
        g_ref[...] = g
        d_ref[...] = d
        nm_ref[...] = nm
        nv_ref[...] = nv
        loss_ref[...] = jnp.broadcast_to(0.5 * jnp.sum(g[7:8, :], axis=1, keepdims=True), (1, LANES))

    return pl.pallas_call(
        body, name="reduce_adam_small",
        out_shape=[jax.ShapeDtypeStruct((R, C), F32)] * 4 + [jax.ShapeDtypeStruct((1, LANES), F32)],
    )(landing, w, m, v)


def _pad_lanes(a, width):
    return jnp.pad(a, ((0, 0), (0, width - a.shape[1])))


def _pack_small(D, n_fox, g_cat, l1g, l1b, l2g, l2b, bf, last, tail):
    return jnp.concatenate([g_cat, l1g, l1b, l2g, l2b, _pad_lanes(bf, D), jnp.zeros((1, D), F32), last, tail],
                           axis=0)


def kernel(x, w_in, b_f, g_sb, g_fox, w_out, ln1_g, ln1_b, ln2_g, ln2_b, w_gate_up, w_down, loss_target, m_w_in, m_b_f, m_g_sb, m_g_fox, m_w_out, m_ln1_g, m_ln1_b, m_ln2_g, m_ln2_b, m_w_gate_up, m_w_down, v_w_in, v_b_f, v_g_sb, v_g_fox, v_w_out, v_ln1_g, v_ln1_b, v_ln2_g, v_ln2_b, v_w_gate_up, v_w_down):
    x2, tgt = x[0], loss_target[0]
    S, D = x2.shape
    W = D // 2
    n_pair = W // LANES
    n_fox = W // HEAD_DIM
    F = w_down.shape[1] * N_DEV

    (g_in,) = _exchange_call([w_in[0].astype(BF16)], False, "w_in_all_gather")
    w_in_full = g_in.transpose(1, 0, 2).reshape(D, -1)
    w_qkv = w_in_full[:, :6 * W]
    w_f = _pad_lanes(w_in_full[:, 6 * W:], LANES)
    gmat = _tri(D, lambda r, c: (r // HEAD_DIM) == (c // HEAD_DIM))
    g_cat = jnp.concatenate([g_sb, g_fox], axis=1)

    qkv, u, c, ksq = _proj_fwd(x2, w_qkv, w_f, _pad_lanes(b_f, LANES), n_fox)
    c_t = c[:, :n_fox].T
    c_rows = c_t.reshape(n_fox, S // FOX_BLOCK, FOX_BLOCK)
    kmax = jnp.sqrt(ksq[0:1]) * NORM_SLACK
    o_sb, st_sb, js_sb, g_out, g_gu, g_dn = _sb_fwd(
        qkv, n_pair, [w_out[0].astype(BF16), w_gate_up[0].astype(BF16), w_down[0].astype(BF16)])
    w_o = g_out.reshape(D, D)
    w_gu = g_gu.transpose(1, 0, 2).reshape(D, 2 * F)
    w_dn = g_dn.reshape(F, D)
    o_fx, st_fx, js_fx = _fox_fwd(qkv, c, c_t.reshape(n_fox, S // FOX_BK, FOX_BK), kmax, n_pair)
    js_fx = js_fx // (FOX_BLOCK // FOX_BK)
    h1, xh1, rs1, on_b, rr = _post_attn_fwd(o_sb, o_fx, x2, g_cat, gmat, w_o, ln1_g, ln1_b)
    gu, act_b, dyp, sm2 = _mlp_fwd(h1, tgt, w_gu, w_dn, ln2_g, ln2_b)

    dgu_b, dh1 = _mlp_bwd(gu, dyp, w_gu.T, w_dn.T)
    gw_gu = _matmul_tn(h1, [dgu_b], "grad_w_gate_up", n_split=2)
    gw_dn = _matmul_tn(act_b, [dyp], "grad_w_down")
    dxa, dmix_b, do_sb, do_fx, sm1, l_gu, l_dn = _post_attn_bwd(
        dh1, xh1, rs1, ln1_g, o_sb, o_fx, rr, g_cat, gmat, w_o.T,
        [gw_gu.astype(BF16).reshape(D, N_DEV, -1).transpose(1, 0, 2),
         gw_dn.astype(BF16).reshape(N_DEV, F // N_DEV, D)])
    dq_sb, dk_sb, dv_sb = _sb_bwd(qkv, do_sb, st_sb, js_sb, n_pair)
    dq_fx, dk_fx, dv_fx, dcr = _fox_bwd(qkv, do_fx, o_fx, st_fx, c, c_rows, js_fx, n_pair)
    dc = _pad_lanes(dcr.reshape(n_fox, S).T, LANES)
    pieces = [dq_sb, dk_sb, dv_sb, dq_fx, dk_fx, dv_fx]
    gw_qkv = _matmul_tn(x2, pieces, "grad_w_qkv")
    gw_out = _matmul_tn(on_b, [dmix_b], "grad_w_out")
    dx, df_b, sm0, l_in, l_out = _proj_bwd(
        dxa, pieces, dc, u, w_qkv.T, w_f.T, n_fox,
        [_pad_lanes(gw_qkv.astype(BF16), 6 * W + n_fox).reshape(D, N_DEV, -1).transpose(1, 0, 2),
         gw_out.astype(BF16).reshape(N_DEV, D // N_DEV, D)])
    gw_f = _matmul_tn(x2, [df_b], "grad_w_f")[:, :n_fox]

    small = _pack_small(D, n_fox, sm1[2:3], sm1[0:1], sm1[1:2], sm2[0:1], sm2[1:2], sm0[0:1, :n_fox],
                        sm2[2:3] * (1.0 / D), gw_f.T)
    (l_small,) = _exchange_call([jnp.broadcast_to(small[None], (N_DEV,) + small.shape)], True, "small_exchange")

    zero = jnp.zeros((1, D), F32)
    pack = lambda gc, a, b_, c_, d_, bf: _pack_small(D, n_fox, gc, a, b_, c_, d_, bf, zero,
                                                     jnp.zeros((n_fox, D), F32))
    r_small = _reduce_adam_small(
        l_small,
        pack(g_cat, ln1_g, ln1_b, ln2_g, ln2_b, b_f),
        pack(jnp.concatenate([m_g_sb, m_g_fox], axis=1), m_ln1_g, m_ln1_b, m_ln2_g, m_ln2_b, m_b_f),
        pack(jnp.concatenate([v_g_sb, v_g_fox], axis=1), v_ln1_g, v_ln1_b, v_ln2_g, v_ln2_b, v_b_f))
    loss = r_small[4][0, 0]
    cols = w_in.shape[2]
    gf_cols = jnp.pad(r_small[0][8:8 + n_fox].T, ((0, 0), (cols - n_fox, 0)))
    extra = jnp.where(_my_index() == N_DEV - 1, gf_cols, 0.0)
    r_in = _reduce_adam(l_in, w_in[0], m_w_in[0], v_w_in[0], "reduce_adam_w_in", extra)
    r_out = _reduce_adam(l_out, w_out[0], m_w_out[0], v_w_out[0], "reduce_adam_w_out")
    r_gu = _reduce_adam(l_gu, w_gate_up[0], m_w_gate_up[0], v_w_gate_up[0], "reduce_adam_w_gate_up")
    r_dn = _reduce_adam(l_dn, w_down[0], m_w_down[0], v_w_down[0], "reduce_adam_w_down")

    def unpack(kind):
        big = [r_in[kind][None], None, None, None, r_out[kind][None], None, None, None, None,
               r_gu[kind][None], r_dn[kind][None]]
        s = r_small[kind]
        big[1] = s[5:6, :n_fox]
        big[2] = s[0:1, :W]
        big[3] = s[0:1, W:]
        big[5], big[6], big[7], big[8] = s[1:2], s[2:3], s[3:4], s[4:5]
        return big

    return (loss, dx[None], *unpack(0), *unpack(1), *unpack(2), *unpack(3))
```

```python
import functools

import jax
import jax.numpy as jnp
from jax import lax
from jax.experimental import pallas as pl
from jax.experimental.pallas import tpu as pltpu

F32 = jnp.float32
BF16 = jnp.bfloat16
I32 = jnp.int32

N_DEV = 8
HEAD_DIM = 64
LANES = 128
SCALE = HEAD_DIM ** -0.5
ALPHA = 2.0 ** 0.25
LN_EPS = 1e-5
RMS_EPS = 1e-6
ADAM_LR, ADAM_B1, ADAM_B2, ADAM_EPS, ADAM_WD, ADAM_STEP = 0.001, 0.9, 0.999, 1e-08, 0.01, 10
NEG_BIG = -1e30
NORM_SLACK = 1.01
EXP_ZERO = 88.5
VMEM_LIMIT = 60 * 1024 * 1024
ROW_TILE = 256
SB_BLOCK = 256
FOX_BLOCK = 512
FOX_BQ, FOX_BK = 512, 256
assert FOX_BQ == FOX_BLOCK and FOX_BLOCK % FOX_BK == 0
MESH = pl.DeviceIdType.MESH


def _cparams(sem):
    return pltpu.CompilerParams(dimension_semantics=sem, vmem_limit_bytes=VMEM_LIMIT)


def _dot(a, b):
    return jnp.dot(a, b, preferred_element_type=F32)


def _dot_nt(a, b):
    return lax.dot_general(a, b, (((1,), (1,)), ((), ())), preferred_element_type=F32)


def _dot_tn(a, b):
    return lax.dot_general(a, b, (((0,), (0,)), ((), ())), preferred_element_type=F32)


def _split2(a):
    hi = a.astype(BF16)
    lo = (a - hi.astype(F32)).astype(BF16)
    return hi, lo


def _split3(a):
    hi = a.astype(BF16)
    r1 = a - hi.astype(F32)
    mid = r1.astype(BF16)
    lo = (r1 - mid.astype(F32)).astype(BF16)
    return hi, mid, lo


def _dot_acc(a, m):
    hi, lo = _split2(a)
    return _dot(hi, m) + _dot(lo, m)


def _tri(n, fn):
    r = lax.broadcasted_iota(I32, (n, n), 0)
    c = lax.broadcasted_iota(I32, (n, n), 1)
    return fn(r, c).astype(BF16)


def _full(shape):
    nd = len(shape)
    return pl.BlockSpec(shape, lambda *_: (0,) * nd)


def _peer(k):
    x, y, c = lax.axis_index("x"), lax.axis_index("y"), lax.axis_index("c")
    return (1 - x if k & 4 else x, 1 - y if k & 2 else y, 1 - c if k & 1 else c)


def _my_index():
    return 4 * lax.axis_index("x") + 2 * lax.axis_index("y") + lax.axis_index("c")


class _Exchange:
    def __init__(self, n, scatter):
        self.n, self.scatter = n, scatter

    def sem_shapes(self):
        return [pltpu.SemaphoreType.DMA(((N_DEV - 1) * self.n,)), pltpu.SemaphoreType.DMA(((N_DEV - 1) * self.n,)),
                pltpu.SemaphoreType.DMA((self.n,))]

    def out_shapes(self, arrays):
        if self.scatter:
            return [jax.ShapeDtypeStruct(s.shape, s.dtype) for s in arrays]
        return [jax.ShapeDtypeStruct((N_DEV,) + s.shape, s.dtype) for s in arrays]

    def _copies(self, ins, outs, sems, landing):
        send_sems, recv_sems, local_sems = sems
        me = _my_index()
        src = lambda a, d: ins[a].at[d] if self.scatter else ins[a]
        local = [pltpu.make_async_copy(src(a, me), outs[a].at[me], local_sems.at[a]) for a in range(self.n)]
        remote = [pltpu.make_async_remote_copy(
            src_ref=src(a, me ^ k), dst_ref=outs[a].at[me ^ k if landing else me],
            send_sem=send_sems.at[(k - 1) * self.n + a], recv_sem=recv_sems.at[(k - 1) * self.n + a],
            device_id=_peer(k), device_id_type=MESH) for k in range(1, N_DEV) for a in range(self.n)]
        return local, remote

    def start(self, ins, outs, sems):
        local, sent = self._copies(ins, outs, sems, landing=False)
        for cp in local + sent:
            cp.start()

    def finish(self, ins, outs, sems):
        local, landed = self._copies(ins, outs, sems, landing=True)
        for cp in landed:
            cp.wait_recv()
        for cp in landed:
            cp.wait_send()
        for cp in local:
            cp.wait()


def _exchange_call(arrays, scatter, name):
    n = len(arrays)
    ex = _Exchange(n, scatter)

    def body(*refs):
        ins, outs, sems = refs[:n], refs[n:2 * n], refs[2 * n:]
        ex.start(ins, outs, sems)
        ex.finish(ins, outs, sems)

    any_spec = pl.BlockSpec(memory_space=pl.ANY)
    return pl.pallas_call(
        body, name=name, out_shape=ex.out_shapes(arrays),
        in_specs=[any_spec] * n, out_specs=[any_spec] * n, scratch_shapes=ex.sem_shapes(),
    )(*arrays)


def _log_sigmoid(u):
    return jnp.minimum(u, 0.0) - jnp.log1p(jnp.exp(-jnp.abs(u)))


def _proj_fwd(x, w_qkv, w_f, bf_pad, n_fox):
    S, D = x.shape
    N = w_qkv.shape[1]
    W = D // 2
    TM = ROW_TILE
    tri = _tri(TM, lambda r, c: c <= r)
    r_ = lax.broadcasted_iota(I32, (W, LANES), 0)
    c_ = lax.broadcasted_iota(I32, (W, LANES), 1)
    head_of = (r_ // HEAD_DIM == c_).astype(BF16)

    def body(x_ref, wq_ref, wf_ref, bf_ref, tri_ref, ho_ref, qkv_ref, u_ref, c_ref, ksq_ref, run_ref):
        @pl.when(pl.program_id(0) == 0)
        def _():
            run_ref[...] = jnp.zeros_like(run_ref)
            ksq_ref[...] = jnp.zeros_like(ksq_ref)

        xb = x_ref[...].astype(BF16)
        for n0 in range(0, N, D):
            chunk = _dot(xb, wq_ref[:, n0:n0 + D]).astype(BF16)
            qkv_ref[:, n0:n0 + D] = chunk
            if n0 == 4 * W:
                kf = chunk[:, :W].astype(F32)
                ksq = jnp.max(_dot_acc(kf * kf, ho_ref[...]), axis=0, keepdims=True)
                ksq_ref[...] = jnp.maximum(ksq_ref[...], ksq)
        u = _dot(xb, wf_ref[...]) + bf_ref[...]
        lane = lax.broadcasted_iota(I32, u.shape, 1)
        logf = jnp.where(lane < n_fox, _log_sigmoid(u), 0.0)
        u_ref[...] = u
        hi, mid, lo = _split3(logf)
        t = tri_ref[...]
        cs = _dot(t, hi) + _dot(t, mid) + _dot(t, lo) + run_ref[...]
        c_ref[...] = cs
        run_ref[...] = cs[TM - 1:TM, :]

    return pl.pallas_call(
        body, name="proj_fwd", grid=(S // TM,),
        in_specs=[pl.BlockSpec((TM, D), lambda i: (i, 0)), _full(w_qkv.shape), _full(w_f.shape),
                  _full(bf_pad.shape), _full(tri.shape), _full(head_of.shape)],
        out_specs=[pl.BlockSpec((TM, N), lambda i: (i, 0)), pl.BlockSpec((TM, LANES), lambda i: (i, 0)),
                   pl.BlockSpec((TM, LANES), lambda i: (i, 0)), _full((8, LANES))],
        out_shape=[jax.ShapeDtypeStruct((S, N), BF16), jax.ShapeDtypeStruct((S, LANES), F32),
                   jax.ShapeDtypeStruct((S, LANES), F32), jax.ShapeDtypeStruct((8, LANES), F32)],
        scratch_shapes=[pltpu.VMEM((1, LANES), F32)],
        compiler_params=_cparams(("arbitrary",)),
    )(x, w_qkv, w_f, bf_pad, tri, head_of)


def _post_attn_fwd(o_sb, o_fx, x, g_cat, gmat, w_out, ln_g, ln_b):
    S, D = x.shape
    H = D // 2
    TM = ROW_TILE

    def body(osb_ref, ofx_ref, x_ref, g_ref, gm_ref, wo_ref, lg_ref, lb_ref,
             h1_ref, xh_ref, rs_ref, on_ref, rr_ref):
        o = jnp.concatenate([osb_ref[...], ofx_ref[...]], axis=1)
        ms = _dot_acc(o * o, gm_ref[...]) * (1.0 / HEAD_DIM)
        r = lax.rsqrt(ms + RMS_EPS)
        onb = (o * r * g_ref[...]).astype(BF16)
        hp = ALPHA * x_ref[...] + _dot(onb, wo_ref[...])
        mu = jnp.mean(hp, axis=-1, keepdims=True)
        d = hp - mu
        rstd = lax.rsqrt(jnp.mean(d * d, axis=-1, keepdims=True) + LN_EPS)
        xh = d * rstd
        h1_ref[...] = xh * lg_ref[...] + lb_ref[...]
        xh_ref[...] = xh
        rs_ref[...] = jnp.broadcast_to(rstd, (TM, LANES))
        on_ref[...] = onb
        rr_ref[...] = r

    row = lambda w: pl.BlockSpec((TM, w), lambda i: (i, 0))
    return pl.pallas_call(
        body, name="post_attn_fwd", grid=(S // TM,),
        in_specs=[row(H), row(H), row(D), _full((1, D)), _full((D, D)), _full((D, D)), _full((1, D)), _full((1, D))],
        out_specs=[row(D), row(D), row(LANES), row(D), row(D)],
        out_shape=[jax.ShapeDtypeStruct((S, D), F32), jax.ShapeDtypeStruct((S, D), F32),
                   jax.ShapeDtypeStruct((S, LANES), F32), jax.ShapeDtypeStruct((S, D), BF16),
                   jax.ShapeDtypeStruct((S, D), F32)],
        compiler_params=_cparams(("arbitrary",)),
    )(o_sb, o_fx, x, g_cat, gmat, w_out, ln_g, ln_b)


def _ln_bwd(dxh, xh, rstd):
    m1 = jnp.mean(dxh, axis=-1, keepdims=True)
    m2 = jnp.mean(dxh * xh, axis=-1, keepdims=True)
    return rstd * (dxh - m1 - xh * m2)


def _mlp_fwd(h1, target, w_gu, w_dn, ln_g, ln_b):
    S, D = h1.shape
    F = w_dn.shape[0]
    TM = ROW_TILE
    FC = F // 2

    def body(h1_ref, tg_ref, wgu_hbm, wdn_hbm, lg_ref, lb_ref, gu_ref, act_ref, dyp_ref, sm_ref, wgu, wdn):
        @pl.when(pl.program_id(0) == 0)
        def _():
            pltpu.sync_copy(wgu_hbm, wgu)
            pltpu.sync_copy(wdn_hbm, wdn)
            sm_ref[...] = jnp.zeros_like(sm_ref)

        h1v = h1_ref[...]
        hb = h1v.astype(BF16)
        ff = jnp.zeros((TM, D), F32)
        for c0 in range(0, F, FC):
            g = _dot(hb, wgu[:, c0:c0 + FC])
            u = _dot(hb, wgu[:, F + c0:F + c0 + FC])
            gu_ref[:, c0:c0 + FC] = g
            gu_ref[:, F + c0:F + c0 + FC] = u
            ab = ((g * jax.nn.sigmoid(g)) * u).astype(BF16)
            act_ref[:, c0:c0 + FC] = ab
            ff = ff + _dot(ab, wdn[c0:c0 + FC, :])
        yp = ALPHA * h1v + ff
        mu = jnp.mean(yp, axis=-1, keepdims=True)
        d = yp - mu
        rstd = lax.rsqrt(jnp.mean(d * d, axis=-1, keepdims=True) + LN_EPS)
        xh = d * rstd
        err = (xh * lg_ref[...] + lb_ref[...]) - tg_ref[...]
        dy = err * (1.0 / D)
        sm_ref[0:1, :] += jnp.sum(dy * xh, axis=0, keepdims=True)
        sm_ref[1:2, :] += jnp.sum(dy, axis=0, keepdims=True)
        sm_ref[2:3, :] += jnp.sum(err * err, axis=0, keepdims=True)
        dyp_ref[...] = _ln_bwd(dy * lg_ref[...], xh, rstd)

    row = lambda w: pl.BlockSpec((TM, w), lambda i: (i, 0))
    hbm = pl.BlockSpec(memory_space=pl.ANY)
    return pl.pallas_call(
        body, name="mlp_fwd", grid=(S // TM,),
        in_specs=[row(D), row(D), hbm, hbm, _full((1, D)), _full((1, D))],
        out_specs=[row(2 * F), row(F), row(D), _full((8, D))],
        out_shape=[jax.ShapeDtypeStruct((S, 2 * F), F32), jax.ShapeDtypeStruct((S, F), BF16),
                   jax.ShapeDtypeStruct((S, D), F32), jax.ShapeDtypeStruct((8, D), F32)],
        scratch_shapes=[pltpu.VMEM(w_gu.shape, BF16), pltpu.VMEM(w_dn.shape, BF16)],
        compiler_params=_cparams(("arbitrary",)),
    )(h1, target, w_gu, w_dn, ln_g, ln_b)


def _mlp_bwd(gu, dyp, w_guT, w_dnT):
    S, D = dyp.shape
    F = w_dnT.shape[1]
    TM = ROW_TILE
    FC = F // 2

    def body(gu_ref, dyp_ref, wguT_hbm, wdnT_hbm, dgu_ref, dh1_ref, wguT, wdnT):
        @pl.when(pl.program_id(0) == 0)
        def _():
            pltpu.sync_copy(wguT_hbm, wguT)
            pltpu.sync_copy(wdnT_hbm, wdnT)

        dypv = dyp_ref[...]
        db = dypv.astype(BF16)
        dh1 = ALPHA * dypv
        for c0 in range(0, F, FC):
            dact = _dot(db, wdnT[:, c0:c0 + FC])
            g = gu_ref[:, c0:c0 + FC]
            u = gu_ref[:, F + c0:F + c0 + FC]
            sg = jax.nn.sigmoid(g)
            dgb = (dact * u * (sg * (1.0 + g * (1.0 - sg)))).astype(BF16)
            dub = (dact * (g * sg)).astype(BF16)
            dgu_ref[:, c0:c0 + FC] = dgb
            dgu_ref[:, F + c0:F + c0 + FC] = dub
            dh1 = dh1 + _dot(dgb, wguT[c0:c0 + FC, :]) + _dot(dub, wguT[F + c0:F + c0 + FC, :])
        dh1_ref[...] = dh1

    row = lambda w: pl.BlockSpec((TM, w), lambda i: (i, 0))
    hbm = pl.BlockSpec(memory_space=pl.ANY)
    return pl.pallas_call(
        body, name="mlp_bwd", grid=(S // TM,),
        in_specs=[row(2 * F), row(D), hbm, hbm],
        out_specs=[row(2 * F), row(D)],
        out_shape=[jax.ShapeDtypeStruct((S, 2 * F), BF16), jax.ShapeDtypeStruct((S, D), F32)],
        scratch_shapes=[pltpu.VMEM(w_guT.shape, BF16), pltpu.VMEM(w_dnT.shape, BF16)],
        compiler_params=_cparams(("arbitrary",)),
    )(gu, dyp, w_guT, w_dnT)


def _post_attn_bwd(dh1, xh, rs, ln_g, o_sb, o_fx, rr, g_cat, gmat, w_outT, chunked):
    S, D = dh1.shape
    H = D // 2
    TM = ROW_TILE
    nT = S // TM
    n_ch = len(chunked)
    ex = _Exchange(n_ch, scatter=True)

    def body(*refs):
        dh1_ref, xh_ref, rs_ref, lg_ref, osb_ref, ofx_ref, rr_ref, g_ref, gm_ref, woT_ref = refs[:10]
        ch_in = refs[10:10 + n_ch]
        dxa_ref, dmix_ref, dosb_ref, dofx_ref, sm_ref = refs[10 + n_ch:15 + n_ch]
        ch_out = refs[15 + n_ch:15 + 2 * n_ch]
        sems = refs[15 + 2 * n_ch:]

        @pl.when(pl.program_id(0) == 0)
        def _():
            ex.start(ch_in, ch_out, sems)
            sm_ref[...] = jnp.zeros_like(sm_ref)

        dh = dh1_ref[...]
        xhv = xh_ref[...]
        dhp = _ln_bwd(dh * lg_ref[...], xhv, rs_ref[:, 0:1])
        dxa_ref[...] = ALPHA * dhp
        dmb = dhp.astype(BF16)
        dmix_ref[...] = dmb
        don = _dot(dmb, woT_ref[...])
        o = jnp.concatenate([osb_ref[...], ofx_ref[...]], axis=1)
        r = rr_ref[...]
        u = don * g_ref[...]
        t = _dot_acc(u * o, gm_ref[...]) * (1.0 / HEAD_DIM)
        do = r * u - o * (r * r * r) * t
        dosb_ref[...] = do[:, :H]
        dofx_ref[...] = do[:, H:]
        sm_ref[0:1, :] += jnp.sum(dh * xhv, axis=0, keepdims=True)
        sm_ref[1:2, :] += jnp.sum(dh, axis=0, keepdims=True)
        sm_ref[2:3, :] += jnp.sum(don * o * r, axis=0, keepdims=True)

        @pl.when(pl.program_id(0) == nT - 1)
        def _():
            ex.finish(ch_in, ch_out, sems)

    row = lambda w: pl.BlockSpec((TM, w), lambda i: (i, 0))
    any_spec = pl.BlockSpec(memory_space=pl.ANY)
    return pl.pallas_call(
        body, name="post_attn_bwd", grid=(nT,),
        in_specs=[row(D), row(D), row(LANES), _full((1, D)), row(H), row(H), row(D), _full((1, D)),
                  _full((D, D)), _full((D, D))] + [any_spec] * n_ch,
        out_specs=[row(D), row(D), row(H), row(H), _full((8, D))] + [any_spec] * n_ch,
        out_shape=[jax.ShapeDtypeStruct((S, D), F32), jax.ShapeDtypeStruct((S, D), BF16),
                   jax.ShapeDtypeStruct((S, H), F32), jax.ShapeDtypeStruct((S, H), F32),
                   jax.ShapeDtypeStruct((8, D), F32)] + ex.out_shapes(chunked),
        scratch_shapes=ex.sem_shapes(),
        compiler_params=_cparams(("arbitrary",)),
    )(dh1, xh, rs, ln_g, o_sb, o_fx, rr, g_cat, gmat, w_outT, *chunked)


def _proj_bwd(dxa, x, pieces, dc, u, w_qkvT, w_fT, n_fox, chunked):
    S, D = dxa.shape
    H = D // 2
    TM = ROW_TILE
    nT = S // TM
    tri = _tri(TM, lambda r, c: c >= r)
    n_p = len(pieces)
    n_ch = len(chunked)
    ex = _Exchange(n_ch, scatter=True)

    def body(*refs):
        dxa_ref, x_ref = refs[:2]
        p_refs = refs[2:2 + n_p]
        dc_ref, u_ref, wq_ref, wf_ref, tri_ref = refs[2 + n_p:7 + n_p]
        ch_in = refs[7 + n_p:7 + n_p + n_ch]
        dx_ref, gwf_ref, sm_ref = refs[7 + n_p + n_ch:10 + n_p + n_ch]
        ch_out = refs[10 + n_p + n_ch:10 + n_p + 2 * n_ch]
        run_ref = refs[10 + n_p + 2 * n_ch]
        sems = refs[11 + n_p + 2 * n_ch:]

        @pl.when(pl.program_id(0) == 0)
        def _():
            ex.start(ch_in, ch_out, sems)
            run_ref[...] = jnp.zeros_like(run_ref)
            sm_ref[...] = jnp.zeros_like(sm_ref)
            gwf_ref[...] = jnp.zeros_like(gwf_ref)

        hi, mid, lo = _split3(dc_ref[...])
        t = tri_ref[...]
        dlogf = _dot(t, hi) + _dot(t, mid) + _dot(t, lo) + run_ref[...]
        run_ref[...] = dlogf[0:1, :]
        uv = u_ref[...]
        lane = lax.broadcasted_iota(I32, uv.shape, 1)
        df = jnp.where(lane < n_fox, dlogf * jax.nn.sigmoid(-uv), 0.0)
        sm_ref[0:1, :] += jnp.sum(df, axis=0, keepdims=True)
        dfb = df.astype(BF16)
        gwf_ref[...] += _dot_tn(x_ref[...].astype(BF16), dfb)
        acc = dxa_ref[...] + _dot(dfb, wf_ref[...])
        for a in range(n_p):
            acc = acc + _dot(p_refs[a][...].astype(BF16), wq_ref[a * H:(a + 1) * H, :])
        dx_ref[...] = acc

        @pl.when(pl.program_id(0) == nT - 1)
        def _():
            ex.finish(ch_in, ch_out, sems)

    rev = lambda w: pl.BlockSpec((TM, w), lambda i: (nT - 1 - i, 0))
    any_spec = pl.BlockSpec(memory_space=pl.ANY)
    return pl.pallas_call(
        body, name="proj_bwd", grid=(nT,),
        in_specs=[rev(D), rev(D)] + [rev(H)] * n_p + [rev(LANES), rev(LANES), _full(w_qkvT.shape),
                                                       _full(w_fT.shape), _full(tri.shape)] + [any_spec] * n_ch,
        out_specs=[rev(D), _full((D, LANES)), _full((8, LANES))] + [any_spec] * n_ch,
        out_shape=[jax.ShapeDtypeStruct((S, D), F32), jax.ShapeDtypeStruct((D, LANES), F32),
                   jax.ShapeDtypeStruct((8, LANES), F32)] + ex.out_shapes(chunked),
        scratch_shapes=[pltpu.VMEM((1, LANES), F32)] + ex.sem_shapes(),
        compiler_params=_cparams(("arbitrary",)),
    )(dxa, x, *pieces, dc, u, w_qkvT, w_fT, tri, *chunked)


def _matmul_tn(a, bs, name, n_split=1, pad_cols=0):
    S, M = a.shape
    widths = [b.shape[1] for b in bs]
    N = sum(widths) + pad_cols
    assert n_split == 1 or (len(bs) == 1 and pad_cols == 0)
    TK = 512 if S % 512 == 0 else ROW_TILE
    MC = 512 if M % 512 == 0 else 256
    nb = len(bs)

    def body(*refs):
        a_ref, b_refs, o_ref = refs[0], refs[1:1 + nb], refs[1 + nb]

        @pl.when(pl.program_id(1) == 0)
        def _():
            o_ref[...] = jnp.zeros_like(o_ref)

        n0 = 0
        for b_ref in b_refs:
            bv = b_ref[...].astype(BF16)
            w = bv.shape[1]
            for m0 in range(0, M, MC):
                o_ref[m0:m0 + MC, n0:n0 + w] += _dot_tn(a_ref[:, m0:m0 + MC].astype(BF16), bv)
            n0 += w

    return pl.pallas_call(
        body, name=name, grid=(n_split, S // TK),
        in_specs=[pl.BlockSpec((TK, M), lambda n, k: (k, 0))]
        + [pl.BlockSpec((TK, w // n_split), lambda n, k: (k, n)) for w in widths],
        out_specs=pl.BlockSpec((M, N // n_split), lambda n, k: (0, n)),
        out_shape=jax.ShapeDtypeStruct((M, N), F32),
        compiler_params=_cparams(("arbitrary", "arbitrary")),
    )(a, *bs)


def _half_mask(half):
    lane = lax.broadcasted_iota(I32, (1, LANES), 1)
    return (lane >= half * HEAD_DIM) & (lane < half * HEAD_DIM + HEAD_DIM)


def _lane_tile(a, width):
    return jnp.concatenate([a] * (width // LANES), axis=1)


def _softplus_neg_abs(z):
    return jnp.log1p(jnp.exp(-jnp.abs(z)))


def _sb_fwd(qkv, n_pair, shards):
    S = qkv.shape[0]
    B = SB_BLOCK
    nq = S // B
    us = _tri(B, lambda r, c: r > c)
    n_sh = len(shards)
    ex = _Exchange(n_sh, scatter=False)

    def body(*refs):
        q_ref, k_ref, v_ref, us_ref = refs[:4]
        sh_in = refs[4:4 + n_sh]
        o_ref, st_ref, js_ref = refs[4 + n_sh:7 + n_sh]
        sh_out = refs[7 + n_sh:7 + 2 * n_sh]
        acc_ref, r_ref = refs[7 + 2 * n_sh:9 + 2 * n_sh]
        sems = refs[9 + 2 * n_sh:]
        p, i = pl.program_id(0), pl.program_id(1)

        @pl.when((p == 0) & (i == 0))
        def _():
            ex.start(sh_in, sh_out, sems)

        hms = [_half_mask(h) for h in range(2)]
        qv = q_ref[...]
        qss = [jnp.where(hm, qv, jnp.zeros_like(qv)) * SCALE for hm in hms]
        row = lax.broadcasted_iota(I32, (B, B), 0)
        col = lax.broadcasted_iota(I32, (B, B), 1)
        tri = col < row
        acc_ref[...] = jnp.zeros_like(acc_ref)
        r_ref[...] = jnp.zeros_like(r_ref)

        def block(j, diag):
            off = pl.multiple_of(j * B, B)
            kj = k_ref[pl.ds(off, B), :]
            vj = v_ref[pl.ds(off, B), :]
            zs = [_dot_nt(qss[h], kj) for h in range(2)]
            sps = [_softplus_neg_abs(z) for z in zs]
            bs = [jnp.minimum(-z, 0.0) - sp for z, sp in zip(zs, sps)]
            if diag:
                bs = [jnp.where(tri, b, 0.0) for b in bs]
            lexcs = [_dot_acc(b, us_ref[...]) for b in bs]
            ws = [jnp.exp(jnp.minimum(z, 0.0) - sp + (_lane_tile(r_ref[h], B) + lexc))
                  for h, (z, sp, lexc) in enumerate(zip(zs, sps, lexcs))]
            if diag:
                ws = [jnp.where(tri, w, 0.0) for w in ws]
            for h in range(2):
                acc_ref[h] += _dot(ws[h].astype(BF16), vj)
                r_ref[h] += jnp.broadcast_to(lexcs[h][:, 0:1] + bs[h][:, 0:1], (B, LANES))

        def live():
            return (jnp.max(r_ref[...]) > -EXP_ZERO).astype(I32)

        block(i, True)

        def step(carry):
            j, _ = carry
            block(j, False)
            return j - 1, live()

        j_end, _ = lax.while_loop(lambda c: (c[0] >= 0) & (c[1] > 0), step, (i - 1, live()))
        js_ref[2 * p, i] = j_end + 1
        js_ref[2 * p + 1, i] = j_end + 1
        o_ref[...] = jnp.where(hms[0], acc_ref[0], acc_ref[1])
        st_ref[...] = r_ref[...]

        @pl.when((p == n_pair - 1) & (i == nq - 1))
        def _():
            ex.finish(sh_in, sh_out, sems)

    any_spec = pl.BlockSpec(memory_space=pl.ANY)
    return pl.pallas_call(
        body, name="sb_attn_fwd", grid=(n_pair, nq),
        in_specs=[pl.BlockSpec((B, LANES), lambda p, i: (i, p)),
                  pl.BlockSpec((S, LANES), lambda p, i: (0, n_pair + p)),
                  pl.BlockSpec((S, LANES), lambda p, i: (0, 2 * n_pair + p)),
                  _full((B, B))] + [any_spec] * n_sh,
        out_specs=[pl.BlockSpec((B, LANES), lambda p, i: (i, p)),
                   pl.BlockSpec((2, B, LANES), lambda p, i: (p, i, 0)),
                   pl.BlockSpec(memory_space=pltpu.SMEM)] + [any_spec] * n_sh,
        out_shape=[jax.ShapeDtypeStruct((S, n_pair * LANES), F32),
                   jax.ShapeDtypeStruct((2 * n_pair, S, LANES), F32),
                   jax.ShapeDtypeStruct((2 * n_pair, nq), I32)] + ex.out_shapes(shards),
        scratch_shapes=[pltpu.VMEM((2, B, LANES), F32), pltpu.VMEM((2, B, LANES), F32)] + ex.sem_shapes(),
        compiler_params=_cparams(("arbitrary", "arbitrary")),
    )(qkv, qkv, qkv, us, *shards)


def _sb_bwd(qkv, do, st, js, n_pair):
    S = qkv.shape[0]
    B = SB_BLOCK
    nq = S // B
    us = _tri(B, lambda r, c: r > c)
    ti = _tri(B, lambda r, c: r <= c)

    def body(js_ref, q_ref, k_ref, v_ref, do_ref, st_ref, us_ref, ti_ref, dq_ref, dk_ref, dv_ref,
             dqa_ref, pr_ref, er_ref):
        p, i = pl.program_id(0), pl.program_id(1)

        @pl.when(i == 0)
        def _():
            dk_ref[...] = jnp.zeros_like(dk_ref)
            dv_ref[...] = jnp.zeros_like(dv_ref)

        hms = [_half_mask(h) for h in range(2)]
        qv = q_ref[...]
        dov = do_ref[...]
        qss = [jnp.where(hm, qv, jnp.zeros_like(qv)) * SCALE for hm in hms]
        dobs = [jnp.where(hm, dov, 0.0).astype(BF16) for hm in hms]
        row = lax.broadcasted_iota(I32, (B, B), 0)
        col = lax.broadcasted_iota(I32, (B, B), 1)
        tri = col < row
        dqa_ref[...] = jnp.zeros_like(dqa_ref)
        pr_ref[...] = jnp.zeros_like(pr_ref)
        er_ref[...] = jnp.zeros_like(er_ref)

        def block(j, diag):
            off = pl.multiple_of(j * B, B)
            kj = k_ref[pl.ds(off, B), :]
            vj = v_ref[pl.ds(off, B), :]
            hs = range(2)
            zs = [_dot_nt(qss[h], kj) for h in hs]
            dws = [_dot_nt(dobs[h], vj) for h in hs]
            sps = [_softplus_neg_abs(z) for z in zs]
            bs = [jnp.minimum(-z, 0.0) - sp for z, sp in zip(zs, sps)]
            if diag:
                bs = [jnp.where(tri, b, 0.0) for b in bs]
            lexcs = [_dot_acc(b, us_ref[...]) for b in bs]
            ws = []
            for h in hs:
                pr_new = pr_ref[h] + jnp.broadcast_to(lexcs[h][:, 0:1] + bs[h][:, 0:1], (B, LANES))
                pr_ref[h] = pr_new
                w = jnp.exp(jnp.minimum(zs[h], 0.0) - sps[h] + (_lane_tile(st_ref[h] - pr_new, B) + lexcs[h]))
                ws.append(jnp.where(tri, w, 0.0) if diag else w)
            es = [dw * w for dw, w in zip(dws, ws)]
            eincs = [_dot_acc(e, ti_ref[...]) for e in es]
            dzbs = []
            for h in hs:
                er = er_ref[h]
                big_e = _lane_tile(er, B) + (eincs[h] - es[h])
                er_ref[h] = er + jnp.broadcast_to(eincs[h][:, B - 1:B], (B, LANES))
                eb = jnp.exp(bs[h])
                dzbs.append((es[h] * eb - big_e * (1.0 - eb)).astype(BF16))
            for h in hs:
                dqa_ref[h] += _dot(dzbs[h], kj)
            dk_ref[pl.ds(off, B), :] += _dot_tn(dzbs[0], qss[0]) + _dot_tn(dzbs[1], qss[1])
            dv_ref[pl.ds(off, B), :] += (_dot_tn(ws[0].astype(BF16), dobs[0])
                                         + _dot_tn(ws[1].astype(BF16), dobs[1]))

        def step(j, carry):
            block(j, False)
            return carry

        lax.fori_loop(js_ref[2 * p, i], i, step, 0)
        block(i, True)
        dq_ref[...] = jnp.where(hms[0], dqa_ref[0], dqa_ref[1]) * SCALE

    W = n_pair * LANES
    return pl.pallas_call(
        body, name="sb_attn_bwd",
        grid_spec=pltpu.PrefetchScalarGridSpec(
            num_scalar_prefetch=1, grid=(n_pair, nq),
            in_specs=[pl.BlockSpec((B, LANES), lambda p, i, js: (i, p)),
                      pl.BlockSpec((S, LANES), lambda p, i, js: (0, n_pair + p)),
                      pl.BlockSpec((S, LANES), lambda p, i, js: (0, 2 * n_pair + p)),
                      pl.BlockSpec((B, LANES), lambda p, i, js: (i, p)),
                      pl.BlockSpec((2, B, LANES), lambda p, i, js: (p, i, 0)),
                      pl.BlockSpec((B, B), lambda p, i, js: (0, 0)),
                      pl.BlockSpec((B, B), lambda p, i, js: (0, 0))],
            out_specs=[pl.BlockSpec((B, LANES), lambda p, i, js: (i, p)),
                       pl.BlockSpec((S, LANES), lambda p, i, js: (0, p)),
                       pl.BlockSpec((S, LANES), lambda p, i, js: (0, p))],
            scratch_shapes=[pltpu.VMEM((2, B, LANES), F32), pltpu.VMEM((2, B, LANES), F32),
                            pltpu.VMEM((2, B, LANES), F32)]),
        out_shape=[jax.ShapeDtypeStruct((S, W), F32)] * 3,
        compiler_params=_cparams(("arbitrary", "arbitrary")),
    )(js, qkv, qkv, qkv, do, st, us, ti)


def _head_column(blk, head):
    lane = lax.broadcasted_iota(I32, (1, LANES), 1)
    return jnp.sum(jnp.where(lane == head, blk, 0.0), axis=1, keepdims=True)


def _fox_fwd(qkv, c, c_rows, kmax, n_pair):
    S = qkv.shape[0]
    BQ, BK = FOX_BQ, FOX_BK
    R = BQ // BK
    nq = S // BQ

    def body(q_ref, k_ref, v_ref, c_ref, cr_ref, km_ref, o_ref, st_ref, js_ref, acc_ref, m_ref, cb_ref, qkb_ref):
        p, i, half = pl.program_id(0), pl.program_id(1), pl.program_id(2)
        hm = _half_mask(half)
        qv = q_ref[...]
        qs = jnp.where(hm, qv, jnp.zeros_like(qv)) * SCALE
        ccol = _head_column(c_ref[...], 2 * p + half)
        cb_ref[...] = jnp.broadcast_to(ccol, (BQ, BK))
        qf = qs.astype(F32)
        qkb_ref[...] = jnp.broadcast_to(
            jnp.sqrt(jnp.sum(qf * qf, axis=1, keepdims=True)) * NORM_SLACK
            * _head_column(km_ref[...], 2 * p + half) + ccol, (BQ, LANES))
        row = lax.broadcasted_iota(I32, (BQ, BK), 0)
        col = lax.broadcasted_iota(I32, (BQ, BK), 1)
        acc_ref[...] = jnp.zeros_like(acc_ref)
        m_ref[...] = jnp.full_like(m_ref, NEG_BIG)

        def blocks(j_top, diag):
            ss, v1s, keeps = [], [], []
            for d in range(R):
                j = j_top - d
                off = pl.multiple_of(j * BK, BK)
                vj = v_ref[pl.ds(off, BK), :]
                v1s.append(jnp.where(hm, vj, jnp.ones_like(vj)))
                s = _dot_nt(qs, k_ref[pl.ds(off, BK), :]) + (cb_ref[...] - cr_ref[0, pl.ds(j, 1), :])
                if diag:
                    keeps.append(col + (R - 1 - d) * BK <= row)
                    s = jnp.where(keeps[-1], s, NEG_BIG)
                ss.append(s)
            m_old = m_ref[...]
            s_max = jnp.max(functools.reduce(jnp.maximum, ss), axis=1, keepdims=True)
            m_new = jnp.maximum(m_old, jnp.broadcast_to(s_max, (BQ, LANES)))
            m_wide = _lane_tile(m_new, BK)
            pvs = [jnp.exp(s - m_wide) for s in ss]
            if diag:
                pvs = [jnp.where(keep, pv, 0.0) for keep, pv in zip(keeps, pvs)]
            new = _dot(pvs[0].astype(BF16), v1s[0])
            for pv, v1 in zip(pvs[1:], v1s[1:]):
                new = new + _dot(pv.astype(BF16), v1)
            acc_ref[...] = jnp.exp(m_old - m_new) * acc_ref[...] + new
            m_ref[...] = m_new

        def live(j):
            c_end = cr_ref[0, pl.ds(jnp.maximum(j, 0), 1), :][:, BK - 1:BK]
            return (jnp.max(qkb_ref[...] - c_end - m_ref[...]) > -EXP_ZERO).astype(I32)

        blocks(R * i + R - 1, True)

        def step(carry):
            j, _ = carry
            go_on = live(j - R)
            blocks(j, False)
            return j - R, go_on

        j_end, _ = lax.while_loop(lambda cr: (cr[0] >= 0) & (cr[1] > 0), step, (R * i - 1, live(R * i - 1)))
        js_ref[2 * p + half, i] = j_end + 1
        acc = acc_ref[...]
        denom = jnp.where(hm, pltpu.roll(acc, HEAD_DIM, 1), acc)
        res = jnp.where(hm, acc / denom, 0.0)

        @pl.when(half == 0)
        def _():
            o_ref[...] = res

        @pl.when(half == 1)
        def _():
            o_ref[...] += res

        st_ref[0] = m_ref[...] + jnp.log(denom)

    return pl.pallas_call(
        body, name="fox_attn_fwd", grid=(n_pair, nq, 2),
        in_specs=[pl.BlockSpec((BQ, LANES), lambda p, i, h: (i, 3 * n_pair + p)),
                  pl.BlockSpec((S, LANES), lambda p, i, h: (0, 4 * n_pair + p)),
                  pl.BlockSpec((S, LANES), lambda p, i, h: (0, 5 * n_pair + p)),
                  pl.BlockSpec((BQ, LANES), lambda p, i, h: (i, 0)),
                  pl.BlockSpec((1, S // BK, BK), lambda p, i, h: (2 * p + h, 0, 0)),
                  _full((1, LANES))],
        out_specs=[pl.BlockSpec((BQ, LANES), lambda p, i, h: (i, p)),
                   pl.BlockSpec((1, BQ, LANES), lambda p, i, h: (2 * p + h, i, 0)),
                   pl.BlockSpec(memory_space=pltpu.SMEM)],
        out_shape=[jax.ShapeDtypeStruct((S, n_pair * LANES), F32),
                   jax.ShapeDtypeStruct((2 * n_pair, S, LANES), F32),
                   jax.ShapeDtypeStruct((2 * n_pair, nq), I32)],
        scratch_shapes=[pltpu.VMEM((BQ, LANES), F32), pltpu.VMEM((BQ, LANES), F32), pltpu.VMEM((BQ, BK), F32),
                        pltpu.VMEM((BQ, LANES), F32)],
        compiler_params=_cparams(("arbitrary", "arbitrary", "arbitrary")),
    )(qkv, qkv, qkv, c, c_rows, kmax)


def _fox_bwd(qkv, do, o, st, c, c_rows, js, n_pair):
    S = qkv.shape[0]
    B = FOX_BLOCK
    nq = S // B

    def body(js_ref, q_ref, k_ref, v_ref, do_ref, o_ref, st_ref, c_ref, cr_ref, dq_ref, dk_ref, dv_ref,
             dc_ref, dqa_ref, rs_ref, cb_ref, db_ref):
        p, i, half = pl.program_id(0), pl.program_id(1), pl.program_id(2)

        @pl.when((i == 0) & (half == 0))
        def _():
            dk_ref[...] = jnp.zeros_like(dk_ref)
            dv_ref[...] = jnp.zeros_like(dv_ref)
            dc_ref[...] = jnp.zeros_like(dc_ref)

        hm = _half_mask(half)
        qv = q_ref[...]
        qs = jnp.where(hm, qv, jnp.zeros_like(qv)) * SCALE
        dov = jnp.where(hm, do_ref[...], 0.0)
        dob = dov.astype(BF16)
        cb_ref[...] = jnp.broadcast_to(_head_column(c_ref[...], 2 * p + half), (B, LANES)) - st_ref[0]
        db_ref[...] = jnp.broadcast_to(jnp.sum(dov * o_ref[...], axis=1, keepdims=True), (B, LANES))
        row = lax.broadcasted_iota(I32, (B, B), 0)
        col = lax.broadcasted_iota(I32, (B, B), 1)
        dqa_ref[...] = jnp.zeros_like(dqa_ref)
        rs_ref[...] = jnp.zeros_like(rs_ref)

        def block(j, diag):
            off = pl.multiple_of(j * B, B)
            kj = k_ref[pl.ds(off, B), :]
            vj = v_ref[pl.ds(off, B), :]
            pv = jnp.exp(_dot_nt(qs, kj) + (_lane_tile(cb_ref[...], B) - cr_ref[0, pl.ds(j, 1), :]))
            if diag:
                pv = jnp.where(col <= row, pv, 0.0)
            ds = pv * (_dot_nt(dob, vj) - _lane_tile(db_ref[...], B))
            dsb = ds.astype(BF16)
            dqa_ref[...] += _dot(dsb, kj)
            dk_ref[pl.ds(off, B), :] += _dot_tn(dsb, qs)
            dv_ref[pl.ds(off, B), :] += _dot_tn(pv.astype(BF16), dob)
            dc_ref[0, half, pl.ds(j, 1), :] -= jnp.sum(ds, axis=0, keepdims=True)
            rs_ref[...] += jnp.sum(ds, axis=1, keepdims=True)

        def step(j, carry):
            block(j, False)
            return carry

        lax.fori_loop(js_ref[2 * p + half, i], i, step, 0)
        block(i, True)
        res = jnp.where(hm, dqa_ref[...] * SCALE, 0.0)

        @pl.when(half == 0)
        def _():
            dq_ref[...] = res

        @pl.when(half == 1)
        def _():
            dq_ref[...] += res

        dc_ref[0, half, pl.ds(i, 1), :] += jnp.transpose(jnp.broadcast_to(rs_ref[...], (B, LANES)))[0:1, :]

    W = n_pair * LANES
    return pl.pallas_call(
        body, name="fox_attn_bwd",
        grid_spec=pltpu.PrefetchScalarGridSpec(
            num_scalar_prefetch=1, grid=(n_pair, nq, 2),
            in_specs=[pl.BlockSpec((B, LANES), lambda p, i, h, js: (i, 3 * n_pair + p)),
                      pl.BlockSpec((S, LANES), lambda p, i, h, js: (0, 4 * n_pair + p)),
                      pl.BlockSpec((S, LANES), lambda p, i, h, js: (0, 5 * n_pair + p)),
                      pl.BlockSpec((B, LANES), lambda p, i, h, js: (i, p)),
                      pl.BlockSpec((B, LANES), lambda p, i, h, js: (i, p)),
                      pl.BlockSpec((1, B, LANES), lambda p, i, h, js: (2 * p + h, i, 0)),
                      pl.BlockSpec((B, LANES), lambda p, i, h, js: (i, 0)),
                      pl.BlockSpec((1, nq, B), lambda p, i, h, js: (2 * p + h, 0, 0))],
            out_specs=[pl.BlockSpec((B, LANES), lambda p, i, h, js: (i, p)),
                       pl.BlockSpec((S, LANES), lambda p, i, h, js: (0, p)),
                       pl.BlockSpec((S, LANES), lambda p, i, h, js: (0, p)),
                       pl.BlockSpec((1, 2, nq, B), lambda p, i, h, js: (p, 0, 0, 0))],
            scratch_shapes=[pltpu.VMEM((B, LANES), F32), pltpu.VMEM((B, 1), F32), pltpu.VMEM((B, LANES), F32),
                            pltpu.VMEM((B, LANES), F32)]),
        out_shape=[jax.ShapeDtypeStruct((S, W), F32)] * 3 + [jax.ShapeDtypeStruct((n_pair, 2, nq, B), F32)],
        compiler_params=_cparams(("arbitrary", "arbitrary", "arbitrary")),
    )(js, qkv, qkv, qkv, do, o, st, c, c_rows)


def _adam(w, g, m, v):
    m = ADAM_B1 * m + (1.0 - ADAM_B1) * g
    v = ADAM_B2 * v + (1.0 - ADAM_B2) * (g * g)
    m_hat = m / (1.0 - ADAM_B1 ** ADAM_STEP)
    v_hat = v / (1.0 - ADAM_B2 ** ADAM_STEP)
    delta = -ADAM_LR * (m_hat / (jnp.sqrt(v_hat) + ADAM_EPS) + ADAM_WD * w)
    return delta, m, v


def _reduce_adam(landing, w, m, v, name, extra=None):
    R, C = w.shape
    TR = next(t for t in (256, 128, R) if R % t == 0)
    more = [] if extra is None else [extra]

    def body(*refs):
        l_ref, w_ref, m_ref, v_ref = refs[:4]
        g_ref, d_ref, nm_ref, nv_ref = refs[4 + len(more):]
        g = l_ref[0].astype(F32)
        for s in range(1, N_DEV):
            g = g + l_ref[s].astype(F32)
        if more:
            g = g + refs[4][...]
        d, nm, nv = _adam(w_ref[...], g, m_ref[...], v_ref[...])
        g_ref[...] = g
        d_ref[...] = d
        nm_ref[...] = nm
        nv_ref[...] = nv

    blk = pl.BlockSpec((TR, C), lambda i: (i, 0))
    return pl.pallas_call(
        body, name=name, grid=(R // TR,),
        in_specs=[pl.BlockSpec((N_DEV, TR, C), lambda i: (0, i, 0)), blk, blk, blk] + [blk] * len(more),
        out_specs=[blk] * 4,
        out_shape=[jax.ShapeDtypeStruct((R, C), F32)] * 4,
        compiler_params=_cparams(("arbitrary",)),
    )(landing, w, m, v, *more)


def _reduce_adam_small(landing, w, m, v):
    R, C = w.shape

    def body(l_ref, w_ref, m_ref, v_ref, g_ref, d_ref, nm_ref, nv_ref, loss_ref):
        g = l_ref[0]
        for s in range(1, N_DEV):
            g = g + l_ref[s]
        d, nm, nv = _adam(w_ref[...], g, m_ref[...], v_ref[...])
        g_ref[...] = g
        d_ref[...] = d
        nm_ref[...] = nm
        nv_ref[...] = nv
        loss_ref[...] = jnp.broadcast_to(0.5 * jnp.sum(g[7:8, :], axis=1, keepdims=True), (1, LANES))

    return pl.pallas_call(
        body, name="reduce_adam_small",
        out_shape=[jax.ShapeDtypeStruct((R, C), F32)] * 4 + [jax.ShapeDtypeStruct((1, LANES), F32)],
    )(landing, w, m, v)


def _pad_lanes(a, width):
    return jnp.pad(a, ((0, 0), (0, width - a.shape[1])))


def _pack_small(D, n_fox, g_cat, l1g, l1b, l2g, l2b, bf, last, tail):
    return jnp.concatenate([g_cat, l1g, l1b, l2g, l2b, _pad_lanes(bf, D), jnp.zeros((1, D), F32), last, tail],
                           axis=0)


def kernel(x, w_in, b_f, g_sb, g_fox, w_out, ln1_g, ln1_b, ln2_g, ln2_b, w_gate_up, w_down, loss_target, m_w_in, m_b_f, m_g_sb, m_g_fox, m_w_out, m_ln1_g, m_ln1_b, m_ln2_g, m_ln2_b, m_w_gate_up, m_w_down, v_w_in, v_b_f, v_g_sb, v_g_fox, v_w_out, v_ln1_g, v_ln1_b, v_ln2_g, v_ln2_b, v_w_gate_up, v_w_down):
    x2, tgt = x[0], loss_target[0]
    S, D = x2.shape
    W = D // 2
    n_pair = W // LANES
    n_fox = W // HEAD_DIM
    F = w_down.shape[1] * N_DEV

    (g_in,) = _exchange_call([w_in[0].astype(BF16)], False, "w_in_all_gather")
    w_in_full = g_in.transpose(1, 0, 2).reshape(D, -1)
    w_qkv = w_in_full[:, :6 * W]
    w_f = _pad_lanes(w_in_full[:, 6 * W:], LANES)
    gmat = _tri(D, lambda r, c: (r // HEAD_DIM) == (c // HEAD_DIM))
    g_cat = jnp.concatenate([g_sb, g_fox], axis=1)

    qkv, u, c, ksq = _proj_fwd(x2, w_qkv, w_f, _pad_lanes(b_f, LANES), n_fox)
    c_t = c[:, :n_fox].T
    c_rows = c_t.reshape(n_fox, S // FOX_BLOCK, FOX_BLOCK)
    kmax = jnp.sqrt(ksq[0:1]) * NORM_SLACK
    o_sb, st_sb, js_sb, g_out, g_gu, g_dn = _sb_fwd(
        qkv, n_pair, [w_out[0].astype(BF16), w_gate_up[0].astype(BF16), w_down[0].astype(BF16)])
    w_o = g_out.reshape(D, D)
    w_gu = g_gu.transpose(1, 0, 2).reshape(D, 2 * F)
    w_dn = g_dn.reshape(F, D)
    o_fx, st_fx, js_fx = _fox_fwd(qkv, c, c_t.reshape(n_fox, S // FOX_BK, FOX_BK), kmax, n_pair)
    js_fx = js_fx // (FOX_BLOCK // FOX_BK)
    h1, xh1, rs1, on_b, rr = _post_attn_fwd(o_sb, o_fx, x2, g_cat, gmat, w_o, ln1_g, ln1_b)
    gu, act_b, dyp, sm2 = _mlp_fwd(h1, tgt, w_gu, w_dn, ln2_g, ln2_b)

    dgu_b, dh1 = _mlp_bwd(gu, dyp, w_gu.T, w_dn.T)
    gw_gu = _matmul_tn(h1, [dgu_b], "grad_w_gate_up", n_split=2)
    gw_dn = _matmul_tn(act_b, [dyp], "grad_w_down")
    dxa, dmix_b, do_sb, do_fx, sm1, l_gu, l_dn = _post_attn_bwd(
        dh1, xh1, rs1, ln1_g, o_sb, o_fx, rr, g_cat, gmat, w_o.T,
        [gw_gu.astype(BF16).reshape(D, N_DEV, -1).transpose(1, 0, 2),
         gw_dn.astype(BF16).reshape(N_DEV, F // N_DEV, D)])
    dq_sb, dk_sb, dv_sb = _sb_bwd(qkv, do_sb, st_sb, js_sb, n_pair)
    dq_fx, dk_fx, dv_fx, dcr = _fox_bwd(qkv, do_fx, o_fx, st_fx, c, c_rows, js_fx, n_pair)
    dc = _pad_lanes(dcr.reshape(n_fox, S).T, LANES)
    pieces = [dq_sb, dk_sb, dv_sb, dq_fx, dk_fx, dv_fx]
    gw_qkv = _matmul_tn(x2, pieces, "grad_w_qkv", pad_cols=LANES)[:, :6 * W + n_fox]
    gw_out = _matmul_tn(on_b, [dmix_b], "grad_w_out")
    dx, gw_f, sm0, l_in, l_out = _proj_bwd(
        dxa, x2, pieces, dc, u, w_qkv.T, w_f.T, n_fox,
        [gw_qkv.astype(BF16).reshape(D, N_DEV, -1).transpose(1, 0, 2),
         gw_out.astype(BF16).reshape(N_DEV, D // N_DEV, D)])
    gw_f = gw_f[:, :n_fox]

    small = _pack_small(D, n_fox, sm1[2:3], sm1[0:1], sm1[1:2], sm2[0:1], sm2[1:2], sm0[0:1, :n_fox],
                        sm2[2:3] * (1.0 / D), gw_f.T)
    (l_small,) = _exchange_call([jnp.broadcast_to(small[None], (N_DEV,) + small.shape)], True, "small_exchange")

    zero = jnp.zeros((1, D), F32)
    pack = lambda gc, a, b_, c_, d_, bf: _pack_small(D, n_fox, gc, a, b_, c_, d_, bf, zero,
                                                     jnp.zeros((n_fox, D), F32))
    r_small = _reduce_adam_small(
        l_small,
        pack(g_cat, ln1_g, ln1_b, ln2_g, ln2_b, b_f),
        pack(jnp.concatenate([m_g_sb, m_g_fox], axis=1), m_ln1_g, m_ln1_b, m_ln2_g, m_ln2_b, m_b_f),
        pack(jnp.concatenate([v_g_sb, v_g_fox], axis=1), v_ln1_g, v_ln1_b, v_ln2_g, v_ln2_b, v_b_f))
    loss = r_small[4][0, 0]
    cols = w_in.shape[2]
    gf_cols = jnp.pad(r_small[0][8:8 + n_fox].T, ((0, 0), (cols - n_fox, 0)))
    extra = jnp.where(_my_index() == N_DEV - 1, gf_cols, 0.0)
    r_in = _reduce_adam(l_in, w_in[0], m_w_in[0], v_w_in[0], "reduce_adam_w_in", extra)
    r_out = _reduce_adam(l_out, w_out[0], m_w_out[0], v_w_out[0], "reduce_adam_w_out")
    r_gu = _reduce_adam(l_gu, w_gate_up[0], m_w_gate_up[0], v_w_gate_up[0], "reduce_adam_w_gate_up")
    r_dn = _reduce_adam(l_dn, w_down[0], m_w_down[0], v_w_down[0], "reduce_adam_w_down")

    def unpack(kind):
        big = [r_in[kind][None], None, None, None, r_out[kind][None], None, None, None, None,
               r_gu[kind][None], r_dn[kind][None]]
        s = r_small[kind]
        big[1] = s[5:6, :n_fox]
        big[2] = s[0:1, :W]
        big[3] = s[0:1, W:]
        big[5], big[6], big[7], big[8] = s[1:2], s[2:3], s[3:4], s[4:5]
        return big

    return (loss, dx[None], *unpack(0), *unpack(1), *unpack(2), *unpack(3))
```

```python
import functools

import jax
import jax.numpy as jnp
from jax import lax
from jax.experimental import pallas as pl
from jax.experimental.pallas import tpu as pltpu

F32 = jnp.float32
BF16 = jnp.bfloat16
I32 = jnp.int32

N_DEV = 8
HEAD_DIM = 64
LANES = 128
SCALE = HEAD_DIM ** -0.5
ALPHA = 2.0 ** 0.25
LN_EPS = 1e-5
RMS_EPS = 1e-6
ADAM_LR, ADAM_B1, ADAM_B2, ADAM_EPS, ADAM_WD, ADAM_STEP = 0.001, 0.9, 0.999, 1e-08, 0.01, 10
NEG_BIG = -1e30
NORM_SLACK = 1.01
EXP_ZERO = 88.5
VMEM_LIMIT = 60 * 1024 * 1024
ROW_TILE = 256
SB_BLOCK = 256
FOX_BLOCK = 512
FOX_BQ, FOX_BK = 512, 256
assert FOX_BQ == FOX_BLOCK and FOX_BLOCK % FOX_BK == 0
MESH = pl.DeviceIdType.MESH


def _cparams(sem):
    return pltpu.CompilerParams(dimension_semantics=sem, vmem_limit_bytes=VMEM_LIMIT)


def _dot(a, b):
    return jnp.dot(a, b, preferred_element_type=F32)


def _dot_nt(a, b):
    return lax.dot_general(a, b, (((1,), (1,)), ((), ())), preferred_element_type=F32)


def _dot_tn(a, b):
    return lax.dot_general(a, b, (((0,), (0,)), ((), ())), preferred_element_type=F32)


def _split2(a):
    hi = a.astype(BF16)
    lo = (a - hi.astype(F32)).astype(BF16)
    return hi, lo


def _split3(a):
    hi = a.astype(BF16)
    r1 = a - hi.astype(F32)
    mid = r1.astype(BF16)
    lo = (r1 - mid.astype(F32)).astype(BF16)
    return hi, mid, lo


def _dot_acc(a, m):
    hi, lo = _split2(a)
    return _dot(hi, m) + _dot(lo, m)


def _tri(n, fn):
    r = lax.broadcasted_iota(I32, (n, n), 0)
    c = lax.broadcasted_iota(I32, (n, n), 1)
    return fn(r, c).astype(BF16)


def _full(shape):
    nd = len(shape)
    return pl.BlockSpec(shape, lambda *_: (0,) * nd)


def _peer(k):
    x, y, c = lax.axis_index("x"), lax.axis_index("y"), lax.axis_index("c")
    return (1 - x if k & 4 else x, 1 - y if k & 2 else y, 1 - c if k & 1 else c)


def _my_index():
    return 4 * lax.axis_index("x") + 2 * lax.axis_index("y") + lax.axis_index("c")


class _Exchange:
    def __init__(self, n, scatter):
        self.n, self.scatter = n, scatter

    def sem_shapes(self):
        return [pltpu.SemaphoreType.DMA(((N_DEV - 1) * self.n,)), pltpu.SemaphoreType.DMA(((N_DEV - 1) * self.n,)),
                pltpu.SemaphoreType.DMA((self.n,))]

    def out_shapes(self, arrays):
        if self.scatter:
            return [jax.ShapeDtypeStruct(s.shape, s.dtype) for s in arrays]
        return [jax.ShapeDtypeStruct((N_DEV,) + s.shape, s.dtype) for s in arrays]

    def _copies(self, ins, outs, sems, landing):
        send_sems, recv_sems, local_sems = sems
        me = _my_index()
        src = lambda a, d: ins[a].at[d] if self.scatter else ins[a]
        local = [pltpu.make_async_copy(src(a, me), outs[a].at[me], local_sems.at[a]) for a in range(self.n)]
        remote = [pltpu.make_async_remote_copy(
            src_ref=src(a, me ^ k), dst_ref=outs[a].at[me ^ k if landing else me],
            send_sem=send_sems.at[(k - 1) * self.n + a], recv_sem=recv_sems.at[(k - 1) * self.n + a],
            device_id=_peer(k), device_id_type=MESH) for k in range(1, N_DEV) for a in range(self.n)]
        return local, remote

    def start(self, ins, outs, sems):
        local, sent = self._copies(ins, outs, sems, landing=False)
        for cp in local + sent:
            cp.start()

    def finish(self, ins, outs, sems):
        local, landed = self._copies(ins, outs, sems, landing=True)
        for cp in landed:
            cp.wait_recv()
        for cp in landed:
            cp.wait_send()
        for cp in local:
            cp.wait()


def _all_gather_two_level(shard, name):
    def body(x_ref, out_ref, send_sems, recv_sems, local_sem):
        x, y, c = lax.axis_index("x"), lax.axis_index("y"), lax.axis_index("c")
        me, sibling = (x, y, c), (x, y, 1 - c)
        chips = [(1 - x, y), (x, 1 - y), (1 - x, 1 - y)]

        def slot(px, py, pc):
            return out_ref.at[4 * px + 2 * py + pc]

        def copy(k, block, to, src=None):
            return pltpu.make_async_remote_copy(
                src_ref=slot(*block) if src is None else src, dst_ref=slot(*block),
                send_sem=send_sems.at[k], recv_sem=recv_sems.at[k], device_id=to, device_id_type=MESH)

        mine = pltpu.make_async_copy(x_ref, slot(*me), local_sem)
        mine.start()
        first = [copy(0, me, sibling, src=x_ref)]
        first += [copy(1 + j, me, (*chip, c), src=x_ref) for j, chip in enumerate(chips)]
        for cp in first:
            cp.start()
        passed = [copy(4 + j, (*chip, c), sibling) for j, chip in enumerate(chips)]
        for j, chip in enumerate(chips):
            copy(1 + j, (*chip, c), me).wait_recv()
            passed[j].start()
        copy(0, sibling, me).wait_recv()
        for j, chip in enumerate(chips):
            copy(4 + j, (*chip, 1 - c), me).wait_recv()
        for cp in first + passed:
            cp.wait_send()
        mine.wait()

    any_spec = pl.BlockSpec(memory_space=pl.ANY)
    return pl.pallas_call(
        body, name=name, out_shape=jax.ShapeDtypeStruct((N_DEV,) + shard.shape, shard.dtype),
        in_specs=[any_spec], out_specs=any_spec,
        scratch_shapes=[pltpu.SemaphoreType.DMA((N_DEV - 1,)), pltpu.SemaphoreType.DMA((N_DEV - 1,)),
                        pltpu.SemaphoreType.DMA],
    )(shard)


def _exchange_call(arrays, scatter, name):
    n = len(arrays)
    ex = _Exchange(n, scatter)

    def body(*refs):
        ins, outs, sems = refs[:n], refs[n:2 * n], refs[2 * n:]
        ex.start(ins, outs, sems)
        ex.finish(ins, outs, sems)

    any_spec = pl.BlockSpec(memory_space=pl.ANY)
    return pl.pallas_call(
        body, name=name, out_shape=ex.out_shapes(arrays),
        in_specs=[any_spec] * n, out_specs=[any_spec] * n, scratch_shapes=ex.sem_shapes(),
    )(*arrays)


def _log_sigmoid(u):
    return jnp.minimum(u, 0.0) - jnp.log1p(jnp.exp(-jnp.abs(u)))


def _proj_fwd(x, w_qkv, w_f, bf_pad, n_fox):
    S, D = x.shape
    N = w_qkv.shape[1]
    W = D // 2
    TM = ROW_TILE
    tri = _tri(TM, lambda r, c: c <= r)
    r_ = lax.broadcasted_iota(I32, (W, LANES), 0)
    c_ = lax.broadcasted_iota(I32, (W, LANES), 1)
    head_of = (r_ // HEAD_DIM == c_).astype(BF16)

    def body(x_ref, wq_ref, wf_ref, bf_ref, tri_ref, ho_ref, qkv_ref, u_ref, c_ref, ksq_ref, run_ref):
        @pl.when(pl.program_id(0) == 0)
        def _():
            run_ref[...] = jnp.zeros_like(run_ref)
            ksq_ref[...] = jnp.zeros_like(ksq_ref)

        xb = x_ref[...].astype(BF16)
        for n0 in range(0, N, D):
            chunk = _dot(xb, wq_ref[:, n0:n0 + D]).astype(BF16)
            qkv_ref[:, n0:n0 + D] = chunk
            if n0 == 4 * W:
                kf = chunk[:, :W].astype(F32)
                ksq = jnp.max(_dot_acc(kf * kf, ho_ref[...]), axis=0, keepdims=True)
                ksq_ref[...] = jnp.maximum(ksq_ref[...], ksq)
        u = _dot(xb, wf_ref[...]) + bf_ref[...]
        lane = lax.broadcasted_iota(I32, u.shape, 1)
        logf = jnp.where(lane < n_fox, _log_sigmoid(u), 0.0)
        u_ref[...] = u
        hi, mid, lo = _split3(logf)
        t = tri_ref[...]
        cs = _dot(t, hi) + _dot(t, mid) + _dot(t, lo) + run_ref[...]
        c_ref[...] = cs
        run_ref[...] = cs[TM - 1:TM, :]

    return pl.pallas_call(
        body, name="proj_fwd", grid=(S // TM,),
        in_specs=[pl.BlockSpec((TM, D), lambda i: (i, 0)), _full(w_qkv.shape), _full(w_f.shape),
                  _full(bf_pad.shape), _full(tri.shape), _full(head_of.shape)],
        out_specs=[pl.BlockSpec((TM, N), lambda i: (i, 0)), pl.BlockSpec((TM, LANES), lambda i: (i, 0)),
                   pl.BlockSpec((TM, LANES), lambda i: (i, 0)), _full((8, LANES))],
        out_shape=[jax.ShapeDtypeStruct((S, N), BF16), jax.ShapeDtypeStruct((S, LANES), F32),
                   jax.ShapeDtypeStruct((S, LANES), F32), jax.ShapeDtypeStruct((8, LANES), F32)],
        scratch_shapes=[pltpu.VMEM((1, LANES), F32)],
        compiler_params=_cparams(("arbitrary",)),
    )(x, w_qkv, w_f, bf_pad, tri, head_of)


def _post_attn_fwd(o_sb, o_fx, x, g_cat, gmat, w_out, ln_g, ln_b):
    S, D = x.shape
    H = D // 2
    TM = ROW_TILE

    def body(osb_ref, ofx_ref, x_ref, g_ref, gm_ref, wo_ref, lg_ref, lb_ref,
             h1_ref, xh_ref, rs_ref, on_ref, rr_ref):
        o = jnp.concatenate([osb_ref[...], ofx_ref[...]], axis=1)
        ms = _dot_acc(o * o, gm_ref[...]) * (1.0 / HEAD_DIM)
        r = lax.rsqrt(ms + RMS_EPS)
        onb = (o * r * g_ref[...]).astype(BF16)
        hp = ALPHA * x_ref[...] + _dot(onb, wo_ref[...])
        mu = jnp.mean(hp, axis=-1, keepdims=True)
        d = hp - mu
        rstd = lax.rsqrt(jnp.mean(d * d, axis=-1, keepdims=True) + LN_EPS)
        xh = d * rstd
        h1_ref[...] = xh * lg_ref[...] + lb_ref[...]
        xh_ref[...] = xh
        rs_ref[...] = jnp.broadcast_to(rstd, (TM, LANES))
        on_ref[...] = onb
        rr_ref[...] = r

    row = lambda w: pl.BlockSpec((TM, w), lambda i: (i, 0))
    return pl.pallas_call(
        body, name="post_attn_fwd", grid=(S // TM,),
        in_specs=[row(H), row(H), row(D), _full((1, D)), _full((D, D)), _full((D, D)), _full((1, D)), _full((1, D))],
        out_specs=[row(D), row(D), row(LANES), row(D), row(D)],
        out_shape=[jax.ShapeDtypeStruct((S, D), F32), jax.ShapeDtypeStruct((S, D), F32),
                   jax.ShapeDtypeStruct((S, LANES), F32), jax.ShapeDtypeStruct((S, D), BF16),
                   jax.ShapeDtypeStruct((S, D), F32)],
        compiler_params=_cparams(("arbitrary",)),
    )(o_sb, o_fx, x, g_cat, gmat, w_out, ln_g, ln_b)


def _ln_bwd(dxh, xh, rstd):
    m1 = jnp.mean(dxh, axis=-1, keepdims=True)
    m2 = jnp.mean(dxh * xh, axis=-1, keepdims=True)
    return rstd * (dxh - m1 - xh * m2)


def _mlp_fwd(h1, target, w_gu, w_dn, ln_g, ln_b):
    S, D = h1.shape
    F = w_dn.shape[0]
    TM = ROW_TILE
    FC = F // 2

    def body(h1_ref, tg_ref, wgu_hbm, wdn_hbm, lg_ref, lb_ref, gu_ref, act_ref, dyp_ref, sm_ref, wgu, wdn):
        @pl.when(pl.program_id(0) == 0)
        def _():
            pltpu.sync_copy(wgu_hbm, wgu)
            pltpu.sync_copy(wdn_hbm, wdn)
            sm_ref[...] = jnp.zeros_like(sm_ref)

        h1v = h1_ref[...]
        hb = h1v.astype(BF16)
        ff = jnp.zeros((TM, D), F32)
        for c0 in range(0, F, FC):
            g = _dot(hb, wgu[:, c0:c0 + FC])
            u = _dot(hb, wgu[:, F + c0:F + c0 + FC])
            gu_ref[:, c0:c0 + FC] = g
            gu_ref[:, F + c0:F + c0 + FC] = u
            ab = ((g * jax.nn.sigmoid(g)) * u).astype(BF16)
            act_ref[:, c0:c0 + FC] = ab
            ff = ff + _dot(ab, wdn[c0:c0 + FC, :])
        yp = ALPHA * h1v + ff
        mu = jnp.mean(yp, axis=-1, keepdims=True)
        d = yp - mu
        rstd = lax.rsqrt(jnp.mean(d * d, axis=-1, keepdims=True) + LN_EPS)
        xh = d * rstd
        err = (xh * lg_ref[...] + lb_ref[...]) - tg_ref[...]
        dy = err * (1.0 / D)
        sm_ref[0:1, :] += jnp.sum(dy * xh, axis=0, keepdims=True)
        sm_ref[1:2, :] += jnp.sum(dy, axis=0, keepdims=True)
        sm_ref[2:3, :] += jnp.sum(err * err, axis=0, keepdims=True)
        dyp_ref[...] = _ln_bwd(dy * lg_ref[...], xh, rstd)

    row = lambda w: pl.BlockSpec((TM, w), lambda i: (i, 0))
    hbm = pl.BlockSpec(memory_space=pl.ANY)
    return pl.pallas_call(
        body, name="mlp_fwd", grid=(S // TM,),
        in_specs=[row(D), row(D), hbm, hbm, _full((1, D)), _full((1, D))],
        out_specs=[row(2 * F), row(F), row(D), _full((8, D))],
        out_shape=[jax.ShapeDtypeStruct((S, 2 * F), F32), jax.ShapeDtypeStruct((S, F), BF16),
                   jax.ShapeDtypeStruct((S, D), F32), jax.ShapeDtypeStruct((8, D), F32)],
        scratch_shapes=[pltpu.VMEM(w_gu.shape, BF16), pltpu.VMEM(w_dn.shape, BF16)],
        compiler_params=_cparams(("arbitrary",)),
    )(h1, target, w_gu, w_dn, ln_g, ln_b)


def _mlp_bwd(gu, dyp, w_guT, w_dnT):
    S, D = dyp.shape
    F = w_dnT.shape[1]
    TM = ROW_TILE
    FC = F // 2

    def body(gu_ref, dyp_ref, wguT_hbm, wdnT_hbm, dgu_ref, dh1_ref, wguT, wdnT):
        @pl.when(pl.program_id(0) == 0)
        def _():
            pltpu.sync_copy(wguT_hbm, wguT)
            pltpu.sync_copy(wdnT_hbm, wdnT)

        dypv = dyp_ref[...]
        db = dypv.astype(BF16)
        dh1 = ALPHA * dypv
        for c0 in range(0, F, FC):
            dact = _dot(db, wdnT[:, c0:c0 + FC])
            g = gu_ref[:, c0:c0 + FC]
            u = gu_ref[:, F + c0:F + c0 + FC]
            sg = jax.nn.sigmoid(g)
            dgb = (dact * u * (sg * (1.0 + g * (1.0 - sg)))).astype(BF16)
            dub = (dact * (g * sg)).astype(BF16)
            dgu_ref[:, c0:c0 + FC] = dgb
            dgu_ref[:, F + c0:F + c0 + FC] = dub
            dh1 = dh1 + _dot(dgb, wguT[c0:c0 + FC, :]) + _dot(dub, wguT[F + c0:F + c0 + FC, :])
        dh1_ref[...] = dh1

    row = lambda w: pl.BlockSpec((TM, w), lambda i: (i, 0))
    hbm = pl.BlockSpec(memory_space=pl.ANY)
    return pl.pallas_call(
        body, name="mlp_bwd", grid=(S // TM,),
        in_specs=[row(2 * F), row(D), hbm, hbm],
        out_specs=[row(2 * F), row(D)],
        out_shape=[jax.ShapeDtypeStruct((S, 2 * F), BF16), jax.ShapeDtypeStruct((S, D), F32)],
        scratch_shapes=[pltpu.VMEM(w_guT.shape, BF16), pltpu.VMEM(w_dnT.shape, BF16)],
        compiler_params=_cparams(("arbitrary",)),
    )(gu, dyp, w_guT, w_dnT)


def _post_attn_bwd(dh1, xh, rs, ln_g, o_sb, o_fx, rr, g_cat, gmat, w_outT, chunked):
    S, D = dh1.shape
    H = D // 2
    TM = ROW_TILE
    nT = S // TM
    n_ch = len(chunked)
    ex = _Exchange(n_ch, scatter=True)

    def body(*refs):
        dh1_ref, xh_ref, rs_ref, lg_ref, osb_ref, ofx_ref, rr_ref, g_ref, gm_ref, woT_ref = refs[:10]
        ch_in = refs[10:10 + n_ch]
        dxa_ref, dmix_ref, dosb_ref, dofx_ref, sm_ref = refs[10 + n_ch:15 + n_ch]
        ch_out = refs[15 + n_ch:15 + 2 * n_ch]
        sems = refs[15 + 2 * n_ch:]

        @pl.when(pl.program_id(0) == 0)
        def _():
            ex.start(ch_in, ch_out, sems)
            sm_ref[...] = jnp.zeros_like(sm_ref)

        dh = dh1_ref[...]
        xhv = xh_ref[...]
        dhp = _ln_bwd(dh * lg_ref[...], xhv, rs_ref[:, 0:1])
        dxa_ref[...] = ALPHA * dhp
        dmb = dhp.astype(BF16)
        dmix_ref[...] = dmb
        don = _dot(dmb, woT_ref[...])
        o = jnp.concatenate([osb_ref[...], ofx_ref[...]], axis=1)
        r = rr_ref[...]
        u = don * g_ref[...]
        t = _dot_acc(u * o, gm_ref[...]) * (1.0 / HEAD_DIM)
        do = r * u - o * (r * r * r) * t
        dosb_ref[...] = do[:, :H]
        dofx_ref[...] = do[:, H:]
        sm_ref[0:1, :] += jnp.sum(dh * xhv, axis=0, keepdims=True)
        sm_ref[1:2, :] += jnp.sum(dh, axis=0, keepdims=True)
        sm_ref[2:3, :] += jnp.sum(don * o * r, axis=0, keepdims=True)

        @pl.when(pl.program_id(0) == nT - 1)
        def _():
            ex.finish(ch_in, ch_out, sems)

    row = lambda w: pl.BlockSpec((TM, w), lambda i: (i, 0))
    any_spec = pl.BlockSpec(memory_space=pl.ANY)
    return pl.pallas_call(
        body, name="post_attn_bwd", grid=(nT,),
        in_specs=[row(D), row(D), row(LANES), _full((1, D)), row(H), row(H), row(D), _full((1, D)),
                  _full((D, D)), _full((D, D))] + [any_spec] * n_ch,
        out_specs=[row(D), row(D), row(H), row(H), _full((8, D))] + [any_spec] * n_ch,
        out_shape=[jax.ShapeDtypeStruct((S, D), F32), jax.ShapeDtypeStruct((S, D), BF16),
                   jax.ShapeDtypeStruct((S, H), F32), jax.ShapeDtypeStruct((S, H), F32),
                   jax.ShapeDtypeStruct((8, D), F32)] + ex.out_shapes(chunked),
        scratch_shapes=ex.sem_shapes(),
        compiler_params=_cparams(("arbitrary",)),
    )(dh1, xh, rs, ln_g, o_sb, o_fx, rr, g_cat, gmat, w_outT, *chunked)


def _proj_bwd(dxa, x, pieces, dc, u, w_qkvT, w_fT, n_fox, chunked):
    S, D = dxa.shape
    H = D // 2
    TM = ROW_TILE
    nT = S // TM
    tri = _tri(TM, lambda r, c: c >= r)
    n_p = len(pieces)
    n_ch = len(chunked)
    ex = _Exchange(n_ch, scatter=True)

    def body(*refs):
        dxa_ref, x_ref = refs[:2]
        p_refs = refs[2:2 + n_p]
        dc_ref, u_ref, wq_ref, wf_ref, tri_ref = refs[2 + n_p:7 + n_p]
        ch_in = refs[7 + n_p:7 + n_p + n_ch]
        dx_ref, gwf_ref, sm_ref = refs[7 + n_p + n_ch:10 + n_p + n_ch]
        ch_out = refs[10 + n_p + n_ch:10 + n_p + 2 * n_ch]
        run_ref = refs[10 + n_p + 2 * n_ch]
        sems = refs[11 + n_p + 2 * n_ch:]

        @pl.when(pl.program_id(0) == 0)
        def _():
            ex.start(ch_in, ch_out, sems)
            run_ref[...] = jnp.zeros_like(run_ref)
            sm_ref[...] = jnp.zeros_like(sm_ref)
            gwf_ref[...] = jnp.zeros_like(gwf_ref)

        hi, mid, lo = _split3(dc_ref[...])
        t = tri_ref[...]
        dlogf = _dot(t, hi) + _dot(t, mid) + _dot(t, lo) + run_ref[...]
        run_ref[...] = dlogf[0:1, :]
        uv = u_ref[...]
        lane = lax.broadcasted_iota(I32, uv.shape, 1)
        df = jnp.where(lane < n_fox, dlogf * jax.nn.sigmoid(-uv), 0.0)
        sm_ref[0:1, :] += jnp.sum(df, axis=0, keepdims=True)
        dfb = df.astype(BF16)
        gwf_ref[...] += _dot_tn(x_ref[...].astype(BF16), dfb)
        acc = dxa_ref[...] + _dot(dfb, wf_ref[...])
        for a in range(n_p):
            acc = acc + _dot(p_refs[a][...].astype(BF16), wq_ref[a * H:(a + 1) * H, :])
        dx_ref[...] = acc

        @pl.when(pl.program_id(0) == nT - 1)
        def _():
            ex.finish(ch_in, ch_out, sems)

    rev = lambda w: pl.BlockSpec((TM, w), lambda i: (nT - 1 - i, 0))
    any_spec = pl.BlockSpec(memory_space=pl.ANY)
    return pl.pallas_call(
        body, name="proj_bwd", grid=(nT,),
        in_specs=[rev(D), rev(D)] + [rev(H)] * n_p + [rev(LANES), rev(LANES), _full(w_qkvT.shape),
                                                       _full(w_fT.shape), _full(tri.shape)] + [any_spec] * n_ch,
        out_specs=[rev(D), _full((D, LANES)), _full((8, LANES))] + [any_spec] * n_ch,
        out_shape=[jax.ShapeDtypeStruct((S, D), F32), jax.ShapeDtypeStruct((D, LANES), F32),
                   jax.ShapeDtypeStruct((8, LANES), F32)] + ex.out_shapes(chunked),
        scratch_shapes=[pltpu.VMEM((1, LANES), F32)] + ex.sem_shapes(),
        compiler_params=_cparams(("arbitrary",)),
    )(dxa, x, *pieces, dc, u, w_qkvT, w_fT, tri, *chunked)


def _matmul_tn(a, bs, name, n_split=1, pad_cols=0):
    S, M = a.shape
    widths = [b.shape[1] for b in bs]
    N = sum(widths) + pad_cols
    assert n_split == 1 or (len(bs) == 1 and pad_cols == 0)
    TK = 512 if S % 512 == 0 else ROW_TILE
    MC = 512 if M % 512 == 0 else 256
    nb = len(bs)

    def body(*refs):
        a_ref, b_refs, o_ref, acc_ref = refs[0], refs[1:1 + nb], refs[1 + nb], refs[2 + nb]

        @pl.when(pl.program_id(1) == 0)
        def _():
            acc_ref[...] = jnp.zeros_like(acc_ref)

        n0 = 0
        for b_ref in b_refs:
            bv = b_ref[...].astype(BF16)
            w = bv.shape[1]
            for m0 in range(0, M, MC):
                acc_ref[m0:m0 + MC, n0:n0 + w] += _dot_tn(a_ref[:, m0:m0 + MC].astype(BF16), bv)
            n0 += w

        @pl.when(pl.program_id(1) == S // TK - 1)
        def _():
            o_ref[...] = acc_ref[...].astype(BF16)

    return pl.pallas_call(
        body, name=name, grid=(n_split, S // TK),
        in_specs=[pl.BlockSpec((TK, M), lambda n, k: (k, 0))]
        + [pl.BlockSpec((TK, w // n_split), lambda n, k: (k, n)) for w in widths],
        out_specs=pl.BlockSpec((M, N // n_split), lambda n, k: (0, n)),
        out_shape=jax.ShapeDtypeStruct((M, N), BF16),
        scratch_shapes=[pltpu.VMEM((M, N // n_split), F32)],
        compiler_params=_cparams(("arbitrary", "arbitrary")),
    )(a, *bs)


def _half_mask(half):
    lane = lax.broadcasted_iota(I32, (1, LANES), 1)
    return (lane >= half * HEAD_DIM) & (lane < half * HEAD_DIM + HEAD_DIM)


def _lane_tile(a, width):
    return jnp.concatenate([a] * (width // LANES), axis=1)


def _softplus_neg_abs(z):
    return jnp.log1p(jnp.exp(-jnp.abs(z)))


def _sb_fwd(qkv, n_pair, shards):
    S = qkv.shape[0]
    B = SB_BLOCK
    nq = S // B
    us = _tri(B, lambda r, c: r > c)
    n_sh = len(shards)
    ex = _Exchange(n_sh, scatter=False)

    def body(*refs):
        q_ref, k_ref, v_ref, us_ref = refs[:4]
        sh_in = refs[4:4 + n_sh]
        o_ref, st_ref, js_ref = refs[4 + n_sh:7 + n_sh]
        sh_out = refs[7 + n_sh:7 + 2 * n_sh]
        acc_ref, r_ref = refs[7 + 2 * n_sh:9 + 2 * n_sh]
        sems = refs[9 + 2 * n_sh:]
        p, i = pl.program_id(0), pl.program_id(1)

        @pl.when((p == 0) & (i == 0))
        def _():
            ex.start(sh_in, sh_out, sems)

        hms = [_half_mask(h) for h in range(2)]
        qv = q_ref[...]
        qss = [jnp.where(hm, qv, jnp.zeros_like(qv)) * SCALE for hm in hms]
        row = lax.broadcasted_iota(I32, (B, B), 0)
        col = lax.broadcasted_iota(I32, (B, B), 1)
        tri = col < row
        acc_ref[...] = jnp.zeros_like(acc_ref)
        r_ref[...] = jnp.zeros_like(r_ref)

        def block(j, diag):
            off = pl.multiple_of(j * B, B)
            kj = k_ref[pl.ds(off, B), :]
            vj = v_ref[pl.ds(off, B), :]
            zs = [_dot_nt(qss[h], kj) for h in range(2)]
            sps = [_softplus_neg_abs(z) for z in zs]
            bs = [jnp.minimum(-z, 0.0) - sp for z, sp in zip(zs, sps)]
            if diag:
                bs = [jnp.where(tri, b, 0.0) for b in bs]
            lexcs = [_dot_acc(b, us_ref[...]) for b in bs]
            ws = [jnp.exp(jnp.minimum(z, 0.0) - sp + (_lane_tile(r_ref[h], B) + lexc))
                  for h, (z, sp, lexc) in enumerate(zip(zs, sps, lexcs))]
            if diag:
                ws = [jnp.where(tri, w, 0.0) for w in ws]
            for h in range(2):
                acc_ref[h] += _dot(ws[h].astype(BF16), vj)
                r_ref[h] += jnp.broadcast_to(lexcs[h][:, 0:1] + bs[h][:, 0:1], (B, LANES))

        def live():
            return (jnp.max(r_ref[...]) > -EXP_ZERO).astype(I32)

        block(i, True)

        def step(carry):
            j, _ = carry
            block(j, False)
            return j - 1, live()

        j_end, _ = lax.while_loop(lambda c: (c[0] >= 0) & (c[1] > 0), step, (i - 1, live()))
        js_ref[2 * p, i] = j_end + 1
        js_ref[2 * p + 1, i] = j_end + 1
        o_ref[...] = jnp.where(hms[0], acc_ref[0], acc_ref[1])
        st_ref[...] = r_ref[...]

        @pl.when((p == n_pair - 1) & (i == nq - 1))
        def _():
            ex.finish(sh_in, sh_out, sems)

    any_spec = pl.BlockSpec(memory_space=pl.ANY)
    return pl.pallas_call(
        body, name="sb_attn_fwd", grid=(n_pair, nq),
        in_specs=[pl.BlockSpec((B, LANES), lambda p, i: (i, p)),
                  pl.BlockSpec((S, LANES), lambda p, i: (0, n_pair + p)),
                  pl.BlockSpec((S, LANES), lambda p, i: (0, 2 * n_pair + p)),
                  _full((B, B))] + [any_spec] * n_sh,
        out_specs=[pl.BlockSpec((B, LANES), lambda p, i: (i, p)),
                   pl.BlockSpec((2, B, LANES), lambda p, i: (p, i, 0)),
                   pl.BlockSpec(memory_space=pltpu.SMEM)] + [any_spec] * n_sh,
        out_shape=[jax.ShapeDtypeStruct((S, n_pair * LANES), F32),
                   jax.ShapeDtypeStruct((2 * n_pair, S, LANES), F32),
                   jax.ShapeDtypeStruct((2 * n_pair, nq), I32)] + ex.out_shapes(shards),
        scratch_shapes=[pltpu.VMEM((2, B, LANES), F32), pltpu.VMEM((2, B, LANES), F32)] + ex.sem_shapes(),
        compiler_params=_cparams(("arbitrary", "arbitrary")),
    )(qkv, qkv, qkv, us, *shards)


def _sb_bwd(qkv, do, st, js, n_pair):
    S = qkv.shape[0]
    B = SB_BLOCK
    nq = S // B
    us = _tri(B, lambda r, c: r > c)
    ti = _tri(B, lambda r, c: r <= c)

    def body(js_ref, q_ref, k_ref, v_ref, do_ref, st_ref, us_ref, ti_ref, dq_ref, dk_ref, dv_ref,
             dqa_ref, pr_ref, er_ref):
        p, i = pl.program_id(0), pl.program_id(1)

        @pl.when(i == 0)
        def _():
            dk_ref[...] = jnp.zeros_like(dk_ref)
            dv_ref[...] = jnp.zeros_like(dv_ref)

        hms = [_half_mask(h) for h in range(2)]
        qv = q_ref[...]
        dov = do_ref[...]
        qss = [jnp.where(hm, qv, jnp.zeros_like(qv)) * SCALE for hm in hms]
        dobs = [jnp.where(hm, dov, 0.0).astype(BF16) for hm in hms]
        row = lax.broadcasted_iota(I32, (B, B), 0)
        col = lax.broadcasted_iota(I32, (B, B), 1)
        tri = col < row
        dqa_ref[...] = jnp.zeros_like(dqa_ref)
        pr_ref[...] = jnp.zeros_like(pr_ref)
        er_ref[...] = jnp.zeros_like(er_ref)

        def block(j, diag):
            off = pl.multiple_of(j * B, B)
            kj = k_ref[pl.ds(off, B), :]
            vj = v_ref[pl.ds(off, B), :]
            hs = range(2)
            zs = [_dot_nt(qss[h], kj) for h in hs]
            dws = [_dot_nt(dobs[h], vj) for h in hs]
            sps = [_softplus_neg_abs(z) for z in zs]
            bs = [jnp.minimum(-z, 0.0) - sp for z, sp in zip(zs, sps)]
            if diag:
                bs = [jnp.where(tri, b, 0.0) for b in bs]
            lexcs = [_dot_acc(b, us_ref[...]) for b in bs]
            ws = []
            for h in hs:
                pr_new = pr_ref[h] + jnp.broadcast_to(lexcs[h][:, 0:1] + bs[h][:, 0:1], (B, LANES))
                pr_ref[h] = pr_new
                w = jnp.exp(jnp.minimum(zs[h], 0.0) - sps[h] + (_lane_tile(st_ref[h] - pr_new, B) + lexcs[h]))
                ws.append(jnp.where(tri, w, 0.0) if diag else w)
            es = [dw * w for dw, w in zip(dws, ws)]
            eincs = [_dot_acc(e, ti_ref[...]) for e in es]
            dzbs = []
            for h in hs:
                er = er_ref[h]
                big_e = _lane_tile(er, B) + (eincs[h] - es[h])
                er_ref[h] = er + jnp.broadcast_to(eincs[h][:, B - 1:B], (B, LANES))
                eb = jnp.exp(bs[h])
                dzbs.append((es[h] * eb - big_e * (1.0 - eb)).astype(BF16))
            for h in hs:
                dqa_ref[h] += _dot(dzbs[h], kj)
            dk_ref[pl.ds(off, B), :] += _dot_tn(dzbs[0], qss[0]) + _dot_tn(dzbs[1], qss[1])
            dv_ref[pl.ds(off, B), :] += (_dot_tn(ws[0].astype(BF16), dobs[0])
                                         + _dot_tn(ws[1].astype(BF16), dobs[1]))

        def step(j, carry):
            block(j, False)
            return carry

        lax.fori_loop(js_ref[2 * p, i], i, step, 0)
        block(i, True)
        dq_ref[...] = jnp.where(hms[0], dqa_ref[0], dqa_ref[1]) * SCALE

    W = n_pair * LANES
    return pl.pallas_call(
        body, name="sb_attn_bwd",
        grid_spec=pltpu.PrefetchScalarGridSpec(
            num_scalar_prefetch=1, grid=(n_pair, nq),
            in_specs=[pl.BlockSpec((B, LANES), lambda p, i, js: (i, p)),
                      pl.BlockSpec((S, LANES), lambda p, i, js: (0, n_pair + p)),
                      pl.BlockSpec((S, LANES), lambda p, i, js: (0, 2 * n_pair + p)),
                      pl.BlockSpec((B, LANES), lambda p, i, js: (i, p)),
                      pl.BlockSpec((2, B, LANES), lambda p, i, js: (p, i, 0)),
                      pl.BlockSpec((B, B), lambda p, i, js: (0, 0)),
                      pl.BlockSpec((B, B), lambda p, i, js: (0, 0))],
            out_specs=[pl.BlockSpec((B, LANES), lambda p, i, js: (i, p)),
                       pl.BlockSpec((S, LANES), lambda p, i, js: (0, p)),
                       pl.BlockSpec((S, LANES), lambda p, i, js: (0, p))],
            scratch_shapes=[pltpu.VMEM((2, B, LANES), F32), pltpu.VMEM((2, B, LANES), F32),
                            pltpu.VMEM((2, B, LANES), F32)]),
        out_shape=[jax.ShapeDtypeStruct((S, W), F32)] * 3,
        compiler_params=_cparams(("arbitrary", "arbitrary")),
    )(js, qkv, qkv, qkv, do, st, us, ti)


def _head_column(blk, head):
    lane = lax.broadcasted_iota(I32, (1, LANES), 1)
    return jnp.sum(jnp.where(lane == head, blk, 0.0), axis=1, keepdims=True)


def _fox_fwd(qkv, c, c_rows, kmax, n_pair):
    S = qkv.shape[0]
    BQ, BK = FOX_BQ, FOX_BK
    R = BQ // BK
    nq = S // BQ

    def body(q_ref, k_ref, v_ref, c_ref, cr_ref, km_ref, o_ref, st_ref, js_ref, acc_ref, m_ref, cb_ref, qkb_ref):
        p, i, half = pl.program_id(0), pl.program_id(1), pl.program_id(2)
        hm = _half_mask(half)
        qv = q_ref[...]
        qs = jnp.where(hm, qv, jnp.zeros_like(qv)) * SCALE
        ccol = _head_column(c_ref[...], 2 * p + half)
        cb_ref[...] = jnp.broadcast_to(ccol, (BQ, BK))
        qf = qs.astype(F32)
        qkb_ref[...] = jnp.broadcast_to(
            jnp.sqrt(jnp.sum(qf * qf, axis=1, keepdims=True)) * NORM_SLACK
            * _head_column(km_ref[...], 2 * p + half) + ccol, (BQ, LANES))
        row = lax.broadcasted_iota(I32, (BQ, BK), 0)
        col = lax.broadcasted_iota(I32, (BQ, BK), 1)
        acc_ref[...] = jnp.zeros_like(acc_ref)
        m_ref[...] = jnp.full_like(m_ref, NEG_BIG)

        def blocks(j_top, diag):
            ss, v1s, keeps = [], [], []
            for d in range(R):
                j = j_top - d
                off = pl.multiple_of(j * BK, BK)
                vj = v_ref[pl.ds(off, BK), :]
                v1s.append(jnp.where(hm, vj, jnp.ones_like(vj)))
                s = _dot_nt(qs, k_ref[pl.ds(off, BK), :]) + (cb_ref[...] - cr_ref[0, pl.ds(j, 1), :])
                if diag:
                    keeps.append(col + (R - 1 - d) * BK <= row)
                    s = jnp.where(keeps[-1], s, NEG_BIG)
                ss.append(s)
            m_old = m_ref[...]
            s_max = jnp.max(functools.reduce(jnp.maximum, ss), axis=1, keepdims=True)
            m_new = jnp.maximum(m_old, jnp.broadcast_to(s_max, (BQ, LANES)))
            m_wide = _lane_tile(m_new, BK)
            pvs = [jnp.exp(s - m_wide) for s in ss]
            if diag:
                pvs = [jnp.where(keep, pv, 0.0) for keep, pv in zip(keeps, pvs)]
            new = _dot(pvs[0].astype(BF16), v1s[0])
            for pv, v1 in zip(pvs[1:], v1s[1:]):
                new = new + _dot(pv.astype(BF16), v1)
            acc_ref[...] = jnp.exp(m_old - m_new) * acc_ref[...] + new
            m_ref[...] = m_new

        def live(j):
            c_end = cr_ref[0, pl.ds(jnp.maximum(j, 0), 1), :][:, BK - 1:BK]
            return (jnp.max(qkb_ref[...] - c_end - m_ref[...]) > -EXP_ZERO).astype(I32)

        blocks(R * i + R - 1, True)

        def step(carry):
            j, _ = carry
            go_on = live(j - R)
            blocks(j, False)
            return j - R, go_on

        j_end, _ = lax.while_loop(lambda cr: (cr[0] >= 0) & (cr[1] > 0), step, (R * i - 1, live(R * i - 1)))
        js_ref[2 * p + half, i] = j_end + 1
        acc = acc_ref[...]
        denom = jnp.where(hm, pltpu.roll(acc, HEAD_DIM, 1), acc)
        res = jnp.where(hm, acc / denom, 0.0)

        @pl.when(half == 0)
        def _():
            o_ref[...] = res

        @pl.when(half == 1)
        def _():
            o_ref[...] += res

        st_ref[0] = m_ref[...] + jnp.log(denom)

    return pl.pallas_call(
        body, name="fox_attn_fwd", grid=(n_pair, nq, 2),
        in_specs=[pl.BlockSpec((BQ, LANES), lambda p, i, h: (i, 3 * n_pair + p)),
                  pl.BlockSpec((S, LANES), lambda p, i, h: (0, 4 * n_pair + p)),
                  pl.BlockSpec((S, LANES), lambda p, i, h: (0, 5 * n_pair + p)),
                  pl.BlockSpec((BQ, LANES), lambda p, i, h: (i, 0)),
                  pl.BlockSpec((1, S // BK, BK), lambda p, i, h: (2 * p + h, 0, 0)),
                  _full((1, LANES))],
        out_specs=[pl.BlockSpec((BQ, LANES), lambda p, i, h: (i, p)),
                   pl.BlockSpec((1, BQ, LANES), lambda p, i, h: (2 * p + h, i, 0)),
                   pl.BlockSpec(memory_space=pltpu.SMEM)],
        out_shape=[jax.ShapeDtypeStruct((S, n_pair * LANES), F32),
                   jax.ShapeDtypeStruct((2 * n_pair, S, LANES), F32),
                   jax.ShapeDtypeStruct((2 * n_pair, nq), I32)],
        scratch_shapes=[pltpu.VMEM((BQ, LANES), F32), pltpu.VMEM((BQ, LANES), F32), pltpu.VMEM((BQ, BK), F32),
                        pltpu.VMEM((BQ, LANES), F32)],
        compiler_params=_cparams(("arbitrary", "arbitrary", "arbitrary")),
    )(qkv, qkv, qkv, c, c_rows, kmax)


def _fox_bwd(qkv, do, o, st, c, c_rows, js, n_pair):
    S = qkv.shape[0]
    B = FOX_BLOCK
    nq = S // B

    def body(js_ref, q_ref, k_ref, v_ref, do_ref, o_ref, st_ref, c_ref, cr_ref, dq_ref, dk_ref, dv_ref,
             dc_ref, dqa_ref, rs_ref, cb_ref, db_ref):
        p, i, half = pl.program_id(0), pl.program_id(1), pl.program_id(2)

        @pl.when((i == 0) & (half == 0))
        def _():
            dk_ref[...] = jnp.zeros_like(dk_ref)
            dv_ref[...] = jnp.zeros_like(dv_ref)
            dc_ref[...] = jnp.zeros_like(dc_ref)

        hm = _half_mask(half)
        qv = q_ref[...]
        qs = jnp.where(hm, qv, jnp.zeros_like(qv)) * SCALE
        dov = jnp.where(hm, do_ref[...], 0.0)
        dob = dov.astype(BF16)
        cb_ref[...] = jnp.broadcast_to(_head_column(c_ref[...], 2 * p + half), (B, LANES)) - st_ref[0]
        db_ref[...] = jnp.broadcast_to(jnp.sum(dov * o_ref[...], axis=1, keepdims=True), (B, LANES))
        row = lax.broadcasted_iota(I32, (B, B), 0)
        col = lax.broadcasted_iota(I32, (B, B), 1)
        dqa_ref[...] = jnp.zeros_like(dqa_ref)
        rs_ref[...] = jnp.zeros_like(rs_ref)

        def block(j, diag):
            off = pl.multiple_of(j * B, B)
            kj = k_ref[pl.ds(off, B), :]
            vj = v_ref[pl.ds(off, B), :]
            pv = jnp.exp(_dot_nt(qs, kj) + (_lane_tile(cb_ref[...], B) - cr_ref[0, pl.ds(j, 1), :]))
            if diag:
                pv = jnp.where(col <= row, pv, 0.0)
            ds = pv * (_dot_nt(dob, vj) - _lane_tile(db_ref[...], B))
            dsb = ds.astype(BF16)
            dqa_ref[...] += _dot(dsb, kj)
            dk_ref[pl.ds(off, B), :] += _dot_tn(dsb, qs)
            dv_ref[pl.ds(off, B), :] += _dot_tn(pv.astype(BF16), dob)
            dc_ref[0, half, pl.ds(j, 1), :] -= jnp.sum(ds, axis=0, keepdims=True)
            rs_ref[...] += jnp.sum(ds, axis=1, keepdims=True)

        def step(j, carry):
            block(j, False)
            return carry

        lax.fori_loop(js_ref[2 * p + half, i], i, step, 0)
        block(i, True)
        res = jnp.where(hm, dqa_ref[...] * SCALE, 0.0)

        @pl.when(half == 0)
        def _():
            dq_ref[...] = res

        @pl.when(half == 1)
        def _():
            dq_ref[...] += res

        dc_ref[0, half, pl.ds(i, 1), :] += jnp.transpose(jnp.broadcast_to(rs_ref[...], (B, LANES)))[0:1, :]

    W = n_pair * LANES
    return pl.pallas_call(
        body, name="fox_attn_bwd",
        grid_spec=pltpu.PrefetchScalarGridSpec(
            num_scalar_prefetch=1, grid=(n_pair, nq, 2),
            in_specs=[pl.BlockSpec((B, LANES), lambda p, i, h, js: (i, 3 * n_pair + p)),
                      pl.BlockSpec((S, LANES), lambda p, i, h, js: (0, 4 * n_pair + p)),
                      pl.BlockSpec((S, LANES), lambda p, i, h, js: (0, 5 * n_pair + p)),
                      pl.BlockSpec((B, LANES), lambda p, i, h, js: (i, p)),
                      pl.BlockSpec((B, LANES), lambda p, i, h, js: (i, p)),
                      pl.BlockSpec((1, B, LANES), lambda p, i, h, js: (2 * p + h, i, 0)),
                      pl.BlockSpec((B, LANES), lambda p, i, h, js: (i, 0)),
                      pl.BlockSpec((1, nq, B), lambda p, i, h, js: (2 * p + h, 0, 0))],
            out_specs=[pl.BlockSpec((B, LANES), lambda p, i, h, js: (i, p)),
                       pl.BlockSpec((S, LANES), lambda p, i, h, js: (0, p)),
                       pl.BlockSpec((S, LANES), lambda p, i, h, js: (0, p)),
                       pl.BlockSpec((1, 2, nq, B), lambda p, i, h, js: (p, 0, 0, 0))],
            scratch_shapes=[pltpu.VMEM((B, LANES), F32), pltpu.VMEM((B, 1), F32), pltpu.VMEM((B, LANES), F32),
                            pltpu.VMEM((B, LANES), F32)]),
        out_shape=[jax.ShapeDtypeStruct((S, W), F32)] * 3 + [jax.ShapeDtypeStruct((n_pair, 2, nq, B), F32)],
        compiler_params=_cparams(("arbitrary", "arbitrary", "arbitrary")),
    )(js, qkv, qkv, qkv, do, o, st, c, c_rows)


def _adam(w, g, m, v):
    m = ADAM_B1 * m + (1.0 - ADAM_B1) * g
    v = ADAM_B2 * v + (1.0 - ADAM_B2) * (g * g)
    m_hat = m / (1.0 - ADAM_B1 ** ADAM_STEP)
    v_hat = v / (1.0 - ADAM_B2 ** ADAM_STEP)
    delta = -ADAM_LR * (m_hat / (jnp.sqrt(v_hat) + ADAM_EPS) + ADAM_WD * w)
    return delta, m, v


def _reduce_adam(landing, w, m, v, name, extra=None):
    R, C = w.shape
    TR = next(t for t in (256, 128, R) if R % t == 0)
    more = [] if extra is None else [extra]

    def body(*refs):
        l_ref, w_ref, m_ref, v_ref = refs[:4]
        g_ref, d_ref, nm_ref, nv_ref = refs[4 + len(more):]
        g = l_ref[0].astype(F32)
        for s in range(1, N_DEV):
            g = g + l_ref[s].astype(F32)
        if more:
            g = g + refs[4][...]
        d, nm, nv = _adam(w_ref[...], g, m_ref[...], v_ref[...])
        g_ref[...] = g
        d_ref[...] = d
        nm_ref[...] = nm
        nv_ref[...] = nv

    blk = pl.BlockSpec((TR, C), lambda i: (i, 0))
    return pl.pallas_call(
        body, name=name, grid=(R // TR,),
        in_specs=[pl.BlockSpec((N_DEV, TR, C), lambda i: (0, i, 0)), blk, blk, blk] + [blk] * len(more),
        out_specs=[blk] * 4,
        out_shape=[jax.ShapeDtypeStruct((R, C), F32)] * 4,
        compiler_params=_cparams(("arbitrary",)),
    )(landing, w, m, v, *more)


def _reduce_adam_small(landing, w, m, v):
    R, C = w.shape

    def body(l_ref, w_ref, m_ref, v_ref, g_ref, d_ref, nm_ref, nv_ref, loss_ref):
        g = l_ref[0]
        for s in range(1, N_DEV):
            g = g + l_ref[s]
        d, nm, nv = _adam(w_ref[...], g, m_ref[...], v_ref[...])
        g_ref[...] = g
        d_ref[...] = d
        nm_ref[...] = nm
        nv_ref[...] = nv
        loss_ref[...] = jnp.broadcast_to(0.5 * jnp.sum(g[7:8, :], axis=1, keepdims=True), (1, LANES))

    return pl.pallas_call(
        body, name="reduce_adam_small",
        out_shape=[jax.ShapeDtypeStruct((R, C), F32)] * 4 + [jax.ShapeDtypeStruct((1, LANES), F32)],
    )(landing, w, m, v)


def _pad_lanes(a, width):
    return jnp.pad(a, ((0, 0), (0, width - a.shape[1])))


def _pack_small(D, n_fox, g_cat, l1g, l1b, l2g, l2b, bf, last, tail):
    return jnp.concatenate([g_cat, l1g, l1b, l2g, l2b, _pad_lanes(bf, D), jnp.zeros((1, D), F32), last, tail],
                           axis=0)


def kernel(x, w_in, b_f, g_sb, g_fox, w_out, ln1_g, ln1_b, ln2_g, ln2_b, w_gate_up, w_down, loss_target, m_w_in, m_b_f, m_g_sb, m_g_fox, m_w_out, m_ln1_g, m_ln1_b, m_ln2_g, m_ln2_b, m_w_gate_up, m_w_down, v_w_in, v_b_f, v_g_sb, v_g_fox, v_w_out, v_ln1_g, v_ln1_b, v_ln2_g, v_ln2_b, v_w_gate_up, v_w_down):
    x2, tgt = x[0], loss_target[0]
    S, D = x2.shape
    W = D // 2
    n_pair = W // LANES
    n_fox = W // HEAD_DIM
    F = w_down.shape[1] * N_DEV

    g_in = _all_gather_two_level(w_in[0].astype(BF16), "w_in_all_gather")
    w_in_full = g_in.transpose(1, 0, 2).reshape(D, -1)
    w_qkv = w_in_full[:, :6 * W]
    w_f = _pad_lanes(w_in_full[:, 6 * W:], LANES)
    gmat = _tri(D, lambda r, c: (r // HEAD_DIM) == (c // HEAD_DIM))
    g_cat = jnp.concatenate([g_sb, g_fox], axis=1)

    qkv, u, c, ksq = _proj_fwd(x2, w_qkv, w_f, _pad_lanes(b_f, LANES), n_fox)
    c_t = c[:, :n_fox].T
    c_rows = c_t.reshape(n_fox, S // FOX_BLOCK, FOX_BLOCK)
    kmax = jnp.sqrt(ksq[0:1]) * NORM_SLACK
    o_sb, st_sb, js_sb, g_out, g_gu, g_dn = _sb_fwd(
        qkv, n_pair, [w_out[0].astype(BF16), w_gate_up[0].astype(BF16), w_down[0].astype(BF16)])
    w_o = g_out.reshape(D, D)
    w_gu = g_gu.transpose(1, 0, 2).reshape(D, 2 * F)
    w_dn = g_dn.reshape(F, D)
    o_fx, st_fx, js_fx = _fox_fwd(qkv, c, c_t.reshape(n_fox, S // FOX_BK, FOX_BK), kmax, n_pair)
    js_fx = js_fx // (FOX_BLOCK // FOX_BK)
    h1, xh1, rs1, on_b, rr = _post_attn_fwd(o_sb, o_fx, x2, g_cat, gmat, w_o, ln1_g, ln1_b)
    gu, act_b, dyp, sm2 = _mlp_fwd(h1, tgt, w_gu, w_dn, ln2_g, ln2_b)

    dgu_b, dh1 = _mlp_bwd(gu, dyp, w_gu.T, w_dn.T)
    gw_gu = _matmul_tn(h1, [dgu_b], "grad_w_gate_up", n_split=2)
    gw_dn = _matmul_tn(act_b, [dyp], "grad_w_down")
    dxa, dmix_b, do_sb, do_fx, sm1, l_gu, l_dn = _post_attn_bwd(
        dh1, xh1, rs1, ln1_g, o_sb, o_fx, rr, g_cat, gmat, w_o.T,
        [gw_gu.reshape(D, N_DEV, -1).transpose(1, 0, 2), gw_dn.reshape(N_DEV, F // N_DEV, D)])
    dq_sb, dk_sb, dv_sb = _sb_bwd(qkv, do_sb, st_sb, js_sb, n_pair)
    dq_fx, dk_fx, dv_fx, dcr = _fox_bwd(qkv, do_fx, o_fx, st_fx, c, c_rows, js_fx, n_pair)
    dc = _pad_lanes(dcr.reshape(n_fox, S).T, LANES)
    pieces = [dq_sb, dk_sb, dv_sb, dq_fx, dk_fx, dv_fx]
    gw_qkv = _matmul_tn(x2, pieces, "grad_w_qkv", pad_cols=LANES)[:, :6 * W + n_fox]
    gw_out = _matmul_tn(on_b, [dmix_b], "grad_w_out")
    dx, gw_f, sm0, l_in, l_out = _proj_bwd(
        dxa, x2, pieces, dc, u, w_qkv.T, w_f.T, n_fox,
        [gw_qkv.reshape(D, N_DEV, -1).transpose(1, 0, 2), gw_out.reshape(N_DEV, D // N_DEV, D)])
    gw_f = gw_f[:, :n_fox]

    small = _pack_small(D, n_fox, sm1[2:3], sm1[0:1], sm1[1:2], sm2[0:1], sm2[1:2], sm0[0:1, :n_fox],
                        sm2[2:3] * (1.0 / D), gw_f.T)
    (l_small,) = _exchange_call([jnp.broadcast_to(small[None], (N_DEV,) + small.shape)], True, "small_exchange")

    zero = jnp.zeros((1, D), F32)
    pack = lambda gc, a, b_, c_, d_, bf: _pack_small(D, n_fox, gc, a, b_, c_, d_, bf, zero,
                                                     jnp.zeros((n_fox, D), F32))
    r_small = _reduce_adam_small(
        l_small,
        pack(g_cat, ln1_g, ln1_b, ln2_g, ln2_b, b_f),
        pack(jnp.concatenate([m_g_sb, m_g_fox], axis=1), m_ln1_g, m_ln1_b, m_ln2_g, m_ln2_b, m_b_f),
        pack(jnp.concatenate([v_g_sb, v_g_fox], axis=1), v_ln1_g, v_ln1_b, v_ln2_g, v_ln2_b, v_b_f))
    loss = r_small[4][0, 0]
    cols = w_in.shape[2]
    gf_cols = jnp.pad(r_small[0][8:8 + n_fox].T, ((0, 0), (cols - n_fox, 0)))
    extra = jnp.where(_my_index() == N_DEV - 1, gf_cols, 0.0)
    r_in = _reduce_adam(l_in, w_in[0], m_w_in[0], v_w_in[0], "reduce_adam_w_in", extra)
    r_out = _reduce_adam(l_out, w_out[0], m_w_out[0], v_w_out[0], "reduce_adam_w_out")
    r_gu = _reduce_adam(l_gu, w_gate_up[0], m_w_gate_up[0], v_w_gate_up[0], "reduce_adam_w_gate_up")
    r_dn = _reduce_adam(l_dn, w_down[0], m_w_down[0], v_w_down[0], "reduce_adam_w_down")

    def unpack(kind):
        big = [r_in[kind][None], None, None, None, r_out[kind][None], None, None, None, None,
               r_gu[kind][None], r_dn[kind][None]]
        s = r_small[kind]
        big[1] = s[5:6, :n_fox]
        big[2] = s[0:1, :W]
        big[3] = s[0:1, W:]
        big[5], big[6], big[7], big[8] = s[1:2], s[2:3], s[3:4], s[4:5]
        return big

    return (loss, dx[None], *unpack(0), *unpack(1), *unpack(2), *unpack(3))
```

```python
import functools

import jax
import jax.numpy as jnp
from jax import lax
from jax.experimental import pallas as pl
from jax.experimental.pallas import tpu as pltpu

F32 = jnp.float32
BF16 = jnp.bfloat16
I32 = jnp.int32

N_DEV = 8
HEAD_DIM = 64
LANES = 128
SCALE = HEAD_DIM ** -0.5
ALPHA = 2.0 ** 0.25
LN_EPS = 1e-5
RMS_EPS = 1e-6
ADAM_LR, ADAM_B1, ADAM_B2, ADAM_EPS, ADAM_WD, ADAM_STEP = 0.001, 0.9, 0.999, 1e-08, 0.01, 10
NEG_BIG = -1e30
NORM_SLACK = 1.01
EXP_ZERO = 88.5
VMEM_LIMIT = 60 * 1024 * 1024
ROW_TILE = 256
SB_BLOCK = 256
FOX_BLOCK = 512
FOX_BQ, FOX_BK = 512, 256
assert FOX_BQ == FOX_BLOCK and FOX_BLOCK % FOX_BK == 0
MESH = pl.DeviceIdType.MESH


def _cparams(sem):
    return pltpu.CompilerParams(dimension_semantics=sem, vmem_limit_bytes=VMEM_LIMIT)


def _dot(a, b):
    return jnp.dot(a, b, preferred_element_type=F32)


def _dot_nt(a, b):
    return lax.dot_general(a, b, (((1,), (1,)), ((), ())), preferred_element_type=F32)


def _dot_tn(a, b):
    return lax.dot_general(a, b, (((0,), (0,)), ((), ())), preferred_element_type=F32)


def _split2(a):
    hi = a.astype(BF16)
    lo = (a - hi.astype(F32)).astype(BF16)
    return hi, lo


def _split3(a):
    hi = a.astype(BF16)
    r1 = a - hi.astype(F32)
    mid = r1.astype(BF16)
    lo = (r1 - mid.astype(F32)).astype(BF16)
    return hi, mid, lo


def _dot_acc(a, m):
    hi, lo = _split2(a)
    return _dot(jnp.concatenate([hi, lo], axis=1), jnp.concatenate([m, m], axis=0))


def _tri(n, fn):
    r = lax.broadcasted_iota(I32, (n, n), 0)
    c = lax.broadcasted_iota(I32, (n, n), 1)
    return fn(r, c).astype(BF16)


def _full(shape):
    nd = len(shape)
    return pl.BlockSpec(shape, lambda *_: (0,) * nd)


def _peer(k):
    x, y, c = lax.axis_index("x"), lax.axis_index("y"), lax.axis_index("c")
    return (1 - x if k & 4 else x, 1 - y if k & 2 else y, 1 - c if k & 1 else c)


def _my_index():
    return 4 * lax.axis_index("x") + 2 * lax.axis_index("y") + lax.axis_index("c")


class _Exchange:
    def __init__(self, n, scatter):
        self.n, self.scatter = n, scatter

    def sem_shapes(self):
        return [pltpu.SemaphoreType.DMA(((N_DEV - 1) * self.n,)), pltpu.SemaphoreType.DMA(((N_DEV - 1) * self.n,)),
                pltpu.SemaphoreType.DMA((self.n,))]

    def out_shapes(self, arrays):
        if self.scatter:
            return [jax.ShapeDtypeStruct(s.shape, s.dtype) for s in arrays]
        return [jax.ShapeDtypeStruct((N_DEV,) + s.shape, s.dtype) for s in arrays]

    def _copies(self, ins, outs, sems, landing):
        send_sems, recv_sems, local_sems = sems
        me = _my_index()
        src = lambda a, d: ins[a].at[d] if self.scatter else ins[a]
        local = [pltpu.make_async_copy(src(a, me), outs[a].at[me], local_sems.at[a]) for a in range(self.n)]
        remote = [pltpu.make_async_remote_copy(
            src_ref=src(a, me ^ k), dst_ref=outs[a].at[me ^ k if landing else me],
            send_sem=send_sems.at[(k - 1) * self.n + a], recv_sem=recv_sems.at[(k - 1) * self.n + a],
            device_id=_peer(k), device_id_type=MESH) for k in range(1, N_DEV) for a in range(self.n)]
        return local, remote

    def start(self, ins, outs, sems):
        local, sent = self._copies(ins, outs, sems, landing=False)
        for cp in local + sent:
            cp.start()

    def finish(self, ins, outs, sems):
        local, landed = self._copies(ins, outs, sems, landing=True)
        for cp in landed:
            cp.wait_recv()
        for cp in landed:
            cp.wait_send()
        for cp in local:
            cp.wait()


def _all_gather_two_level(shard, name):
    def body(x_ref, out_ref, send_sems, recv_sems, local_sem):
        x, y, c = lax.axis_index("x"), lax.axis_index("y"), lax.axis_index("c")
        me, sibling = (x, y, c), (x, y, 1 - c)
        chips = [(1 - x, y), (x, 1 - y), (1 - x, 1 - y)]

        def slot(px, py, pc):
            return out_ref.at[4 * px + 2 * py + pc]

        def copy(k, block, to, src=None):
            return pltpu.make_async_remote_copy(
                src_ref=slot(*block) if src is None else src, dst_ref=slot(*block),
                send_sem=send_sems.at[k], recv_sem=recv_sems.at[k], device_id=to, device_id_type=MESH)

        mine = pltpu.make_async_copy(x_ref, slot(*me), local_sem)
        mine.start()
        first = [copy(0, me, sibling, src=x_ref)]
        first += [copy(1 + j, me, (*chip, c), src=x_ref) for j, chip in enumerate(chips)]
        for cp in first:
            cp.start()
        passed = [copy(4 + j, (*chip, c), sibling) for j, chip in enumerate(chips)]
        for j, chip in enumerate(chips):
            copy(1 + j, (*chip, c), me).wait_recv()
            passed[j].start()
        copy(0, sibling, me).wait_recv()
        for j, chip in enumerate(chips):
            copy(4 + j, (*chip, 1 - c), me).wait_recv()
        for cp in first + passed:
            cp.wait_send()
        mine.wait()

    any_spec = pl.BlockSpec(memory_space=pl.ANY)
    return pl.pallas_call(
        body, name=name, out_shape=jax.ShapeDtypeStruct((N_DEV,) + shard.shape, shard.dtype),
        in_specs=[any_spec], out_specs=any_spec,
        scratch_shapes=[pltpu.SemaphoreType.DMA((N_DEV - 1,)), pltpu.SemaphoreType.DMA((N_DEV - 1,)),
                        pltpu.SemaphoreType.DMA],
    )(shard)


def _exchange_call(arrays, scatter, name):
    n = len(arrays)
    ex = _Exchange(n, scatter)

    def body(*refs):
        ins, outs, sems = refs[:n], refs[n:2 * n], refs[2 * n:]
        ex.start(ins, outs, sems)
        ex.finish(ins, outs, sems)

    any_spec = pl.BlockSpec(memory_space=pl.ANY)
    return pl.pallas_call(
        body, name=name, out_shape=ex.out_shapes(arrays),
        in_specs=[any_spec] * n, out_specs=[any_spec] * n, scratch_shapes=ex.sem_shapes(),
    )(*arrays)


def _log_sigmoid(u):
    return jnp.minimum(u, 0.0) - jnp.log1p(jnp.exp(-jnp.abs(u)))


def _proj_fwd(x, w_qkv, w_f, bf_pad, n_fox):
    S, D = x.shape
    N = w_qkv.shape[1]
    W = D // 2
    TM = ROW_TILE
    tri = _tri(TM, lambda r, c: c <= r)
    r_ = lax.broadcasted_iota(I32, (W, LANES), 0)
    c_ = lax.broadcasted_iota(I32, (W, LANES), 1)
    head_of = (r_ // HEAD_DIM == c_).astype(BF16)

    def body(x_ref, wq_ref, wf_ref, bf_ref, tri_ref, ho_ref, qkv_ref, u_ref, c_ref, ksq_ref, run_ref):
        @pl.when(pl.program_id(0) == 0)
        def _():
            run_ref[...] = jnp.zeros_like(run_ref)
            ksq_ref[...] = jnp.zeros_like(ksq_ref)

        xb = x_ref[...].astype(BF16)
        for n0 in range(0, N, D):
            chunk = _dot(xb, wq_ref[:, n0:n0 + D]).astype(BF16)
            qkv_ref[:, n0:n0 + D] = chunk
            if n0 == 4 * W:
                kf = chunk[:, :W].astype(F32)
                ksq = jnp.max(_dot_acc(kf * kf, ho_ref[...]), axis=0, keepdims=True)
                ksq_ref[...] = jnp.maximum(ksq_ref[...], ksq)
        u = _dot(xb, wf_ref[...]) + bf_ref[...]
        lane = lax.broadcasted_iota(I32, u.shape, 1)
        logf = jnp.where(lane < n_fox, _log_sigmoid(u), 0.0)
        u_ref[...] = u
        hi, mid, lo = _split3(logf)
        t = tri_ref[...]
        cs = _dot(t, hi) + _dot(t, mid) + _dot(t, lo) + run_ref[...]
        c_ref[...] = cs
        run_ref[...] = cs[TM - 1:TM, :]

    return pl.pallas_call(
        body, name="proj_fwd", grid=(S // TM,),
        in_specs=[pl.BlockSpec((TM, D), lambda i: (i, 0)), _full(w_qkv.shape), _full(w_f.shape),
                  _full(bf_pad.shape), _full(tri.shape), _full(head_of.shape)],
        out_specs=[pl.BlockSpec((TM, N), lambda i: (i, 0)), pl.BlockSpec((TM, LANES), lambda i: (i, 0)),
                   pl.BlockSpec((TM, LANES), lambda i: (i, 0)), _full((8, LANES))],
        out_shape=[jax.ShapeDtypeStruct((S, N), BF16), jax.ShapeDtypeStruct((S, LANES), F32),
                   jax.ShapeDtypeStruct((S, LANES), F32), jax.ShapeDtypeStruct((8, LANES), F32)],
        scratch_shapes=[pltpu.VMEM((1, LANES), F32)],
        compiler_params=_cparams(("arbitrary",)),
    )(x, w_qkv, w_f, bf_pad, tri, head_of)


def _post_attn_fwd(o_sb, o_fx, x, g_cat, gmat, w_out, ln_g, ln_b):
    S, D = x.shape
    H = D // 2
    TM = ROW_TILE

    def body(osb_ref, ofx_ref, x_ref, g_ref, gm_ref, wo_ref, lg_ref, lb_ref,
             h1_ref, xh_ref, rs_ref, on_ref, rr_ref):
        o = jnp.concatenate([osb_ref[...], ofx_ref[...]], axis=1)
        ms = _dot_acc(o * o, gm_ref[...]) * (1.0 / HEAD_DIM)
        r = lax.rsqrt(ms + RMS_EPS)
        onb = (o * r * g_ref[...]).astype(BF16)
        hp = ALPHA * x_ref[...] + _dot(onb, wo_ref[...])
        mu = jnp.mean(hp, axis=-1, keepdims=True)
        d = hp - mu
        rstd = lax.rsqrt(jnp.mean(d * d, axis=-1, keepdims=True) + LN_EPS)
        xh = d * rstd
        h1_ref[...] = xh * lg_ref[...] + lb_ref[...]
        xh_ref[...] = xh
        rs_ref[...] = jnp.broadcast_to(rstd, (TM, LANES))
        on_ref[...] = onb
        rr_ref[...] = r

    row = lambda w: pl.BlockSpec((TM, w), lambda i: (i, 0))
    return pl.pallas_call(
        body, name="post_attn_fwd", grid=(S // TM,),
        in_specs=[row(H), row(H), row(D), _full((1, D)), _full((D, D)), _full((D, D)), _full((1, D)), _full((1, D))],
        out_specs=[row(D), row(D), row(LANES), row(D), row(D)],
        out_shape=[jax.ShapeDtypeStruct((S, D), F32), jax.ShapeDtypeStruct((S, D), F32),
                   jax.ShapeDtypeStruct((S, LANES), F32), jax.ShapeDtypeStruct((S, D), BF16),
                   jax.ShapeDtypeStruct((S, D), F32)],
        compiler_params=_cparams(("arbitrary",)),
    )(o_sb, o_fx, x, g_cat, gmat, w_out, ln_g, ln_b)


def _ln_bwd(dxh, xh, rstd):
    m1 = jnp.mean(dxh, axis=-1, keepdims=True)
    m2 = jnp.mean(dxh * xh, axis=-1, keepdims=True)
    return rstd * (dxh - m1 - xh * m2)


def _mlp_fwd(h1, target, w_gu, w_dn, ln_g, ln_b):
    S, D = h1.shape
    F = w_dn.shape[0]
    TM = ROW_TILE
    FC = F // 2

    def body(h1_ref, tg_ref, wgu_hbm, wdn_hbm, lg_ref, lb_ref, gu_ref, act_ref, dyp_ref, sm_ref, wgu, wdn):
        @pl.when(pl.program_id(0) == 0)
        def _():
            pltpu.sync_copy(wgu_hbm, wgu)
            pltpu.sync_copy(wdn_hbm, wdn)
            sm_ref[...] = jnp.zeros_like(sm_ref)

        h1v = h1_ref[...]
        hb = h1v.astype(BF16)
        ff = jnp.zeros((TM, D), F32)
        for c0 in range(0, F, FC):
            g = _dot(hb, wgu[:, c0:c0 + FC])
            u = _dot(hb, wgu[:, F + c0:F + c0 + FC])
            gu_ref[:, c0:c0 + FC] = g
            gu_ref[:, F + c0:F + c0 + FC] = u
            ab = ((g * jax.nn.sigmoid(g)) * u).astype(BF16)
            act_ref[:, c0:c0 + FC] = ab
            ff = ff + _dot(ab, wdn[c0:c0 + FC, :])
        yp = ALPHA * h1v + ff
        mu = jnp.mean(yp, axis=-1, keepdims=True)
        d = yp - mu
        rstd = lax.rsqrt(jnp.mean(d * d, axis=-1, keepdims=True) + LN_EPS)
        xh = d * rstd
        err = (xh * lg_ref[...] + lb_ref[...]) - tg_ref[...]
        dy = err * (1.0 / D)
        sm_ref[0:1, :] += jnp.sum(dy * xh, axis=0, keepdims=True)
        sm_ref[1:2, :] += jnp.sum(dy, axis=0, keepdims=True)
        sm_ref[2:3, :] += jnp.sum(err * err, axis=0, keepdims=True)
        dyp_ref[...] = _ln_bwd(dy * lg_ref[...], xh, rstd)

    row = lambda w: pl.BlockSpec((TM, w), lambda i: (i, 0))
    hbm = pl.BlockSpec(memory_space=pl.ANY)
    return pl.pallas_call(
        body, name="mlp_fwd", grid=(S // TM,),
        in_specs=[row(D), row(D), hbm, hbm, _full((1, D)), _full((1, D))],
        out_specs=[row(2 * F), row(F), row(D), _full((8, D))],
        out_shape=[jax.ShapeDtypeStruct((S, 2 * F), F32), jax.ShapeDtypeStruct((S, F), BF16),
                   jax.ShapeDtypeStruct((S, D), F32), jax.ShapeDtypeStruct((8, D), F32)],
        scratch_shapes=[pltpu.VMEM(w_gu.shape, BF16), pltpu.VMEM(w_dn.shape, BF16)],
        compiler_params=_cparams(("arbitrary",)),
    )(h1, target, w_gu, w_dn, ln_g, ln_b)


def _mlp_bwd(gu, dyp, w_guT, w_dnT):
    S, D = dyp.shape
    F = w_dnT.shape[1]
    TM = ROW_TILE
    FC = F // 2

    def body(gu_ref, dyp_ref, wguT_hbm, wdnT_hbm, dgu_ref, dh1_ref, wguT, wdnT):
        @pl.when(pl.program_id(0) == 0)
        def _():
            pltpu.sync_copy(wguT_hbm, wguT)
            pltpu.sync_copy(wdnT_hbm, wdnT)

        dypv = dyp_ref[...]
        db = dypv.astype(BF16)
        dh1 = ALPHA * dypv
        for c0 in range(0, F, FC):
            dact = _dot(db, wdnT[:, c0:c0 + FC])
            g = gu_ref[:, c0:c0 + FC]
            u = gu_ref[:, F + c0:F + c0 + FC]
            sg = jax.nn.sigmoid(g)
            dgb = (dact * u * (sg * (1.0 + g * (1.0 - sg)))).astype(BF16)
            dub = (dact * (g * sg)).astype(BF16)
            dgu_ref[:, c0:c0 + FC] = dgb
            dgu_ref[:, F + c0:F + c0 + FC] = dub
            dh1 = dh1 + _dot(dgb, wguT[c0:c0 + FC, :]) + _dot(dub, wguT[F + c0:F + c0 + FC, :])
        dh1_ref[...] = dh1

    row = lambda w: pl.BlockSpec((TM, w), lambda i: (i, 0))
    hbm = pl.BlockSpec(memory_space=pl.ANY)
    return pl.pallas_call(
        body, name="mlp_bwd", grid=(S // TM,),
        in_specs=[row(2 * F), row(D), hbm, hbm],
        out_specs=[row(2 * F), row(D)],
        out_shape=[jax.ShapeDtypeStruct((S, 2 * F), BF16), jax.ShapeDtypeStruct((S, D), F32)],
        scratch_shapes=[pltpu.VMEM(w_guT.shape, BF16), pltpu.VMEM(w_dnT.shape, BF16)],
        compiler_params=_cparams(("arbitrary",)),
    )(gu, dyp, w_guT, w_dnT)


def _post_attn_bwd(dh1, xh, rs, ln_g, o_sb, o_fx, rr, g_cat, gmat, w_outT, chunked):
    S, D = dh1.shape
    H = D // 2
    TM = ROW_TILE
    nT = S // TM
    n_ch = len(chunked)
    ex = _Exchange(n_ch, scatter=True)

    def body(*refs):
        dh1_ref, xh_ref, rs_ref, lg_ref, osb_ref, ofx_ref, rr_ref, g_ref, gm_ref, woT_ref = refs[:10]
        ch_in = refs[10:10 + n_ch]
        dxa_ref, dmix_ref, dosb_ref, dofx_ref, sm_ref = refs[10 + n_ch:15 + n_ch]
        ch_out = refs[15 + n_ch:15 + 2 * n_ch]
        sems = refs[15 + 2 * n_ch:]

        @pl.when(pl.program_id(0) == 0)
        def _():
            ex.start(ch_in, ch_out, sems)
            sm_ref[...] = jnp.zeros_like(sm_ref)

        dh = dh1_ref[...]
        xhv = xh_ref[...]
        dhp = _ln_bwd(dh * lg_ref[...], xhv, rs_ref[:, 0:1])
        dxa_ref[...] = ALPHA * dhp
        dmb = dhp.astype(BF16)
        dmix_ref[...] = dmb
        don = _dot(dmb, woT_ref[...])
        o = jnp.concatenate([osb_ref[...], ofx_ref[...]], axis=1)
        r = rr_ref[...]
        u = don * g_ref[...]
        t = _dot_acc(u * o, gm_ref[...]) * (1.0 / HEAD_DIM)
        do = r * u - o * (r * r * r) * t
        dosb_ref[...] = do[:, :H]
        dofx_ref[...] = do[:, H:]
        sm_ref[0:1, :] += jnp.sum(dh * xhv, axis=0, keepdims=True)
        sm_ref[1:2, :] += jnp.sum(dh, axis=0, keepdims=True)
        sm_ref[2:3, :] += jnp.sum(don * o * r, axis=0, keepdims=True)

        @pl.when(pl.program_id(0) == nT - 1)
        def _():
            ex.finish(ch_in, ch_out, sems)

    row = lambda w: pl.BlockSpec((TM, w), lambda i: (i, 0))
    any_spec = pl.BlockSpec(memory_space=pl.ANY)
    return pl.pallas_call(
        body, name="post_attn_bwd", grid=(nT,),
        in_specs=[row(D), row(D), row(LANES), _full((1, D)), row(H), row(H), row(D), _full((1, D)),
                  _full((D, D)), _full((D, D))] + [any_spec] * n_ch,
        out_specs=[row(D), row(D), row(H), row(H), _full((8, D))] + [any_spec] * n_ch,
        out_shape=[jax.ShapeDtypeStruct((S, D), F32), jax.ShapeDtypeStruct((S, D), BF16),
                   jax.ShapeDtypeStruct((S, H), F32), jax.ShapeDtypeStruct((S, H), F32),
                   jax.ShapeDtypeStruct((8, D), F32)] + ex.out_shapes(chunked),
        scratch_shapes=ex.sem_shapes(),
        compiler_params=_cparams(("arbitrary",)),
    )(dh1, xh, rs, ln_g, o_sb, o_fx, rr, g_cat, gmat, w_outT, *chunked)


def _proj_bwd(dxa, x, pieces, dc, u, w_qkvT, w_fT, n_fox, chunked):
    S, D = dxa.shape
    H = D // 2
    TM = ROW_TILE
    nT = S // TM
    tri = _tri(TM, lambda r, c: c >= r)
    n_p = len(pieces)
    n_ch = len(chunked)
    ex = _Exchange(n_ch, scatter=True)

    def body(*refs):
        dxa_ref, x_ref = refs[:2]
        p_refs = refs[2:2 + n_p]
        dc_ref, u_ref, wq_ref, wf_ref, tri_ref = refs[2 + n_p:7 + n_p]
        ch_in = refs[7 + n_p:7 + n_p + n_ch]
        dx_ref, gwf_ref, sm_ref = refs[7 + n_p + n_ch:10 + n_p + n_ch]
        ch_out = refs[10 + n_p + n_ch:10 + n_p + 2 * n_ch]
        run_ref = refs[10 + n_p + 2 * n_ch]
        sems = refs[11 + n_p + 2 * n_ch:]

        @pl.when(pl.program_id(0) == 0)
        def _():
            ex.start(ch_in, ch_out, sems)
            run_ref[...] = jnp.zeros_like(run_ref)
            sm_ref[...] = jnp.zeros_like(sm_ref)
            gwf_ref[...] = jnp.zeros_like(gwf_ref)

        hi, mid, lo = _split3(dc_ref[...])
        t = tri_ref[...]
        dlogf = _dot(t, hi) + _dot(t, mid) + _dot(t, lo) + run_ref[...]
        run_ref[...] = dlogf[0:1, :]
        uv = u_ref[...]
        lane = lax.broadcasted_iota(I32, uv.shape, 1)
        df = jnp.where(lane < n_fox, dlogf * jax.nn.sigmoid(-uv), 0.0)
        sm_ref[0:1, :] += jnp.sum(df, axis=0, keepdims=True)
        dfb = df.astype(BF16)
        gwf_ref[...] += _dot_tn(x_ref[...].astype(BF16), dfb)
        acc = dxa_ref[...] + _dot(dfb, wf_ref[...])
        for a in range(n_p):
            acc = acc + _dot(p_refs[a][...].astype(BF16), wq_ref[a * H:(a + 1) * H, :])
        dx_ref[...] = acc

        @pl.when(pl.program_id(0) == nT - 1)
        def _():
            ex.finish(ch_in, ch_out, sems)

    rev = lambda w: pl.BlockSpec((TM, w), lambda i: (nT - 1 - i, 0))
    any_spec = pl.BlockSpec(memory_space=pl.ANY)
    return pl.pallas_call(
        body, name="proj_bwd", grid=(nT,),
        in_specs=[rev(D), rev(D)] + [rev(H)] * n_p + [rev(LANES), rev(LANES), _full(w_qkvT.shape),
                                                       _full(w_fT.shape), _full(tri.shape)] + [any_spec] * n_ch,
        out_specs=[rev(D), _full((D, LANES)), _full((8, LANES))] + [any_spec] * n_ch,
        out_shape=[jax.ShapeDtypeStruct((S, D), F32), jax.ShapeDtypeStruct((D, LANES), F32),
                   jax.ShapeDtypeStruct((8, LANES), F32)] + ex.out_shapes(chunked),
        scratch_shapes=[pltpu.VMEM((1, LANES), F32)] + ex.sem_shapes(),
        compiler_params=_cparams(("arbitrary",)),
    )(dxa, x, *pieces, dc, u, w_qkvT, w_fT, tri, *chunked)


def _matmul_tn(a, bs, name, n_split=1, pad_cols=0):
    S, M = a.shape
    widths = [b.shape[1] for b in bs]
    N = sum(widths) + pad_cols
    assert n_split == 1 or (len(bs) == 1 and pad_cols == 0)
    TK = 512 if S % 512 == 0 else ROW_TILE
    MC = 512 if M % 512 == 0 else 256
    nb = len(bs)

    def body(*refs):
        a_ref, b_refs, o_ref, acc_ref = refs[0], refs[1:1 + nb], refs[1 + nb], refs[2 + nb]

        @pl.when(pl.program_id(1) == 0)
        def _():
            acc_ref[...] = jnp.zeros_like(acc_ref)

        n0 = 0
        for b_ref in b_refs:
            bv = b_ref[...].astype(BF16)
            w = bv.shape[1]
            for m0 in range(0, M, MC):
                acc_ref[m0:m0 + MC, n0:n0 + w] += _dot_tn(a_ref[:, m0:m0 + MC].astype(BF16), bv)
            n0 += w

        @pl.when(pl.program_id(1) == S // TK - 1)
        def _():
            o_ref[...] = acc_ref[...].astype(BF16)

    return pl.pallas_call(
        body, name=name, grid=(n_split, S // TK),
        in_specs=[pl.BlockSpec((TK, M), lambda n, k: (k, 0))]
        + [pl.BlockSpec((TK, w // n_split), lambda n, k: (k, n)) for w in widths],
        out_specs=pl.BlockSpec((M, N // n_split), lambda n, k: (0, n)),
        out_shape=jax.ShapeDtypeStruct((M, N), BF16),
        scratch_shapes=[pltpu.VMEM((M, N // n_split), F32)],
        compiler_params=_cparams(("arbitrary", "arbitrary")),
    )(a, *bs)


def _half_mask(half):
    lane = lax.broadcasted_iota(I32, (1, LANES), 1)
    return (lane >= half * HEAD_DIM) & (lane < half * HEAD_DIM + HEAD_DIM)


def _lane_tile(a, width):
    return jnp.concatenate([a] * (width // LANES), axis=1)


def _softplus_neg_abs(z):
    return jnp.log(1.0 + jnp.exp(-jnp.abs(z)))


def _sb_fwd(qkv, n_pair, shards):
    S = qkv.shape[0]
    B = SB_BLOCK
    nq = S // B
    us = _tri(B, lambda r, c: r > c)
    n_sh = len(shards)
    ex = _Exchange(n_sh, scatter=False)

    def body(*refs):
        q_ref, k_ref, v_ref, us_ref = refs[:4]
        sh_in = refs[4:4 + n_sh]
        o_ref, st_ref, js_ref = refs[4 + n_sh:7 + n_sh]
        sh_out = refs[7 + n_sh:7 + 2 * n_sh]
        acc_ref, r_ref = refs[7 + 2 * n_sh:9 + 2 * n_sh]
        sems = refs[9 + 2 * n_sh:]
        p, i = pl.program_id(0), pl.program_id(1)

        @pl.when((p == 0) & (i == 0))
        def _():
            ex.start(sh_in, sh_out, sems)

        hms = [_half_mask(h) for h in range(2)]
        qv = q_ref[...]
        qss = [jnp.where(hm, qv, jnp.zeros_like(qv)) * SCALE for hm in hms]
        row = lax.broadcasted_iota(I32, (B, B), 0)
        col = lax.broadcasted_iota(I32, (B, B), 1)
        tri = col < row
        acc_ref[...] = jnp.zeros_like(acc_ref)
        r_ref[...] = jnp.zeros_like(r_ref)

        def block(j, diag):
            off = pl.multiple_of(j * B, B)
            kj = k_ref[pl.ds(off, B), :]
            vj = v_ref[pl.ds(off, B), :]
            zs = [_dot_nt(qss[h], kj) for h in range(2)]
            sps = [_softplus_neg_abs(z) for z in zs]
            bs = [jnp.minimum(-z, 0.0) - sp for z, sp in zip(zs, sps)]
            if diag:
                bs = [jnp.where(tri, b, 0.0) for b in bs]
            lexcs = [_dot_acc(b, us_ref[...]) for b in bs]
            ws = [jnp.exp(jnp.minimum(z, 0.0) - sp + (_lane_tile(r_ref[h], B) + lexc))
                  for h, (z, sp, lexc) in enumerate(zip(zs, sps, lexcs))]
            if diag:
                ws = [jnp.where(tri, w, 0.0) for w in ws]
            for h in range(2):
                acc_ref[h] += _dot(ws[h].astype(BF16), vj)
                r_ref[h] += jnp.broadcast_to(lexcs[h][:, 0:1] + bs[h][:, 0:1], (B, LANES))

        def live():
            return (jnp.max(r_ref[...]) > -EXP_ZERO).astype(I32)

        def first_two():
            has_prev = (i > 0).astype(F32)
            off_d = pl.multiple_of(i * B, B)
            off_o = pl.multiple_of(jnp.maximum(i - 1, 0) * B, B)
            k_d, v_d = k_ref[pl.ds(off_d, B), :], v_ref[pl.ds(off_d, B), :]
            k_o, v_o = k_ref[pl.ds(off_o, B), :], v_ref[pl.ds(off_o, B), :]
            hs = range(2)
            z_d = [_dot_nt(qss[h], k_d) for h in hs]
            z_o = [_dot_nt(qss[h], k_o) for h in hs]
            sp_d = [_softplus_neg_abs(z) for z in z_d]
            sp_o = [_softplus_neg_abs(z) for z in z_o]
            b_d = [jnp.where(tri, jnp.minimum(-z, 0.0) - sp, 0.0) for z, sp in zip(z_d, sp_d)]
            b_o = [(jnp.minimum(-z, 0.0) - sp) * has_prev for z, sp in zip(z_o, sp_o)]
            l_d = [_dot_acc(b, us_ref[...]) for b in b_d]
            l_o = [_dot_acc(b, us_ref[...]) for b in b_o]
            r_d = [jnp.broadcast_to(l[:, 0:1] + b[:, 0:1], (B, LANES)) for l, b in zip(l_d, b_d)]
            w_d = [jnp.where(tri, jnp.exp(jnp.minimum(z, 0.0) - sp + l), 0.0) for z, sp, l in zip(z_d, sp_d, l_d)]
            w_o = [jnp.exp(jnp.minimum(z, 0.0) - sp + (_lane_tile(r, B) + l)) * has_prev
                   for z, sp, l, r in zip(z_o, sp_o, l_o, r_d)]
            for h in hs:
                acc_ref[h] = _dot(w_d[h].astype(BF16), v_d) + _dot(w_o[h].astype(BF16), v_o)
                r_ref[h] = r_d[h] + jnp.broadcast_to(l_o[h][:, 0:1] + b_o[h][:, 0:1], (B, LANES))

        first_two()

        def step(carry):
            j, _ = carry
            block(j, False)
            return j - 1, live()

        j_end, _ = lax.while_loop(lambda c: (c[0] >= 0) & (c[1] > 0), step, (i - 2, live()))
        js = jnp.maximum(j_end + 1, 0)
        js_ref[2 * p, i] = js
        js_ref[2 * p + 1, i] = js
        o_ref[...] = jnp.where(hms[0], acc_ref[0], acc_ref[1])
        st_ref[...] = r_ref[...]

        @pl.when((p == n_pair - 1) & (i == nq - 1))
        def _():
            ex.finish(sh_in, sh_out, sems)

    any_spec = pl.BlockSpec(memory_space=pl.ANY)
    return pl.pallas_call(
        body, name="sb_attn_fwd", grid=(n_pair, nq),
        in_specs=[pl.BlockSpec((B, LANES), lambda p, i: (i, p)),
                  pl.BlockSpec((S, LANES), lambda p, i: (0, n_pair + p)),
                  pl.BlockSpec((S, LANES), lambda p, i: (0, 2 * n_pair + p)),
                  _full((B, B))] + [any_spec] * n_sh,
        out_specs=[pl.BlockSpec((B, LANES), lambda p, i: (i, p)),
                   pl.BlockSpec((2, B, LANES), lambda p, i: (p, i, 0)),
                   pl.BlockSpec(memory_space=pltpu.SMEM)] + [any_spec] * n_sh,
        out_shape=[jax.ShapeDtypeStruct((S, n_pair * LANES), F32),
                   jax.ShapeDtypeStruct((2 * n_pair, S, LANES), F32),
                   jax.ShapeDtypeStruct((2 * n_pair, nq), I32)] + ex.out_shapes(shards),
        scratch_shapes=[pltpu.VMEM((2, B, LANES), F32), pltpu.VMEM((2, B, LANES), F32)] + ex.sem_shapes(),
        compiler_params=_cparams(("arbitrary", "arbitrary")),
    )(qkv, qkv, qkv, us, *shards)


def _sb_bwd(qkv, do, st, js, n_pair):
    S = qkv.shape[0]
    B = SB_BLOCK
    nq = S // B
    us = _tri(B, lambda r, c: r > c)
    ti = _tri(B, lambda r, c: r <= c)

    def body(js_ref, q_ref, k_ref, v_ref, do_ref, st_ref, us_ref, ti_ref, dq_ref, dk_ref, dv_ref,
             dqa_ref, pr_ref, er_ref):
        p, i = pl.program_id(0), pl.program_id(1)

        @pl.when(i == 0)
        def _():
            dk_ref[...] = jnp.zeros_like(dk_ref)
            dv_ref[...] = jnp.zeros_like(dv_ref)

        hms = [_half_mask(h) for h in range(2)]
        qv = q_ref[...]
        dov = do_ref[...]
        qss = [jnp.where(hm, qv, jnp.zeros_like(qv)) * SCALE for hm in hms]
        dobs = [jnp.where(hm, dov, 0.0).astype(BF16) for hm in hms]
        row = lax.broadcasted_iota(I32, (B, B), 0)
        col = lax.broadcasted_iota(I32, (B, B), 1)
        tri = col < row
        dqa_ref[...] = jnp.zeros_like(dqa_ref)
        pr_ref[...] = jnp.zeros_like(pr_ref)
        er_ref[...] = jnp.zeros_like(er_ref)

        def block(j, diag):
            off = pl.multiple_of(j * B, B)
            kj = k_ref[pl.ds(off, B), :]
            vj = v_ref[pl.ds(off, B), :]
            hs = range(2)
            zs = [_dot_nt(qss[h], kj) for h in hs]
            dws = [_dot_nt(dobs[h], vj) for h in hs]
            sps = [_softplus_neg_abs(z) for z in zs]
            bs = [jnp.minimum(-z, 0.0) - sp for z, sp in zip(zs, sps)]
            if diag:
                bs = [jnp.where(tri, b, 0.0) for b in bs]
            lexcs = [_dot_acc(b, us_ref[...]) for b in bs]
            ws = []
            for h in hs:
                pr_new = pr_ref[h] + jnp.broadcast_to(lexcs[h][:, 0:1] + bs[h][:, 0:1], (B, LANES))
                pr_ref[h] = pr_new
                w = jnp.exp(jnp.minimum(zs[h], 0.0) - sps[h] + (_lane_tile(st_ref[h] - pr_new, B) + lexcs[h]))
                ws.append(jnp.where(tri, w, 0.0) if diag else w)
            es = [dw * w for dw, w in zip(dws, ws)]
            eincs = [_dot_acc(e, ti_ref[...]) for e in es]
            dzbs = []
            for h in hs:
                er = er_ref[h]
                big_e = _lane_tile(er, B) + (eincs[h] - es[h])
                er_ref[h] = er + jnp.broadcast_to(eincs[h][:, B - 1:B], (B, LANES))
                eb = jnp.exp(bs[h])
                dzbs.append((es[h] * eb - big_e * (1.0 - eb)).astype(BF16))
            for h in hs:
                dqa_ref[h] += _dot(dzbs[h], kj)
            dk_ref[pl.ds(off, B), :] += _dot_tn(dzbs[0], qss[0]) + _dot_tn(dzbs[1], qss[1])
            dv_ref[pl.ds(off, B), :] += (_dot_tn(ws[0].astype(BF16), dobs[0])
                                         + _dot_tn(ws[1].astype(BF16), dobs[1]))

        def step(j, carry):
            block(j, False)
            return carry

        lax.fori_loop(js_ref[2 * p, i], i, step, 0)
        block(i, True)
        dq_ref[...] = jnp.where(hms[0], dqa_ref[0], dqa_ref[1]) * SCALE

    W = n_pair * LANES
    return pl.pallas_call(
        body, name="sb_attn_bwd",
        grid_spec=pltpu.PrefetchScalarGridSpec(
            num_scalar_prefetch=1, grid=(n_pair, nq),
            in_specs=[pl.BlockSpec((B, LANES), lambda p, i, js: (i, p)),
                      pl.BlockSpec((S, LANES), lambda p, i, js: (0, n_pair + p)),
                      pl.BlockSpec((S, LANES), lambda p, i, js: (0, 2 * n_pair + p)),
                      pl.BlockSpec((B, LANES), lambda p, i, js: (i, p)),
                      pl.BlockSpec((2, B, LANES), lambda p, i, js: (p, i, 0)),
                      pl.BlockSpec((B, B), lambda p, i, js: (0, 0)),
                      pl.BlockSpec((B, B), lambda p, i, js: (0, 0))],
            out_specs=[pl.BlockSpec((B, LANES), lambda p, i, js: (i, p)),
                       pl.BlockSpec((S, LANES), lambda p, i, js: (0, p)),
                       pl.BlockSpec((S, LANES), lambda p, i, js: (0, p))],
            scratch_shapes=[pltpu.VMEM((2, B, LANES), F32), pltpu.VMEM((2, B, LANES), F32),
                            pltpu.VMEM((2, B, LANES), F32)]),
        out_shape=[jax.ShapeDtypeStruct((S, W), F32)] * 3,
        compiler_params=_cparams(("arbitrary", "arbitrary")),
    )(js, qkv, qkv, qkv, do, st, us, ti)


def _head_column(blk, head):
    lane = lax.broadcasted_iota(I32, (1, LANES), 1)
    return jnp.sum(jnp.where(lane == head, blk, 0.0), axis=1, keepdims=True)


def _fox_fwd(qkv, c, c_rows, kmax, n_pair):
    S = qkv.shape[0]
    BQ, BK = FOX_BQ, FOX_BK
    R = BQ // BK
    nq = S // BQ

    def body(q_ref, k_ref, v_ref, c_ref, cr_ref, km_ref, o_ref, st_ref, js_ref, acc_ref, m_ref, cb_ref, qkb_ref):
        p, i, half = pl.program_id(0), pl.program_id(1), pl.program_id(2)
        hm = _half_mask(half)
        qv = q_ref[...]
        qs = jnp.where(hm, qv, jnp.zeros_like(qv)) * SCALE
        ccol = _head_column(c_ref[...], 2 * p + half)
        cb_ref[...] = jnp.broadcast_to(ccol, (BQ, BK))
        qf = qs.astype(F32)
        qkb_ref[...] = jnp.broadcast_to(
            jnp.sqrt(jnp.sum(qf * qf, axis=1, keepdims=True)) * NORM_SLACK
            * _head_column(km_ref[...], 2 * p + half) + ccol, (BQ, LANES))
        row = lax.broadcasted_iota(I32, (BQ, BK), 0)
        col = lax.broadcasted_iota(I32, (BQ, BK), 1)
        acc_ref[...] = jnp.zeros_like(acc_ref)
        m_ref[...] = jnp.full_like(m_ref, NEG_BIG)

        def blocks(j_top, diag):
            ss, v1s, keeps = [], [], []
            for d in range(R):
                j = j_top - d
                off = pl.multiple_of(j * BK, BK)
                vj = v_ref[pl.ds(off, BK), :]
                v1s.append(jnp.where(hm, vj, jnp.ones_like(vj)))
                s = _dot_nt(qs, k_ref[pl.ds(off, BK), :]) + (cb_ref[...] - cr_ref[0, pl.ds(j, 1), :])
                if diag:
                    keeps.append(col + (R - 1 - d) * BK <= row)
                    s = jnp.where(keeps[-1], s, NEG_BIG)
                ss.append(s)
            m_old = m_ref[...]
            s_max = jnp.max(functools.reduce(jnp.maximum, ss), axis=1, keepdims=True)
            m_new = jnp.maximum(m_old, jnp.broadcast_to(s_max, (BQ, LANES)))
            m_wide = _lane_tile(m_new, BK)
            pvs = [jnp.exp(s - m_wide) for s in ss]
            if diag:
                pvs = [jnp.where(keep, pv, 0.0) for keep, pv in zip(keeps, pvs)]
            new = _dot(pvs[0].astype(BF16), v1s[0])
            for pv, v1 in zip(pvs[1:], v1s[1:]):
                new = new + _dot(pv.astype(BF16), v1)
            acc_ref[...] = jnp.exp(m_old - m_new) * acc_ref[...] + new
            m_ref[...] = m_new

        def live(j):
            c_end = cr_ref[0, pl.ds(jnp.maximum(j, 0), 1), :][:, BK - 1:BK]
            return (jnp.max(qkb_ref[...] - c_end - m_ref[...]) > -EXP_ZERO).astype(I32)

        blocks(R * i + R - 1, True)

        def step(carry):
            j, _ = carry
            go_on = live(j - R)
            blocks(j, False)
            return j - R, go_on

        j_end, _ = lax.while_loop(lambda cr: (cr[0] >= 0) & (cr[1] > 0), step, (R * i - 1, live(R * i - 1)))
        js_ref[2 * p + half, i] = j_end + 1
        acc = acc_ref[...]
        denom = jnp.where(hm, pltpu.roll(acc, HEAD_DIM, 1), acc)
        res = jnp.where(hm, acc / denom, 0.0)

        @pl.when(half == 0)
        def _():
            o_ref[...] = res

        @pl.when(half == 1)
        def _():
            o_ref[...] += res

        st_ref[0] = m_ref[...] + jnp.log(denom)

    return pl.pallas_call(
        body, name="fox_attn_fwd", grid=(n_pair, nq, 2),
        in_specs=[pl.BlockSpec((BQ, LANES), lambda p, i, h: (i, 3 * n_pair + p)),
                  pl.BlockSpec((S, LANES), lambda p, i, h: (0, 4 * n_pair + p)),
                  pl.BlockSpec((S, LANES), lambda p, i, h: (0, 5 * n_pair + p)),
                  pl.BlockSpec((BQ, LANES), lambda p, i, h: (i, 0)),
                  pl.BlockSpec((1, S // BK, BK), lambda p, i, h: (2 * p + h, 0, 0)),
                  _full((1, LANES))],
        out_specs=[pl.BlockSpec((BQ, LANES), lambda p, i, h: (i, p)),
                   pl.BlockSpec((1, BQ, LANES), lambda p, i, h: (2 * p + h, i, 0)),
                   pl.BlockSpec(memory_space=pltpu.SMEM)],
        out_shape=[jax.ShapeDtypeStruct((S, n_pair * LANES), F32),
                   jax.ShapeDtypeStruct((2 * n_pair, S, LANES), F32),
                   jax.ShapeDtypeStruct((2 * n_pair, nq), I32)],
        scratch_shapes=[pltpu.VMEM((BQ, LANES), F32), pltpu.VMEM((BQ, LANES), F32), pltpu.VMEM((BQ, BK), F32),
                        pltpu.VMEM((BQ, LANES), F32)],
        compiler_params=_cparams(("arbitrary", "arbitrary", "arbitrary")),
    )(qkv, qkv, qkv, c, c_rows, kmax)


def _fox_bwd(qkv, do, o, st, c, c_rows, js, n_pair):
    S = qkv.shape[0]
    B = FOX_BLOCK
    nq = S // B

    def body(js_ref, q_ref, k_ref, v_ref, do_ref, o_ref, st_ref, c_ref, cr_ref, dq_ref, dk_ref, dv_ref,
             dc_ref, dqa_ref, rs_ref, cb_ref, db_ref):
        p, i, half = pl.program_id(0), pl.program_id(1), pl.program_id(2)

        @pl.when((i == 0) & (half == 0))
        def _():
            dk_ref[...] = jnp.zeros_like(dk_ref)
            dv_ref[...] = jnp.zeros_like(dv_ref)
            dc_ref[...] = jnp.zeros_like(dc_ref)

        hm = _half_mask(half)
        qv = q_ref[...]
        qs = jnp.where(hm, qv, jnp.zeros_like(qv)) * SCALE
        dov = jnp.where(hm, do_ref[...], 0.0)
        dob = dov.astype(BF16)
        cb_ref[...] = jnp.broadcast_to(_head_column(c_ref[...], 2 * p + half), (B, LANES)) - st_ref[0]
        db_ref[...] = jnp.broadcast_to(jnp.sum(dov * o_ref[...], axis=1, keepdims=True), (B, LANES))
        row = lax.broadcasted_iota(I32, (B, B), 0)
        col = lax.broadcasted_iota(I32, (B, B), 1)
        dqa_ref[...] = jnp.zeros_like(dqa_ref)
        rs_ref[...] = jnp.zeros_like(rs_ref)

        def block(j, diag):
            off = pl.multiple_of(j * B, B)
            kj = k_ref[pl.ds(off, B), :]
            vj = v_ref[pl.ds(off, B), :]
            pv = jnp.exp(_dot_nt(qs, kj) + (_lane_tile(cb_ref[...], B) - cr_ref[0, pl.ds(j, 1), :]))
            if diag:
                pv = jnp.where(col <= row, pv, 0.0)
            ds = pv * (_dot_nt(dob, vj) - _lane_tile(db_ref[...], B))
            dsb = ds.astype(BF16)
            dqa_ref[...] += _dot(dsb, kj)
            dk_ref[pl.ds(off, B), :] += _dot_tn(dsb, qs)
            dv_ref[pl.ds(off, B), :] += _dot_tn(pv.astype(BF16), dob)
            dc_ref[0, half, pl.ds(j, 1), :] -= jnp.sum(ds, axis=0, keepdims=True)
            rs_ref[...] += jnp.sum(ds, axis=1, keepdims=True)

        def step(j, carry):
            block(j, False)
            return carry

        lax.fori_loop(js_ref[2 * p + half, i], i, step, 0)
        block(i, True)
        res = jnp.where(hm, dqa_ref[...] * SCALE, 0.0)

        @pl.when(half == 0)
        def _():
            dq_ref[...] = res

        @pl.when(half == 1)
        def _():
            dq_ref[...] += res

        dc_ref[0, half, pl.ds(i, 1), :] += jnp.transpose(jnp.broadcast_to(rs_ref[...], (B, LANES)))[0:1, :]

    W = n_pair * LANES
    return pl.pallas_call(
        body, name="fox_attn_bwd",
        grid_spec=pltpu.PrefetchScalarGridSpec(
            num_scalar_prefetch=1, grid=(n_pair, nq, 2),
            in_specs=[pl.BlockSpec((B, LANES), lambda p, i, h, js: (i, 3 * n_pair + p)),
                      pl.BlockSpec((S, LANES), lambda p, i, h, js: (0, 4 * n_pair + p)),
                      pl.BlockSpec((S, LANES), lambda p, i, h, js: (0, 5 * n_pair + p)),
                      pl.BlockSpec((B, LANES), lambda p, i, h, js: (i, p)),
                      pl.BlockSpec((B, LANES), lambda p, i, h, js: (i, p)),
                      pl.BlockSpec((1, B, LANES), lambda p, i, h, js: (2 * p + h, i, 0)),
                      pl.BlockSpec((B, LANES), lambda p, i, h, js: (i, 0)),
                      pl.BlockSpec((1, nq, B), lambda p, i, h, js: (2 * p + h, 0, 0))],
            out_specs=[pl.BlockSpec((B, LANES), lambda p, i, h, js: (i, p)),
                       pl.BlockSpec((S, LANES), lambda p, i, h, js: (0, p)),
                       pl.BlockSpec((S, LANES), lambda p, i, h, js: (0, p)),
                       pl.BlockSpec((1, 2, nq, B), lambda p, i, h, js: (p, 0, 0, 0))],
            scratch_shapes=[pltpu.VMEM((B, LANES), F32), pltpu.VMEM((B, 1), F32), pltpu.VMEM((B, LANES), F32),
                            pltpu.VMEM((B, LANES), F32)]),
        out_shape=[jax.ShapeDtypeStruct((S, W), F32)] * 3 + [jax.ShapeDtypeStruct((n_pair, 2, nq, B), F32)],
        compiler_params=_cparams(("arbitrary", "arbitrary", "arbitrary")),
    )(js, qkv, qkv, qkv, do, o, st, c, c_rows)


def _adam(w, g, m, v):
    m = ADAM_B1 * m + (1.0 - ADAM_B1) * g
    v = ADAM_B2 * v + (1.0 - ADAM_B2) * (g * g)
    m_hat = m / (1.0 - ADAM_B1 ** ADAM_STEP)
    v_hat = v / (1.0 - ADAM_B2 ** ADAM_STEP)
    delta = -ADAM_LR * (m_hat / (jnp.sqrt(v_hat) + ADAM_EPS) + ADAM_WD * w)
    return delta, m, v


def _reduce_adam(landing, w, m, v, name, extra=None):
    R, C = w.shape
    TR = next(t for t in (256, 128, R) if R % t == 0)
    more = [] if extra is None else [extra]

    def body(*refs):
        l_ref, w_ref, m_ref, v_ref = refs[:4]
        g_ref, d_ref, nm_ref, nv_ref = refs[4 + len(more):]
        g = l_ref[0].astype(F32)
        for s in range(1, N_DEV):
            g = g + l_ref[s].astype(F32)
        if more:
            g = g + refs[4][...]
        d, nm, nv = _adam(w_ref[...], g, m_ref[...], v_ref[...])
        g_ref[...] = g
        d_ref[...] = d
        nm_ref[...] = nm
        nv_ref[...] = nv

    blk = pl.BlockSpec((TR, C), lambda i: (i, 0))
    return pl.pallas_call(
        body, name=name, grid=(R // TR,),
        in_specs=[pl.BlockSpec((N_DEV, TR, C), lambda i: (0, i, 0)), blk, blk, blk] + [blk] * len(more),
        out_specs=[blk] * 4,
        out_shape=[jax.ShapeDtypeStruct((R, C), F32)] * 4,
        compiler_params=_cparams(("arbitrary",)),
    )(landing, w, m, v, *more)


def _reduce_adam_small(landing, w, m, v):
    R, C = w.shape

    def body(l_ref, w_ref, m_ref, v_ref, g_ref, d_ref, nm_ref, nv_ref, loss_ref):
        g = l_ref[0]
        for s in range(1, N_DEV):
            g = g + l_ref[s]
        d, nm, nv = _adam(w_ref[...], g, m_ref[...], v_ref[...])
        g_ref[...] = g
        d_ref[...] = d
        nm_ref[...] = nm
        nv_ref[...] = nv
        loss_ref[...] = jnp.broadcast_to(0.5 * jnp.sum(g[7:8, :], axis=1, keepdims=True), (1, LANES))

    return pl.pallas_call(
        body, name="reduce_adam_small",
        out_shape=[jax.ShapeDtypeStruct((R, C), F32)] * 4 + [jax.ShapeDtypeStruct((1, LANES), F32)],
    )(landing, w, m, v)


def _pad_lanes(a, width):
    return jnp.pad(a, ((0, 0), (0, width - a.shape[1])))


def _pack_small(D, n_fox, g_cat, l1g, l1b, l2g, l2b, bf, last, tail):
    return jnp.concatenate([g_cat, l1g, l1b, l2g, l2b, _pad_lanes(bf, D), jnp.zeros((1, D), F32), last, tail],
                           axis=0)


def kernel(x, w_in, b_f, g_sb, g_fox, w_out, ln1_g, ln1_b, ln2_g, ln2_b, w_gate_up, w_down, loss_target, m_w_in, m_b_f, m_g_sb, m_g_fox, m_w_out, m_ln1_g, m_ln1_b, m_ln2_g, m_ln2_b, m_w_gate_up, m_w_down, v_w_in, v_b_f, v_g_sb, v_g_fox, v_w_out, v_ln1_g, v_ln1_b, v_ln2_g, v_ln2_b, v_w_gate_up, v_w_down):
    x2, tgt = x[0], loss_target[0]
    S, D = x2.shape
    W = D // 2
    n_pair = W // LANES
    n_fox = W // HEAD_DIM
    F = w_down.shape[1] * N_DEV

    g_in = _all_gather_two_level(w_in[0].astype(BF16), "w_in_all_gather")
    w_in_full = g_in.transpose(1, 0, 2).reshape(D, -1)
    w_qkv = w_in_full[:, :6 * W]
    w_f = _pad_lanes(w_in_full[:, 6 * W:], LANES)
    gmat = _tri(D, lambda r, c: (r // HEAD_DIM) == (c // HEAD_DIM))
    g_cat = jnp.concatenate([g_sb, g_fox], axis=1)

    qkv, u, c, ksq = _proj_fwd(x2, w_qkv, w_f, _pad_lanes(b_f, LANES), n_fox)
    c_t = c[:, :n_fox].T
    c_rows = c_t.reshape(n_fox, S // FOX_BLOCK, FOX_BLOCK)
    kmax = jnp.sqrt(ksq[0:1]) * NORM_SLACK
    o_sb, st_sb, js_sb, g_out, g_gu, g_dn = _sb_fwd(
        qkv, n_pair, [w_out[0].astype(BF16), w_gate_up[0].astype(BF16), w_down[0].astype(BF16)])
    w_o = g_out.reshape(D, D)
    w_gu = g_gu.transpose(1, 0, 2).reshape(D, 2 * F)
    w_dn = g_dn.reshape(F, D)
    o_fx, st_fx, js_fx = _fox_fwd(qkv, c, c_t.reshape(n_fox, S // FOX_BK, FOX_BK), kmax, n_pair)
    js_fx = js_fx // (FOX_BLOCK // FOX_BK)
    h1, xh1, rs1, on_b, rr = _post_attn_fwd(o_sb, o_fx, x2, g_cat, gmat, w_o, ln1_g, ln1_b)
    gu, act_b, dyp, sm2 = _mlp_fwd(h1, tgt, w_gu, w_dn, ln2_g, ln2_b)

    dgu_b, dh1 = _mlp_bwd(gu, dyp, w_gu.T, w_dn.T)
    gw_gu = _matmul_tn(h1, [dgu_b], "grad_w_gate_up", n_split=2)
    gw_dn = _matmul_tn(act_b, [dyp], "grad_w_down")
    dxa, dmix_b, do_sb, do_fx, sm1, l_gu, l_dn = _post_attn_bwd(
        dh1, xh1, rs1, ln1_g, o_sb, o_fx, rr, g_cat, gmat, w_o.T,
        [gw_gu.reshape(D, N_DEV, -1).transpose(1, 0, 2), gw_dn.reshape(N_DEV, F // N_DEV, D)])
    dq_sb, dk_sb, dv_sb = _sb_bwd(qkv, do_sb, st_sb, js_sb, n_pair)
    dq_fx, dk_fx, dv_fx, dcr = _fox_bwd(qkv, do_fx, o_fx, st_fx, c, c_rows, js_fx, n_pair)
    dc = _pad_lanes(dcr.reshape(n_fox, S).T, LANES)
    pieces = [dq_sb, dk_sb, dv_sb, dq_fx, dk_fx, dv_fx]
    gw_qkv = _matmul_tn(x2, pieces, "grad_w_qkv", pad_cols=LANES)[:, :6 * W + n_fox]
    gw_out = _matmul_tn(on_b, [dmix_b], "grad_w_out")
    dx, gw_f, sm0, l_in, l_out = _proj_bwd(
        dxa, x2, pieces, dc, u, w_qkv.T, w_f.T, n_fox,
        [gw_qkv.reshape(D, N_DEV, -1).transpose(1, 0, 2), gw_out.reshape(N_DEV, D // N_DEV, D)])
    gw_f = gw_f[:, :n_fox]

    small = _pack_small(D, n_fox, sm1[2:3], sm1[0:1], sm1[1:2], sm2[0:1], sm2[1:2], sm0[0:1, :n_fox],
                        sm2[2:3] * (1.0 / D), gw_f.T)
    (l_small,) = _exchange_call([jnp.broadcast_to(small[None], (N_DEV,) + small.shape)], True, "small_exchange")

    zero = jnp.zeros((1, D), F32)
    pack = lambda gc, a, b_, c_, d_, bf: _pack_small(D, n_fox, gc, a, b_, c_, d_, bf, zero,
                                                     jnp.zeros((n_fox, D), F32))
    r_small = _reduce_adam_small(
        l_small,
        pack(g_cat, ln1_g, ln1_b, ln2_g, ln2_b, b_f),
        pack(jnp.concatenate([m_g_sb, m_g_fox], axis=1), m_ln1_g, m_ln1_b, m_ln2_g, m_ln2_b, m_b_f),
        pack(jnp.concatenate([v_g_sb, v_g_fox], axis=1), v_ln1_g, v_ln1_b, v_ln2_g, v_ln2_b, v_b_f))
    loss = r_small[4][0, 0]
    cols = w_in.shape[2]
    gf_cols = jnp.pad(r_small[0][8:8 + n_fox].T, ((0, 0), (cols - n_fox, 0)))
    extra = jnp.where(_my_index() == N_DEV - 1, gf_cols, 0.0)
    r_in = _reduce_adam(l_in, w_in[0], m_w_in[0], v_w_in[0], "reduce_adam_w_in", extra)
    r_out = _reduce_adam(l_out, w_out[0], m_w_out[0], v_w_out[0], "reduce_adam_w_out")
    r_gu = _reduce_adam(l_gu, w_gate_up[0], m_w_gate_up[0], v_w_gate_up[0], "reduce_adam_w_gate_up")
    r_dn = _reduce_adam(l_dn, w_down[0], m_w_down[0], v_w_down[0], "reduce_adam_w_down")

    def unpack(kind):
        big = [r_in[kind][None], None, None, None, r_out[kind][None], None, None, None, None,
               r_gu[kind][None], r_dn[kind][None]]
        s = r_small[kind]
        big[1] = s[5:6, :n_fox]
        big[2] = s[0:1, :W]
        big[3] = s[0:1, W:]
        big[5], big[6], big[7], big[8] = s[1:2], s[2:3], s[3:4], s[4:5]
        return big

    return (loss, dx[None], *unpack(0), *unpack(1), *unpack(2), *unpack(3))
```

```python
import functools

import jax
import jax.numpy as jnp
from jax import lax
from jax.experimental import pallas as pl
from jax.experimental.pallas import tpu as pltpu

F32 = jnp.float32
BF16 = jnp.bfloat16
I32 = jnp.int32

N_DEV = 8
HEAD_DIM = 64
LANES = 128
SCALE = HEAD_DIM ** -0.5
ALPHA = 2.0 ** 0.25
LN_EPS = 1e-5
RMS_EPS = 1e-6
ADAM_LR, ADAM_B1, ADAM_B2, ADAM_EPS, ADAM_WD, ADAM_STEP = 0.001, 0.9, 0.999, 1e-08, 0.01, 10
NEG_BIG = -1e30
NORM_SLACK = 1.01
EXP_ZERO = 88.5
VMEM_LIMIT = 60 * 1024 * 1024
ROW_TILE = 256
SB_BLOCK = 256
FOX_BLOCK = 512
FOX_BQ, FOX_BK = 512, 256
assert FOX_BQ == FOX_BLOCK and FOX_BLOCK % FOX_BK == 0
MESH = pl.DeviceIdType.MESH


def _cparams(sem):
    return pltpu.CompilerParams(dimension_semantics=sem, vmem_limit_bytes=VMEM_LIMIT)


def _dot(a, b):
    return jnp.dot(a, b, preferred_element_type=F32)


def _dot_nt(a, b):
    return lax.dot_general(a, b, (((1,), (1,)), ((), ())), preferred_element_type=F32)


def _dot_tn(a, b):
    return lax.dot_general(a, b, (((0,), (0,)), ((), ())), preferred_element_type=F32)


def _split2(a):
    hi = a.astype(BF16)
    lo = (a - hi.astype(F32)).astype(BF16)
    return hi, lo


def _split3(a):
    hi = a.astype(BF16)
    r1 = a - hi.astype(F32)
    mid = r1.astype(BF16)
    lo = (r1 - mid.astype(F32)).astype(BF16)
    return hi, mid, lo


def _dot_acc(a, m):
    hi, lo = _split2(a)
    return _dot(jnp.concatenate([hi, lo], axis=1), jnp.concatenate([m, m], axis=0))


def _tri(n, fn):
    r = lax.broadcasted_iota(I32, (n, n), 0)
    c = lax.broadcasted_iota(I32, (n, n), 1)
    return fn(r, c).astype(BF16)


def _full(shape):
    nd = len(shape)
    return pl.BlockSpec(shape, lambda *_: (0,) * nd)


def _peer(k):
    x, y, c = lax.axis_index("x"), lax.axis_index("y"), lax.axis_index("c")
    return (1 - x if k & 4 else x, 1 - y if k & 2 else y, 1 - c if k & 1 else c)


def _my_index():
    return 4 * lax.axis_index("x") + 2 * lax.axis_index("y") + lax.axis_index("c")


class _Exchange:
    def __init__(self, n, scatter):
        self.n, self.scatter = n, scatter

    def sem_shapes(self):
        return [pltpu.SemaphoreType.DMA(((N_DEV - 1) * self.n,)), pltpu.SemaphoreType.DMA(((N_DEV - 1) * self.n,)),
                pltpu.SemaphoreType.DMA((self.n,))]

    def out_shapes(self, arrays):
        if self.scatter:
            return [jax.ShapeDtypeStruct(s.shape, s.dtype) for s in arrays]
        return [jax.ShapeDtypeStruct((N_DEV,) + s.shape, s.dtype) for s in arrays]

    def _copies(self, ins, outs, sems, landing):
        send_sems, recv_sems, local_sems = sems
        me = _my_index()
        src = lambda a, d: ins[a].at[d] if self.scatter else ins[a]
        local = [pltpu.make_async_copy(src(a, me), outs[a].at[me], local_sems.at[a]) for a in range(self.n)]
        remote = [pltpu.make_async_remote_copy(
            src_ref=src(a, me ^ k), dst_ref=outs[a].at[me ^ k if landing else me],
            send_sem=send_sems.at[(k - 1) * self.n + a], recv_sem=recv_sems.at[(k - 1) * self.n + a],
            device_id=_peer(k), device_id_type=MESH) for k in range(1, N_DEV) for a in range(self.n)]
        return local, remote

    def start(self, ins, outs, sems):
        local, sent = self._copies(ins, outs, sems, landing=False)
        for cp in local + sent:
            cp.start()

    def finish(self, ins, outs, sems):
        local, landed = self._copies(ins, outs, sems, landing=True)
        for cp in landed:
            cp.wait_recv()
        for cp in landed:
            cp.wait_send()
        for cp in local:
            cp.wait()


def _all_gather_two_level(shard, name):
    def body(x_ref, out_ref, send_sems, recv_sems, local_sem):
        x, y, c = lax.axis_index("x"), lax.axis_index("y"), lax.axis_index("c")
        me, sibling = (x, y, c), (x, y, 1 - c)
        chips = [(1 - x, y), (x, 1 - y), (1 - x, 1 - y)]

        def slot(px, py, pc):
            return out_ref.at[4 * px + 2 * py + pc]

        def copy(k, block, to, src=None):
            return pltpu.make_async_remote_copy(
                src_ref=slot(*block) if src is None else src, dst_ref=slot(*block),
                send_sem=send_sems.at[k], recv_sem=recv_sems.at[k], device_id=to, device_id_type=MESH)

        mine = pltpu.make_async_copy(x_ref, slot(*me), local_sem)
        mine.start()
        first = [copy(0, me, sibling, src=x_ref)]
        first += [copy(1 + j, me, (*chip, c), src=x_ref) for j, chip in enumerate(chips)]
        for cp in first:
            cp.start()
        passed = [copy(4 + j, (*chip, c), sibling) for j, chip in enumerate(chips)]
        for j, chip in enumerate(chips):
            copy(1 + j, (*chip, c), me).wait_recv()
            passed[j].start()
        copy(0, sibling, me).wait_recv()
        for j, chip in enumerate(chips):
            copy(4 + j, (*chip, 1 - c), me).wait_recv()
        for cp in first + passed:
            cp.wait_send()
        mine.wait()

    any_spec = pl.BlockSpec(memory_space=pl.ANY)
    return pl.pallas_call(
        body, name=name, out_shape=jax.ShapeDtypeStruct((N_DEV,) + shard.shape, shard.dtype),
        in_specs=[any_spec], out_specs=any_spec,
        scratch_shapes=[pltpu.SemaphoreType.DMA((N_DEV - 1,)), pltpu.SemaphoreType.DMA((N_DEV - 1,)),
                        pltpu.SemaphoreType.DMA],
    )(shard)


def _exchange_call(arrays, scatter, name):
    n = len(arrays)
    ex = _Exchange(n, scatter)

    def body(*refs):
        ins, outs, sems = refs[:n], refs[n:2 * n], refs[2 * n:]
        ex.start(ins, outs, sems)
        ex.finish(ins, outs, sems)

    any_spec = pl.BlockSpec(memory_space=pl.ANY)
    return pl.pallas_call(
        body, name=name, out_shape=ex.out_shapes(arrays),
        in_specs=[any_spec] * n, out_specs=[any_spec] * n, scratch_shapes=ex.sem_shapes(),
    )(*arrays)


def _log_sigmoid(u):
    return jnp.minimum(u, 0.0) - jnp.log1p(jnp.exp(-jnp.abs(u)))


def _proj_fwd(x, w_qkv, w_f, bf_pad, n_fox):
    S, D = x.shape
    N = w_qkv.shape[1]
    W = D // 2
    TM = ROW_TILE
    tri = _tri(TM, lambda r, c: c <= r)
    r_ = lax.broadcasted_iota(I32, (W, LANES), 0)
    c_ = lax.broadcasted_iota(I32, (W, LANES), 1)
    head_of = (r_ // HEAD_DIM == c_).astype(BF16)

    def body(x_ref, wq_ref, wf_ref, bf_ref, tri_ref, ho_ref, qkv_ref, u_ref, c_ref, ksq_ref, run_ref):
        @pl.when(pl.program_id(0) == 0)
        def _():
            run_ref[...] = jnp.zeros_like(run_ref)
            ksq_ref[...] = jnp.zeros_like(ksq_ref)

        xb = x_ref[...].astype(BF16)
        for n0 in range(0, N, D):
            chunk = _dot(xb, wq_ref[:, n0:n0 + D]).astype(BF16)
            qkv_ref[:, n0:n0 + D] = chunk
            if n0 == 4 * W:
                kf = chunk[:, :W].astype(F32)
                ksq = jnp.max(_dot_acc(kf * kf, ho_ref[...]), axis=0, keepdims=True)
                ksq_ref[...] = jnp.maximum(ksq_ref[...], ksq)
        u = _dot(xb, wf_ref[...]) + bf_ref[...]
        lane = lax.broadcasted_iota(I32, u.shape, 1)
        logf = jnp.where(lane < n_fox, _log_sigmoid(u), 0.0)
        u_ref[...] = u
        hi, mid, lo = _split3(logf)
        t = tri_ref[...]
        cs = _dot(t, hi) + _dot(t, mid) + _dot(t, lo) + run_ref[...]
        c_ref[...] = cs
        run_ref[...] = cs[TM - 1:TM, :]

    return pl.pallas_call(
        body, name="proj_fwd", grid=(S // TM,),
        in_specs=[pl.BlockSpec((TM, D), lambda i: (i, 0)), _full(w_qkv.shape), _full(w_f.shape),
                  _full(bf_pad.shape), _full(tri.shape), _full(head_of.shape)],
        out_specs=[pl.BlockSpec((TM, N), lambda i: (i, 0)), pl.BlockSpec((TM, LANES), lambda i: (i, 0)),
                   pl.BlockSpec((TM, LANES), lambda i: (i, 0)), _full((8, LANES))],
        out_shape=[jax.ShapeDtypeStruct((S, N), BF16), jax.ShapeDtypeStruct((S, LANES), F32),
                   jax.ShapeDtypeStruct((S, LANES), F32), jax.ShapeDtypeStruct((8, LANES), F32)],
        scratch_shapes=[pltpu.VMEM((1, LANES), F32)],
        compiler_params=_cparams(("arbitrary",)),
    )(x, w_qkv, w_f, bf_pad, tri, head_of)


def _post_attn_fwd(o_sb, o_fx, x, g_cat, gmat, w_out, ln_g, ln_b):
    S, D = x.shape
    H = D // 2
    TM = ROW_TILE

    def body(osb_ref, ofx_ref, x_ref, g_ref, gm_ref, wo_ref, lg_ref, lb_ref,
             h1_ref, xh_ref, rs_ref, on_ref, rr_ref):
        o = jnp.concatenate([osb_ref[...], ofx_ref[...]], axis=1)
        ms = _dot_acc(o * o, gm_ref[...]) * (1.0 / HEAD_DIM)
        r = lax.rsqrt(ms + RMS_EPS)
        onb = (o * r * g_ref[...]).astype(BF16)
        hp = ALPHA * x_ref[...] + _dot(onb, wo_ref[...])
        mu = jnp.mean(hp, axis=-1, keepdims=True)
        d = hp - mu
        rstd = lax.rsqrt(jnp.mean(d * d, axis=-1, keepdims=True) + LN_EPS)
        xh = d * rstd
        h1_ref[...] = xh * lg_ref[...] + lb_ref[...]
        xh_ref[...] = xh
        rs_ref[...] = jnp.broadcast_to(rstd, (TM, LANES))
        on_ref[...] = onb
        rr_ref[...] = r

    row = lambda w: pl.BlockSpec((TM, w), lambda i: (i, 0))
    return pl.pallas_call(
        body, name="post_attn_fwd", grid=(S // TM,),
        in_specs=[row(H), row(H), row(D), _full((1, D)), _full((D, D)), _full((D, D)), _full((1, D)), _full((1, D))],
        out_specs=[row(D), row(D), row(LANES), row(D), row(D)],
        out_shape=[jax.ShapeDtypeStruct((S, D), F32), jax.ShapeDtypeStruct((S, D), F32),
                   jax.ShapeDtypeStruct((S, LANES), F32), jax.ShapeDtypeStruct((S, D), BF16),
                   jax.ShapeDtypeStruct((S, D), F32)],
        compiler_params=_cparams(("arbitrary",)),
    )(o_sb, o_fx, x, g_cat, gmat, w_out, ln_g, ln_b)


def _ln_bwd(dxh, xh, rstd):
    m1 = jnp.mean(dxh, axis=-1, keepdims=True)
    m2 = jnp.mean(dxh * xh, axis=-1, keepdims=True)
    return rstd * (dxh - m1 - xh * m2)


def _mlp_fwd(h1, target, w_gu, w_dn, ln_g, ln_b):
    S, D = h1.shape
    F = w_dn.shape[0]
    TM = ROW_TILE
    FC = F // 2

    def body(h1_ref, tg_ref, wgu_hbm, wdn_hbm, lg_ref, lb_ref, gu_ref, act_ref, dyp_ref, sm_ref, wgu, wdn):
        @pl.when(pl.program_id(0) == 0)
        def _():
            pltpu.sync_copy(wgu_hbm, wgu)
            pltpu.sync_copy(wdn_hbm, wdn)
            sm_ref[...] = jnp.zeros_like(sm_ref)

        h1v = h1_ref[...]
        hb = h1v.astype(BF16)
        ff = jnp.zeros((TM, D), F32)
        for c0 in range(0, F, FC):
            g = _dot(hb, wgu[:, c0:c0 + FC])
            u = _dot(hb, wgu[:, F + c0:F + c0 + FC])
            gu_ref[:, c0:c0 + FC] = g
            gu_ref[:, F + c0:F + c0 + FC] = u
            ab = ((g * jax.nn.sigmoid(g)) * u).astype(BF16)
            act_ref[:, c0:c0 + FC] = ab
            ff = ff + _dot(ab, wdn[c0:c0 + FC, :])
        yp = ALPHA * h1v + ff
        mu = jnp.mean(yp, axis=-1, keepdims=True)
        d = yp - mu
        rstd = lax.rsqrt(jnp.mean(d * d, axis=-1, keepdims=True) + LN_EPS)
        xh = d * rstd
        err = (xh * lg_ref[...] + lb_ref[...]) - tg_ref[...]
        dy = err * (1.0 / D)
        sm_ref[0:1, :] += jnp.sum(dy * xh, axis=0, keepdims=True)
        sm_ref[1:2, :] += jnp.sum(dy, axis=0, keepdims=True)
        sm_ref[2:3, :] += jnp.sum(err * err, axis=0, keepdims=True)
        dyp_ref[...] = _ln_bwd(dy * lg_ref[...], xh, rstd)

    row = lambda w: pl.BlockSpec((TM, w), lambda i: (i, 0))
    hbm = pl.BlockSpec(memory_space=pl.ANY)
    return pl.pallas_call(
        body, name="mlp_fwd", grid=(S // TM,),
        in_specs=[row(D), row(D), hbm, hbm, _full((1, D)), _full((1, D))],
        out_specs=[row(2 * F), row(F), row(D), _full((8, D))],
        out_shape=[jax.ShapeDtypeStruct((S, 2 * F), F32), jax.ShapeDtypeStruct((S, F), BF16),
                   jax.ShapeDtypeStruct((S, D), F32), jax.ShapeDtypeStruct((8, D), F32)],
        scratch_shapes=[pltpu.VMEM(w_gu.shape, BF16), pltpu.VMEM(w_dn.shape, BF16)],
        compiler_params=_cparams(("arbitrary",)),
    )(h1, target, w_gu, w_dn, ln_g, ln_b)


def _mlp_bwd(gu, dyp, w_guT, w_dnT):
    S, D = dyp.shape
    F = w_dnT.shape[1]
    TM = ROW_TILE
    FC = F // 2

    def body(gu_ref, dyp_ref, wguT_hbm, wdnT_hbm, dgu_ref, dh1_ref, wguT, wdnT):
        @pl.when(pl.program_id(0) == 0)
        def _():
            pltpu.sync_copy(wguT_hbm, wguT)
            pltpu.sync_copy(wdnT_hbm, wdnT)

        dypv = dyp_ref[...]
        db = dypv.astype(BF16)
        dh1 = ALPHA * dypv
        for c0 in range(0, F, FC):
            dact = _dot(db, wdnT[:, c0:c0 + FC])
            g = gu_ref[:, c0:c0 + FC]
            u = gu_ref[:, F + c0:F + c0 + FC]
            sg = jax.nn.sigmoid(g)
            dgb = (dact * u * (sg * (1.0 + g * (1.0 - sg)))).astype(BF16)
            dub = (dact * (g * sg)).astype(BF16)
            dgu_ref[:, c0:c0 + FC] = dgb
            dgu_ref[:, F + c0:F + c0 + FC] = dub
            dh1 = dh1 + _dot(dgb, wguT[c0:c0 + FC, :]) + _dot(dub, wguT[F + c0:F + c0 + FC, :])
        dh1_ref[...] = dh1

    row = lambda w: pl.BlockSpec((TM, w), lambda i: (i, 0))
    hbm = pl.BlockSpec(memory_space=pl.ANY)
    return pl.pallas_call(
        body, name="mlp_bwd", grid=(S // TM,),
        in_specs=[row(2 * F), row(D), hbm, hbm],
        out_specs=[row(2 * F), row(D)],
        out_shape=[jax.ShapeDtypeStruct((S, 2 * F), BF16), jax.ShapeDtypeStruct((S, D), F32)],
        scratch_shapes=[pltpu.VMEM(w_guT.shape, BF16), pltpu.VMEM(w_dnT.shape, BF16)],
        compiler_params=_cparams(("arbitrary",)),
    )(gu, dyp, w_guT, w_dnT)


def _post_attn_bwd(dh1, xh, rs, ln_g, o_sb, o_fx, rr, g_cat, gmat, w_outT, chunked):
    S, D = dh1.shape
    H = D // 2
    TM = ROW_TILE
    nT = S // TM
    n_ch = len(chunked)
    ex = _Exchange(n_ch, scatter=True)

    def body(*refs):
        dh1_ref, xh_ref, rs_ref, lg_ref, osb_ref, ofx_ref, rr_ref, g_ref, gm_ref, woT_ref = refs[:10]
        ch_in = refs[10:10 + n_ch]
        dxa_ref, dmix_ref, dosb_ref, dofx_ref, sm_ref = refs[10 + n_ch:15 + n_ch]
        ch_out = refs[15 + n_ch:15 + 2 * n_ch]
        sems = refs[15 + 2 * n_ch:]

        @pl.when(pl.program_id(0) == 0)
        def _():
            ex.start(ch_in, ch_out, sems)
            sm_ref[...] = jnp.zeros_like(sm_ref)

        dh = dh1_ref[...]
        xhv = xh_ref[...]
        dhp = _ln_bwd(dh * lg_ref[...], xhv, rs_ref[:, 0:1])
        dxa_ref[...] = ALPHA * dhp
        dmb = dhp.astype(BF16)
        dmix_ref[...] = dmb
        don = _dot(dmb, woT_ref[...])
        o = jnp.concatenate([osb_ref[...], ofx_ref[...]], axis=1)
        r = rr_ref[...]
        u = don * g_ref[...]
        t = _dot_acc(u * o, gm_ref[...]) * (1.0 / HEAD_DIM)
        do = r * u - o * (r * r * r) * t
        dosb_ref[...] = do[:, :H]
        dofx_ref[...] = do[:, H:]
        sm_ref[0:1, :] += jnp.sum(dh * xhv, axis=0, keepdims=True)
        sm_ref[1:2, :] += jnp.sum(dh, axis=0, keepdims=True)
        sm_ref[2:3, :] += jnp.sum(don * o * r, axis=0, keepdims=True)

        @pl.when(pl.program_id(0) == nT - 1)
        def _():
            ex.finish(ch_in, ch_out, sems)

    row = lambda w: pl.BlockSpec((TM, w), lambda i: (i, 0))
    any_spec = pl.BlockSpec(memory_space=pl.ANY)
    return pl.pallas_call(
        body, name="post_attn_bwd", grid=(nT,),
        in_specs=[row(D), row(D), row(LANES), _full((1, D)), row(H), row(H), row(D), _full((1, D)),
                  _full((D, D)), _full((D, D))] + [any_spec] * n_ch,
        out_specs=[row(D), row(D), row(H), row(H), _full((8, D))] + [any_spec] * n_ch,
        out_shape=[jax.ShapeDtypeStruct((S, D), F32), jax.ShapeDtypeStruct((S, D), BF16),
                   jax.ShapeDtypeStruct((S, H), F32), jax.ShapeDtypeStruct((S, H), F32),
                   jax.ShapeDtypeStruct((8, D), F32)] + ex.out_shapes(chunked),
        scratch_shapes=ex.sem_shapes(),
        compiler_params=_cparams(("arbitrary",)),
    )(dh1, xh, rs, ln_g, o_sb, o_fx, rr, g_cat, gmat, w_outT, *chunked)


def _proj_bwd(dxa, x, pieces, dc, u, w_qkvT, w_fT, n_fox, chunked):
    S, D = dxa.shape
    H = D // 2
    TM = ROW_TILE
    nT = S // TM
    tri = _tri(TM, lambda r, c: c >= r)
    n_p = len(pieces)
    n_ch = len(chunked)
    ex = _Exchange(n_ch, scatter=True)

    def body(*refs):
        dxa_ref, x_ref = refs[:2]
        p_refs = refs[2:2 + n_p]
        dc_ref, u_ref, wq_ref, wf_ref, tri_ref = refs[2 + n_p:7 + n_p]
        ch_in = refs[7 + n_p:7 + n_p + n_ch]
        dx_ref, gwf_ref, sm_ref = refs[7 + n_p + n_ch:10 + n_p + n_ch]
        ch_out = refs[10 + n_p + n_ch:10 + n_p + 2 * n_ch]
        run_ref = refs[10 + n_p + 2 * n_ch]
        sems = refs[11 + n_p + 2 * n_ch:]

        @pl.when(pl.program_id(0) == 0)
        def _():
            ex.start(ch_in, ch_out, sems)
            run_ref[...] = jnp.zeros_like(run_ref)
            sm_ref[...] = jnp.zeros_like(sm_ref)
            gwf_ref[...] = jnp.zeros_like(gwf_ref)

        hi, mid, lo = _split3(dc_ref[...])
        t = tri_ref[...]
        dlogf = _dot(t, hi) + _dot(t, mid) + _dot(t, lo) + run_ref[...]
        run_ref[...] = dlogf[0:1, :]
        uv = u_ref[...]
        lane = lax.broadcasted_iota(I32, uv.shape, 1)
        df = jnp.where(lane < n_fox, dlogf * jax.nn.sigmoid(-uv), 0.0)
        sm_ref[0:1, :] += jnp.sum(df, axis=0, keepdims=True)
        dfb = df.astype(BF16)
        gwf_ref[...] += _dot_tn(x_ref[...].astype(BF16), dfb)
        acc = dxa_ref[...] + _dot(dfb, wf_ref[...])
        for a in range(n_p):
            acc = acc + _dot(p_refs[a][...].astype(BF16), wq_ref[a * H:(a + 1) * H, :])
        dx_ref[...] = acc

        @pl.when(pl.program_id(0) == nT - 1)
        def _():
            ex.finish(ch_in, ch_out, sems)

    rev = lambda w: pl.BlockSpec((TM, w), lambda i: (nT - 1 - i, 0))
    any_spec = pl.BlockSpec(memory_space=pl.ANY)
    return pl.pallas_call(
        body, name="proj_bwd", grid=(nT,),
        in_specs=[rev(D), rev(D)] + [rev(H)] * n_p + [rev(LANES), rev(LANES), _full(w_qkvT.shape),
                                                       _full(w_fT.shape), _full(tri.shape)] + [any_spec] * n_ch,
        out_specs=[rev(D), _full((D, LANES)), _full((8, LANES))] + [any_spec] * n_ch,
        out_shape=[jax.ShapeDtypeStruct((S, D), F32), jax.ShapeDtypeStruct((D, LANES), F32),
                   jax.ShapeDtypeStruct((8, LANES), F32)] + ex.out_shapes(chunked),
        scratch_shapes=[pltpu.VMEM((1, LANES), F32)] + ex.sem_shapes(),
        compiler_params=_cparams(("arbitrary",)),
    )(dxa, x, *pieces, dc, u, w_qkvT, w_fT, tri, *chunked)


def _matmul_tn(a, bs, name, n_split=1, pad_cols=0):
    S, M = a.shape
    widths = [b.shape[1] for b in bs]
    N = sum(widths) + pad_cols
    assert n_split == 1 or (len(bs) == 1 and pad_cols == 0)
    TK = 512 if S % 512 == 0 else ROW_TILE
    MC = 512 if M % 512 == 0 else 256
    nb = len(bs)

    def body(*refs):
        a_ref, b_refs, o_ref, acc_ref = refs[0], refs[1:1 + nb], refs[1 + nb], refs[2 + nb]

        @pl.when(pl.program_id(1) == 0)
        def _():
            acc_ref[...] = jnp.zeros_like(acc_ref)

        n0 = 0
        for b_ref in b_refs:
            bv = b_ref[...].astype(BF16)
            w = bv.shape[1]
            for m0 in range(0, M, MC):
                acc_ref[m0:m0 + MC, n0:n0 + w] += _dot_tn(a_ref[:, m0:m0 + MC].astype(BF16), bv)
            n0 += w

        @pl.when(pl.program_id(1) == S // TK - 1)
        def _():
            o_ref[...] = acc_ref[...].astype(BF16)

    return pl.pallas_call(
        body, name=name, grid=(n_split, S // TK),
        in_specs=[pl.BlockSpec((TK, M), lambda n, k: (k, 0))]
        + [pl.BlockSpec((TK, w // n_split), lambda n, k: (k, n)) for w in widths],
        out_specs=pl.BlockSpec((M, N // n_split), lambda n, k: (0, n)),
        out_shape=jax.ShapeDtypeStruct((M, N), BF16),
        scratch_shapes=[pltpu.VMEM((M, N // n_split), F32)],
        compiler_params=_cparams(("arbitrary", "arbitrary")),
    )(a, *bs)


def _half_mask(half):
    lane = lax.broadcasted_iota(I32, (1, LANES), 1)
    return (lane >= half * HEAD_DIM) & (lane < half * HEAD_DIM + HEAD_DIM)


def _lane_tile(a, width):
    return jnp.concatenate([a] * (width // LANES), axis=1)


def _softplus_neg_abs(z):
    return jnp.log(1.0 + jnp.exp(-jnp.abs(z)))


def _sb_fwd(qkv, n_pair, shards):
    S = qkv.shape[0]
    B = SB_BLOCK
    nq = S // B
    us = _tri(B, lambda r, c: r > c)
    n_sh = len(shards)
    ex = _Exchange(n_sh, scatter=False)

    def body(*refs):
        q_ref, k_ref, v_ref, us_ref = refs[:4]
        sh_in = refs[4:4 + n_sh]
        o_ref, st_ref, js_ref = refs[4 + n_sh:7 + n_sh]
        sh_out = refs[7 + n_sh:7 + 2 * n_sh]
        acc_ref, r_ref = refs[7 + 2 * n_sh:9 + 2 * n_sh]
        sems = refs[9 + 2 * n_sh:]
        p, i = pl.program_id(0), pl.program_id(1)

        @pl.when((p == 0) & (i == 0))
        def _():
            ex.start(sh_in, sh_out, sems)

        hms = [_half_mask(h) for h in range(2)]
        qv = q_ref[...]
        qss = [jnp.where(hm, qv, jnp.zeros_like(qv)) * SCALE for hm in hms]
        row = lax.broadcasted_iota(I32, (B, B), 0)
        col = lax.broadcasted_iota(I32, (B, B), 1)
        tri = col < row
        acc_ref[...] = jnp.zeros_like(acc_ref)
        r_ref[...] = jnp.zeros_like(r_ref)

        def block(j, diag):
            off = pl.multiple_of(j * B, B)
            kj = k_ref[pl.ds(off, B), :]
            vj = v_ref[pl.ds(off, B), :]
            zs = [_dot_nt(qss[h], kj) for h in range(2)]
            sps = [_softplus_neg_abs(z) for z in zs]
            bs = [jnp.minimum(-z, 0.0) - sp for z, sp in zip(zs, sps)]
            if diag:
                bs = [jnp.where(tri, b, 0.0) for b in bs]
            lexcs = [_dot_acc(b, us_ref[...]) for b in bs]
            ws = [jnp.exp(jnp.minimum(z, 0.0) - sp + (_lane_tile(r_ref[h], B) + lexc))
                  for h, (z, sp, lexc) in enumerate(zip(zs, sps, lexcs))]
            if diag:
                ws = [jnp.where(tri, w, 0.0) for w in ws]
            for h in range(2):
                acc_ref[h] += _dot(ws[h].astype(BF16), vj)
                r_ref[h] += jnp.broadcast_to(lexcs[h][:, 0:1] + bs[h][:, 0:1], (B, LANES))

        def live():
            return (jnp.max(r_ref[...]) > -EXP_ZERO).astype(I32)

        def first_two():
            has_prev = (i > 0).astype(F32)
            off_d = pl.multiple_of(i * B, B)
            off_o = pl.multiple_of(jnp.maximum(i - 1, 0) * B, B)
            k_d, v_d = k_ref[pl.ds(off_d, B), :], v_ref[pl.ds(off_d, B), :]
            k_o, v_o = k_ref[pl.ds(off_o, B), :], v_ref[pl.ds(off_o, B), :]
            hs = range(2)
            z_d = [_dot_nt(qss[h], k_d) for h in hs]
            z_o = [_dot_nt(qss[h], k_o) for h in hs]
            sp_d = [_softplus_neg_abs(z) for z in z_d]
            sp_o = [_softplus_neg_abs(z) for z in z_o]
            b_d = [jnp.where(tri, jnp.minimum(-z, 0.0) - sp, 0.0) for z, sp in zip(z_d, sp_d)]
            b_o = [(jnp.minimum(-z, 0.0) - sp) * has_prev for z, sp in zip(z_o, sp_o)]
            l_d = [_dot_acc(b, us_ref[...]) for b in b_d]
            l_o = [_dot_acc(b, us_ref[...]) for b in b_o]
            r_d = [jnp.broadcast_to(l[:, 0:1] + b[:, 0:1], (B, LANES)) for l, b in zip(l_d, b_d)]
            w_d = [jnp.where(tri, jnp.exp(jnp.minimum(z, 0.0) - sp + l), 0.0) for z, sp, l in zip(z_d, sp_d, l_d)]
            w_o = [jnp.exp(jnp.minimum(z, 0.0) - sp + (_lane_tile(r, B) + l)) * has_prev
                   for z, sp, l, r in zip(z_o, sp_o, l_o, r_d)]
            for h in hs:
                acc_ref[h] = _dot(w_d[h].astype(BF16), v_d) + _dot(w_o[h].astype(BF16), v_o)
                r_ref[h] = r_d[h] + jnp.broadcast_to(l_o[h][:, 0:1] + b_o[h][:, 0:1], (B, LANES))

        first_two()

        def step(carry):
            j, _ = carry
            block(j, False)
            return j - 1, live()

        j_end, _ = lax.while_loop(lambda c: (c[0] >= 0) & (c[1] > 0), step, (i - 2, live()))
        js = jnp.maximum(j_end + 1, 0)
        js_ref[2 * p, i] = js
        js_ref[2 * p + 1, i] = js
        o_ref[...] = jnp.where(hms[0], acc_ref[0], acc_ref[1])
        st_ref[...] = r_ref[...]

        @pl.when((p == n_pair - 1) & (i == nq - 1))
        def _():
            ex.finish(sh_in, sh_out, sems)

    any_spec = pl.BlockSpec(memory_space=pl.ANY)
    return pl.pallas_call(
        body, name="sb_attn_fwd", grid=(n_pair, nq),
        in_specs=[pl.BlockSpec((B, LANES), lambda p, i: (i, p)),
                  pl.BlockSpec((S, LANES), lambda p, i: (0, n_pair + p)),
                  pl.BlockSpec((S, LANES), lambda p, i: (0, 2 * n_pair + p)),
                  _full((B, B))] + [any_spec] * n_sh,
        out_specs=[pl.BlockSpec((B, LANES), lambda p, i: (i, p)),
                   pl.BlockSpec((2, B, LANES), lambda p, i: (p, i, 0)),
                   pl.BlockSpec(memory_space=pltpu.SMEM)] + [any_spec] * n_sh,
        out_shape=[jax.ShapeDtypeStruct((S, n_pair * LANES), F32),
                   jax.ShapeDtypeStruct((2 * n_pair, S, LANES), F32),
                   jax.ShapeDtypeStruct((2 * n_pair, nq), I32)] + ex.out_shapes(shards),
        scratch_shapes=[pltpu.VMEM((2, B, LANES), F32), pltpu.VMEM((2, B, LANES), F32)] + ex.sem_shapes(),
        compiler_params=_cparams(("arbitrary", "arbitrary")),
    )(qkv, qkv, qkv, us, *shards)


def _sb_bwd(qkv, do, st, js, n_pair):
    S = qkv.shape[0]
    B = SB_BLOCK
    nq = S // B
    us = _tri(B, lambda r, c: r > c)
    ti = _tri(B, lambda r, c: r <= c)

    def body(js_ref, q_ref, k_ref, v_ref, do_ref, st_ref, us_ref, ti_ref, dq_ref, dk_ref, dv_ref,
             dqa_ref, pr_ref, er_ref):
        p, i = pl.program_id(0), pl.program_id(1)

        @pl.when(i == 0)
        def _():
            dk_ref[...] = jnp.zeros_like(dk_ref)
            dv_ref[...] = jnp.zeros_like(dv_ref)

        hms = [_half_mask(h) for h in range(2)]
        qv = q_ref[...]
        dov = do_ref[...]
        qss = [jnp.where(hm, qv, jnp.zeros_like(qv)) * SCALE for hm in hms]
        dobs = [jnp.where(hm, dov, 0.0).astype(BF16) for hm in hms]
        row = lax.broadcasted_iota(I32, (B, B), 0)
        col = lax.broadcasted_iota(I32, (B, B), 1)
        tri = col < row
        dqa_ref[...] = jnp.zeros_like(dqa_ref)
        pr_ref[...] = jnp.zeros_like(pr_ref)
        er_ref[...] = jnp.zeros_like(er_ref)

        def block(j, diag):
            off = pl.multiple_of(j * B, B)
            kj = k_ref[pl.ds(off, B), :]
            vj = v_ref[pl.ds(off, B), :]
            hs = range(2)
            zs = [_dot_nt(qss[h], kj) for h in hs]
            dws = [_dot_nt(dobs[h], vj) for h in hs]
            sps = [_softplus_neg_abs(z) for z in zs]
            bs = [jnp.minimum(-z, 0.0) - sp for z, sp in zip(zs, sps)]
            if diag:
                bs = [jnp.where(tri, b, 0.0) for b in bs]
            lexcs = [_dot_acc(b, us_ref[...]) for b in bs]
            ws = []
            for h in hs:
                pr_new = pr_ref[h] + jnp.broadcast_to(lexcs[h][:, 0:1] + bs[h][:, 0:1], (B, LANES))
                pr_ref[h] = pr_new
                w = jnp.exp(jnp.minimum(zs[h], 0.0) - sps[h] + (_lane_tile(st_ref[h] - pr_new, B) + lexcs[h]))
                ws.append(jnp.where(tri, w, 0.0) if diag else w)
            es = [dw * w for dw, w in zip(dws, ws)]
            eincs = [_dot_acc(e, ti_ref[...]) for e in es]
            dzbs = []
            for h in hs:
                er = er_ref[h]
                big_e = _lane_tile(er, B) + (eincs[h] - es[h])
                er_ref[h] = er + jnp.broadcast_to(eincs[h][:, B - 1:B], (B, LANES))
                eb = jnp.exp(bs[h])
                dzbs.append((es[h] * eb - big_e * (1.0 - eb)).astype(BF16))
            for h in hs:
                dqa_ref[h] += _dot(dzbs[h], kj)
            dk_ref[pl.ds(off, B), :] += _dot_tn(dzbs[0], qss[0]) + _dot_tn(dzbs[1], qss[1])
            dv_ref[pl.ds(off, B), :] += (_dot_tn(ws[0].astype(BF16), dobs[0])
                                         + _dot_tn(ws[1].astype(BF16), dobs[1]))

        def step(j, carry):
            block(j, False)
            return carry

        def last_two():
            has_prev = (i > 0).astype(F32)
            off_d = pl.multiple_of(i * B, B)
            off_o = pl.multiple_of(jnp.maximum(i - 1, 0) * B, B)
            k_d, v_d = k_ref[pl.ds(off_d, B), :], v_ref[pl.ds(off_d, B), :]
            k_o, v_o = k_ref[pl.ds(off_o, B), :], v_ref[pl.ds(off_o, B), :]
            hs = range(2)
            z_o = [_dot_nt(qss[h], k_o) for h in hs]
            z_d = [_dot_nt(qss[h], k_d) for h in hs]
            dw_o = [_dot_nt(dobs[h], v_o) for h in hs]
            dw_d = [_dot_nt(dobs[h], v_d) for h in hs]
            sp_o = [_softplus_neg_abs(z) for z in z_o]
            sp_d = [_softplus_neg_abs(z) for z in z_d]
            b_o = [(jnp.minimum(-z, 0.0) - sp) * has_prev for z, sp in zip(z_o, sp_o)]
            b_d = [jnp.where(tri, jnp.minimum(-z, 0.0) - sp, 0.0) for z, sp in zip(z_d, sp_d)]
            l_o = [_dot_acc(b, us_ref[...]) for b in b_o]
            l_d = [_dot_acc(b, us_ref[...]) for b in b_d]
            w_o, w_d = [], []
            for h in hs:
                pr1 = pr_ref[h] + jnp.broadcast_to(l_o[h][:, 0:1] + b_o[h][:, 0:1], (B, LANES))
                pr2 = pr1 + jnp.broadcast_to(l_d[h][:, 0:1] + b_d[h][:, 0:1], (B, LANES))
                st = st_ref[h]
                w_o.append(jnp.exp(jnp.minimum(z_o[h], 0.0) - sp_o[h] + (_lane_tile(st - pr1, B) + l_o[h])) * has_prev)
                w_d.append(jnp.where(tri, jnp.exp(jnp.minimum(z_d[h], 0.0) - sp_d[h]
                                                  + (_lane_tile(st - pr2, B) + l_d[h])), 0.0))
            e_o = [dw * w for dw, w in zip(dw_o, w_o)]
            e_d = [dw * w for dw, w in zip(dw_d, w_d)]
            ei_o = [_dot_acc(e, ti_ref[...]) for e in e_o]
            ei_d = [_dot_acc(e, ti_ref[...]) for e in e_d]
            dz_o, dz_d = [], []
            for h in hs:
                er = er_ref[h]
                big_o = _lane_tile(er, B) + (ei_o[h] - e_o[h])
                er1 = er + jnp.broadcast_to(ei_o[h][:, B - 1:B], (B, LANES))
                big_d = _lane_tile(er1, B) + (ei_d[h] - e_d[h])
                eb_o, eb_d = jnp.exp(b_o[h]), jnp.exp(b_d[h])
                dz_o.append((e_o[h] * eb_o - big_o * (1.0 - eb_o)).astype(BF16))
                dz_d.append((e_d[h] * eb_d - big_d * (1.0 - eb_d)).astype(BF16))
            for h in hs:
                dqa_ref[h] += _dot(dz_o[h], k_o) + _dot(dz_d[h], k_d)
            dk_ref[pl.ds(off_o, B), :] += _dot_tn(dz_o[0], qss[0]) + _dot_tn(dz_o[1], qss[1])
            dv_ref[pl.ds(off_o, B), :] += (_dot_tn(w_o[0].astype(BF16), dobs[0])
                                           + _dot_tn(w_o[1].astype(BF16), dobs[1]))
            dk_ref[pl.ds(off_d, B), :] += _dot_tn(dz_d[0], qss[0]) + _dot_tn(dz_d[1], qss[1])
            dv_ref[pl.ds(off_d, B), :] += (_dot_tn(w_d[0].astype(BF16), dobs[0])
                                           + _dot_tn(w_d[1].astype(BF16), dobs[1]))

        lax.fori_loop(js_ref[2 * p, i], i - 1, step, 0)
        last_two()
        dq_ref[...] = jnp.where(hms[0], dqa_ref[0], dqa_ref[1]) * SCALE

    W = n_pair * LANES
    return pl.pallas_call(
        body, name="sb_attn_bwd",
        grid_spec=pltpu.PrefetchScalarGridSpec(
            num_scalar_prefetch=1, grid=(n_pair, nq),
            in_specs=[pl.BlockSpec((B, LANES), lambda p, i, js: (i, p)),
                      pl.BlockSpec((S, LANES), lambda p, i, js: (0, n_pair + p)),
                      pl.BlockSpec((S, LANES), lambda p, i, js: (0, 2 * n_pair + p)),
                      pl.BlockSpec((B, LANES), lambda p, i, js: (i, p)),
                      pl.BlockSpec((2, B, LANES), lambda p, i, js: (p, i, 0)),
                      pl.BlockSpec((B, B), lambda p, i, js: (0, 0)),
                      pl.BlockSpec((B, B), lambda p, i, js: (0, 0))],
            out_specs=[pl.BlockSpec((B, LANES), lambda p, i, js: (i, p)),
                       pl.BlockSpec((S, LANES), lambda p, i, js: (0, p)),
                       pl.BlockSpec((S, LANES), lambda p, i, js: (0, p))],
            scratch_shapes=[pltpu.VMEM((2, B, LANES), F32), pltpu.VMEM((2, B, LANES), F32),
                            pltpu.VMEM((2, B, LANES), F32)]),
        out_shape=[jax.ShapeDtypeStruct((S, W), F32)] * 3,
        compiler_params=_cparams(("arbitrary", "arbitrary")),
    )(js, qkv, qkv, qkv, do, st, us, ti)


def _head_column(blk, head):
    lane = lax.broadcasted_iota(I32, (1, LANES), 1)
    return jnp.sum(jnp.where(lane == head, blk, 0.0), axis=1, keepdims=True)


def _fox_fwd(qkv, c, c_rows, kmax, n_pair):
    S = qkv.shape[0]
    BQ, BK = FOX_BQ, FOX_BK
    R = BQ // BK
    nq = S // BQ

    def body(q_ref, k_ref, v_ref, c_ref, cr_ref, km_ref, o_ref, st_ref, js_ref, acc_ref, m_ref, cb_ref, qkb_ref):
        p, i, half = pl.program_id(0), pl.program_id(1), pl.program_id(2)
        hm = _half_mask(half)
        qv = q_ref[...]
        qs = jnp.where(hm, qv, jnp.zeros_like(qv)) * SCALE
        ccol = _head_column(c_ref[...], 2 * p + half)
        cb_ref[...] = jnp.broadcast_to(ccol, (BQ, BK))
        qf = qs.astype(F32)
        qkb_ref[...] = jnp.broadcast_to(
            jnp.sqrt(jnp.sum(qf * qf, axis=1, keepdims=True)) * NORM_SLACK
            * _head_column(km_ref[...], 2 * p + half) + ccol, (BQ, LANES))
        row = lax.broadcasted_iota(I32, (BQ, BK), 0)
        col = lax.broadcasted_iota(I32, (BQ, BK), 1)
        acc_ref[...] = jnp.zeros_like(acc_ref)
        m_ref[...] = jnp.full_like(m_ref, NEG_BIG)

        def blocks(j_top, diag):
            ss, v1s, keeps = [], [], []
            for d in range(R):
                j = j_top - d
                off = pl.multiple_of(j * BK, BK)
                vj = v_ref[pl.ds(off, BK), :]
                v1s.append(jnp.where(hm, vj, jnp.ones_like(vj)))
                s = _dot_nt(qs, k_ref[pl.ds(off, BK), :]) + (cb_ref[...] - cr_ref[0, pl.ds(j, 1), :])
                if diag:
                    keeps.append(col + (R - 1 - d) * BK <= row)
                    s = jnp.where(keeps[-1], s, NEG_BIG)
                ss.append(s)
            m_old = m_ref[...]
            s_max = jnp.max(functools.reduce(jnp.maximum, ss), axis=1, keepdims=True)
            m_new = jnp.maximum(m_old, jnp.broadcast_to(s_max, (BQ, LANES)))
            m_wide = _lane_tile(m_new, BK)
            pvs = [jnp.exp(s - m_wide) for s in ss]
            if diag:
                pvs = [jnp.where(keep, pv, 0.0) for keep, pv in zip(keeps, pvs)]
            new = _dot(pvs[0].astype(BF16), v1s[0])
            for pv, v1 in zip(pvs[1:], v1s[1:]):
                new = new + _dot(pv.astype(BF16), v1)
            acc_ref[...] = jnp.exp(m_old - m_new) * acc_ref[...] + new
            m_ref[...] = m_new

        def live(j):
            c_end = cr_ref[0, pl.ds(jnp.maximum(j, 0), 1), :][:, BK - 1:BK]
            return (jnp.max(qkb_ref[...] - c_end - m_ref[...]) > -EXP_ZERO).astype(I32)

        blocks(R * i + R - 1, True)

        def step(carry):
            j, _ = carry
            go_on = live(j - R)
            blocks(j, False)
            return j - R, go_on

        j_end, _ = lax.while_loop(lambda cr: (cr[0] >= 0) & (cr[1] > 0), step, (R * i - 1, live(R * i - 1)))
        js_ref[2 * p + half, i] = j_end + 1
        acc = acc_ref[...]
        denom = jnp.where(hm, pltpu.roll(acc, HEAD_DIM, 1), acc)
        res = jnp.where(hm, acc / denom, 0.0)

        @pl.when(half == 0)
        def _():
            o_ref[...] = res

        @pl.when(half == 1)
        def _():
            o_ref[...] += res

        st_ref[0] = m_ref[...] + jnp.log(denom)

    return pl.pallas_call(
        body, name="fox_attn_fwd", grid=(n_pair, nq, 2),
        in_specs=[pl.BlockSpec((BQ, LANES), lambda p, i, h: (i, 3 * n_pair + p)),
                  pl.BlockSpec((S, LANES), lambda p, i, h: (0, 4 * n_pair + p)),
                  pl.BlockSpec((S, LANES), lambda p, i, h: (0, 5 * n_pair + p)),
                  pl.BlockSpec((BQ, LANES), lambda p, i, h: (i, 0)),
                  pl.BlockSpec((1, S // BK, BK), lambda p, i, h: (2 * p + h, 0, 0)),
                  _full((1, LANES))],
        out_specs=[pl.BlockSpec((BQ, LANES), lambda p, i, h: (i, p)),
                   pl.BlockSpec((1, BQ, LANES), lambda p, i, h: (2 * p + h, i, 0)),
                   pl.BlockSpec(memory_space=pltpu.SMEM)],
        out_shape=[jax.ShapeDtypeStruct((S, n_pair * LANES), F32),
                   jax.ShapeDtypeStruct((2 * n_pair, S, LANES), F32),
                   jax.ShapeDtypeStruct((2 * n_pair, nq), I32)],
        scratch_shapes=[pltpu.VMEM((BQ, LANES), F32), pltpu.VMEM((BQ, LANES), F32), pltpu.VMEM((BQ, BK), F32),
                        pltpu.VMEM((BQ, LANES), F32)],
        compiler_params=_cparams(("arbitrary", "arbitrary", "arbitrary")),
    )(qkv, qkv, qkv, c, c_rows, kmax)


def _fox_bwd(qkv, do, o, st, c, c_rows, js, n_pair):
    S = qkv.shape[0]
    B = FOX_BLOCK
    nq = S // B

    def body(js_ref, q_ref, k_ref, v_ref, do_ref, o_ref, st_ref, c_ref, cr_ref, dq_ref, dk_ref, dv_ref,
             dc_ref, dqa_ref, rs_ref, cb_ref, db_ref):
        p, i, half = pl.program_id(0), pl.program_id(1), pl.program_id(2)

        @pl.when((i == 0) & (half == 0))
        def _():
            dk_ref[...] = jnp.zeros_like(dk_ref)
            dv_ref[...] = jnp.zeros_like(dv_ref)
            dc_ref[...] = jnp.zeros_like(dc_ref)

        hm = _half_mask(half)
        qv = q_ref[...]
        qs = jnp.where(hm, qv, jnp.zeros_like(qv)) * SCALE
        dov = jnp.where(hm, do_ref[...], 0.0)
        dob = dov.astype(BF16)
        cb_ref[...] = jnp.broadcast_to(_head_column(c_ref[...], 2 * p + half), (B, LANES)) - st_ref[0]
        db_ref[...] = jnp.broadcast_to(jnp.sum(dov * o_ref[...], axis=1, keepdims=True), (B, LANES))
        row = lax.broadcasted_iota(I32, (B, B), 0)
        col = lax.broadcasted_iota(I32, (B, B), 1)
        dqa_ref[...] = jnp.zeros_like(dqa_ref)
        rs_ref[...] = jnp.zeros_like(rs_ref)

        def block(j, diag):
            off = pl.multiple_of(j * B, B)
            kj = k_ref[pl.ds(off, B), :]
            vj = v_ref[pl.ds(off, B), :]
            pv = jnp.exp(_dot_nt(qs, kj) + (_lane_tile(cb_ref[...], B) - cr_ref[0, pl.ds(j, 1), :]))
            if diag:
                pv = jnp.where(col <= row, pv, 0.0)
            ds = pv * (_dot_nt(dob, vj) - _lane_tile(db_ref[...], B))
            dsb = ds.astype(BF16)
            dqa_ref[...] += _dot(dsb, kj)
            dk_ref[pl.ds(off, B), :] += _dot_tn(dsb, qs)
            dv_ref[pl.ds(off, B), :] += _dot_tn(pv.astype(BF16), dob)
            dc_ref[0, half, pl.ds(j, 1), :] -= jnp.sum(ds, axis=0, keepdims=True)
            rs_ref[...] += jnp.sum(ds, axis=1, keepdims=True)

        def step(j, carry):
            block(j, False)
            return carry

        lax.fori_loop(js_ref[2 * p + half, i], i, step, 0)
        block(i, True)
        res = jnp.where(hm, dqa_ref[...] * SCALE, 0.0)

        @pl.when(half == 0)
        def _():
            dq_ref[...] = res

        @pl.when(half == 1)
        def _():
            dq_ref[...] += res

        dc_ref[0, half, pl.ds(i, 1), :] += jnp.transpose(jnp.broadcast_to(rs_ref[...], (B, LANES)))[0:1, :]

    W = n_pair * LANES
    return pl.pallas_call(
        body, name="fox_attn_bwd",
        grid_spec=pltpu.PrefetchScalarGridSpec(
            num_scalar_prefetch=1, grid=(n_pair, nq, 2),
            in_specs=[pl.BlockSpec((B, LANES), lambda p, i, h, js: (i, 3 * n_pair + p)),
                      pl.BlockSpec((S, LANES), lambda p, i, h, js: (0, 4 * n_pair + p)),
                      pl.BlockSpec((S, LANES), lambda p, i, h, js: (0, 5 * n_pair + p)),
                      pl.BlockSpec((B, LANES), lambda p, i, h, js: (i, p)),
                      pl.BlockSpec((B, LANES), lambda p, i, h, js: (i, p)),
                      pl.BlockSpec((1, B, LANES), lambda p, i, h, js: (2 * p + h, i, 0)),
                      pl.BlockSpec((B, LANES), lambda p, i, h, js: (i, 0)),
                      pl.BlockSpec((1, nq, B), lambda p, i, h, js: (2 * p + h, 0, 0))],
            out_specs=[pl.BlockSpec((B, LANES), lambda p, i, h, js: (i, p)),
                       pl.BlockSpec((S, LANES), lambda p, i, h, js: (0, p)),
                       pl.BlockSpec((S, LANES), lambda p, i, h, js: (0, p)),
                       pl.BlockSpec((1, 2, nq, B), lambda p, i, h, js: (p, 0, 0, 0))],
            scratch_shapes=[pltpu.VMEM((B, LANES), F32), pltpu.VMEM((B, 1), F32), pltpu.VMEM((B, LANES), F32),
                            pltpu.VMEM((B, LANES), F32)]),
        out_shape=[jax.ShapeDtypeStruct((S, W), F32)] * 3 + [jax.ShapeDtypeStruct((n_pair, 2, nq, B), F32)],
        compiler_params=_cparams(("arbitrary", "arbitrary", "arbitrary")),
    )(js, qkv, qkv, qkv, do, o, st, c, c_rows)


def _adam(w, g, m, v):
    m = ADAM_B1 * m + (1.0 - ADAM_B1) * g
    v = ADAM_B2 * v + (1.0 - ADAM_B2) * (g * g)
    m_hat = m / (1.0 - ADAM_B1 ** ADAM_STEP)
    v_hat = v / (1.0 - ADAM_B2 ** ADAM_STEP)
    delta = -ADAM_LR * (m_hat / (jnp.sqrt(v_hat) + ADAM_EPS) + ADAM_WD * w)
    return delta, m, v


def _reduce_adam(landing, w, m, v, name, extra=None):
    R, C = w.shape
    TR = next(t for t in (256, 128, R) if R % t == 0)
    more = [] if extra is None else [extra]

    def body(*refs):
        l_ref, w_ref, m_ref, v_ref = refs[:4]
        g_ref, d_ref, nm_ref, nv_ref = refs[4 + len(more):]
        g = l_ref[0].astype(F32)
        for s in range(1, N_DEV):
            g = g + l_ref[s].astype(F32)
        if more:
            g = g + refs[4][...]
        d, nm, nv = _adam(w_ref[...], g, m_ref[...], v_ref[...])
        g_ref[...] = g
        d_ref[...] = d
        nm_ref[...] = nm
        nv_ref[...] = nv

    blk = pl.BlockSpec((TR, C), lambda i: (i, 0))
    return pl.pallas_call(
        body, name=name, grid=(R // TR,),
        in_specs=[pl.BlockSpec((N_DEV, TR, C), lambda i: (0, i, 0)), blk, blk, blk] + [blk] * len(more),
        out_specs=[blk] * 4,
        out_shape=[jax.ShapeDtypeStruct((R, C), F32)] * 4,
        compiler_params=_cparams(("arbitrary",)),
    )(landing, w, m, v, *more)


def _reduce_adam_small(landing, w, m, v):
    R, C = w.shape

    def body(l_ref, w_ref, m_ref, v_ref, g_ref, d_ref, nm_ref, nv_ref, loss_ref):
        g = l_ref[0]
        for s in range(1, N_DEV):
            g = g + l_ref[s]
        d, nm, nv = _adam(w_ref[...], g, m_ref[...], v_ref[...])
        g_ref[...] = g
        d_ref[...] = d
        nm_ref[...] = nm
        nv_ref[...] = nv
        loss_ref[...] = jnp.broadcast_to(0.5 * jnp.sum(g[7:8, :], axis=1, keepdims=True), (1, LANES))

    return pl.pallas_call(
        body, name="reduce_adam_small",
        out_shape=[jax.ShapeDtypeStruct((R, C), F32)] * 4 + [jax.ShapeDtypeStruct((1, LANES), F32)],
    )(landing, w, m, v)


def _pad_lanes(a, width):
    return jnp.pad(a, ((0, 0), (0, width - a.shape[1])))


def _pack_small(D, n_fox, g_cat, l1g, l1b, l2g, l2b, bf, last, tail):
    return jnp.concatenate([g_cat, l1g, l1b, l2g, l2b, _pad_lanes(bf, D), jnp.zeros((1, D), F32), last, tail],
                           axis=0)


def kernel(x, w_in, b_f, g_sb, g_fox, w_out, ln1_g, ln1_b, ln2_g, ln2_b, w_gate_up, w_down, loss_target, m_w_in, m_b_f, m_g_sb, m_g_fox, m_w_out, m_ln1_g, m_ln1_b, m_ln2_g, m_ln2_b, m_w_gate_up, m_w_down, v_w_in, v_b_f, v_g_sb, v_g_fox, v_w_out, v_ln1_g, v_ln1_b, v_ln2_g, v_ln2_b, v_w_gate_up, v_w_down):
    x2, tgt = x[0], loss_target[0]
    S, D = x2.shape
    W = D // 2
    n_pair = W // LANES
    n_fox = W // HEAD_DIM
    F = w_down.shape[1] * N_DEV

    g_in = _all_gather_two_level(w_in[0].astype(BF16), "w_in_all_gather")
    w_in_full = g_in.transpose(1, 0, 2).reshape(D, -1)
    w_qkv = w_in_full[:, :6 * W]
    w_f = _pad_lanes(w_in_full[:, 6 * W:], LANES)
    gmat = _tri(D, lambda r, c: (r // HEAD_DIM) == (c // HEAD_DIM))
    g_cat = jnp.concatenate([g_sb, g_fox], axis=1)

    qkv, u, c, ksq = _proj_fwd(x2, w_qkv, w_f, _pad_lanes(b_f, LANES), n_fox)
    c_t = c[:, :n_fox].T
    c_rows = c_t.reshape(n_fox, S // FOX_BLOCK, FOX_BLOCK)
    kmax = jnp.sqrt(ksq[0:1]) * NORM_SLACK
    o_sb, st_sb, js_sb, g_out, g_gu, g_dn = _sb_fwd(
        qkv, n_pair, [w_out[0].astype(BF16), w_gate_up[0].astype(BF16), w_down[0].astype(BF16)])
    w_o = g_out.reshape(D, D)
    w_gu = g_gu.transpose(1, 0, 2).reshape(D, 2 * F)
    w_dn = g_dn.reshape(F, D)
    o_fx, st_fx, js_fx = _fox_fwd(qkv, c, c_t.reshape(n_fox, S // FOX_BK, FOX_BK), kmax, n_pair)
    js_fx = js_fx // (FOX_BLOCK // FOX_BK)
    h1, xh1, rs1, on_b, rr = _post_attn_fwd(o_sb, o_fx, x2, g_cat, gmat, w_o, ln1_g, ln1_b)
    gu, act_b, dyp, sm2 = _mlp_fwd(h1, tgt, w_gu, w_dn, ln2_g, ln2_b)

    dgu_b, dh1 = _mlp_bwd(gu, dyp, w_gu.T, w_dn.T)
    gw_gu = _matmul_tn(h1, [dgu_b], "grad_w_gate_up", n_split=2)
    gw_dn = _matmul_tn(act_b, [dyp], "grad_w_down")
    dxa, dmix_b, do_sb, do_fx, sm1, l_gu, l_dn = _post_attn_bwd(
        dh1, xh1, rs1, ln1_g, o_sb, o_fx, rr, g_cat, gmat, w_o.T,
        [gw_gu.reshape(D, N_DEV, -1).transpose(1, 0, 2), gw_dn.reshape(N_DEV, F // N_DEV, D)])
    dq_sb, dk_sb, dv_sb = _sb_bwd(qkv, do_sb, st_sb, js_sb, n_pair)
    dq_fx, dk_fx, dv_fx, dcr = _fox_bwd(qkv, do_fx, o_fx, st_fx, c, c_rows, js_fx, n_pair)
    dc = _pad_lanes(dcr.reshape(n_fox, S).T, LANES)
    pieces = [dq_sb, dk_sb, dv_sb, dq_fx, dk_fx, dv_fx]
    gw_qkv = _matmul_tn(x2, pieces, "grad_w_qkv", pad_cols=LANES)[:, :6 * W + n_fox]
    gw_out = _matmul_tn(on_b, [dmix_b], "grad_w_out")
    dx, gw_f, sm0, l_in, l_out = _proj_bwd(
        dxa, x2, pieces, dc, u, w_qkv.T, w_f.T, n_fox,
        [gw_qkv.reshape(D, N_DEV, -1).transpose(1, 0, 2), gw_out.reshape(N_DEV, D // N_DEV, D)])
    gw_f = gw_f[:, :n_fox]

    small = _pack_small(D, n_fox, sm1[2:3], sm1[0:1], sm1[1:2], sm2[0:1], sm2[1:2], sm0[0:1, :n_fox],
                        sm2[2:3] * (1.0 / D), gw_f.T)
    (l_small,) = _exchange_call([jnp.broadcast_to(small[None], (N_DEV,) + small.shape)], True, "small_exchange")

    zero = jnp.zeros((1, D), F32)
    pack = lambda gc, a, b_, c_, d_, bf: _pack_small(D, n_fox, gc, a, b_, c_, d_, bf, zero,
                                                     jnp.zeros((n_fox, D), F32))
    r_small = _reduce_adam_small(
        l_small,
        pack(g_cat, ln1_g, ln1_b, ln2_g, ln2_b, b_f),
        pack(jnp.concatenate([m_g_sb, m_g_fox], axis=1), m_ln1_g, m_ln1_b, m_ln2_g, m_ln2_b, m_b_f),
        pack(jnp.concatenate([v_g_sb, v_g_fox], axis=1), v_ln1_g, v_ln1_b, v_ln2_g, v_ln2_b, v_b_f))
    loss = r_small[4][0, 0]
    cols = w_in.shape[2]
    gf_cols = jnp.pad(r_small[0][8:8 + n_fox].T, ((0, 0), (cols - n_fox, 0)))
    extra = jnp.where(_my_index() == N_DEV - 1, gf_cols, 0.0)
    r_in = _reduce_adam(l_in, w_in[0], m_w_in[0], v_w_in[0], "reduce_adam_w_in", extra)
    r_out = _reduce_adam(l_out, w_out[0], m_w_out[0], v_w_out[0], "reduce_adam_w_out")
    r_gu = _reduce_adam(l_gu, w_gate_up[0], m_w_gate_up[0], v_w_gate_up[0], "reduce_adam_w_gate_up")
    r_dn = _reduce_adam(l_dn, w_down[0], m_w_down[0], v_w_down[0], "reduce_adam_w_down")

    def unpack(kind):
        big = [r_in[kind][None], None, None, None, r_out[kind][None], None, None, None, None,
               r_gu[kind][None], r_dn[kind][None]]
        s = r_small[kind]
        big[1] = s[5:6, :n_fox]
        big[2] = s[0:1, :W]
        big[3] = s[0:1, W:]
        big[5], big[6], big[7], big[8] = s[1:2], s[2:3], s[3:4], s[4:5]
        return big

    return (loss, dx[None], *unpack(0), *unpack(1), *unpack(2), *unpack(3))
```

```python
import functools

import jax
import jax.numpy as jnp
from jax import lax
from jax.experimental import pallas as pl
from jax.experimental.pallas import tpu as pltpu

F32 = jnp.float32
BF16 = jnp.bfloat16
I32 = jnp.int32

N_DEV = 8
HEAD_DIM = 64
LANES = 128
SCALE = HEAD_DIM ** -0.5
ALPHA = 2.0 ** 0.25
LN_EPS = 1e-5
RMS_EPS = 1e-6
ADAM_LR, ADAM_B1, ADAM_B2, ADAM_EPS, ADAM_WD, ADAM_STEP = 0.001, 0.9, 0.999, 1e-08, 0.01, 10
NEG_BIG = -1e30
NORM_SLACK = 1.01
EXP_ZERO = 88.5
VMEM_LIMIT = 60 * 1024 * 1024
ROW_TILE = 256
SB_BLOCK = 256
FOX_BLOCK = 512
FOX_BQ, FOX_BK = 512, 256
assert FOX_BQ == FOX_BLOCK and FOX_BLOCK % FOX_BK == 0
MESH = pl.DeviceIdType.MESH


def _cparams(sem):
    return pltpu.CompilerParams(dimension_semantics=sem, vmem_limit_bytes=VMEM_LIMIT)


def _dot(a, b):
    return jnp.dot(a, b, preferred_element_type=F32)


def _dot_nt(a, b):
    return lax.dot_general(a, b, (((1,), (1,)), ((), ())), preferred_element_type=F32)


def _dot_tn(a, b):
    return lax.dot_general(a, b, (((0,), (0,)), ((), ())), preferred_element_type=F32)


def _split2(a):
    hi = a.astype(BF16)
    lo = (a - hi.astype(F32)).astype(BF16)
    return hi, lo


def _split3(a):
    hi = a.astype(BF16)
    r1 = a - hi.astype(F32)
    mid = r1.astype(BF16)
    lo = (r1 - mid.astype(F32)).astype(BF16)
    return hi, mid, lo


def _dot_acc(a, m):
    hi, lo = _split2(a)
    return _dot(jnp.concatenate([hi, lo], axis=1), jnp.concatenate([m, m], axis=0))


def _tri(n, fn):
    r = lax.broadcasted_iota(I32, (n, n), 0)
    c = lax.broadcasted_iota(I32, (n, n), 1)
    return fn(r, c).astype(BF16)


def _full(shape):
    nd = len(shape)
    return pl.BlockSpec(shape, lambda *_: (0,) * nd)


def _peer(k):
    x, y, c = lax.axis_index("x"), lax.axis_index("y"), lax.axis_index("c")
    return (1 - x if k & 4 else x, 1 - y if k & 2 else y, 1 - c if k & 1 else c)


def _my_index():
    return 4 * lax.axis_index("x") + 2 * lax.axis_index("y") + lax.axis_index("c")


class _Exchange:
    def __init__(self, n, scatter):
        self.n, self.scatter = n, scatter

    def sem_shapes(self):
        return [pltpu.SemaphoreType.DMA(((N_DEV - 1) * self.n,)), pltpu.SemaphoreType.DMA(((N_DEV - 1) * self.n,)),
                pltpu.SemaphoreType.DMA((self.n,))]

    def out_shapes(self, arrays):
        if self.scatter:
            return [jax.ShapeDtypeStruct(s.shape, s.dtype) for s in arrays]
        return [jax.ShapeDtypeStruct((N_DEV,) + s.shape, s.dtype) for s in arrays]

    def _copies(self, ins, outs, sems, landing):
        send_sems, recv_sems, local_sems = sems
        me = _my_index()
        src = lambda a, d: ins[a].at[d] if self.scatter else ins[a]
        local = [pltpu.make_async_copy(src(a, me), outs[a].at[me], local_sems.at[a]) for a in range(self.n)]
        remote = [pltpu.make_async_remote_copy(
            src_ref=src(a, me ^ k), dst_ref=outs[a].at[me ^ k if landing else me],
            send_sem=send_sems.at[(k - 1) * self.n + a], recv_sem=recv_sems.at[(k - 1) * self.n + a],
            device_id=_peer(k), device_id_type=MESH) for k in range(1, N_DEV) for a in range(self.n)]
        return local, remote

    def start(self, ins, outs, sems):
        local, sent = self._copies(ins, outs, sems, landing=False)
        for cp in local + sent:
            cp.start()

    def finish(self, ins, outs, sems):
        local, landed = self._copies(ins, outs, sems, landing=True)
        for cp in landed:
            cp.wait_recv()
        for cp in landed:
            cp.wait_send()
        for cp in local:
            cp.wait()


def _all_gather_two_level(shard, name):
    def body(x_ref, out_ref, send_sems, recv_sems, local_sem):
        x, y, c = lax.axis_index("x"), lax.axis_index("y"), lax.axis_index("c")
        me, sibling = (x, y, c), (x, y, 1 - c)
        chips = [(1 - x, y), (x, 1 - y), (1 - x, 1 - y)]

        def slot(px, py, pc):
            return out_ref.at[4 * px + 2 * py + pc]

        def copy(k, block, to, src=None):
            return pltpu.make_async_remote_copy(
                src_ref=slot(*block) if src is None else src, dst_ref=slot(*block),
                send_sem=send_sems.at[k], recv_sem=recv_sems.at[k], device_id=to, device_id_type=MESH)

        mine = pltpu.make_async_copy(x_ref, slot(*me), local_sem)
        mine.start()
        first = [copy(0, me, sibling, src=x_ref)]
        first += [copy(1 + j, me, (*chip, c), src=x_ref) for j, chip in enumerate(chips)]
        for cp in first:
            cp.start()
        passed = [copy(4 + j, (*chip, c), sibling) for j, chip in enumerate(chips)]
        for j, chip in enumerate(chips):
            copy(1 + j, (*chip, c), me).wait_recv()
            passed[j].start()
        copy(0, sibling, me).wait_recv()
        for j, chip in enumerate(chips):
            copy(4 + j, (*chip, 1 - c), me).wait_recv()
        for cp in first + passed:
            cp.wait_send()
        mine.wait()

    any_spec = pl.BlockSpec(memory_space=pl.ANY)
    return pl.pallas_call(
        body, name=name, out_shape=jax.ShapeDtypeStruct((N_DEV,) + shard.shape, shard.dtype),
        in_specs=[any_spec], out_specs=any_spec,
        scratch_shapes=[pltpu.SemaphoreType.DMA((N_DEV - 1,)), pltpu.SemaphoreType.DMA((N_DEV - 1,)),
                        pltpu.SemaphoreType.DMA],
    )(shard)


def _exchange_call(arrays, scatter, name):
    n = len(arrays)
    ex = _Exchange(n, scatter)

    def body(*refs):
        ins, outs, sems = refs[:n], refs[n:2 * n], refs[2 * n:]
        ex.start(ins, outs, sems)
        ex.finish(ins, outs, sems)

    any_spec = pl.BlockSpec(memory_space=pl.ANY)
    return pl.pallas_call(
        body, name=name, out_shape=ex.out_shapes(arrays),
        in_specs=[any_spec] * n, out_specs=[any_spec] * n, scratch_shapes=ex.sem_shapes(),
    )(*arrays)


def _log_sigmoid(u):
    return jnp.minimum(u, 0.0) - jnp.log1p(jnp.exp(-jnp.abs(u)))


def _proj_fwd(x, w_qkv, w_f, bf_pad, n_fox):
    S, D = x.shape
    N = w_qkv.shape[1]
    W = D // 2
    TM = ROW_TILE
    tri = _tri(TM, lambda r, c: c <= r)
    r_ = lax.broadcasted_iota(I32, (W, LANES), 0)
    c_ = lax.broadcasted_iota(I32, (W, LANES), 1)
    head_of = (r_ // HEAD_DIM == c_).astype(BF16)

    def body(x_ref, wq_ref, wf_ref, bf_ref, tri_ref, ho_ref, qkv_ref, u_ref, c_ref, ksq_ref, run_ref):
        @pl.when(pl.program_id(0) == 0)
        def _():
            run_ref[...] = jnp.zeros_like(run_ref)
            ksq_ref[...] = jnp.zeros_like(ksq_ref)

        xb = x_ref[...].astype(BF16)
        for n0 in range(0, N, D):
            chunk = _dot(xb, wq_ref[:, n0:n0 + D]).astype(BF16)
            qkv_ref[:, n0:n0 + D] = chunk
            if n0 == 4 * W:
                kf = chunk[:, :W].astype(F32)
                ksq = jnp.max(_dot_acc(kf * kf, ho_ref[...]), axis=0, keepdims=True)
                ksq_ref[...] = jnp.maximum(ksq_ref[...], ksq)
        u = _dot(xb, wf_ref[...]) + bf_ref[...]
        lane = lax.broadcasted_iota(I32, u.shape, 1)
        logf = jnp.where(lane < n_fox, _log_sigmoid(u), 0.0)
        u_ref[...] = u
        hi, mid, lo = _split3(logf)
        t = tri_ref[...]
        cs = _dot(t, hi) + _dot(t, mid) + _dot(t, lo) + run_ref[...]
        c_ref[...] = cs
        run_ref[...] = cs[TM - 1:TM, :]

    return pl.pallas_call(
        body, name="proj_fwd", grid=(S // TM,),
        in_specs=[pl.BlockSpec((TM, D), lambda i: (i, 0)), _full(w_qkv.shape), _full(w_f.shape),
                  _full(bf_pad.shape), _full(tri.shape), _full(head_of.shape)],
        out_specs=[pl.BlockSpec((TM, N), lambda i: (i, 0)), pl.BlockSpec((TM, LANES), lambda i: (i, 0)),
                   pl.BlockSpec((TM, LANES), lambda i: (i, 0)), _full((8, LANES))],
        out_shape=[jax.ShapeDtypeStruct((S, N), BF16), jax.ShapeDtypeStruct((S, LANES), F32),
                   jax.ShapeDtypeStruct((S, LANES), F32), jax.ShapeDtypeStruct((8, LANES), F32)],
        scratch_shapes=[pltpu.VMEM((1, LANES), F32)],
        compiler_params=_cparams(("arbitrary",)),
    )(x, w_qkv, w_f, bf_pad, tri, head_of)


def _post_attn_fwd(o_sb, o_fx, x, g_cat, gmat, w_out, ln_g, ln_b):
    S, D = x.shape
    H = D // 2
    TM = ROW_TILE

    def body(osb_ref, ofx_ref, x_ref, g_ref, gm_ref, wo_ref, lg_ref, lb_ref,
             h1_ref, xh_ref, rs_ref, on_ref, rr_ref):
        o = jnp.concatenate([osb_ref[...], ofx_ref[...]], axis=1)
        ms = _dot_acc(o * o, gm_ref[...]) * (1.0 / HEAD_DIM)
        r = lax.rsqrt(ms + RMS_EPS)
        onb = (o * r * g_ref[...]).astype(BF16)
        hp = ALPHA * x_ref[...] + _dot(onb, wo_ref[...])
        mu = jnp.mean(hp, axis=-1, keepdims=True)
        d = hp - mu
        rstd = lax.rsqrt(jnp.mean(d * d, axis=-1, keepdims=True) + LN_EPS)
        xh = d * rstd
        h1_ref[...] = xh * lg_ref[...] + lb_ref[...]
        xh_ref[...] = xh
        rs_ref[...] = jnp.broadcast_to(rstd, (TM, LANES))
        on_ref[...] = onb
        rr_ref[...] = r

    row = lambda w: pl.BlockSpec((TM, w), lambda i: (i, 0))
    return pl.pallas_call(
        body, name="post_attn_fwd", grid=(S // TM,),
        in_specs=[row(H), row(H), row(D), _full((1, D)), _full((D, D)), _full((D, D)), _full((1, D)), _full((1, D))],
        out_specs=[row(D), row(D), row(LANES), row(D), row(D)],
        out_shape=[jax.ShapeDtypeStruct((S, D), F32), jax.ShapeDtypeStruct((S, D), F32),
                   jax.ShapeDtypeStruct((S, LANES), F32), jax.ShapeDtypeStruct((S, D), BF16),
                   jax.ShapeDtypeStruct((S, D), F32)],
        compiler_params=_cparams(("arbitrary",)),
    )(o_sb, o_fx, x, g_cat, gmat, w_out, ln_g, ln_b)


def _ln_bwd(dxh, xh, rstd):
    m1 = jnp.mean(dxh, axis=-1, keepdims=True)
    m2 = jnp.mean(dxh * xh, axis=-1, keepdims=True)
    return rstd * (dxh - m1 - xh * m2)


def _mlp_fwd(h1, target, w_gu, w_dn, ln_g, ln_b):
    S, D = h1.shape
    F = w_dn.shape[0]
    TM = ROW_TILE
    FC = F // 2

    def body(h1_ref, tg_ref, wgu_hbm, wdn_hbm, lg_ref, lb_ref, gu_ref, act_ref, dyp_ref, sm_ref, wgu, wdn):
        @pl.when(pl.program_id(0) == 0)
        def _():
            pltpu.sync_copy(wgu_hbm, wgu)
            pltpu.sync_copy(wdn_hbm, wdn)
            sm_ref[...] = jnp.zeros_like(sm_ref)

        h1v = h1_ref[...]
        hb = h1v.astype(BF16)
        ff = jnp.zeros((TM, D), F32)
        for c0 in range(0, F, FC):
            g = _dot(hb, wgu[:, c0:c0 + FC])
            u = _dot(hb, wgu[:, F + c0:F + c0 + FC])
            gu_ref[:, c0:c0 + FC] = g
            gu_ref[:, F + c0:F + c0 + FC] = u
            ab = ((g * jax.nn.sigmoid(g)) * u).astype(BF16)
            act_ref[:, c0:c0 + FC] = ab
            ff = ff + _dot(ab, wdn[c0:c0 + FC, :])
        yp = ALPHA * h1v + ff
        mu = jnp.mean(yp, axis=-1, keepdims=True)
        d = yp - mu
        rstd = lax.rsqrt(jnp.mean(d * d, axis=-1, keepdims=True) + LN_EPS)
        xh = d * rstd
        err = (xh * lg_ref[...] + lb_ref[...]) - tg_ref[...]
        dy = err * (1.0 / D)
        sm_ref[0:1, :] += jnp.sum(dy * xh, axis=0, keepdims=True)
        sm_ref[1:2, :] += jnp.sum(dy, axis=0, keepdims=True)
        sm_ref[2:3, :] += jnp.sum(err * err, axis=0, keepdims=True)
        dyp_ref[...] = _ln_bwd(dy * lg_ref[...], xh, rstd)

    row = lambda w: pl.BlockSpec((TM, w), lambda i: (i, 0))
    hbm = pl.BlockSpec(memory_space=pl.ANY)
    return pl.pallas_call(
        body, name="mlp_fwd", grid=(S // TM,),
        in_specs=[row(D), row(D), hbm, hbm, _full((1, D)), _full((1, D))],
        out_specs=[row(2 * F), row(F), row(D), _full((8, D))],
        out_shape=[jax.ShapeDtypeStruct((S, 2 * F), F32), jax.ShapeDtypeStruct((S, F), BF16),
                   jax.ShapeDtypeStruct((S, D), F32), jax.ShapeDtypeStruct((8, D), F32)],
        scratch_shapes=[pltpu.VMEM(w_gu.shape, BF16), pltpu.VMEM(w_dn.shape, BF16)],
        compiler_params=_cparams(("arbitrary",)),
    )(h1, target, w_gu, w_dn, ln_g, ln_b)


def _mlp_bwd(gu, dyp, w_guT, w_dnT):
    S, D = dyp.shape
    F = w_dnT.shape[1]
    TM = ROW_TILE
    FC = F // 2

    def body(gu_ref, dyp_ref, wguT_hbm, wdnT_hbm, dgu_ref, dh1_ref, wguT, wdnT):
        @pl.when(pl.program_id(0) == 0)
        def _():
            pltpu.sync_copy(wguT_hbm, wguT)
            pltpu.sync_copy(wdnT_hbm, wdnT)

        dypv = dyp_ref[...]
        db = dypv.astype(BF16)
        dh1 = ALPHA * dypv
        for c0 in range(0, F, FC):
            dact = _dot(db, wdnT[:, c0:c0 + FC])
            g = gu_ref[:, c0:c0 + FC]
            u = gu_ref[:, F + c0:F + c0 + FC]
            sg = jax.nn.sigmoid(g)
            dgb = (dact * u * (sg * (1.0 + g * (1.0 - sg)))).astype(BF16)
            dub = (dact * (g * sg)).astype(BF16)
            dgu_ref[:, c0:c0 + FC] = dgb
            dgu_ref[:, F + c0:F + c0 + FC] = dub
            dh1 = dh1 + _dot(dgb, wguT[c0:c0 + FC, :]) + _dot(dub, wguT[F + c0:F + c0 + FC, :])
        dh1_ref[...] = dh1

    row = lambda w: pl.BlockSpec((TM, w), lambda i: (i, 0))
    hbm = pl.BlockSpec(memory_space=pl.ANY)
    return pl.pallas_call(
        body, name="mlp_bwd", grid=(S // TM,),
        in_specs=[row(2 * F), row(D), hbm, hbm],
        out_specs=[row(2 * F), row(D)],
        out_shape=[jax.ShapeDtypeStruct((S, 2 * F), BF16), jax.ShapeDtypeStruct((S, D), F32)],
        scratch_shapes=[pltpu.VMEM(w_guT.shape, BF16), pltpu.VMEM(w_dnT.shape, BF16)],
        compiler_params=_cparams(("arbitrary",)),
    )(gu, dyp, w_guT, w_dnT)


def _post_attn_bwd(dh1, xh, rs, ln_g, o_sb, o_fx, rr, g_cat, gmat, w_outT, chunked):
    S, D = dh1.shape
    H = D // 2
    TM = ROW_TILE
    nT = S // TM
    n_ch = len(chunked)
    ex = _Exchange(n_ch, scatter=True)

    def body(*refs):
        dh1_ref, xh_ref, rs_ref, lg_ref, osb_ref, ofx_ref, rr_ref, g_ref, gm_ref, woT_ref = refs[:10]
        ch_in = refs[10:10 + n_ch]
        dxa_ref, dmix_ref, dosb_ref, dofx_ref, sm_ref = refs[10 + n_ch:15 + n_ch]
        ch_out = refs[15 + n_ch:15 + 2 * n_ch]
        sems = refs[15 + 2 * n_ch:]

        @pl.when(pl.program_id(0) == 0)
        def _():
            ex.start(ch_in, ch_out, sems)
            sm_ref[...] = jnp.zeros_like(sm_ref)

        dh = dh1_ref[...]
        xhv = xh_ref[...]
        dhp = _ln_bwd(dh * lg_ref[...], xhv, rs_ref[:, 0:1])
        dxa_ref[...] = ALPHA * dhp
        dmb = dhp.astype(BF16)
        dmix_ref[...] = dmb
        don = _dot(dmb, woT_ref[...])
        o = jnp.concatenate([osb_ref[...], ofx_ref[...]], axis=1)
        r = rr_ref[...]
        u = don * g_ref[...]
        t = _dot_acc(u * o, gm_ref[...]) * (1.0 / HEAD_DIM)
        do = r * u - o * (r * r * r) * t
        dosb_ref[...] = do[:, :H]
        dofx_ref[...] = do[:, H:]
        sm_ref[0:1, :] += jnp.sum(dh * xhv, axis=0, keepdims=True)
        sm_ref[1:2, :] += jnp.sum(dh, axis=0, keepdims=True)
        sm_ref[2:3, :] += jnp.sum(don * o * r, axis=0, keepdims=True)

        @pl.when(pl.program_id(0) == nT - 1)
        def _():
            ex.finish(ch_in, ch_out, sems)

    row = lambda w: pl.BlockSpec((TM, w), lambda i: (i, 0))
    any_spec = pl.BlockSpec(memory_space=pl.ANY)
    return pl.pallas_call(
        body, name="post_attn_bwd", grid=(nT,),
        in_specs=[row(D), row(D), row(LANES), _full((1, D)), row(H), row(H), row(D), _full((1, D)),
                  _full((D, D)), _full((D, D))] + [any_spec] * n_ch,
        out_specs=[row(D), row(D), row(H), row(H), _full((8, D))] + [any_spec] * n_ch,
        out_shape=[jax.ShapeDtypeStruct((S, D), F32), jax.ShapeDtypeStruct((S, D), BF16),
                   jax.ShapeDtypeStruct((S, H), F32), jax.ShapeDtypeStruct((S, H), F32),
                   jax.ShapeDtypeStruct((8, D), F32)] + ex.out_shapes(chunked),
        scratch_shapes=ex.sem_shapes(),
        compiler_params=_cparams(("arbitrary",)),
    )(dh1, xh, rs, ln_g, o_sb, o_fx, rr, g_cat, gmat, w_outT, *chunked)


def _proj_bwd(dxa, x, pieces, dc, u, w_qkvT, w_fT, n_fox, chunked):
    S, D = dxa.shape
    H = D // 2
    TM = ROW_TILE
    nT = S // TM
    tri = _tri(TM, lambda r, c: c >= r)
    n_p = len(pieces)
    n_ch = len(chunked)
    ex = _Exchange(n_ch, scatter=True)

    def body(*refs):
        dxa_ref, x_ref = refs[:2]
        p_refs = refs[2:2 + n_p]
        dc_ref, u_ref, wq_ref, wf_ref, tri_ref = refs[2 + n_p:7 + n_p]
        ch_in = refs[7 + n_p:7 + n_p + n_ch]
        dx_ref, gwf_ref, sm_ref = refs[7 + n_p + n_ch:10 + n_p + n_ch]
        ch_out = refs[10 + n_p + n_ch:10 + n_p + 2 * n_ch]
        run_ref = refs[10 + n_p + 2 * n_ch]
        sems = refs[11 + n_p + 2 * n_ch:]

        @pl.when(pl.program_id(0) == 0)
        def _():
            ex.start(ch_in, ch_out, sems)
            run_ref[...] = jnp.zeros_like(run_ref)
            sm_ref[...] = jnp.zeros_like(sm_ref)
            gwf_ref[...] = jnp.zeros_like(gwf_ref)

        hi, mid, lo = _split3(dc_ref[...])
        t = tri_ref[...]
        dlogf = _dot(t, hi) + _dot(t, mid) + _dot(t, lo) + run_ref[...]
        run_ref[...] = dlogf[0:1, :]
        uv = u_ref[...]
        lane = lax.broadcasted_iota(I32, uv.shape, 1)
        df = jnp.where(lane < n_fox, dlogf * jax.nn.sigmoid(-uv), 0.0)
        sm_ref[0:1, :] += jnp.sum(df, axis=0, keepdims=True)
        dfb = df.astype(BF16)
        gwf_ref[...] += _dot_tn(x_ref[...].astype(BF16), dfb)
        acc = dxa_ref[...] + _dot(dfb, wf_ref[...])
        for a in range(n_p):
            acc = acc + _dot(p_refs[a][...].astype(BF16), wq_ref[a * H:(a + 1) * H, :])
        dx_ref[...] = acc

        @pl.when(pl.program_id(0) == nT - 1)
        def _():
            ex.finish(ch_in, ch_out, sems)

    rev = lambda w: pl.BlockSpec((TM, w), lambda i: (nT - 1 - i, 0))
    any_spec = pl.BlockSpec(memory_space=pl.ANY)
    return pl.pallas_call(
        body, name="proj_bwd", grid=(nT,),
        in_specs=[rev(D), rev(D)] + [rev(H)] * n_p + [rev(LANES), rev(LANES), _full(w_qkvT.shape),
                                                       _full(w_fT.shape), _full(tri.shape)] + [any_spec] * n_ch,
        out_specs=[rev(D), _full((D, LANES)), _full((8, LANES))] + [any_spec] * n_ch,
        out_shape=[jax.ShapeDtypeStruct((S, D), F32), jax.ShapeDtypeStruct((D, LANES), F32),
                   jax.ShapeDtypeStruct((8, LANES), F32)] + ex.out_shapes(chunked),
        scratch_shapes=[pltpu.VMEM((1, LANES), F32)] + ex.sem_shapes(),
        compiler_params=_cparams(("arbitrary",)),
    )(dxa, x, *pieces, dc, u, w_qkvT, w_fT, tri, *chunked)


def _matmul_tn(a, bs, name, n_split=1, pad_cols=0):
    S, M = a.shape
    widths = [b.shape[1] for b in bs]
    N = sum(widths) + pad_cols
    assert n_split == 1 or (len(bs) == 1 and pad_cols == 0)
    TK = 512 if S % 512 == 0 else ROW_TILE
    MC = 512 if M % 512 == 0 else 256
    nb = len(bs)

    def body(*refs):
        a_ref, b_refs, o_ref, acc_ref = refs[0], refs[1:1 + nb], refs[1 + nb], refs[2 + nb]

        @pl.when(pl.program_id(1) == 0)
        def _():
            acc_ref[...] = jnp.zeros_like(acc_ref)

        n0 = 0
        for b_ref in b_refs:
            bv = b_ref[...].astype(BF16)
            w = bv.shape[1]
            for m0 in range(0, M, MC):
                acc_ref[m0:m0 + MC, n0:n0 + w] += _dot_tn(a_ref[:, m0:m0 + MC].astype(BF16), bv)
            n0 += w

        @pl.when(pl.program_id(1) == S // TK - 1)
        def _():
            o_ref[...] = acc_ref[...].astype(BF16)

    return pl.pallas_call(
        body, name=name, grid=(n_split, S // TK),
        in_specs=[pl.BlockSpec((TK, M), lambda n, k: (k, 0))]
        + [pl.BlockSpec((TK, w // n_split), lambda n, k: (k, n)) for w in widths],
        out_specs=pl.BlockSpec((M, N // n_split), lambda n, k: (0, n)),
        out_shape=jax.ShapeDtypeStruct((M, N), BF16),
        scratch_shapes=[pltpu.VMEM((M, N // n_split), F32)],
        compiler_params=_cparams(("arbitrary", "arbitrary")),
    )(a, *bs)


def _half_mask(half):
    lane = lax.broadcasted_iota(I32, (1, LANES), 1)
    return (lane >= half * HEAD_DIM) & (lane < half * HEAD_DIM + HEAD_DIM)


def _lane_tile(a, width):
    return jnp.concatenate([a] * (width // LANES), axis=1)


def _softplus_neg_abs(z):
    return jnp.log(1.0 + jnp.exp(-jnp.abs(z)))


def _sb_fwd(qkv, n_pair, shards):
    S = qkv.shape[0]
    B = SB_BLOCK
    nq = S // B
    us = _tri(B, lambda r, c: r > c)
    n_sh = len(shards)
    ex = _Exchange(n_sh, scatter=False)

    def body(*refs):
        q_ref, k_ref, v_ref, us_ref = refs[:4]
        sh_in = refs[4:4 + n_sh]
        o_ref, st_ref, js_ref = refs[4 + n_sh:7 + n_sh]
        sh_out = refs[7 + n_sh:7 + 2 * n_sh]
        acc_ref, r_ref = refs[7 + 2 * n_sh:9 + 2 * n_sh]
        sems = refs[9 + 2 * n_sh:]
        p, i = pl.program_id(0), pl.program_id(1)

        @pl.when((p == 0) & (i == 0))
        def _():
            ex.start(sh_in, sh_out, sems)

        hms = [_half_mask(h) for h in range(2)]
        qv = q_ref[...]
        qss = [jnp.where(hm, qv, jnp.zeros_like(qv)) * SCALE for hm in hms]
        row = lax.broadcasted_iota(I32, (B, B), 0)
        col = lax.broadcasted_iota(I32, (B, B), 1)
        tri = col < row
        acc_ref[...] = jnp.zeros_like(acc_ref)
        r_ref[...] = jnp.zeros_like(r_ref)

        def block(j, diag):
            off = pl.multiple_of(j * B, B)
            kj = k_ref[pl.ds(off, B), :]
            vj = v_ref[pl.ds(off, B), :]
            zs = [_dot_nt(qss[h], kj) for h in range(2)]
            sps = [_softplus_neg_abs(z) for z in zs]
            bs = [jnp.minimum(-z, 0.0) - sp for z, sp in zip(zs, sps)]
            if diag:
                bs = [jnp.where(tri, b, 0.0) for b in bs]
            lexcs = [_dot_acc(b, us_ref[...]) for b in bs]
            ws = [jnp.exp(jnp.minimum(z, 0.0) - sp + (_lane_tile(r_ref[h], B) + lexc))
                  for h, (z, sp, lexc) in enumerate(zip(zs, sps, lexcs))]
            if diag:
                ws = [jnp.where(tri, w, 0.0) for w in ws]
            for h in range(2):
                acc_ref[h] += _dot(ws[h].astype(BF16), vj)
                r_ref[h] += jnp.broadcast_to(lexcs[h][:, 0:1] + bs[h][:, 0:1], (B, LANES))

        def live():
            return (jnp.max(r_ref[...]) > -EXP_ZERO).astype(I32)

        def first_two():
            has_prev = (i > 0).astype(F32)
            off_d = pl.multiple_of(i * B, B)
            off_o = pl.multiple_of(jnp.maximum(i - 1, 0) * B, B)
            k_d, v_d = k_ref[pl.ds(off_d, B), :], v_ref[pl.ds(off_d, B), :]
            k_o, v_o = k_ref[pl.ds(off_o, B), :], v_ref[pl.ds(off_o, B), :]
            hs = range(2)
            z_d = [_dot_nt(qss[h], k_d) for h in hs]
            z_o = [_dot_nt(qss[h], k_o) for h in hs]
            sp_d = [_softplus_neg_abs(z) for z in z_d]
            sp_o = [_softplus_neg_abs(z) for z in z_o]
            b_d = [jnp.where(tri, jnp.minimum(-z, 0.0) - sp, 0.0) for z, sp in zip(z_d, sp_d)]
            b_o = [(jnp.minimum(-z, 0.0) - sp) * has_prev for z, sp in zip(z_o, sp_o)]
            l_d = [_dot_acc(b, us_ref[...]) for b in b_d]
            l_o = [_dot_acc(b, us_ref[...]) for b in b_o]
            r_d = [jnp.broadcast_to(l[:, 0:1] + b[:, 0:1], (B, LANES)) for l, b in zip(l_d, b_d)]
            w_d = [jnp.where(tri, jnp.exp(jnp.minimum(z, 0.0) - sp + l), 0.0) for z, sp, l in zip(z_d, sp_d, l_d)]
            w_o = [jnp.exp(jnp.minimum(z, 0.0) - sp + (_lane_tile(r, B) + l)) * has_prev
                   for z, sp, l, r in zip(z_o, sp_o, l_o, r_d)]
            for h in hs:
                acc_ref[h] = _dot(w_d[h].astype(BF16), v_d) + _dot(w_o[h].astype(BF16), v_o)
                r_ref[h] = r_d[h] + jnp.broadcast_to(l_o[h][:, 0:1] + b_o[h][:, 0:1], (B, LANES))

        first_two()

        def step(carry):
            j, _ = carry
            block(j, False)
            return j - 1, live()

        j_end, _ = lax.while_loop(lambda c: (c[0] >= 0) & (c[1] > 0), step, (i - 2, live()))
        js = jnp.maximum(j_end + 1, 0)
        js_ref[2 * p, i] = js
        js_ref[2 * p + 1, i] = js
        o_ref[...] = jnp.where(hms[0], acc_ref[0], acc_ref[1])
        st_ref[...] = r_ref[...]

        @pl.when((p == n_pair - 1) & (i == nq - 1))
        def _():
            ex.finish(sh_in, sh_out, sems)

    any_spec = pl.BlockSpec(memory_space=pl.ANY)
    return pl.pallas_call(
        body, name="sb_attn_fwd", grid=(n_pair, nq),
        in_specs=[pl.BlockSpec((B, LANES), lambda p, i: (i, p)),
                  pl.BlockSpec((S, LANES), lambda p, i: (0, n_pair + p)),
                  pl.BlockSpec((S, LANES), lambda p, i: (0, 2 * n_pair + p)),
                  _full((B, B))] + [any_spec] * n_sh,
        out_specs=[pl.BlockSpec((B, LANES), lambda p, i: (i, p)),
                   pl.BlockSpec((2, B, LANES), lambda p, i: (p, i, 0)),
                   pl.BlockSpec(memory_space=pltpu.SMEM)] + [any_spec] * n_sh,
        out_shape=[jax.ShapeDtypeStruct((S, n_pair * LANES), F32),
                   jax.ShapeDtypeStruct((2 * n_pair, S, LANES), F32),
                   jax.ShapeDtypeStruct((2 * n_pair, nq), I32)] + ex.out_shapes(shards),
        scratch_shapes=[pltpu.VMEM((2, B, LANES), F32), pltpu.VMEM((2, B, LANES), F32)] + ex.sem_shapes(),
        compiler_params=_cparams(("arbitrary", "arbitrary")),
    )(qkv, qkv, qkv, us, *shards)


def _sb_bwd(qkv, do, st, js, n_pair):
    S = qkv.shape[0]
    B = SB_BLOCK
    nq = S // B
    us = _tri(B, lambda r, c: r > c)
    ti = _tri(B, lambda r, c: r <= c)

    def body(js_ref, q_ref, k_ref, v_ref, do_ref, st_ref, us_ref, ti_ref, dq_ref, dk_ref, dv_ref,
             dqa_ref, pr_ref, er_ref):
        p, i = pl.program_id(0), pl.program_id(1)

        @pl.when(i == 0)
        def _():
            dk_ref[...] = jnp.zeros_like(dk_ref)
            dv_ref[...] = jnp.zeros_like(dv_ref)

        hms = [_half_mask(h) for h in range(2)]
        qv = q_ref[...]
        dov = do_ref[...]
        qss = [jnp.where(hm, qv, jnp.zeros_like(qv)) * SCALE for hm in hms]
        dobs = [jnp.where(hm, dov, 0.0).astype(BF16) for hm in hms]
        row = lax.broadcasted_iota(I32, (B, B), 0)
        col = lax.broadcasted_iota(I32, (B, B), 1)
        tri = col < row
        dqa_ref[...] = jnp.zeros_like(dqa_ref)
        pr_ref[...] = jnp.zeros_like(pr_ref)
        er_ref[...] = jnp.zeros_like(er_ref)

        def block(j, diag):
            off = pl.multiple_of(j * B, B)
            kj = k_ref[pl.ds(off, B), :]
            vj = v_ref[pl.ds(off, B), :]
            hs = range(2)
            zs = [_dot_nt(qss[h], kj) for h in hs]
            dws = [_dot_nt(dobs[h], vj) for h in hs]
            sps = [_softplus_neg_abs(z) for z in zs]
            bs = [jnp.minimum(-z, 0.0) - sp for z, sp in zip(zs, sps)]
            if diag:
                bs = [jnp.where(tri, b, 0.0) for b in bs]
            lexcs = [_dot_acc(b, us_ref[...]) for b in bs]
            ws = []
            for h in hs:
                pr_new = pr_ref[h] + jnp.broadcast_to(lexcs[h][:, 0:1] + bs[h][:, 0:1], (B, LANES))
                pr_ref[h] = pr_new
                w = jnp.exp(jnp.minimum(zs[h], 0.0) - sps[h] + (_lane_tile(st_ref[h] - pr_new, B) + lexcs[h]))
                ws.append(jnp.where(tri, w, 0.0) if diag else w)
            es = [dw * w for dw, w in zip(dws, ws)]
            eincs = [_dot_acc(e, ti_ref[...]) for e in es]
            dzbs = []
            for h in hs:
                er = er_ref[h]
                big_e = _lane_tile(er, B) + (eincs[h] - es[h])
                er_ref[h] = er + jnp.broadcast_to(eincs[h][:, B - 1:B], (B, LANES))
                eb = jnp.exp(bs[h])
                dzbs.append((es[h] * eb - big_e * (1.0 - eb)).astype(BF16))
            for h in hs:
                dqa_ref[h] += _dot(dzbs[h], kj)
            dk_ref[pl.ds(off, B), :] += _dot_tn(dzbs[0], qss[0]) + _dot_tn(dzbs[1], qss[1])
            dv_ref[pl.ds(off, B), :] += (_dot_tn(ws[0].astype(BF16), dobs[0])
                                         + _dot_tn(ws[1].astype(BF16), dobs[1]))

        def step(j, carry):
            block(j, False)
            return carry

        def last_two():
            has_prev = (i > 0).astype(F32)
            off_d = pl.multiple_of(i * B, B)
            off_o = pl.multiple_of(jnp.maximum(i - 1, 0) * B, B)
            k_d, v_d = k_ref[pl.ds(off_d, B), :], v_ref[pl.ds(off_d, B), :]
            k_o, v_o = k_ref[pl.ds(off_o, B), :], v_ref[pl.ds(off_o, B), :]
            hs = range(2)
            z_o = [_dot_nt(qss[h], k_o) for h in hs]
            z_d = [_dot_nt(qss[h], k_d) for h in hs]
            dw_o = [_dot_nt(dobs[h], v_o) for h in hs]
            dw_d = [_dot_nt(dobs[h], v_d) for h in hs]
            sp_o = [_softplus_neg_abs(z) for z in z_o]
            sp_d = [_softplus_neg_abs(z) for z in z_d]
            b_o = [(jnp.minimum(-z, 0.0) - sp) * has_prev for z, sp in zip(z_o, sp_o)]
            b_d = [jnp.where(tri, jnp.minimum(-z, 0.0) - sp, 0.0) for z, sp in zip(z_d, sp_d)]
            l_o = [_dot_acc(b, us_ref[...]) for b in b_o]
            l_d = [_dot_acc(b, us_ref[...]) for b in b_d]
            w_o, w_d = [], []
            for h in hs:
                pr1 = pr_ref[h] + jnp.broadcast_to(l_o[h][:, 0:1] + b_o[h][:, 0:1], (B, LANES))
                pr2 = pr1 + jnp.broadcast_to(l_d[h][:, 0:1] + b_d[h][:, 0:1], (B, LANES))
                st = st_ref[h]
                w_o.append(jnp.exp(jnp.minimum(z_o[h], 0.0) - sp_o[h] + (_lane_tile(st - pr1, B) + l_o[h])) * has_prev)
                w_d.append(jnp.where(tri, jnp.exp(jnp.minimum(z_d[h], 0.0) - sp_d[h]
                                                  + (_lane_tile(st - pr2, B) + l_d[h])), 0.0))
            e_o = [dw * w for dw, w in zip(dw_o, w_o)]
            e_d = [dw * w for dw, w in zip(dw_d, w_d)]
            ei_o = [_dot_acc(e, ti_ref[...]) for e in e_o]
            ei_d = [_dot_acc(e, ti_ref[...]) for e in e_d]
            dz_o, dz_d = [], []
            for h in hs:
                er = er_ref[h]
                big_o = _lane_tile(er, B) + (ei_o[h] - e_o[h])
                er1 = er + jnp.broadcast_to(ei_o[h][:, B - 1:B], (B, LANES))
                big_d = _lane_tile(er1, B) + (ei_d[h] - e_d[h])
                eb_o, eb_d = jnp.exp(b_o[h]), jnp.exp(b_d[h])
                dz_o.append((e_o[h] * eb_o - big_o * (1.0 - eb_o)).astype(BF16))
                dz_d.append((e_d[h] * eb_d - big_d * (1.0 - eb_d)).astype(BF16))
            for h in hs:
                dqa_ref[h] += _dot(dz_o[h], k_o) + _dot(dz_d[h], k_d)
            dk_ref[pl.ds(off_o, B), :] += _dot_tn(dz_o[0], qss[0]) + _dot_tn(dz_o[1], qss[1])
            dv_ref[pl.ds(off_o, B), :] += (_dot_tn(w_o[0].astype(BF16), dobs[0])
                                           + _dot_tn(w_o[1].astype(BF16), dobs[1]))
            dk_ref[pl.ds(off_d, B), :] += _dot_tn(dz_d[0], qss[0]) + _dot_tn(dz_d[1], qss[1])
            dv_ref[pl.ds(off_d, B), :] += (_dot_tn(w_d[0].astype(BF16), dobs[0])
                                           + _dot_tn(w_d[1].astype(BF16), dobs[1]))

        lax.fori_loop(js_ref[2 * p, i], i - 1, step, 0)
        last_two()
        dq_ref[...] = jnp.where(hms[0], dqa_ref[0], dqa_ref[1]) * SCALE

    W = n_pair * LANES
    return pl.pallas_call(
        body, name="sb_attn_bwd",
        grid_spec=pltpu.PrefetchScalarGridSpec(
            num_scalar_prefetch=1, grid=(n_pair, nq),
            in_specs=[pl.BlockSpec((B, LANES), lambda p, i, js: (i, p)),
                      pl.BlockSpec((S, LANES), lambda p, i, js: (0, n_pair + p)),
                      pl.BlockSpec((S, LANES), lambda p, i, js: (0, 2 * n_pair + p)),
                      pl.BlockSpec((B, LANES), lambda p, i, js: (i, p)),
                      pl.BlockSpec((2, B, LANES), lambda p, i, js: (p, i, 0)),
                      pl.BlockSpec((B, B), lambda p, i, js: (0, 0)),
                      pl.BlockSpec((B, B), lambda p, i, js: (0, 0))],
            out_specs=[pl.BlockSpec((B, LANES), lambda p, i, js: (i, p)),
                       pl.BlockSpec((S, LANES), lambda p, i, js: (0, p)),
                       pl.BlockSpec((S, LANES), lambda p, i, js: (0, p))],
            scratch_shapes=[pltpu.VMEM((2, B, LANES), F32), pltpu.VMEM((2, B, LANES), F32),
                            pltpu.VMEM((2, B, LANES), F32)]),
        out_shape=[jax.ShapeDtypeStruct((S, W), F32)] * 3,
        compiler_params=_cparams(("arbitrary", "arbitrary")),
    )(js, qkv, qkv, qkv, do, st, us, ti)


def _head_column(blk, head):
    lane = lax.broadcasted_iota(I32, (1, LANES), 1)
    return jnp.sum(jnp.where(lane == head, blk, 0.0), axis=1, keepdims=True)


def _fox_fwd(qkv, c, c_rows, kmax, n_pair):
    S = qkv.shape[0]
    BQ, BK = FOX_BQ, FOX_BK
    R = BQ // BK
    nq = S // BQ

    def body(q_ref, k_ref, v_ref, c_ref, cr_ref, km_ref, o_ref, st_ref, js_ref, acc_ref, m_ref, cb_ref, qkb_ref):
        p, i, half = pl.program_id(0), pl.program_id(1), pl.program_id(2)
        hm = _half_mask(half)
        qv = q_ref[...]
        qs = jnp.where(hm, qv, jnp.zeros_like(qv)) * SCALE
        ccol = _head_column(c_ref[...], 2 * p + half)
        cb_ref[...] = jnp.broadcast_to(ccol, (BQ, BK))
        qf = qs.astype(F32)
        qkb_ref[...] = jnp.broadcast_to(
            jnp.sqrt(jnp.sum(qf * qf, axis=1, keepdims=True)) * NORM_SLACK
            * _head_column(km_ref[...], 2 * p + half) + ccol, (BQ, LANES))
        row = lax.broadcasted_iota(I32, (BQ, BK), 0)
        col = lax.broadcasted_iota(I32, (BQ, BK), 1)
        acc_ref[...] = jnp.zeros_like(acc_ref)
        m_ref[...] = jnp.full_like(m_ref, NEG_BIG)

        def blocks(j_top, diag):
            ss, v1s, keeps = [], [], []
            for d in range(R):
                j = j_top - d
                off = pl.multiple_of(j * BK, BK)
                vj = v_ref[pl.ds(off, BK), :]
                v1s.append(jnp.where(hm, vj, jnp.ones_like(vj)))
                s = _dot_nt(qs, k_ref[pl.ds(off, BK), :]) + (cb_ref[...] - cr_ref[0, pl.ds(j, 1), :])
                if diag:
                    keeps.append(col + (R - 1 - d) * BK <= row)
                    s = jnp.where(keeps[-1], s, NEG_BIG)
                ss.append(s)
            m_old = m_ref[...]
            s_max = jnp.max(functools.reduce(jnp.maximum, ss), axis=1, keepdims=True)
            m_new = jnp.maximum(m_old, jnp.broadcast_to(s_max, (BQ, LANES)))
            m_wide = _lane_tile(m_new, BK)
            pvs = [jnp.exp(s - m_wide) for s in ss]
            if diag:
                pvs = [jnp.where(keep, pv, 0.0) for keep, pv in zip(keeps, pvs)]
            new = _dot(pvs[0].astype(BF16), v1s[0])
            for pv, v1 in zip(pvs[1:], v1s[1:]):
                new = new + _dot(pv.astype(BF16), v1)
            acc_ref[...] = jnp.exp(m_old - m_new) * acc_ref[...] + new
            m_ref[...] = m_new

        def live(j):
            c_end = cr_ref[0, pl.ds(jnp.maximum(j, 0), 1), :][:, BK - 1:BK]
            return (jnp.max(qkb_ref[...] - c_end - m_ref[...]) > -EXP_ZERO).astype(I32)

        blocks(R * i + R - 1, True)

        def step(carry):
            j, _ = carry
            go_on = live(j - R)
            blocks(j, False)
            return j - R, go_on

        j_end, _ = lax.while_loop(lambda cr: (cr[0] >= 0) & (cr[1] > 0), step, (R * i - 1, live(R * i - 1)))
        js_ref[2 * p + half, i] = j_end + 1
        acc = acc_ref[...]
        denom = jnp.where(hm, pltpu.roll(acc, HEAD_DIM, 1), acc)
        res = jnp.where(hm, acc / denom, 0.0)

        @pl.when(half == 0)
        def _():
            o_ref[...] = res

        @pl.when(half == 1)
        def _():
            o_ref[...] += res

        st_ref[0] = m_ref[...] + jnp.log(denom)

    return pl.pallas_call(
        body, name="fox_attn_fwd", grid=(n_pair, nq, 2),
        in_specs=[pl.BlockSpec((BQ, LANES), lambda p, i, h: (i, 3 * n_pair + p)),
                  pl.BlockSpec((S, LANES), lambda p, i, h: (0, 4 * n_pair + p)),
                  pl.BlockSpec((S, LANES), lambda p, i, h: (0, 5 * n_pair + p)),
                  pl.BlockSpec((BQ, LANES), lambda p, i, h: (i, 0)),
                  pl.BlockSpec((1, S // BK, BK), lambda p, i, h: (2 * p + h, 0, 0)),
                  _full((1, LANES))],
        out_specs=[pl.BlockSpec((BQ, LANES), lambda p, i, h: (i, p)),
                   pl.BlockSpec((1, BQ, LANES), lambda p, i, h: (2 * p + h, i, 0)),
                   pl.BlockSpec(memory_space=pltpu.SMEM)],
        out_shape=[jax.ShapeDtypeStruct((S, n_pair * LANES), F32),
                   jax.ShapeDtypeStruct((2 * n_pair, S, LANES), F32),
                   jax.ShapeDtypeStruct((2 * n_pair, nq), I32)],
        scratch_shapes=[pltpu.VMEM((BQ, LANES), F32), pltpu.VMEM((BQ, LANES), F32), pltpu.VMEM((BQ, BK), F32),
                        pltpu.VMEM((BQ, LANES), F32)],
        compiler_params=_cparams(("arbitrary", "arbitrary", "arbitrary")),
    )(qkv, qkv, qkv, c, c_rows, kmax)


def _fox_bwd(qkv, do, o, st, c, c_rows, js, n_pair):
    S = qkv.shape[0]
    B = FOX_BLOCK
    nq = S // B

    def body(js_ref, q_ref, k_ref, v_ref, do_ref, o_ref, st_ref, c_ref, cr_ref, dq_ref, dk_ref, dv_ref,
             dc_ref, dqa_ref, rs_ref, cb_ref, db_ref):
        p, i, half = pl.program_id(0), pl.program_id(1), pl.program_id(2)

        @pl.when((i == 0) & (half == 0))
        def _():
            dk_ref[...] = jnp.zeros_like(dk_ref)
            dv_ref[...] = jnp.zeros_like(dv_ref)
            dc_ref[...] = jnp.zeros_like(dc_ref)

        hm = _half_mask(half)
        qv = q_ref[...]
        qs = jnp.where(hm, qv, jnp.zeros_like(qv)) * SCALE
        dov = jnp.where(hm, do_ref[...], 0.0)
        dob = dov.astype(BF16)
        qs_t = jnp.transpose(qs.astype(F32)).astype(BF16)
        do_t = jnp.transpose(dov).astype(BF16)
        cb_ref[...] = jnp.broadcast_to(_head_column(c_ref[...], 2 * p + half), (B, LANES)) - st_ref[0]
        db_ref[...] = jnp.broadcast_to(jnp.sum(dov * o_ref[...], axis=1, keepdims=True), (B, LANES))
        row = lax.broadcasted_iota(I32, (B, B), 0)
        col = lax.broadcasted_iota(I32, (B, B), 1)
        dqa_ref[...] = jnp.zeros_like(dqa_ref)
        rs_ref[...] = jnp.zeros_like(rs_ref)

        def block(j, diag):
            off = pl.multiple_of(j * B, B)
            kj = k_ref[pl.ds(off, B), :]
            vj = v_ref[pl.ds(off, B), :]
            pv = jnp.exp(_dot_nt(qs, kj) + (_lane_tile(cb_ref[...], B) - cr_ref[0, pl.ds(j, 1), :]))
            if diag:
                pv = jnp.where(col <= row, pv, 0.0)
            ds = pv * (_dot_nt(dob, vj) - _lane_tile(db_ref[...], B))
            dsb = ds.astype(BF16)
            dqa_ref[...] += _dot(dsb, kj)
            dk_ref[:, pl.ds(off, B)] += _dot(qs_t, dsb)
            dv_ref[:, pl.ds(off, B)] += _dot(do_t, pv.astype(BF16))
            dc_ref[0, half, pl.ds(j, 1), :] -= jnp.sum(ds, axis=0, keepdims=True)
            rs_ref[...] += jnp.sum(ds, axis=1, keepdims=True)

        def step(j, carry):
            block(j, False)
            return carry

        lax.fori_loop(js_ref[2 * p + half, i], i, step, 0)
        block(i, True)
        res = jnp.where(hm, dqa_ref[...] * SCALE, 0.0)

        @pl.when(half == 0)
        def _():
            dq_ref[...] = res

        @pl.when(half == 1)
        def _():
            dq_ref[...] += res

        dc_ref[0, half, pl.ds(i, 1), :] += jnp.transpose(jnp.broadcast_to(rs_ref[...], (B, LANES)))[0:1, :]

    W = n_pair * LANES
    return pl.pallas_call(
        body, name="fox_attn_bwd",
        grid_spec=pltpu.PrefetchScalarGridSpec(
            num_scalar_prefetch=1, grid=(n_pair, nq, 2),
            in_specs=[pl.BlockSpec((B, LANES), lambda p, i, h, js: (i, 3 * n_pair + p)),
                      pl.BlockSpec((S, LANES), lambda p, i, h, js: (0, 4 * n_pair + p)),
                      pl.BlockSpec((S, LANES), lambda p, i, h, js: (0, 5 * n_pair + p)),
                      pl.BlockSpec((B, LANES), lambda p, i, h, js: (i, p)),
                      pl.BlockSpec((B, LANES), lambda p, i, h, js: (i, p)),
                      pl.BlockSpec((1, B, LANES), lambda p, i, h, js: (2 * p + h, i, 0)),
                      pl.BlockSpec((B, LANES), lambda p, i, h, js: (i, 0)),
                      pl.BlockSpec((1, nq, B), lambda p, i, h, js: (2 * p + h, 0, 0))],
            out_specs=[pl.BlockSpec((B, LANES), lambda p, i, h, js: (i, p)),
                       pl.BlockSpec((LANES, S), lambda p, i, h, js: (p, 0)),
                       pl.BlockSpec((LANES, S), lambda p, i, h, js: (p, 0)),
                       pl.BlockSpec((1, 2, nq, B), lambda p, i, h, js: (p, 0, 0, 0))],
            scratch_shapes=[pltpu.VMEM((B, LANES), F32), pltpu.VMEM((B, 1), F32), pltpu.VMEM((B, LANES), F32),
                            pltpu.VMEM((B, LANES), F32)]),
        out_shape=[jax.ShapeDtypeStruct((S, W), F32)] + [jax.ShapeDtypeStruct((W, S), F32)] * 2
        + [jax.ShapeDtypeStruct((n_pair, 2, nq, B), F32)],
        compiler_params=_cparams(("arbitrary", "arbitrary", "arbitrary")),
    )(js, qkv, qkv, qkv, do, o, st, c, c_rows)


def _adam(w, g, m, v):
    m = ADAM_B1 * m + (1.0 - ADAM_B1) * g
    v = ADAM_B2 * v + (1.0 - ADAM_B2) * (g * g)
    m_hat = m / (1.0 - ADAM_B1 ** ADAM_STEP)
    v_hat = v / (1.0 - ADAM_B2 ** ADAM_STEP)
    delta = -ADAM_LR * (m_hat / (jnp.sqrt(v_hat) + ADAM_EPS) + ADAM_WD * w)
    return delta, m, v


def _reduce_adam(landing, w, m, v, name, extra=None):
    R, C = w.shape
    TR = next(t for t in (256, 128, R) if R % t == 0)
    more = [] if extra is None else [extra]

    def body(*refs):
        l_ref, w_ref, m_ref, v_ref = refs[:4]
        g_ref, d_ref, nm_ref, nv_ref = refs[4 + len(more):]
        g = l_ref[0].astype(F32)
        for s in range(1, N_DEV):
            g = g + l_ref[s].astype(F32)
        if more:
            g = g + refs[4][...]
        d, nm, nv = _adam(w_ref[...], g, m_ref[...], v_ref[...])
        g_ref[...] = g
        d_ref[...] = d
        nm_ref[...] = nm
        nv_ref[...] = nv

    blk = pl.BlockSpec((TR, C), lambda i: (i, 0))
    return pl.pallas_call(
        body, name=name, grid=(R // TR,),
        in_specs=[pl.BlockSpec((N_DEV, TR, C), lambda i: (0, i, 0)), blk, blk, blk] + [blk] * len(more),
        out_specs=[blk] * 4,
        out_shape=[jax.ShapeDtypeStruct((R, C), F32)] * 4,
        compiler_params=_cparams(("arbitrary",)),
    )(landing, w, m, v, *more)


def _reduce_adam_small(landing, w, m, v):
    R, C = w.shape

    def body(l_ref, w_ref, m_ref, v_ref, g_ref, d_ref, nm_ref, nv_ref, loss_ref):
        g = l_ref[0]
        for s in range(1, N_DEV):
            g = g + l_ref[s]
        d, nm, nv = _adam(w_ref[...], g, m_ref[...], v_ref[...])
        g_ref[...] = g
        d_ref[...] = d
        nm_ref[...] = nm
        nv_ref[...] = nv
        loss_ref[...] = jnp.broadcast_to(0.5 * jnp.sum(g[7:8, :], axis=1, keepdims=True), (1, LANES))

    return pl.pallas_call(
        body, name="reduce_adam_small",
        out_shape=[jax.ShapeDtypeStruct((R, C), F32)] * 4 + [jax.ShapeDtypeStruct((1, LANES), F32)],
    )(landing, w, m, v)


def _pad_lanes(a, width):
    return jnp.pad(a, ((0, 0), (0, width - a.shape[1])))


def _pack_small(D, n_fox, g_cat, l1g, l1b, l2g, l2b, bf, last, tail):
    return jnp.concatenate([g_cat, l1g, l1b, l2g, l2b, _pad_lanes(bf, D), jnp.zeros((1, D), F32), last, tail],
                           axis=0)


def kernel(x, w_in, b_f, g_sb, g_fox, w_out, ln1_g, ln1_b, ln2_g, ln2_b, w_gate_up, w_down, loss_target, m_w_in, m_b_f, m_g_sb, m_g_fox, m_w_out, m_ln1_g, m_ln1_b, m_ln2_g, m_ln2_b, m_w_gate_up, m_w_down, v_w_in, v_b_f, v_g_sb, v_g_fox, v_w_out, v_ln1_g, v_ln1_b, v_ln2_g, v_ln2_b, v_w_gate_up, v_w_down):
    x2, tgt = x[0], loss_target[0]
    S, D = x2.shape
    W = D // 2
    n_pair = W // LANES
    n_fox = W // HEAD_DIM
    F = w_down.shape[1] * N_DEV

    g_in = _all_gather_two_level(w_in[0].astype(BF16), "w_in_all_gather")
    w_in_full = g_in.transpose(1, 0, 2).reshape(D, -1)
    w_qkv = w_in_full[:, :6 * W]
    w_f = _pad_lanes(w_in_full[:, 6 * W:], LANES)
    gmat = _tri(D, lambda r, c: (r // HEAD_DIM) == (c // HEAD_DIM))
    g_cat = jnp.concatenate([g_sb, g_fox], axis=1)

    qkv, u, c, ksq = _proj_fwd(x2, w_qkv, w_f, _pad_lanes(b_f, LANES), n_fox)
    c_t = c[:, :n_fox].T
    c_rows = c_t.reshape(n_fox, S // FOX_BLOCK, FOX_BLOCK)
    kmax = jnp.sqrt(ksq[0:1]) * NORM_SLACK
    o_sb, st_sb, js_sb, g_out, g_gu, g_dn = _sb_fwd(
        qkv, n_pair, [w_out[0].astype(BF16), w_gate_up[0].astype(BF16), w_down[0].astype(BF16)])
    w_o = g_out.reshape(D, D)
    w_gu = g_gu.transpose(1, 0, 2).reshape(D, 2 * F)
    w_dn = g_dn.reshape(F, D)
    o_fx, st_fx, js_fx = _fox_fwd(qkv, c, c_t.reshape(n_fox, S // FOX_BK, FOX_BK), kmax, n_pair)
    js_fx = js_fx // (FOX_BLOCK // FOX_BK)
    h1, xh1, rs1, on_b, rr = _post_attn_fwd(o_sb, o_fx, x2, g_cat, gmat, w_o, ln1_g, ln1_b)
    gu, act_b, dyp, sm2 = _mlp_fwd(h1, tgt, w_gu, w_dn, ln2_g, ln2_b)

    dgu_b, dh1 = _mlp_bwd(gu, dyp, w_gu.T, w_dn.T)
    gw_gu = _matmul_tn(h1, [dgu_b], "grad_w_gate_up", n_split=2)
    gw_dn = _matmul_tn(act_b, [dyp], "grad_w_down")
    dxa, dmix_b, do_sb, do_fx, sm1, l_gu, l_dn = _post_attn_bwd(
        dh1, xh1, rs1, ln1_g, o_sb, o_fx, rr, g_cat, gmat, w_o.T,
        [gw_gu.reshape(D, N_DEV, -1).transpose(1, 0, 2), gw_dn.reshape(N_DEV, F // N_DEV, D)])
    dq_sb, dk_sb, dv_sb = _sb_bwd(qkv, do_sb, st_sb, js_sb, n_pair)
    dq_fx, dk_fx, dv_fx, dcr = _fox_bwd(qkv, do_fx, o_fx, st_fx, c, c_rows, js_fx, n_pair)
    dc = _pad_lanes(dcr.reshape(n_fox, S).T, LANES)
    pieces = [dq_sb, dk_sb, dv_sb, dq_fx, dk_fx.T, dv_fx.T]
    gw_qkv = _matmul_tn(x2, pieces, "grad_w_qkv", pad_cols=LANES)[:, :6 * W + n_fox]
    gw_out = _matmul_tn(on_b, [dmix_b], "grad_w_out")
    dx, gw_f, sm0, l_in, l_out = _proj_bwd(
        dxa, x2, pieces, dc, u, w_qkv.T, w_f.T, n_fox,
        [gw_qkv.reshape(D, N_DEV, -1).transpose(1, 0, 2), gw_out.reshape(N_DEV, D // N_DEV, D)])
    gw_f = gw_f[:, :n_fox]

    small = _pack_small(D, n_fox, sm1[2:3], sm1[0:1], sm1[1:2], sm2[0:1], sm2[1:2], sm0[0:1, :n_fox],
                        sm2[2:3] * (1.0 / D), gw_f.T)
    (l_small,) = _exchange_call([jnp.broadcast_to(small[None], (N_DEV,) + small.shape)], True, "small_exchange")

    zero = jnp.zeros((1, D), F32)
    pack = lambda gc, a, b_, c_, d_, bf: _pack_small(D, n_fox, gc, a, b_, c_, d_, bf, zero,
                                                     jnp.zeros((n_fox, D), F32))
    r_small = _reduce_adam_small(
        l_small,
        pack(g_cat, ln1_g, ln1_b, ln2_g, ln2_b, b_f),
        pack(jnp.concatenate([m_g_sb, m_g_fox], axis=1), m_ln1_g, m_ln1_b, m_ln2_g, m_ln2_b, m_b_f),
        pack(jnp.concatenate([v_g_sb, v_g_fox], axis=1), v_ln1_g, v_ln1_b, v_ln2_g, v_ln2_b, v_b_f))
    loss = r_small[4][0, 0]
    cols = w_in.shape[2]
    gf_cols = jnp.pad(r_small[0][8:8 + n_fox].T, ((0, 0), (cols - n_fox, 0)))
    extra = jnp.where(_my_index() == N_DEV - 1, gf_cols, 0.0)
    r_in = _reduce_adam(l_in, w_in[0], m_w_in[0], v_w_in[0], "reduce_adam_w_in", extra)
    r_out = _reduce_adam(l_out, w_out[0], m_w_out[0], v_w_out[0], "reduce_adam_w_out")
    r_gu = _reduce_adam(l_gu, w_gate_up[0], m_w_gate_up[0], v_w_gate_up[0], "reduce_adam_w_gate_up")
    r_dn = _reduce_adam(l_dn, w_down[0], m_w_down[0], v_w_down[0], "reduce_adam_w_down")

    def unpack(kind):
        big = [r_in[kind][None], None, None, None, r_out[kind][None], None, None, None, None,
               r_gu[kind][None], r_dn[kind][None]]
        s = r_small[kind]
        big[1] = s[5:6, :n_fox]
        big[2] = s[0:1, :W]
        big[3] = s[0:1, W:]
        big[5], big[6], big[7], big[8] = s[1:2], s[2:3], s[3:4], s[4:5]
        return big

    return (loss, dx[None], *unpack(0), *unpack(1), *unpack(2), *unpack(3))
```

```python
import functools

import jax
import jax.numpy as jnp
from jax import lax
from jax.experimental import pallas as pl
from jax.experimental.pallas import tpu as pltpu

F32 = jnp.float32
BF16 = jnp.bfloat16
I32 = jnp.int32

N_DEV = 8
HEAD_DIM = 64
LANES = 128
SCALE = HEAD_DIM ** -0.5
ALPHA = 2.0 ** 0.25
LN_EPS = 1e-5
RMS_EPS = 1e-6
ADAM_LR, ADAM_B1, ADAM_B2, ADAM_EPS, ADAM_WD, ADAM_STEP = 0.001, 0.9, 0.999, 1e-08, 0.01, 10
NEG_BIG = -1e30
NORM_SLACK = 1.01
EXP_ZERO = 88.5
VMEM_LIMIT = 60 * 1024 * 1024
ROW_TILE = 256
SB_BLOCK = 256
FOX_BLOCK = 512
FOX_BQ, FOX_BK = 512, 256
assert FOX_BQ == FOX_BLOCK and FOX_BLOCK % FOX_BK == 0
MESH = pl.DeviceIdType.MESH


def _cparams(sem):
    return pltpu.CompilerParams(dimension_semantics=sem, vmem_limit_bytes=VMEM_LIMIT)


def _dot(a, b):
    return jnp.dot(a, b, preferred_element_type=F32)


def _dot_nt(a, b):
    return lax.dot_general(a, b, (((1,), (1,)), ((), ())), preferred_element_type=F32)


def _dot_tn(a, b):
    return lax.dot_general(a, b, (((0,), (0,)), ((), ())), preferred_element_type=F32)


def _split2(a):
    hi = a.astype(BF16)
    lo = (a - hi.astype(F32)).astype(BF16)
    return hi, lo


def _split3(a):
    hi = a.astype(BF16)
    r1 = a - hi.astype(F32)
    mid = r1.astype(BF16)
    lo = (r1 - mid.astype(F32)).astype(BF16)
    return hi, mid, lo


def _dot_acc(a, m):
    hi, lo = _split2(a)
    return _dot(jnp.concatenate([hi, lo], axis=1), jnp.concatenate([m, m], axis=0))


def _tri(n, fn):
    r = lax.broadcasted_iota(I32, (n, n), 0)
    c = lax.broadcasted_iota(I32, (n, n), 1)
    return fn(r, c).astype(BF16)


def _full(shape):
    nd = len(shape)
    return pl.BlockSpec(shape, lambda *_: (0,) * nd)


def _peer(k):
    x, y, c = lax.axis_index("x"), lax.axis_index("y"), lax.axis_index("c")
    return (1 - x if k & 4 else x, 1 - y if k & 2 else y, 1 - c if k & 1 else c)


def _my_index():
    return 4 * lax.axis_index("x") + 2 * lax.axis_index("y") + lax.axis_index("c")


class _Exchange:
    def __init__(self, n, scatter):
        self.n, self.scatter = n, scatter

    def sem_shapes(self):
        return [pltpu.SemaphoreType.DMA(((N_DEV - 1) * self.n,)), pltpu.SemaphoreType.DMA(((N_DEV - 1) * self.n,)),
                pltpu.SemaphoreType.DMA((self.n,))]

    def out_shapes(self, arrays):
        if self.scatter:
            return [jax.ShapeDtypeStruct(s.shape, s.dtype) for s in arrays]
        return [jax.ShapeDtypeStruct((N_DEV,) + s.shape, s.dtype) for s in arrays]

    def _copies(self, ins, outs, sems, landing):
        send_sems, recv_sems, local_sems = sems
        me = _my_index()
        src = lambda a, d: ins[a].at[d] if self.scatter else ins[a]
        local = [pltpu.make_async_copy(src(a, me), outs[a].at[me], local_sems.at[a]) for a in range(self.n)]
        remote = [pltpu.make_async_remote_copy(
            src_ref=src(a, me ^ k), dst_ref=outs[a].at[me ^ k if landing else me],
            send_sem=send_sems.at[(k - 1) * self.n + a], recv_sem=recv_sems.at[(k - 1) * self.n + a],
            device_id=_peer(k), device_id_type=MESH) for k in range(1, N_DEV) for a in range(self.n)]
        return local, remote

    def start(self, ins, outs, sems):
        local, sent = self._copies(ins, outs, sems, landing=False)
        for cp in local + sent:
            cp.start()

    def finish(self, ins, outs, sems):
        local, landed = self._copies(ins, outs, sems, landing=True)
        for cp in landed:
            cp.wait_recv()
        for cp in landed:
            cp.wait_send()
        for cp in local:
            cp.wait()


def _all_gather_two_level(shard, name):
    def body(x_ref, out_ref, send_sems, recv_sems, local_sem):
        x, y, c = lax.axis_index("x"), lax.axis_index("y"), lax.axis_index("c")
        me, sibling = (x, y, c), (x, y, 1 - c)
        chips = [(1 - x, y), (x, 1 - y), (1 - x, 1 - y)]

        def slot(px, py, pc):
            return out_ref.at[4 * px + 2 * py + pc]

        def copy(k, block, to, src=None):
            return pltpu.make_async_remote_copy(
                src_ref=slot(*block) if src is None else src, dst_ref=slot(*block),
                send_sem=send_sems.at[k], recv_sem=recv_sems.at[k], device_id=to, device_id_type=MESH)

        mine = pltpu.make_async_copy(x_ref, slot(*me), local_sem)
        mine.start()
        first = [copy(0, me, sibling, src=x_ref)]
        first += [copy(1 + j, me, (*chip, c), src=x_ref) for j, chip in enumerate(chips)]
        for cp in first:
            cp.start()
        passed = [copy(4 + j, (*chip, c), sibling) for j, chip in enumerate(chips)]
        for j, chip in enumerate(chips):
            copy(1 + j, (*chip, c), me).wait_recv()
            passed[j].start()
        copy(0, sibling, me).wait_recv()
        for j, chip in enumerate(chips):
            copy(4 + j, (*chip, 1 - c), me).wait_recv()
        for cp in first + passed:
            cp.wait_send()
        mine.wait()

    any_spec = pl.BlockSpec(memory_space=pl.ANY)
    return pl.pallas_call(
        body, name=name, out_shape=jax.ShapeDtypeStruct((N_DEV,) + shard.shape, shard.dtype),
        in_specs=[any_spec], out_specs=any_spec,
        scratch_shapes=[pltpu.SemaphoreType.DMA((N_DEV - 1,)), pltpu.SemaphoreType.DMA((N_DEV - 1,)),
                        pltpu.SemaphoreType.DMA],
    )(shard)


def _exchange_call(arrays, scatter, name):
    n = len(arrays)
    ex = _Exchange(n, scatter)

    def body(*refs):
        ins, outs, sems = refs[:n], refs[n:2 * n], refs[2 * n:]
        ex.start(ins, outs, sems)
        ex.finish(ins, outs, sems)

    any_spec = pl.BlockSpec(memory_space=pl.ANY)
    return pl.pallas_call(
        body, name=name, out_shape=ex.out_shapes(arrays),
        in_specs=[any_spec] * n, out_specs=[any_spec] * n, scratch_shapes=ex.sem_shapes(),
    )(*arrays)


def _log_sigmoid(u):
    return jnp.minimum(u, 0.0) - jnp.log1p(jnp.exp(-jnp.abs(u)))


def _proj_fwd(x, w_qkv, w_f, bf_pad, n_fox):
    S, D = x.shape
    N = w_qkv.shape[1]
    W = D // 2
    TM = ROW_TILE
    tri = _tri(TM, lambda r, c: c <= r)
    r_ = lax.broadcasted_iota(I32, (W, LANES), 0)
    c_ = lax.broadcasted_iota(I32, (W, LANES), 1)
    head_of = (r_ // HEAD_DIM == c_).astype(BF16)

    def body(x_ref, wq_ref, wf_ref, bf_ref, tri_ref, ho_ref, qkv_ref, u_ref, c_ref, ksq_ref, run_ref):
        @pl.when(pl.program_id(0) == 0)
        def _():
            run_ref[...] = jnp.zeros_like(run_ref)
            ksq_ref[...] = jnp.zeros_like(ksq_ref)

        xb = x_ref[...].astype(BF16)
        for n0 in range(0, N, D):
            chunk = _dot(xb, wq_ref[:, n0:n0 + D]).astype(BF16)
            qkv_ref[:, n0:n0 + D] = chunk
            if n0 == 4 * W:
                kf = chunk[:, :W].astype(F32)
                ksq = jnp.max(_dot_acc(kf * kf, ho_ref[...]), axis=0, keepdims=True)
                ksq_ref[...] = jnp.maximum(ksq_ref[...], ksq)
        u = _dot(xb, wf_ref[...]) + bf_ref[...]
        lane = lax.broadcasted_iota(I32, u.shape, 1)
        logf = jnp.where(lane < n_fox, _log_sigmoid(u), 0.0)
        u_ref[...] = u
        hi, mid, lo = _split3(logf)
        t = tri_ref[...]
        cs = _dot(t, hi) + _dot(t, mid) + _dot(t, lo) + run_ref[...]
        c_ref[...] = cs
        run_ref[...] = cs[TM - 1:TM, :]

    return pl.pallas_call(
        body, name="proj_fwd", grid=(S // TM,),
        in_specs=[pl.BlockSpec((TM, D), lambda i: (i, 0)), _full(w_qkv.shape), _full(w_f.shape),
                  _full(bf_pad.shape), _full(tri.shape), _full(head_of.shape)],
        out_specs=[pl.BlockSpec((TM, N), lambda i: (i, 0)), pl.BlockSpec((TM, LANES), lambda i: (i, 0)),
                   pl.BlockSpec((TM, LANES), lambda i: (i, 0)), _full((8, LANES))],
        out_shape=[jax.ShapeDtypeStruct((S, N), BF16), jax.ShapeDtypeStruct((S, LANES), F32),
                   jax.ShapeDtypeStruct((S, LANES), F32), jax.ShapeDtypeStruct((8, LANES), F32)],
        scratch_shapes=[pltpu.VMEM((1, LANES), F32)],
        compiler_params=_cparams(("arbitrary",)),
    )(x, w_qkv, w_f, bf_pad, tri, head_of)


def _post_attn_fwd(o_sb, o_fx, x, g_cat, gmat, w_out, ln_g, ln_b):
    S, D = x.shape
    H = D // 2
    TM = ROW_TILE

    def body(osb_ref, ofx_ref, x_ref, g_ref, gm_ref, wo_ref, lg_ref, lb_ref,
             h1_ref, xh_ref, rs_ref, on_ref, rr_ref):
        o = jnp.concatenate([osb_ref[...], ofx_ref[...]], axis=1)
        ms = _dot_acc(o * o, gm_ref[...]) * (1.0 / HEAD_DIM)
        r = lax.rsqrt(ms + RMS_EPS)
        onb = (o * r * g_ref[...]).astype(BF16)
        hp = ALPHA * x_ref[...] + _dot(onb, wo_ref[...])
        mu = jnp.mean(hp, axis=-1, keepdims=True)
        d = hp - mu
        rstd = lax.rsqrt(jnp.mean(d * d, axis=-1, keepdims=True) + LN_EPS)
        xh = d * rstd
        h1_ref[...] = xh * lg_ref[...] + lb_ref[...]
        xh_ref[...] = xh
        rs_ref[...] = jnp.broadcast_to(rstd, (TM, LANES))
        on_ref[...] = onb
        rr_ref[...] = r

    row = lambda w: pl.BlockSpec((TM, w), lambda i: (i, 0))
    return pl.pallas_call(
        body, name="post_attn_fwd", grid=(S // TM,),
        in_specs=[row(H), row(H), row(D), _full((1, D)), _full((D, D)), _full((D, D)), _full((1, D)), _full((1, D))],
        out_specs=[row(D), row(D), row(LANES), row(D), row(D)],
        out_shape=[jax.ShapeDtypeStruct((S, D), F32), jax.ShapeDtypeStruct((S, D), F32),
                   jax.ShapeDtypeStruct((S, LANES), F32), jax.ShapeDtypeStruct((S, D), BF16),
                   jax.ShapeDtypeStruct((S, D), F32)],
        compiler_params=_cparams(("arbitrary",)),
    )(o_sb, o_fx, x, g_cat, gmat, w_out, ln_g, ln_b)


def _ln_bwd(dxh, xh, rstd):
    m1 = jnp.mean(dxh, axis=-1, keepdims=True)
    m2 = jnp.mean(dxh * xh, axis=-1, keepdims=True)
    return rstd * (dxh - m1 - xh * m2)


def _mlp_fwd(h1, target, w_gu, w_dn, ln_g, ln_b):
    S, D = h1.shape
    F = w_dn.shape[0]
    TM = ROW_TILE
    FC = F // 2

    def body(h1_ref, tg_ref, wgu_hbm, wdn_hbm, lg_ref, lb_ref, gu_ref, act_ref, dyp_ref, sm_ref, wgu, wdn):
        @pl.when(pl.program_id(0) == 0)
        def _():
            pltpu.sync_copy(wgu_hbm, wgu)
            pltpu.sync_copy(wdn_hbm, wdn)
            sm_ref[...] = jnp.zeros_like(sm_ref)

        h1v = h1_ref[...]
        hb = h1v.astype(BF16)
        ff = jnp.zeros((TM, D), F32)
        for c0 in range(0, F, FC):
            g = _dot(hb, wgu[:, c0:c0 + FC])
            u = _dot(hb, wgu[:, F + c0:F + c0 + FC])
            gu_ref[:, c0:c0 + FC] = g
            gu_ref[:, F + c0:F + c0 + FC] = u
            ab = ((g * jax.nn.sigmoid(g)) * u).astype(BF16)
            act_ref[:, c0:c0 + FC] = ab
            ff = ff + _dot(ab, wdn[c0:c0 + FC, :])
        yp = ALPHA * h1v + ff
        mu = jnp.mean(yp, axis=-1, keepdims=True)
        d = yp - mu
        rstd = lax.rsqrt(jnp.mean(d * d, axis=-1, keepdims=True) + LN_EPS)
        xh = d * rstd
        err = (xh * lg_ref[...] + lb_ref[...]) - tg_ref[...]
        dy = err * (1.0 / D)
        sm_ref[0:1, :] += jnp.sum(dy * xh, axis=0, keepdims=True)
        sm_ref[1:2, :] += jnp.sum(dy, axis=0, keepdims=True)
        sm_ref[2:3, :] += jnp.sum(err * err, axis=0, keepdims=True)
        dyp_ref[...] = _ln_bwd(dy * lg_ref[...], xh, rstd)

    row = lambda w: pl.BlockSpec((TM, w), lambda i: (i, 0))
    hbm = pl.BlockSpec(memory_space=pl.ANY)
    return pl.pallas_call(
        body, name="mlp_fwd", grid=(S // TM,),
        in_specs=[row(D), row(D), hbm, hbm, _full((1, D)), _full((1, D))],
        out_specs=[row(2 * F), row(F), row(D), _full((8, D))],
        out_shape=[jax.ShapeDtypeStruct((S, 2 * F), F32), jax.ShapeDtypeStruct((S, F), BF16),
                   jax.ShapeDtypeStruct((S, D), F32), jax.ShapeDtypeStruct((8, D), F32)],
        scratch_shapes=[pltpu.VMEM(w_gu.shape, BF16), pltpu.VMEM(w_dn.shape, BF16)],
        compiler_params=_cparams(("arbitrary",)),
    )(h1, target, w_gu, w_dn, ln_g, ln_b)


def _mlp_bwd(gu, dyp, w_guT, w_dnT):
    S, D = dyp.shape
    F = w_dnT.shape[1]
    TM = ROW_TILE
    FC = F // 2

    def body(gu_ref, dyp_ref, wguT_hbm, wdnT_hbm, dgu_ref, dh1_ref, wguT, wdnT):
        @pl.when(pl.program_id(0) == 0)
        def _():
            pltpu.sync_copy(wguT_hbm, wguT)
            pltpu.sync_copy(wdnT_hbm, wdnT)

        dypv = dyp_ref[...]
        db = dypv.astype(BF16)
        dh1 = ALPHA * dypv
        for c0 in range(0, F, FC):
            dact = _dot(db, wdnT[:, c0:c0 + FC])
            g = gu_ref[:, c0:c0 + FC]
            u = gu_ref[:, F + c0:F + c0 + FC]
            sg = jax.nn.sigmoid(g)
            dgb = (dact * u * (sg * (1.0 + g * (1.0 - sg)))).astype(BF16)
            dub = (dact * (g * sg)).astype(BF16)
            dgu_ref[:, c0:c0 + FC] = dgb
            dgu_ref[:, F + c0:F + c0 + FC] = dub
            dh1 = dh1 + _dot(dgb, wguT[c0:c0 + FC, :]) + _dot(dub, wguT[F + c0:F + c0 + FC, :])
        dh1_ref[...] = dh1

    row = lambda w: pl.BlockSpec((TM, w), lambda i: (i, 0))
    hbm = pl.BlockSpec(memory_space=pl.ANY)
    return pl.pallas_call(
        body, name="mlp_bwd", grid=(S // TM,),
        in_specs=[row(2 * F), row(D), hbm, hbm],
        out_specs=[row(2 * F), row(D)],
        out_shape=[jax.ShapeDtypeStruct((S, 2 * F), BF16), jax.ShapeDtypeStruct((S, D), F32)],
        scratch_shapes=[pltpu.VMEM(w_guT.shape, BF16), pltpu.VMEM(w_dnT.shape, BF16)],
        compiler_params=_cparams(("arbitrary",)),
    )(gu, dyp, w_guT, w_dnT)


def _post_attn_bwd(dh1, xh, rs, ln_g, o_sb, o_fx, rr, g_cat, gmat, w_outT, chunked):
    S, D = dh1.shape
    H = D // 2
    TM = ROW_TILE
    nT = S // TM
    n_ch = len(chunked)
    ex = _Exchange(n_ch, scatter=True)

    def body(*refs):
        dh1_ref, xh_ref, rs_ref, lg_ref, osb_ref, ofx_ref, rr_ref, g_ref, gm_ref, woT_ref = refs[:10]
        ch_in = refs[10:10 + n_ch]
        dxa_ref, dmix_ref, dosb_ref, dofx_ref, sm_ref = refs[10 + n_ch:15 + n_ch]
        ch_out = refs[15 + n_ch:15 + 2 * n_ch]
        sems = refs[15 + 2 * n_ch:]

        @pl.when(pl.program_id(0) == 0)
        def _():
            ex.start(ch_in, ch_out, sems)
            sm_ref[...] = jnp.zeros_like(sm_ref)

        dh = dh1_ref[...]
        xhv = xh_ref[...]
        dhp = _ln_bwd(dh * lg_ref[...], xhv, rs_ref[:, 0:1])
        dxa_ref[...] = ALPHA * dhp
        dmb = dhp.astype(BF16)
        dmix_ref[...] = dmb
        don = _dot(dmb, woT_ref[...])
        o = jnp.concatenate([osb_ref[...], ofx_ref[...]], axis=1)
        r = rr_ref[...]
        u = don * g_ref[...]
        t = _dot_acc(u * o, gm_ref[...]) * (1.0 / HEAD_DIM)
        do = r * u - o * (r * r * r) * t
        dosb_ref[...] = do[:, :H]
        dofx_ref[...] = do[:, H:]
        sm_ref[0:1, :] += jnp.sum(dh * xhv, axis=0, keepdims=True)
        sm_ref[1:2, :] += jnp.sum(dh, axis=0, keepdims=True)
        sm_ref[2:3, :] += jnp.sum(don * o * r, axis=0, keepdims=True)

        @pl.when(pl.program_id(0) == nT - 1)
        def _():
            ex.finish(ch_in, ch_out, sems)

    row = lambda w: pl.BlockSpec((TM, w), lambda i: (i, 0))
    any_spec = pl.BlockSpec(memory_space=pl.ANY)
    return pl.pallas_call(
        body, name="post_attn_bwd", grid=(nT,),
        in_specs=[row(D), row(D), row(LANES), _full((1, D)), row(H), row(H), row(D), _full((1, D)),
                  _full((D, D)), _full((D, D))] + [any_spec] * n_ch,
        out_specs=[row(D), row(D), row(H), row(H), _full((8, D))] + [any_spec] * n_ch,
        out_shape=[jax.ShapeDtypeStruct((S, D), F32), jax.ShapeDtypeStruct((S, D), BF16),
                   jax.ShapeDtypeStruct((S, H), F32), jax.ShapeDtypeStruct((S, H), F32),
                   jax.ShapeDtypeStruct((8, D), F32)] + ex.out_shapes(chunked),
        scratch_shapes=ex.sem_shapes(),
        compiler_params=_cparams(("arbitrary",)),
    )(dh1, xh, rs, ln_g, o_sb, o_fx, rr, g_cat, gmat, w_outT, *chunked)


def _proj_bwd(dxa, x, pieces, piece_t, dc, u, w_qkvT, w_fT, n_fox, chunked):
    S, D = dxa.shape
    H = D // 2
    TM = ROW_TILE
    nT = S // TM
    tri = _tri(TM, lambda r, c: c >= r)
    n_p = len(pieces)
    n_ch = len(chunked)
    ex = _Exchange(n_ch, scatter=True)

    def body(*refs):
        dxa_ref, x_ref = refs[:2]
        p_refs = refs[2:2 + n_p]
        dc_ref, u_ref, wq_ref, wf_ref, tri_ref = refs[2 + n_p:7 + n_p]
        ch_in = refs[7 + n_p:7 + n_p + n_ch]
        dx_ref, gwf_ref, sm_ref = refs[7 + n_p + n_ch:10 + n_p + n_ch]
        ch_out = refs[10 + n_p + n_ch:10 + n_p + 2 * n_ch]
        run_ref = refs[10 + n_p + 2 * n_ch]
        sems = refs[11 + n_p + 2 * n_ch:]

        @pl.when(pl.program_id(0) == 0)
        def _():
            ex.start(ch_in, ch_out, sems)
            run_ref[...] = jnp.zeros_like(run_ref)
            sm_ref[...] = jnp.zeros_like(sm_ref)
            gwf_ref[...] = jnp.zeros_like(gwf_ref)

        hi, mid, lo = _split3(dc_ref[...])
        t = tri_ref[...]
        dlogf = _dot(t, hi) + _dot(t, mid) + _dot(t, lo) + run_ref[...]
        run_ref[...] = dlogf[0:1, :]
        uv = u_ref[...]
        lane = lax.broadcasted_iota(I32, uv.shape, 1)
        df = jnp.where(lane < n_fox, dlogf * jax.nn.sigmoid(-uv), 0.0)
        sm_ref[0:1, :] += jnp.sum(df, axis=0, keepdims=True)
        dfb = df.astype(BF16)
        gwf_ref[...] += _dot_tn(x_ref[...].astype(BF16), dfb)
        acc = dxa_ref[...] + _dot(dfb, wf_ref[...])
        for a in range(n_p):
            mm = _dot_tn if piece_t[a] else _dot
            acc = acc + mm(p_refs[a][...].astype(BF16), wq_ref[a * H:(a + 1) * H, :])
        dx_ref[...] = acc

        @pl.when(pl.program_id(0) == nT - 1)
        def _():
            ex.finish(ch_in, ch_out, sems)

    rev = lambda w: pl.BlockSpec((TM, w), lambda i: (nT - 1 - i, 0))
    any_spec = pl.BlockSpec(memory_space=pl.ANY)
    return pl.pallas_call(
        body, name="proj_bwd", grid=(nT,),
        in_specs=[rev(D), rev(D)]
        + [pl.BlockSpec((H, TM), lambda i: (0, nT - 1 - i)) if t else rev(H) for t in piece_t]
        + [rev(LANES), rev(LANES), _full(w_qkvT.shape), _full(w_fT.shape), _full(tri.shape)] + [any_spec] * n_ch,
        out_specs=[rev(D), _full((D, LANES)), _full((8, LANES))] + [any_spec] * n_ch,
        out_shape=[jax.ShapeDtypeStruct((S, D), F32), jax.ShapeDtypeStruct((D, LANES), F32),
                   jax.ShapeDtypeStruct((8, LANES), F32)] + ex.out_shapes(chunked),
        scratch_shapes=[pltpu.VMEM((1, LANES), F32)] + ex.sem_shapes(),
        compiler_params=_cparams(("arbitrary",)),
    )(dxa, x, *pieces, dc, u, w_qkvT, w_fT, tri, *chunked)


def _matmul_tn(a, bs, name, n_split=1, pad_cols=0, b_t=None):
    S, M = a.shape
    b_t = tuple(b_t) if b_t is not None else (False,) * len(bs)
    widths = [b.shape[0] if t else b.shape[1] for b, t in zip(bs, b_t)]
    N = sum(widths) + pad_cols
    assert n_split == 1 or (len(bs) == 1 and pad_cols == 0 and not b_t[0])
    TK = 512 if S % 512 == 0 else ROW_TILE
    MC = 512 if M % 512 == 0 else 256
    nb = len(bs)

    def body(*refs):
        a_ref, b_refs, o_ref, acc_ref = refs[0], refs[1:1 + nb], refs[1 + nb], refs[2 + nb]
        t_refs = list(refs[3 + nb:])

        @pl.when(pl.program_id(1) == 0)
        def _():
            acc_ref[...] = jnp.zeros_like(acc_ref)
            for t_ref in t_refs:
                t_ref[...] = jnp.zeros_like(t_ref)

        n0 = 0
        turned = []
        for b_ref, t, w in zip(b_refs, b_t, widths):
            bv = b_ref[...].astype(BF16)
            if t:
                t_ref = t_refs[len(turned)]
                t_ref[...] += _dot(bv, a_ref[...].astype(BF16))
                turned.append((t_ref, n0, w))
            else:
                for m0 in range(0, M, MC):
                    acc_ref[m0:m0 + MC, n0:n0 + w] += _dot_tn(a_ref[:, m0:m0 + MC].astype(BF16), bv)
            n0 += w

        @pl.when(pl.program_id(1) == S // TK - 1)
        def _():
            for t_ref, c0, w in turned:
                acc_ref[:, c0:c0 + w] = jnp.transpose(t_ref[...])
            o_ref[...] = acc_ref[...].astype(BF16)

    return pl.pallas_call(
        body, name=name, grid=(n_split, S // TK),
        in_specs=[pl.BlockSpec((TK, M), lambda n, k: (k, 0))]
        + [pl.BlockSpec((w, TK), lambda n, k: (0, k)) if t else pl.BlockSpec((TK, w // n_split), lambda n, k: (k, n))
           for w, t in zip(widths, b_t)],
        out_specs=pl.BlockSpec((M, N // n_split), lambda n, k: (0, n)),
        out_shape=jax.ShapeDtypeStruct((M, N), BF16),
        scratch_shapes=[pltpu.VMEM((M, N // n_split), F32)]
        + [pltpu.VMEM((w, M), F32) for w, t in zip(widths, b_t) if t],
        compiler_params=_cparams(("arbitrary", "arbitrary")),
    )(a, *bs)


def _half_mask(half):
    lane = lax.broadcasted_iota(I32, (1, LANES), 1)
    return (lane >= half * HEAD_DIM) & (lane < half * HEAD_DIM + HEAD_DIM)


def _lane_tile(a, width):
    return jnp.concatenate([a] * (width // LANES), axis=1)


def _softplus_neg_abs(z):
    return jnp.log(1.0 + jnp.exp(-jnp.abs(z)))


def _sb_fwd(qkv, n_pair, shards):
    S = qkv.shape[0]
    B = SB_BLOCK
    nq = S // B
    us = _tri(B, lambda r, c: r > c)
    n_sh = len(shards)
    ex = _Exchange(n_sh, scatter=False)

    def body(*refs):
        q_ref, k_ref, v_ref, us_ref = refs[:4]
        sh_in = refs[4:4 + n_sh]
        o_ref, st_ref, js_ref = refs[4 + n_sh:7 + n_sh]
        sh_out = refs[7 + n_sh:7 + 2 * n_sh]
        acc_ref, r_ref = refs[7 + 2 * n_sh:9 + 2 * n_sh]
        sems = refs[9 + 2 * n_sh:]
        p, i = pl.program_id(0), pl.program_id(1)

        @pl.when((p == 0) & (i == 0))
        def _():
            ex.start(sh_in, sh_out, sems)

        hms = [_half_mask(h) for h in range(2)]
        qv = q_ref[...]
        qss = [jnp.where(hm, qv, jnp.zeros_like(qv)) * SCALE for hm in hms]
        row = lax.broadcasted_iota(I32, (B, B), 0)
        col = lax.broadcasted_iota(I32, (B, B), 1)
        tri = col < row
        acc_ref[...] = jnp.zeros_like(acc_ref)
        r_ref[...] = jnp.zeros_like(r_ref)

        def block(j, diag):
            off = pl.multiple_of(j * B, B)
            kj = k_ref[pl.ds(off, B), :]
            vj = v_ref[pl.ds(off, B), :]
            zs = [_dot_nt(qss[h], kj) for h in range(2)]
            sps = [_softplus_neg_abs(z) for z in zs]
            bs = [jnp.minimum(-z, 0.0) - sp for z, sp in zip(zs, sps)]
            if diag:
                bs = [jnp.where(tri, b, 0.0) for b in bs]
            lexcs = [_dot_acc(b, us_ref[...]) for b in bs]
            ws = [jnp.exp(jnp.minimum(z, 0.0) - sp + (_lane_tile(r_ref[h], B) + lexc))
                  for h, (z, sp, lexc) in enumerate(zip(zs, sps, lexcs))]
            if diag:
                ws = [jnp.where(tri, w, 0.0) for w in ws]
            for h in range(2):
                acc_ref[h] += _dot(ws[h].astype(BF16), vj)
                r_ref[h] += jnp.broadcast_to(lexcs[h][:, 0:1] + bs[h][:, 0:1], (B, LANES))

        def live():
            return (jnp.max(r_ref[...]) > -EXP_ZERO).astype(I32)

        def first_two():
            has_prev = (i > 0).astype(F32)
            off_d = pl.multiple_of(i * B, B)
            off_o = pl.multiple_of(jnp.maximum(i - 1, 0) * B, B)
            k_d, v_d = k_ref[pl.ds(off_d, B), :], v_ref[pl.ds(off_d, B), :]
            k_o, v_o = k_ref[pl.ds(off_o, B), :], v_ref[pl.ds(off_o, B), :]
            hs = range(2)
            z_d = [_dot_nt(qss[h], k_d) for h in hs]
            z_o = [_dot_nt(qss[h], k_o) for h in hs]
            sp_d = [_softplus_neg_abs(z) for z in z_d]
            sp_o = [_softplus_neg_abs(z) for z in z_o]
            b_d = [jnp.where(tri, jnp.minimum(-z, 0.0) - sp, 0.0) for z, sp in zip(z_d, sp_d)]
            b_o = [(jnp.minimum(-z, 0.0) - sp) * has_prev for z, sp in zip(z_o, sp_o)]
            l_d = [_dot_acc(b, us_ref[...]) for b in b_d]
            l_o = [_dot_acc(b, us_ref[...]) for b in b_o]
            r_d = [jnp.broadcast_to(l[:, 0:1] + b[:, 0:1], (B, LANES)) for l, b in zip(l_d, b_d)]
            w_d = [jnp.where(tri, jnp.exp(jnp.minimum(z, 0.0) - sp + l), 0.0) for z, sp, l in zip(z_d, sp_d, l_d)]
            w_o = [jnp.exp(jnp.minimum(z, 0.0) - sp + (_lane_tile(r, B) + l)) * has_prev
                   for z, sp, l, r in zip(z_o, sp_o, l_o, r_d)]
            for h in hs:
                acc_ref[h] = _dot(w_d[h].astype(BF16), v_d) + _dot(w_o[h].astype(BF16), v_o)
                r_ref[h] = r_d[h] + jnp.broadcast_to(l_o[h][:, 0:1] + b_o[h][:, 0:1], (B, LANES))

        first_two()

        def step(carry):
            j, _ = carry
            block(j, False)
            return j - 1, live()

        j_end, _ = lax.while_loop(lambda c: (c[0] >= 0) & (c[1] > 0), step, (i - 2, live()))
        js = jnp.maximum(j_end + 1, 0)
        js_ref[2 * p, i] = js
        js_ref[2 * p + 1, i] = js
        o_ref[...] = jnp.where(hms[0], acc_ref[0], acc_ref[1])
        st_ref[...] = r_ref[...]

        @pl.when((p == n_pair - 1) & (i == nq - 1))
        def _():
            ex.finish(sh_in, sh_out, sems)

    any_spec = pl.BlockSpec(memory_space=pl.ANY)
    return pl.pallas_call(
        body, name="sb_attn_fwd", grid=(n_pair, nq),
        in_specs=[pl.BlockSpec((B, LANES), lambda p, i: (i, p)),
                  pl.BlockSpec((S, LANES), lambda p, i: (0, n_pair + p)),
                  pl.BlockSpec((S, LANES), lambda p, i: (0, 2 * n_pair + p)),
                  _full((B, B))] + [any_spec] * n_sh,
        out_specs=[pl.BlockSpec((B, LANES), lambda p, i: (i, p)),
                   pl.BlockSpec((2, B, LANES), lambda p, i: (p, i, 0)),
                   pl.BlockSpec(memory_space=pltpu.SMEM)] + [any_spec] * n_sh,
        out_shape=[jax.ShapeDtypeStruct((S, n_pair * LANES), F32),
                   jax.ShapeDtypeStruct((2 * n_pair, S, LANES), F32),
                   jax.ShapeDtypeStruct((2 * n_pair, nq), I32)] + ex.out_shapes(shards),
        scratch_shapes=[pltpu.VMEM((2, B, LANES), F32), pltpu.VMEM((2, B, LANES), F32)] + ex.sem_shapes(),
        compiler_params=_cparams(("arbitrary", "arbitrary")),
    )(qkv, qkv, qkv, us, *shards)


def _sb_bwd(qkv, do, st, js, n_pair):
    S = qkv.shape[0]
    B = SB_BLOCK
    nq = S // B
    us = _tri(B, lambda r, c: r > c)
    ti = _tri(B, lambda r, c: r <= c)

    def body(js_ref, q_ref, k_ref, v_ref, do_ref, st_ref, us_ref, ti_ref, dq_ref, dk_ref, dv_ref,
             dqa_ref, pr_ref, er_ref):
        p, i = pl.program_id(0), pl.program_id(1)

        @pl.when(i == 0)
        def _():
            dk_ref[...] = jnp.zeros_like(dk_ref)
            dv_ref[...] = jnp.zeros_like(dv_ref)

        hms = [_half_mask(h) for h in range(2)]
        qv = q_ref[...]
        dov = do_ref[...]
        qss = [jnp.where(hm, qv, jnp.zeros_like(qv)) * SCALE for hm in hms]
        dobs = [jnp.where(hm, dov, 0.0).astype(BF16) for hm in hms]
        row = lax.broadcasted_iota(I32, (B, B), 0)
        col = lax.broadcasted_iota(I32, (B, B), 1)
        tri = col < row
        dqa_ref[...] = jnp.zeros_like(dqa_ref)
        pr_ref[...] = jnp.zeros_like(pr_ref)
        er_ref[...] = jnp.zeros_like(er_ref)

        def block(j, diag):
            off = pl.multiple_of(j * B, B)
            kj = k_ref[pl.ds(off, B), :]
            vj = v_ref[pl.ds(off, B), :]
            hs = range(2)
            zs = [_dot_nt(qss[h], kj) for h in hs]
            dws = [_dot_nt(dobs[h], vj) for h in hs]
            sps = [_softplus_neg_abs(z) for z in zs]
            bs = [jnp.minimum(-z, 0.0) - sp for z, sp in zip(zs, sps)]
            if diag:
                bs = [jnp.where(tri, b, 0.0) for b in bs]
            lexcs = [_dot_acc(b, us_ref[...]) for b in bs]
            ws = []
            for h in hs:
                pr_new = pr_ref[h] + jnp.broadcast_to(lexcs[h][:, 0:1] + bs[h][:, 0:1], (B, LANES))
                pr_ref[h] = pr_new
                w = jnp.exp(jnp.minimum(zs[h], 0.0) - sps[h] + (_lane_tile(st_ref[h] - pr_new, B) + lexcs[h]))
                ws.append(jnp.where(tri, w, 0.0) if diag else w)
            es = [dw * w for dw, w in zip(dws, ws)]
            eincs = [_dot_acc(e, ti_ref[...]) for e in es]
            dzbs = []
            for h in hs:
                er = er_ref[h]
                big_e = _lane_tile(er, B) + (eincs[h] - es[h])
                er_ref[h] = er + jnp.broadcast_to(eincs[h][:, B - 1:B], (B, LANES))
                eb = jnp.exp(bs[h])
                dzbs.append((es[h] * eb - big_e * (1.0 - eb)).astype(BF16))
            for h in hs:
                dqa_ref[h] += _dot(dzbs[h], kj)
            dk_ref[pl.ds(off, B), :] += _dot_tn(dzbs[0], qss[0]) + _dot_tn(dzbs[1], qss[1])
            dv_ref[pl.ds(off, B), :] += (_dot_tn(ws[0].astype(BF16), dobs[0])
                                         + _dot_tn(ws[1].astype(BF16), dobs[1]))

        def step(j, carry):
            block(j, False)
            return carry

        def last_two():
            has_prev = (i > 0).astype(F32)
            off_d = pl.multiple_of(i * B, B)
            off_o = pl.multiple_of(jnp.maximum(i - 1, 0) * B, B)
            k_d, v_d = k_ref[pl.ds(off_d, B), :], v_ref[pl.ds(off_d, B), :]
            k_o, v_o = k_ref[pl.ds(off_o, B), :], v_ref[pl.ds(off_o, B), :]
            hs = range(2)
            z_o = [_dot_nt(qss[h], k_o) for h in hs]
            z_d = [_dot_nt(qss[h], k_d) for h in hs]
            dw_o = [_dot_nt(dobs[h], v_o) for h in hs]
            dw_d = [_dot_nt(dobs[h], v_d) for h in hs]
            sp_o = [_softplus_neg_abs(z) for z in z_o]
            sp_d = [_softplus_neg_abs(z) for z in z_d]
            b_o = [(jnp.minimum(-z, 0.0) - sp) * has_prev for z, sp in zip(z_o, sp_o)]
            b_d = [jnp.where(tri, jnp.minimum(-z, 0.0) - sp, 0.0) for z, sp in zip(z_d, sp_d)]
            l_o = [_dot_acc(b, us_ref[...]) for b in b_o]
            l_d = [_dot_acc(b, us_ref[...]) for b in b_d]
            w_o, w_d = [], []
            for h in hs:
                pr1 = pr_ref[h] + jnp.broadcast_to(l_o[h][:, 0:1] + b_o[h][:, 0:1], (B, LANES))
                pr2 = pr1 + jnp.broadcast_to(l_d[h][:, 0:1] + b_d[h][:, 0:1], (B, LANES))
                st = st_ref[h]
                w_o.append(jnp.exp(jnp.minimum(z_o[h], 0.0) - sp_o[h] + (_lane_tile(st - pr1, B) + l_o[h])) * has_prev)
                w_d.append(jnp.where(tri, jnp.exp(jnp.minimum(z_d[h], 0.0) - sp_d[h]
                                                  + (_lane_tile(st - pr2, B) + l_d[h])), 0.0))
            e_o = [dw * w for dw, w in zip(dw_o, w_o)]
            e_d = [dw * w for dw, w in zip(dw_d, w_d)]
            ei_o = [_dot_acc(e, ti_ref[...]) for e in e_o]
            ei_d = [_dot_acc(e, ti_ref[...]) for e in e_d]
            dz_o, dz_d = [], []
            for h in hs:
                er = er_ref[h]
                big_o = _lane_tile(er, B) + (ei_o[h] - e_o[h])
                er1 = er + jnp.broadcast_to(ei_o[h][:, B - 1:B], (B, LANES))
                big_d = _lane_tile(er1, B) + (ei_d[h] - e_d[h])
                eb_o, eb_d = jnp.exp(b_o[h]), jnp.exp(b_d[h])
                dz_o.append((e_o[h] * eb_o - big_o * (1.0 - eb_o)).astype(BF16))
                dz_d.append((e_d[h] * eb_d - big_d * (1.0 - eb_d)).astype(BF16))
            for h in hs:
                dqa_ref[h] += _dot(dz_o[h], k_o) + _dot(dz_d[h], k_d)
            dk_ref[pl.ds(off_o, B), :] += _dot_tn(dz_o[0], qss[0]) + _dot_tn(dz_o[1], qss[1])
            dv_ref[pl.ds(off_o, B), :] += (_dot_tn(w_o[0].astype(BF16), dobs[0])
                                           + _dot_tn(w_o[1].astype(BF16), dobs[1]))
            dk_ref[pl.ds(off_d, B), :] += _dot_tn(dz_d[0], qss[0]) + _dot_tn(dz_d[1], qss[1])
            dv_ref[pl.ds(off_d, B), :] += (_dot_tn(w_d[0].astype(BF16), dobs[0])
                                           + _dot_tn(w_d[1].astype(BF16), dobs[1]))

        lax.fori_loop(js_ref[2 * p, i], i - 1, step, 0)
        last_two()
        dq_ref[...] = jnp.where(hms[0], dqa_ref[0], dqa_ref[1]) * SCALE

    W = n_pair * LANES
    return pl.pallas_call(
        body, name="sb_attn_bwd",
        grid_spec=pltpu.PrefetchScalarGridSpec(
            num_scalar_prefetch=1, grid=(n_pair, nq),
            in_specs=[pl.BlockSpec((B, LANES), lambda p, i, js: (i, p)),
                      pl.BlockSpec((S, LANES), lambda p, i, js: (0, n_pair + p)),
                      pl.BlockSpec((S, LANES), lambda p, i, js: (0, 2 * n_pair + p)),
                      pl.BlockSpec((B, LANES), lambda p, i, js: (i, p)),
                      pl.BlockSpec((2, B, LANES), lambda p, i, js: (p, i, 0)),
                      pl.BlockSpec((B, B), lambda p, i, js: (0, 0)),
                      pl.BlockSpec((B, B), lambda p, i, js: (0, 0))],
            out_specs=[pl.BlockSpec((B, LANES), lambda p, i, js: (i, p)),
                       pl.BlockSpec((S, LANES), lambda p, i, js: (0, p)),
                       pl.BlockSpec((S, LANES), lambda p, i, js: (0, p))],
            scratch_shapes=[pltpu.VMEM((2, B, LANES), F32), pltpu.VMEM((2, B, LANES), F32),
                            pltpu.VMEM((2, B, LANES), F32)]),
        out_shape=[jax.ShapeDtypeStruct((S, W), F32)] * 3,
        compiler_params=_cparams(("arbitrary", "arbitrary")),
    )(js, qkv, qkv, qkv, do, st, us, ti)


def _head_column(blk, head):
    lane = lax.broadcasted_iota(I32, (1, LANES), 1)
    return jnp.sum(jnp.where(lane == head, blk, 0.0), axis=1, keepdims=True)


def _fox_fwd(qkv, c, c_rows, kmax, n_pair):
    S = qkv.shape[0]
    BQ, BK = FOX_BQ, FOX_BK
    R = BQ // BK
    nq = S // BQ

    def body(q_ref, k_ref, v_ref, c_ref, cr_ref, km_ref, o_ref, st_ref, js_ref, acc_ref, m_ref, cb_ref, qkb_ref):
        p, i, half = pl.program_id(0), pl.program_id(1), pl.program_id(2)
        hm = _half_mask(half)
        qv = q_ref[...]
        qs = jnp.where(hm, qv, jnp.zeros_like(qv)) * SCALE
        ccol = _head_column(c_ref[...], 2 * p + half)
        cb_ref[...] = jnp.broadcast_to(ccol, (BQ, BK))
        qf = qs.astype(F32)
        qkb_ref[...] = jnp.broadcast_to(
            jnp.sqrt(jnp.sum(qf * qf, axis=1, keepdims=True)) * NORM_SLACK
            * _head_column(km_ref[...], 2 * p + half) + ccol, (BQ, LANES))
        row = lax.broadcasted_iota(I32, (BQ, BK), 0)
        col = lax.broadcasted_iota(I32, (BQ, BK), 1)
        acc_ref[...] = jnp.zeros_like(acc_ref)
        m_ref[...] = jnp.full_like(m_ref, NEG_BIG)

        def blocks(j_top, diag):
            ss, v1s, keeps = [], [], []
            for d in range(R):
                j = j_top - d
                off = pl.multiple_of(j * BK, BK)
                vj = v_ref[pl.ds(off, BK), :]
                v1s.append(jnp.where(hm, vj, jnp.ones_like(vj)))
                s = _dot_nt(qs, k_ref[pl.ds(off, BK), :]) + (cb_ref[...] - cr_ref[0, pl.ds(j, 1), :])
                if diag:
                    keeps.append(col + (R - 1 - d) * BK <= row)
                    s = jnp.where(keeps[-1], s, NEG_BIG)
                ss.append(s)
            m_old = m_ref[...]
            s_max = jnp.max(functools.reduce(jnp.maximum, ss), axis=1, keepdims=True)
            m_new = jnp.maximum(m_old, jnp.broadcast_to(s_max, (BQ, LANES)))
            m_wide = _lane_tile(m_new, BK)
            pvs = [jnp.exp(s - m_wide) for s in ss]
            if diag:
                pvs = [jnp.where(keep, pv, 0.0) for keep, pv in zip(keeps, pvs)]
            new = _dot(pvs[0].astype(BF16), v1s[0])
            for pv, v1 in zip(pvs[1:], v1s[1:]):
                new = new + _dot(pv.astype(BF16), v1)
            acc_ref[...] = jnp.exp(m_old - m_new) * acc_ref[...] + new
            m_ref[...] = m_new

        def live(j):
            c_end = cr_ref[0, pl.ds(jnp.maximum(j, 0), 1), :][:, BK - 1:BK]
            return (jnp.max(qkb_ref[...] - c_end - m_ref[...]) > -EXP_ZERO).astype(I32)

        blocks(R * i + R - 1, True)

        def step(carry):
            j, _ = carry
            go_on = live(j - R)
            blocks(j, False)
            return j - R, go_on

        j_end, _ = lax.while_loop(lambda cr: (cr[0] >= 0) & (cr[1] > 0), step, (R * i - 1, live(R * i - 1)))
        js_ref[2 * p + half, i] = j_end + 1
        acc = acc_ref[...]
        denom = jnp.where(hm, pltpu.roll(acc, HEAD_DIM, 1), acc)
        res = jnp.where(hm, acc / denom, 0.0)

        @pl.when(half == 0)
        def _():
            o_ref[...] = res

        @pl.when(half == 1)
        def _():
            o_ref[...] += res

        st_ref[0] = m_ref[...] + jnp.log(denom)

    return pl.pallas_call(
        body, name="fox_attn_fwd", grid=(n_pair, nq, 2),
        in_specs=[pl.BlockSpec((BQ, LANES), lambda p, i, h: (i, 3 * n_pair + p)),
                  pl.BlockSpec((S, LANES), lambda p, i, h: (0, 4 * n_pair + p)),
                  pl.BlockSpec((S, LANES), lambda p, i, h: (0, 5 * n_pair + p)),
                  pl.BlockSpec((BQ, LANES), lambda p, i, h: (i, 0)),
                  pl.BlockSpec((1, S // BK, BK), lambda p, i, h: (2 * p + h, 0, 0)),
                  _full((1, LANES))],
        out_specs=[pl.BlockSpec((BQ, LANES), lambda p, i, h: (i, p)),
                   pl.BlockSpec((1, BQ, LANES), lambda p, i, h: (2 * p + h, i, 0)),
                   pl.BlockSpec(memory_space=pltpu.SMEM)],
        out_shape=[jax.ShapeDtypeStruct((S, n_pair * LANES), F32),
                   jax.ShapeDtypeStruct((2 * n_pair, S, LANES), F32),
                   jax.ShapeDtypeStruct((2 * n_pair, nq), I32)],
        scratch_shapes=[pltpu.VMEM((BQ, LANES), F32), pltpu.VMEM((BQ, LANES), F32), pltpu.VMEM((BQ, BK), F32),
                        pltpu.VMEM((BQ, LANES), F32)],
        compiler_params=_cparams(("arbitrary", "arbitrary", "arbitrary")),
    )(qkv, qkv, qkv, c, c_rows, kmax)


def _fox_bwd(qkv, do, o, st, c, c_rows, js, n_pair):
    S = qkv.shape[0]
    B = FOX_BLOCK
    nq = S // B

    def body(js_ref, q_ref, k_ref, v_ref, do_ref, o_ref, st_ref, c_ref, cr_ref, dq_ref, dk_ref, dv_ref,
             dc_ref, dqa_ref, rs_ref, cb_ref, db_ref):
        p, i, half = pl.program_id(0), pl.program_id(1), pl.program_id(2)

        @pl.when((i == 0) & (half == 0))
        def _():
            dk_ref[...] = jnp.zeros_like(dk_ref)
            dv_ref[...] = jnp.zeros_like(dv_ref)
            dc_ref[...] = jnp.zeros_like(dc_ref)

        hm = _half_mask(half)
        qv = q_ref[...]
        qs = jnp.where(hm, qv, jnp.zeros_like(qv)) * SCALE
        dov = jnp.where(hm, do_ref[...], 0.0)
        dob = dov.astype(BF16)
        qs_t = jnp.transpose(qs.astype(F32)).astype(BF16)
        do_t = jnp.transpose(dov).astype(BF16)
        cb_ref[...] = jnp.broadcast_to(_head_column(c_ref[...], 2 * p + half), (B, LANES)) - st_ref[0]
        db_ref[...] = jnp.broadcast_to(jnp.sum(dov * o_ref[...], axis=1, keepdims=True), (B, LANES))
        row = lax.broadcasted_iota(I32, (B, B), 0)
        col = lax.broadcasted_iota(I32, (B, B), 1)
        dqa_ref[...] = jnp.zeros_like(dqa_ref)
        rs_ref[...] = jnp.zeros_like(rs_ref)

        def block(j, diag):
            off = pl.multiple_of(j * B, B)
            kj = k_ref[pl.ds(off, B), :]
            vj = v_ref[pl.ds(off, B), :]
            pv = jnp.exp(_dot_nt(qs, kj) + (_lane_tile(cb_ref[...], B) - cr_ref[0, pl.ds(j, 1), :]))
            if diag:
                pv = jnp.where(col <= row, pv, 0.0)
            ds = pv * (_dot_nt(dob, vj) - _lane_tile(db_ref[...], B))
            dsb = ds.astype(BF16)
            dqa_ref[...] += _dot(dsb, kj)
            dk_ref[:, pl.ds(off, B)] += _dot(qs_t, dsb)
            dv_ref[:, pl.ds(off, B)] += _dot(do_t, pv.astype(BF16))
            dc_ref[0, half, pl.ds(j, 1), :] -= jnp.sum(ds, axis=0, keepdims=True)
            rs_ref[...] += jnp.sum(ds, axis=1, keepdims=True)

        def step(j, carry):
            block(j, False)
            return carry

        lax.fori_loop(js_ref[2 * p + half, i], i, step, 0)
        block(i, True)
        res = jnp.where(hm, dqa_ref[...] * SCALE, 0.0)

        @pl.when(half == 0)
        def _():
            dq_ref[...] = res

        @pl.when(half == 1)
        def _():
            dq_ref[...] += res

        dc_ref[0, half, pl.ds(i, 1), :] += jnp.transpose(jnp.broadcast_to(rs_ref[...], (B, LANES)))[0:1, :]

    W = n_pair * LANES
    return pl.pallas_call(
        body, name="fox_attn_bwd",
        grid_spec=pltpu.PrefetchScalarGridSpec(
            num_scalar_prefetch=1, grid=(n_pair, nq, 2),
            in_specs=[pl.BlockSpec((B, LANES), lambda p, i, h, js: (i, 3 * n_pair + p)),
                      pl.BlockSpec((S, LANES), lambda p, i, h, js: (0, 4 * n_pair + p)),
                      pl.BlockSpec((S, LANES), lambda p, i, h, js: (0, 5 * n_pair + p)),
                      pl.BlockSpec((B, LANES), lambda p, i, h, js: (i, p)),
                      pl.BlockSpec((B, LANES), lambda p, i, h, js: (i, p)),
                      pl.BlockSpec((1, B, LANES), lambda p, i, h, js: (2 * p + h, i, 0)),
                      pl.BlockSpec((B, LANES), lambda p, i, h, js: (i, 0)),
                      pl.BlockSpec((1, nq, B), lambda p, i, h, js: (2 * p + h, 0, 0))],
            out_specs=[pl.BlockSpec((B, LANES), lambda p, i, h, js: (i, p)),
                       pl.BlockSpec((LANES, S), lambda p, i, h, js: (p, 0)),
                       pl.BlockSpec((LANES, S), lambda p, i, h, js: (p, 0)),
                       pl.BlockSpec((1, 2, nq, B), lambda p, i, h, js: (p, 0, 0, 0))],
            scratch_shapes=[pltpu.VMEM((B, LANES), F32), pltpu.VMEM((B, 1), F32), pltpu.VMEM((B, LANES), F32),
                            pltpu.VMEM((B, LANES), F32)]),
        out_shape=[jax.ShapeDtypeStruct((S, W), F32)] + [jax.ShapeDtypeStruct((W, S), F32)] * 2
        + [jax.ShapeDtypeStruct((n_pair, 2, nq, B), F32)],
        compiler_params=_cparams(("arbitrary", "arbitrary", "arbitrary")),
    )(js, qkv, qkv, qkv, do, o, st, c, c_rows)


def _adam(w, g, m, v):
    m = ADAM_B1 * m + (1.0 - ADAM_B1) * g
    v = ADAM_B2 * v + (1.0 - ADAM_B2) * (g * g)
    m_hat = m / (1.0 - ADAM_B1 ** ADAM_STEP)
    v_hat = v / (1.0 - ADAM_B2 ** ADAM_STEP)
    delta = -ADAM_LR * (m_hat / (jnp.sqrt(v_hat) + ADAM_EPS) + ADAM_WD * w)
    return delta, m, v


def _reduce_adam(landing, w, m, v, name, extra=None):
    R, C = w.shape
    TR = next(t for t in (256, 128, R) if R % t == 0)
    more = [] if extra is None else [extra]

    def body(*refs):
        l_ref, w_ref, m_ref, v_ref = refs[:4]
        g_ref, d_ref, nm_ref, nv_ref = refs[4 + len(more):]
        g = l_ref[0].astype(F32)
        for s in range(1, N_DEV):
            g = g + l_ref[s].astype(F32)
        if more:
            g = g + refs[4][...]
        d, nm, nv = _adam(w_ref[...], g, m_ref[...], v_ref[...])
        g_ref[...] = g
        d_ref[...] = d
        nm_ref[...] = nm
        nv_ref[...] = nv

    blk = pl.BlockSpec((TR, C), lambda i: (i, 0))
    return pl.pallas_call(
        body, name=name, grid=(R // TR,),
        in_specs=[pl.BlockSpec((N_DEV, TR, C), lambda i: (0, i, 0)), blk, blk, blk] + [blk] * len(more),
        out_specs=[blk] * 4,
        out_shape=[jax.ShapeDtypeStruct((R, C), F32)] * 4,
        compiler_params=_cparams(("arbitrary",)),
    )(landing, w, m, v, *more)


def _reduce_adam_small(landing, w, m, v):
    R, C = w.shape

    def body(l_ref, w_ref, m_ref, v_ref, g_ref, d_ref, nm_ref, nv_ref, loss_ref):
        g = l_ref[0]
        for s in range(1, N_DEV):
            g = g + l_ref[s]
        d, nm, nv = _adam(w_ref[...], g, m_ref[...], v_ref[...])
        g_ref[...] = g
        d_ref[...] = d
        nm_ref[...] = nm
        nv_ref[...] = nv
        loss_ref[...] = jnp.broadcast_to(0.5 * jnp.sum(g[7:8, :], axis=1, keepdims=True), (1, LANES))

    return pl.pallas_call(
        body, name="reduce_adam_small",
        out_shape=[jax.ShapeDtypeStruct((R, C), F32)] * 4 + [jax.ShapeDtypeStruct((1, LANES), F32)],
    )(landing, w, m, v)


def _pad_lanes(a, width):
    return jnp.pad(a, ((0, 0), (0, width - a.shape[1])))


def _pack_small(D, n_fox, g_cat, l1g, l1b, l2g, l2b, bf, last, tail):
    return jnp.concatenate([g_cat, l1g, l1b, l2g, l2b, _pad_lanes(bf, D), jnp.zeros((1, D), F32), last, tail],
                           axis=0)


def kernel(x, w_in, b_f, g_sb, g_fox, w_out, ln1_g, ln1_b, ln2_g, ln2_b, w_gate_up, w_down, loss_target, m_w_in, m_b_f, m_g_sb, m_g_fox, m_w_out, m_ln1_g, m_ln1_b, m_ln2_g, m_ln2_b, m_w_gate_up, m_w_down, v_w_in, v_b_f, v_g_sb, v_g_fox, v_w_out, v_ln1_g, v_ln1_b, v_ln2_g, v_ln2_b, v_w_gate_up, v_w_down):
    x2, tgt = x[0], loss_target[0]
    S, D = x2.shape
    W = D // 2
    n_pair = W // LANES
    n_fox = W // HEAD_DIM
    F = w_down.shape[1] * N_DEV

    g_in = _all_gather_two_level(w_in[0].astype(BF16), "w_in_all_gather")
    w_in_full = g_in.transpose(1, 0, 2).reshape(D, -1)
    w_qkv = w_in_full[:, :6 * W]
    w_f = _pad_lanes(w_in_full[:, 6 * W:], LANES)
    gmat = _tri(D, lambda r, c: (r // HEAD_DIM) == (c // HEAD_DIM))
    g_cat = jnp.concatenate([g_sb, g_fox], axis=1)

    qkv, u, c, ksq = _proj_fwd(x2, w_qkv, w_f, _pad_lanes(b_f, LANES), n_fox)
    c_t = c[:, :n_fox].T
    c_rows = c_t.reshape(n_fox, S // FOX_BLOCK, FOX_BLOCK)
    kmax = jnp.sqrt(ksq[0:1]) * NORM_SLACK
    o_sb, st_sb, js_sb, g_out, g_gu, g_dn = _sb_fwd(
        qkv, n_pair, [w_out[0].astype(BF16), w_gate_up[0].astype(BF16), w_down[0].astype(BF16)])
    w_o = g_out.reshape(D, D)
    w_gu = g_gu.transpose(1, 0, 2).reshape(D, 2 * F)
    w_dn = g_dn.reshape(F, D)
    o_fx, st_fx, js_fx = _fox_fwd(qkv, c, c_t.reshape(n_fox, S // FOX_BK, FOX_BK), kmax, n_pair)
    js_fx = js_fx // (FOX_BLOCK // FOX_BK)
    h1, xh1, rs1, on_b, rr = _post_attn_fwd(o_sb, o_fx, x2, g_cat, gmat, w_o, ln1_g, ln1_b)
    gu, act_b, dyp, sm2 = _mlp_fwd(h1, tgt, w_gu, w_dn, ln2_g, ln2_b)

    dgu_b, dh1 = _mlp_bwd(gu, dyp, w_gu.T, w_dn.T)
    gw_gu = _matmul_tn(h1, [dgu_b], "grad_w_gate_up", n_split=2)
    gw_dn = _matmul_tn(act_b, [dyp], "grad_w_down")
    dxa, dmix_b, do_sb, do_fx, sm1, l_gu, l_dn = _post_attn_bwd(
        dh1, xh1, rs1, ln1_g, o_sb, o_fx, rr, g_cat, gmat, w_o.T,
        [gw_gu.reshape(D, N_DEV, -1).transpose(1, 0, 2), gw_dn.reshape(N_DEV, F // N_DEV, D)])
    dq_sb, dk_sb, dv_sb = _sb_bwd(qkv, do_sb, st_sb, js_sb, n_pair)
    dq_fx, dk_fx, dv_fx, dcr = _fox_bwd(qkv, do_fx, o_fx, st_fx, c, c_rows, js_fx, n_pair)
    dc = _pad_lanes(dcr.reshape(n_fox, S).T, LANES)
    pieces = [dq_sb, dk_sb, dv_sb, dq_fx, dk_fx, dv_fx]
    piece_t = (False, False, False, False, True, True)
    gw_qkv = _matmul_tn(x2, pieces, "grad_w_qkv", pad_cols=LANES, b_t=piece_t)[:, :6 * W + n_fox]
    gw_out = _matmul_tn(on_b, [dmix_b], "grad_w_out")
    dx, gw_f, sm0, l_in, l_out = _proj_bwd(
        dxa, x2, pieces, piece_t, dc, u, w_qkv.T, w_f.T, n_fox,
        [gw_qkv.reshape(D, N_DEV, -1).transpose(1, 0, 2), gw_out.reshape(N_DEV, D // N_DEV, D)])
    gw_f = gw_f[:, :n_fox]

    small = _pack_small(D, n_fox, sm1[2:3], sm1[0:1], sm1[1:2], sm2[0:1], sm2[1:2], sm0[0:1, :n_fox],
                        sm2[2:3] * (1.0 / D), gw_f.T)
    (l_small,) = _exchange_call([jnp.broadcast_to(small[None], (N_DEV,) + small.shape)], True, "small_exchange")

    zero = jnp.zeros((1, D), F32)
    pack = lambda gc, a, b_, c_, d_, bf: _pack_small(D, n_fox, gc, a, b_, c_, d_, bf, zero,
                                                     jnp.zeros((n_fox, D), F32))
    r_small = _reduce_adam_small(
        l_small,
        pack(g_cat, ln1_g, ln1_b, ln2_g, ln2_b, b_f),
        pack(jnp.concatenate([m_g_sb, m_g_fox], axis=1), m_ln1_g, m_ln1_b, m_ln2_g, m_ln2_b, m_b_f),
        pack(jnp.concatenate([v_g_sb, v_g_fox], axis=1), v_ln1_g, v_ln1_b, v_ln2_g, v_ln2_b, v_b_f))
    loss = r_small[4][0, 0]
    cols = w_in.shape[2]
    gf_cols = jnp.pad(r_small[0][8:8 + n_fox].T, ((0, 0), (cols - n_fox, 0)))
    extra = jnp.where(_my_index() == N_DEV - 1, gf_cols, 0.0)
    r_in = _reduce_adam(l_in, w_in[0], m_w_in[0], v_w_in[0], "reduce_adam_w_in", extra)
    r_out = _reduce_adam(l_out, w_out[0], m_w_out[0], v_w_out[0], "reduce_adam_w_out")
    r_gu = _reduce_adam(l_gu, w_gate_up[0], m_w_gate_up[0], v_w_gate_up[0], "reduce_adam_w_gate_up")
    r_dn = _reduce_adam(l_dn, w_down[0], m_w_down[0], v_w_down[0], "reduce_adam_w_down")

    def unpack(kind):
        big = [r_in[kind][None], None, None, None, r_out[kind][None], None, None, None, None,
               r_gu[kind][None], r_dn[kind][None]]
        s = r_small[kind]
        big[1] = s[5:6, :n_fox]
        big[2] = s[0:1, :W]
        big[3] = s[0:1, W:]
        big[5], big[6], big[7], big[8] = s[1:2], s[2:3], s[3:4], s[4:5]
        return big

    return (loss, dx[None], *unpack(0), *unpack(1), *unpack(2), *unpack(3))
```

```python
import functools

import jax
import jax.numpy as jnp
from jax import lax
from jax.experimental import pallas as pl
from jax.experimental.pallas import tpu as pltpu

F32 = jnp.float32
BF16 = jnp.bfloat16
I32 = jnp.int32

N_DEV = 8
HEAD_DIM = 64
LANES = 128
SCALE = HEAD_DIM ** -0.5
ALPHA = 2.0 ** 0.25
LN_EPS = 1e-5
RMS_EPS = 1e-6
ADAM_LR, ADAM_B1, ADAM_B2, ADAM_EPS, ADAM_WD, ADAM_STEP = 0.001, 0.9, 0.999, 1e-08, 0.01, 10
NEG_BIG = -1e30
NORM_SLACK = 1.01
EXP_ZERO = 88.5
VMEM_LIMIT = 60 * 1024 * 1024
ROW_TILE = 256
SB_BLOCK = 256
FOX_BLOCK = 512
FOX_BQ, FOX_BK = 512, 256
assert FOX_BQ == FOX_BLOCK and FOX_BLOCK % FOX_BK == 0
MESH = pl.DeviceIdType.MESH


def _cparams(sem):
    return pltpu.CompilerParams(dimension_semantics=sem, vmem_limit_bytes=VMEM_LIMIT)


def _dot(a, b):
    return jnp.dot(a, b, preferred_element_type=F32)


def _dot_nt(a, b):
    return lax.dot_general(a, b, (((1,), (1,)), ((), ())), preferred_element_type=F32)


def _dot_tn(a, b):
    return lax.dot_general(a, b, (((0,), (0,)), ((), ())), preferred_element_type=F32)


def _split2(a):
    hi = a.astype(BF16)
    lo = (a - hi.astype(F32)).astype(BF16)
    return hi, lo


def _split3(a):
    hi = a.astype(BF16)
    r1 = a - hi.astype(F32)
    mid = r1.astype(BF16)
    lo = (r1 - mid.astype(F32)).astype(BF16)
    return hi, mid, lo


def _dot_acc(a, m):
    hi, lo = _split2(a)
    return _dot(jnp.concatenate([hi, lo], axis=1), jnp.concatenate([m, m], axis=0))


def _tri(n, fn):
    r = lax.broadcasted_iota(I32, (n, n), 0)
    c = lax.broadcasted_iota(I32, (n, n), 1)
    return fn(r, c).astype(BF16)


def _full(shape):
    nd = len(shape)
    return pl.BlockSpec(shape, lambda *_: (0,) * nd)


def _peer(k):
    x, y, c = lax.axis_index("x"), lax.axis_index("y"), lax.axis_index("c")
    return (1 - x if k & 4 else x, 1 - y if k & 2 else y, 1 - c if k & 1 else c)


def _my_index():
    return 4 * lax.axis_index("x") + 2 * lax.axis_index("y") + lax.axis_index("c")


class _Exchange:
    def __init__(self, n, scatter):
        self.n, self.scatter = n, scatter

    def sem_shapes(self):
        return [pltpu.SemaphoreType.DMA(((N_DEV - 1) * self.n,)), pltpu.SemaphoreType.DMA(((N_DEV - 1) * self.n,)),
                pltpu.SemaphoreType.DMA((self.n,))]

    def out_shapes(self, arrays):
        if self.scatter:
            return [jax.ShapeDtypeStruct(s.shape, s.dtype) for s in arrays]
        return [jax.ShapeDtypeStruct((N_DEV,) + s.shape, s.dtype) for s in arrays]

    def _copies(self, ins, outs, sems, landing):
        send_sems, recv_sems, local_sems = sems
        me = _my_index()
        src = lambda a, d: ins[a].at[d] if self.scatter else ins[a]
        local = [pltpu.make_async_copy(src(a, me), outs[a].at[me], local_sems.at[a]) for a in range(self.n)]
        remote = [pltpu.make_async_remote_copy(
            src_ref=src(a, me ^ k), dst_ref=outs[a].at[me ^ k if landing else me],
            send_sem=send_sems.at[(k - 1) * self.n + a], recv_sem=recv_sems.at[(k - 1) * self.n + a],
            device_id=_peer(k), device_id_type=MESH) for k in range(1, N_DEV) for a in range(self.n)]
        return local, remote

    def start(self, ins, outs, sems):
        local, sent = self._copies(ins, outs, sems, landing=False)
        for cp in local + sent:
            cp.start()

    def finish(self, ins, outs, sems):
        local, landed = self._copies(ins, outs, sems, landing=True)
        for cp in landed:
            cp.wait_recv()
        for cp in landed:
            cp.wait_send()
        for cp in local:
            cp.wait()


def _all_gather_two_level(shard, name):
    def body(x_ref, out_ref, send_sems, recv_sems, local_sem):
        x, y, c = lax.axis_index("x"), lax.axis_index("y"), lax.axis_index("c")
        me, sibling = (x, y, c), (x, y, 1 - c)
        chips = [(1 - x, y), (x, 1 - y), (1 - x, 1 - y)]

        def slot(px, py, pc):
            return out_ref.at[4 * px + 2 * py + pc]

        def copy(k, block, to, src=None):
            return pltpu.make_async_remote_copy(
                src_ref=slot(*block) if src is None else src, dst_ref=slot(*block),
                send_sem=send_sems.at[k], recv_sem=recv_sems.at[k], device_id=to, device_id_type=MESH)

        mine = pltpu.make_async_copy(x_ref, slot(*me), local_sem)
        mine.start()
        first = [copy(0, me, sibling, src=x_ref)]
        first += [copy(1 + j, me, (*chip, c), src=x_ref) for j, chip in enumerate(chips)]
        for cp in first:
            cp.start()
        passed = [copy(4 + j, (*chip, c), sibling) for j, chip in enumerate(chips)]
        for j, chip in enumerate(chips):
            copy(1 + j, (*chip, c), me).wait_recv()
            passed[j].start()
        copy(0, sibling, me).wait_recv()
        for j, chip in enumerate(chips):
            copy(4 + j, (*chip, 1 - c), me).wait_recv()
        for cp in first + passed:
            cp.wait_send()
        mine.wait()

    any_spec = pl.BlockSpec(memory_space=pl.ANY)
    return pl.pallas_call(
        body, name=name, out_shape=jax.ShapeDtypeStruct((N_DEV,) + shard.shape, shard.dtype),
        in_specs=[any_spec], out_specs=any_spec,
        scratch_shapes=[pltpu.SemaphoreType.DMA((N_DEV - 1,)), pltpu.SemaphoreType.DMA((N_DEV - 1,)),
                        pltpu.SemaphoreType.DMA],
    )(shard)


def _exchange_call(arrays, scatter, name):
    n = len(arrays)
    ex = _Exchange(n, scatter)

    def body(*refs):
        ins, outs, sems = refs[:n], refs[n:2 * n], refs[2 * n:]
        ex.start(ins, outs, sems)
        ex.finish(ins, outs, sems)

    any_spec = pl.BlockSpec(memory_space=pl.ANY)
    return pl.pallas_call(
        body, name=name, out_shape=ex.out_shapes(arrays),
        in_specs=[any_spec] * n, out_specs=[any_spec] * n, scratch_shapes=ex.sem_shapes(),
    )(*arrays)


def _log_sigmoid(u):
    return jnp.minimum(u, 0.0) - jnp.log1p(jnp.exp(-jnp.abs(u)))


def _proj_fwd(x, w_qkv, w_f, bf_pad, n_fox):
    S, D = x.shape
    N = w_qkv.shape[1]
    W = D // 2
    TM = ROW_TILE
    tri = _tri(TM, lambda r, c: c <= r)
    r_ = lax.broadcasted_iota(I32, (W, LANES), 0)
    c_ = lax.broadcasted_iota(I32, (W, LANES), 1)
    head_of = (r_ // HEAD_DIM == c_).astype(BF16)

    def body(x_ref, wq_ref, wf_ref, bf_ref, tri_ref, ho_ref, qkv_ref, u_ref, c_ref, ksq_ref, run_ref):
        @pl.when(pl.program_id(0) == 0)
        def _():
            run_ref[...] = jnp.zeros_like(run_ref)
            ksq_ref[...] = jnp.zeros_like(ksq_ref)

        xb = x_ref[...].astype(BF16)
        for n0 in range(0, N, D):
            chunk = _dot(xb, wq_ref[:, n0:n0 + D]).astype(BF16)
            qkv_ref[:, n0:n0 + D] = chunk
            if n0 == 4 * W:
                kf = chunk[:, :W].astype(F32)
                ksq = jnp.max(_dot_acc(kf * kf, ho_ref[...]), axis=0, keepdims=True)
                ksq_ref[...] = jnp.maximum(ksq_ref[...], ksq)
        u = _dot(xb, wf_ref[...]) + bf_ref[...]
        lane = lax.broadcasted_iota(I32, u.shape, 1)
        logf = jnp.where(lane < n_fox, _log_sigmoid(u), 0.0)
        u_ref[...] = u
        hi, mid, lo = _split3(logf)
        t = tri_ref[...]
        cs = _dot(t, hi) + _dot(t, mid) + _dot(t, lo) + run_ref[...]
        c_ref[...] = cs
        run_ref[...] = cs[TM - 1:TM, :]

    return pl.pallas_call(
        body, name="proj_fwd", grid=(S // TM,),
        in_specs=[pl.BlockSpec((TM, D), lambda i: (i, 0)), _full(w_qkv.shape), _full(w_f.shape),
                  _full(bf_pad.shape), _full(tri.shape), _full(head_of.shape)],
        out_specs=[pl.BlockSpec((TM, N), lambda i: (i, 0)), pl.BlockSpec((TM, LANES), lambda i: (i, 0)),
                   pl.BlockSpec((TM, LANES), lambda i: (i, 0)), _full((8, LANES))],
        out_shape=[jax.ShapeDtypeStruct((S, N), BF16), jax.ShapeDtypeStruct((S, LANES), F32),
                   jax.ShapeDtypeStruct((S, LANES), F32), jax.ShapeDtypeStruct((8, LANES), F32)],
        scratch_shapes=[pltpu.VMEM((1, LANES), F32)],
        compiler_params=_cparams(("arbitrary",)),
    )(x, w_qkv, w_f, bf_pad, tri, head_of)


def _post_attn_fwd(o_sb, o_fx, x, g_cat, gmat, w_out, ln_g, ln_b):
    S, D = x.shape
    H = D // 2
    TM = ROW_TILE

    def body(osb_ref, ofx_ref, x_ref, g_ref, gm_ref, wo_ref, lg_ref, lb_ref,
             h1_ref, xh_ref, rs_ref, on_ref, rr_ref):
        o = jnp.concatenate([osb_ref[...], ofx_ref[...]], axis=1)
        ms = _dot_acc(o * o, gm_ref[...]) * (1.0 / HEAD_DIM)
        r = lax.rsqrt(ms + RMS_EPS)
        onb = (o * r * g_ref[...]).astype(BF16)
        hp = ALPHA * x_ref[...] + _dot(onb, wo_ref[...])
        mu = jnp.mean(hp, axis=-1, keepdims=True)
        d = hp - mu
        rstd = lax.rsqrt(jnp.mean(d * d, axis=-1, keepdims=True) + LN_EPS)
        xh = d * rstd
        h1_ref[...] = xh * lg_ref[...] + lb_ref[...]
        xh_ref[...] = xh
        rs_ref[...] = jnp.broadcast_to(rstd, (TM, LANES))
        on_ref[...] = onb
        rr_ref[...] = r

    row = lambda w: pl.BlockSpec((TM, w), lambda i: (i, 0))
    return pl.pallas_call(
        body, name="post_attn_fwd", grid=(S // TM,),
        in_specs=[row(H), row(H), row(D), _full((1, D)), _full((D, D)), _full((D, D)), _full((1, D)), _full((1, D))],
        out_specs=[row(D), row(D), row(LANES), row(D), row(D)],
        out_shape=[jax.ShapeDtypeStruct((S, D), F32), jax.ShapeDtypeStruct((S, D), F32),
                   jax.ShapeDtypeStruct((S, LANES), F32), jax.ShapeDtypeStruct((S, D), BF16),
                   jax.ShapeDtypeStruct((S, D), F32)],
        compiler_params=_cparams(("arbitrary",)),
    )(o_sb, o_fx, x, g_cat, gmat, w_out, ln_g, ln_b)


def _ln_bwd(dxh, xh, rstd):
    m1 = jnp.mean(dxh, axis=-1, keepdims=True)
    m2 = jnp.mean(dxh * xh, axis=-1, keepdims=True)
    return rstd * (dxh - m1 - xh * m2)


def _mlp_fwd(h1, target, w_gu, w_dn, ln_g, ln_b):
    S, D = h1.shape
    F = w_dn.shape[0]
    TM = ROW_TILE
    FC = F // 2

    def body(h1_ref, tg_ref, wgu_hbm, wdn_hbm, lg_ref, lb_ref, gu_ref, act_ref, dyp_ref, sm_ref, wgu, wdn):
        @pl.when(pl.program_id(0) == 0)
        def _():
            pltpu.sync_copy(wgu_hbm, wgu)
            pltpu.sync_copy(wdn_hbm, wdn)
            sm_ref[...] = jnp.zeros_like(sm_ref)

        h1v = h1_ref[...]
        hb = h1v.astype(BF16)
        ff = jnp.zeros((TM, D), F32)
        for c0 in range(0, F, FC):
            g = _dot(hb, wgu[:, c0:c0 + FC])
            u = _dot(hb, wgu[:, F + c0:F + c0 + FC])
            gu_ref[:, c0:c0 + FC] = g
            gu_ref[:, F + c0:F + c0 + FC] = u
            ab = ((g * jax.nn.sigmoid(g)) * u).astype(BF16)
            act_ref[:, c0:c0 + FC] = ab
            ff = ff + _dot(ab, wdn[c0:c0 + FC, :])
        yp = ALPHA * h1v + ff
        mu = jnp.mean(yp, axis=-1, keepdims=True)
        d = yp - mu
        rstd = lax.rsqrt(jnp.mean(d * d, axis=-1, keepdims=True) + LN_EPS)
        xh = d * rstd
        err = (xh * lg_ref[...] + lb_ref[...]) - tg_ref[...]
        dy = err * (1.0 / D)
        sm_ref[0:1, :] += jnp.sum(dy * xh, axis=0, keepdims=True)
        sm_ref[1:2, :] += jnp.sum(dy, axis=0, keepdims=True)
        sm_ref[2:3, :] += jnp.sum(err * err, axis=0, keepdims=True)
        dyp_ref[...] = _ln_bwd(dy * lg_ref[...], xh, rstd)

    row = lambda w: pl.BlockSpec((TM, w), lambda i: (i, 0))
    hbm = pl.BlockSpec(memory_space=pl.ANY)
    return pl.pallas_call(
        body, name="mlp_fwd", grid=(S // TM,),
        in_specs=[row(D), row(D), hbm, hbm, _full((1, D)), _full((1, D))],
        out_specs=[row(2 * F), row(F), row(D), _full((8, D))],
        out_shape=[jax.ShapeDtypeStruct((S, 2 * F), F32), jax.ShapeDtypeStruct((S, F), BF16),
                   jax.ShapeDtypeStruct((S, D), F32), jax.ShapeDtypeStruct((8, D), F32)],
        scratch_shapes=[pltpu.VMEM(w_gu.shape, BF16), pltpu.VMEM(w_dn.shape, BF16)],
        compiler_params=_cparams(("arbitrary",)),
    )(h1, target, w_gu, w_dn, ln_g, ln_b)


def _mlp_bwd(gu, dyp, w_guT, w_dnT):
    S, D = dyp.shape
    F = w_dnT.shape[1]
    TM = ROW_TILE
    FC = F // 2

    def body(gu_ref, dyp_ref, wguT_hbm, wdnT_hbm, dgu_ref, dh1_ref, wguT, wdnT):
        @pl.when(pl.program_id(0) == 0)
        def _():
            pltpu.sync_copy(wguT_hbm, wguT)
            pltpu.sync_copy(wdnT_hbm, wdnT)

        dypv = dyp_ref[...]
        db = dypv.astype(BF16)
        dh1 = ALPHA * dypv
        for c0 in range(0, F, FC):
            dact = _dot(db, wdnT[:, c0:c0 + FC])
            g = gu_ref[:, c0:c0 + FC]
            u = gu_ref[:, F + c0:F + c0 + FC]
            sg = jax.nn.sigmoid(g)
            dgb = (dact * u * (sg * (1.0 + g * (1.0 - sg)))).astype(BF16)
            dub = (dact * (g * sg)).astype(BF16)
            dgu_ref[:, c0:c0 + FC] = dgb
            dgu_ref[:, F + c0:F + c0 + FC] = dub
            dh1 = dh1 + _dot(dgb, wguT[c0:c0 + FC, :]) + _dot(dub, wguT[F + c0:F + c0 + FC, :])
        dh1_ref[...] = dh1

    row = lambda w: pl.BlockSpec((TM, w), lambda i: (i, 0))
    hbm = pl.BlockSpec(memory_space=pl.ANY)
    return pl.pallas_call(
        body, name="mlp_bwd", grid=(S // TM,),
        in_specs=[row(2 * F), row(D), hbm, hbm],
        out_specs=[row(2 * F), row(D)],
        out_shape=[jax.ShapeDtypeStruct((S, 2 * F), BF16), jax.ShapeDtypeStruct((S, D), F32)],
        scratch_shapes=[pltpu.VMEM(w_guT.shape, BF16), pltpu.VMEM(w_dnT.shape, BF16)],
        compiler_params=_cparams(("arbitrary",)),
    )(gu, dyp, w_guT, w_dnT)


def _post_attn_bwd(dh1, xh, rs, ln_g, o_sb, o_fx, rr, g_cat, gmat, w_outT, chunked):
    S, D = dh1.shape
    H = D // 2
    TM = ROW_TILE
    nT = S // TM
    n_ch = len(chunked)
    ex = _Exchange(n_ch, scatter=True)

    def body(*refs):
        dh1_ref, xh_ref, rs_ref, lg_ref, osb_ref, ofx_ref, rr_ref, g_ref, gm_ref, woT_ref = refs[:10]
        ch_in = refs[10:10 + n_ch]
        dxa_ref, dmix_ref, dosb_ref, dofx_ref, sm_ref = refs[10 + n_ch:15 + n_ch]
        ch_out = refs[15 + n_ch:15 + 2 * n_ch]
        sems = refs[15 + 2 * n_ch:]

        @pl.when(pl.program_id(0) == 0)
        def _():
            ex.start(ch_in, ch_out, sems)
            sm_ref[...] = jnp.zeros_like(sm_ref)

        dh = dh1_ref[...]
        xhv = xh_ref[...]
        dhp = _ln_bwd(dh * lg_ref[...], xhv, rs_ref[:, 0:1])
        dxa_ref[...] = ALPHA * dhp
        dmb = dhp.astype(BF16)
        dmix_ref[...] = dmb
        don = _dot(dmb, woT_ref[...])
        o = jnp.concatenate([osb_ref[...], ofx_ref[...]], axis=1)
        r = rr_ref[...]
        u = don * g_ref[...]
        t = _dot_acc(u * o, gm_ref[...]) * (1.0 / HEAD_DIM)
        do = r * u - o * (r * r * r) * t
        dosb_ref[...] = do[:, :H]
        dofx_ref[...] = do[:, H:]
        sm_ref[0:1, :] += jnp.sum(dh * xhv, axis=0, keepdims=True)
        sm_ref[1:2, :] += jnp.sum(dh, axis=0, keepdims=True)
        sm_ref[2:3, :] += jnp.sum(don * o * r, axis=0, keepdims=True)

        @pl.when(pl.program_id(0) == nT - 1)
        def _():
            ex.finish(ch_in, ch_out, sems)

    row = lambda w: pl.BlockSpec((TM, w), lambda i: (i, 0))
    any_spec = pl.BlockSpec(memory_space=pl.ANY)
    return pl.pallas_call(
        body, name="post_attn_bwd", grid=(nT,),
        in_specs=[row(D), row(D), row(LANES), _full((1, D)), row(H), row(H), row(D), _full((1, D)),
                  _full((D, D)), _full((D, D))] + [any_spec] * n_ch,
        out_specs=[row(D), row(D), row(H), row(H), _full((8, D))] + [any_spec] * n_ch,
        out_shape=[jax.ShapeDtypeStruct((S, D), F32), jax.ShapeDtypeStruct((S, D), BF16),
                   jax.ShapeDtypeStruct((S, H), F32), jax.ShapeDtypeStruct((S, H), F32),
                   jax.ShapeDtypeStruct((8, D), F32)] + ex.out_shapes(chunked),
        scratch_shapes=ex.sem_shapes(),
        compiler_params=_cparams(("arbitrary",)),
    )(dh1, xh, rs, ln_g, o_sb, o_fx, rr, g_cat, gmat, w_outT, *chunked)


def _proj_bwd(dxa, x, pieces, piece_t, dc, u, w_qkvT, w_fT, n_fox, chunked):
    S, D = dxa.shape
    H = D // 2
    TM = ROW_TILE
    nT = S // TM
    tri = _tri(TM, lambda r, c: c >= r)
    n_p = len(pieces)
    n_ch = len(chunked)
    ex = _Exchange(n_ch, scatter=True)

    def body(*refs):
        dxa_ref, x_ref = refs[:2]
        p_refs = refs[2:2 + n_p]
        dc_ref, u_ref, wq_ref, wf_ref, tri_ref = refs[2 + n_p:7 + n_p]
        ch_in = refs[7 + n_p:7 + n_p + n_ch]
        dx_ref, gwf_ref, sm_ref = refs[7 + n_p + n_ch:10 + n_p + n_ch]
        ch_out = refs[10 + n_p + n_ch:10 + n_p + 2 * n_ch]
        run_ref = refs[10 + n_p + 2 * n_ch]
        sems = refs[11 + n_p + 2 * n_ch:]

        @pl.when(pl.program_id(0) == 0)
        def _():
            ex.start(ch_in, ch_out, sems)
            run_ref[...] = jnp.zeros_like(run_ref)
            sm_ref[...] = jnp.zeros_like(sm_ref)
            gwf_ref[...] = jnp.zeros_like(gwf_ref)

        hi, mid, lo = _split3(dc_ref[...])
        t = tri_ref[...]
        dlogf = _dot(t, hi) + _dot(t, mid) + _dot(t, lo) + run_ref[...]
        run_ref[...] = dlogf[0:1, :]
        uv = u_ref[...]
        lane = lax.broadcasted_iota(I32, uv.shape, 1)
        df = jnp.where(lane < n_fox, dlogf * jax.nn.sigmoid(-uv), 0.0)
        sm_ref[0:1, :] += jnp.sum(df, axis=0, keepdims=True)
        dfb = df.astype(BF16)
        gwf_ref[...] += _dot_tn(x_ref[...].astype(BF16), dfb)
        acc = dxa_ref[...] + _dot(dfb, wf_ref[...])
        for a in range(n_p):
            mm = _dot_tn if piece_t[a] else _dot
            acc = acc + mm(p_refs[a][...].astype(BF16), wq_ref[a * H:(a + 1) * H, :])
        dx_ref[...] = acc

        @pl.when(pl.program_id(0) == nT - 1)
        def _():
            ex.finish(ch_in, ch_out, sems)

    rev = lambda w: pl.BlockSpec((TM, w), lambda i: (nT - 1 - i, 0))
    any_spec = pl.BlockSpec(memory_space=pl.ANY)
    return pl.pallas_call(
        body, name="proj_bwd", grid=(nT,),
        in_specs=[rev(D), rev(D)]
        + [pl.BlockSpec((H, TM), lambda i: (0, nT - 1 - i)) if t else rev(H) for t in piece_t]
        + [rev(LANES), rev(LANES), _full(w_qkvT.shape), _full(w_fT.shape), _full(tri.shape)] + [any_spec] * n_ch,
        out_specs=[rev(D), _full((D, LANES)), _full((8, LANES))] + [any_spec] * n_ch,
        out_shape=[jax.ShapeDtypeStruct((S, D), F32), jax.ShapeDtypeStruct((D, LANES), F32),
                   jax.ShapeDtypeStruct((8, LANES), F32)] + ex.out_shapes(chunked),
        scratch_shapes=[pltpu.VMEM((1, LANES), F32)] + ex.sem_shapes(),
        compiler_params=_cparams(("arbitrary",)),
    )(dxa, x, *pieces, dc, u, w_qkvT, w_fT, tri, *chunked)


def _matmul_tn(a, bs, name, n_split=1, pad_cols=0, b_t=None):
    S, M = a.shape
    b_t = tuple(b_t) if b_t is not None else (False,) * len(bs)
    widths = [b.shape[0] if t else b.shape[1] for b, t in zip(bs, b_t)]
    N = sum(widths) + pad_cols
    assert n_split == 1 or (len(bs) == 1 and pad_cols == 0 and not b_t[0])
    TK = 512 if S % 512 == 0 else ROW_TILE
    MC = 512 if M % 512 == 0 else 256
    nb = len(bs)

    def body(*refs):
        a_ref, b_refs, o_ref, acc_ref = refs[0], refs[1:1 + nb], refs[1 + nb], refs[2 + nb]
        t_refs = list(refs[3 + nb:])

        @pl.when(pl.program_id(1) == 0)
        def _():
            acc_ref[...] = jnp.zeros_like(acc_ref)
            for t_ref in t_refs:
                t_ref[...] = jnp.zeros_like(t_ref)

        n0 = 0
        turned = []
        for b_ref, t, w in zip(b_refs, b_t, widths):
            bv = b_ref[...].astype(BF16)
            if t:
                t_ref = t_refs[len(turned)]
                t_ref[...] += _dot(bv, a_ref[...].astype(BF16))
                turned.append((t_ref, n0, w))
            else:
                for m0 in range(0, M, MC):
                    acc_ref[m0:m0 + MC, n0:n0 + w] += _dot_tn(a_ref[:, m0:m0 + MC].astype(BF16), bv)
            n0 += w

        @pl.when(pl.program_id(1) == S // TK - 1)
        def _():
            for t_ref, c0, w in turned:
                acc_ref[:, c0:c0 + w] = jnp.transpose(t_ref[...])
            o_ref[...] = acc_ref[...].astype(BF16)

    return pl.pallas_call(
        body, name=name, grid=(n_split, S // TK),
        in_specs=[pl.BlockSpec((TK, M), lambda n, k: (k, 0))]
        + [pl.BlockSpec((w, TK), lambda n, k: (0, k)) if t else pl.BlockSpec((TK, w // n_split), lambda n, k: (k, n))
           for w, t in zip(widths, b_t)],
        out_specs=pl.BlockSpec((M, N // n_split), lambda n, k: (0, n)),
        out_shape=jax.ShapeDtypeStruct((M, N), BF16),
        scratch_shapes=[pltpu.VMEM((M, N // n_split), F32)]
        + [pltpu.VMEM((w, M), F32) for w, t in zip(widths, b_t) if t],
        compiler_params=_cparams(("arbitrary", "arbitrary")),
    )(a, *bs)


def _half_mask(half):
    lane = lax.broadcasted_iota(I32, (1, LANES), 1)
    return (lane >= half * HEAD_DIM) & (lane < half * HEAD_DIM + HEAD_DIM)


def _lane_tile(a, width):
    return jnp.concatenate([a] * (width // LANES), axis=1)


def _softplus_neg_abs(z):
    return jnp.log(1.0 + jnp.exp(-jnp.abs(z)))


def _sb_fwd(qkv, n_pair, shards):
    S = qkv.shape[0]
    B = SB_BLOCK
    nq = S // B
    us = _tri(B, lambda r, c: r > c)
    n_sh = len(shards)
    ex = _Exchange(n_sh, scatter=False)

    def body(*refs):
        q_ref, k_ref, v_ref, us_ref = refs[:4]
        sh_in = refs[4:4 + n_sh]
        o_ref, st_ref, js_ref = refs[4 + n_sh:7 + n_sh]
        sh_out = refs[7 + n_sh:7 + 2 * n_sh]
        acc_ref, r_ref = refs[7 + 2 * n_sh:9 + 2 * n_sh]
        sems = refs[9 + 2 * n_sh:]
        p, i = pl.program_id(0), pl.program_id(1)

        @pl.when((p == 0) & (i == 0))
        def _():
            ex.start(sh_in, sh_out, sems)

        hms = [_half_mask(h) for h in range(2)]
        qv = q_ref[...]
        qss = [jnp.where(hm, qv, jnp.zeros_like(qv)) * SCALE for hm in hms]
        row = lax.broadcasted_iota(I32, (B, B), 0)
        col = lax.broadcasted_iota(I32, (B, B), 1)
        tri = col < row
        acc_ref[...] = jnp.zeros_like(acc_ref)
        r_ref[...] = jnp.zeros_like(r_ref)

        def block(j, diag):
            off = pl.multiple_of(j * B, B)
            kj = k_ref[pl.ds(off, B), :]
            vj = v_ref[pl.ds(off, B), :]
            zs = [_dot_nt(qss[h], kj) for h in range(2)]
            sps = [_softplus_neg_abs(z) for z in zs]
            bs = [jnp.minimum(-z, 0.0) - sp for z, sp in zip(zs, sps)]
            if diag:
                bs = [jnp.where(tri, b, 0.0) for b in bs]
            lexcs = [_dot_acc(b, us_ref[...]) for b in bs]
            ws = [jnp.exp(jnp.minimum(z, 0.0) - sp + (_lane_tile(r_ref[h], B) + lexc))
                  for h, (z, sp, lexc) in enumerate(zip(zs, sps, lexcs))]
            if diag:
                ws = [jnp.where(tri, w, 0.0) for w in ws]
            for h in range(2):
                acc_ref[h] += _dot(ws[h].astype(BF16), vj)
                r_ref[h] += jnp.broadcast_to(lexcs[h][:, 0:1] + bs[h][:, 0:1], (B, LANES))

        def live():
            return (jnp.max(r_ref[...]) > -EXP_ZERO).astype(I32)

        def first_two():
            has_prev = (i > 0).astype(F32)
            off_d = pl.multiple_of(i * B, B)
            off_o = pl.multiple_of(jnp.maximum(i - 1, 0) * B, B)
            k_d, v_d = k_ref[pl.ds(off_d, B), :], v_ref[pl.ds(off_d, B), :]
            k_o, v_o = k_ref[pl.ds(off_o, B), :], v_ref[pl.ds(off_o, B), :]
            hs = range(2)
            z_d = [_dot_nt(qss[h], k_d) for h in hs]
            z_o = [_dot_nt(qss[h], k_o) for h in hs]
            sp_d = [_softplus_neg_abs(z) for z in z_d]
            sp_o = [_softplus_neg_abs(z) for z in z_o]
            b_d = [jnp.where(tri, jnp.minimum(-z, 0.0) - sp, 0.0) for z, sp in zip(z_d, sp_d)]
            b_o = [(jnp.minimum(-z, 0.0) - sp) * has_prev for z, sp in zip(z_o, sp_o)]
            l_d = [_dot_acc(b, us_ref[...]) for b in b_d]
            l_o = [_dot_acc(b, us_ref[...]) for b in b_o]
            r_d = [jnp.broadcast_to(l[:, 0:1] + b[:, 0:1], (B, LANES)) for l, b in zip(l_d, b_d)]
            w_d = [jnp.where(tri, jnp.exp(jnp.minimum(z, 0.0) - sp + l), 0.0) for z, sp, l in zip(z_d, sp_d, l_d)]
            w_o = [jnp.exp(jnp.minimum(z, 0.0) - sp + (_lane_tile(r, B) + l)) * has_prev
                   for z, sp, l, r in zip(z_o, sp_o, l_o, r_d)]
            for h in hs:
                acc_ref[h] = _dot(w_d[h].astype(BF16), v_d) + _dot(w_o[h].astype(BF16), v_o)
                r_ref[h] = r_d[h] + jnp.broadcast_to(l_o[h][:, 0:1] + b_o[h][:, 0:1], (B, LANES))

        first_two()

        def step(carry):
            j, _ = carry
            block(j, False)
            return j - 1, live()

        j_end, _ = lax.while_loop(lambda c: (c[0] >= 0) & (c[1] > 0), step, (i - 2, live()))
        js = jnp.maximum(j_end + 1, 0)
        js_ref[2 * p, i] = js
        js_ref[2 * p + 1, i] = js
        o_ref[...] = jnp.where(hms[0], acc_ref[0], acc_ref[1])
        st_ref[...] = r_ref[...]

        @pl.when((p == n_pair - 1) & (i == nq - 1))
        def _():
            ex.finish(sh_in, sh_out, sems)

    any_spec = pl.BlockSpec(memory_space=pl.ANY)
    return pl.pallas_call(
        body, name="sb_attn_fwd", grid=(n_pair, nq),
        in_specs=[pl.BlockSpec((B, LANES), lambda p, i: (i, p)),
                  pl.BlockSpec((S, LANES), lambda p, i: (0, n_pair + p)),
                  pl.BlockSpec((S, LANES), lambda p, i: (0, 2 * n_pair + p)),
                  _full((B, B))] + [any_spec] * n_sh,
        out_specs=[pl.BlockSpec((B, LANES), lambda p, i: (i, p)),
                   pl.BlockSpec((2, B, LANES), lambda p, i: (p, i, 0)),
                   pl.BlockSpec(memory_space=pltpu.SMEM)] + [any_spec] * n_sh,
        out_shape=[jax.ShapeDtypeStruct((S, n_pair * LANES), F32),
                   jax.ShapeDtypeStruct((2 * n_pair, S, LANES), F32),
                   jax.ShapeDtypeStruct((2 * n_pair, nq), I32)] + ex.out_shapes(shards),
        scratch_shapes=[pltpu.VMEM((2, B, LANES), F32), pltpu.VMEM((2, B, LANES), F32)] + ex.sem_shapes(),
        compiler_params=_cparams(("arbitrary", "arbitrary")),
    )(qkv, qkv, qkv, us, *shards)


def _sb_bwd(qkv, do, st, js, n_pair):
    S = qkv.shape[0]
    B = SB_BLOCK
    nq = S // B
    us = _tri(B, lambda r, c: r > c)
    ti = _tri(B, lambda r, c: r <= c)

    def body(js_ref, q_ref, k_ref, v_ref, do_ref, st_ref, us_ref, ti_ref, dq_ref, dk_ref, dv_ref,
             dqa_ref, pr_ref, er_ref):
        p, i = pl.program_id(0), pl.program_id(1)

        @pl.when(i == 0)
        def _():
            dk_ref[...] = jnp.zeros_like(dk_ref)
            dv_ref[...] = jnp.zeros_like(dv_ref)

        hms = [_half_mask(h) for h in range(2)]
        qv = q_ref[...]
        dov = do_ref[...]
        qss = [jnp.where(hm, qv, jnp.zeros_like(qv)) * SCALE for hm in hms]
        dobs = [jnp.where(hm, dov, 0.0).astype(BF16) for hm in hms]
        row = lax.broadcasted_iota(I32, (B, B), 0)
        col = lax.broadcasted_iota(I32, (B, B), 1)
        tri = col < row
        dqa_ref[...] = jnp.zeros_like(dqa_ref)
        pr_ref[...] = jnp.zeros_like(pr_ref)
        er_ref[...] = jnp.zeros_like(er_ref)

        def block(j, diag):
            off = pl.multiple_of(j * B, B)
            kj = k_ref[pl.ds(off, B), :]
            vj = v_ref[pl.ds(off, B), :]
            hs = range(2)
            zs = [_dot_nt(qss[h], kj) for h in hs]
            dws = [_dot_nt(dobs[h], vj) for h in hs]
            sps = [_softplus_neg_abs(z) for z in zs]
            bs = [jnp.minimum(-z, 0.0) - sp for z, sp in zip(zs, sps)]
            if diag:
                bs = [jnp.where(tri, b, 0.0) for b in bs]
            lexcs = [_dot_acc(b, us_ref[...]) for b in bs]
            ws = []
            for h in hs:
                pr_new = pr_ref[h] + jnp.broadcast_to(lexcs[h][:, 0:1] + bs[h][:, 0:1], (B, LANES))
                pr_ref[h] = pr_new
                w = jnp.exp(jnp.minimum(zs[h], 0.0) - sps[h] + (_lane_tile(st_ref[h] - pr_new, B) + lexcs[h]))
                ws.append(jnp.where(tri, w, 0.0) if diag else w)
            es = [dw * w for dw, w in zip(dws, ws)]
            eincs = [_dot_acc(e, ti_ref[...]) for e in es]
            dzbs = []
            for h in hs:
                er = er_ref[h]
                big_e = _lane_tile(er, B) + (eincs[h] - es[h])
                er_ref[h] = er + jnp.broadcast_to(eincs[h][:, B - 1:B], (B, LANES))
                eb = jnp.exp(bs[h])
                dzbs.append((es[h] * eb - big_e * (1.0 - eb)).astype(BF16))
            for h in hs:
                dqa_ref[h] += _dot(dzbs[h], kj)
            dk_ref[pl.ds(off, B), :] += _dot_tn(dzbs[0], qss[0]) + _dot_tn(dzbs[1], qss[1])
            dv_ref[pl.ds(off, B), :] += (_dot_tn(ws[0].astype(BF16), dobs[0])
                                         + _dot_tn(ws[1].astype(BF16), dobs[1]))

        def step(j, carry):
            block(j, False)
            return carry

        def last_two():
            has_prev = (i > 0).astype(F32)
            off_d = pl.multiple_of(i * B, B)
            off_o = pl.multiple_of(jnp.maximum(i - 1, 0) * B, B)
            k_d, v_d = k_ref[pl.ds(off_d, B), :], v_ref[pl.ds(off_d, B), :]
            k_o, v_o = k_ref[pl.ds(off_o, B), :], v_ref[pl.ds(off_o, B), :]
            hs = range(2)
            z_o = [_dot_nt(qss[h], k_o) for h in hs]
            z_d = [_dot_nt(qss[h], k_d) for h in hs]
            dw_o = [_dot_nt(dobs[h], v_o) for h in hs]
            dw_d = [_dot_nt(dobs[h], v_d) for h in hs]
            sp_o = [_softplus_neg_abs(z) for z in z_o]
            sp_d = [_softplus_neg_abs(z) for z in z_d]
            b_o = [(jnp.minimum(-z, 0.0) - sp) * has_prev for z, sp in zip(z_o, sp_o)]
            b_d = [jnp.where(tri, jnp.minimum(-z, 0.0) - sp, 0.0) for z, sp in zip(z_d, sp_d)]
            l_o = [_dot_acc(b, us_ref[...]) for b in b_o]
            l_d = [_dot_acc(b, us_ref[...]) for b in b_d]
            w_o, w_d = [], []
            for h in hs:
                pr1 = pr_ref[h] + jnp.broadcast_to(l_o[h][:, 0:1] + b_o[h][:, 0:1], (B, LANES))
                pr2 = pr1 + jnp.broadcast_to(l_d[h][:, 0:1] + b_d[h][:, 0:1], (B, LANES))
                st = st_ref[h]
                w_o.append(jnp.exp(jnp.minimum(z_o[h], 0.0) - sp_o[h] + (_lane_tile(st - pr1, B) + l_o[h])) * has_prev)
                w_d.append(jnp.where(tri, jnp.exp(jnp.minimum(z_d[h], 0.0) - sp_d[h]
                                                  + (_lane_tile(st - pr2, B) + l_d[h])), 0.0))
            e_o = [dw * w for dw, w in zip(dw_o, w_o)]
            e_d = [dw * w for dw, w in zip(dw_d, w_d)]
            ei_o = [_dot_acc(e, ti_ref[...]) for e in e_o]
            ei_d = [_dot_acc(e, ti_ref[...]) for e in e_d]
            dz_o, dz_d = [], []
            for h in hs:
                er = er_ref[h]
                big_o = _lane_tile(er, B) + (ei_o[h] - e_o[h])
                er1 = er + jnp.broadcast_to(ei_o[h][:, B - 1:B], (B, LANES))
                big_d = _lane_tile(er1, B) + (ei_d[h] - e_d[h])
                eb_o, eb_d = jnp.exp(b_o[h]), jnp.exp(b_d[h])
                dz_o.append((e_o[h] * eb_o - big_o * (1.0 - eb_o)).astype(BF16))
                dz_d.append((e_d[h] * eb_d - big_d * (1.0 - eb_d)).astype(BF16))
            for h in hs:
                dqa_ref[h] += _dot(dz_o[h], k_o) + _dot(dz_d[h], k_d)
            dk_ref[pl.ds(off_o, B), :] += _dot_tn(dz_o[0], qss[0]) + _dot_tn(dz_o[1], qss[1])
            dv_ref[pl.ds(off_o, B), :] += (_dot_tn(w_o[0].astype(BF16), dobs[0])
                                           + _dot_tn(w_o[1].astype(BF16), dobs[1]))
            dk_ref[pl.ds(off_d, B), :] += _dot_tn(dz_d[0], qss[0]) + _dot_tn(dz_d[1], qss[1])
            dv_ref[pl.ds(off_d, B), :] += (_dot_tn(w_d[0].astype(BF16), dobs[0])
                                           + _dot_tn(w_d[1].astype(BF16), dobs[1]))

        lax.fori_loop(js_ref[2 * p, i], i - 1, step, 0)
        last_two()
        dq_ref[...] = jnp.where(hms[0], dqa_ref[0], dqa_ref[1]) * SCALE

    W = n_pair * LANES
    return pl.pallas_call(
        body, name="sb_attn_bwd",
        grid_spec=pltpu.PrefetchScalarGridSpec(
            num_scalar_prefetch=1, grid=(n_pair, nq),
            in_specs=[pl.BlockSpec((B, LANES), lambda p, i, js: (i, p)),
                      pl.BlockSpec((S, LANES), lambda p, i, js: (0, n_pair + p)),
                      pl.BlockSpec((S, LANES), lambda p, i, js: (0, 2 * n_pair + p)),
                      pl.BlockSpec((B, LANES), lambda p, i, js: (i, p)),
                      pl.BlockSpec((2, B, LANES), lambda p, i, js: (p, i, 0)),
                      pl.BlockSpec((B, B), lambda p, i, js: (0, 0)),
                      pl.BlockSpec((B, B), lambda p, i, js: (0, 0))],
            out_specs=[pl.BlockSpec((B, LANES), lambda p, i, js: (i, p)),
                       pl.BlockSpec((S, LANES), lambda p, i, js: (0, p)),
                       pl.BlockSpec((S, LANES), lambda p, i, js: (0, p))],
            scratch_shapes=[pltpu.VMEM((2, B, LANES), F32), pltpu.VMEM((2, B, LANES), F32),
                            pltpu.VMEM((2, B, LANES), F32)]),
        out_shape=[jax.ShapeDtypeStruct((S, W), F32)] * 3,
        compiler_params=_cparams(("arbitrary", "arbitrary")),
    )(js, qkv, qkv, qkv, do, st, us, ti)


def _head_column(blk, head):
    lane = lax.broadcasted_iota(I32, (1, LANES), 1)
    return jnp.sum(jnp.where(lane == head, blk, 0.0), axis=1, keepdims=True)


def _fox_fwd(qkv, c, c_rows, kmax, n_pair):
    S = qkv.shape[0]
    BQ, BK = FOX_BQ, FOX_BK
    R = BQ // BK
    nq = S // BQ

    def body(q_ref, k_ref, v_ref, c_ref, cr_ref, km_ref, o_ref, st_ref, js_ref, acc_ref, m_ref, cb_ref, qkb_ref):
        p, i, half = pl.program_id(0), pl.program_id(1), pl.program_id(2)
        hm = _half_mask(half)
        qv = q_ref[...]
        qs = jnp.where(hm, qv, jnp.zeros_like(qv)) * SCALE
        ccol = _head_column(c_ref[...], 2 * p + half)
        cb_ref[...] = jnp.broadcast_to(ccol, (BQ, BK))
        qf = qs.astype(F32)
        qkb_ref[...] = jnp.broadcast_to(
            jnp.sqrt(jnp.sum(qf * qf, axis=1, keepdims=True)) * NORM_SLACK
            * _head_column(km_ref[...], 2 * p + half) + ccol, (BQ, LANES))
        row = lax.broadcasted_iota(I32, (BQ, BK), 0)
        col = lax.broadcasted_iota(I32, (BQ, BK), 1)
        acc_ref[...] = jnp.zeros_like(acc_ref)
        m_ref[...] = jnp.full_like(m_ref, NEG_BIG)

        def blocks(j_top, diag):
            ss, v1s, keeps = [], [], []
            for d in range(R):
                j = j_top - d
                off = pl.multiple_of(j * BK, BK)
                vj = v_ref[pl.ds(off, BK), :]
                v1s.append(jnp.where(hm, vj, jnp.ones_like(vj)))
                s = _dot_nt(qs, k_ref[pl.ds(off, BK), :]) + (cb_ref[...] - cr_ref[0, pl.ds(j, 1), :])
                if diag:
                    keeps.append(col + (R - 1 - d) * BK <= row)
                    s = jnp.where(keeps[-1], s, NEG_BIG)
                ss.append(s)
            m_old = m_ref[...]
            s_max = jnp.max(functools.reduce(jnp.maximum, ss), axis=1, keepdims=True)
            m_new = jnp.maximum(m_old, jnp.broadcast_to(s_max, (BQ, LANES)))
            m_wide = _lane_tile(m_new, BK)
            pvs = [jnp.exp(s - m_wide) for s in ss]
            if diag:
                pvs = [jnp.where(keep, pv, 0.0) for keep, pv in zip(keeps, pvs)]
            new = _dot(pvs[0].astype(BF16), v1s[0])
            for pv, v1 in zip(pvs[1:], v1s[1:]):
                new = new + _dot(pv.astype(BF16), v1)
            acc_ref[...] = jnp.exp(m_old - m_new) * acc_ref[...] + new
            m_ref[...] = m_new

        def live(j):
            c_end = cr_ref[0, pl.ds(jnp.maximum(j, 0), 1), :][:, BK - 1:BK]
            return (jnp.max(qkb_ref[...] - c_end - m_ref[...]) > -EXP_ZERO).astype(I32)

        blocks(R * i + R - 1, True)

        def step(carry):
            j, _ = carry
            go_on = live(j - R)
            blocks(j, False)
            return j - R, go_on

        j_end, _ = lax.while_loop(lambda cr: (cr[0] >= 0) & (cr[1] > 0), step, (R * i - 1, live(R * i - 1)))
        js_ref[2 * p + half, i] = j_end + 1
        acc = acc_ref[...]
        denom = jnp.where(hm, pltpu.roll(acc, HEAD_DIM, 1), acc)
        res = jnp.where(hm, acc / denom, 0.0)

        @pl.when(half == 0)
        def _():
            o_ref[...] = res

        @pl.when(half == 1)
        def _():
            o_ref[...] += res

        st_ref[0] = m_ref[...] + jnp.log(denom)

    return pl.pallas_call(
        body, name="fox_attn_fwd", grid=(n_pair, nq, 2),
        in_specs=[pl.BlockSpec((BQ, LANES), lambda p, i, h: (i, 3 * n_pair + p)),
                  pl.BlockSpec((S, LANES), lambda p, i, h: (0, 4 * n_pair + p)),
                  pl.BlockSpec((S, LANES), lambda p, i, h: (0, 5 * n_pair + p)),
                  pl.BlockSpec((BQ, LANES), lambda p, i, h: (i, 0)),
                  pl.BlockSpec((1, S // BK, BK), lambda p, i, h: (2 * p + h, 0, 0)),
                  _full((1, LANES))],
        out_specs=[pl.BlockSpec((BQ, LANES), lambda p, i, h: (i, p)),
                   pl.BlockSpec((1, BQ, LANES), lambda p, i, h: (2 * p + h, i, 0)),
                   pl.BlockSpec(memory_space=pltpu.SMEM)],
        out_shape=[jax.ShapeDtypeStruct((S, n_pair * LANES), F32),
                   jax.ShapeDtypeStruct((2 * n_pair, S, LANES), F32),
                   jax.ShapeDtypeStruct((2 * n_pair, nq), I32)],
        scratch_shapes=[pltpu.VMEM((BQ, LANES), F32), pltpu.VMEM((BQ, LANES), F32), pltpu.VMEM((BQ, BK), F32),
                        pltpu.VMEM((BQ, LANES), F32)],
        compiler_params=_cparams(("arbitrary", "arbitrary", "arbitrary")),
    )(qkv, qkv, qkv, c, c_rows, kmax)


def _fox_bwd(qkv, do, o, st, c, c_rows, js, n_pair):
    S = qkv.shape[0]
    B = FOX_BLOCK
    nq = S // B

    def body(js_ref, q_ref, k_ref, v_ref, do_ref, o_ref, st_ref, c_ref, cr_ref, dq_ref, dk_ref, dv_ref,
             dc_ref, dqa_ref, rs_ref, cb_ref, db_ref):
        p, i, half = pl.program_id(0), pl.program_id(1), pl.program_id(2)

        @pl.when((i == 0) & (half == 0))
        def _():
            dk_ref[...] = jnp.zeros_like(dk_ref)
            dv_ref[...] = jnp.zeros_like(dv_ref)
            dc_ref[...] = jnp.zeros_like(dc_ref)

        hm = _half_mask(half)
        qv = q_ref[...]
        qs = jnp.where(hm, qv, jnp.zeros_like(qv)) * SCALE
        dov = jnp.where(hm, do_ref[...], 0.0)
        dob = dov.astype(BF16)
        qs_t = jnp.transpose(qs.astype(F32)).astype(BF16)
        do_t = jnp.transpose(dov).astype(BF16)
        cb_ref[...] = jnp.broadcast_to(_head_column(c_ref[...], 2 * p + half), (B, LANES)) - st_ref[0]
        db_ref[...] = jnp.broadcast_to(jnp.sum(dov * o_ref[...], axis=1, keepdims=True), (B, LANES))
        row = lax.broadcasted_iota(I32, (B, B), 0)
        col = lax.broadcasted_iota(I32, (B, B), 1)
        dqa_ref[...] = jnp.zeros_like(dqa_ref)
        rs_ref[...] = jnp.zeros_like(rs_ref)

        def block(j, diag):
            off = pl.multiple_of(j * B, B)
            kj = k_ref[pl.ds(off, B), :]
            vj = v_ref[pl.ds(off, B), :]
            pv = jnp.exp(_dot_nt(qs, kj) + (_lane_tile(cb_ref[...], B) - cr_ref[0, pl.ds(j, 1), :]))
            if diag:
                pv = jnp.where(col <= row, pv, 0.0)
            ds = pv * (_dot_nt(dob, vj) - _lane_tile(db_ref[...], B))
            dsb = ds.astype(BF16)
            dqa_ref[...] += _dot(dsb, kj)
            dk_ref[:, pl.ds(off, B)] += _dot(qs_t, dsb)
            dv_ref[:, pl.ds(off, B)] += _dot(do_t, pv.astype(BF16))
            dc_ref[0, half, pl.ds(j, 1), :] -= jnp.sum(ds, axis=0, keepdims=True)
            rs_ref[...] += jnp.sum(ds, axis=1, keepdims=True)

        def step(j, carry):
            block(j, False)
            return carry

        lax.fori_loop(js_ref[2 * p + half, i], i, step, 0)
        block(i, True)
        res = jnp.where(hm, dqa_ref[...] * SCALE, 0.0)

        @pl.when(half == 0)
        def _():
            dq_ref[...] = res

        @pl.when(half == 1)
        def _():
            dq_ref[...] += res

        dc_ref[0, half, pl.ds(i, 1), :] += jnp.transpose(jnp.broadcast_to(rs_ref[...], (B, LANES)))[0:1, :]

    W = n_pair * LANES
    return pl.pallas_call(
        body, name="fox_attn_bwd",
        grid_spec=pltpu.PrefetchScalarGridSpec(
            num_scalar_prefetch=1, grid=(n_pair, nq, 2),
            in_specs=[pl.BlockSpec((B, LANES), lambda p, i, h, js: (i, 3 * n_pair + p)),
                      pl.BlockSpec((S, LANES), lambda p, i, h, js: (0, 4 * n_pair + p)),
                      pl.BlockSpec((S, LANES), lambda p, i, h, js: (0, 5 * n_pair + p)),
                      pl.BlockSpec((B, LANES), lambda p, i, h, js: (i, p)),
                      pl.BlockSpec((B, LANES), lambda p, i, h, js: (i, p)),
                      pl.BlockSpec((1, B, LANES), lambda p, i, h, js: (2 * p + h, i, 0)),
                      pl.BlockSpec((B, LANES), lambda p, i, h, js: (i, 0)),
                      pl.BlockSpec((1, nq, B), lambda p, i, h, js: (2 * p + h, 0, 0))],
            out_specs=[pl.BlockSpec((B, LANES), lambda p, i, h, js: (i, p)),
                       pl.BlockSpec((LANES, S), lambda p, i, h, js: (p, 0)),
                       pl.BlockSpec((LANES, S), lambda p, i, h, js: (p, 0)),
                       pl.BlockSpec((1, 2, nq, B), lambda p, i, h, js: (p, 0, 0, 0))],
            scratch_shapes=[pltpu.VMEM((B, LANES), F32), pltpu.VMEM((B, 1), F32), pltpu.VMEM((B, LANES), F32),
                            pltpu.VMEM((B, LANES), F32)]),
        out_shape=[jax.ShapeDtypeStruct((S, W), F32)] + [jax.ShapeDtypeStruct((W, S), F32)] * 2
        + [jax.ShapeDtypeStruct((n_pair, 2, nq, B), F32)],
        compiler_params=_cparams(("arbitrary", "arbitrary", "arbitrary")),
    )(js, qkv, qkv, qkv, do, o, st, c, c_rows)


def _adam(w, g, m, v):
    m = ADAM_B1 * m + (1.0 - ADAM_B1) * g
    v = ADAM_B2 * v + (1.0 - ADAM_B2) * (g * g)
    m_hat = m / (1.0 - ADAM_B1 ** ADAM_STEP)
    v_hat = v / (1.0 - ADAM_B2 ** ADAM_STEP)
    delta = -ADAM_LR * (m_hat / (jnp.sqrt(v_hat) + ADAM_EPS) + ADAM_WD * w)
    return delta, m, v


def _reduce_adam(landing, w, m, v, name, extra=None):
    _, R, C = w.shape
    TR = next(t for t in (256, 128, R) if R % t == 0)
    more = [] if extra is None else [extra]

    def body(*refs):
        l_ref, w_ref, m_ref, v_ref = refs[:4]
        g_ref, d_ref, nm_ref, nv_ref = refs[4 + len(more):]
        g = l_ref[0].astype(F32)
        for s in range(1, N_DEV):
            g = g + l_ref[s].astype(F32)
        if more:
            g = g + refs[4][...]
        d, nm, nv = _adam(w_ref[0], g, m_ref[0], v_ref[0])
        g_ref[0] = g
        d_ref[0] = d
        nm_ref[0] = nm
        nv_ref[0] = nv

    blk = pl.BlockSpec((1, TR, C), lambda i: (0, i, 0))
    return pl.pallas_call(
        body, name=name, grid=(R // TR,),
        in_specs=[pl.BlockSpec((N_DEV, TR, C), lambda i: (0, i, 0)), blk, blk, blk]
        + [pl.BlockSpec((TR, C), lambda i: (i, 0))] * len(more),
        out_specs=[blk] * 4,
        out_shape=[jax.ShapeDtypeStruct((1, R, C), F32)] * 4,
        compiler_params=_cparams(("arbitrary",)),
    )(landing, w, m, v, *more)


def _reduce_adam_small(landing, w, m, v):
    R, C = w.shape

    def body(l_ref, w_ref, m_ref, v_ref, g_ref, d_ref, nm_ref, nv_ref, loss_ref):
        g = l_ref[0]
        for s in range(1, N_DEV):
            g = g + l_ref[s]
        d, nm, nv = _adam(w_ref[...], g, m_ref[...], v_ref[...])
        g_ref[...] = g
        d_ref[...] = d
        nm_ref[...] = nm
        nv_ref[...] = nv
        loss_ref[...] = jnp.broadcast_to(0.5 * jnp.sum(g[7:8, :], axis=1, keepdims=True), (1, LANES))

    return pl.pallas_call(
        body, name="reduce_adam_small",
        out_shape=[jax.ShapeDtypeStruct((R, C), F32)] * 4 + [jax.ShapeDtypeStruct((1, LANES), F32)],
    )(landing, w, m, v)


def _pad_lanes(a, width):
    return jnp.pad(a, ((0, 0), (0, width - a.shape[1])))


def _pack_small(D, n_fox, g_cat, l1g, l1b, l2g, l2b, bf, last, tail):
    return jnp.concatenate([g_cat, l1g, l1b, l2g, l2b, _pad_lanes(bf, D), jnp.zeros((1, D), F32), last, tail],
                           axis=0)


def kernel(x, w_in, b_f, g_sb, g_fox, w_out, ln1_g, ln1_b, ln2_g, ln2_b, w_gate_up, w_down, loss_target, m_w_in, m_b_f, m_g_sb, m_g_fox, m_w_out, m_ln1_g, m_ln1_b, m_ln2_g, m_ln2_b, m_w_gate_up, m_w_down, v_w_in, v_b_f, v_g_sb, v_g_fox, v_w_out, v_ln1_g, v_ln1_b, v_ln2_g, v_ln2_b, v_w_gate_up, v_w_down):
    x2, tgt = x[0], loss_target[0]
    S, D = x2.shape
    W = D // 2
    n_pair = W // LANES
    n_fox = W // HEAD_DIM
    F = w_down.shape[1] * N_DEV

    g_in = _all_gather_two_level(w_in[0].astype(BF16), "w_in_all_gather")
    w_in_full = g_in.transpose(1, 0, 2).reshape(D, -1)
    w_qkv = w_in_full[:, :6 * W]
    w_f = _pad_lanes(w_in_full[:, 6 * W:], LANES)
    gmat = _tri(D, lambda r, c: (r // HEAD_DIM) == (c // HEAD_DIM))
    g_cat = jnp.concatenate([g_sb, g_fox], axis=1)

    qkv, u, c, ksq = _proj_fwd(x2, w_qkv, w_f, _pad_lanes(b_f, LANES), n_fox)
    c_t = c[:, :n_fox].T
    c_rows = c_t.reshape(n_fox, S // FOX_BLOCK, FOX_BLOCK)
    kmax = jnp.sqrt(ksq[0:1]) * NORM_SLACK
    o_sb, st_sb, js_sb, g_out, g_gu, g_dn = _sb_fwd(
        qkv, n_pair, [w_out[0].astype(BF16), w_gate_up[0].astype(BF16), w_down[0].astype(BF16)])
    w_o = g_out.reshape(D, D)
    w_gu = g_gu.transpose(1, 0, 2).reshape(D, 2 * F)
    w_dn = g_dn.reshape(F, D)
    o_fx, st_fx, js_fx = _fox_fwd(qkv, c, c_t.reshape(n_fox, S // FOX_BK, FOX_BK), kmax, n_pair)
    js_fx = js_fx // (FOX_BLOCK // FOX_BK)
    h1, xh1, rs1, on_b, rr = _post_attn_fwd(o_sb, o_fx, x2, g_cat, gmat, w_o, ln1_g, ln1_b)
    gu, act_b, dyp, sm2 = _mlp_fwd(h1, tgt, w_gu, w_dn, ln2_g, ln2_b)

    dgu_b, dh1 = _mlp_bwd(gu, dyp, w_gu.T, w_dn.T)
    gw_gu = _matmul_tn(h1, [dgu_b], "grad_w_gate_up", n_split=2)
    gw_dn = _matmul_tn(act_b, [dyp], "grad_w_down")
    dxa, dmix_b, do_sb, do_fx, sm1, l_gu, l_dn = _post_attn_bwd(
        dh1, xh1, rs1, ln1_g, o_sb, o_fx, rr, g_cat, gmat, w_o.T,
        [gw_gu.reshape(D, N_DEV, -1).transpose(1, 0, 2), gw_dn.reshape(N_DEV, F // N_DEV, D)])
    dq_sb, dk_sb, dv_sb = _sb_bwd(qkv, do_sb, st_sb, js_sb, n_pair)
    dq_fx, dk_fx, dv_fx, dcr = _fox_bwd(qkv, do_fx, o_fx, st_fx, c, c_rows, js_fx, n_pair)
    dc = _pad_lanes(dcr.reshape(n_fox, S).T, LANES)
    pieces = [dq_sb, dk_sb, dv_sb, dq_fx, dk_fx, dv_fx]
    piece_t = (False, False, False, False, True, True)
    gw_qkv = _matmul_tn(x2, pieces, "grad_w_qkv", pad_cols=LANES, b_t=piece_t)[:, :6 * W + n_fox]
    gw_out = _matmul_tn(on_b, [dmix_b], "grad_w_out")
    dx, gw_f, sm0, l_in, l_out = _proj_bwd(
        dxa, x2, pieces, piece_t, dc, u, w_qkv.T, w_f.T, n_fox,
        [gw_qkv.reshape(D, N_DEV, -1).transpose(1, 0, 2), gw_out.reshape(N_DEV, D // N_DEV, D)])
    gw_f = gw_f[:, :n_fox]

    small = _pack_small(D, n_fox, sm1[2:3], sm1[0:1], sm1[1:2], sm2[0:1], sm2[1:2], sm0[0:1, :n_fox],
                        sm2[2:3] * (1.0 / D), gw_f.T)
    (l_small,) = _exchange_call([jnp.broadcast_to(small[None], (N_DEV,) + small.shape)], True, "small_exchange")

    zero = jnp.zeros((1, D), F32)
    pack = lambda gc, a, b_, c_, d_, bf: _pack_small(D, n_fox, gc, a, b_, c_, d_, bf, zero,
                                                     jnp.zeros((n_fox, D), F32))
    r_small = _reduce_adam_small(
        l_small,
        pack(g_cat, ln1_g, ln1_b, ln2_g, ln2_b, b_f),
        pack(jnp.concatenate([m_g_sb, m_g_fox], axis=1), m_ln1_g, m_ln1_b, m_ln2_g, m_ln2_b, m_b_f),
        pack(jnp.concatenate([v_g_sb, v_g_fox], axis=1), v_ln1_g, v_ln1_b, v_ln2_g, v_ln2_b, v_b_f))
    loss = r_small[4][0, 0]
    cols = w_in.shape[2]
    gf_cols = jnp.pad(r_small[0][8:8 + n_fox].T, ((0, 0), (cols - n_fox, 0)))
    extra = jnp.where(_my_index() == N_DEV - 1, gf_cols, 0.0)
    r_in = _reduce_adam(l_in, w_in, m_w_in, v_w_in, "reduce_adam_w_in", extra)
    r_out = _reduce_adam(l_out, w_out, m_w_out, v_w_out, "reduce_adam_w_out")
    r_gu = _reduce_adam(l_gu, w_gate_up, m_w_gate_up, v_w_gate_up, "reduce_adam_w_gate_up")
    r_dn = _reduce_adam(l_dn, w_down, m_w_down, v_w_down, "reduce_adam_w_down")

    def unpack(kind):
        big = [r_in[kind], None, None, None, r_out[kind], None, None, None, None, r_gu[kind], r_dn[kind]]
        s = r_small[kind]
        big[1] = s[5:6, :n_fox]
        big[2] = s[0:1, :W]
        big[3] = s[0:1, W:]
        big[5], big[6], big[7], big[8] = s[1:2], s[2:3], s[3:4], s[4:5]
        return big

    return (loss, dx[None], *unpack(0), *unpack(1), *unpack(2), *unpack(3))
```

```python
import functools

import jax
import jax.numpy as jnp
from jax import lax
from jax.experimental import pallas as pl
from jax.experimental.pallas import tpu as pltpu

F32 = jnp.float32
BF16 = jnp.bfloat16
I32 = jnp.int32

N_DEV = 8
HEAD_DIM = 64
LANES = 128
SCALE = HEAD_DIM ** -0.5
ALPHA = 2.0 ** 0.25
LN_EPS = 1e-5
RMS_EPS = 1e-6
ADAM_LR, ADAM_B1, ADAM_B2, ADAM_EPS, ADAM_WD, ADAM_STEP = 0.001, 0.9, 0.999, 1e-08, 0.01, 10
NEG_BIG = -1e30
NORM_SLACK = 1.01
EXP_ZERO = 88.5
VMEM_LIMIT = 60 * 1024 * 1024
ROW_TILE = 256
SB_BLOCK = 256
FOX_BLOCK = 512
FOX_BQ, FOX_BK = 512, 256
assert FOX_BQ == FOX_BLOCK and FOX_BLOCK % FOX_BK == 0
MESH = pl.DeviceIdType.MESH


def _cparams(sem):
    return pltpu.CompilerParams(dimension_semantics=sem, vmem_limit_bytes=VMEM_LIMIT)


def _dot(a, b):
    return jnp.dot(a, b, preferred_element_type=F32)


def _dot_nt(a, b):
    return lax.dot_general(a, b, (((1,), (1,)), ((), ())), preferred_element_type=F32)


def _dot_tn(a, b):
    return lax.dot_general(a, b, (((0,), (0,)), ((), ())), preferred_element_type=F32)


def _split2(a):
    hi = a.astype(BF16)
    lo = (a - hi.astype(F32)).astype(BF16)
    return hi, lo


def _split3(a):
    hi = a.astype(BF16)
    r1 = a - hi.astype(F32)
    mid = r1.astype(BF16)
    lo = (r1 - mid.astype(F32)).astype(BF16)
    return hi, mid, lo


def _dot_acc(a, m):
    hi, lo = _split2(a)
    return _dot(jnp.concatenate([hi, lo], axis=1), jnp.concatenate([m, m], axis=0))


def _tri(n, fn):
    r = lax.broadcasted_iota(I32, (n, n), 0)
    c = lax.broadcasted_iota(I32, (n, n), 1)
    return fn(r, c).astype(BF16)


def _full(shape):
    nd = len(shape)
    return pl.BlockSpec(shape, lambda *_: (0,) * nd)


def _peer(k):
    x, y, c = lax.axis_index("x"), lax.axis_index("y"), lax.axis_index("c")
    return (1 - x if k & 4 else x, 1 - y if k & 2 else y, 1 - c if k & 1 else c)


def _my_index():
    return 4 * lax.axis_index("x") + 2 * lax.axis_index("y") + lax.axis_index("c")


class _Exchange:
    def __init__(self, n, scatter):
        self.n, self.scatter = n, scatter

    def sem_shapes(self):
        return [pltpu.SemaphoreType.DMA(((N_DEV - 1) * self.n,)), pltpu.SemaphoreType.DMA(((N_DEV - 1) * self.n,)),
                pltpu.SemaphoreType.DMA((self.n,))]

    def out_shapes(self, arrays):
        if self.scatter:
            return [jax.ShapeDtypeStruct(s.shape, s.dtype) for s in arrays]
        return [jax.ShapeDtypeStruct((N_DEV,) + s.shape, s.dtype) for s in arrays]

    def _copies(self, ins, outs, sems, landing):
        send_sems, recv_sems, local_sems = sems
        me = _my_index()
        src = lambda a, d: ins[a].at[d] if self.scatter else ins[a]
        local = [pltpu.make_async_copy(src(a, me), outs[a].at[me], local_sems.at[a]) for a in range(self.n)]
        remote = [pltpu.make_async_remote_copy(
            src_ref=src(a, me ^ k), dst_ref=outs[a].at[me ^ k if landing else me],
            send_sem=send_sems.at[(k - 1) * self.n + a], recv_sem=recv_sems.at[(k - 1) * self.n + a],
            device_id=_peer(k), device_id_type=MESH) for k in range(1, N_DEV) for a in range(self.n)]
        return local, remote

    def start(self, ins, outs, sems):
        local, sent = self._copies(ins, outs, sems, landing=False)
        for cp in local + sent:
            cp.start()

    def finish(self, ins, outs, sems):
        local, landed = self._copies(ins, outs, sems, landing=True)
        for cp in landed:
            cp.wait_recv()
        for cp in landed:
            cp.wait_send()
        for cp in local:
            cp.wait()


def _all_gather_two_level(shard, name):
    def body(x_ref, out_ref, send_sems, recv_sems, local_sem):
        x, y, c = lax.axis_index("x"), lax.axis_index("y"), lax.axis_index("c")
        me, sibling = (x, y, c), (x, y, 1 - c)
        chips = [(1 - x, y), (x, 1 - y), (1 - x, 1 - y)]

        def slot(px, py, pc):
            return out_ref.at[4 * px + 2 * py + pc]

        def copy(k, block, to, src=None):
            return pltpu.make_async_remote_copy(
                src_ref=slot(*block) if src is None else src, dst_ref=slot(*block),
                send_sem=send_sems.at[k], recv_sem=recv_sems.at[k], device_id=to, device_id_type=MESH)

        mine = pltpu.make_async_copy(x_ref, slot(*me), local_sem)
        mine.start()
        first = [copy(0, me, sibling, src=x_ref)]
        first += [copy(1 + j, me, (*chip, c), src=x_ref) for j, chip in enumerate(chips)]
        for cp in first:
            cp.start()
        passed = [copy(4 + j, (*chip, c), sibling) for j, chip in enumerate(chips)]
        for j, chip in enumerate(chips):
            copy(1 + j, (*chip, c), me).wait_recv()
            passed[j].start()
        copy(0, sibling, me).wait_recv()
        for j, chip in enumerate(chips):
            copy(4 + j, (*chip, 1 - c), me).wait_recv()
        for cp in first + passed:
            cp.wait_send()
        mine.wait()

    any_spec = pl.BlockSpec(memory_space=pl.ANY)
    return pl.pallas_call(
        body, name=name, out_shape=jax.ShapeDtypeStruct((N_DEV,) + shard.shape, shard.dtype),
        in_specs=[any_spec], out_specs=any_spec,
        scratch_shapes=[pltpu.SemaphoreType.DMA((N_DEV - 1,)), pltpu.SemaphoreType.DMA((N_DEV - 1,)),
                        pltpu.SemaphoreType.DMA],
    )(shard)


def _exchange_call(arrays, scatter, name):
    n = len(arrays)
    ex = _Exchange(n, scatter)

    def body(*refs):
        ins, outs, sems = refs[:n], refs[n:2 * n], refs[2 * n:]
        ex.start(ins, outs, sems)
        ex.finish(ins, outs, sems)

    any_spec = pl.BlockSpec(memory_space=pl.ANY)
    return pl.pallas_call(
        body, name=name, out_shape=ex.out_shapes(arrays),
        in_specs=[any_spec] * n, out_specs=[any_spec] * n, scratch_shapes=ex.sem_shapes(),
    )(*arrays)


def _log_sigmoid(u):
    return jnp.minimum(u, 0.0) - jnp.log1p(jnp.exp(-jnp.abs(u)))


def _proj_fwd(x, w_qkv, w_f, bf_pad, n_fox):
    S, D = x.shape
    N = w_qkv.shape[1]
    W = D // 2
    TM = ROW_TILE
    tri = _tri(TM, lambda r, c: c <= r)
    r_ = lax.broadcasted_iota(I32, (W, LANES), 0)
    c_ = lax.broadcasted_iota(I32, (W, LANES), 1)
    head_of = (r_ // HEAD_DIM == c_).astype(BF16)

    def body(x_ref, wq_ref, wf_ref, bf_ref, tri_ref, ho_ref, qkv_ref, u_ref, c_ref, ksq_ref, run_ref):
        @pl.when(pl.program_id(0) == 0)
        def _():
            run_ref[...] = jnp.zeros_like(run_ref)
            ksq_ref[...] = jnp.zeros_like(ksq_ref)

        xb = x_ref[...].astype(BF16)
        for n0 in range(0, N, D):
            chunk = _dot(xb, wq_ref[:, n0:n0 + D]).astype(BF16)
            qkv_ref[:, n0:n0 + D] = chunk
            if n0 == 4 * W:
                kf = chunk[:, :W].astype(F32)
                ksq = jnp.max(_dot_acc(kf * kf, ho_ref[...]), axis=0, keepdims=True)
                ksq_ref[...] = jnp.maximum(ksq_ref[...], ksq)
        u = _dot(xb, wf_ref[...]) + bf_ref[...]
        lane = lax.broadcasted_iota(I32, u.shape, 1)
        logf = jnp.where(lane < n_fox, _log_sigmoid(u), 0.0)
        u_ref[...] = u
        hi, mid, lo = _split3(logf)
        t = tri_ref[...]
        cs = _dot(t, hi) + _dot(t, mid) + _dot(t, lo) + run_ref[...]
        c_ref[...] = cs
        run_ref[...] = cs[TM - 1:TM, :]

    return pl.pallas_call(
        body, name="proj_fwd", grid=(S // TM,),
        in_specs=[pl.BlockSpec((TM, D), lambda i: (i, 0)), _full(w_qkv.shape), _full(w_f.shape),
                  _full(bf_pad.shape), _full(tri.shape), _full(head_of.shape)],
        out_specs=[pl.BlockSpec((TM, N), lambda i: (i, 0)), pl.BlockSpec((TM, LANES), lambda i: (i, 0)),
                   pl.BlockSpec((TM, LANES), lambda i: (i, 0)), _full((8, LANES))],
        out_shape=[jax.ShapeDtypeStruct((S, N), BF16), jax.ShapeDtypeStruct((S, LANES), F32),
                   jax.ShapeDtypeStruct((S, LANES), F32), jax.ShapeDtypeStruct((8, LANES), F32)],
        scratch_shapes=[pltpu.VMEM((1, LANES), F32)],
        compiler_params=_cparams(("arbitrary",)),
    )(x, w_qkv, w_f, bf_pad, tri, head_of)


def _post_attn_fwd(o_sb, o_fx, x, g_cat, gmat, w_out, ln_g, ln_b):
    S, D = x.shape
    H = D // 2
    TM = ROW_TILE

    def body(osb_ref, ofx_ref, x_ref, g_ref, gm_ref, wo_ref, lg_ref, lb_ref,
             h1_ref, xh_ref, rs_ref, on_ref, rr_ref):
        o = jnp.concatenate([osb_ref[...], ofx_ref[...]], axis=1)
        ms = _dot_acc(o * o, gm_ref[...]) * (1.0 / HEAD_DIM)
        r = lax.rsqrt(ms + RMS_EPS)
        onb = (o * r * g_ref[...]).astype(BF16)
        hp = ALPHA * x_ref[...] + _dot(onb, wo_ref[...])
        mu = jnp.mean(hp, axis=-1, keepdims=True)
        d = hp - mu
        rstd = lax.rsqrt(jnp.mean(d * d, axis=-1, keepdims=True) + LN_EPS)
        xh = d * rstd
        h1_ref[...] = xh * lg_ref[...] + lb_ref[...]
        xh_ref[...] = xh
        rs_ref[...] = jnp.broadcast_to(rstd, (TM, LANES))
        on_ref[...] = onb
        rr_ref[...] = r

    row = lambda w: pl.BlockSpec((TM, w), lambda i: (i, 0))
    return pl.pallas_call(
        body, name="post_attn_fwd", grid=(S // TM,),
        in_specs=[row(H), row(H), row(D), _full((1, D)), _full((D, D)), _full((D, D)), _full((1, D)), _full((1, D))],
        out_specs=[row(D), row(D), row(LANES), row(D), row(D)],
        out_shape=[jax.ShapeDtypeStruct((S, D), F32), jax.ShapeDtypeStruct((S, D), F32),
                   jax.ShapeDtypeStruct((S, LANES), F32), jax.ShapeDtypeStruct((S, D), BF16),
                   jax.ShapeDtypeStruct((S, D), F32)],
        compiler_params=_cparams(("arbitrary",)),
    )(o_sb, o_fx, x, g_cat, gmat, w_out, ln_g, ln_b)


def _ln_bwd(dxh, xh, rstd):
    m1 = jnp.mean(dxh, axis=-1, keepdims=True)
    m2 = jnp.mean(dxh * xh, axis=-1, keepdims=True)
    return rstd * (dxh - m1 - xh * m2)


def _mlp_fwd(h1, target, w_gu, w_dn, ln_g, ln_b):
    S, D = h1.shape
    F = w_dn.shape[0]
    TM = ROW_TILE
    FC = F // 2

    def body(h1_ref, tg_ref, wgu_hbm, wdn_hbm, lg_ref, lb_ref, gu_ref, act_ref, dyp_ref, sm_ref, wgu, wdn):
        @pl.when(pl.program_id(0) == 0)
        def _():
            pltpu.sync_copy(wgu_hbm, wgu)
            pltpu.sync_copy(wdn_hbm, wdn)
            sm_ref[...] = jnp.zeros_like(sm_ref)

        h1v = h1_ref[...]
        hb = h1v.astype(BF16)
        ff = jnp.zeros((TM, D), F32)
        for c0 in range(0, F, FC):
            g = _dot(hb, wgu[:, c0:c0 + FC])
            u = _dot(hb, wgu[:, F + c0:F + c0 + FC])
            gu_ref[:, c0:c0 + FC] = g
            gu_ref[:, F + c0:F + c0 + FC] = u
            ab = ((g * jax.nn.sigmoid(g)) * u).astype(BF16)
            act_ref[:, c0:c0 + FC] = ab
            ff = ff + _dot(ab, wdn[c0:c0 + FC, :])
        yp = ALPHA * h1v + ff
        mu = jnp.mean(yp, axis=-1, keepdims=True)
        d = yp - mu
        rstd = lax.rsqrt(jnp.mean(d * d, axis=-1, keepdims=True) + LN_EPS)
        xh = d * rstd
        err = (xh * lg_ref[...] + lb_ref[...]) - tg_ref[...]
        dy = err * (1.0 / D)
        sm_ref[0:1, :] += jnp.sum(dy * xh, axis=0, keepdims=True)
        sm_ref[1:2, :] += jnp.sum(dy, axis=0, keepdims=True)
        sm_ref[2:3, :] += jnp.sum(err * err, axis=0, keepdims=True)
        dyp_ref[...] = _ln_bwd(dy * lg_ref[...], xh, rstd)

    row = lambda w: pl.BlockSpec((TM, w), lambda i: (i, 0))
    hbm = pl.BlockSpec(memory_space=pl.ANY)
    return pl.pallas_call(
        body, name="mlp_fwd", grid=(S // TM,),
        in_specs=[row(D), row(D), hbm, hbm, _full((1, D)), _full((1, D))],
        out_specs=[row(2 * F), row(F), row(D), _full((8, D))],
        out_shape=[jax.ShapeDtypeStruct((S, 2 * F), F32), jax.ShapeDtypeStruct((S, F), BF16),
                   jax.ShapeDtypeStruct((S, D), F32), jax.ShapeDtypeStruct((8, D), F32)],
        scratch_shapes=[pltpu.VMEM(w_gu.shape, BF16), pltpu.VMEM(w_dn.shape, BF16)],
        compiler_params=_cparams(("arbitrary",)),
    )(h1, target, w_gu, w_dn, ln_g, ln_b)


def _mlp_bwd(gu, dyp, w_guT, w_dnT):
    S, D = dyp.shape
    F = w_dnT.shape[1]
    TM = ROW_TILE
    FC = F // 2

    def body(gu_ref, dyp_ref, wguT_hbm, wdnT_hbm, dgu_ref, dh1_ref, wguT, wdnT):
        @pl.when(pl.program_id(0) == 0)
        def _():
            pltpu.sync_copy(wguT_hbm, wguT)
            pltpu.sync_copy(wdnT_hbm, wdnT)

        dypv = dyp_ref[...]
        db = dypv.astype(BF16)
        dh1 = ALPHA * dypv
        for c0 in range(0, F, FC):
            dact = _dot(db, wdnT[:, c0:c0 + FC])
            g = gu_ref[:, c0:c0 + FC]
            u = gu_ref[:, F + c0:F + c0 + FC]
            sg = jax.nn.sigmoid(g)
            dgb = (dact * u * (sg * (1.0 + g * (1.0 - sg)))).astype(BF16)
            dub = (dact * (g * sg)).astype(BF16)
            dgu_ref[:, c0:c0 + FC] = dgb
            dgu_ref[:, F + c0:F + c0 + FC] = dub
            dh1 = dh1 + _dot(dgb, wguT[c0:c0 + FC, :]) + _dot(dub, wguT[F + c0:F + c0 + FC, :])
        dh1_ref[...] = dh1

    row = lambda w: pl.BlockSpec((TM, w), lambda i: (i, 0))
    hbm = pl.BlockSpec(memory_space=pl.ANY)
    return pl.pallas_call(
        body, name="mlp_bwd", grid=(S // TM,),
        in_specs=[row(2 * F), row(D), hbm, hbm],
        out_specs=[row(2 * F), row(D)],
        out_shape=[jax.ShapeDtypeStruct((S, 2 * F), BF16), jax.ShapeDtypeStruct((S, D), F32)],
        scratch_shapes=[pltpu.VMEM(w_guT.shape, BF16), pltpu.VMEM(w_dnT.shape, BF16)],
        compiler_params=_cparams(("arbitrary",)),
    )(gu, dyp, w_guT, w_dnT)


def _post_attn_bwd(dh1, xh, rs, ln_g, o_sb, o_fx, rr, g_cat, gmat, w_outT, chunked):
    S, D = dh1.shape
    H = D // 2
    TM = ROW_TILE
    nT = S // TM
    n_ch = len(chunked)
    ex = _Exchange(n_ch, scatter=True)

    def body(*refs):
        dh1_ref, xh_ref, rs_ref, lg_ref, osb_ref, ofx_ref, rr_ref, g_ref, gm_ref, woT_ref = refs[:10]
        ch_in = refs[10:10 + n_ch]
        dxa_ref, dmix_ref, dosb_ref, dofx_ref, sm_ref = refs[10 + n_ch:15 + n_ch]
        ch_out = refs[15 + n_ch:15 + 2 * n_ch]
        sems = refs[15 + 2 * n_ch:]

        @pl.when(pl.program_id(0) == 0)
        def _():
            ex.start(ch_in, ch_out, sems)
            sm_ref[...] = jnp.zeros_like(sm_ref)

        dh = dh1_ref[...]
        xhv = xh_ref[...]
        dhp = _ln_bwd(dh * lg_ref[...], xhv, rs_ref[:, 0:1])
        dxa_ref[...] = ALPHA * dhp
        dmb = dhp.astype(BF16)
        dmix_ref[...] = dmb
        don = _dot(dmb, woT_ref[...])
        o = jnp.concatenate([osb_ref[...], ofx_ref[...]], axis=1)
        r = rr_ref[...]
        u = don * g_ref[...]
        t = _dot_acc(u * o, gm_ref[...]) * (1.0 / HEAD_DIM)
        do = r * u - o * (r * r * r) * t
        dosb_ref[...] = do[:, :H]
        dofx_ref[...] = do[:, H:]
        sm_ref[0:1, :] += jnp.sum(dh * xhv, axis=0, keepdims=True)
        sm_ref[1:2, :] += jnp.sum(dh, axis=0, keepdims=True)
        sm_ref[2:3, :] += jnp.sum(don * o * r, axis=0, keepdims=True)

        @pl.when(pl.program_id(0) == nT - 1)
        def _():
            ex.finish(ch_in, ch_out, sems)

    row = lambda w: pl.BlockSpec((TM, w), lambda i: (i, 0))
    any_spec = pl.BlockSpec(memory_space=pl.ANY)
    return pl.pallas_call(
        body, name="post_attn_bwd", grid=(nT,),
        in_specs=[row(D), row(D), row(LANES), _full((1, D)), row(H), row(H), row(D), _full((1, D)),
                  _full((D, D)), _full((D, D))] + [any_spec] * n_ch,
        out_specs=[row(D), row(D), row(H), row(H), _full((8, D))] + [any_spec] * n_ch,
        out_shape=[jax.ShapeDtypeStruct((S, D), F32), jax.ShapeDtypeStruct((S, D), BF16),
                   jax.ShapeDtypeStruct((S, H), F32), jax.ShapeDtypeStruct((S, H), F32),
                   jax.ShapeDtypeStruct((8, D), F32)] + ex.out_shapes(chunked),
        scratch_shapes=ex.sem_shapes(),
        compiler_params=_cparams(("arbitrary",)),
    )(dh1, xh, rs, ln_g, o_sb, o_fx, rr, g_cat, gmat, w_outT, *chunked)


def _proj_bwd(dxa, x, pieces, piece_t, dc, u, w_qkvT, w_fT, n_fox, chunked):
    S, D = dxa.shape
    H = D // 2
    TM = ROW_TILE
    nT = S // TM
    tri = _tri(TM, lambda r, c: c >= r)
    n_p = len(pieces)
    n_ch = len(chunked)
    ex = _Exchange(n_ch, scatter=True)

    def body(*refs):
        dxa_ref, x_ref = refs[:2]
        p_refs = refs[2:2 + n_p]
        dc_ref, u_ref, wq_ref, wf_ref, tri_ref = refs[2 + n_p:7 + n_p]
        ch_in = refs[7 + n_p:7 + n_p + n_ch]
        dx_ref, gwf_ref, sm_ref = refs[7 + n_p + n_ch:10 + n_p + n_ch]
        ch_out = refs[10 + n_p + n_ch:10 + n_p + 2 * n_ch]
        run_ref = refs[10 + n_p + 2 * n_ch]
        sems = refs[11 + n_p + 2 * n_ch:]

        @pl.when(pl.program_id(0) == 0)
        def _():
            ex.start(ch_in, ch_out, sems)
            run_ref[...] = jnp.zeros_like(run_ref)
            sm_ref[...] = jnp.zeros_like(sm_ref)
            gwf_ref[...] = jnp.zeros_like(gwf_ref)

        hi, mid, lo = _split3(dc_ref[...])
        t = tri_ref[...]
        dlogf = _dot(t, hi) + _dot(t, mid) + _dot(t, lo) + run_ref[...]
        run_ref[...] = dlogf[0:1, :]
        uv = u_ref[...]
        lane = lax.broadcasted_iota(I32, uv.shape, 1)
        df = jnp.where(lane < n_fox, dlogf * jax.nn.sigmoid(-uv), 0.0)
        sm_ref[0:1, :] += jnp.sum(df, axis=0, keepdims=True)
        dfb = df.astype(BF16)
        gwf_ref[...] += _dot(jnp.transpose(df).astype(BF16), x_ref[...].astype(BF16))
        acc = dxa_ref[...] + _dot(dfb, wf_ref[...])
        for a in range(n_p):
            mm = _dot_tn if piece_t[a] else _dot
            acc = acc + mm(p_refs[a][...].astype(BF16), wq_ref[a * H:(a + 1) * H, :])
        dx_ref[...] = acc

        @pl.when(pl.program_id(0) == nT - 1)
        def _():
            ex.finish(ch_in, ch_out, sems)

    rev = lambda w: pl.BlockSpec((TM, w), lambda i: (nT - 1 - i, 0))
    any_spec = pl.BlockSpec(memory_space=pl.ANY)
    return pl.pallas_call(
        body, name="proj_bwd", grid=(nT,),
        in_specs=[rev(D), rev(D)]
        + [pl.BlockSpec((H, TM), lambda i: (0, nT - 1 - i)) if t else rev(H) for t in piece_t]
        + [rev(LANES), rev(LANES), _full(w_qkvT.shape), _full(w_fT.shape), _full(tri.shape)] + [any_spec] * n_ch,
        out_specs=[rev(D), _full((LANES, D)), _full((8, LANES))] + [any_spec] * n_ch,
        out_shape=[jax.ShapeDtypeStruct((S, D), F32), jax.ShapeDtypeStruct((LANES, D), F32),
                   jax.ShapeDtypeStruct((8, LANES), F32)] + ex.out_shapes(chunked),
        scratch_shapes=[pltpu.VMEM((1, LANES), F32)] + ex.sem_shapes(),
        compiler_params=_cparams(("arbitrary",)),
    )(dxa, x, *pieces, dc, u, w_qkvT, w_fT, tri, *chunked)


def _matmul_tn(a, bs, name, n_split=1, pad_cols=0, b_t=None):
    S, M = a.shape
    b_t = tuple(b_t) if b_t is not None else (False,) * len(bs)
    widths = [b.shape[0] if t else b.shape[1] for b, t in zip(bs, b_t)]
    N = sum(widths) + pad_cols
    assert n_split == 1 or (len(bs) == 1 and pad_cols == 0 and not b_t[0])
    TK = 512 if S % 512 == 0 else ROW_TILE
    MC = 512 if M % 512 == 0 else 256
    nb = len(bs)

    def body(*refs):
        a_ref, b_refs, o_ref, acc_ref = refs[0], refs[1:1 + nb], refs[1 + nb], refs[2 + nb]
        t_refs = list(refs[3 + nb:])

        @pl.when(pl.program_id(1) == 0)
        def _():
            acc_ref[...] = jnp.zeros_like(acc_ref)
            for t_ref in t_refs:
                t_ref[...] = jnp.zeros_like(t_ref)

        n0 = 0
        turned = []
        for b_ref, t, w in zip(b_refs, b_t, widths):
            bv = b_ref[...].astype(BF16)
            if t:
                t_ref = t_refs[len(turned)]
                t_ref[...] += _dot(bv, a_ref[...].astype(BF16))
                turned.append((t_ref, n0, w))
            else:
                for m0 in range(0, M, MC):
                    acc_ref[m0:m0 + MC, n0:n0 + w] += _dot_tn(a_ref[:, m0:m0 + MC].astype(BF16), bv)
            n0 += w

        @pl.when(pl.program_id(1) == S // TK - 1)
        def _():
            for t_ref, c0, w in turned:
                acc_ref[:, c0:c0 + w] = jnp.transpose(t_ref[...])
            o_ref[...] = acc_ref[...].astype(BF16)

    return pl.pallas_call(
        body, name=name, grid=(n_split, S // TK),
        in_specs=[pl.BlockSpec((TK, M), lambda n, k: (k, 0))]
        + [pl.BlockSpec((w, TK), lambda n, k: (0, k)) if t else pl.BlockSpec((TK, w // n_split), lambda n, k: (k, n))
           for w, t in zip(widths, b_t)],
        out_specs=pl.BlockSpec((M, N // n_split), lambda n, k: (0, n)),
        out_shape=jax.ShapeDtypeStruct((M, N), BF16),
        scratch_shapes=[pltpu.VMEM((M, N // n_split), F32)]
        + [pltpu.VMEM((w, M), F32) for w, t in zip(widths, b_t) if t],
        compiler_params=_cparams(("arbitrary", "arbitrary")),
    )(a, *bs)


def _half_mask(half):
    lane = lax.broadcasted_iota(I32, (1, LANES), 1)
    return (lane >= half * HEAD_DIM) & (lane < half * HEAD_DIM + HEAD_DIM)


def _lane_tile(a, width):
    return jnp.concatenate([a] * (width // LANES), axis=1)


def _softplus_neg_abs(z):
    return jnp.log(1.0 + jnp.exp(-jnp.abs(z)))


def _sb_fwd(qkv, n_pair, shards):
    S = qkv.shape[0]
    B = SB_BLOCK
    nq = S // B
    us = _tri(B, lambda r, c: r > c)
    n_sh = len(shards)
    ex = _Exchange(n_sh, scatter=False)

    def body(*refs):
        q_ref, k_ref, v_ref, us_ref = refs[:4]
        sh_in = refs[4:4 + n_sh]
        o_ref, st_ref, js_ref = refs[4 + n_sh:7 + n_sh]
        sh_out = refs[7 + n_sh:7 + 2 * n_sh]
        acc_ref, r_ref = refs[7 + 2 * n_sh:9 + 2 * n_sh]
        sems = refs[9 + 2 * n_sh:]
        p, i = pl.program_id(0), pl.program_id(1)

        @pl.when((p == 0) & (i == 0))
        def _():
            ex.start(sh_in, sh_out, sems)

        hms = [_half_mask(h) for h in range(2)]
        qv = q_ref[...]
        qss = [jnp.where(hm, qv, jnp.zeros_like(qv)) * SCALE for hm in hms]
        row = lax.broadcasted_iota(I32, (B, B), 0)
        col = lax.broadcasted_iota(I32, (B, B), 1)
        tri = col < row
        acc_ref[...] = jnp.zeros_like(acc_ref)
        r_ref[...] = jnp.zeros_like(r_ref)

        def block(j, diag):
            off = pl.multiple_of(j * B, B)
            kj = k_ref[pl.ds(off, B), :]
            vj = v_ref[pl.ds(off, B), :]
            zs = [_dot_nt(qss[h], kj) for h in range(2)]
            sps = [_softplus_neg_abs(z) for z in zs]
            bs = [jnp.minimum(-z, 0.0) - sp for z, sp in zip(zs, sps)]
            if diag:
                bs = [jnp.where(tri, b, 0.0) for b in bs]
            lexcs = [_dot_acc(b, us_ref[...]) for b in bs]
            ws = [jnp.exp(jnp.minimum(z, 0.0) - sp + (_lane_tile(r_ref[h], B) + lexc))
                  for h, (z, sp, lexc) in enumerate(zip(zs, sps, lexcs))]
            if diag:
                ws = [jnp.where(tri, w, 0.0) for w in ws]
            for h in range(2):
                acc_ref[h] += _dot(ws[h].astype(BF16), vj)
                r_ref[h] += jnp.broadcast_to(lexcs[h][:, 0:1] + bs[h][:, 0:1], (B, LANES))

        def live():
            return (jnp.max(r_ref[...]) > -EXP_ZERO).astype(I32)

        def first_two():
            has_prev = (i > 0).astype(F32)
            off_d = pl.multiple_of(i * B, B)
            off_o = pl.multiple_of(jnp.maximum(i - 1, 0) * B, B)
            k_d, v_d = k_ref[pl.ds(off_d, B), :], v_ref[pl.ds(off_d, B), :]
            k_o, v_o = k_ref[pl.ds(off_o, B), :], v_ref[pl.ds(off_o, B), :]
            hs = range(2)
            z_d = [_dot_nt(qss[h], k_d) for h in hs]
            z_o = [_dot_nt(qss[h], k_o) for h in hs]
            sp_d = [_softplus_neg_abs(z) for z in z_d]
            sp_o = [_softplus_neg_abs(z) for z in z_o]
            b_d = [jnp.where(tri, jnp.minimum(-z, 0.0) - sp, 0.0) for z, sp in zip(z_d, sp_d)]
            b_o = [(jnp.minimum(-z, 0.0) - sp) * has_prev for z, sp in zip(z_o, sp_o)]
            l_d = [_dot_acc(b, us_ref[...]) for b in b_d]
            l_o = [_dot_acc(b, us_ref[...]) for b in b_o]
            r_d = [jnp.broadcast_to(l[:, 0:1] + b[:, 0:1], (B, LANES)) for l, b in zip(l_d, b_d)]
            w_d = [jnp.where(tri, jnp.exp(jnp.minimum(z, 0.0) - sp + l), 0.0) for z, sp, l in zip(z_d, sp_d, l_d)]
            w_o = [jnp.exp(jnp.minimum(z, 0.0) - sp + (_lane_tile(r, B) + l)) * has_prev
                   for z, sp, l, r in zip(z_o, sp_o, l_o, r_d)]
            for h in hs:
                acc_ref[h] = _dot(w_d[h].astype(BF16), v_d) + _dot(w_o[h].astype(BF16), v_o)
                r_ref[h] = r_d[h] + jnp.broadcast_to(l_o[h][:, 0:1] + b_o[h][:, 0:1], (B, LANES))

        first_two()

        def step(carry):
            j, _ = carry
            block(j, False)
            return j - 1, live()

        j_end, _ = lax.while_loop(lambda c: (c[0] >= 0) & (c[1] > 0), step, (i - 2, live()))
        js = jnp.maximum(j_end + 1, 0)
        js_ref[2 * p, i] = js
        js_ref[2 * p + 1, i] = js
        o_ref[...] = jnp.where(hms[0], acc_ref[0], acc_ref[1])
        st_ref[...] = r_ref[...]

        @pl.when((p == n_pair - 1) & (i == nq - 1))
        def _():
            ex.finish(sh_in, sh_out, sems)

    any_spec = pl.BlockSpec(memory_space=pl.ANY)
    return pl.pallas_call(
        body, name="sb_attn_fwd", grid=(n_pair, nq),
        in_specs=[pl.BlockSpec((B, LANES), lambda p, i: (i, p)),
                  pl.BlockSpec((S, LANES), lambda p, i: (0, n_pair + p)),
                  pl.BlockSpec((S, LANES), lambda p, i: (0, 2 * n_pair + p)),
                  _full((B, B))] + [any_spec] * n_sh,
        out_specs=[pl.BlockSpec((B, LANES), lambda p, i: (i, p)),
                   pl.BlockSpec((2, B, LANES), lambda p, i: (p, i, 0)),
                   pl.BlockSpec(memory_space=pltpu.SMEM)] + [any_spec] * n_sh,
        out_shape=[jax.ShapeDtypeStruct((S, n_pair * LANES), F32),
                   jax.ShapeDtypeStruct((2 * n_pair, S, LANES), F32),
                   jax.ShapeDtypeStruct((2 * n_pair, nq), I32)] + ex.out_shapes(shards),
        scratch_shapes=[pltpu.VMEM((2, B, LANES), F32), pltpu.VMEM((2, B, LANES), F32)] + ex.sem_shapes(),
        compiler_params=_cparams(("arbitrary", "arbitrary")),
    )(qkv, qkv, qkv, us, *shards)


def _sb_bwd(qkv, do, st, js, n_pair):
    S = qkv.shape[0]
    B = SB_BLOCK
    nq = S // B
    us = _tri(B, lambda r, c: r > c)
    ti = _tri(B, lambda r, c: r <= c)

    def body(js_ref, q_ref, k_ref, v_ref, do_ref, st_ref, us_ref, ti_ref, dq_ref, dk_ref, dv_ref,
             dqa_ref, pr_ref, er_ref):
        p, i = pl.program_id(0), pl.program_id(1)

        @pl.when(i == 0)
        def _():
            dk_ref[...] = jnp.zeros_like(dk_ref)
            dv_ref[...] = jnp.zeros_like(dv_ref)

        hms = [_half_mask(h) for h in range(2)]
        qv = q_ref[...]
        dov = do_ref[...]
        qss = [jnp.where(hm, qv, jnp.zeros_like(qv)) * SCALE for hm in hms]
        dobs = [jnp.where(hm, dov, 0.0).astype(BF16) for hm in hms]
        qs_t = [jnp.transpose(q.astype(F32)).astype(BF16) for q in qss]
        do_t = [jnp.transpose(jnp.where(hm, dov, 0.0)).astype(BF16) for hm in hms]
        row = lax.broadcasted_iota(I32, (B, B), 0)
        col = lax.broadcasted_iota(I32, (B, B), 1)
        tri = col < row
        dqa_ref[...] = jnp.zeros_like(dqa_ref)
        pr_ref[...] = jnp.zeros_like(pr_ref)
        er_ref[...] = jnp.zeros_like(er_ref)

        def block(j, diag):
            off = pl.multiple_of(j * B, B)
            kj = k_ref[pl.ds(off, B), :]
            vj = v_ref[pl.ds(off, B), :]
            hs = range(2)
            zs = [_dot_nt(qss[h], kj) for h in hs]
            dws = [_dot_nt(dobs[h], vj) for h in hs]
            sps = [_softplus_neg_abs(z) for z in zs]
            bs = [jnp.minimum(-z, 0.0) - sp for z, sp in zip(zs, sps)]
            if diag:
                bs = [jnp.where(tri, b, 0.0) for b in bs]
            lexcs = [_dot_acc(b, us_ref[...]) for b in bs]
            ws = []
            for h in hs:
                pr_new = pr_ref[h] + jnp.broadcast_to(lexcs[h][:, 0:1] + bs[h][:, 0:1], (B, LANES))
                pr_ref[h] = pr_new
                w = jnp.exp(jnp.minimum(zs[h], 0.0) - sps[h] + (_lane_tile(st_ref[h] - pr_new, B) + lexcs[h]))
                ws.append(jnp.where(tri, w, 0.0) if diag else w)
            es = [dw * w for dw, w in zip(dws, ws)]
            eincs = [_dot_acc(e, ti_ref[...]) for e in es]
            dzbs = []
            for h in hs:
                er = er_ref[h]
                big_e = _lane_tile(er, B) + (eincs[h] - es[h])
                er_ref[h] = er + jnp.broadcast_to(eincs[h][:, B - 1:B], (B, LANES))
                eb = jnp.exp(bs[h])
                dzbs.append((es[h] * eb - big_e * (1.0 - eb)).astype(BF16))
            for h in hs:
                dqa_ref[h] += _dot(dzbs[h], kj)
            dk_ref[:, pl.ds(off, B)] += _dot(qs_t[0], dzbs[0]) + _dot(qs_t[1], dzbs[1])
            dv_ref[:, pl.ds(off, B)] += _dot(do_t[0], ws[0].astype(BF16)) + _dot(do_t[1], ws[1].astype(BF16))

        def step(j, carry):
            block(j, False)
            return carry

        def last_two():
            has_prev = (i > 0).astype(F32)
            off_d = pl.multiple_of(i * B, B)
            off_o = pl.multiple_of(jnp.maximum(i - 1, 0) * B, B)
            k_d, v_d = k_ref[pl.ds(off_d, B), :], v_ref[pl.ds(off_d, B), :]
            k_o, v_o = k_ref[pl.ds(off_o, B), :], v_ref[pl.ds(off_o, B), :]
            hs = range(2)
            z_o = [_dot_nt(qss[h], k_o) for h in hs]
            z_d = [_dot_nt(qss[h], k_d) for h in hs]
            dw_o = [_dot_nt(dobs[h], v_o) for h in hs]
            dw_d = [_dot_nt(dobs[h], v_d) for h in hs]
            sp_o = [_softplus_neg_abs(z) for z in z_o]
            sp_d = [_softplus_neg_abs(z) for z in z_d]
            b_o = [(jnp.minimum(-z, 0.0) - sp) * has_prev for z, sp in zip(z_o, sp_o)]
            b_d = [jnp.where(tri, jnp.minimum(-z, 0.0) - sp, 0.0) for z, sp in zip(z_d, sp_d)]
            l_o = [_dot_acc(b, us_ref[...]) for b in b_o]
            l_d = [_dot_acc(b, us_ref[...]) for b in b_d]
            w_o, w_d = [], []
            for h in hs:
                pr1 = pr_ref[h] + jnp.broadcast_to(l_o[h][:, 0:1] + b_o[h][:, 0:1], (B, LANES))
                pr2 = pr1 + jnp.broadcast_to(l_d[h][:, 0:1] + b_d[h][:, 0:1], (B, LANES))
                st = st_ref[h]
                w_o.append(jnp.exp(jnp.minimum(z_o[h], 0.0) - sp_o[h] + (_lane_tile(st - pr1, B) + l_o[h])) * has_prev)
                w_d.append(jnp.where(tri, jnp.exp(jnp.minimum(z_d[h], 0.0) - sp_d[h]
                                                  + (_lane_tile(st - pr2, B) + l_d[h])), 0.0))
            e_o = [dw * w for dw, w in zip(dw_o, w_o)]
            e_d = [dw * w for dw, w in zip(dw_d, w_d)]
            ei_o = [_dot_acc(e, ti_ref[...]) for e in e_o]
            ei_d = [_dot_acc(e, ti_ref[...]) for e in e_d]
            dz_o, dz_d = [], []
            for h in hs:
                er = er_ref[h]
                big_o = _lane_tile(er, B) + (ei_o[h] - e_o[h])
                er1 = er + jnp.broadcast_to(ei_o[h][:, B - 1:B], (B, LANES))
                big_d = _lane_tile(er1, B) + (ei_d[h] - e_d[h])
                eb_o, eb_d = jnp.exp(b_o[h]), jnp.exp(b_d[h])
                dz_o.append((e_o[h] * eb_o - big_o * (1.0 - eb_o)).astype(BF16))
                dz_d.append((e_d[h] * eb_d - big_d * (1.0 - eb_d)).astype(BF16))
            for h in hs:
                dqa_ref[h] += _dot(dz_o[h], k_o) + _dot(dz_d[h], k_d)
            dk_ref[:, pl.ds(off_o, B)] += _dot(qs_t[0], dz_o[0]) + _dot(qs_t[1], dz_o[1])
            dv_ref[:, pl.ds(off_o, B)] += _dot(do_t[0], w_o[0].astype(BF16)) + _dot(do_t[1], w_o[1].astype(BF16))
            dk_ref[:, pl.ds(off_d, B)] += _dot(qs_t[0], dz_d[0]) + _dot(qs_t[1], dz_d[1])
            dv_ref[:, pl.ds(off_d, B)] += _dot(do_t[0], w_d[0].astype(BF16)) + _dot(do_t[1], w_d[1].astype(BF16))

        lax.fori_loop(js_ref[2 * p, i], i - 1, step, 0)
        last_two()
        dq_ref[...] = jnp.where(hms[0], dqa_ref[0], dqa_ref[1]) * SCALE

    W = n_pair * LANES
    return pl.pallas_call(
        body, name="sb_attn_bwd",
        grid_spec=pltpu.PrefetchScalarGridSpec(
            num_scalar_prefetch=1, grid=(n_pair, nq),
            in_specs=[pl.BlockSpec((B, LANES), lambda p, i, js: (i, p)),
                      pl.BlockSpec((S, LANES), lambda p, i, js: (0, n_pair + p)),
                      pl.BlockSpec((S, LANES), lambda p, i, js: (0, 2 * n_pair + p)),
                      pl.BlockSpec((B, LANES), lambda p, i, js: (i, p)),
                      pl.BlockSpec((2, B, LANES), lambda p, i, js: (p, i, 0)),
                      pl.BlockSpec((B, B), lambda p, i, js: (0, 0)),
                      pl.BlockSpec((B, B), lambda p, i, js: (0, 0))],
            out_specs=[pl.BlockSpec((B, LANES), lambda p, i, js: (i, p)),
                       pl.BlockSpec((LANES, S), lambda p, i, js: (p, 0)),
                       pl.BlockSpec((LANES, S), lambda p, i, js: (p, 0))],
            scratch_shapes=[pltpu.VMEM((2, B, LANES), F32), pltpu.VMEM((2, B, LANES), F32),
                            pltpu.VMEM((2, B, LANES), F32)]),
        out_shape=[jax.ShapeDtypeStruct((S, W), F32)] + [jax.ShapeDtypeStruct((W, S), F32)] * 2,
        compiler_params=_cparams(("arbitrary", "arbitrary")),
    )(js, qkv, qkv, qkv, do, st, us, ti)


def _head_column(blk, head):
    lane = lax.broadcasted_iota(I32, (1, LANES), 1)
    return jnp.sum(jnp.where(lane == head, blk, 0.0), axis=1, keepdims=True)


def _fox_fwd(qkv, c, c_rows, kmax, n_pair):
    S = qkv.shape[0]
    BQ, BK = FOX_BQ, FOX_BK
    R = BQ // BK
    nq = S // BQ

    def body(q_ref, k_ref, v_ref, c_ref, cr_ref, km_ref, o_ref, st_ref, js_ref, acc_ref, m_ref, cb_ref, qkb_ref):
        p, i, half = pl.program_id(0), pl.program_id(1), pl.program_id(2)
        hm = _half_mask(half)
        qv = q_ref[...]
        qs = jnp.where(hm, qv, jnp.zeros_like(qv)) * SCALE
        ccol = _head_column(c_ref[...], 2 * p + half)
        cb_ref[...] = jnp.broadcast_to(ccol, (BQ, BK))
        qf = qs.astype(F32)
        qkb_ref[...] = jnp.broadcast_to(
            jnp.sqrt(jnp.sum(qf * qf, axis=1, keepdims=True)) * NORM_SLACK
            * _head_column(km_ref[...], 2 * p + half) + ccol, (BQ, LANES))
        row = lax.broadcasted_iota(I32, (BQ, BK), 0)
        col = lax.broadcasted_iota(I32, (BQ, BK), 1)
        acc_ref[...] = jnp.zeros_like(acc_ref)
        m_ref[...] = jnp.full_like(m_ref, NEG_BIG)

        def blocks(j_top, diag):
            ss, v1s, keeps = [], [], []
            for d in range(R):
                j = j_top - d
                off = pl.multiple_of(j * BK, BK)
                vj = v_ref[pl.ds(off, BK), :]
                v1s.append(jnp.where(hm, vj, jnp.ones_like(vj)))
                s = _dot_nt(qs, k_ref[pl.ds(off, BK), :]) + (cb_ref[...] - cr_ref[0, pl.ds(j, 1), :])
                if diag:
                    keeps.append(col + (R - 1 - d) * BK <= row)
                    s = jnp.where(keeps[-1], s, NEG_BIG)
                ss.append(s)
            m_old = m_ref[...]
            s_max = jnp.max(functools.reduce(jnp.maximum, ss), axis=1, keepdims=True)
            m_new = jnp.maximum(m_old, jnp.broadcast_to(s_max, (BQ, LANES)))
            m_wide = _lane_tile(m_new, BK)
            pvs = [jnp.exp(s - m_wide) for s in ss]
            if diag:
                pvs = [jnp.where(keep, pv, 0.0) for keep, pv in zip(keeps, pvs)]
            new = _dot(pvs[0].astype(BF16), v1s[0])
            for pv, v1 in zip(pvs[1:], v1s[1:]):
                new = new + _dot(pv.astype(BF16), v1)
            acc_ref[...] = jnp.exp(m_old - m_new) * acc_ref[...] + new
            m_ref[...] = m_new

        def live(j):
            c_end = cr_ref[0, pl.ds(jnp.maximum(j, 0), 1), :][:, BK - 1:BK]
            return (jnp.max(qkb_ref[...] - c_end - m_ref[...]) > -EXP_ZERO).astype(I32)

        blocks(R * i + R - 1, True)

        def step(carry):
            j, _ = carry
            go_on = live(j - R)
            blocks(j, False)
            return j - R, go_on

        j_end, _ = lax.while_loop(lambda cr: (cr[0] >= 0) & (cr[1] > 0), step, (R * i - 1, live(R * i - 1)))
        js_ref[2 * p + half, i] = j_end + 1
        acc = acc_ref[...]
        denom = jnp.where(hm, pltpu.roll(acc, HEAD_DIM, 1), acc)
        res = jnp.where(hm, acc / denom, 0.0)

        @pl.when(half == 0)
        def _():
            o_ref[...] = res

        @pl.when(half == 1)
        def _():
            o_ref[...] += res

        st_ref[0] = m_ref[...] + jnp.log(denom)

    return pl.pallas_call(
        body, name="fox_attn_fwd", grid=(n_pair, nq, 2),
        in_specs=[pl.BlockSpec((BQ, LANES), lambda p, i, h: (i, 3 * n_pair + p)),
                  pl.BlockSpec((S, LANES), lambda p, i, h: (0, 4 * n_pair + p)),
                  pl.BlockSpec((S, LANES), lambda p, i, h: (0, 5 * n_pair + p)),
                  pl.BlockSpec((BQ, LANES), lambda p, i, h: (i, 0)),
                  pl.BlockSpec((1, S // BK, BK), lambda p, i, h: (2 * p + h, 0, 0)),
                  _full((1, LANES))],
        out_specs=[pl.BlockSpec((BQ, LANES), lambda p, i, h: (i, p)),
                   pl.BlockSpec((1, BQ, LANES), lambda p, i, h: (2 * p + h, i, 0)),
                   pl.BlockSpec(memory_space=pltpu.SMEM)],
        out_shape=[jax.ShapeDtypeStruct((S, n_pair * LANES), F32),
                   jax.ShapeDtypeStruct((2 * n_pair, S, LANES), F32),
                   jax.ShapeDtypeStruct((2 * n_pair, nq), I32)],
        scratch_shapes=[pltpu.VMEM((BQ, LANES), F32), pltpu.VMEM((BQ, LANES), F32), pltpu.VMEM((BQ, BK), F32),
                        pltpu.VMEM((BQ, LANES), F32)],
        compiler_params=_cparams(("arbitrary", "arbitrary", "arbitrary")),
    )(qkv, qkv, qkv, c, c_rows, kmax)


def _fox_bwd(qkv, do, o, st, c, c_rows, js, n_pair):
    S = qkv.shape[0]
    B = FOX_BLOCK
    nq = S // B

    def body(js_ref, q_ref, k_ref, v_ref, do_ref, o_ref, st_ref, c_ref, cr_ref, dq_ref, dk_ref, dv_ref,
             dc_ref, dqa_ref, rs_ref, cb_ref, db_ref):
        p, i, half = pl.program_id(0), pl.program_id(1), pl.program_id(2)

        @pl.when((i == 0) & (half == 0))
        def _():
            dk_ref[...] = jnp.zeros_like(dk_ref)
            dv_ref[...] = jnp.zeros_like(dv_ref)
            dc_ref[...] = jnp.zeros_like(dc_ref)

        hm = _half_mask(half)
        qv = q_ref[...]
        qs = jnp.where(hm, qv, jnp.zeros_like(qv)) * SCALE
        dov = jnp.where(hm, do_ref[...], 0.0)
        dob = dov.astype(BF16)
        qs_t = jnp.transpose(qs.astype(F32)).astype(BF16)
        do_t = jnp.transpose(dov).astype(BF16)
        cb_ref[...] = jnp.broadcast_to(_head_column(c_ref[...], 2 * p + half), (B, LANES)) - st_ref[0]
        db_ref[...] = jnp.broadcast_to(jnp.sum(dov * o_ref[...], axis=1, keepdims=True), (B, LANES))
        row = lax.broadcasted_iota(I32, (B, B), 0)
        col = lax.broadcasted_iota(I32, (B, B), 1)
        dqa_ref[...] = jnp.zeros_like(dqa_ref)
        rs_ref[...] = jnp.zeros_like(rs_ref)

        def block(j, diag):
            off = pl.multiple_of(j * B, B)
            kj = k_ref[pl.ds(off, B), :]
            vj = v_ref[pl.ds(off, B), :]
            pv = jnp.exp(_dot_nt(qs, kj) + (_lane_tile(cb_ref[...], B) - cr_ref[0, pl.ds(j, 1), :]))
            if diag:
                pv = jnp.where(col <= row, pv, 0.0)
            ds = pv * (_dot_nt(dob, vj) - _lane_tile(db_ref[...], B))
            dsb = ds.astype(BF16)
            dqa_ref[...] += _dot(dsb, kj)
            dk_ref[:, pl.ds(off, B)] += _dot(qs_t, dsb)
            dv_ref[:, pl.ds(off, B)] += _dot(do_t, pv.astype(BF16))
            dc_ref[0, half, pl.ds(j, 1), :] -= jnp.sum(ds, axis=0, keepdims=True)
            rs_ref[...] += jnp.sum(ds, axis=1, keepdims=True)

        def step(j, carry):
            block(j, False)
            return carry

        lax.fori_loop(js_ref[2 * p + half, i], i, step, 0)
        block(i, True)
        res = jnp.where(hm, dqa_ref[...] * SCALE, 0.0)

        @pl.when(half == 0)
        def _():
            dq_ref[...] = res

        @pl.when(half == 1)
        def _():
            dq_ref[...] += res

        dc_ref[0, half, pl.ds(i, 1), :] += jnp.transpose(jnp.broadcast_to(rs_ref[...], (B, LANES)))[0:1, :]

    W = n_pair * LANES
    return pl.pallas_call(
        body, name="fox_attn_bwd",
        grid_spec=pltpu.PrefetchScalarGridSpec(
            num_scalar_prefetch=1, grid=(n_pair, nq, 2),
            in_specs=[pl.BlockSpec((B, LANES), lambda p, i, h, js: (i, 3 * n_pair + p)),
                      pl.BlockSpec((S, LANES), lambda p, i, h, js: (0, 4 * n_pair + p)),
                      pl.BlockSpec((S, LANES), lambda p, i, h, js: (0, 5 * n_pair + p)),
                      pl.BlockSpec((B, LANES), lambda p, i, h, js: (i, p)),
                      pl.BlockSpec((B, LANES), lambda p, i, h, js: (i, p)),
                      pl.BlockSpec((1, B, LANES), lambda p, i, h, js: (2 * p + h, i, 0)),
                      pl.BlockSpec((B, LANES), lambda p, i, h, js: (i, 0)),
                      pl.BlockSpec((1, nq, B), lambda p, i, h, js: (2 * p + h, 0, 0))],
            out_specs=[pl.BlockSpec((B, LANES), lambda p, i, h, js: (i, p)),
                       pl.BlockSpec((LANES, S), lambda p, i, h, js: (p, 0)),
                       pl.BlockSpec((LANES, S), lambda p, i, h, js: (p, 0)),
                       pl.BlockSpec((1, 2, nq, B), lambda p, i, h, js: (p, 0, 0, 0))],
            scratch_shapes=[pltpu.VMEM((B, LANES), F32), pltpu.VMEM((B, 1), F32), pltpu.VMEM((B, LANES), F32),
                            pltpu.VMEM((B, LANES), F32)]),
        out_shape=[jax.ShapeDtypeStruct((S, W), F32)] + [jax.ShapeDtypeStruct((W, S), F32)] * 2
        + [jax.ShapeDtypeStruct((n_pair, 2, nq, B), F32)],
        compiler_params=_cparams(("arbitrary", "arbitrary", "arbitrary")),
    )(js, qkv, qkv, qkv, do, o, st, c, c_rows)


def _adam(w, g, m, v):
    m = ADAM_B1 * m + (1.0 - ADAM_B1) * g
    v = ADAM_B2 * v + (1.0 - ADAM_B2) * (g * g)
    m_hat = m / (1.0 - ADAM_B1 ** ADAM_STEP)
    v_hat = v / (1.0 - ADAM_B2 ** ADAM_STEP)
    delta = -ADAM_LR * (m_hat / (jnp.sqrt(v_hat) + ADAM_EPS) + ADAM_WD * w)
    return delta, m, v


def _reduce_adam(landing, w, m, v, name, extra=None):
    _, R, C = w.shape
    TR = next(t for t in (256, 128, R) if R % t == 0)
    more = [] if extra is None else [extra]

    def body(*refs):
        l_ref, w_ref, m_ref, v_ref = refs[:4]
        g_ref, d_ref, nm_ref, nv_ref = refs[4 + len(more):]
        g = l_ref[0].astype(F32)
        for s in range(1, N_DEV):
            g = g + l_ref[s].astype(F32)
        if more:
            g = g + refs[4][...]
        d, nm, nv = _adam(w_ref[0], g, m_ref[0], v_ref[0])
        g_ref[0] = g
        d_ref[0] = d
        nm_ref[0] = nm
        nv_ref[0] = nv

    blk = pl.BlockSpec((1, TR, C), lambda i: (0, i, 0))
    return pl.pallas_call(
        body, name=name, grid=(R // TR,),
        in_specs=[pl.BlockSpec((N_DEV, TR, C), lambda i: (0, i, 0)), blk, blk, blk]
        + [pl.BlockSpec((TR, C), lambda i: (i, 0))] * len(more),
        out_specs=[blk] * 4,
        out_shape=[jax.ShapeDtypeStruct((1, R, C), F32)] * 4,
        compiler_params=_cparams(("arbitrary",)),
    )(landing, w, m, v, *more)


def _reduce_adam_small(landing, w, m, v):
    R, C = w.shape

    def body(l_ref, w_ref, m_ref, v_ref, g_ref, d_ref, nm_ref, nv_ref, loss_ref):
        g = l_ref[0]
        for s in range(1, N_DEV):
            g = g + l_ref[s]
        d, nm, nv = _adam(w_ref[...], g, m_ref[...], v_ref[...])
        g_ref[...] = g
        d_ref[...] = d
        nm_ref[...] = nm
        nv_ref[...] = nv
        loss_ref[...] = jnp.broadcast_to(0.5 * jnp.sum(g[7:8, :], axis=1, keepdims=True), (1, LANES))

    return pl.pallas_call(
        body, name="reduce_adam_small",
        out_shape=[jax.ShapeDtypeStruct((R, C), F32)] * 4 + [jax.ShapeDtypeStruct((1, LANES), F32)],
    )(landing, w, m, v)


def _pad_lanes(a, width):
    return jnp.pad(a, ((0, 0), (0, width - a.shape[1])))


def _pack_small(D, n_fox, g_cat, l1g, l1b, l2g, l2b, bf, last, tail):
    return jnp.concatenate([g_cat, l1g, l1b, l2g, l2b, _pad_lanes(bf, D), jnp.zeros((1, D), F32), last, tail],
                           axis=0)


def kernel(x, w_in, b_f, g_sb, g_fox, w_out, ln1_g, ln1_b, ln2_g, ln2_b, w_gate_up, w_down, loss_target, m_w_in, m_b_f, m_g_sb, m_g_fox, m_w_out, m_ln1_g, m_ln1_b, m_ln2_g, m_ln2_b, m_w_gate_up, m_w_down, v_w_in, v_b_f, v_g_sb, v_g_fox, v_w_out, v_ln1_g, v_ln1_b, v_ln2_g, v_ln2_b, v_w_gate_up, v_w_down):
    x2, tgt = x[0], loss_target[0]
    S, D = x2.shape
    W = D // 2
    n_pair = W // LANES
    n_fox = W // HEAD_DIM
    F = w_down.shape[1] * N_DEV

    g_in = _all_gather_two_level(w_in[0].astype(BF16), "w_in_all_gather")
    w_in_full = g_in.transpose(1, 0, 2).reshape(D, -1)
    w_qkv = w_in_full[:, :6 * W]
    w_f = _pad_lanes(w_in_full[:, 6 * W:], LANES)
    gmat = _tri(D, lambda r, c: (r // HEAD_DIM) == (c // HEAD_DIM))
    g_cat = jnp.concatenate([g_sb, g_fox], axis=1)

    qkv, u, c, ksq = _proj_fwd(x2, w_qkv, w_f, _pad_lanes(b_f, LANES), n_fox)
    c_t = c[:, :n_fox].T
    c_rows = c_t.reshape(n_fox, S // FOX_BLOCK, FOX_BLOCK)
    kmax = jnp.sqrt(ksq[0:1]) * NORM_SLACK
    o_sb, st_sb, js_sb, g_out, g_gu, g_dn = _sb_fwd(
        qkv, n_pair, [w_out[0].astype(BF16), w_gate_up[0].astype(BF16), w_down[0].astype(BF16)])
    w_o = g_out.reshape(D, D)
    w_gu = g_gu.transpose(1, 0, 2).reshape(D, 2 * F)
    w_dn = g_dn.reshape(F, D)
    o_fx, st_fx, js_fx = _fox_fwd(qkv, c, c_t.reshape(n_fox, S // FOX_BK, FOX_BK), kmax, n_pair)
    js_fx = js_fx // (FOX_BLOCK // FOX_BK)
    h1, xh1, rs1, on_b, rr = _post_attn_fwd(o_sb, o_fx, x2, g_cat, gmat, w_o, ln1_g, ln1_b)
    gu, act_b, dyp, sm2 = _mlp_fwd(h1, tgt, w_gu, w_dn, ln2_g, ln2_b)

    dgu_b, dh1 = _mlp_bwd(gu, dyp, w_gu.T, w_dn.T)
    gw_gu = _matmul_tn(h1, [dgu_b], "grad_w_gate_up", n_split=2)
    gw_dn = _matmul_tn(act_b, [dyp], "grad_w_down")
    dxa, dmix_b, do_sb, do_fx, sm1, l_gu, l_dn = _post_attn_bwd(
        dh1, xh1, rs1, ln1_g, o_sb, o_fx, rr, g_cat, gmat, w_o.T,
        [gw_gu.reshape(D, N_DEV, -1).transpose(1, 0, 2), gw_dn.reshape(N_DEV, F // N_DEV, D)])
    dq_sb, dk_sb, dv_sb = _sb_bwd(qkv, do_sb, st_sb, js_sb, n_pair)
    dq_fx, dk_fx, dv_fx, dcr = _fox_bwd(qkv, do_fx, o_fx, st_fx, c, c_rows, js_fx, n_pair)
    dc = _pad_lanes(dcr.reshape(n_fox, S).T, LANES)
    pieces = [dq_sb, dk_sb, dv_sb, dq_fx, dk_fx, dv_fx]
    piece_t = (False, True, True, False, True, True)
    gw_qkv = _matmul_tn(x2, pieces, "grad_w_qkv", pad_cols=LANES, b_t=piece_t)[:, :6 * W + n_fox]
    gw_out = _matmul_tn(on_b, [dmix_b], "grad_w_out")
    dx, gw_f_t, sm0, l_in, l_out = _proj_bwd(
        dxa, x2, pieces, piece_t, dc, u, w_qkv.T, w_f.T, n_fox,
        [gw_qkv.reshape(D, N_DEV, -1).transpose(1, 0, 2), gw_out.reshape(N_DEV, D // N_DEV, D)])

    small = _pack_small(D, n_fox, sm1[2:3], sm1[0:1], sm1[1:2], sm2[0:1], sm2[1:2], sm0[0:1, :n_fox],
                        sm2[2:3] * (1.0 / D), gw_f_t[:n_fox])
    (l_small,) = _exchange_call([jnp.broadcast_to(small[None], (N_DEV,) + small.shape)], True, "small_exchange")

    zero = jnp.zeros((1, D), F32)
    pack = lambda gc, a, b_, c_, d_, bf: _pack_small(D, n_fox, gc, a, b_, c_, d_, bf, zero,
                                                     jnp.zeros((n_fox, D), F32))
    r_small = _reduce_adam_small(
        l_small,
        pack(g_cat, ln1_g, ln1_b, ln2_g, ln2_b, b_f),
        pack(jnp.concatenate([m_g_sb, m_g_fox], axis=1), m_ln1_g, m_ln1_b, m_ln2_g, m_ln2_b, m_b_f),
        pack(jnp.concatenate([v_g_sb, v_g_fox], axis=1), v_ln1_g, v_ln1_b, v_ln2_g, v_ln2_b, v_b_f))
    loss = r_small[4][0, 0]
    cols = w_in.shape[2]
    gf_cols = jnp.pad(r_small[0][8:8 + n_fox].T, ((0, 0), (cols - n_fox, 0)))
    extra = jnp.where(_my_index() == N_DEV - 1, gf_cols, 0.0)
    r_in = _reduce_adam(l_in, w_in, m_w_in, v_w_in, "reduce_adam_w_in", extra)
    r_out = _reduce_adam(l_out, w_out, m_w_out, v_w_out, "reduce_adam_w_out")
    r_gu = _reduce_adam(l_gu, w_gate_up, m_w_gate_up, v_w_gate_up, "reduce_adam_w_gate_up")
    r_dn = _reduce_adam(l_dn, w_down, m_w_down, v_w_down, "reduce_adam_w_down")

    def unpack(kind):
        big = [r_in[kind], None, None, None, r_out[kind], None, None, None, None, r_gu[kind], r_dn[kind]]
        s = r_small[kind]
        big[1] = s[5:6, :n_fox]
        big[2] = s[0:1, :W]
        big[3] = s[0:1, W:]
        big[5], big[6], big[7], big[8] = s[1:2], s[2:3], s[3:4], s[4:5]
        return big

    return (loss, dx[None], *unpack(0), *unpack(1), *unpack(2), *unpack(3))
```

```python
import functools

import jax
import jax.numpy as jnp
from jax import lax
from jax.experimental import pallas as pl
from jax.experimental.pallas import tpu as pltpu

F32 = jnp.float32
BF16 = jnp.bfloat16
I32 = jnp.int32

N_DEV = 8
HEAD_DIM = 64
LANES = 128
SCALE = HEAD_DIM ** -0.5
ALPHA = 2.0 ** 0.25
LN_EPS = 1e-5
RMS_EPS = 1e-6
ADAM_LR, ADAM_B1, ADAM_B2, ADAM_EPS, ADAM_WD, ADAM_STEP = 0.001, 0.9, 0.999, 1e-08, 0.01, 10
NEG_BIG = -1e30
NORM_SLACK = 1.01
EXP_ZERO = 88.5
VMEM_LIMIT = 60 * 1024 * 1024
ROW_TILE = 256
SB_BLOCK = 256
FOX_BLOCK = 512
FOX_BQ, FOX_BK = 512, 256
assert FOX_BQ == FOX_BLOCK and FOX_BLOCK % FOX_BK == 0
MESH = pl.DeviceIdType.MESH


def _cparams(sem):
    return pltpu.CompilerParams(dimension_semantics=sem, vmem_limit_bytes=VMEM_LIMIT)


def _dot(a, b):
    return jnp.dot(a, b, preferred_element_type=F32)


def _dot_nt(a, b):
    return lax.dot_general(a, b, (((1,), (1,)), ((), ())), preferred_element_type=F32)


def _dot_tn(a, b):
    return lax.dot_general(a, b, (((0,), (0,)), ((), ())), preferred_element_type=F32)


def _split2(a):
    hi = a.astype(BF16)
    lo = (a - hi.astype(F32)).astype(BF16)
    return hi, lo


def _split3(a):
    hi = a.astype(BF16)
    r1 = a - hi.astype(F32)
    mid = r1.astype(BF16)
    lo = (r1 - mid.astype(F32)).astype(BF16)
    return hi, mid, lo


def _dot_acc(a, m):
    hi, lo = _split2(a)
    return _dot(jnp.concatenate([hi, lo], axis=1), jnp.concatenate([m, m], axis=0))


def _tri(n, fn):
    r = lax.broadcasted_iota(I32, (n, n), 0)
    c = lax.broadcasted_iota(I32, (n, n), 1)
    return fn(r, c).astype(BF16)


def _full(shape):
    nd = len(shape)
    return pl.BlockSpec(shape, lambda *_: (0,) * nd)


def _peer(k):
    x, y, c = lax.axis_index("x"), lax.axis_index("y"), lax.axis_index("c")
    return (1 - x if k & 4 else x, 1 - y if k & 2 else y, 1 - c if k & 1 else c)


def _my_index():
    return 4 * lax.axis_index("x") + 2 * lax.axis_index("y") + lax.axis_index("c")


class _Exchange:
    def __init__(self, n, scatter):
        self.n, self.scatter = n, scatter

    def sem_shapes(self):
        return [pltpu.SemaphoreType.DMA(((N_DEV - 1) * self.n,)), pltpu.SemaphoreType.DMA(((N_DEV - 1) * self.n,)),
                pltpu.SemaphoreType.DMA((self.n,))]

    def out_shapes(self, arrays):
        if self.scatter:
            return [jax.ShapeDtypeStruct(s.shape, s.dtype) for s in arrays]
        return [jax.ShapeDtypeStruct((N_DEV,) + s.shape, s.dtype) for s in arrays]

    def _copies(self, ins, outs, sems, landing):
        send_sems, recv_sems, local_sems = sems
        me = _my_index()
        src = lambda a, d: ins[a].at[d] if self.scatter else ins[a]
        local = [pltpu.make_async_copy(src(a, me), outs[a].at[me], local_sems.at[a]) for a in range(self.n)]
        remote = [pltpu.make_async_remote_copy(
            src_ref=src(a, me ^ k), dst_ref=outs[a].at[me ^ k if landing else me],
            send_sem=send_sems.at[(k - 1) * self.n + a], recv_sem=recv_sems.at[(k - 1) * self.n + a],
            device_id=_peer(k), device_id_type=MESH) for k in range(1, N_DEV) for a in range(self.n)]
        return local, remote

    def start(self, ins, outs, sems):
        local, sent = self._copies(ins, outs, sems, landing=False)
        for cp in local + sent:
            cp.start()

    def finish(self, ins, outs, sems):
        local, landed = self._copies(ins, outs, sems, landing=True)
        for cp in landed:
            cp.wait_recv()
        for cp in landed:
            cp.wait_send()
        for cp in local:
            cp.wait()


def _all_gather_two_level(shard, name):
    def body(x_ref, out_ref, send_sems, recv_sems, local_sem):
        x, y, c = lax.axis_index("x"), lax.axis_index("y"), lax.axis_index("c")
        me, sibling = (x, y, c), (x, y, 1 - c)
        chips = [(1 - x, y), (x, 1 - y), (1 - x, 1 - y)]

        def slot(px, py, pc):
            return out_ref.at[4 * px + 2 * py + pc]

        def copy(k, block, to, src=None):
            return pltpu.make_async_remote_copy(
                src_ref=slot(*block) if src is None else src, dst_ref=slot(*block),
                send_sem=send_sems.at[k], recv_sem=recv_sems.at[k], device_id=to, device_id_type=MESH)

        mine = pltpu.make_async_copy(x_ref, slot(*me), local_sem)
        mine.start()
        first = [copy(0, me, sibling, src=x_ref)]
        first += [copy(1 + j, me, (*chip, c), src=x_ref) for j, chip in enumerate(chips)]
        for cp in first:
            cp.start()
        passed = [copy(4 + j, (*chip, c), sibling) for j, chip in enumerate(chips)]
        for j, chip in enumerate(chips):
            copy(1 + j, (*chip, c), me).wait_recv()
            passed[j].start()
        copy(0, sibling, me).wait_recv()
        for j, chip in enumerate(chips):
            copy(4 + j, (*chip, 1 - c), me).wait_recv()
        for cp in first + passed:
            cp.wait_send()
        mine.wait()

    any_spec = pl.BlockSpec(memory_space=pl.ANY)
    return pl.pallas_call(
        body, name=name, out_shape=jax.ShapeDtypeStruct((N_DEV,) + shard.shape, shard.dtype),
        in_specs=[any_spec], out_specs=any_spec,
        scratch_shapes=[pltpu.SemaphoreType.DMA((N_DEV - 1,)), pltpu.SemaphoreType.DMA((N_DEV - 1,)),
                        pltpu.SemaphoreType.DMA],
    )(shard)


def _exchange_call(arrays, scatter, name):
    n = len(arrays)
    ex = _Exchange(n, scatter)

    def body(*refs):
        ins, outs, sems = refs[:n], refs[n:2 * n], refs[2 * n:]
        ex.start(ins, outs, sems)
        ex.finish(ins, outs, sems)

    any_spec = pl.BlockSpec(memory_space=pl.ANY)
    return pl.pallas_call(
        body, name=name, out_shape=ex.out_shapes(arrays),
        in_specs=[any_spec] * n, out_specs=[any_spec] * n, scratch_shapes=ex.sem_shapes(),
    )(*arrays)


def _log_sigmoid(u):
    return jnp.minimum(u, 0.0) - jnp.log1p(jnp.exp(-jnp.abs(u)))


def _proj_fwd(x, w_qkv, w_f, bf_pad, n_fox):
    S, D = x.shape
    N = w_qkv.shape[1]
    W = D // 2
    TM = ROW_TILE
    tri = _tri(TM, lambda r, c: c <= r)
    r_ = lax.broadcasted_iota(I32, (W, LANES), 0)
    c_ = lax.broadcasted_iota(I32, (W, LANES), 1)
    head_of = (r_ // HEAD_DIM == c_).astype(BF16)

    def body(x_ref, wq_ref, wf_ref, bf_ref, tri_ref, ho_ref, qkv_ref, u_ref, c_ref, ksq_ref, run_ref):
        @pl.when(pl.program_id(0) == 0)
        def _():
            run_ref[...] = jnp.zeros_like(run_ref)
            ksq_ref[...] = jnp.zeros_like(ksq_ref)

        xb = x_ref[...].astype(BF16)
        for n0 in range(0, N, D):
            chunk = _dot(xb, wq_ref[:, n0:n0 + D]).astype(BF16)
            qkv_ref[:, n0:n0 + D] = chunk
            if n0 == 4 * W:
                kf = chunk[:, :W].astype(F32)
                ksq = jnp.max(_dot_acc(kf * kf, ho_ref[...]), axis=0, keepdims=True)
                ksq_ref[...] = jnp.maximum(ksq_ref[...], ksq)
        u = _dot(xb, wf_ref[...]) + bf_ref[...]
        lane = lax.broadcasted_iota(I32, u.shape, 1)
        logf = jnp.where(lane < n_fox, _log_sigmoid(u), 0.0)
        u_ref[...] = u
        hi, mid, lo = _split3(logf)
        t = tri_ref[...]
        cs = _dot(t, hi) + _dot(t, mid) + _dot(t, lo) + run_ref[...]
        c_ref[...] = cs
        run_ref[...] = cs[TM - 1:TM, :]

    return pl.pallas_call(
        body, name="proj_fwd", grid=(S // TM,),
        in_specs=[pl.BlockSpec((TM, D), lambda i: (i, 0)), _full(w_qkv.shape), _full(w_f.shape),
                  _full(bf_pad.shape), _full(tri.shape), _full(head_of.shape)],
        out_specs=[pl.BlockSpec((TM, N), lambda i: (i, 0)), pl.BlockSpec((TM, LANES), lambda i: (i, 0)),
                   pl.BlockSpec((TM, LANES), lambda i: (i, 0)), _full((8, LANES))],
        out_shape=[jax.ShapeDtypeStruct((S, N), BF16), jax.ShapeDtypeStruct((S, LANES), F32),
                   jax.ShapeDtypeStruct((S, LANES), F32), jax.ShapeDtypeStruct((8, LANES), F32)],
        scratch_shapes=[pltpu.VMEM((1, LANES), F32)],
        compiler_params=_cparams(("arbitrary",)),
    )(x, w_qkv, w_f, bf_pad, tri, head_of)


def _post_attn_fwd(o_sb, o_fx, x, g_cat, g_down, g_up, w_out, ln_g, ln_b):
    S, D = x.shape
    H = D // 2
    TM = ROW_TILE

    def body(osb_ref, ofx_ref, x_ref, g_ref, gd_ref, gu_ref, wo_ref, lg_ref, lb_ref,
             h1_ref, xh_ref, rs_ref, on_ref, rr_ref):
        o = jnp.concatenate([osb_ref[...], ofx_ref[...]], axis=1)
        ms = _dot_acc(o * o, gd_ref[...]) * (1.0 / HEAD_DIM)
        r = _dot_acc(lax.rsqrt(ms + RMS_EPS), gu_ref[...])
        onb = (o * r * g_ref[...]).astype(BF16)
        hp = ALPHA * x_ref[...] + _dot(onb, wo_ref[...])
        mu = jnp.mean(hp, axis=-1, keepdims=True)
        d = hp - mu
        rstd = lax.rsqrt(jnp.mean(d * d, axis=-1, keepdims=True) + LN_EPS)
        xh = d * rstd
        h1_ref[...] = xh * lg_ref[...] + lb_ref[...]
        xh_ref[...] = xh
        rs_ref[...] = jnp.broadcast_to(rstd, (TM, LANES))
        on_ref[...] = onb
        rr_ref[...] = r

    row = lambda w: pl.BlockSpec((TM, w), lambda i: (i, 0))
    return pl.pallas_call(
        body, name="post_attn_fwd", grid=(S // TM,),
        in_specs=[row(H), row(H), row(D), _full((1, D)), _full((D, LANES)), _full((LANES, D)), _full((D, D)),
                  _full((1, D)), _full((1, D))],
        out_specs=[row(D), row(D), row(LANES), row(D), row(D)],
        out_shape=[jax.ShapeDtypeStruct((S, D), F32), jax.ShapeDtypeStruct((S, D), F32),
                   jax.ShapeDtypeStruct((S, LANES), F32), jax.ShapeDtypeStruct((S, D), BF16),
                   jax.ShapeDtypeStruct((S, D), F32)],
        compiler_params=_cparams(("arbitrary",)),
    )(o_sb, o_fx, x, g_cat, g_down, g_up, w_out, ln_g, ln_b)


def _ln_bwd(dxh, xh, rstd):
    m1 = jnp.mean(dxh, axis=-1, keepdims=True)
    m2 = jnp.mean(dxh * xh, axis=-1, keepdims=True)
    return rstd * (dxh - m1 - xh * m2)


def _mlp_fwd(h1, target, w_gu, w_dn, ln_g, ln_b):
    S, D = h1.shape
    F = w_dn.shape[0]
    TM = ROW_TILE
    FC = F // 2

    def body(h1_ref, tg_ref, wgu_hbm, wdn_hbm, lg_ref, lb_ref, gu_ref, act_ref, dyp_ref, sm_ref, wgu, wdn):
        @pl.when(pl.program_id(0) == 0)
        def _():
            pltpu.sync_copy(wgu_hbm, wgu)
            pltpu.sync_copy(wdn_hbm, wdn)
            sm_ref[...] = jnp.zeros_like(sm_ref)

        h1v = h1_ref[...]
        hb = h1v.astype(BF16)
        ff = jnp.zeros((TM, D), F32)
        for c0 in range(0, F, FC):
            g = _dot(hb, wgu[:, c0:c0 + FC])
            u = _dot(hb, wgu[:, F + c0:F + c0 + FC])
            gu_ref[:, c0:c0 + FC] = g
            gu_ref[:, F + c0:F + c0 + FC] = u
            ab = ((g * jax.nn.sigmoid(g)) * u).astype(BF16)
            act_ref[:, c0:c0 + FC] = ab
            ff = ff + _dot(ab, wdn[c0:c0 + FC, :])
        yp = ALPHA * h1v + ff
        mu = jnp.mean(yp, axis=-1, keepdims=True)
        d = yp - mu
        rstd = lax.rsqrt(jnp.mean(d * d, axis=-1, keepdims=True) + LN_EPS)
        xh = d * rstd
        err = (xh * lg_ref[...] + lb_ref[...]) - tg_ref[...]
        dy = err * (1.0 / D)
        sm_ref[0:1, :] += jnp.sum(dy * xh, axis=0, keepdims=True)
        sm_ref[1:2, :] += jnp.sum(dy, axis=0, keepdims=True)
        sm_ref[2:3, :] += jnp.sum(err * err, axis=0, keepdims=True)
        dyp_ref[...] = _ln_bwd(dy * lg_ref[...], xh, rstd)

    row = lambda w: pl.BlockSpec((TM, w), lambda i: (i, 0))
    hbm = pl.BlockSpec(memory_space=pl.ANY)
    return pl.pallas_call(
        body, name="mlp_fwd", grid=(S // TM,),
        in_specs=[row(D), row(D), hbm, hbm, _full((1, D)), _full((1, D))],
        out_specs=[row(2 * F), row(F), row(D), _full((8, D))],
        out_shape=[jax.ShapeDtypeStruct((S, 2 * F), F32), jax.ShapeDtypeStruct((S, F), BF16),
                   jax.ShapeDtypeStruct((S, D), F32), jax.ShapeDtypeStruct((8, D), F32)],
        scratch_shapes=[pltpu.VMEM(w_gu.shape, BF16), pltpu.VMEM(w_dn.shape, BF16)],
        compiler_params=_cparams(("arbitrary",)),
    )(h1, target, w_gu, w_dn, ln_g, ln_b)


def _mlp_bwd(gu, dyp, w_guT, w_dnT):
    S, D = dyp.shape
    F = w_dnT.shape[1]
    TM = ROW_TILE
    FC = F // 2

    def body(gu_ref, dyp_ref, wguT_hbm, wdnT_hbm, dgu_ref, dh1_ref, wguT, wdnT):
        @pl.when(pl.program_id(0) == 0)
        def _():
            pltpu.sync_copy(wguT_hbm, wguT)
            pltpu.sync_copy(wdnT_hbm, wdnT)

        dypv = dyp_ref[...]
        db = dypv.astype(BF16)
        dh1 = ALPHA * dypv
        for c0 in range(0, F, FC):
            dact = _dot(db, wdnT[:, c0:c0 + FC])
            g = gu_ref[:, c0:c0 + FC]
            u = gu_ref[:, F + c0:F + c0 + FC]
            sg = jax.nn.sigmoid(g)
            dgb = (dact * u * (sg * (1.0 + g * (1.0 - sg)))).astype(BF16)
            dub = (dact * (g * sg)).astype(BF16)
            dgu_ref[:, c0:c0 + FC] = dgb
            dgu_ref[:, F + c0:F + c0 + FC] = dub
            dh1 = dh1 + _dot(dgb, wguT[c0:c0 + FC, :]) + _dot(dub, wguT[F + c0:F + c0 + FC, :])
        dh1_ref[...] = dh1

    row = lambda w: pl.BlockSpec((TM, w), lambda i: (i, 0))
    hbm = pl.BlockSpec(memory_space=pl.ANY)
    return pl.pallas_call(
        body, name="mlp_bwd", grid=(S // TM,),
        in_specs=[row(2 * F), row(D), hbm, hbm],
        out_specs=[row(2 * F), row(D)],
        out_shape=[jax.ShapeDtypeStruct((S, 2 * F), BF16), jax.ShapeDtypeStruct((S, D), F32)],
        scratch_shapes=[pltpu.VMEM(w_guT.shape, BF16), pltpu.VMEM(w_dnT.shape, BF16)],
        compiler_params=_cparams(("arbitrary",)),
    )(gu, dyp, w_guT, w_dnT)


def _post_attn_bwd(dh1, xh, rs, ln_g, o_sb, o_fx, rr, g_cat, g_down, g_up, w_outT, chunked):
    S, D = dh1.shape
    H = D // 2
    TM = ROW_TILE
    nT = S // TM
    n_ch = len(chunked)
    ex = _Exchange(n_ch, scatter=True)

    def body(*refs):
        dh1_ref, xh_ref, rs_ref, lg_ref, osb_ref, ofx_ref, rr_ref, g_ref, gd_ref, gu_ref, woT_ref = refs[:11]
        ch_in = refs[11:11 + n_ch]
        dxa_ref, dmix_ref, dosb_ref, dofx_ref, sm_ref = refs[11 + n_ch:16 + n_ch]
        ch_out = refs[16 + n_ch:16 + 2 * n_ch]
        sems = refs[16 + 2 * n_ch:]

        @pl.when(pl.program_id(0) == 0)
        def _():
            ex.start(ch_in, ch_out, sems)
            sm_ref[...] = jnp.zeros_like(sm_ref)

        dh = dh1_ref[...]
        xhv = xh_ref[...]
        dhp = _ln_bwd(dh * lg_ref[...], xhv, rs_ref[:, 0:1])
        dxa_ref[...] = ALPHA * dhp
        dmb = dhp.astype(BF16)
        dmix_ref[...] = dmb
        don = _dot(dmb, woT_ref[...])
        o = jnp.concatenate([osb_ref[...], ofx_ref[...]], axis=1)
        r = rr_ref[...]
        u = don * g_ref[...]
        t = _dot_acc(_dot_acc(u * o, gd_ref[...]) * (1.0 / HEAD_DIM), gu_ref[...])
        do = r * u - o * (r * r * r) * t
        dosb_ref[...] = do[:, :H]
        dofx_ref[...] = do[:, H:]
        sm_ref[0:1, :] += jnp.sum(dh * xhv, axis=0, keepdims=True)
        sm_ref[1:2, :] += jnp.sum(dh, axis=0, keepdims=True)
        sm_ref[2:3, :] += jnp.sum(don * o * r, axis=0, keepdims=True)

        @pl.when(pl.program_id(0) == nT - 1)
        def _():
            ex.finish(ch_in, ch_out, sems)

    row = lambda w: pl.BlockSpec((TM, w), lambda i: (i, 0))
    any_spec = pl.BlockSpec(memory_space=pl.ANY)
    return pl.pallas_call(
        body, name="post_attn_bwd", grid=(nT,),
        in_specs=[row(D), row(D), row(LANES), _full((1, D)), row(H), row(H), row(D), _full((1, D)),
                  _full((D, LANES)), _full((LANES, D)), _full((D, D))] + [any_spec] * n_ch,
        out_specs=[row(D), row(D), row(H), row(H), _full((8, D))] + [any_spec] * n_ch,
        out_shape=[jax.ShapeDtypeStruct((S, D), F32), jax.ShapeDtypeStruct((S, D), BF16),
                   jax.ShapeDtypeStruct((S, H), F32), jax.ShapeDtypeStruct((S, H), F32),
                   jax.ShapeDtypeStruct((8, D), F32)] + ex.out_shapes(chunked),
        scratch_shapes=ex.sem_shapes(),
        compiler_params=_cparams(("arbitrary",)),
    )(dh1, xh, rs, ln_g, o_sb, o_fx, rr, g_cat, g_down, g_up, w_outT, *chunked)


def _proj_bwd(dxa, x, pieces, piece_t, dc, u, w_qkvT, w_fT, n_fox, chunked):
    S, D = dxa.shape
    H = D // 2
    TM = ROW_TILE
    nT = S // TM
    tri = _tri(TM, lambda r, c: c >= r)
    n_p = len(pieces)
    n_ch = len(chunked)
    ex = _Exchange(n_ch, scatter=True)

    def body(*refs):
        dxa_ref, x_ref = refs[:2]
        p_refs = refs[2:2 + n_p]
        dc_ref, u_ref, wq_ref, wf_ref, tri_ref = refs[2 + n_p:7 + n_p]
        ch_in = refs[7 + n_p:7 + n_p + n_ch]
        dx_ref, gwf_ref, sm_ref = refs[7 + n_p + n_ch:10 + n_p + n_ch]
        ch_out = refs[10 + n_p + n_ch:10 + n_p + 2 * n_ch]
        run_ref = refs[10 + n_p + 2 * n_ch]
        sems = refs[11 + n_p + 2 * n_ch:]

        @pl.when(pl.program_id(0) == 0)
        def _():
            ex.start(ch_in, ch_out, sems)
            run_ref[...] = jnp.zeros_like(run_ref)
            sm_ref[...] = jnp.zeros_like(sm_ref)
            gwf_ref[...] = jnp.zeros_like(gwf_ref)

        hi, mid, lo = _split3(dc_ref[...])
        t = tri_ref[...]
        dlogf = _dot(t, hi) + _dot(t, mid) + _dot(t, lo) + run_ref[...]
        run_ref[...] = dlogf[0:1, :]
        uv = u_ref[...]
        lane = lax.broadcasted_iota(I32, uv.shape, 1)
        df = jnp.where(lane < n_fox, dlogf * jax.nn.sigmoid(-uv), 0.0)
        sm_ref[0:1, :] += jnp.sum(df, axis=0, keepdims=True)
        dfb = df.astype(BF16)
        gwf_ref[...] += _dot(jnp.transpose(df).astype(BF16), x_ref[...].astype(BF16))
        acc = dxa_ref[...] + _dot(dfb, wf_ref[...])
        for a in range(n_p):
            mm = _dot_tn if piece_t[a] else _dot
            acc = acc + mm(p_refs[a][...].astype(BF16), wq_ref[a * H:(a + 1) * H, :])
        dx_ref[...] = acc

        @pl.when(pl.program_id(0) == nT - 1)
        def _():
            ex.finish(ch_in, ch_out, sems)

    rev = lambda w: pl.BlockSpec((TM, w), lambda i: (nT - 1 - i, 0))
    any_spec = pl.BlockSpec(memory_space=pl.ANY)
    return pl.pallas_call(
        body, name="proj_bwd", grid=(nT,),
        in_specs=[rev(D), rev(D)]
        + [pl.BlockSpec((H, TM), lambda i: (0, nT - 1 - i)) if t else rev(H) for t in piece_t]
        + [rev(LANES), rev(LANES), _full(w_qkvT.shape), _full(w_fT.shape), _full(tri.shape)] + [any_spec] * n_ch,
        out_specs=[rev(D), _full((LANES, D)), _full((8, LANES))] + [any_spec] * n_ch,
        out_shape=[jax.ShapeDtypeStruct((S, D), F32), jax.ShapeDtypeStruct((LANES, D), F32),
                   jax.ShapeDtypeStruct((8, LANES), F32)] + ex.out_shapes(chunked),
        scratch_shapes=[pltpu.VMEM((1, LANES), F32)] + ex.sem_shapes(),
        compiler_params=_cparams(("arbitrary",)),
    )(dxa, x, *pieces, dc, u, w_qkvT, w_fT, tri, *chunked)


def _matmul_tn(a, bs, name, n_split=1, pad_cols=0, b_t=None):
    S, M = a.shape
    b_t = tuple(b_t) if b_t is not None else (False,) * len(bs)
    widths = [b.shape[0] if t else b.shape[1] for b, t in zip(bs, b_t)]
    N = sum(widths) + pad_cols
    assert n_split == 1 or (len(bs) == 1 and pad_cols == 0 and not b_t[0])
    TK = 512 if S % 512 == 0 else ROW_TILE
    MC = 512 if M % 512 == 0 else 256
    nb = len(bs)

    def body(*refs):
        a_ref, b_refs, o_ref, acc_ref = refs[0], refs[1:1 + nb], refs[1 + nb], refs[2 + nb]
        t_refs = list(refs[3 + nb:])

        @pl.when(pl.program_id(1) == 0)
        def _():
            acc_ref[...] = jnp.zeros_like(acc_ref)
            for t_ref in t_refs:
                t_ref[...] = jnp.zeros_like(t_ref)

        n0 = 0
        turned = []
        for b_ref, t, w in zip(b_refs, b_t, widths):
            bv = b_ref[...].astype(BF16)
            if t:
                t_ref = t_refs[len(turned)]
                t_ref[...] += _dot(bv, a_ref[...].astype(BF16))
                turned.append((t_ref, n0, w))
            else:
                for m0 in range(0, M, MC):
                    acc_ref[m0:m0 + MC, n0:n0 + w] += _dot_tn(a_ref[:, m0:m0 + MC].astype(BF16), bv)
            n0 += w

        @pl.when(pl.program_id(1) == S // TK - 1)
        def _():
            for t_ref, c0, w in turned:
                acc_ref[:, c0:c0 + w] = jnp.transpose(t_ref[...])
            o_ref[...] = acc_ref[...].astype(BF16)

    return pl.pallas_call(
        body, name=name, grid=(n_split, S // TK),
        in_specs=[pl.BlockSpec((TK, M), lambda n, k: (k, 0))]
        + [pl.BlockSpec((w, TK), lambda n, k: (0, k)) if t else pl.BlockSpec((TK, w // n_split), lambda n, k: (k, n))
           for w, t in zip(widths, b_t)],
        out_specs=pl.BlockSpec((M, N // n_split), lambda n, k: (0, n)),
        out_shape=jax.ShapeDtypeStruct((M, N), BF16),
        scratch_shapes=[pltpu.VMEM((M, N // n_split), F32)]
        + [pltpu.VMEM((w, M), F32) for w, t in zip(widths, b_t) if t],
        compiler_params=_cparams(("arbitrary", "arbitrary")),
    )(a, *bs)


def _half_mask(half):
    lane = lax.broadcasted_iota(I32, (1, LANES), 1)
    return (lane >= half * HEAD_DIM) & (lane < half * HEAD_DIM + HEAD_DIM)


def _lane_tile(a, width):
    return jnp.concatenate([a] * (width // LANES), axis=1)


def _softplus_neg_abs(z):
    return jnp.log(1.0 + jnp.exp(-jnp.abs(z)))


def _sb_fwd(qkv, n_pair, shards):
    S = qkv.shape[0]
    B = SB_BLOCK
    nq = S // B
    us = _tri(B, lambda r, c: r > c)
    n_sh = len(shards)
    ex = _Exchange(n_sh, scatter=False)

    def body(*refs):
        q_ref, k_ref, v_ref, us_ref = refs[:4]
        sh_in = refs[4:4 + n_sh]
        o_ref, st_ref, js_ref = refs[4 + n_sh:7 + n_sh]
        sh_out = refs[7 + n_sh:7 + 2 * n_sh]
        acc_ref, r_ref = refs[7 + 2 * n_sh:9 + 2 * n_sh]
        sems = refs[9 + 2 * n_sh:]
        p, i = pl.program_id(0), pl.program_id(1)

        @pl.when((p == 0) & (i == 0))
        def _():
            ex.start(sh_in, sh_out, sems)

        hms = [_half_mask(h) for h in range(2)]
        qv = q_ref[...]
        qss = [jnp.where(hm, qv, jnp.zeros_like(qv)) * SCALE for hm in hms]
        row = lax.broadcasted_iota(I32, (B, B), 0)
        col = lax.broadcasted_iota(I32, (B, B), 1)
        tri = col < row
        acc_ref[...] = jnp.zeros_like(acc_ref)
        r_ref[...] = jnp.zeros_like(r_ref)

        def block(j, diag):
            off = pl.multiple_of(j * B, B)
            kj = k_ref[pl.ds(off, B), :]
            vj = v_ref[pl.ds(off, B), :]
            zs = [_dot_nt(qss[h], kj) for h in range(2)]
            sps = [_softplus_neg_abs(z) for z in zs]
            bs = [jnp.minimum(-z, 0.0) - sp for z, sp in zip(zs, sps)]
            if diag:
                bs = [jnp.where(tri, b, 0.0) for b in bs]
            lexcs = [_dot_acc(b, us_ref[...]) for b in bs]
            ws = [jnp.exp(jnp.minimum(z, 0.0) - sp + (_lane_tile(r_ref[h], B) + lexc))
                  for h, (z, sp, lexc) in enumerate(zip(zs, sps, lexcs))]
            if diag:
                ws = [jnp.where(tri, w, 0.0) for w in ws]
            for h in range(2):
                acc_ref[h] += _dot(ws[h].astype(BF16), vj)
                r_ref[h] += jnp.broadcast_to(lexcs[h][:, 0:1] + bs[h][:, 0:1], (B, LANES))

        def live():
            return (jnp.max(r_ref[...]) > -EXP_ZERO).astype(I32)

        def first_two():
            has_prev = (i > 0).astype(F32)
            off_d = pl.multiple_of(i * B, B)
            off_o = pl.multiple_of(jnp.maximum(i - 1, 0) * B, B)
            k_d, v_d = k_ref[pl.ds(off_d, B), :], v_ref[pl.ds(off_d, B), :]
            k_o, v_o = k_ref[pl.ds(off_o, B), :], v_ref[pl.ds(off_o, B), :]
            hs = range(2)
            z_d = [_dot_nt(qss[h], k_d) for h in hs]
            z_o = [_dot_nt(qss[h], k_o) for h in hs]
            sp_d = [_softplus_neg_abs(z) for z in z_d]
            sp_o = [_softplus_neg_abs(z) for z in z_o]
            b_d = [jnp.where(tri, jnp.minimum(-z, 0.0) - sp, 0.0) for z, sp in zip(z_d, sp_d)]
            b_o = [(jnp.minimum(-z, 0.0) - sp) * has_prev for z, sp in zip(z_o, sp_o)]
            l_d = [_dot_acc(b, us_ref[...]) for b in b_d]
            l_o = [_dot_acc(b, us_ref[...]) for b in b_o]
            r_d = [jnp.broadcast_to(l[:, 0:1] + b[:, 0:1], (B, LANES)) for l, b in zip(l_d, b_d)]
            w_d = [jnp.where(tri, jnp.exp(jnp.minimum(z, 0.0) - sp + l), 0.0) for z, sp, l in zip(z_d, sp_d, l_d)]
            w_o = [jnp.exp(jnp.minimum(z, 0.0) - sp + (_lane_tile(r, B) + l)) * has_prev
                   for z, sp, l, r in zip(z_o, sp_o, l_o, r_d)]
            for h in hs:
                acc_ref[h] = _dot(w_d[h].astype(BF16), v_d) + _dot(w_o[h].astype(BF16), v_o)
                r_ref[h] = r_d[h] + jnp.broadcast_to(l_o[h][:, 0:1] + b_o[h][:, 0:1], (B, LANES))

        first_two()

        def step(carry):
            j, _ = carry
            block(j, False)
            return j - 1, live()

        j_end, _ = lax.while_loop(lambda c: (c[0] >= 0) & (c[1] > 0), step, (i - 2, live()))
        js = jnp.maximum(j_end + 1, 0)
        js_ref[2 * p, i] = js
        js_ref[2 * p + 1, i] = js
        o_ref[...] = jnp.where(hms[0], acc_ref[0], acc_ref[1])
        st_ref[...] = r_ref[...]

        @pl.when((p == n_pair - 1) & (i == nq - 1))
        def _():
            ex.finish(sh_in, sh_out, sems)

    any_spec = pl.BlockSpec(memory_space=pl.ANY)
    return pl.pallas_call(
        body, name="sb_attn_fwd", grid=(n_pair, nq),
        in_specs=[pl.BlockSpec((B, LANES), lambda p, i: (i, p)),
                  pl.BlockSpec((S, LANES), lambda p, i: (0, n_pair + p)),
                  pl.BlockSpec((S, LANES), lambda p, i: (0, 2 * n_pair + p)),
                  _full((B, B))] + [any_spec] * n_sh,
        out_specs=[pl.BlockSpec((B, LANES), lambda p, i: (i, p)),
                   pl.BlockSpec((2, B, LANES), lambda p, i: (p, i, 0)),
                   pl.BlockSpec(memory_space=pltpu.SMEM)] + [any_spec] * n_sh,
        out_shape=[jax.ShapeDtypeStruct((S, n_pair * LANES), F32),
                   jax.ShapeDtypeStruct((2 * n_pair, S, LANES), F32),
                   jax.ShapeDtypeStruct((2 * n_pair, nq), I32)] + ex.out_shapes(shards),
        scratch_shapes=[pltpu.VMEM((2, B, LANES), F32), pltpu.VMEM((2, B, LANES), F32)] + ex.sem_shapes(),
        compiler_params=_cparams(("arbitrary", "arbitrary")),
    )(qkv, qkv, qkv, us, *shards)


def _sb_bwd(qkv, do, st, js, n_pair):
    S = qkv.shape[0]
    B = SB_BLOCK
    nq = S // B
    us = _tri(B, lambda r, c: r > c)
    ti = _tri(B, lambda r, c: r <= c)

    def body(js_ref, q_ref, k_ref, v_ref, do_ref, st_ref, us_ref, ti_ref, dq_ref, dk_ref, dv_ref,
             dqa_ref, pr_ref, er_ref):
        p, i = pl.program_id(0), pl.program_id(1)

        @pl.when(i == 0)
        def _():
            dk_ref[...] = jnp.zeros_like(dk_ref)
            dv_ref[...] = jnp.zeros_like(dv_ref)

        hms = [_half_mask(h) for h in range(2)]
        qv = q_ref[...]
        dov = do_ref[...]
        qss = [jnp.where(hm, qv, jnp.zeros_like(qv)) * SCALE for hm in hms]
        dobs = [jnp.where(hm, dov, 0.0).astype(BF16) for hm in hms]
        qs_t = [jnp.transpose(q.astype(F32)).astype(BF16) for q in qss]
        do_t = [jnp.transpose(jnp.where(hm, dov, 0.0)).astype(BF16) for hm in hms]
        row = lax.broadcasted_iota(I32, (B, B), 0)
        col = lax.broadcasted_iota(I32, (B, B), 1)
        tri = col < row
        dqa_ref[...] = jnp.zeros_like(dqa_ref)
        pr_ref[...] = jnp.zeros_like(pr_ref)
        er_ref[...] = jnp.zeros_like(er_ref)

        def block(j, diag):
            off = pl.multiple_of(j * B, B)
            kj = k_ref[pl.ds(off, B), :]
            vj = v_ref[pl.ds(off, B), :]
            hs = range(2)
            zs = [_dot_nt(qss[h], kj) for h in hs]
            dws = [_dot_nt(dobs[h], vj) for h in hs]
            sps = [_softplus_neg_abs(z) for z in zs]
            bs = [jnp.minimum(-z, 0.0) - sp for z, sp in zip(zs, sps)]
            if diag:
                bs = [jnp.where(tri, b, 0.0) for b in bs]
            lexcs = [_dot_acc(b, us_ref[...]) for b in bs]
            ws = []
            for h in hs:
                pr_new = pr_ref[h] + jnp.broadcast_to(lexcs[h][:, 0:1] + bs[h][:, 0:1], (B, LANES))
                pr_ref[h] = pr_new
                w = jnp.exp(jnp.minimum(zs[h], 0.0) - sps[h] + (_lane_tile(st_ref[h] - pr_new, B) + lexcs[h]))
                ws.append(jnp.where(tri, w, 0.0) if diag else w)
            es = [dw * w for dw, w in zip(dws, ws)]
            eincs = [_dot_acc(e, ti_ref[...]) for e in es]
            dzbs = []
            for h in hs:
                er = er_ref[h]
                big_e = _lane_tile(er, B) + (eincs[h] - es[h])
                er_ref[h] = er + jnp.broadcast_to(eincs[h][:, B - 1:B], (B, LANES))
                eb = jnp.exp(bs[h])
                dzbs.append((es[h] * eb - big_e * (1.0 - eb)).astype(BF16))
            for h in hs:
                dqa_ref[h] += _dot(dzbs[h], kj)
            dk_ref[:, pl.ds(off, B)] += _dot(qs_t[0], dzbs[0]) + _dot(qs_t[1], dzbs[1])
            dv_ref[:, pl.ds(off, B)] += _dot(do_t[0], ws[0].astype(BF16)) + _dot(do_t[1], ws[1].astype(BF16))

        def step(j, carry):
            block(j, False)
            return carry

        def last_two():
            has_prev = (i > 0).astype(F32)
            off_d = pl.multiple_of(i * B, B)
            off_o = pl.multiple_of(jnp.maximum(i - 1, 0) * B, B)
            k_d, v_d = k_ref[pl.ds(off_d, B), :], v_ref[pl.ds(off_d, B), :]
            k_o, v_o = k_ref[pl.ds(off_o, B), :], v_ref[pl.ds(off_o, B), :]
            hs = range(2)
            z_o = [_dot_nt(qss[h], k_o) for h in hs]
            z_d = [_dot_nt(qss[h], k_d) for h in hs]
            dw_o = [_dot_nt(dobs[h], v_o) for h in hs]
            dw_d = [_dot_nt(dobs[h], v_d) for h in hs]
            sp_o = [_softplus_neg_abs(z) for z in z_o]
            sp_d = [_softplus_neg_abs(z) for z in z_d]
            b_o = [(jnp.minimum(-z, 0.0) - sp) * has_prev for z, sp in zip(z_o, sp_o)]
            b_d = [jnp.where(tri, jnp.minimum(-z, 0.0) - sp, 0.0) for z, sp in zip(z_d, sp_d)]
            l_o = [_dot_acc(b, us_ref[...]) for b in b_o]
            l_d = [_dot_acc(b, us_ref[...]) for b in b_d]
            w_o, w_d = [], []
            for h in hs:
                pr1 = pr_ref[h] + jnp.broadcast_to(l_o[h][:, 0:1] + b_o[h][:, 0:1], (B, LANES))
                pr2 = pr1 + jnp.broadcast_to(l_d[h][:, 0:1] + b_d[h][:, 0:1], (B, LANES))
                st = st_ref[h]
                w_o.append(jnp.exp(jnp.minimum(z_o[h], 0.0) - sp_o[h] + (_lane_tile(st - pr1, B) + l_o[h])) * has_prev)
                w_d.append(jnp.where(tri, jnp.exp(jnp.minimum(z_d[h], 0.0) - sp_d[h]
                                                  + (_lane_tile(st - pr2, B) + l_d[h])), 0.0))
            e_o = [dw * w for dw, w in zip(dw_o, w_o)]
            e_d = [dw * w for dw, w in zip(dw_d, w_d)]
            ei_o = [_dot_acc(e, ti_ref[...]) for e in e_o]
            ei_d = [_dot_acc(e, ti_ref[...]) for e in e_d]
            dz_o, dz_d = [], []
            for h in hs:
                er = er_ref[h]
                big_o = _lane_tile(er, B) + (ei_o[h] - e_o[h])
                er1 = er + jnp.broadcast_to(ei_o[h][:, B - 1:B], (B, LANES))
                big_d = _lane_tile(er1, B) + (ei_d[h] - e_d[h])
                eb_o, eb_d = jnp.exp(b_o[h]), jnp.exp(b_d[h])
                dz_o.append((e_o[h] * eb_o - big_o * (1.0 - eb_o)).astype(BF16))
                dz_d.append((e_d[h] * eb_d - big_d * (1.0 - eb_d)).astype(BF16))
            for h in hs:
                dqa_ref[h] += _dot(dz_o[h], k_o) + _dot(dz_d[h], k_d)
            dk_ref[:, pl.ds(off_o, B)] += _dot(qs_t[0], dz_o[0]) + _dot(qs_t[1], dz_o[1])
            dv_ref[:, pl.ds(off_o, B)] += _dot(do_t[0], w_o[0].astype(BF16)) + _dot(do_t[1], w_o[1].astype(BF16))
            dk_ref[:, pl.ds(off_d, B)] += _dot(qs_t[0], dz_d[0]) + _dot(qs_t[1], dz_d[1])
            dv_ref[:, pl.ds(off_d, B)] += _dot(do_t[0], w_d[0].astype(BF16)) + _dot(do_t[1], w_d[1].astype(BF16))

        lax.fori_loop(js_ref[2 * p, i], i - 1, step, 0)
        last_two()
        dq_ref[...] = jnp.where(hms[0], dqa_ref[0], dqa_ref[1]) * SCALE

    W = n_pair * LANES
    return pl.pallas_call(
        body, name="sb_attn_bwd",
        grid_spec=pltpu.PrefetchScalarGridSpec(
            num_scalar_prefetch=1, grid=(n_pair, nq),
            in_specs=[pl.BlockSpec((B, LANES), lambda p, i, js: (i, p)),
                      pl.BlockSpec((S, LANES), lambda p, i, js: (0, n_pair + p)),
                      pl.BlockSpec((S, LANES), lambda p, i, js: (0, 2 * n_pair + p)),
                      pl.BlockSpec((B, LANES), lambda p, i, js: (i, p)),
                      pl.BlockSpec((2, B, LANES), lambda p, i, js: (p, i, 0)),
                      pl.BlockSpec((B, B), lambda p, i, js: (0, 0)),
                      pl.BlockSpec((B, B), lambda p, i, js: (0, 0))],
            out_specs=[pl.BlockSpec((B, LANES), lambda p, i, js: (i, p)),
                       pl.BlockSpec((LANES, S), lambda p, i, js: (p, 0)),
                       pl.BlockSpec((LANES, S), lambda p, i, js: (p, 0))],
            scratch_shapes=[pltpu.VMEM((2, B, LANES), F32), pltpu.VMEM((2, B, LANES), F32),
                            pltpu.VMEM((2, B, LANES), F32)]),
        out_shape=[jax.ShapeDtypeStruct((S, W), F32)] + [jax.ShapeDtypeStruct((W, S), F32)] * 2,
        compiler_params=_cparams(("arbitrary", "arbitrary")),
    )(js, qkv, qkv, qkv, do, st, us, ti)


def _head_column(blk, head):
    lane = lax.broadcasted_iota(I32, (1, LANES), 1)
    return jnp.sum(jnp.where(lane == head, blk, 0.0), axis=1, keepdims=True)


def _fox_fwd(qkv, c, c_rows, kmax, n_pair):
    S = qkv.shape[0]
    BQ, BK = FOX_BQ, FOX_BK
    R = BQ // BK
    nq = S // BQ

    def body(q_ref, k_ref, v_ref, c_ref, cr_ref, km_ref, o_ref, st_ref, js_ref, acc_ref, m_ref, cb_ref, qkb_ref):
        p, i, half = pl.program_id(0), pl.program_id(1), pl.program_id(2)
        hm = _half_mask(half)
        qv = q_ref[...]
        qs = jnp.where(hm, qv, jnp.zeros_like(qv)) * SCALE
        ccol = _head_column(c_ref[...], 2 * p + half)
        cb_ref[...] = jnp.broadcast_to(ccol, (BQ, BK))
        qf = qs.astype(F32)
        qkb_ref[...] = jnp.broadcast_to(
            jnp.sqrt(jnp.sum(qf * qf, axis=1, keepdims=True)) * NORM_SLACK
            * _head_column(km_ref[...], 2 * p + half) + ccol, (BQ, LANES))
        row = lax.broadcasted_iota(I32, (BQ, BK), 0)
        col = lax.broadcasted_iota(I32, (BQ, BK), 1)
        acc_ref[...] = jnp.zeros_like(acc_ref)
        m_ref[...] = jnp.full_like(m_ref, NEG_BIG)

        def blocks(j_top, diag):
            ss, v1s, keeps = [], [], []
            for d in range(R):
                j = j_top - d
                off = pl.multiple_of(j * BK, BK)
                vj = v_ref[pl.ds(off, BK), :]
                v1s.append(jnp.where(hm, vj, jnp.ones_like(vj)))
                s = _dot_nt(qs, k_ref[pl.ds(off, BK), :]) + (cb_ref[...] - cr_ref[0, pl.ds(j, 1), :])
                if diag:
                    keeps.append(col + (R - 1 - d) * BK <= row)
                    s = jnp.where(keeps[-1], s, NEG_BIG)
                ss.append(s)
            m_old = m_ref[...]
            s_max = jnp.max(functools.reduce(jnp.maximum, ss), axis=1, keepdims=True)
            m_new = jnp.maximum(m_old, jnp.broadcast_to(s_max, (BQ, LANES)))
            m_wide = _lane_tile(m_new, BK)
            pvs = [jnp.exp(s - m_wide) for s in ss]
            if diag:
                pvs = [jnp.where(keep, pv, 0.0) for keep, pv in zip(keeps, pvs)]
            new = _dot(pvs[0].astype(BF16), v1s[0])
            for pv, v1 in zip(pvs[1:], v1s[1:]):
                new = new + _dot(pv.astype(BF16), v1)
            acc_ref[...] = jnp.exp(m_old - m_new) * acc_ref[...] + new
            m_ref[...] = m_new

        def live(j):
            c_end = cr_ref[0, pl.ds(jnp.maximum(j, 0), 1), :][:, BK - 1:BK]
            return (jnp.max(qkb_ref[...] - c_end - m_ref[...]) > -EXP_ZERO).astype(I32)

        blocks(R * i + R - 1, True)

        def step(carry):
            j, _ = carry
            go_on = live(j - R)
            blocks(j, False)
            return j - R, go_on

        j_end, _ = lax.while_loop(lambda cr: (cr[0] >= 0) & (cr[1] > 0), step, (R * i - 1, live(R * i - 1)))
        js_ref[2 * p + half, i] = j_end + 1
        acc = acc_ref[...]
        denom = jnp.where(hm, pltpu.roll(acc, HEAD_DIM, 1), acc)
        res = jnp.where(hm, acc / denom, 0.0)

        @pl.when(half == 0)
        def _():
            o_ref[...] = res

        @pl.when(half == 1)
        def _():
            o_ref[...] += res

        st_ref[0] = m_ref[...] + jnp.log(denom)

    return pl.pallas_call(
        body, name="fox_attn_fwd", grid=(n_pair, nq, 2),
        in_specs=[pl.BlockSpec((BQ, LANES), lambda p, i, h: (i, 3 * n_pair + p)),
                  pl.BlockSpec((S, LANES), lambda p, i, h: (0, 4 * n_pair + p)),
                  pl.BlockSpec((S, LANES), lambda p, i, h: (0, 5 * n_pair + p)),
                  pl.BlockSpec((BQ, LANES), lambda p, i, h: (i, 0)),
                  pl.BlockSpec((1, S // BK, BK), lambda p, i, h: (2 * p + h, 0, 0)),
                  _full((1, LANES))],
        out_specs=[pl.BlockSpec((BQ, LANES), lambda p, i, h: (i, p)),
                   pl.BlockSpec((1, BQ, LANES), lambda p, i, h: (2 * p + h, i, 0)),
                   pl.BlockSpec(memory_space=pltpu.SMEM)],
        out_shape=[jax.ShapeDtypeStruct((S, n_pair * LANES), F32),
                   jax.ShapeDtypeStruct((2 * n_pair, S, LANES), F32),
                   jax.ShapeDtypeStruct((2 * n_pair, nq), I32)],
        scratch_shapes=[pltpu.VMEM((BQ, LANES), F32), pltpu.VMEM((BQ, LANES), F32), pltpu.VMEM((BQ, BK), F32),
                        pltpu.VMEM((BQ, LANES), F32)],
        compiler_params=_cparams(("arbitrary", "arbitrary", "arbitrary")),
    )(qkv, qkv, qkv, c, c_rows, kmax)


def _fox_bwd(qkv, do, o, st, c, c_rows, js, n_pair):
    S = qkv.shape[0]
    B = FOX_BLOCK
    nq = S // B

    def body(js_ref, q_ref, k_ref, v_ref, do_ref, o_ref, st_ref, c_ref, cr_ref, dq_ref, dk_ref, dv_ref,
             dc_ref, dqa_ref, rs_ref, cb_ref, db_ref):
        p, i, half = pl.program_id(0), pl.program_id(1), pl.program_id(2)

        @pl.when((i == 0) & (half == 0))
        def _():
            dk_ref[...] = jnp.zeros_like(dk_ref)
            dv_ref[...] = jnp.zeros_like(dv_ref)
            dc_ref[...] = jnp.zeros_like(dc_ref)

        hm = _half_mask(half)
        qv = q_ref[...]
        qs = jnp.where(hm, qv, jnp.zeros_like(qv)) * SCALE
        dov = jnp.where(hm, do_ref[...], 0.0)
        dob = dov.astype(BF16)
        qs_t = jnp.transpose(qs.astype(F32)).astype(BF16)
        do_t = jnp.transpose(dov).astype(BF16)
        cb_ref[...] = jnp.broadcast_to(_head_column(c_ref[...], 2 * p + half), (B, LANES)) - st_ref[0]
        db_ref[...] = jnp.broadcast_to(jnp.sum(dov * o_ref[...], axis=1, keepdims=True), (B, LANES))
        row = lax.broadcasted_iota(I32, (B, B), 0)
        col = lax.broadcasted_iota(I32, (B, B), 1)
        dqa_ref[...] = jnp.zeros_like(dqa_ref)
        rs_ref[...] = jnp.zeros_like(rs_ref)

        def block(j, diag):
            off = pl.multiple_of(j * B, B)
            kj = k_ref[pl.ds(off, B), :]
            vj = v_ref[pl.ds(off, B), :]
            pv = jnp.exp(_dot_nt(qs, kj) + (_lane_tile(cb_ref[...], B) - cr_ref[0, pl.ds(j, 1), :]))
            if diag:
                pv = jnp.where(col <= row, pv, 0.0)
            ds = pv * (_dot_nt(dob, vj) - _lane_tile(db_ref[...], B))
            dsb = ds.astype(BF16)
            dqa_ref[...] += _dot(dsb, kj)
            dk_ref[:, pl.ds(off, B)] += _dot(qs_t, dsb)
            dv_ref[:, pl.ds(off, B)] += _dot(do_t, pv.astype(BF16))
            dc_ref[0, half, pl.ds(j, 1), :] -= jnp.sum(ds, axis=0, keepdims=True)
            rs_ref[...] += jnp.sum(ds, axis=1, keepdims=True)

        def step(j, carry):
            block(j, False)
            return carry

        lax.fori_loop(js_ref[2 * p + half, i], i, step, 0)
        block(i, True)
        res = jnp.where(hm, dqa_ref[...] * SCALE, 0.0)

        @pl.when(half == 0)
        def _():
            dq_ref[...] = res

        @pl.when(half == 1)
        def _():
            dq_ref[...] += res

        dc_ref[0, half, pl.ds(i, 1), :] += jnp.transpose(jnp.broadcast_to(rs_ref[...], (B, LANES)))[0:1, :]

    W = n_pair * LANES
    return pl.pallas_call(
        body, name="fox_attn_bwd",
        grid_spec=pltpu.PrefetchScalarGridSpec(
            num_scalar_prefetch=1, grid=(n_pair, nq, 2),
            in_specs=[pl.BlockSpec((B, LANES), lambda p, i, h, js: (i, 3 * n_pair + p)),
                      pl.BlockSpec((S, LANES), lambda p, i, h, js: (0, 4 * n_pair + p)),
                      pl.BlockSpec((S, LANES), lambda p, i, h, js: (0, 5 * n_pair + p)),
                      pl.BlockSpec((B, LANES), lambda p, i, h, js: (i, p)),
                      pl.BlockSpec((B, LANES), lambda p, i, h, js: (i, p)),
                      pl.BlockSpec((1, B, LANES), lambda p, i, h, js: (2 * p + h, i, 0)),
                      pl.BlockSpec((B, LANES), lambda p, i, h, js: (i, 0)),
                      pl.BlockSpec((1, nq, B), lambda p, i, h, js: (2 * p + h, 0, 0))],
            out_specs=[pl.BlockSpec((B, LANES), lambda p, i, h, js: (i, p)),
                       pl.BlockSpec((LANES, S), lambda p, i, h, js: (p, 0)),
                       pl.BlockSpec((LANES, S), lambda p, i, h, js: (p, 0)),
                       pl.BlockSpec((1, 2, nq, B), lambda p, i, h, js: (p, 0, 0, 0))],
            scratch_shapes=[pltpu.VMEM((B, LANES), F32), pltpu.VMEM((B, 1), F32), pltpu.VMEM((B, LANES), F32),
                            pltpu.VMEM((B, LANES), F32)]),
        out_shape=[jax.ShapeDtypeStruct((S, W), F32)] + [jax.ShapeDtypeStruct((W, S), F32)] * 2
        + [jax.ShapeDtypeStruct((n_pair, 2, nq, B), F32)],
        compiler_params=_cparams(("arbitrary", "arbitrary", "arbitrary")),
    )(js, qkv, qkv, qkv, do, o, st, c, c_rows)


def _adam(w, g, m, v):
    m = ADAM_B1 * m + (1.0 - ADAM_B1) * g
    v = ADAM_B2 * v + (1.0 - ADAM_B2) * (g * g)
    m_hat = m / (1.0 - ADAM_B1 ** ADAM_STEP)
    v_hat = v / (1.0 - ADAM_B2 ** ADAM_STEP)
    delta = -ADAM_LR * (m_hat / (jnp.sqrt(v_hat) + ADAM_EPS) + ADAM_WD * w)
    return delta, m, v


def _reduce_adam(landing, w, m, v, name, extra=None):
    _, R, C = w.shape
    TR = next(t for t in (256, 128, R) if R % t == 0)
    more = [] if extra is None else [extra]

    def body(*refs):
        l_ref, w_ref, m_ref, v_ref = refs[:4]
        g_ref, d_ref, nm_ref, nv_ref = refs[4 + len(more):]
        g = l_ref[0].astype(F32)
        for s in range(1, N_DEV):
            g = g + l_ref[s].astype(F32)
        if more:
            g = g + refs[4][...]
        d, nm, nv = _adam(w_ref[0], g, m_ref[0], v_ref[0])
        g_ref[0] = g
        d_ref[0] = d
        nm_ref[0] = nm
        nv_ref[0] = nv

    blk = pl.BlockSpec((1, TR, C), lambda i: (0, i, 0))
    return pl.pallas_call(
        body, name=name, grid=(R // TR,),
        in_specs=[pl.BlockSpec((N_DEV, TR, C), lambda i: (0, i, 0)), blk, blk, blk]
        + [pl.BlockSpec((TR, C), lambda i: (i, 0))] * len(more),
        out_specs=[blk] * 4,
        out_shape=[jax.ShapeDtypeStruct((1, R, C), F32)] * 4,
        compiler_params=_cparams(("arbitrary",)),
    )(landing, w, m, v, *more)


def _reduce_adam_small(landing, w, m, v):
    R, C = w.shape

    def body(l_ref, w_ref, m_ref, v_ref, g_ref, d_ref, nm_ref, nv_ref, loss_ref):
        g = l_ref[0]
        for s in range(1, N_DEV):
            g = g + l_ref[s]
        d, nm, nv = _adam(w_ref[...], g, m_ref[...], v_ref[...])
        g_ref[...] = g
        d_ref[...] = d
        nm_ref[...] = nm
        nv_ref[...] = nv
        loss_ref[...] = jnp.broadcast_to(0.5 * jnp.sum(g[7:8, :], axis=1, keepdims=True), (1, LANES))

    return pl.pallas_call(
        body, name="reduce_adam_small",
        out_shape=[jax.ShapeDtypeStruct((R, C), F32)] * 4 + [jax.ShapeDtypeStruct((1, LANES), F32)],
    )(landing, w, m, v)


def _pad_lanes(a, width):
    return jnp.pad(a, ((0, 0), (0, width - a.shape[1])))


def _pack_small(D, n_fox, g_cat, l1g, l1b, l2g, l2b, bf, last, tail):
    return jnp.concatenate([g_cat, l1g, l1b, l2g, l2b, _pad_lanes(bf, D), jnp.zeros((1, D), F32), last, tail],
                           axis=0)


def kernel(x, w_in, b_f, g_sb, g_fox, w_out, ln1_g, ln1_b, ln2_g, ln2_b, w_gate_up, w_down, loss_target, m_w_in, m_b_f, m_g_sb, m_g_fox, m_w_out, m_ln1_g, m_ln1_b, m_ln2_g, m_ln2_b, m_w_gate_up, m_w_down, v_w_in, v_b_f, v_g_sb, v_g_fox, v_w_out, v_ln1_g, v_ln1_b, v_ln2_g, v_ln2_b, v_w_gate_up, v_w_down):
    x2, tgt = x[0], loss_target[0]
    S, D = x2.shape
    W = D // 2
    n_pair = W // LANES
    n_fox = W // HEAD_DIM
    F = w_down.shape[1] * N_DEV

    g_in = _all_gather_two_level(w_in[0].astype(BF16), "w_in_all_gather")
    w_in_full = g_in.transpose(1, 0, 2).reshape(D, -1)
    w_qkv = w_in_full[:, :6 * W]
    w_f = _pad_lanes(w_in_full[:, 6 * W:], LANES)
    g_down = (lax.broadcasted_iota(I32, (D, LANES), 0) // HEAD_DIM
              == lax.broadcasted_iota(I32, (D, LANES), 1)).astype(BF16)
    g_up = g_down.T
    g_cat = jnp.concatenate([g_sb, g_fox], axis=1)

    qkv, u, c, ksq = _proj_fwd(x2, w_qkv, w_f, _pad_lanes(b_f, LANES), n_fox)
    c_t = c[:, :n_fox].T
    c_rows = c_t.reshape(n_fox, S // FOX_BLOCK, FOX_BLOCK)
    kmax = jnp.sqrt(ksq[0:1]) * NORM_SLACK
    o_sb, st_sb, js_sb, g_out, g_gu, g_dn = _sb_fwd(
        qkv, n_pair, [w_out[0].astype(BF16), w_gate_up[0].astype(BF16), w_down[0].astype(BF16)])
    w_o = g_out.reshape(D, D)
    w_gu = g_gu.transpose(1, 0, 2).reshape(D, 2 * F)
    w_dn = g_dn.reshape(F, D)
    o_fx, st_fx, js_fx = _fox_fwd(qkv, c, c_t.reshape(n_fox, S // FOX_BK, FOX_BK), kmax, n_pair)
    js_fx = js_fx // (FOX_BLOCK // FOX_BK)
    h1, xh1, rs1, on_b, rr = _post_attn_fwd(o_sb, o_fx, x2, g_cat, g_down, g_up, w_o, ln1_g, ln1_b)
    gu, act_b, dyp, sm2 = _mlp_fwd(h1, tgt, w_gu, w_dn, ln2_g, ln2_b)

    dgu_b, dh1 = _mlp_bwd(gu, dyp, w_gu.T, w_dn.T)
    gw_gu = _matmul_tn(h1, [dgu_b], "grad_w_gate_up", n_split=2)
    gw_dn = _matmul_tn(act_b, [dyp], "grad_w_down")
    dxa, dmix_b, do_sb, do_fx, sm1, l_gu, l_dn = _post_attn_bwd(
        dh1, xh1, rs1, ln1_g, o_sb, o_fx, rr, g_cat, g_down, g_up, w_o.T,
        [gw_gu.reshape(D, N_DEV, -1).transpose(1, 0, 2), gw_dn.reshape(N_DEV, F // N_DEV, D)])
    dq_sb, dk_sb, dv_sb = _sb_bwd(qkv, do_sb, st_sb, js_sb, n_pair)
    dq_fx, dk_fx, dv_fx, dcr = _fox_bwd(qkv, do_fx, o_fx, st_fx, c, c_rows, js_fx, n_pair)
    dc = _pad_lanes(dcr.reshape(n_fox, S).T, LANES)
    pieces = [dq_sb, dk_sb, dv_sb, dq_fx, dk_fx, dv_fx]
    piece_t = (False, True, True, False, True, True)
    gw_qkv = _matmul_tn(x2, pieces, "grad_w_qkv", pad_cols=LANES, b_t=piece_t)[:, :6 * W + n_fox]
    gw_out = _matmul_tn(on_b, [dmix_b], "grad_w_out")
    dx, gw_f_t, sm0, l_in, l_out = _proj_bwd(
        dxa, x2, pieces, piece_t, dc, u, w_qkv.T, w_f.T, n_fox,
        [gw_qkv.reshape(D, N_DEV, -1).transpose(1, 0, 2), gw_out.reshape(N_DEV, D // N_DEV, D)])

    small = _pack_small(D, n_fox, sm1[2:3], sm1[0:1], sm1[1:2], sm2[0:1], sm2[1:2], sm0[0:1, :n_fox],
                        sm2[2:3] * (1.0 / D), gw_f_t[:n_fox])
    (l_small,) = _exchange_call([jnp.broadcast_to(small[None], (N_DEV,) + small.shape)], True, "small_exchange")

    zero = jnp.zeros((1, D), F32)
    pack = lambda gc, a, b_, c_, d_, bf: _pack_small(D, n_fox, gc, a, b_, c_, d_, bf, zero,
                                                     jnp.zeros((n_fox, D), F32))
    r_small = _reduce_adam_small(
        l_small,
        pack(g_cat, ln1_g, ln1_b, ln2_g, ln2_b, b_f),
        pack(jnp.concatenate([m_g_sb, m_g_fox], axis=1), m_ln1_g, m_ln1_b, m_ln2_g, m_ln2_b, m_b_f),
        pack(jnp.concatenate([v_g_sb, v_g_fox], axis=1), v_ln1_g, v_ln1_b, v_ln2_g, v_ln2_b, v_b_f))
    loss = r_small[4][0, 0]
    cols = w_in.shape[2]
    gf_cols = jnp.pad(r_small[0][8:8 + n_fox].T, ((0, 0), (cols - n_fox, 0)))
    extra = jnp.where(_my_index() == N_DEV - 1, gf_cols, 0.0)
    r_in = _reduce_adam(l_in, w_in, m_w_in, v_w_in, "reduce_adam_w_in", extra)
    r_out = _reduce_adam(l_out, w_out, m_w_out, v_w_out, "reduce_adam_w_out")
    r_gu = _reduce_adam(l_gu, w_gate_up, m_w_gate_up, v_w_gate_up, "reduce_adam_w_gate_up")
    r_dn = _reduce_adam(l_dn, w_down, m_w_down, v_w_down, "reduce_adam_w_down")

    def unpack(kind):
        big = [r_in[kind], None, None, None, r_out[kind], None, None, None, None, r_gu[kind], r_dn[kind]]
        s = r_small[kind]
        big[1] = s[5:6, :n_fox]
        big[2] = s[0:1, :W]
        big[3] = s[0:1, W:]
        big[5], big[6], big[7], big[8] = s[1:2], s[2:3], s[3:4], s[4:5]
        return big

    return (loss, dx[None], *unpack(0), *unpack(1), *unpack(2), *unpack(3))
```

```python
import functools

import jax
import jax.numpy as jnp
from jax import lax
from jax.experimental import pallas as pl
from jax.experimental.pallas import tpu as pltpu

F32 = jnp.float32
BF16 = jnp.bfloat16
I32 = jnp.int32

N_DEV = 8
HEAD_DIM = 64
LANES = 128
SCALE = HEAD_DIM ** -0.5
ALPHA = 2.0 ** 0.25
LN_EPS = 1e-5
RMS_EPS = 1e-6
ADAM_LR, ADAM_B1, ADAM_B2, ADAM_EPS, ADAM_WD, ADAM_STEP = 0.001, 0.9, 0.999, 1e-08, 0.01, 10
NEG_BIG = -1e30
NORM_SLACK = 1.01
EXP_ZERO = 88.5
VMEM_LIMIT = 60 * 1024 * 1024
ROW_TILE = 256
SB_BLOCK = 256
FOX_BLOCK = 512
FOX_BQ, FOX_BK = 512, 256
assert FOX_BQ == FOX_BLOCK and FOX_BLOCK % FOX_BK == 0
MESH = pl.DeviceIdType.MESH


def _cparams(sem):
    return pltpu.CompilerParams(dimension_semantics=sem, vmem_limit_bytes=VMEM_LIMIT)


def _dot(a, b):
    return jnp.dot(a, b, preferred_element_type=F32)


def _dot_nt(a, b):
    return lax.dot_general(a, b, (((1,), (1,)), ((), ())), preferred_element_type=F32)


def _dot_tn(a, b):
    return lax.dot_general(a, b, (((0,), (0,)), ((), ())), preferred_element_type=F32)


def _split2(a):
    hi = a.astype(BF16)
    lo = (a - hi.astype(F32)).astype(BF16)
    return hi, lo


def _split3(a):
    hi = a.astype(BF16)
    r1 = a - hi.astype(F32)
    mid = r1.astype(BF16)
    lo = (r1 - mid.astype(F32)).astype(BF16)
    return hi, mid, lo


def _dot_acc(a, m):
    hi, lo = _split2(a)
    return _dot(jnp.concatenate([hi, lo], axis=1), jnp.concatenate([m, m], axis=0))


def _tri(n, fn):
    r = lax.broadcasted_iota(I32, (n, n), 0)
    c = lax.broadcasted_iota(I32, (n, n), 1)
    return fn(r, c).astype(BF16)


def _full(shape):
    nd = len(shape)
    return pl.BlockSpec(shape, lambda *_: (0,) * nd)


def _peer(k):
    x, y, c = lax.axis_index("x"), lax.axis_index("y"), lax.axis_index("c")
    return (1 - x if k & 4 else x, 1 - y if k & 2 else y, 1 - c if k & 1 else c)


def _my_index():
    return 4 * lax.axis_index("x") + 2 * lax.axis_index("y") + lax.axis_index("c")


class _Exchange:
    def __init__(self, n, scatter):
        self.n, self.scatter = n, scatter

    def sem_shapes(self):
        return [pltpu.SemaphoreType.DMA(((N_DEV - 1) * self.n,)), pltpu.SemaphoreType.DMA(((N_DEV - 1) * self.n,)),
                pltpu.SemaphoreType.DMA((self.n,))]

    def out_shapes(self, arrays):
        if self.scatter:
            return [jax.ShapeDtypeStruct(s.shape, s.dtype) for s in arrays]
        return [jax.ShapeDtypeStruct((N_DEV,) + s.shape, s.dtype) for s in arrays]

    def _copies(self, ins, outs, sems, landing):
        send_sems, recv_sems, local_sems = sems
        me = _my_index()
        src = lambda a, d: ins[a].at[d] if self.scatter else ins[a]
        local = [pltpu.make_async_copy(src(a, me), outs[a].at[me], local_sems.at[a]) for a in range(self.n)]
        remote = [pltpu.make_async_remote_copy(
            src_ref=src(a, me ^ k), dst_ref=outs[a].at[me ^ k if landing else me],
            send_sem=send_sems.at[(k - 1) * self.n + a], recv_sem=recv_sems.at[(k - 1) * self.n + a],
            device_id=_peer(k), device_id_type=MESH) for k in range(1, N_DEV) for a in range(self.n)]
        return local, remote

    def start(self, ins, outs, sems):
        local, sent = self._copies(ins, outs, sems, landing=False)
        for cp in local + sent:
            cp.start()

    def finish(self, ins, outs, sems):
        local, landed = self._copies(ins, outs, sems, landing=True)
        for cp in landed:
            cp.wait_recv()
        for cp in landed:
            cp.wait_send()
        for cp in local:
            cp.wait()


def _all_gather_two_level(shard, name):
    def body(x_ref, out_ref, send_sems, recv_sems, local_sem):
        x, y, c = lax.axis_index("x"), lax.axis_index("y"), lax.axis_index("c")
        me, sibling = (x, y, c), (x, y, 1 - c)
        chips = [(1 - x, y), (x, 1 - y), (1 - x, 1 - y)]

        def slot(px, py, pc):
            return out_ref.at[4 * px + 2 * py + pc]

        def copy(k, block, to, src=None):
            return pltpu.make_async_remote_copy(
                src_ref=slot(*block) if src is None else src, dst_ref=slot(*block),
                send_sem=send_sems.at[k], recv_sem=recv_sems.at[k], device_id=to, device_id_type=MESH)

        mine = pltpu.make_async_copy(x_ref, slot(*me), local_sem)
        mine.start()
        first = [copy(0, me, sibling, src=x_ref)]
        first += [copy(1 + j, me, (*chip, c), src=x_ref) for j, chip in enumerate(chips)]
        for cp in first:
            cp.start()
        passed = [copy(4 + j, (*chip, c), sibling) for j, chip in enumerate(chips)]
        for j, chip in enumerate(chips):
            copy(1 + j, (*chip, c), me).wait_recv()
            passed[j].start()
        copy(0, sibling, me).wait_recv()
        for j, chip in enumerate(chips):
            copy(4 + j, (*chip, 1 - c), me).wait_recv()
        for cp in first + passed:
            cp.wait_send()
        mine.wait()

    any_spec = pl.BlockSpec(memory_space=pl.ANY)
    return pl.pallas_call(
        body, name=name, out_shape=jax.ShapeDtypeStruct((N_DEV,) + shard.shape, shard.dtype),
        in_specs=[any_spec], out_specs=any_spec,
        scratch_shapes=[pltpu.SemaphoreType.DMA((N_DEV - 1,)), pltpu.SemaphoreType.DMA((N_DEV - 1,)),
                        pltpu.SemaphoreType.DMA],
    )(shard)


def _exchange_call(arrays, scatter, name):
    n = len(arrays)
    ex = _Exchange(n, scatter)

    def body(*refs):
        ins, outs, sems = refs[:n], refs[n:2 * n], refs[2 * n:]
        ex.start(ins, outs, sems)
        ex.finish(ins, outs, sems)

    any_spec = pl.BlockSpec(memory_space=pl.ANY)
    return pl.pallas_call(
        body, name=name, out_shape=ex.out_shapes(arrays),
        in_specs=[any_spec] * n, out_specs=[any_spec] * n, scratch_shapes=ex.sem_shapes(),
    )(*arrays)


def _log_sigmoid(u):
    return jnp.minimum(u, 0.0) - jnp.log1p(jnp.exp(-jnp.abs(u)))


def _proj_fwd(x, w_qkv, w_f, bf_pad, n_fox):
    S, D = x.shape
    N = w_qkv.shape[1]
    W = D // 2
    TM = ROW_TILE
    tri = _tri(TM, lambda r, c: c <= r)
    r_ = lax.broadcasted_iota(I32, (W, LANES), 0)
    c_ = lax.broadcasted_iota(I32, (W, LANES), 1)
    head_of = (r_ // HEAD_DIM == c_).astype(BF16)

    def body(x_ref, wq_ref, wf_ref, bf_ref, tri_ref, ho_ref, qkv_ref, u_ref, c_ref, ksq_ref, run_ref):
        @pl.when(pl.program_id(0) == 0)
        def _():
            run_ref[...] = jnp.zeros_like(run_ref)
            ksq_ref[...] = jnp.zeros_like(ksq_ref)

        xb = x_ref[...].astype(BF16)
        for n0 in range(0, N, D):
            chunk = _dot(xb, wq_ref[:, n0:n0 + D]).astype(BF16)
            qkv_ref[:, n0:n0 + D] = chunk
            if n0 == 4 * W:
                kf = chunk[:, :W].astype(F32)
                ksq = jnp.max(_dot_acc(kf * kf, ho_ref[...]), axis=0, keepdims=True)
                ksq_ref[...] = jnp.maximum(ksq_ref[...], ksq)
        u = _dot(xb, wf_ref[...]) + bf_ref[...]
        lane = lax.broadcasted_iota(I32, u.shape, 1)
        logf = jnp.where(lane < n_fox, _log_sigmoid(u), 0.0)
        u_ref[...] = u
        hi, mid, lo = _split3(logf)
        t = tri_ref[...]
        cs = _dot(t, hi) + _dot(t, mid) + _dot(t, lo) + run_ref[...]
        c_ref[...] = cs
        run_ref[...] = cs[TM - 1:TM, :]

    return pl.pallas_call(
        body, name="proj_fwd", grid=(S // TM,),
        in_specs=[pl.BlockSpec((TM, D), lambda i: (i, 0)), _full(w_qkv.shape), _full(w_f.shape),
                  _full(bf_pad.shape), _full(tri.shape), _full(head_of.shape)],
        out_specs=[pl.BlockSpec((TM, N), lambda i: (i, 0)), pl.BlockSpec((TM, LANES), lambda i: (i, 0)),
                   pl.BlockSpec((TM, LANES), lambda i: (i, 0)), _full((8, LANES))],
        out_shape=[jax.ShapeDtypeStruct((S, N), BF16), jax.ShapeDtypeStruct((S, LANES), F32),
                   jax.ShapeDtypeStruct((S, LANES), F32), jax.ShapeDtypeStruct((8, LANES), F32)],
        scratch_shapes=[pltpu.VMEM((1, LANES), F32)],
        compiler_params=_cparams(("arbitrary",)),
    )(x, w_qkv, w_f, bf_pad, tri, head_of)


def _post_attn_fwd(o_sb, o_fx, x, g_cat, g_down, g_up, w_out, ln_g, ln_b):
    S, D = x.shape
    H = D // 2
    TM = ROW_TILE

    def body(osb_ref, ofx_ref, x_ref, g_ref, gd_ref, gu_ref, wo_ref, lg_ref, lb_ref,
             h1_ref, xh_ref, rs_ref, on_ref, rr_ref):
        o = jnp.concatenate([osb_ref[...], ofx_ref[...]], axis=1)
        ms = _dot_acc(o * o, gd_ref[...]) * (1.0 / HEAD_DIM)
        r = _dot_acc(lax.rsqrt(ms + RMS_EPS), gu_ref[...])
        onb = (o * r * g_ref[...]).astype(BF16)
        hp = ALPHA * x_ref[...] + _dot(onb, wo_ref[...])
        mu = jnp.mean(hp, axis=-1, keepdims=True)
        d = hp - mu
        rstd = lax.rsqrt(jnp.mean(d * d, axis=-1, keepdims=True) + LN_EPS)
        xh = d * rstd
        h1_ref[...] = xh * lg_ref[...] + lb_ref[...]
        xh_ref[...] = xh
        rs_ref[...] = jnp.broadcast_to(rstd, (TM, LANES))
        on_ref[...] = onb
        rr_ref[...] = r

    row = lambda w: pl.BlockSpec((TM, w), lambda i: (i, 0))
    return pl.pallas_call(
        body, name="post_attn_fwd", grid=(S // TM,),
        in_specs=[row(H), row(H), row(D), _full((1, D)), _full((D, LANES)), _full((LANES, D)), _full((D, D)),
                  _full((1, D)), _full((1, D))],
        out_specs=[row(D), row(D), row(LANES), row(D), row(D)],
        out_shape=[jax.ShapeDtypeStruct((S, D), F32), jax.ShapeDtypeStruct((S, D), F32),
                   jax.ShapeDtypeStruct((S, LANES), F32), jax.ShapeDtypeStruct((S, D), BF16),
                   jax.ShapeDtypeStruct((S, D), F32)],
        compiler_params=_cparams(("arbitrary",)),
    )(o_sb, o_fx, x, g_cat, g_down, g_up, w_out, ln_g, ln_b)


def _ln_bwd(dxh, xh, rstd):
    m1 = jnp.mean(dxh, axis=-1, keepdims=True)
    m2 = jnp.mean(dxh * xh, axis=-1, keepdims=True)
    return rstd * (dxh - m1 - xh * m2)


def _mlp_fwd(h1, target, w_gu, w_dn, ln_g, ln_b):
    S, D = h1.shape
    F = w_dn.shape[0]
    TM = ROW_TILE
    FC = F // 2

    def body(h1_ref, tg_ref, wgu_hbm, wdn_hbm, lg_ref, lb_ref, gu_ref, act_ref, dyp_ref, sm_ref, wgu, wdn):
        @pl.when(pl.program_id(0) == 0)
        def _():
            pltpu.sync_copy(wgu_hbm, wgu)
            pltpu.sync_copy(wdn_hbm, wdn)
            sm_ref[...] = jnp.zeros_like(sm_ref)

        h1v = h1_ref[...]
        hb = h1v.astype(BF16)
        ff = jnp.zeros((TM, D), F32)
        for c0 in range(0, F, FC):
            g = _dot(hb, wgu[:, c0:c0 + FC])
            u = _dot(hb, wgu[:, F + c0:F + c0 + FC])
            gu_ref[:, c0:c0 + FC] = g
            gu_ref[:, F + c0:F + c0 + FC] = u
            ab = ((g * jax.nn.sigmoid(g)) * u).astype(BF16)
            act_ref[:, c0:c0 + FC] = ab
            ff = ff + _dot(ab, wdn[c0:c0 + FC, :])
        yp = ALPHA * h1v + ff
        mu = jnp.mean(yp, axis=-1, keepdims=True)
        d = yp - mu
        rstd = lax.rsqrt(jnp.mean(d * d, axis=-1, keepdims=True) + LN_EPS)
        xh = d * rstd
        err = (xh * lg_ref[...] + lb_ref[...]) - tg_ref[...]
        dy = err * (1.0 / D)
        sm_ref[0:1, :] += jnp.sum(dy * xh, axis=0, keepdims=True)
        sm_ref[1:2, :] += jnp.sum(dy, axis=0, keepdims=True)
        sm_ref[2:3, :] += jnp.sum(err * err, axis=0, keepdims=True)
        dyp_ref[...] = _ln_bwd(dy * lg_ref[...], xh, rstd)

    row = lambda w: pl.BlockSpec((TM, w), lambda i: (i, 0))
    hbm = pl.BlockSpec(memory_space=pl.ANY)
    return pl.pallas_call(
        body, name="mlp_fwd", grid=(S // TM,),
        in_specs=[row(D), row(D), hbm, hbm, _full((1, D)), _full((1, D))],
        out_specs=[row(2 * F), row(F), row(D), _full((8, D))],
        out_shape=[jax.ShapeDtypeStruct((S, 2 * F), F32), jax.ShapeDtypeStruct((S, F), BF16),
                   jax.ShapeDtypeStruct((S, D), F32), jax.ShapeDtypeStruct((8, D), F32)],
        scratch_shapes=[pltpu.VMEM(w_gu.shape, BF16), pltpu.VMEM(w_dn.shape, BF16)],
        compiler_params=_cparams(("arbitrary",)),
    )(h1, target, w_gu, w_dn, ln_g, ln_b)


def _mlp_bwd(gu, dyp, w_guT, w_dnT):
    S, D = dyp.shape
    F = w_dnT.shape[1]
    TM = ROW_TILE
    FC = F // 2

    def body(gu_ref, dyp_ref, wguT_hbm, wdnT_hbm, dgu_ref, dh1_ref, wguT, wdnT):
        @pl.when(pl.program_id(0) == 0)
        def _():
            pltpu.sync_copy(wguT_hbm, wguT)
            pltpu.sync_copy(wdnT_hbm, wdnT)

        dypv = dyp_ref[...]
        db = dypv.astype(BF16)
        dh1 = ALPHA * dypv
        for c0 in range(0, F, FC):
            dact = _dot(db, wdnT[:, c0:c0 + FC])
            g = gu_ref[:, c0:c0 + FC]
            u = gu_ref[:, F + c0:F + c0 + FC]
            sg = jax.nn.sigmoid(g)
            dgb = (dact * u * (sg * (1.0 + g * (1.0 - sg)))).astype(BF16)
            dub = (dact * (g * sg)).astype(BF16)
            dgu_ref[:, c0:c0 + FC] = dgb
            dgu_ref[:, F + c0:F + c0 + FC] = dub
            dh1 = dh1 + _dot(dgb, wguT[c0:c0 + FC, :]) + _dot(dub, wguT[F + c0:F + c0 + FC, :])
        dh1_ref[...] = dh1

    row = lambda w: pl.BlockSpec((TM, w), lambda i: (i, 0))
    hbm = pl.BlockSpec(memory_space=pl.ANY)
    return pl.pallas_call(
        body, name="mlp_bwd", grid=(S // TM,),
        in_specs=[row(2 * F), row(D), hbm, hbm],
        out_specs=[row(2 * F), row(D)],
        out_shape=[jax.ShapeDtypeStruct((S, 2 * F), BF16), jax.ShapeDtypeStruct((S, D), F32)],
        scratch_shapes=[pltpu.VMEM(w_guT.shape, BF16), pltpu.VMEM(w_dnT.shape, BF16)],
        compiler_params=_cparams(("arbitrary",)),
    )(gu, dyp, w_guT, w_dnT)


def _post_attn_bwd(dh1, xh, rs, ln_g, o_sb, o_fx, rr, g_cat, g_down, g_up, w_outT, chunked):
    S, D = dh1.shape
    H = D // 2
    TM = ROW_TILE
    nT = S // TM
    n_ch = len(chunked)
    ex = _Exchange(n_ch, scatter=True)

    def body(*refs):
        dh1_ref, xh_ref, rs_ref, lg_ref, osb_ref, ofx_ref, rr_ref, g_ref, gd_ref, gu_ref, woT_ref = refs[:11]
        ch_in = refs[11:11 + n_ch]
        dxa_ref, dmix_ref, dosb_ref, dofx_ref, sm_ref = refs[11 + n_ch:16 + n_ch]
        ch_out = refs[16 + n_ch:16 + 2 * n_ch]
        sems = refs[16 + 2 * n_ch:]

        @pl.when(pl.program_id(0) == 0)
        def _():
            ex.start(ch_in, ch_out, sems)
            sm_ref[...] = jnp.zeros_like(sm_ref)

        dh = dh1_ref[...]
        xhv = xh_ref[...]
        dhp = _ln_bwd(dh * lg_ref[...], xhv, rs_ref[:, 0:1])
        dxa_ref[...] = ALPHA * dhp
        dmb = dhp.astype(BF16)
        dmix_ref[...] = dmb
        don = _dot(dmb, woT_ref[...])
        o = jnp.concatenate([osb_ref[...], ofx_ref[...]], axis=1)
        r = rr_ref[...]
        u = don * g_ref[...]
        t = _dot_acc(_dot_acc(u * o, gd_ref[...]) * (1.0 / HEAD_DIM), gu_ref[...])
        do = r * u - o * (r * r * r) * t
        dosb_ref[...] = do[:, :H]
        dofx_ref[...] = do[:, H:]
        sm_ref[0:1, :] += jnp.sum(dh * xhv, axis=0, keepdims=True)
        sm_ref[1:2, :] += jnp.sum(dh, axis=0, keepdims=True)
        sm_ref[2:3, :] += jnp.sum(don * o * r, axis=0, keepdims=True)

        @pl.when(pl.program_id(0) == nT - 1)
        def _():
            ex.finish(ch_in, ch_out, sems)

    row = lambda w: pl.BlockSpec((TM, w), lambda i: (i, 0))
    any_spec = pl.BlockSpec(memory_space=pl.ANY)
    return pl.pallas_call(
        body, name="post_attn_bwd", grid=(nT,),
        in_specs=[row(D), row(D), row(LANES), _full((1, D)), row(H), row(H), row(D), _full((1, D)),
                  _full((D, LANES)), _full((LANES, D)), _full((D, D))] + [any_spec] * n_ch,
        out_specs=[row(D), row(D), row(H), row(H), _full((8, D))] + [any_spec] * n_ch,
        out_shape=[jax.ShapeDtypeStruct((S, D), F32), jax.ShapeDtypeStruct((S, D), BF16),
                   jax.ShapeDtypeStruct((S, H), F32), jax.ShapeDtypeStruct((S, H), F32),
                   jax.ShapeDtypeStruct((8, D), F32)] + ex.out_shapes(chunked),
        scratch_shapes=ex.sem_shapes(),
        compiler_params=_cparams(("arbitrary",)),
    )(dh1, xh, rs, ln_g, o_sb, o_fx, rr, g_cat, g_down, g_up, w_outT, *chunked)


def _proj_bwd(dxa, x, pieces, piece_t, dc, u, w_qkvT, w_fT, n_fox, chunked):
    S, D = dxa.shape
    H = D // 2
    TM = ROW_TILE
    nT = S // TM
    tri = _tri(TM, lambda r, c: c >= r)
    n_p = len(pieces)
    n_ch = len(chunked)
    ex = _Exchange(n_ch, scatter=True)

    def body(*refs):
        dxa_ref, x_ref = refs[:2]
        p_refs = refs[2:2 + n_p]
        dc_ref, u_ref, wq_ref, wf_ref, tri_ref = refs[2 + n_p:7 + n_p]
        ch_in = refs[7 + n_p:7 + n_p + n_ch]
        dx_ref, gwf_ref, sm_ref = refs[7 + n_p + n_ch:10 + n_p + n_ch]
        ch_out = refs[10 + n_p + n_ch:10 + n_p + 2 * n_ch]
        run_ref = refs[10 + n_p + 2 * n_ch]
        sems = refs[11 + n_p + 2 * n_ch:]

        @pl.when(pl.program_id(0) == 0)
        def _():
            ex.start(ch_in, ch_out, sems)
            run_ref[...] = jnp.zeros_like(run_ref)
            sm_ref[...] = jnp.zeros_like(sm_ref)
            gwf_ref[...] = jnp.zeros_like(gwf_ref)

        hi, mid, lo = _split3(dc_ref[...])
        t = tri_ref[...]
        dlogf = _dot(t, hi) + _dot(t, mid) + _dot(t, lo) + run_ref[...]
        run_ref[...] = dlogf[0:1, :]
        uv = u_ref[...]
        lane = lax.broadcasted_iota(I32, uv.shape, 1)
        df = jnp.where(lane < n_fox, dlogf * jax.nn.sigmoid(-uv), 0.0)
        sm_ref[0:1, :] += jnp.sum(df, axis=0, keepdims=True)
        dfb = df.astype(BF16)
        gwf_ref[...] += _dot(jnp.transpose(df).astype(BF16), x_ref[...].astype(BF16))
        acc = dxa_ref[...] + _dot(dfb, wf_ref[...])
        for a in range(n_p):
            mm = _dot_tn if piece_t[a] else _dot
            acc = acc + mm(p_refs[a][...].astype(BF16), wq_ref[a * H:(a + 1) * H, :])
        dx_ref[...] = acc

        @pl.when(pl.program_id(0) == nT - 1)
        def _():
            ex.finish(ch_in, ch_out, sems)

    rev = lambda w: pl.BlockSpec((TM, w), lambda i: (nT - 1 - i, 0))
    any_spec = pl.BlockSpec(memory_space=pl.ANY)
    return pl.pallas_call(
        body, name="proj_bwd", grid=(nT,),
        in_specs=[rev(D), rev(D)]
        + [pl.BlockSpec((H, TM), lambda i: (0, nT - 1 - i)) if t else rev(H) for t in piece_t]
        + [rev(LANES), rev(LANES), _full(w_qkvT.shape), _full(w_fT.shape), _full(tri.shape)] + [any_spec] * n_ch,
        out_specs=[rev(D), _full((LANES, D)), _full((8, LANES))] + [any_spec] * n_ch,
        out_shape=[jax.ShapeDtypeStruct((S, D), F32), jax.ShapeDtypeStruct((LANES, D), F32),
                   jax.ShapeDtypeStruct((8, LANES), F32)] + ex.out_shapes(chunked),
        scratch_shapes=[pltpu.VMEM((1, LANES), F32)] + ex.sem_shapes(),
        compiler_params=_cparams(("arbitrary",)),
    )(dxa, x, *pieces, dc, u, w_qkvT, w_fT, tri, *chunked)


def _matmul_tn(a, bs, name, n_split=1, pad_cols=0, b_t=None):
    S, M = a.shape
    b_t = tuple(b_t) if b_t is not None else (False,) * len(bs)
    widths = [b.shape[0] if t else b.shape[1] for b, t in zip(bs, b_t)]
    N = sum(widths) + pad_cols
    assert n_split == 1 or (len(bs) == 1 and pad_cols == 0 and not b_t[0])
    TK = 512 if S % 512 == 0 else ROW_TILE
    MC = 512 if M % 512 == 0 else 256
    nb = len(bs)

    def body(*refs):
        a_ref, b_refs, o_ref, acc_ref = refs[0], refs[1:1 + nb], refs[1 + nb], refs[2 + nb]
        t_refs = list(refs[3 + nb:])

        @pl.when(pl.program_id(1) == 0)
        def _():
            acc_ref[...] = jnp.zeros_like(acc_ref)
            for t_ref in t_refs:
                t_ref[...] = jnp.zeros_like(t_ref)

        n0 = 0
        turned = []
        for b_ref, t, w in zip(b_refs, b_t, widths):
            bv = b_ref[...].astype(BF16)
            if t:
                t_ref = t_refs[len(turned)]
                t_ref[...] += _dot(bv, a_ref[...].astype(BF16))
                turned.append((t_ref, n0, w))
            else:
                for m0 in range(0, M, MC):
                    acc_ref[m0:m0 + MC, n0:n0 + w] += _dot_tn(a_ref[:, m0:m0 + MC].astype(BF16), bv)
            n0 += w

        @pl.when(pl.program_id(1) == S // TK - 1)
        def _():
            for t_ref, c0, w in turned:
                acc_ref[:, c0:c0 + w] = jnp.transpose(t_ref[...])
            o_ref[...] = acc_ref[...].astype(BF16)

    return pl.pallas_call(
        body, name=name, grid=(n_split, S // TK),
        in_specs=[pl.BlockSpec((TK, M), lambda n, k: (k, 0))]
        + [pl.BlockSpec((w, TK), lambda n, k: (0, k)) if t else pl.BlockSpec((TK, w // n_split), lambda n, k: (k, n))
           for w, t in zip(widths, b_t)],
        out_specs=pl.BlockSpec((M, N // n_split), lambda n, k: (0, n)),
        out_shape=jax.ShapeDtypeStruct((M, N), BF16),
        scratch_shapes=[pltpu.VMEM((M, N // n_split), F32)]
        + [pltpu.VMEM((w, M), F32) for w, t in zip(widths, b_t) if t],
        compiler_params=_cparams(("arbitrary", "arbitrary")),
    )(a, *bs)


def _half_mask(half):
    lane = lax.broadcasted_iota(I32, (1, LANES), 1)
    return (lane >= half * HEAD_DIM) & (lane < half * HEAD_DIM + HEAD_DIM)


def _lane_tile(a, width):
    return jnp.concatenate([a] * (width // LANES), axis=1)


def _softplus_neg_abs(z):
    return jnp.log(1.0 + jnp.exp(-jnp.abs(z)))


def _sb_fwd(qkv, n_pair, shards):
    S = qkv.shape[0]
    B = SB_BLOCK
    nq = S // B
    us = _tri(B, lambda r, c: r > c)
    n_sh = len(shards)
    ex = _Exchange(n_sh, scatter=False)

    def body(*refs):
        q_ref, k_ref, v_ref, us_ref = refs[:4]
        sh_in = refs[4:4 + n_sh]
        o_ref, st_ref, js_ref = refs[4 + n_sh:7 + n_sh]
        sh_out = refs[7 + n_sh:7 + 2 * n_sh]
        acc_ref, r_ref = refs[7 + 2 * n_sh:9 + 2 * n_sh]
        sems = refs[9 + 2 * n_sh:]
        p, i = pl.program_id(0), pl.program_id(1)

        @pl.when((p == 0) & (i == 0))
        def _():
            ex.start(sh_in, sh_out, sems)

        hms = [_half_mask(h) for h in range(2)]
        qv = q_ref[...]
        qss = [jnp.where(hm, qv, jnp.zeros_like(qv)) * SCALE for hm in hms]
        row = lax.broadcasted_iota(I32, (B, B), 0)
        col = lax.broadcasted_iota(I32, (B, B), 1)
        tri = col < row
        acc_ref[...] = jnp.zeros_like(acc_ref)
        r_ref[...] = jnp.zeros_like(r_ref)

        def block(j, diag):
            off = pl.multiple_of(j * B, B)
            kj = k_ref[pl.ds(off, B), :]
            vj = v_ref[pl.ds(off, B), :]
            zs = [_dot_nt(qss[h], kj) for h in range(2)]
            sps = [_softplus_neg_abs(z) for z in zs]
            bs = [jnp.minimum(-z, 0.0) - sp for z, sp in zip(zs, sps)]
            if diag:
                bs = [jnp.where(tri, b, 0.0) for b in bs]
            lexcs = [_dot_acc(b, us_ref[...]) for b in bs]
            ws = [jnp.exp(jnp.minimum(z, 0.0) - sp + (_lane_tile(r_ref[h], B) + lexc))
                  for h, (z, sp, lexc) in enumerate(zip(zs, sps, lexcs))]
            if diag:
                ws = [jnp.where(tri, w, 0.0) for w in ws]
            for h in range(2):
                acc_ref[h] += _dot(ws[h].astype(BF16), vj)
                r_ref[h] += jnp.broadcast_to(lexcs[h][:, 0:1] + bs[h][:, 0:1], (B, LANES))

        def live():
            return (jnp.max(r_ref[...]) > -EXP_ZERO).astype(I32)

        def first_two():
            has_prev = (i > 0).astype(F32)
            off_d = pl.multiple_of(i * B, B)
            off_o = pl.multiple_of(jnp.maximum(i - 1, 0) * B, B)
            k_d, v_d = k_ref[pl.ds(off_d, B), :], v_ref[pl.ds(off_d, B), :]
            k_o, v_o = k_ref[pl.ds(off_o, B), :], v_ref[pl.ds(off_o, B), :]
            hs = range(2)
            z_d = [_dot_nt(qss[h], k_d) for h in hs]
            z_o = [_dot_nt(qss[h], k_o) for h in hs]
            sp_d = [_softplus_neg_abs(z) for z in z_d]
            sp_o = [_softplus_neg_abs(z) for z in z_o]
            b_d = [jnp.where(tri, jnp.minimum(-z, 0.0) - sp, 0.0) for z, sp in zip(z_d, sp_d)]
            b_o = [(jnp.minimum(-z, 0.0) - sp) * has_prev for z, sp in zip(z_o, sp_o)]
            l_d = [_dot_acc(b, us_ref[...]) for b in b_d]
            l_o = [_dot_acc(b, us_ref[...]) for b in b_o]
            r_d = [jnp.broadcast_to(l[:, 0:1] + b[:, 0:1], (B, LANES)) for l, b in zip(l_d, b_d)]
            w_d = [jnp.where(tri, jnp.exp(jnp.minimum(z, 0.0) - sp + l), 0.0) for z, sp, l in zip(z_d, sp_d, l_d)]
            w_o = [jnp.exp(jnp.minimum(z, 0.0) - sp + (_lane_tile(r, B) + l)) * has_prev
                   for z, sp, l, r in zip(z_o, sp_o, l_o, r_d)]
            for h in hs:
                acc_ref[h] = _dot(w_d[h].astype(BF16), v_d) + _dot(w_o[h].astype(BF16), v_o)
                r_ref[h] = r_d[h] + jnp.broadcast_to(l_o[h][:, 0:1] + b_o[h][:, 0:1], (B, LANES))

        first_two()

        def step(carry):
            j, _ = carry
            block(j, False)
            return j - 1, live()

        j_end, _ = lax.while_loop(lambda c: (c[0] >= 0) & (c[1] > 0), step, (i - 2, live()))
        js = jnp.maximum(j_end + 1, 0)
        js_ref[2 * p, i] = js
        js_ref[2 * p + 1, i] = js
        o_ref[...] = jnp.where(hms[0], acc_ref[0], acc_ref[1])
        st_ref[...] = r_ref[...]

        @pl.when((p == n_pair - 1) & (i == nq - 1))
        def _():
            ex.finish(sh_in, sh_out, sems)

    any_spec = pl.BlockSpec(memory_space=pl.ANY)
    return pl.pallas_call(
        body, name="sb_attn_fwd", grid=(n_pair, nq),
        in_specs=[pl.BlockSpec((B, LANES), lambda p, i: (i, p)),
                  pl.BlockSpec((S, LANES), lambda p, i: (0, n_pair + p)),
                  pl.BlockSpec((S, LANES), lambda p, i: (0, 2 * n_pair + p)),
                  _full((B, B))] + [any_spec] * n_sh,
        out_specs=[pl.BlockSpec((B, LANES), lambda p, i: (i, p)),
                   pl.BlockSpec((2, B, LANES), lambda p, i: (p, i, 0)),
                   pl.BlockSpec(memory_space=pltpu.SMEM)] + [any_spec] * n_sh,
        out_shape=[jax.ShapeDtypeStruct((S, n_pair * LANES), F32),
                   jax.ShapeDtypeStruct((2 * n_pair, S, LANES), F32),
                   jax.ShapeDtypeStruct((2 * n_pair, nq), I32)] + ex.out_shapes(shards),
        scratch_shapes=[pltpu.VMEM((2, B, LANES), F32), pltpu.VMEM((2, B, LANES), F32)] + ex.sem_shapes(),
        compiler_params=_cparams(("arbitrary", "arbitrary")),
    )(qkv, qkv, qkv, us, *shards)


def _sb_bwd(qkv, do, st, js, n_pair):
    S = qkv.shape[0]
    B = SB_BLOCK
    nq = S // B
    us = _tri(B, lambda r, c: r > c)
    ti = _tri(B, lambda r, c: r <= c)

    def body(js_ref, q_ref, k_ref, v_ref, do_ref, st_ref, us_ref, ti_ref, dq_ref, dk_ref, dv_ref,
             dqa_ref, pr_ref, er_ref):
        p, i = pl.program_id(0), pl.program_id(1)

        @pl.when(i == 0)
        def _():
            dk_ref[...] = jnp.zeros_like(dk_ref)
            dv_ref[...] = jnp.zeros_like(dv_ref)

        hms = [_half_mask(h) for h in range(2)]
        qv = q_ref[...]
        dov = do_ref[...]
        qss = [jnp.where(hm, qv, jnp.zeros_like(qv)) * SCALE for hm in hms]
        dobs = [jnp.where(hm, dov, 0.0).astype(BF16) for hm in hms]
        qs_t = [jnp.transpose(q) for q in qss]
        do_t = [jnp.transpose(d) for d in dobs]
        row = lax.broadcasted_iota(I32, (B, B), 0)
        col = lax.broadcasted_iota(I32, (B, B), 1)
        tri = col < row
        dqa_ref[...] = jnp.zeros_like(dqa_ref)
        pr_ref[...] = jnp.zeros_like(pr_ref)
        er_ref[...] = jnp.zeros_like(er_ref)

        def block(j, diag):
            off = pl.multiple_of(j * B, B)
            kj = k_ref[pl.ds(off, B), :]
            vj = v_ref[pl.ds(off, B), :]
            hs = range(2)
            zs = [_dot_nt(qss[h], kj) for h in hs]
            dws = [_dot_nt(dobs[h], vj) for h in hs]
            sps = [_softplus_neg_abs(z) for z in zs]
            bs = [jnp.minimum(-z, 0.0) - sp for z, sp in zip(zs, sps)]
            if diag:
                bs = [jnp.where(tri, b, 0.0) for b in bs]
            lexcs = [_dot_acc(b, us_ref[...]) for b in bs]
            ws = []
            for h in hs:
                pr_new = pr_ref[h] + jnp.broadcast_to(lexcs[h][:, 0:1] + bs[h][:, 0:1], (B, LANES))
                pr_ref[h] = pr_new
                w = jnp.exp(jnp.minimum(zs[h], 0.0) - sps[h] + (_lane_tile(st_ref[h] - pr_new, B) + lexcs[h]))
                ws.append(jnp.where(tri, w, 0.0) if diag else w)
            es = [dw * w for dw, w in zip(dws, ws)]
            eincs = [_dot_acc(e, ti_ref[...]) for e in es]
            dzbs = []
            for h in hs:
                er = er_ref[h]
                big_e = _lane_tile(er, B) + (eincs[h] - es[h])
                er_ref[h] = er + jnp.broadcast_to(eincs[h][:, B - 1:B], (B, LANES))
                eb = jnp.exp(bs[h])
                dzbs.append((es[h] * eb - big_e * (1.0 - eb)).astype(BF16))
            for h in hs:
                dqa_ref[h] += _dot(dzbs[h], kj)
            dk_ref[:, pl.ds(off, B)] += _dot(qs_t[0], dzbs[0]) + _dot(qs_t[1], dzbs[1])
            dv_ref[:, pl.ds(off, B)] += _dot(do_t[0], ws[0].astype(BF16)) + _dot(do_t[1], ws[1].astype(BF16))

        def step(j, carry):
            block(j, False)
            return carry

        def last_two():
            has_prev = (i > 0).astype(F32)
            off_d = pl.multiple_of(i * B, B)
            off_o = pl.multiple_of(jnp.maximum(i - 1, 0) * B, B)
            k_d, v_d = k_ref[pl.ds(off_d, B), :], v_ref[pl.ds(off_d, B), :]
            k_o, v_o = k_ref[pl.ds(off_o, B), :], v_ref[pl.ds(off_o, B), :]
            hs = range(2)
            z_o = [_dot_nt(qss[h], k_o) for h in hs]
            z_d = [_dot_nt(qss[h], k_d) for h in hs]
            dw_o = [_dot_nt(dobs[h], v_o) for h in hs]
            dw_d = [_dot_nt(dobs[h], v_d) for h in hs]
            sp_o = [_softplus_neg_abs(z) for z in z_o]
            sp_d = [_softplus_neg_abs(z) for z in z_d]
            b_o = [(jnp.minimum(-z, 0.0) - sp) * has_prev for z, sp in zip(z_o, sp_o)]
            b_d = [jnp.where(tri, jnp.minimum(-z, 0.0) - sp, 0.0) for z, sp in zip(z_d, sp_d)]
            l_o = [_dot_acc(b, us_ref[...]) for b in b_o]
            l_d = [_dot_acc(b, us_ref[...]) for b in b_d]
            w_o, w_d = [], []
            for h in hs:
                pr1 = pr_ref[h] + jnp.broadcast_to(l_o[h][:, 0:1] + b_o[h][:, 0:1], (B, LANES))
                pr2 = pr1 + jnp.broadcast_to(l_d[h][:, 0:1] + b_d[h][:, 0:1], (B, LANES))
                st = st_ref[h]
                w_o.append(jnp.exp(jnp.minimum(z_o[h], 0.0) - sp_o[h] + (_lane_tile(st - pr1, B) + l_o[h])) * has_prev)
                w_d.append(jnp.where(tri, jnp.exp(jnp.minimum(z_d[h], 0.0) - sp_d[h]
                                                  + (_lane_tile(st - pr2, B) + l_d[h])), 0.0))
            e_o = [dw * w for dw, w in zip(dw_o, w_o)]
            e_d = [dw * w for dw, w in zip(dw_d, w_d)]
            ei_o = [_dot_acc(e, ti_ref[...]) for e in e_o]
            ei_d = [_dot_acc(e, ti_ref[...]) for e in e_d]
            dz_o, dz_d = [], []
            for h in hs:
                er = er_ref[h]
                big_o = _lane_tile(er, B) + (ei_o[h] - e_o[h])
                er1 = er + jnp.broadcast_to(ei_o[h][:, B - 1:B], (B, LANES))
                big_d = _lane_tile(er1, B) + (ei_d[h] - e_d[h])
                eb_o, eb_d = jnp.exp(b_o[h]), jnp.exp(b_d[h])
                dz_o.append((e_o[h] * eb_o - big_o * (1.0 - eb_o)).astype(BF16))
                dz_d.append((e_d[h] * eb_d - big_d * (1.0 - eb_d)).astype(BF16))
            for h in hs:
                dqa_ref[h] += _dot(dz_o[h], k_o) + _dot(dz_d[h], k_d)
            dk_ref[:, pl.ds(off_o, B)] += _dot(qs_t[0], dz_o[0]) + _dot(qs_t[1], dz_o[1])
            dv_ref[:, pl.ds(off_o, B)] += _dot(do_t[0], w_o[0].astype(BF16)) + _dot(do_t[1], w_o[1].astype(BF16))
            dk_ref[:, pl.ds(off_d, B)] += _dot(qs_t[0], dz_d[0]) + _dot(qs_t[1], dz_d[1])
            dv_ref[:, pl.ds(off_d, B)] += _dot(do_t[0], w_d[0].astype(BF16)) + _dot(do_t[1], w_d[1].astype(BF16))

        lax.fori_loop(js_ref[2 * p, i], i - 1, step, 0)
        last_two()
        dq_ref[...] = jnp.where(hms[0], dqa_ref[0], dqa_ref[1]) * SCALE

    W = n_pair * LANES
    return pl.pallas_call(
        body, name="sb_attn_bwd",
        grid_spec=pltpu.PrefetchScalarGridSpec(
            num_scalar_prefetch=1, grid=(n_pair, nq),
            in_specs=[pl.BlockSpec((B, LANES), lambda p, i, js: (i, p)),
                      pl.BlockSpec((S, LANES), lambda p, i, js: (0, n_pair + p)),
                      pl.BlockSpec((S, LANES), lambda p, i, js: (0, 2 * n_pair + p)),
                      pl.BlockSpec((B, LANES), lambda p, i, js: (i, p)),
                      pl.BlockSpec((2, B, LANES), lambda p, i, js: (p, i, 0)),
                      pl.BlockSpec((B, B), lambda p, i, js: (0, 0)),
                      pl.BlockSpec((B, B), lambda p, i, js: (0, 0))],
            out_specs=[pl.BlockSpec((B, LANES), lambda p, i, js: (i, p)),
                       pl.BlockSpec((LANES, S), lambda p, i, js: (p, 0)),
                       pl.BlockSpec((LANES, S), lambda p, i, js: (p, 0))],
            scratch_shapes=[pltpu.VMEM((2, B, LANES), F32), pltpu.VMEM((2, B, LANES), F32),
                            pltpu.VMEM((2, B, LANES), F32)]),
        out_shape=[jax.ShapeDtypeStruct((S, W), F32)] + [jax.ShapeDtypeStruct((W, S), F32)] * 2,
        compiler_params=_cparams(("arbitrary", "arbitrary")),
    )(js, qkv, qkv, qkv, do, st, us, ti)


def _head_column(blk, head):
    lane = lax.broadcasted_iota(I32, (1, LANES), 1)
    return jnp.sum(jnp.where(lane == head, blk, 0.0), axis=1, keepdims=True)


def _fox_fwd(qkv, c, c_rows, kmax, n_pair):
    S = qkv.shape[0]
    BQ, BK = FOX_BQ, FOX_BK
    R = BQ // BK
    nq = S // BQ

    def body(q_ref, k_ref, v_ref, c_ref, cr_ref, km_ref, o_ref, st_ref, js_ref, acc_ref, m_ref, cb_ref, qkb_ref):
        p, i, half = pl.program_id(0), pl.program_id(1), pl.program_id(2)
        hm = _half_mask(half)
        qv = q_ref[...]
        qs = jnp.where(hm, qv, jnp.zeros_like(qv)) * SCALE
        ccol = _head_column(c_ref[...], 2 * p + half)
        cb_ref[...] = jnp.broadcast_to(ccol, (BQ, BK))
        qf = qs.astype(F32)
        qkb_ref[...] = jnp.broadcast_to(
            jnp.sqrt(jnp.sum(qf * qf, axis=1, keepdims=True)) * NORM_SLACK
            * _head_column(km_ref[...], 2 * p + half) + ccol, (BQ, LANES))
        row = lax.broadcasted_iota(I32, (BQ, BK), 0)
        col = lax.broadcasted_iota(I32, (BQ, BK), 1)
        acc_ref[...] = jnp.zeros_like(acc_ref)
        m_ref[...] = jnp.full_like(m_ref, NEG_BIG)

        def blocks(j_top, diag):
            ss, v1s, keeps = [], [], []
            for d in range(R):
                j = j_top - d
                off = pl.multiple_of(j * BK, BK)
                vj = v_ref[pl.ds(off, BK), :]
                v1s.append(jnp.where(hm, vj, jnp.ones_like(vj)))
                s = _dot_nt(qs, k_ref[pl.ds(off, BK), :]) + (cb_ref[...] - cr_ref[0, pl.ds(j, 1), :])
                if diag:
                    keeps.append(col + (R - 1 - d) * BK <= row)
                    s = jnp.where(keeps[-1], s, NEG_BIG)
                ss.append(s)
            m_old = m_ref[...]
            s_max = jnp.max(functools.reduce(jnp.maximum, ss), axis=1, keepdims=True)
            m_new = jnp.maximum(m_old, jnp.broadcast_to(s_max, (BQ, LANES)))
            m_wide = _lane_tile(m_new, BK)
            pvs = [jnp.exp(s - m_wide) for s in ss]
            if diag:
                pvs = [jnp.where(keep, pv, 0.0) for keep, pv in zip(keeps, pvs)]
            new = _dot(pvs[0].astype(BF16), v1s[0])
            for pv, v1 in zip(pvs[1:], v1s[1:]):
                new = new + _dot(pv.astype(BF16), v1)
            acc_ref[...] = jnp.exp(m_old - m_new) * acc_ref[...] + new
            m_ref[...] = m_new

        def live(j):
            c_end = cr_ref[0, pl.ds(jnp.maximum(j, 0), 1), :][:, BK - 1:BK]
            return (jnp.max(qkb_ref[...] - c_end - m_ref[...]) > -EXP_ZERO).astype(I32)

        blocks(R * i + R - 1, True)

        def step(carry):
            j, _ = carry
            go_on = live(j - R)
            blocks(j, False)
            return j - R, go_on

        j_end, _ = lax.while_loop(lambda cr: (cr[0] >= 0) & (cr[1] > 0), step, (R * i - 1, live(R * i - 1)))
        js_ref[2 * p + half, i] = j_end + 1
        acc = acc_ref[...]
        denom = jnp.where(hm, pltpu.roll(acc, HEAD_DIM, 1), acc)
        res = jnp.where(hm, acc / denom, 0.0)

        @pl.when(half == 0)
        def _():
            o_ref[...] = res

        @pl.when(half == 1)
        def _():
            o_ref[...] += res

        st_ref[0] = m_ref[...] + jnp.log(denom)

    return pl.pallas_call(
        body, name="fox_attn_fwd", grid=(n_pair, nq, 2),
        in_specs=[pl.BlockSpec((BQ, LANES), lambda p, i, h: (i, 3 * n_pair + p)),
                  pl.BlockSpec((S, LANES), lambda p, i, h: (0, 4 * n_pair + p)),
                  pl.BlockSpec((S, LANES), lambda p, i, h: (0, 5 * n_pair + p)),
                  pl.BlockSpec((BQ, LANES), lambda p, i, h: (i, 0)),
                  pl.BlockSpec((1, S // BK, BK), lambda p, i, h: (2 * p + h, 0, 0)),
                  _full((1, LANES))],
        out_specs=[pl.BlockSpec((BQ, LANES), lambda p, i, h: (i, p)),
                   pl.BlockSpec((1, BQ, LANES), lambda p, i, h: (2 * p + h, i, 0)),
                   pl.BlockSpec(memory_space=pltpu.SMEM)],
        out_shape=[jax.ShapeDtypeStruct((S, n_pair * LANES), F32),
                   jax.ShapeDtypeStruct((2 * n_pair, S, LANES), F32),
                   jax.ShapeDtypeStruct((2 * n_pair, nq), I32)],
        scratch_shapes=[pltpu.VMEM((BQ, LANES), F32), pltpu.VMEM((BQ, LANES), F32), pltpu.VMEM((BQ, BK), F32),
                        pltpu.VMEM((BQ, LANES), F32)],
        compiler_params=_cparams(("arbitrary", "arbitrary", "arbitrary")),
    )(qkv, qkv, qkv, c, c_rows, kmax)


def _fox_bwd(qkv, do, o, st, c, c_rows, js, n_pair):
    S = qkv.shape[0]
    B = FOX_BLOCK
    nq = S // B

    def body(js_ref, q_ref, k_ref, v_ref, do_ref, o_ref, st_ref, c_ref, cr_ref, dq_ref, dk_ref, dv_ref,
             dc_ref, dqa_ref, rs_ref, cb_ref, db_ref):
        p, i, half = pl.program_id(0), pl.program_id(1), pl.program_id(2)

        @pl.when((i == 0) & (half == 0))
        def _():
            dk_ref[...] = jnp.zeros_like(dk_ref)
            dv_ref[...] = jnp.zeros_like(dv_ref)
            dc_ref[...] = jnp.zeros_like(dc_ref)

        hm = _half_mask(half)
        qv = q_ref[...]
        qs = jnp.where(hm, qv, jnp.zeros_like(qv)) * SCALE
        dov = jnp.where(hm, do_ref[...], 0.0)
        dob = dov.astype(BF16)
        qs_t = jnp.transpose(qs)
        do_t = jnp.transpose(dob)
        cb_ref[...] = jnp.broadcast_to(_head_column(c_ref[...], 2 * p + half), (B, LANES)) - st_ref[0]
        db_ref[...] = jnp.broadcast_to(jnp.sum(dov * o_ref[...], axis=1, keepdims=True), (B, LANES))
        row = lax.broadcasted_iota(I32, (B, B), 0)
        col = lax.broadcasted_iota(I32, (B, B), 1)
        dqa_ref[...] = jnp.zeros_like(dqa_ref)
        rs_ref[...] = jnp.zeros_like(rs_ref)

        def block(j, diag):
            off = pl.multiple_of(j * B, B)
            kj = k_ref[pl.ds(off, B), :]
            vj = v_ref[pl.ds(off, B), :]
            pv = jnp.exp(_dot_nt(qs, kj) + (_lane_tile(cb_ref[...], B) - cr_ref[0, pl.ds(j, 1), :]))
            if diag:
                pv = jnp.where(col <= row, pv, 0.0)
            ds = pv * (_dot_nt(dob, vj) - _lane_tile(db_ref[...], B))
            dsb = ds.astype(BF16)
            dqa_ref[...] += _dot(dsb, kj)
            dk_ref[:, pl.ds(off, B)] += _dot(qs_t, dsb)
            dv_ref[:, pl.ds(off, B)] += _dot(do_t, pv.astype(BF16))
            dc_ref[0, half, pl.ds(j, 1), :] -= jnp.sum(ds, axis=0, keepdims=True)
            rs_ref[...] += jnp.sum(ds, axis=1, keepdims=True)

        def step(j, carry):
            block(j, False)
            return carry

        lax.fori_loop(js_ref[2 * p + half, i], i, step, 0)
        block(i, True)
        res = jnp.where(hm, dqa_ref[...] * SCALE, 0.0)

        @pl.when(half == 0)
        def _():
            dq_ref[...] = res

        @pl.when(half == 1)
        def _():
            dq_ref[...] += res

        dc_ref[0, half, pl.ds(i, 1), :] += jnp.transpose(jnp.broadcast_to(rs_ref[...], (B, LANES)))[0:1, :]

    W = n_pair * LANES
    return pl.pallas_call(
        body, name="fox_attn_bwd",
        grid_spec=pltpu.PrefetchScalarGridSpec(
            num_scalar_prefetch=1, grid=(n_pair, nq, 2),
            in_specs=[pl.BlockSpec((B, LANES), lambda p, i, h, js: (i, 3 * n_pair + p)),
                      pl.BlockSpec((S, LANES), lambda p, i, h, js: (0, 4 * n_pair + p)),
                      pl.BlockSpec((S, LANES), lambda p, i, h, js: (0, 5 * n_pair + p)),
                      pl.BlockSpec((B, LANES), lambda p, i, h, js: (i, p)),
                      pl.BlockSpec((B, LANES), lambda p, i, h, js: (i, p)),
                      pl.BlockSpec((1, B, LANES), lambda p, i, h, js: (2 * p + h, i, 0)),
                      pl.BlockSpec((B, LANES), lambda p, i, h, js: (i, 0)),
                      pl.BlockSpec((1, nq, B), lambda p, i, h, js: (2 * p + h, 0, 0))],
            out_specs=[pl.BlockSpec((B, LANES), lambda p, i, h, js: (i, p)),
                       pl.BlockSpec((LANES, S), lambda p, i, h, js: (p, 0)),
                       pl.BlockSpec((LANES, S), lambda p, i, h, js: (p, 0)),
                       pl.BlockSpec((1, 2, nq, B), lambda p, i, h, js: (p, 0, 0, 0))],
            scratch_shapes=[pltpu.VMEM((B, LANES), F32), pltpu.VMEM((B, 1), F32), pltpu.VMEM((B, LANES), F32),
                            pltpu.VMEM((B, LANES), F32)]),
        out_shape=[jax.ShapeDtypeStruct((S, W), F32)] + [jax.ShapeDtypeStruct((W, S), F32)] * 2
        + [jax.ShapeDtypeStruct((n_pair, 2, nq, B), F32)],
        compiler_params=_cparams(("arbitrary", "arbitrary", "arbitrary")),
    )(js, qkv, qkv, qkv, do, o, st, c, c_rows)


def _adam(w, g, m, v):
    m = ADAM_B1 * m + (1.0 - ADAM_B1) * g
    v = ADAM_B2 * v + (1.0 - ADAM_B2) * (g * g)
    m_hat = m / (1.0 - ADAM_B1 ** ADAM_STEP)
    v_hat = v / (1.0 - ADAM_B2 ** ADAM_STEP)
    delta = -ADAM_LR * (m_hat / (jnp.sqrt(v_hat) + ADAM_EPS) + ADAM_WD * w)
    return delta, m, v


def _reduce_adam(landing, w, m, v, name, extra=None):
    _, R, C = w.shape
    TR = next(t for t in (256, 128, R) if R % t == 0)
    more = [] if extra is None else [extra]

    def body(*refs):
        l_ref, w_ref, m_ref, v_ref = refs[:4]
        g_ref, d_ref, nm_ref, nv_ref = refs[4 + len(more):]
        g = l_ref[0].astype(F32)
        for s in range(1, N_DEV):
            g = g + l_ref[s].astype(F32)
        if more:
            g = g + refs[4][...]
        d, nm, nv = _adam(w_ref[0], g, m_ref[0], v_ref[0])
        g_ref[0] = g
        d_ref[0] = d
        nm_ref[0] = nm
        nv_ref[0] = nv

    blk = pl.BlockSpec((1, TR, C), lambda i: (0, i, 0))
    return pl.pallas_call(
        body, name=name, grid=(R // TR,),
        in_specs=[pl.BlockSpec((N_DEV, TR, C), lambda i: (0, i, 0)), blk, blk, blk]
        + [pl.BlockSpec((TR, C), lambda i: (i, 0))] * len(more),
        out_specs=[blk] * 4,
        out_shape=[jax.ShapeDtypeStruct((1, R, C), F32)] * 4,
        compiler_params=_cparams(("arbitrary",)),
    )(landing, w, m, v, *more)


def _reduce_adam_small(landing, w, m, v):
    R, C = w.shape

    def body(l_ref, w_ref, m_ref, v_ref, g_ref, d_ref, nm_ref, nv_ref, loss_ref):
        g = l_ref[0]
        for s in range(1, N_DEV):
            g = g + l_ref[s]
        d, nm, nv = _adam(w_ref[...], g, m_ref[...], v_ref[...])
        g_ref[...] = g
        d_ref[...] = d
        nm_ref[...] = nm
        nv_ref[...] = nv
        loss_ref[...] = jnp.broadcast_to(0.5 * jnp.sum(g[7:8, :], axis=1, keepdims=True), (1, LANES))

    return pl.pallas_call(
        body, name="reduce_adam_small",
        out_shape=[jax.ShapeDtypeStruct((R, C), F32)] * 4 + [jax.ShapeDtypeStruct((1, LANES), F32)],
    )(landing, w, m, v)


def _pad_lanes(a, width):
    return jnp.pad(a, ((0, 0), (0, width - a.shape[1])))


def _pack_small(D, n_fox, g_cat, l1g, l1b, l2g, l2b, bf, last, tail):
    return jnp.concatenate([g_cat, l1g, l1b, l2g, l2b, _pad_lanes(bf, D), jnp.zeros((1, D), F32), last, tail],
                           axis=0)


def kernel(x, w_in, b_f, g_sb, g_fox, w_out, ln1_g, ln1_b, ln2_g, ln2_b, w_gate_up, w_down, loss_target, m_w_in, m_b_f, m_g_sb, m_g_fox, m_w_out, m_ln1_g, m_ln1_b, m_ln2_g, m_ln2_b, m_w_gate_up, m_w_down, v_w_in, v_b_f, v_g_sb, v_g_fox, v_w_out, v_ln1_g, v_ln1_b, v_ln2_g, v_ln2_b, v_w_gate_up, v_w_down):
    x2, tgt = x[0], loss_target[0]
    S, D = x2.shape
    W = D // 2
    n_pair = W // LANES
    n_fox = W // HEAD_DIM
    F = w_down.shape[1] * N_DEV

    g_in = _all_gather_two_level(w_in[0].astype(BF16), "w_in_all_gather")
    w_in_full = g_in.transpose(1, 0, 2).reshape(D, -1)
    w_qkv = w_in_full[:, :6 * W]
    w_f = _pad_lanes(w_in_full[:, 6 * W:], LANES)
    g_down = (lax.broadcasted_iota(I32, (D, LANES), 0) // HEAD_DIM
              == lax.broadcasted_iota(I32, (D, LANES), 1)).astype(BF16)
    g_up = g_down.T
    g_cat = jnp.concatenate([g_sb, g_fox], axis=1)

    qkv, u, c, ksq = _proj_fwd(x2, w_qkv, w_f, _pad_lanes(b_f, LANES), n_fox)
    c_t = c[:, :n_fox].T
    c_rows = c_t.reshape(n_fox, S // FOX_BLOCK, FOX_BLOCK)
    kmax = jnp.sqrt(ksq[0:1]) * NORM_SLACK
    o_sb, st_sb, js_sb, g_out, g_gu, g_dn = _sb_fwd(
        qkv, n_pair, [w_out[0].astype(BF16), w_gate_up[0].astype(BF16), w_down[0].astype(BF16)])
    w_o = g_out.reshape(D, D)
    w_gu = g_gu.transpose(1, 0, 2).reshape(D, 2 * F)
    w_dn = g_dn.reshape(F, D)
    o_fx, st_fx, js_fx = _fox_fwd(qkv, c, c_t.reshape(n_fox, S // FOX_BK, FOX_BK), kmax, n_pair)
    js_fx = js_fx // (FOX_BLOCK // FOX_BK)
    h1, xh1, rs1, on_b, rr = _post_attn_fwd(o_sb, o_fx, x2, g_cat, g_down, g_up, w_o, ln1_g, ln1_b)
    gu, act_b, dyp, sm2 = _mlp_fwd(h1, tgt, w_gu, w_dn, ln2_g, ln2_b)

    dgu_b, dh1 = _mlp_bwd(gu, dyp, w_gu.T, w_dn.T)
    gw_gu = _matmul_tn(h1, [dgu_b], "grad_w_gate_up", n_split=2)
    gw_dn = _matmul_tn(act_b, [dyp], "grad_w_down")
    dxa, dmix_b, do_sb, do_fx, sm1, l_gu, l_dn = _post_attn_bwd(
        dh1, xh1, rs1, ln1_g, o_sb, o_fx, rr, g_cat, g_down, g_up, w_o.T,
        [gw_gu.reshape(D, N_DEV, -1).transpose(1, 0, 2), gw_dn.reshape(N_DEV, F // N_DEV, D)])
    dq_sb, dk_sb, dv_sb = _sb_bwd(qkv, do_sb, st_sb, js_sb, n_pair)
    dq_fx, dk_fx, dv_fx, dcr = _fox_bwd(qkv, do_fx, o_fx, st_fx, c, c_rows, js_fx, n_pair)
    dc = _pad_lanes(dcr.reshape(n_fox, S).T, LANES)
    pieces = [dq_sb, dk_sb, dv_sb, dq_fx, dk_fx, dv_fx]
    piece_t = (False, True, True, False, True, True)
    gw_qkv = _matmul_tn(x2, pieces, "grad_w_qkv", pad_cols=LANES, b_t=piece_t)[:, :6 * W + n_fox]
    gw_out = _matmul_tn(on_b, [dmix_b], "grad_w_out")
    dx, gw_f_t, sm0, l_in, l_out = _proj_bwd(
        dxa, x2, pieces, piece_t, dc, u, w_qkv.T, w_f.T, n_fox,
        [gw_qkv.reshape(D, N_DEV, -1).transpose(1, 0, 2), gw_out.reshape(N_DEV, D // N_DEV, D)])

    small = _pack_small(D, n_fox, sm1[2:3], sm1[0:1], sm1[1:2], sm2[0:1], sm2[1:2], sm0[0:1, :n_fox],
                        sm2[2:3] * (1.0 / D), gw_f_t[:n_fox])
    (l_small,) = _exchange_call([jnp.broadcast_to(small[None], (N_DEV,) + small.shape)], True, "small_exchange")

    zero = jnp.zeros((1, D), F32)
    pack = lambda gc, a, b_, c_, d_, bf: _pack_small(D, n_fox, gc, a, b_, c_, d_, bf, zero,
                                                     jnp.zeros((n_fox, D), F32))
    r_small = _reduce_adam_small(
        l_small,
        pack(g_cat, ln1_g, ln1_b, ln2_g, ln2_b, b_f),
        pack(jnp.concatenate([m_g_sb, m_g_fox], axis=1), m_ln1_g, m_ln1_b, m_ln2_g, m_ln2_b, m_b_f),
        pack(jnp.concatenate([v_g_sb, v_g_fox], axis=1), v_ln1_g, v_ln1_b, v_ln2_g, v_ln2_b, v_b_f))
    loss = r_small[4][0, 0]
    cols = w_in.shape[2]
    gf_cols = jnp.pad(r_small[0][8:8 + n_fox].T, ((0, 0), (cols - n_fox, 0)))
    extra = jnp.where(_my_index() == N_DEV - 1, gf_cols, 0.0)
    r_in = _reduce_adam(l_in, w_in, m_w_in, v_w_in, "reduce_adam_w_in", extra)
    r_out = _reduce_adam(l_out, w_out, m_w_out, v_w_out, "reduce_adam_w_out")
    r_gu = _reduce_adam(l_gu, w_gate_up, m_w_gate_up, v_w_gate_up, "reduce_adam_w_gate_up")
    r_dn = _reduce_adam(l_dn, w_down, m_w_down, v_w_down, "reduce_adam_w_down")

    def unpack(kind):
        big = [r_in[kind], None, None, None, r_out[kind], None, None, None, None, r_gu[kind], r_dn[kind]]
        s = r_small[kind]
        big[1] = s[5:6, :n_fox]
        big[2] = s[0:1, :W]
        big[3] = s[0:1, W:]
        big[5], big[6], big[7], big[8] = s[1:2], s[2:3], s[3:4], s[4:5]
        return big

    return (loss, dx[None], *unpack(0), *unpack(1), *unpack(2), *unpack(3))
```

```python
import functools

import jax
import jax.numpy as jnp
from jax import lax
from jax.experimental import pallas as pl
from jax.experimental.pallas import tpu as pltpu

F32 = jnp.float32
BF16 = jnp.bfloat16
I32 = jnp.int32

N_DEV = 8
HEAD_DIM = 64
LANES = 128
SCALE = HEAD_DIM ** -0.5
ALPHA = 2.0 ** 0.25
LN_EPS = 1e-5
RMS_EPS = 1e-6
ADAM_LR, ADAM_B1, ADAM_B2, ADAM_EPS, ADAM_WD, ADAM_STEP = 0.001, 0.9, 0.999, 1e-08, 0.01, 10
NEG_BIG = -1e30
NORM_SLACK = 1.01
EXP_ZERO = 88.5
VMEM_LIMIT = 60 * 1024 * 1024
ROW_TILE = 256
SB_BLOCK = 256
FOX_BLOCK = 512
FOX_BQ, FOX_BK = 512, 256
assert FOX_BQ == FOX_BLOCK and FOX_BLOCK % FOX_BK == 0
MESH = pl.DeviceIdType.MESH


def _cparams(sem):
    return pltpu.CompilerParams(dimension_semantics=sem, vmem_limit_bytes=VMEM_LIMIT)


def _dot(a, b):
    return jnp.dot(a, b, preferred_element_type=F32)


def _dot_nt(a, b):
    return lax.dot_general(a, b, (((1,), (1,)), ((), ())), preferred_element_type=F32)


def _dot_tn(a, b):
    return lax.dot_general(a, b, (((0,), (0,)), ((), ())), preferred_element_type=F32)


def _split2(a):
    hi = a.astype(BF16)
    lo = (a - hi.astype(F32)).astype(BF16)
    return hi, lo


def _split3(a):
    hi = a.astype(BF16)
    r1 = a - hi.astype(F32)
    mid = r1.astype(BF16)
    lo = (r1 - mid.astype(F32)).astype(BF16)
    return hi, mid, lo


def _dot_acc(a, m):
    hi, lo = _split2(a)
    return _dot(jnp.concatenate([hi, lo], axis=1), jnp.concatenate([m, m], axis=0))


def _tri(n, fn):
    r = lax.broadcasted_iota(I32, (n, n), 0)
    c = lax.broadcasted_iota(I32, (n, n), 1)
    return fn(r, c).astype(BF16)


def _full(shape):
    nd = len(shape)
    return pl.BlockSpec(shape, lambda *_: (0,) * nd)


def _peer(k):
    x, y, c = lax.axis_index("x"), lax.axis_index("y"), lax.axis_index("c")
    return (1 - x if k & 4 else x, 1 - y if k & 2 else y, 1 - c if k & 1 else c)


def _my_index():
    return 4 * lax.axis_index("x") + 2 * lax.axis_index("y") + lax.axis_index("c")


class _Exchange:
    def __init__(self, n, scatter):
        self.n, self.scatter = n, scatter

    def sem_shapes(self):
        return [pltpu.SemaphoreType.DMA(((N_DEV - 1) * self.n,)), pltpu.SemaphoreType.DMA(((N_DEV - 1) * self.n,)),
                pltpu.SemaphoreType.DMA((self.n,))]

    def out_shapes(self, arrays):
        if self.scatter:
            return [jax.ShapeDtypeStruct(s.shape, s.dtype) for s in arrays]
        return [jax.ShapeDtypeStruct((N_DEV,) + s.shape, s.dtype) for s in arrays]

    def _copies(self, ins, outs, sems, landing):
        send_sems, recv_sems, local_sems = sems
        me = _my_index()
        src = lambda a, d: ins[a].at[d] if self.scatter else ins[a]
        local = [pltpu.make_async_copy(src(a, me), outs[a].at[me], local_sems.at[a]) for a in range(self.n)]
        remote = [pltpu.make_async_remote_copy(
            src_ref=src(a, me ^ k), dst_ref=outs[a].at[me ^ k if landing else me],
            send_sem=send_sems.at[(k - 1) * self.n + a], recv_sem=recv_sems.at[(k - 1) * self.n + a],
            device_id=_peer(k), device_id_type=MESH) for k in range(1, N_DEV) for a in range(self.n)]
        return local, remote

    def start(self, ins, outs, sems):
        local, sent = self._copies(ins, outs, sems, landing=False)
        for cp in local + sent:
            cp.start()

    def finish(self, ins, outs, sems):
        local, landed = self._copies(ins, outs, sems, landing=True)
        for cp in landed:
            cp.wait_recv()
        for cp in landed:
            cp.wait_send()
        for cp in local:
            cp.wait()


def _all_gather_two_level(shard, name):
    def body(x_ref, out_ref, send_sems, recv_sems, local_sem):
        x, y, c = lax.axis_index("x"), lax.axis_index("y"), lax.axis_index("c")
        me, sibling = (x, y, c), (x, y, 1 - c)
        chips = [(1 - x, y), (x, 1 - y), (1 - x, 1 - y)]

        def slot(px, py, pc):
            return out_ref.at[4 * px + 2 * py + pc]

        def copy(k, block, to, src=None):
            return pltpu.make_async_remote_copy(
                src_ref=slot(*block) if src is None else src, dst_ref=slot(*block),
                send_sem=send_sems.at[k], recv_sem=recv_sems.at[k], device_id=to, device_id_type=MESH)

        mine = pltpu.make_async_copy(x_ref, slot(*me), local_sem)
        mine.start()
        first = [copy(0, me, sibling, src=x_ref)]
        first += [copy(1 + j, me, (*chip, c), src=x_ref) for j, chip in enumerate(chips)]
        for cp in first:
            cp.start()
        passed = [copy(4 + j, (*chip, c), sibling) for j, chip in enumerate(chips)]
        for j, chip in enumerate(chips):
            copy(1 + j, (*chip, c), me).wait_recv()
            passed[j].start()
        copy(0, sibling, me).wait_recv()
        for j, chip in enumerate(chips):
            copy(4 + j, (*chip, 1 - c), me).wait_recv()
        for cp in first + passed:
            cp.wait_send()
        mine.wait()

    any_spec = pl.BlockSpec(memory_space=pl.ANY)
    return pl.pallas_call(
        body, name=name, out_shape=jax.ShapeDtypeStruct((N_DEV,) + shard.shape, shard.dtype),
        in_specs=[any_spec], out_specs=any_spec,
        scratch_shapes=[pltpu.SemaphoreType.DMA((N_DEV - 1,)), pltpu.SemaphoreType.DMA((N_DEV - 1,)),
                        pltpu.SemaphoreType.DMA],
    )(shard)


def _exchange_call(arrays, scatter, name):
    n = len(arrays)
    ex = _Exchange(n, scatter)

    def body(*refs):
        ins, outs, sems = refs[:n], refs[n:2 * n], refs[2 * n:]
        ex.start(ins, outs, sems)
        ex.finish(ins, outs, sems)

    any_spec = pl.BlockSpec(memory_space=pl.ANY)
    return pl.pallas_call(
        body, name=name, out_shape=ex.out_shapes(arrays),
        in_specs=[any_spec] * n, out_specs=[any_spec] * n, scratch_shapes=ex.sem_shapes(),
    )(*arrays)


def _log_sigmoid(u):
    return jnp.minimum(u, 0.0) - jnp.log1p(jnp.exp(-jnp.abs(u)))


def _proj_fwd(x, w_qkv, w_f, bf_pad, n_fox):
    S, D = x.shape
    N = w_qkv.shape[1]
    W = D // 2
    TM = ROW_TILE
    tri = _tri(TM, lambda r, c: c <= r)
    r_ = lax.broadcasted_iota(I32, (W, LANES), 0)
    c_ = lax.broadcasted_iota(I32, (W, LANES), 1)
    head_of = (r_ // HEAD_DIM == c_).astype(BF16)

    def body(x_ref, wq_ref, wf_ref, bf_ref, tri_ref, ho_ref, qkv_ref, u_ref, c_ref, ksq_ref, run_ref):
        @pl.when(pl.program_id(0) == 0)
        def _():
            run_ref[...] = jnp.zeros_like(run_ref)
            ksq_ref[...] = jnp.zeros_like(ksq_ref)

        xb = x_ref[...].astype(BF16)
        for n0 in range(0, N, D):
            chunk = _dot(xb, wq_ref[:, n0:n0 + D]).astype(BF16)
            qkv_ref[:, n0:n0 + D] = chunk
            if n0 == 4 * W:
                kf = chunk[:, :W].astype(F32)
                ksq = jnp.max(_dot_acc(kf * kf, ho_ref[...]), axis=0, keepdims=True)
                ksq_ref[...] = jnp.maximum(ksq_ref[...], ksq)
        u = _dot(xb, wf_ref[...]) + bf_ref[...]
        lane = lax.broadcasted_iota(I32, u.shape, 1)
        logf = jnp.where(lane < n_fox, _log_sigmoid(u), 0.0)
        u_ref[...] = u
        hi, mid, lo = _split3(logf)
        t = tri_ref[...]
        cs = _dot(t, hi) + _dot(t, mid) + _dot(t, lo) + run_ref[...]
        c_ref[...] = cs
        run_ref[...] = cs[TM - 1:TM, :]

    return pl.pallas_call(
        body, name="proj_fwd", grid=(S // TM,),
        in_specs=[pl.BlockSpec((TM, D), lambda i: (i, 0)), _full(w_qkv.shape), _full(w_f.shape),
                  _full(bf_pad.shape), _full(tri.shape), _full(head_of.shape)],
        out_specs=[pl.BlockSpec((TM, N), lambda i: (i, 0)), pl.BlockSpec((TM, LANES), lambda i: (i, 0)),
                   pl.BlockSpec((TM, LANES), lambda i: (i, 0)), _full((8, LANES))],
        out_shape=[jax.ShapeDtypeStruct((S, N), BF16), jax.ShapeDtypeStruct((S, LANES), F32),
                   jax.ShapeDtypeStruct((S, LANES), F32), jax.ShapeDtypeStruct((8, LANES), F32)],
        scratch_shapes=[pltpu.VMEM((1, LANES), F32)],
        compiler_params=_cparams(("arbitrary",)),
    )(x, w_qkv, w_f, bf_pad, tri, head_of)


def _post_attn_fwd(o_sb, o_fx, x, g_cat, g_down, g_up, w_out, ln_g, ln_b):
    S, D = x.shape
    H = D // 2
    TM = ROW_TILE

    def body(osb_ref, ofx_ref, x_ref, g_ref, gd_ref, gu_ref, wo_ref, lg_ref, lb_ref,
             h1_ref, xh_ref, rs_ref, on_ref, rr_ref):
        o = jnp.concatenate([osb_ref[...], ofx_ref[...]], axis=1)
        ms = _dot_acc(o * o, gd_ref[...]) * (1.0 / HEAD_DIM)
        r = _dot_acc(lax.rsqrt(ms + RMS_EPS), gu_ref[...])
        onb = (o * r * g_ref[...]).astype(BF16)
        hp = ALPHA * x_ref[...] + _dot(onb, wo_ref[...])
        mu = jnp.mean(hp, axis=-1, keepdims=True)
        d = hp - mu
        rstd = lax.rsqrt(jnp.mean(d * d, axis=-1, keepdims=True) + LN_EPS)
        xh = d * rstd
        h1_ref[...] = xh * lg_ref[...] + lb_ref[...]
        xh_ref[...] = xh
        rs_ref[...] = jnp.broadcast_to(rstd, (TM, LANES))
        on_ref[...] = onb
        rr_ref[...] = r

    row = lambda w: pl.BlockSpec((TM, w), lambda i: (i, 0))
    return pl.pallas_call(
        body, name="post_attn_fwd", grid=(S // TM,),
        in_specs=[row(H), row(H), row(D), _full((1, D)), _full((D, LANES)), _full((LANES, D)), _full((D, D)),
                  _full((1, D)), _full((1, D))],
        out_specs=[row(D), row(D), row(LANES), row(D), row(D)],
        out_shape=[jax.ShapeDtypeStruct((S, D), F32), jax.ShapeDtypeStruct((S, D), F32),
                   jax.ShapeDtypeStruct((S, LANES), F32), jax.ShapeDtypeStruct((S, D), BF16),
                   jax.ShapeDtypeStruct((S, D), F32)],
        compiler_params=_cparams(("arbitrary",)),
    )(o_sb, o_fx, x, g_cat, g_down, g_up, w_out, ln_g, ln_b)


def _ln_bwd(dxh, xh, rstd):
    m1 = jnp.mean(dxh, axis=-1, keepdims=True)
    m2 = jnp.mean(dxh * xh, axis=-1, keepdims=True)
    return rstd * (dxh - m1 - xh * m2)


def _mlp_fwd(h1, target, w_gu, w_dn, ln_g, ln_b):
    S, D = h1.shape
    F = w_dn.shape[0]
    TM = ROW_TILE
    FC = F // 2

    def body(h1_ref, tg_ref, wgu_hbm, wdn_hbm, lg_ref, lb_ref, gu_ref, act_ref, dyp_ref, sm_ref, wgu, wdn):
        @pl.when(pl.program_id(0) == 0)
        def _():
            pltpu.sync_copy(wgu_hbm, wgu)
            pltpu.sync_copy(wdn_hbm, wdn)
            sm_ref[...] = jnp.zeros_like(sm_ref)

        h1v = h1_ref[...]
        hb = h1v.astype(BF16)
        ff = jnp.zeros((TM, D), F32)
        for c0 in range(0, F, FC):
            g = _dot(hb, wgu[:, c0:c0 + FC])
            u = _dot(hb, wgu[:, F + c0:F + c0 + FC])
            gu_ref[:, c0:c0 + FC] = g
            gu_ref[:, F + c0:F + c0 + FC] = u
            ab = ((g * jax.nn.sigmoid(g)) * u).astype(BF16)
            act_ref[:, c0:c0 + FC] = ab
            ff = ff + _dot(ab, wdn[c0:c0 + FC, :])
        yp = ALPHA * h1v + ff
        mu = jnp.mean(yp, axis=-1, keepdims=True)
        d = yp - mu
        rstd = lax.rsqrt(jnp.mean(d * d, axis=-1, keepdims=True) + LN_EPS)
        xh = d * rstd
        err = (xh * lg_ref[...] + lb_ref[...]) - tg_ref[...]
        dy = err * (1.0 / D)
        sm_ref[0:1, :] += jnp.sum(dy * xh, axis=0, keepdims=True)
        sm_ref[1:2, :] += jnp.sum(dy, axis=0, keepdims=True)
        sm_ref[2:3, :] += jnp.sum(err * err, axis=0, keepdims=True)
        dyp_ref[...] = _ln_bwd(dy * lg_ref[...], xh, rstd)

    row = lambda w: pl.BlockSpec((TM, w), lambda i: (i, 0))
    hbm = pl.BlockSpec(memory_space=pl.ANY)
    return pl.pallas_call(
        body, name="mlp_fwd", grid=(S // TM,),
        in_specs=[row(D), row(D), hbm, hbm, _full((1, D)), _full((1, D))],
        out_specs=[row(2 * F), row(F), row(D), _full((8, D))],
        out_shape=[jax.ShapeDtypeStruct((S, 2 * F), F32), jax.ShapeDtypeStruct((S, F), BF16),
                   jax.ShapeDtypeStruct((S, D), F32), jax.ShapeDtypeStruct((8, D), F32)],
        scratch_shapes=[pltpu.VMEM(w_gu.shape, BF16), pltpu.VMEM(w_dn.shape, BF16)],
        compiler_params=_cparams(("arbitrary",)),
    )(h1, target, w_gu, w_dn, ln_g, ln_b)


def _mlp_bwd(gu, dyp, w_guT, w_dnT):
    S, D = dyp.shape
    F = w_dnT.shape[1]
    TM = ROW_TILE
    FC = F // 2

    def body(gu_ref, dyp_ref, wguT_hbm, wdnT_hbm, dgu_ref, dh1_ref, wguT, wdnT):
        @pl.when(pl.program_id(0) == 0)
        def _():
            pltpu.sync_copy(wguT_hbm, wguT)
            pltpu.sync_copy(wdnT_hbm, wdnT)

        dypv = dyp_ref[...]
        db = dypv.astype(BF16)
        dh1 = ALPHA * dypv
        for c0 in range(0, F, FC):
            dact = _dot(db, wdnT[:, c0:c0 + FC])
            g = gu_ref[:, c0:c0 + FC]
            u = gu_ref[:, F + c0:F + c0 + FC]
            sg = jax.nn.sigmoid(g)
            dgb = (dact * u * (sg * (1.0 + g * (1.0 - sg)))).astype(BF16)
            dub = (dact * (g * sg)).astype(BF16)
            dgu_ref[:, c0:c0 + FC] = dgb
            dgu_ref[:, F + c0:F + c0 + FC] = dub
            dh1 = dh1 + _dot(dgb, wguT[c0:c0 + FC, :]) + _dot(dub, wguT[F + c0:F + c0 + FC, :])
        dh1_ref[...] = dh1

    row = lambda w: pl.BlockSpec((TM, w), lambda i: (i, 0))
    hbm = pl.BlockSpec(memory_space=pl.ANY)
    return pl.pallas_call(
        body, name="mlp_bwd", grid=(S // TM,),
        in_specs=[row(2 * F), row(D), hbm, hbm],
        out_specs=[row(2 * F), row(D)],
        out_shape=[jax.ShapeDtypeStruct((S, 2 * F), BF16), jax.ShapeDtypeStruct((S, D), F32)],
        scratch_shapes=[pltpu.VMEM(w_guT.shape, BF16), pltpu.VMEM(w_dnT.shape, BF16)],
        compiler_params=_cparams(("arbitrary",)),
    )(gu, dyp, w_guT, w_dnT)


def _post_attn_bwd(dh1, xh, rs, ln_g, o_sb, o_fx, rr, g_cat, g_down, g_up, w_outT, chunked):
    S, D = dh1.shape
    H = D // 2
    TM = ROW_TILE
    nT = S // TM
    n_ch = len(chunked)
    ex = _Exchange(n_ch, scatter=True)

    def body(*refs):
        dh1_ref, xh_ref, rs_ref, lg_ref, osb_ref, ofx_ref, rr_ref, g_ref, gd_ref, gu_ref, woT_ref = refs[:11]
        ch_in = refs[11:11 + n_ch]
        dxa_ref, dmix_ref, dosb_ref, dofx_ref, sm_ref = refs[11 + n_ch:16 + n_ch]
        ch_out = refs[16 + n_ch:16 + 2 * n_ch]
        sems = refs[16 + 2 * n_ch:]

        @pl.when(pl.program_id(0) == 0)
        def _():
            ex.start(ch_in, ch_out, sems)
            sm_ref[...] = jnp.zeros_like(sm_ref)

        dh = dh1_ref[...]
        xhv = xh_ref[...]
        dhp = _ln_bwd(dh * lg_ref[...], xhv, rs_ref[:, 0:1])
        dxa_ref[...] = ALPHA * dhp
        dmb = dhp.astype(BF16)
        dmix_ref[...] = dmb
        don = _dot(dmb, woT_ref[...])
        o = jnp.concatenate([osb_ref[...], ofx_ref[...]], axis=1)
        r = rr_ref[...]
        u = don * g_ref[...]
        t = _dot_acc(_dot_acc(u * o, gd_ref[...]) * (1.0 / HEAD_DIM), gu_ref[...])
        do = r * u - o * (r * r * r) * t
        dosb_ref[...] = do[:, :H]
        dofx_ref[...] = do[:, H:]
        sm_ref[0:1, :] += jnp.sum(dh * xhv, axis=0, keepdims=True)
        sm_ref[1:2, :] += jnp.sum(dh, axis=0, keepdims=True)
        sm_ref[2:3, :] += jnp.sum(don * o * r, axis=0, keepdims=True)

        @pl.when(pl.program_id(0) == nT - 1)
        def _():
            ex.finish(ch_in, ch_out, sems)

    row = lambda w: pl.BlockSpec((TM, w), lambda i: (i, 0))
    any_spec = pl.BlockSpec(memory_space=pl.ANY)
    return pl.pallas_call(
        body, name="post_attn_bwd", grid=(nT,),
        in_specs=[row(D), row(D), row(LANES), _full((1, D)), row(H), row(H), row(D), _full((1, D)),
                  _full((D, LANES)), _full((LANES, D)), _full((D, D))] + [any_spec] * n_ch,
        out_specs=[row(D), row(D), row(H), row(H), _full((8, D))] + [any_spec] * n_ch,
        out_shape=[jax.ShapeDtypeStruct((S, D), F32), jax.ShapeDtypeStruct((S, D), BF16),
                   jax.ShapeDtypeStruct((S, H), F32), jax.ShapeDtypeStruct((S, H), F32),
                   jax.ShapeDtypeStruct((8, D), F32)] + ex.out_shapes(chunked),
        scratch_shapes=ex.sem_shapes(),
        compiler_params=_cparams(("arbitrary",)),
    )(dh1, xh, rs, ln_g, o_sb, o_fx, rr, g_cat, g_down, g_up, w_outT, *chunked)


def _proj_bwd(dxa, x, pieces, piece_t, dc, u, w_qkvT, w_fT, n_fox, chunked):
    S, D = dxa.shape
    H = D // 2
    TM = ROW_TILE
    nT = S // TM
    tri = _tri(TM, lambda r, c: c >= r)
    n_p = len(pieces)
    n_ch = len(chunked)
    ex = _Exchange(n_ch, scatter=True)

    def body(*refs):
        dxa_ref, x_ref = refs[:2]
        p_refs = refs[2:2 + n_p]
        dc_ref, u_ref, wq_ref, wf_ref, tri_ref = refs[2 + n_p:7 + n_p]
        ch_in = refs[7 + n_p:7 + n_p + n_ch]
        dx_ref, gwf_ref, sm_ref = refs[7 + n_p + n_ch:10 + n_p + n_ch]
        ch_out = refs[10 + n_p + n_ch:10 + n_p + 2 * n_ch]
        run_ref = refs[10 + n_p + 2 * n_ch]
        sems = refs[11 + n_p + 2 * n_ch:]

        @pl.when(pl.program_id(0) == 0)
        def _():
            ex.start(ch_in, ch_out, sems)
            run_ref[...] = jnp.zeros_like(run_ref)
            sm_ref[...] = jnp.zeros_like(sm_ref)
            gwf_ref[...] = jnp.zeros_like(gwf_ref)

        hi, mid, lo = _split3(dc_ref[...])
        t = tri_ref[...]
        dlogf = _dot(t, hi) + _dot(t, mid) + _dot(t, lo) + run_ref[...]
        run_ref[...] = dlogf[0:1, :]
        uv = u_ref[...]
        lane = lax.broadcasted_iota(I32, uv.shape, 1)
        df = jnp.where(lane < n_fox, dlogf * jax.nn.sigmoid(-uv), 0.0)
        sm_ref[0:1, :] += jnp.sum(df, axis=0, keepdims=True)
        dfb = df.astype(BF16)
        gwf_ref[...] += _dot(jnp.transpose(df).astype(BF16), x_ref[...].astype(BF16))
        acc = dxa_ref[...] + _dot(dfb, wf_ref[...])
        for a in range(n_p):
            mm = _dot_tn if piece_t[a] else _dot
            acc = acc + mm(p_refs[a][...].astype(BF16), wq_ref[a * H:(a + 1) * H, :])
        dx_ref[...] = acc

        @pl.when(pl.program_id(0) == nT - 1)
        def _():
            ex.finish(ch_in, ch_out, sems)

    rev = lambda w: pl.BlockSpec((TM, w), lambda i: (nT - 1 - i, 0))
    any_spec = pl.BlockSpec(memory_space=pl.ANY)
    return pl.pallas_call(
        body, name="proj_bwd", grid=(nT,),
        in_specs=[rev(D), rev(D)]
        + [pl.BlockSpec((H, TM), lambda i: (0, nT - 1 - i)) if t else rev(H) for t in piece_t]
        + [rev(LANES), rev(LANES), _full(w_qkvT.shape), _full(w_fT.shape), _full(tri.shape)] + [any_spec] * n_ch,
        out_specs=[rev(D), _full((LANES, D)), _full((8, LANES))] + [any_spec] * n_ch,
        out_shape=[jax.ShapeDtypeStruct((S, D), F32), jax.ShapeDtypeStruct((LANES, D), F32),
                   jax.ShapeDtypeStruct((8, LANES), F32)] + ex.out_shapes(chunked),
        scratch_shapes=[pltpu.VMEM((1, LANES), F32)] + ex.sem_shapes(),
        compiler_params=_cparams(("arbitrary",)),
    )(dxa, x, *pieces, dc, u, w_qkvT, w_fT, tri, *chunked)


def _matmul_tn(a, bs, name, n_split=1, pad_cols=0, b_t=None):
    S, M = a.shape
    b_t = tuple(b_t) if b_t is not None else (False,) * len(bs)
    widths = [b.shape[0] if t else b.shape[1] for b, t in zip(bs, b_t)]
    N = sum(widths) + pad_cols
    assert n_split == 1 or (len(bs) == 1 and pad_cols == 0 and not b_t[0])
    TK = 512 if S % 512 == 0 else ROW_TILE
    MC = 512 if M % 512 == 0 else 256
    nb = len(bs)

    def body(*refs):
        a_ref, b_refs, o_ref, acc_ref = refs[0], refs[1:1 + nb], refs[1 + nb], refs[2 + nb]
        t_refs = list(refs[3 + nb:])

        @pl.when(pl.program_id(1) == 0)
        def _():
            acc_ref[...] = jnp.zeros_like(acc_ref)
            for t_ref in t_refs:
                t_ref[...] = jnp.zeros_like(t_ref)

        n0 = 0
        turned = []
        for b_ref, t, w in zip(b_refs, b_t, widths):
            bv = b_ref[...].astype(BF16)
            if t:
                t_ref = t_refs[len(turned)]
                t_ref[...] += _dot(bv, a_ref[...].astype(BF16))
                turned.append((t_ref, n0, w))
            else:
                for m0 in range(0, M, MC):
                    acc_ref[m0:m0 + MC, n0:n0 + w] += _dot_tn(a_ref[:, m0:m0 + MC].astype(BF16), bv)
            n0 += w

        @pl.when(pl.program_id(1) == S // TK - 1)
        def _():
            for t_ref, c0, w in turned:
                acc_ref[:, c0:c0 + w] = jnp.transpose(t_ref[...])
            o_ref[...] = acc_ref[...].astype(BF16)

    return pl.pallas_call(
        body, name=name, grid=(n_split, S // TK),
        in_specs=[pl.BlockSpec((TK, M), lambda n, k: (k, 0))]
        + [pl.BlockSpec((w, TK), lambda n, k: (0, k)) if t else pl.BlockSpec((TK, w // n_split), lambda n, k: (k, n))
           for w, t in zip(widths, b_t)],
        out_specs=pl.BlockSpec((M, N // n_split), lambda n, k: (0, n)),
        out_shape=jax.ShapeDtypeStruct((M, N), BF16),
        scratch_shapes=[pltpu.VMEM((M, N // n_split), F32)]
        + [pltpu.VMEM((w, M), F32) for w, t in zip(widths, b_t) if t],
        compiler_params=_cparams(("arbitrary", "arbitrary")),
    )(a, *bs)


def _half_mask(half):
    lane = lax.broadcasted_iota(I32, (1, LANES), 1)
    return (lane >= half * HEAD_DIM) & (lane < half * HEAD_DIM + HEAD_DIM)


def _lane_tile(a, width):
    return jnp.concatenate([a] * (width // LANES), axis=1)


def _softplus_neg_abs(z):
    return jnp.log(1.0 + jnp.exp(-jnp.abs(z)))


def _sb_fwd(qkv, n_pair, shards):
    S = qkv.shape[0]
    B = SB_BLOCK
    nq = S // B
    us = _tri(B, lambda r, c: r > c)
    n_sh = len(shards)
    ex = _Exchange(n_sh, scatter=False)

    def body(*refs):
        q_ref, k_ref, v_ref, us_ref = refs[:4]
        sh_in = refs[4:4 + n_sh]
        o_ref, st_ref, js_ref = refs[4 + n_sh:7 + n_sh]
        sh_out = refs[7 + n_sh:7 + 2 * n_sh]
        acc_ref, r_ref = refs[7 + 2 * n_sh:9 + 2 * n_sh]
        sems = refs[9 + 2 * n_sh:]
        p, i = pl.program_id(0), pl.program_id(1)

        @pl.when((p == 0) & (i == 0))
        def _():
            ex.start(sh_in, sh_out, sems)

        hms = [_half_mask(h) for h in range(2)]
        qv = q_ref[...]
        qss = [jnp.where(hm, qv, jnp.zeros_like(qv)) * SCALE for hm in hms]
        row = lax.broadcasted_iota(I32, (B, B), 0)
        col = lax.broadcasted_iota(I32, (B, B), 1)
        tri = col < row
        acc_ref[...] = jnp.zeros_like(acc_ref)
        r_ref[...] = jnp.zeros_like(r_ref)

        def block(j, diag):
            off = pl.multiple_of(j * B, B)
            kj = k_ref[pl.ds(off, B), :]
            vj = v_ref[pl.ds(off, B), :]
            zs = [_dot_nt(qss[h], kj) for h in range(2)]
            sps = [_softplus_neg_abs(z) for z in zs]
            bs = [jnp.minimum(-z, 0.0) - sp for z, sp in zip(zs, sps)]
            if diag:
                bs = [jnp.where(tri, b, 0.0) for b in bs]
            lexcs = [_dot_acc(b, us_ref[...]) for b in bs]
            ws = [jnp.exp(jnp.minimum(z, 0.0) - sp + (_lane_tile(r_ref[h], B) + lexc))
                  for h, (z, sp, lexc) in enumerate(zip(zs, sps, lexcs))]
            if diag:
                ws = [jnp.where(tri, w, 0.0) for w in ws]
            for h in range(2):
                acc_ref[h] += _dot(ws[h].astype(BF16), vj)
                r_ref[h] += jnp.broadcast_to(lexcs[h][:, 0:1] + bs[h][:, 0:1], (B, LANES))

        def live():
            return (jnp.max(r_ref[...]) > -EXP_ZERO).astype(I32)

        def first_two():
            has_prev = (i > 0).astype(F32)
            off_d = pl.multiple_of(i * B, B)
            off_o = pl.multiple_of(jnp.maximum(i - 1, 0) * B, B)
            k_d, v_d = k_ref[pl.ds(off_d, B), :], v_ref[pl.ds(off_d, B), :]
            k_o, v_o = k_ref[pl.ds(off_o, B), :], v_ref[pl.ds(off_o, B), :]
            hs = range(2)
            z_d = [_dot_nt(qss[h], k_d) for h in hs]
            z_o = [_dot_nt(qss[h], k_o) for h in hs]
            sp_d = [_softplus_neg_abs(z) for z in z_d]
            sp_o = [_softplus_neg_abs(z) for z in z_o]
            b_d = [jnp.where(tri, jnp.minimum(-z, 0.0) - sp, 0.0) for z, sp in zip(z_d, sp_d)]
            b_o = [(jnp.minimum(-z, 0.0) - sp) * has_prev for z, sp in zip(z_o, sp_o)]
            l_d = [_dot_acc(b, us_ref[...]) for b in b_d]
            l_o = [_dot_acc(b, us_ref[...]) for b in b_o]
            r_d = [jnp.broadcast_to(l[:, 0:1] + b[:, 0:1], (B, LANES)) for l, b in zip(l_d, b_d)]
            w_d = [jnp.where(tri, jnp.exp(jnp.minimum(z, 0.0) - sp + l), 0.0) for z, sp, l in zip(z_d, sp_d, l_d)]
            w_o = [jnp.exp(jnp.minimum(z, 0.0) - sp + (_lane_tile(r, B) + l)) * has_prev
                   for z, sp, l, r in zip(z_o, sp_o, l_o, r_d)]
            for h in hs:
                acc_ref[h] = _dot(w_d[h].astype(BF16), v_d) + _dot(w_o[h].astype(BF16), v_o)
                r_ref[h] = r_d[h] + jnp.broadcast_to(l_o[h][:, 0:1] + b_o[h][:, 0:1], (B, LANES))

        first_two()

        def step(carry):
            j, _ = carry
            block(j, False)
            return j - 1, live()

        j_end, _ = lax.while_loop(lambda c: (c[0] >= 0) & (c[1] > 0), step, (i - 2, live()))
        js = jnp.maximum(j_end + 1, 0)
        js_ref[2 * p, i] = js
        js_ref[2 * p + 1, i] = js
        o_ref[...] = jnp.where(hms[0], acc_ref[0], acc_ref[1])
        st_ref[...] = r_ref[...]

        @pl.when((p == n_pair - 1) & (i == nq - 1))
        def _():
            ex.finish(sh_in, sh_out, sems)

    any_spec = pl.BlockSpec(memory_space=pl.ANY)
    return pl.pallas_call(
        body, name="sb_attn_fwd", grid=(n_pair, nq),
        in_specs=[pl.BlockSpec((B, LANES), lambda p, i: (i, p)),
                  pl.BlockSpec((S, LANES), lambda p, i: (0, n_pair + p)),
                  pl.BlockSpec((S, LANES), lambda p, i: (0, 2 * n_pair + p)),
                  _full((B, B))] + [any_spec] * n_sh,
        out_specs=[pl.BlockSpec((B, LANES), lambda p, i: (i, p)),
                   pl.BlockSpec((2, B, LANES), lambda p, i: (p, i, 0)),
                   pl.BlockSpec(memory_space=pltpu.SMEM)] + [any_spec] * n_sh,
        out_shape=[jax.ShapeDtypeStruct((S, n_pair * LANES), F32),
                   jax.ShapeDtypeStruct((2 * n_pair, S, LANES), F32),
                   jax.ShapeDtypeStruct((2 * n_pair, nq), I32)] + ex.out_shapes(shards),
        scratch_shapes=[pltpu.VMEM((2, B, LANES), F32), pltpu.VMEM((2, B, LANES), F32)] + ex.sem_shapes(),
        compiler_params=_cparams(("arbitrary", "arbitrary")),
    )(qkv, qkv, qkv, us, *shards)


def _sb_bwd(qkv, do, st, js, n_pair):
    S = qkv.shape[0]
    B = SB_BLOCK
    nq = S // B
    us = _tri(B, lambda r, c: r > c)
    ti = _tri(B, lambda r, c: r <= c)

    def body(js_ref, q_ref, k_ref, v_ref, do_ref, st_ref, us_ref, ti_ref, dq_ref, dk_ref, dv_ref,
             dqa_ref, pr_ref, er_ref):
        p, i = pl.program_id(0), pl.program_id(1)

        @pl.when(i == 0)
        def _():
            dk_ref[...] = jnp.zeros_like(dk_ref)
            dv_ref[...] = jnp.zeros_like(dv_ref)

        hms = [_half_mask(h) for h in range(2)]
        qv = q_ref[...]
        dov = do_ref[...]
        qss = [jnp.where(hm, qv, jnp.zeros_like(qv)) * SCALE for hm in hms]
        dobs = [jnp.where(hm, dov, 0.0).astype(BF16) for hm in hms]
        qs_t = [jnp.transpose(q) for q in qss]
        do_t = [jnp.transpose(d) for d in dobs]
        row = lax.broadcasted_iota(I32, (B, B), 0)
        col = lax.broadcasted_iota(I32, (B, B), 1)
        tri = col < row
        dqa_ref[...] = jnp.zeros_like(dqa_ref)
        pr_ref[...] = jnp.zeros_like(pr_ref)
        er_ref[...] = jnp.zeros_like(er_ref)

        def block(j, diag):
            off = pl.multiple_of(j * B, B)
            kj = k_ref[pl.ds(off, B), :]
            vj = v_ref[pl.ds(off, B), :]
            hs = range(2)
            zs = [_dot_nt(qss[h], kj) for h in hs]
            dws = [_dot_nt(dobs[h], vj) for h in hs]
            sps = [_softplus_neg_abs(z) for z in zs]
            bs = [jnp.minimum(-z, 0.0) - sp for z, sp in zip(zs, sps)]
            if diag:
                bs = [jnp.where(tri, b, 0.0) for b in bs]
            lexcs = [_dot_acc(b, us_ref[...]) for b in bs]
            ws = []
            for h in hs:
                pr_new = pr_ref[h] + jnp.broadcast_to(lexcs[h][:, 0:1] + bs[h][:, 0:1], (B, LANES))
                pr_ref[h] = pr_new
                w = jnp.exp(jnp.minimum(zs[h], 0.0) - sps[h] + (_lane_tile(st_ref[h] - pr_new, B) + lexcs[h]))
                ws.append(jnp.where(tri, w, 0.0) if diag else w)
            es = [dw * w for dw, w in zip(dws, ws)]
            eincs = [_dot_acc(e, ti_ref[...]) for e in es]
            dzbs = []
            for h in hs:
                er = er_ref[h]
                big_e = _lane_tile(er, B) + (eincs[h] - es[h])
                er_ref[h] = er + jnp.broadcast_to(eincs[h][:, B - 1:B], (B, LANES))
                eb = jnp.exp(bs[h])
                dzbs.append((es[h] * eb - big_e * (1.0 - eb)).astype(BF16))
            for h in hs:
                dqa_ref[h] += _dot(dzbs[h], kj)
            dk_ref[:, pl.ds(off, B)] += _dot(qs_t[0], dzbs[0]) + _dot(qs_t[1], dzbs[1])
            dv_ref[:, pl.ds(off, B)] += _dot(do_t[0], ws[0].astype(BF16)) + _dot(do_t[1], ws[1].astype(BF16))

        def step(j, carry):
            block(j, False)
            return carry

        def last_two():
            has_prev = (i > 0).astype(F32)
            off_d = pl.multiple_of(i * B, B)
            off_o = pl.multiple_of(jnp.maximum(i - 1, 0) * B, B)
            k_d, v_d = k_ref[pl.ds(off_d, B), :], v_ref[pl.ds(off_d, B), :]
            k_o, v_o = k_ref[pl.ds(off_o, B), :], v_ref[pl.ds(off_o, B), :]
            hs = range(2)
            z_o = [_dot_nt(qss[h], k_o) for h in hs]
            z_d = [_dot_nt(qss[h], k_d) for h in hs]
            dw_o = [_dot_nt(dobs[h], v_o) for h in hs]
            dw_d = [_dot_nt(dobs[h], v_d) for h in hs]
            sp_o = [_softplus_neg_abs(z) for z in z_o]
            sp_d = [_softplus_neg_abs(z) for z in z_d]
            b_o = [(jnp.minimum(-z, 0.0) - sp) * has_prev for z, sp in zip(z_o, sp_o)]
            b_d = [jnp.where(tri, jnp.minimum(-z, 0.0) - sp, 0.0) for z, sp in zip(z_d, sp_d)]
            l_o = [_dot_acc(b, us_ref[...]) for b in b_o]
            l_d = [_dot_acc(b, us_ref[...]) for b in b_d]
            w_o, w_d = [], []
            for h in hs:
                pr1 = pr_ref[h] + jnp.broadcast_to(l_o[h][:, 0:1] + b_o[h][:, 0:1], (B, LANES))
                pr2 = pr1 + jnp.broadcast_to(l_d[h][:, 0:1] + b_d[h][:, 0:1], (B, LANES))
                st = st_ref[h]
                w_o.append(jnp.exp(jnp.minimum(z_o[h], 0.0) - sp_o[h] + (_lane_tile(st - pr1, B) + l_o[h])) * has_prev)
                w_d.append(jnp.where(tri, jnp.exp(jnp.minimum(z_d[h], 0.0) - sp_d[h]
                                                  + (_lane_tile(st - pr2, B) + l_d[h])), 0.0))
            e_o = [dw * w for dw, w in zip(dw_o, w_o)]
            e_d = [dw * w for dw, w in zip(dw_d, w_d)]
            ei_o = [_dot_acc(e, ti_ref[...]) for e in e_o]
            ei_d = [_dot_acc(e, ti_ref[...]) for e in e_d]
            dz_o, dz_d = [], []
            for h in hs:
                er = er_ref[h]
                big_o = _lane_tile(er, B) + (ei_o[h] - e_o[h])
                er1 = er + jnp.broadcast_to(ei_o[h][:, B - 1:B], (B, LANES))
                big_d = _lane_tile(er1, B) + (ei_d[h] - e_d[h])
                eb_o, eb_d = jnp.exp(b_o[h]), jnp.exp(b_d[h])
                dz_o.append((e_o[h] * eb_o - big_o * (1.0 - eb_o)).astype(BF16))
                dz_d.append((e_d[h] * eb_d - big_d * (1.0 - eb_d)).astype(BF16))
            for h in hs:
                dqa_ref[h] += _dot(dz_o[h], k_o) + _dot(dz_d[h], k_d)
            dk_ref[:, pl.ds(off_o, B)] += _dot(qs_t[0], dz_o[0]) + _dot(qs_t[1], dz_o[1])
            dv_ref[:, pl.ds(off_o, B)] += _dot(do_t[0], w_o[0].astype(BF16)) + _dot(do_t[1], w_o[1].astype(BF16))
            dk_ref[:, pl.ds(off_d, B)] += _dot(qs_t[0], dz_d[0]) + _dot(qs_t[1], dz_d[1])
            dv_ref[:, pl.ds(off_d, B)] += _dot(do_t[0], w_d[0].astype(BF16)) + _dot(do_t[1], w_d[1].astype(BF16))

        lax.fori_loop(js_ref[2 * p, i], i - 1, step, 0)
        last_two()
        dq_ref[...] = jnp.where(hms[0], dqa_ref[0], dqa_ref[1]) * SCALE

    W = n_pair * LANES
    return pl.pallas_call(
        body, name="sb_attn_bwd",
        grid_spec=pltpu.PrefetchScalarGridSpec(
            num_scalar_prefetch=1, grid=(n_pair, nq),
            in_specs=[pl.BlockSpec((B, LANES), lambda p, i, js: (i, p)),
                      pl.BlockSpec((S, LANES), lambda p, i, js: (0, n_pair + p)),
                      pl.BlockSpec((S, LANES), lambda p, i, js: (0, 2 * n_pair + p)),
                      pl.BlockSpec((B, LANES), lambda p, i, js: (i, p)),
                      pl.BlockSpec((2, B, LANES), lambda p, i, js: (p, i, 0)),
                      pl.BlockSpec((B, B), lambda p, i, js: (0, 0)),
                      pl.BlockSpec((B, B), lambda p, i, js: (0, 0))],
            out_specs=[pl.BlockSpec((B, LANES), lambda p, i, js: (i, p)),
                       pl.BlockSpec((LANES, S), lambda p, i, js: (p, 0)),
                       pl.BlockSpec((LANES, S), lambda p, i, js: (p, 0))],
            scratch_shapes=[pltpu.VMEM((2, B, LANES), F32), pltpu.VMEM((2, B, LANES), F32),
                            pltpu.VMEM((2, B, LANES), F32)]),
        out_shape=[jax.ShapeDtypeStruct((S, W), F32)] + [jax.ShapeDtypeStruct((W, S), F32)] * 2,
        compiler_params=_cparams(("arbitrary", "arbitrary")),
    )(js, qkv, qkv, qkv, do, st, us, ti)


def _head_column(blk, head):
    lane = lax.broadcasted_iota(I32, (1, LANES), 1)
    return jnp.sum(jnp.where(lane == head, blk, 0.0), axis=1, keepdims=True)


def _fox_fwd(qkv, c, c_rows, kmax, n_pair):
    S = qkv.shape[0]
    BQ, BK = FOX_BQ, FOX_BK
    R = BQ // BK
    nq = S // BQ

    def head(half, q_ref, k_ref, v_ref, c_ref, cr_ref, km_ref, o_ref, st_ref, js_ref, acc_ref, m_ref, cb_ref,
             qkb_ref):
        p, i = pl.program_id(0), pl.program_id(1)
        hm = _half_mask(half)
        qv = q_ref[...]
        qs = jnp.where(hm, qv, jnp.zeros_like(qv)) * SCALE
        ccol = _head_column(c_ref[...], 2 * p + half)
        cb_ref[...] = jnp.broadcast_to(ccol, (BQ, BK))
        qf = qs.astype(F32)
        qkb_ref[...] = jnp.broadcast_to(
            jnp.sqrt(jnp.sum(qf * qf, axis=1, keepdims=True)) * NORM_SLACK
            * _head_column(km_ref[...], 2 * p + half) + ccol, (BQ, LANES))
        row = lax.broadcasted_iota(I32, (BQ, BK), 0)
        col = lax.broadcasted_iota(I32, (BQ, BK), 1)
        acc_ref[...] = jnp.zeros_like(acc_ref)
        m_ref[...] = jnp.full_like(m_ref, NEG_BIG)

        def blocks(j_top, diag):
            ss, v1s, keeps = [], [], []
            for d in range(R):
                j = j_top - d
                off = pl.multiple_of(j * BK, BK)
                vj = v_ref[pl.ds(off, BK), :]
                v1s.append(jnp.where(hm, vj, jnp.ones_like(vj)))
                s = _dot_nt(qs, k_ref[pl.ds(off, BK), :]) + (cb_ref[...] - cr_ref[half, pl.ds(j, 1), :])
                if diag:
                    keeps.append(col + (R - 1 - d) * BK <= row)
                    s = jnp.where(keeps[-1], s, NEG_BIG)
                ss.append(s)
            m_old = m_ref[...]
            s_max = jnp.max(functools.reduce(jnp.maximum, ss), axis=1, keepdims=True)
            m_new = jnp.maximum(m_old, jnp.broadcast_to(s_max, (BQ, LANES)))
            m_wide = _lane_tile(m_new, BK)
            pvs = [jnp.exp(s - m_wide) for s in ss]
            if diag:
                pvs = [jnp.where(keep, pv, 0.0) for keep, pv in zip(keeps, pvs)]
            new = _dot(pvs[0].astype(BF16), v1s[0])
            for pv, v1 in zip(pvs[1:], v1s[1:]):
                new = new + _dot(pv.astype(BF16), v1)
            acc_ref[...] = jnp.exp(m_old - m_new) * acc_ref[...] + new
            m_ref[...] = m_new

        def live(j):
            c_end = cr_ref[half, pl.ds(jnp.maximum(j, 0), 1), :][:, BK - 1:BK]
            return (jnp.max(qkb_ref[...] - c_end - m_ref[...]) > -EXP_ZERO).astype(I32)

        blocks(R * i + R - 1, True)

        def step(carry):
            j, _ = carry
            go_on = live(j - R)
            blocks(j, False)
            return j - R, go_on

        j_end, _ = lax.while_loop(lambda cr: (cr[0] >= 0) & (cr[1] > 0), step, (R * i - 1, live(R * i - 1)))
        js_ref[2 * p + half, i] = j_end + 1
        acc = acc_ref[...]
        denom = jnp.where(hm, pltpu.roll(acc, HEAD_DIM, 1), acc)
        res = jnp.where(hm, acc / denom, 0.0)

        if half == 0:
            o_ref[...] = res
        else:
            o_ref[...] += res
        st_ref[half] = m_ref[...] + jnp.log(denom)

    def body(*refs):
        head(0, *refs)
        head(1, *refs)

    return pl.pallas_call(
        body, name="fox_attn_fwd", grid=(n_pair, nq),
        in_specs=[pl.BlockSpec((BQ, LANES), lambda p, i: (i, 3 * n_pair + p)),
                  pl.BlockSpec((S, LANES), lambda p, i: (0, 4 * n_pair + p)),
                  pl.BlockSpec((S, LANES), lambda p, i: (0, 5 * n_pair + p)),
                  pl.BlockSpec((BQ, LANES), lambda p, i: (i, 0)),
                  pl.BlockSpec((2, S // BK, BK), lambda p, i: (p, 0, 0)),
                  _full((1, LANES))],
        out_specs=[pl.BlockSpec((BQ, LANES), lambda p, i: (i, p)),
                   pl.BlockSpec((2, BQ, LANES), lambda p, i: (p, i, 0)),
                   pl.BlockSpec(memory_space=pltpu.SMEM)],
        out_shape=[jax.ShapeDtypeStruct((S, n_pair * LANES), F32),
                   jax.ShapeDtypeStruct((2 * n_pair, S, LANES), F32),
                   jax.ShapeDtypeStruct((2 * n_pair, nq), I32)],
        scratch_shapes=[pltpu.VMEM((BQ, LANES), F32), pltpu.VMEM((BQ, LANES), F32), pltpu.VMEM((BQ, BK), F32),
                        pltpu.VMEM((BQ, LANES), F32)],
        compiler_params=_cparams(("arbitrary", "arbitrary")),
    )(qkv, qkv, qkv, c, c_rows, kmax)


def _fox_bwd(qkv, do, o, st, c, c_rows, js, n_pair):
    S = qkv.shape[0]
    B = FOX_BLOCK
    nq = S // B

    def body(js_ref, q_ref, k_ref, v_ref, do_ref, o_ref, st_ref, c_ref, cr_ref, dq_ref, dk_ref, dv_ref,
             dc_ref, dqa_ref, rs_ref, cb_ref, db_ref):
        p, i, half = pl.program_id(0), pl.program_id(1), pl.program_id(2)

        @pl.when((i == 0) & (half == 0))
        def _():
            dk_ref[...] = jnp.zeros_like(dk_ref)
            dv_ref[...] = jnp.zeros_like(dv_ref)
            dc_ref[...] = jnp.zeros_like(dc_ref)

        hm = _half_mask(half)
        qv = q_ref[...]
        qs = jnp.where(hm, qv, jnp.zeros_like(qv)) * SCALE
        dov = jnp.where(hm, do_ref[...], 0.0)
        dob = dov.astype(BF16)
        qs_t = jnp.transpose(qs)
        do_t = jnp.transpose(dob)
        cb_ref[...] = jnp.broadcast_to(_head_column(c_ref[...], 2 * p + half), (B, LANES)) - st_ref[0]
        db_ref[...] = jnp.broadcast_to(jnp.sum(dov * o_ref[...], axis=1, keepdims=True), (B, LANES))
        row = lax.broadcasted_iota(I32, (B, B), 0)
        col = lax.broadcasted_iota(I32, (B, B), 1)
        dqa_ref[...] = jnp.zeros_like(dqa_ref)
        rs_ref[...] = jnp.zeros_like(rs_ref)

        def block(j, diag):
            off = pl.multiple_of(j * B, B)
            kj = k_ref[pl.ds(off, B), :]
            vj = v_ref[pl.ds(off, B), :]
            pv = jnp.exp(_dot_nt(qs, kj) + (_lane_tile(cb_ref[...], B) - cr_ref[0, pl.ds(j, 1), :]))
            if diag:
                pv = jnp.where(col <= row, pv, 0.0)
            ds = pv * (_dot_nt(dob, vj) - _lane_tile(db_ref[...], B))
            dsb = ds.astype(BF16)
            dqa_ref[...] += _dot(dsb, kj)
            dk_ref[:, pl.ds(off, B)] += _dot(qs_t, dsb)
            dv_ref[:, pl.ds(off, B)] += _dot(do_t, pv.astype(BF16))
            dc_ref[0, half, pl.ds(j, 1), :] -= jnp.sum(ds, axis=0, keepdims=True)
            rs_ref[...] += jnp.sum(ds, axis=1, keepdims=True)

        def step(j, carry):
            block(j, False)
            return carry

        lax.fori_loop(js_ref[2 * p + half, i], i, step, 0)
        block(i, True)
        res = jnp.where(hm, dqa_ref[...] * SCALE, 0.0)

        @pl.when(half == 0)
        def _():
            dq_ref[...] = res

        @pl.when(half == 1)
        def _():
            dq_ref[...] += res

        dc_ref[0, half, pl.ds(i, 1), :] += jnp.transpose(jnp.broadcast_to(rs_ref[...], (B, LANES)))[0:1, :]

    W = n_pair * LANES
    return pl.pallas_call(
        body, name="fox_attn_bwd",
        grid_spec=pltpu.PrefetchScalarGridSpec(
            num_scalar_prefetch=1, grid=(n_pair, nq, 2),
            in_specs=[pl.BlockSpec((B, LANES), lambda p, i, h, js: (i, 3 * n_pair + p)),
                      pl.BlockSpec((S, LANES), lambda p, i, h, js: (0, 4 * n_pair + p)),
                      pl.BlockSpec((S, LANES), lambda p, i, h, js: (0, 5 * n_pair + p)),
                      pl.BlockSpec((B, LANES), lambda p, i, h, js: (i, p)),
                      pl.BlockSpec((B, LANES), lambda p, i, h, js: (i, p)),
                      pl.BlockSpec((1, B, LANES), lambda p, i, h, js: (2 * p + h, i, 0)),
                      pl.BlockSpec((B, LANES), lambda p, i, h, js: (i, 0)),
                      pl.BlockSpec((1, nq, B), lambda p, i, h, js: (2 * p + h, 0, 0))],
            out_specs=[pl.BlockSpec((B, LANES), lambda p, i, h, js: (i, p)),
                       pl.BlockSpec((LANES, S), lambda p, i, h, js: (p, 0)),
                       pl.BlockSpec((LANES, S), lambda p, i, h, js: (p, 0)),
                       pl.BlockSpec((1, 2, nq, B), lambda p, i, h, js: (p, 0, 0, 0))],
            scratch_shapes=[pltpu.VMEM((B, LANES), F32), pltpu.VMEM((B, 1), F32), pltpu.VMEM((B, LANES), F32),
                            pltpu.VMEM((B, LANES), F32)]),
        out_shape=[jax.ShapeDtypeStruct((S, W), F32)] + [jax.ShapeDtypeStruct((W, S), F32)] * 2
        + [jax.ShapeDtypeStruct((n_pair, 2, nq, B), F32)],
        compiler_params=_cparams(("arbitrary", "arbitrary", "arbitrary")),
    )(js, qkv, qkv, qkv, do, o, st, c, c_rows)


def _adam(w, g, m, v):
    m = ADAM_B1 * m + (1.0 - ADAM_B1) * g
    v = ADAM_B2 * v + (1.0 - ADAM_B2) * (g * g)
    m_hat = m / (1.0 - ADAM_B1 ** ADAM_STEP)
    v_hat = v / (1.0 - ADAM_B2 ** ADAM_STEP)
    delta = -ADAM_LR * (m_hat / (jnp.sqrt(v_hat) + ADAM_EPS) + ADAM_WD * w)
    return delta, m, v


def _reduce_adam(landing, w, m, v, name, extra=None):
    _, R, C = w.shape
    TR = next(t for t in (256, 128, R) if R % t == 0)
    more = [] if extra is None else [extra]

    def body(*refs):
        l_ref, w_ref, m_ref, v_ref = refs[:4]
        g_ref, d_ref, nm_ref, nv_ref = refs[4 + len(more):]
        g = l_ref[0].astype(F32)
        for s in range(1, N_DEV):
            g = g + l_ref[s].astype(F32)
        if more:
            g = g + refs[4][...]
        d, nm, nv = _adam(w_ref[0], g, m_ref[0], v_ref[0])
        g_ref[0] = g
        d_ref[0] = d
        nm_ref[0] = nm
        nv_ref[0] = nv

    blk = pl.BlockSpec((1, TR, C), lambda i: (0, i, 0))
    return pl.pallas_call(
        body, name=name, grid=(R // TR,),
        in_specs=[pl.BlockSpec((N_DEV, TR, C), lambda i: (0, i, 0)), blk, blk, blk]
        + [pl.BlockSpec((TR, C), lambda i: (i, 0))] * len(more),
        out_specs=[blk] * 4,
        out_shape=[jax.ShapeDtypeStruct((1, R, C), F32)] * 4,
        compiler_params=_cparams(("arbitrary",)),
    )(landing, w, m, v, *more)


def _reduce_adam_small(landing, w, m, v):
    R, C = w.shape

    def body(l_ref, w_ref, m_ref, v_ref, g_ref, d_ref, nm_ref, nv_ref, loss_ref):
        g = l_ref[0]
        for s in range(1, N_DEV):
            g = g + l_ref[s]
        d, nm, nv = _adam(w_ref[...], g, m_ref[...], v_ref[...])
        g_ref[...] = g
        d_ref[...] = d
        nm_ref[...] = nm
        nv_ref[...] = nv
        loss_ref[...] = jnp.broadcast_to(0.5 * jnp.sum(g[7:8, :], axis=1, keepdims=True), (1, LANES))

    return pl.pallas_call(
        body, name="reduce_adam_small",
        out_shape=[jax.ShapeDtypeStruct((R, C), F32)] * 4 + [jax.ShapeDtypeStruct((1, LANES), F32)],
    )(landing, w, m, v)


def _pad_lanes(a, width):
    return jnp.pad(a, ((0, 0), (0, width - a.shape[1])))


def _pack_small(D, n_fox, g_cat, l1g, l1b, l2g, l2b, bf, last, tail):
    return jnp.concatenate([g_cat, l1g, l1b, l2g, l2b, _pad_lanes(bf, D), jnp.zeros((1, D), F32), last, tail],
                           axis=0)


def kernel(x, w_in, b_f, g_sb, g_fox, w_out, ln1_g, ln1_b, ln2_g, ln2_b, w_gate_up, w_down, loss_target, m_w_in, m_b_f, m_g_sb, m_g_fox, m_w_out, m_ln1_g, m_ln1_b, m_ln2_g, m_ln2_b, m_w_gate_up, m_w_down, v_w_in, v_b_f, v_g_sb, v_g_fox, v_w_out, v_ln1_g, v_ln1_b, v_ln2_g, v_ln2_b, v_w_gate_up, v_w_down):
    x2, tgt = x[0], loss_target[0]
    S, D = x2.shape
    W = D // 2
    n_pair = W // LANES
    n_fox = W // HEAD_DIM
    F = w_down.shape[1] * N_DEV

    g_in = _all_gather_two_level(w_in[0].astype(BF16), "w_in_all_gather")
    w_in_full = g_in.transpose(1, 0, 2).reshape(D, -1)
    w_qkv = w_in_full[:, :6 * W]
    w_f = _pad_lanes(w_in_full[:, 6 * W:], LANES)
    g_down = (lax.broadcasted_iota(I32, (D, LANES), 0) // HEAD_DIM
              == lax.broadcasted_iota(I32, (D, LANES), 1)).astype(BF16)
    g_up = g_down.T
    g_cat = jnp.concatenate([g_sb, g_fox], axis=1)

    qkv, u, c, ksq = _proj_fwd(x2, w_qkv, w_f, _pad_lanes(b_f, LANES), n_fox)
    c_t = c[:, :n_fox].T
    c_rows = c_t.reshape(n_fox, S // FOX_BLOCK, FOX_BLOCK)
    kmax = jnp.sqrt(ksq[0:1]) * NORM_SLACK
    o_sb, st_sb, js_sb, g_out, g_gu, g_dn = _sb_fwd(
        qkv, n_pair, [w_out[0].astype(BF16), w_gate_up[0].astype(BF16), w_down[0].astype(BF16)])
    w_o = g_out.reshape(D, D)
    w_gu = g_gu.transpose(1, 0, 2).reshape(D, 2 * F)
    w_dn = g_dn.reshape(F, D)
    o_fx, st_fx, js_fx = _fox_fwd(qkv, c, c_t.reshape(n_fox, S // FOX_BK, FOX_BK), kmax, n_pair)
    js_fx = js_fx // (FOX_BLOCK // FOX_BK)
    h1, xh1, rs1, on_b, rr = _post_attn_fwd(o_sb, o_fx, x2, g_cat, g_down, g_up, w_o, ln1_g, ln1_b)
    gu, act_b, dyp, sm2 = _mlp_fwd(h1, tgt, w_gu, w_dn, ln2_g, ln2_b)

    dgu_b, dh1 = _mlp_bwd(gu, dyp, w_gu.T, w_dn.T)
    gw_gu = _matmul_tn(h1, [dgu_b], "grad_w_gate_up", n_split=2)
    gw_dn = _matmul_tn(act_b, [dyp], "grad_w_down")
    dxa, dmix_b, do_sb, do_fx, sm1, l_gu, l_dn = _post_attn_bwd(
        dh1, xh1, rs1, ln1_g, o_sb, o_fx, rr, g_cat, g_down, g_up, w_o.T,
        [gw_gu.reshape(D, N_DEV, -1).transpose(1, 0, 2), gw_dn.reshape(N_DEV, F // N_DEV, D)])
    dq_sb, dk_sb, dv_sb = _sb_bwd(qkv, do_sb, st_sb, js_sb, n_pair)
    dq_fx, dk_fx, dv_fx, dcr = _fox_bwd(qkv, do_fx, o_fx, st_fx, c, c_rows, js_fx, n_pair)
    dc = _pad_lanes(dcr.reshape(n_fox, S).T, LANES)
    pieces = [dq_sb, dk_sb, dv_sb, dq_fx, dk_fx, dv_fx]
    piece_t = (False, True, True, False, True, True)
    gw_qkv = _matmul_tn(x2, pieces, "grad_w_qkv", pad_cols=LANES, b_t=piece_t)[:, :6 * W + n_fox]
    gw_out = _matmul_tn(on_b, [dmix_b], "grad_w_out")
    dx, gw_f_t, sm0, l_in, l_out = _proj_bwd(
        dxa, x2, pieces, piece_t, dc, u, w_qkv.T, w_f.T, n_fox,
        [gw_qkv.reshape(D, N_DEV, -1).transpose(1, 0, 2), gw_out.reshape(N_DEV, D // N_DEV, D)])

    small = _pack_small(D, n_fox, sm1[2:3], sm1[0:1], sm1[1:2], sm2[0:1], sm2[1:2], sm0[0:1, :n_fox],
                        sm2[2:3] * (1.0 / D), gw_f_t[:n_fox])
    (l_small,) = _exchange_call([jnp.broadcast_to(small[None], (N_DEV,) + small.shape)], True, "small_exchange")

    zero = jnp.zeros((1, D), F32)
    pack = lambda gc, a, b_, c_, d_, bf: _pack_small(D, n_fox, gc, a, b_, c_, d_, bf, zero,
                                                     jnp.zeros((n_fox, D), F32))
    r_small = _reduce_adam_small(
        l_small,
        pack(g_cat, ln1_g, ln1_b, ln2_g, ln2_b, b_f),
        pack(jnp.concatenate([m_g_sb, m_g_fox], axis=1), m_ln1_g, m_ln1_b, m_ln2_g, m_ln2_b, m_b_f),
        pack(jnp.concatenate([v_g_sb, v_g_fox], axis=1), v_ln1_g, v_ln1_b, v_ln2_g, v_ln2_b, v_b_f))
    loss = r_small[4][0, 0]
    cols = w_in.shape[2]
    gf_cols = jnp.pad(r_small[0][8:8 + n_fox].T, ((0, 0), (cols - n_fox, 0)))
    extra = jnp.where(_my_index() == N_DEV - 1, gf_cols, 0.0)
    r_in = _reduce_adam(l_in, w_in, m_w_in, v_w_in, "reduce_adam_w_in", extra)
    r_out = _reduce_adam(l_out, w_out, m_w_out, v_w_out, "reduce_adam_w_out")
    r_gu = _reduce_adam(l_gu, w_gate_up, m_w_gate_up, v_w_gate_up, "reduce_adam_w_gate_up")
    r_dn = _reduce_adam(l_dn, w_down, m_w_down, v_w_down, "reduce_adam_w_down")

    def unpack(kind):
        big = [r_in[kind], None, None, None, r_out[kind], None, None, None, None, r_gu[kind], r_dn[kind]]
        s = r_small[kind]
        big[1] = s[5:6, :n_fox]
        big[2] = s[0:1, :W]
        big[3] = s[0:1, W:]
        big[5], big[6], big[7], big[8] = s[1:2], s[2:3], s[3:4], s[4:5]
        return big

    return (loss, dx[None], *unpack(0), *unpack(1), *unpack(2), *unpack(3))
```
